```python
import math
import jax, jax.numpy as jnp
from jax import lax
import numpy as np

D_MODEL = 2048
BATCH = 8
SEQ = 4096
DEPTH = 2

D_MIX = D_MODEL
N_MIXERS = 4
MIX_W = D_MIX // N_MIXERS
IN_COLS = 6 * MIX_W
S5_GROUP_CH = 16
S5_GROUPS = MIX_W // S5_GROUP_CH
S5_STATE = 64
CONV_WIDTH = 31
LRU_HEADS = 8
LRU_HEAD_DIM = MIX_W // LRU_HEADS
LRU_CONV_WIDTH = 4
LRU_C = 8.0
POOL_WINDOWS = (2, 4, 8, 16)
POOL_GROUP_W = MIX_W // len(POOL_WINDOWS)
FFN_DIM = 5504
FFN_CONV_WIDTH = 3
EPS = 1e-6

kernel_name = "hybrid_parallel_s5_conformer_rglru_pool"


def rmsnorm(x, g):
    xf = x.astype(jnp.float32)
    y = xf * lax.rsqrt(jnp.mean(xf * xf, axis=-1, keepdims=True) + EPS)
    return (y * g.astype(jnp.float32)).astype(x.dtype)


def layernorm(x, g, b):
    xf = x.astype(jnp.float32)
    mu = jnp.mean(xf, axis=-1, keepdims=True)
    var = jnp.mean(jnp.square(xf - mu), axis=-1, keepdims=True)
    y = (xf - mu) * lax.rsqrt(var + EPS)
    return (y * g.astype(jnp.float32) + b.astype(jnp.float32)).astype(x.dtype)


def causal_dwconv(x, w, b):
    k, c = w.shape
    y = lax.conv_general_dilated(
        x, w[:, None, :].astype(x.dtype), window_strides=(1,), padding=((k - 1, 0),),
        dimension_numbers=('NWC', 'WIO', 'NWC'), feature_group_count=c)
    return y + b


def _linear_combine(left, right):
    a_l, b_l = left
    a_r, b_r = right
    return (a_l * a_r, a_r * b_l + b_r)


def s5_mixer(u, lam_re, lam_im, log_step, b_re, b_im, c_re, c_im, d, w_glu, b_glu):
    dt = u.dtype
    bsz, seqlen, _ = u.shape
    uf = u.astype(jnp.float32).reshape(bsz, seqlen, S5_GROUPS, S5_GROUP_CH)
    lam = lax.complex(lam_re.astype(jnp.float32), lam_im.astype(jnp.float32))
    step = jnp.exp(log_step.astype(jnp.float32))[:, None]
    lam_bar = jnp.exp(lam * step)
    bmat = lax.complex(b_re.astype(jnp.float32), b_im.astype(jnp.float32))
    b_bar = ((lam_bar - 1.0) / lam)[..., None] * bmat
    bu = jnp.einsum('blgh,gph->blgp', uf.astype(jnp.complex64), b_bar)
    a = jnp.broadcast_to(lam_bar, bu.shape)
    _, states = lax.associative_scan(_linear_combine, (a, bu), axis=1)
    cmat = lax.complex(c_re.astype(jnp.float32), c_im.astype(jnp.float32))
    y = jnp.einsum('blgp,ghp->blgh', states, cmat).real
    y = y + d.astype(jnp.float32).reshape(S5_GROUPS, S5_GROUP_CH) * uf
    y = y.reshape(bsz, seqlen, MIX_W)
    g = jax.nn.gelu(y, approximate=True)
    out = g * jax.nn.sigmoid(g @ w_glu.astype(jnp.float32) + b_glu.astype(jnp.float32))
    return out.astype(dt)


def conformer_conv_mixer(v, g, w_dw, b_dw, ln_g, ln_b, w_pw, b_pw):
    h = v * jax.nn.sigmoid(g)
    h = causal_dwconv(h, w_dw, b_dw)
    h = layernorm(h, ln_g, ln_b)
    h = jax.nn.silu(h)
    return h @ w_pw + b_pw


def rglru_mixer(xb, gb, w_conv, b_conv, w_r, b_r, w_i, b_i, lam):
    dt = xb.dtype
    bsz, seqlen, _ = xb.shape
    xc = causal_dwconv(xb, w_conv, b_conv)
    xh = xc.reshape(bsz, seqlen, LRU_HEADS, LRU_HEAD_DIM)
    r = jax.nn.sigmoid(jnp.einsum('blhd,hde->blhe', xh, w_r).reshape(bsz, seqlen, MIX_W) + b_r)
    i = jax.nn.sigmoid(jnp.einsum('blhd,hde->blhe', xh, w_i).reshape(bsz, seqlen, MIX_W) + b_i)
    log_a = -LRU_C * r.astype(jnp.float32) * jax.nn.softplus(-lam.astype(jnp.float32))
    a = jnp.exp(log_a)
    mult = jnp.sqrt(-jnp.expm1(2.0 * log_a))
    bt = mult * (i * xc).astype(jnp.float32)
    _, h = lax.associative_scan(_linear_combine, (a, bt), axis=1)
    return h.astype(dt) * jax.nn.gelu(gb, approximate=True)


def pool_mixer(xp, w, scale):
    dt = xp.dtype
    bsz, seqlen, _ = xp.shape
    xf = xp.astype(jnp.float32)
    cs = jnp.cumsum(xf, axis=1)
    cs_pad = jnp.concatenate([jnp.zeros((bsz, 1, MIX_W), jnp.float32), cs], axis=1)
    pos = jnp.arange(seqlen, dtype=jnp.float32) + 1.0
    diffs = []
    for gi, win in enumerate(POOL_WINDOWS):
        sl = slice(gi * POOL_GROUP_W, (gi + 1) * POOL_GROUP_W)
        upper = cs_pad[:, 1:, sl]
        lower = jnp.concatenate(
            [jnp.zeros((bsz, win - 1, POOL_GROUP_W), jnp.float32), cs_pad[:, :seqlen - win + 1, sl]], axis=1)
        count = jnp.minimum(pos, float(win))[None, :, None]
        diffs.append((upper - lower) / count - xf[:, :, sl])
    dg = jnp.stack(diffs, axis=2)
    y = jnp.einsum('blgc,gce->blge', dg, w.astype(jnp.float32)).reshape(bsz, seqlen, MIX_W)
    return (y * scale.astype(jnp.float32)).astype(dt)


def conv_gated_mlp(h, w_up, w_dw, b_dw, w_down):
    up = h @ w_up
    gate, val = jnp.split(up, 2, axis=-1)
    gate = causal_dwconv(gate, w_dw, b_dw)
    return (jax.nn.gelu(gate, approximate=True) * val) @ w_down


def _fwd_setup_inputs(seed: int = 0) -> dict:
    key = jax.random.key(seed)
    ks = iter(jax.random.split(key, 40))
    nrm = lambda shape, std: std * jax.random.normal(next(ks), shape, jnp.float32)
    L_, G, P, H = DEPTH, S5_GROUPS, S5_STATE, S5_GROUP_CH
    x = jax.random.normal(next(ks), (BATCH, SEQ, D_MODEL), jnp.float32)
    lam_im_base = jnp.pi * jnp.arange(P, dtype=jnp.float32)
    a0 = jax.random.uniform(next(ks), (L_, MIX_W), jnp.float32, 0.9, 0.999)
    a_base = a0 ** (1.0 / LRU_C)
    return {
        "x": x,
        "norm_mix_g": 1.0 + nrm((L_, D_MODEL), 0.02),
        "w_in": nrm((L_, D_MODEL, IN_COLS), D_MODEL ** -0.5),
        "s5_lam_re": -0.5 + nrm((L_, G, P), 0.01),
        "s5_lam_im": lam_im_base + nrm((L_, G, P), 0.01),
        "s5_log_step": jax.random.uniform(next(ks), (L_, G), jnp.float32, math.log(0.001), math.log(0.1)),
        "s5_b_re": nrm((L_, G, P, H), (2.0 * H) ** -0.5),
        "s5_b_im": nrm((L_, G, P, H), (2.0 * H) ** -0.5),
        "s5_c_re": nrm((L_, G, H, P), (2.0 * P) ** -0.5),
        "s5_c_im": nrm((L_, G, H, P), (2.0 * P) ** -0.5),
        "s5_d": nrm((L_, MIX_W), 1.0),
        "s5_w_glu": nrm((L_, MIX_W, MIX_W), MIX_W ** -0.5),
        "s5_b_glu": nrm((L_, MIX_W), 0.01),
        "cv_w_dw": nrm((L_, CONV_WIDTH, MIX_W), CONV_WIDTH ** -0.5),
        "cv_b_dw": nrm((L_, MIX_W), 0.01),
        "cv_ln_g": 1.0 + nrm((L_, MIX_W), 0.02),
        "cv_ln_b": nrm((L_, MIX_W), 0.01),
        "cv_w_pw": nrm((L_, MIX_W, MIX_W), MIX_W ** -0.5),
        "cv_b_pw": nrm((L_, MIX_W), 0.01),
        "lru_w_conv": nrm((L_, LRU_CONV_WIDTH, MIX_W), LRU_CONV_WIDTH ** -0.5),
        "lru_b_conv": nrm((L_, MIX_W), 0.01),
        "lru_w_r": nrm((L_, LRU_HEADS, LRU_HEAD_DIM, LRU_HEAD_DIM), LRU_HEAD_DIM ** -0.5),
        "lru_b_r": nrm((L_, MIX_W), 0.01),
        "lru_w_i": nrm((L_, LRU_HEADS, LRU_HEAD_DIM, LRU_HEAD_DIM), LRU_HEAD_DIM ** -0.5),
        "lru_b_i": nrm((L_, MIX_W), 0.01),
        "lru_lam": jnp.log(a_base) - jnp.log1p(-a_base),
        "pool_w": nrm((L_, len(POOL_WINDOWS), POOL_GROUP_W, POOL_GROUP_W), POOL_GROUP_W ** -0.5),
        "pool_scale": 1.0 + nrm((L_, MIX_W), 0.02),
        "w_out": nrm((L_, D_MIX, D_MODEL), D_MIX ** -0.5),
        "norm_ffn_g": 1.0 + nrm((L_, D_MODEL), 0.02),
        "ffn_w_up": nrm((L_, D_MODEL, 2 * FFN_DIM), D_MODEL ** -0.5),
        "ffn_w_dw": nrm((L_, FFN_CONV_WIDTH, FFN_DIM), FFN_CONV_WIDTH ** -0.5),
        "ffn_b_dw": nrm((L_, FFN_DIM), 0.01),
        "ffn_w_down": nrm((L_, FFN_DIM, D_MODEL), FFN_DIM ** -0.5),
        "norm_final_g": 1.0 + nrm((D_MODEL,), 0.02),
    }


def _fwd_reference(x, norm_mix_g, w_in, s5_lam_re, s5_lam_im, s5_log_step, s5_b_re, s5_b_im,
              s5_c_re, s5_c_im, s5_d, s5_w_glu, s5_b_glu, cv_w_dw, cv_b_dw, cv_ln_g, cv_ln_b,
              cv_w_pw, cv_b_pw, lru_w_conv, lru_b_conv, lru_w_r, lru_b_r, lru_w_i, lru_b_i,
              lru_lam, pool_w, pool_scale, w_out, norm_ffn_g, ffn_w_up, ffn_w_dw, ffn_b_dw,
              ffn_w_down, norm_final_g):
    split_idx = [MIX_W * k for k in range(1, 6)]
    for l in range(DEPTH):
        h = rmsnorm(x, norm_mix_g[l])
        proj = h @ w_in[l]
        s5_u, cv_v, cv_g, lru_x, lru_g, pool_x = jnp.split(proj, split_idx, axis=-1)
        y_s5 = s5_mixer(s5_u, s5_lam_re[l], s5_lam_im[l], s5_log_step[l], s5_b_re[l], s5_b_im[l],
                        s5_c_re[l], s5_c_im[l], s5_d[l], s5_w_glu[l], s5_b_glu[l])
        y_cv = conformer_conv_mixer(cv_v, cv_g, cv_w_dw[l], cv_b_dw[l], cv_ln_g[l], cv_ln_b[l],
                                    cv_w_pw[l], cv_b_pw[l])
        y_lru = rglru_mixer(lru_x, lru_g, lru_w_conv[l], lru_b_conv[l], lru_w_r[l], lru_b_r[l],
                            lru_w_i[l], lru_b_i[l], lru_lam[l])
        y_pool = pool_mixer(pool_x, pool_w[l], pool_scale[l])
        mixed = jnp.concatenate([y_s5, y_cv, y_lru, y_pool], axis=-1)
        x = x + mixed @ w_out[l]
        h = rmsnorm(x, norm_ffn_g[l])
        x = x + conv_gated_mlp(h, ffn_w_up[l], ffn_w_dw[l], ffn_b_dw[l], ffn_w_down[l])
    return rmsnorm(x, norm_final_g)


import jax as _jax
import jax.numpy as _jnp

TWIN_FORMAT = 'train_step'
FWD_PARAMS = ['x', 'norm_mix_g', 'w_in', 's5_lam_re', 's5_lam_im', 's5_log_step', 's5_b_re', 's5_b_im', 's5_c_re', 's5_c_im', 's5_d', 's5_w_glu', 's5_b_glu', 'cv_w_dw', 'cv_b_dw', 'cv_ln_g', 'cv_ln_b', 'cv_w_pw', 'cv_b_pw', 'lru_w_conv', 'lru_b_conv', 'lru_w_r', 'lru_b_r', 'lru_w_i', 'lru_b_i', 'lru_lam', 'pool_w', 'pool_scale', 'w_out', 'norm_ffn_g', 'ffn_w_up', 'ffn_w_dw', 'ffn_b_dw', 'ffn_w_down', 'norm_final_g']
TWIN_WEIGHTS = ['norm_mix_g', 'w_in', 's5_lam_re', 's5_lam_im', 's5_log_step', 's5_b_re', 's5_b_im', 's5_c_re', 's5_c_im', 's5_d', 's5_w_glu', 's5_b_glu', 'cv_w_dw', 'cv_b_dw', 'cv_ln_g', 'cv_ln_b', 'cv_w_pw', 'cv_b_pw', 'lru_w_conv', 'lru_b_conv', 'lru_w_r', 'lru_b_r', 'lru_w_i', 'lru_b_i', 'lru_lam', 'pool_w', 'pool_scale', 'w_out', 'norm_ffn_g', 'ffn_w_up', 'ffn_w_dw', 'ffn_b_dw', 'ffn_w_down', 'norm_final_g']
TWIN_DIFF_INPUT = 'x'
TWIN_INPUTS = ['x', 'norm_mix_g', 'w_in', 's5_lam_re', 's5_lam_im', 's5_log_step', 's5_b_re', 's5_b_im', 's5_c_re', 's5_c_im', 's5_d', 's5_w_glu', 's5_b_glu', 'cv_w_dw', 'cv_b_dw', 'cv_ln_g', 'cv_ln_b', 'cv_w_pw', 'cv_b_pw', 'lru_w_conv', 'lru_b_conv', 'lru_w_r', 'lru_b_r', 'lru_w_i', 'lru_b_i', 'lru_lam', 'pool_w', 'pool_scale', 'w_out', 'norm_ffn_g', 'ffn_w_up', 'ffn_w_dw', 'ffn_b_dw', 'ffn_w_down', 'norm_final_g', 'loss_target', 'm_norm_mix_g', 'm_w_in', 'm_s5_lam_re', 'm_s5_lam_im', 'm_s5_log_step', 'm_s5_b_re', 'm_s5_b_im', 'm_s5_c_re', 'm_s5_c_im', 'm_s5_d', 'm_s5_w_glu', 'm_s5_b_glu', 'm_cv_w_dw', 'm_cv_b_dw', 'm_cv_ln_g', 'm_cv_ln_b', 'm_cv_w_pw', 'm_cv_b_pw', 'm_lru_w_conv', 'm_lru_b_conv', 'm_lru_w_r', 'm_lru_b_r', 'm_lru_w_i', 'm_lru_b_i', 'm_lru_lam', 'm_pool_w', 'm_pool_scale', 'm_w_out', 'm_norm_ffn_g', 'm_ffn_w_up', 'm_ffn_w_dw', 'm_ffn_b_dw', 'm_ffn_w_down', 'm_norm_final_g', 'v_norm_mix_g', 'v_w_in', 'v_s5_lam_re', 'v_s5_lam_im', 'v_s5_log_step', 'v_s5_b_re', 'v_s5_b_im', 'v_s5_c_re', 'v_s5_c_im', 'v_s5_d', 'v_s5_w_glu', 'v_s5_b_glu', 'v_cv_w_dw', 'v_cv_b_dw', 'v_cv_ln_g', 'v_cv_ln_b', 'v_cv_w_pw', 'v_cv_b_pw', 'v_lru_w_conv', 'v_lru_b_conv', 'v_lru_w_r', 'v_lru_b_r', 'v_lru_w_i', 'v_lru_b_i', 'v_lru_lam', 'v_pool_w', 'v_pool_scale', 'v_w_out', 'v_norm_ffn_g', 'v_ffn_w_up', 'v_ffn_w_dw', 'v_ffn_b_dw', 'v_ffn_w_down', 'v_norm_final_g']
TWIN_OUTPUTS = ['loss', 'grad_x', 'grad_norm_mix_g', 'grad_w_in', 'grad_s5_lam_re', 'grad_s5_lam_im', 'grad_s5_log_step', 'grad_s5_b_re', 'grad_s5_b_im', 'grad_s5_c_re', 'grad_s5_c_im', 'grad_s5_d', 'grad_s5_w_glu', 'grad_s5_b_glu', 'grad_cv_w_dw', 'grad_cv_b_dw', 'grad_cv_ln_g', 'grad_cv_ln_b', 'grad_cv_w_pw', 'grad_cv_b_pw', 'grad_lru_w_conv', 'grad_lru_b_conv', 'grad_lru_w_r', 'grad_lru_b_r', 'grad_lru_w_i', 'grad_lru_b_i', 'grad_lru_lam', 'grad_pool_w', 'grad_pool_scale', 'grad_w_out', 'grad_norm_ffn_g', 'grad_ffn_w_up', 'grad_ffn_w_dw', 'grad_ffn_b_dw', 'grad_ffn_w_down', 'grad_norm_final_g', 'delta_norm_mix_g', 'delta_w_in', 'delta_s5_lam_re', 'delta_s5_lam_im', 'delta_s5_log_step', 'delta_s5_b_re', 'delta_s5_b_im', 'delta_s5_c_re', 'delta_s5_c_im', 'delta_s5_d', 'delta_s5_w_glu', 'delta_s5_b_glu', 'delta_cv_w_dw', 'delta_cv_b_dw', 'delta_cv_ln_g', 'delta_cv_ln_b', 'delta_cv_w_pw', 'delta_cv_b_pw', 'delta_lru_w_conv', 'delta_lru_b_conv', 'delta_lru_w_r', 'delta_lru_b_r', 'delta_lru_w_i', 'delta_lru_b_i', 'delta_lru_lam', 'delta_pool_w', 'delta_pool_scale', 'delta_w_out', 'delta_norm_ffn_g', 'delta_ffn_w_up', 'delta_ffn_w_dw', 'delta_ffn_b_dw', 'delta_ffn_w_down', 'delta_norm_final_g', 'new_m_norm_mix_g', 'new_m_w_in', 'new_m_s5_lam_re', 'new_m_s5_lam_im', 'new_m_s5_log_step', 'new_m_s5_b_re', 'new_m_s5_b_im', 'new_m_s5_c_re', 'new_m_s5_c_im', 'new_m_s5_d', 'new_m_s5_w_glu', 'new_m_s5_b_glu', 'new_m_cv_w_dw', 'new_m_cv_b_dw', 'new_m_cv_ln_g', 'new_m_cv_ln_b', 'new_m_cv_w_pw', 'new_m_cv_b_pw', 'new_m_lru_w_conv', 'new_m_lru_b_conv', 'new_m_lru_w_r', 'new_m_lru_b_r', 'new_m_lru_w_i', 'new_m_lru_b_i', 'new_m_lru_lam', 'new_m_pool_w', 'new_m_pool_scale', 'new_m_w_out', 'new_m_norm_ffn_g', 'new_m_ffn_w_up', 'new_m_ffn_w_dw', 'new_m_ffn_b_dw', 'new_m_ffn_w_down', 'new_m_norm_final_g', 'new_v_norm_mix_g', 'new_v_w_in', 'new_v_s5_lam_re', 'new_v_s5_lam_im', 'new_v_s5_log_step', 'new_v_s5_b_re', 'new_v_s5_b_im', 'new_v_s5_c_re', 'new_v_s5_c_im', 'new_v_s5_d', 'new_v_s5_w_glu', 'new_v_s5_b_glu', 'new_v_cv_w_dw', 'new_v_cv_b_dw', 'new_v_cv_ln_g', 'new_v_cv_ln_b', 'new_v_cv_w_pw', 'new_v_cv_b_pw', 'new_v_lru_w_conv', 'new_v_lru_b_conv', 'new_v_lru_w_r', 'new_v_lru_b_r', 'new_v_lru_w_i', 'new_v_lru_b_i', 'new_v_lru_lam', 'new_v_pool_w', 'new_v_pool_scale', 'new_v_w_out', 'new_v_norm_ffn_g', 'new_v_ffn_w_up', 'new_v_ffn_w_dw', 'new_v_ffn_b_dw', 'new_v_ffn_w_down', 'new_v_norm_final_g']
TWIN_LEAF_KINDS = {'loss': 'loss', 'grad_x': 'grad_x', 'grad_norm_mix_g': 'grad_w', 'grad_w_in': 'grad_w', 'grad_s5_lam_re': 'grad_w', 'grad_s5_lam_im': 'grad_w', 'grad_s5_log_step': 'grad_w', 'grad_s5_b_re': 'grad_w', 'grad_s5_b_im': 'grad_w', 'grad_s5_c_re': 'grad_w', 'grad_s5_c_im': 'grad_w', 'grad_s5_d': 'grad_w', 'grad_s5_w_glu': 'grad_w', 'grad_s5_b_glu': 'grad_w', 'grad_cv_w_dw': 'grad_w', 'grad_cv_b_dw': 'grad_w', 'grad_cv_ln_g': 'grad_w', 'grad_cv_ln_b': 'grad_w', 'grad_cv_w_pw': 'grad_w', 'grad_cv_b_pw': 'grad_w', 'grad_lru_w_conv': 'grad_w', 'grad_lru_b_conv': 'grad_w', 'grad_lru_w_r': 'grad_w', 'grad_lru_b_r': 'grad_w', 'grad_lru_w_i': 'grad_w', 'grad_lru_b_i': 'grad_w', 'grad_lru_lam': 'grad_w', 'grad_pool_w': 'grad_w', 'grad_pool_scale': 'grad_w', 'grad_w_out': 'grad_w', 'grad_norm_ffn_g': 'grad_w', 'grad_ffn_w_up': 'grad_w', 'grad_ffn_w_dw': 'grad_w', 'grad_ffn_b_dw': 'grad_w', 'grad_ffn_w_down': 'grad_w', 'grad_norm_final_g': 'grad_w', 'delta_norm_mix_g': 'delta_w', 'delta_w_in': 'delta_w', 'delta_s5_lam_re': 'delta_w', 'delta_s5_lam_im': 'delta_w', 'delta_s5_log_step': 'delta_w', 'delta_s5_b_re': 'delta_w', 'delta_s5_b_im': 'delta_w', 'delta_s5_c_re': 'delta_w', 'delta_s5_c_im': 'delta_w', 'delta_s5_d': 'delta_w', 'delta_s5_w_glu': 'delta_w', 'delta_s5_b_glu': 'delta_w', 'delta_cv_w_dw': 'delta_w', 'delta_cv_b_dw': 'delta_w', 'delta_cv_ln_g': 'delta_w', 'delta_cv_ln_b': 'delta_w', 'delta_cv_w_pw': 'delta_w', 'delta_cv_b_pw': 'delta_w', 'delta_lru_w_conv': 'delta_w', 'delta_lru_b_conv': 'delta_w', 'delta_lru_w_r': 'delta_w', 'delta_lru_b_r': 'delta_w', 'delta_lru_w_i': 'delta_w', 'delta_lru_b_i': 'delta_w', 'delta_lru_lam': 'delta_w', 'delta_pool_w': 'delta_w', 'delta_pool_scale': 'delta_w', 'delta_w_out': 'delta_w', 'delta_norm_ffn_g': 'delta_w', 'delta_ffn_w_up': 'delta_w', 'delta_ffn_w_dw': 'delta_w', 'delta_ffn_b_dw': 'delta_w', 'delta_ffn_w_down': 'delta_w', 'delta_norm_final_g': 'delta_w', 'new_m_norm_mix_g': 'new_m', 'new_m_w_in': 'new_m', 'new_m_s5_lam_re': 'new_m', 'new_m_s5_lam_im': 'new_m', 'new_m_s5_log_step': 'new_m', 'new_m_s5_b_re': 'new_m', 'new_m_s5_b_im': 'new_m', 'new_m_s5_c_re': 'new_m', 'new_m_s5_c_im': 'new_m', 'new_m_s5_d': 'new_m', 'new_m_s5_w_glu': 'new_m', 'new_m_s5_b_glu': 'new_m', 'new_m_cv_w_dw': 'new_m', 'new_m_cv_b_dw': 'new_m', 'new_m_cv_ln_g': 'new_m', 'new_m_cv_ln_b': 'new_m', 'new_m_cv_w_pw': 'new_m', 'new_m_cv_b_pw': 'new_m', 'new_m_lru_w_conv': 'new_m', 'new_m_lru_b_conv': 'new_m', 'new_m_lru_w_r': 'new_m', 'new_m_lru_b_r': 'new_m', 'new_m_lru_w_i': 'new_m', 'new_m_lru_b_i': 'new_m', 'new_m_lru_lam': 'new_m', 'new_m_pool_w': 'new_m', 'new_m_pool_scale': 'new_m', 'new_m_w_out': 'new_m', 'new_m_norm_ffn_g': 'new_m', 'new_m_ffn_w_up': 'new_m', 'new_m_ffn_w_dw': 'new_m', 'new_m_ffn_b_dw': 'new_m', 'new_m_ffn_w_down': 'new_m', 'new_m_norm_final_g': 'new_m', 'new_v_norm_mix_g': 'new_v', 'new_v_w_in': 'new_v', 'new_v_s5_lam_re': 'new_v', 'new_v_s5_lam_im': 'new_v', 'new_v_s5_log_step': 'new_v', 'new_v_s5_b_re': 'new_v', 'new_v_s5_b_im': 'new_v', 'new_v_s5_c_re': 'new_v', 'new_v_s5_c_im': 'new_v', 'new_v_s5_d': 'new_v', 'new_v_s5_w_glu': 'new_v', 'new_v_s5_b_glu': 'new_v', 'new_v_cv_w_dw': 'new_v', 'new_v_cv_b_dw': 'new_v', 'new_v_cv_ln_g': 'new_v', 'new_v_cv_ln_b': 'new_v', 'new_v_cv_w_pw': 'new_v', 'new_v_cv_b_pw': 'new_v', 'new_v_lru_w_conv': 'new_v', 'new_v_lru_b_conv': 'new_v', 'new_v_lru_w_r': 'new_v', 'new_v_lru_b_r': 'new_v', 'new_v_lru_w_i': 'new_v', 'new_v_lru_b_i': 'new_v', 'new_v_lru_lam': 'new_v', 'new_v_pool_w': 'new_v', 'new_v_pool_scale': 'new_v', 'new_v_w_out': 'new_v', 'new_v_norm_ffn_g': 'new_v', 'new_v_ffn_w_up': 'new_v', 'new_v_ffn_w_dw': 'new_v', 'new_v_ffn_b_dw': 'new_v', 'new_v_ffn_w_down': 'new_v', 'new_v_norm_final_g': 'new_v'}


def _forward(args):
    return _fwd_reference(*[args[k] for k in FWD_PARAMS])


def _output_shape():
    def fwd():
        inp = _fwd_setup_inputs(0)
        return _fwd_reference(*[inp[k] for k in FWD_PARAMS])
    out = _jax.eval_shape(fwd)
    return out.shape, out.dtype

N_MICROBATCH = 1
ADAM_LR = 0.001
ADAM_B1 = 0.9
ADAM_B2 = 0.999
ADAM_EPS = 1e-08
ADAM_WD = 0.01
ADAM_STEP = 10
PER_EXAMPLE_BATCH_AXIS = {'x': 0, 'loss_target': 0}
SHARED_INPUTS = []
_WEIGHT_DTYPES = {'norm_mix_g': _jnp.float32, 'w_in': _jnp.float32, 's5_lam_re': _jnp.float32, 's5_lam_im': _jnp.float32, 's5_log_step': _jnp.float32, 's5_b_re': _jnp.float32, 's5_b_im': _jnp.float32, 's5_c_re': _jnp.float32, 's5_c_im': _jnp.float32, 's5_d': _jnp.float32, 's5_w_glu': _jnp.float32, 's5_b_glu': _jnp.float32, 'cv_w_dw': _jnp.float32, 'cv_b_dw': _jnp.float32, 'cv_ln_g': _jnp.float32, 'cv_ln_b': _jnp.float32, 'cv_w_pw': _jnp.float32, 'cv_b_pw': _jnp.float32, 'lru_w_conv': _jnp.float32, 'lru_b_conv': _jnp.float32, 'lru_w_r': _jnp.float32, 'lru_b_r': _jnp.float32, 'lru_w_i': _jnp.float32, 'lru_b_i': _jnp.float32, 'lru_lam': _jnp.float32, 'pool_w': _jnp.float32, 'pool_scale': _jnp.float32, 'w_out': _jnp.float32, 'norm_ffn_g': _jnp.float32, 'ffn_w_up': _jnp.float32, 'ffn_w_dw': _jnp.float32, 'ffn_b_dw': _jnp.float32, 'ffn_w_down': _jnp.float32, 'norm_final_g': _jnp.float32}
MOMENT_SCALE = {'norm_mix_g': 5.524405e-02, 'w_in': 4.491174e-02, 's5_lam_re': 1.464161e-03, 's5_lam_im': 1.553525e-03, 's5_log_step': 1.170819e+00, 's5_b_re': 9.754284e-04, 's5_b_im': 9.890522e-04, 's5_c_re': 1.982450e-03, 's5_c_im': 1.978814e-03, 's5_d': 3.487053e-02, 's5_w_glu': 8.634905e-03, 's5_b_glu': 1.465973e-02, 'cv_w_dw': 5.074947e-02, 'cv_b_dw': 1.146475e-01, 'cv_ln_g': 6.659648e-02, 'cv_ln_b': 6.143285e-02, 'cv_w_pw': 5.097047e-02, 'cv_b_pw': 1.141650e-01, 'lru_w_conv': 4.636019e-02, 'lru_b_conv': 3.727016e-01, 'lru_w_r': 1.266074e-02, 'lru_b_r': 1.145400e-02, 'lru_w_i': 2.311761e-02, 'lru_b_i': 1.633832e-02, 'lru_lam': 2.293824e-02, 'pool_w': 7.075014e-02, 'pool_scale': 7.622158e-02, 'w_out': 5.110615e-02, 'norm_ffn_g': 6.247268e-02, 'ffn_w_up': 2.642116e-02, 'ffn_w_dw': 2.677923e-02, 'ffn_b_dw': 2.600161e-02, 'ffn_w_down': 4.260410e-02, 'norm_final_g': 1.599115e+01}


def _to_microbatches(a, axis):
    t = _jnp.moveaxis(a, axis, 0)
    t = t.reshape((N_MICROBATCH, t.shape[0] // N_MICROBATCH) + t.shape[1:])
    return _jnp.moveaxis(t, 1, axis + 1)


def setup_inputs(seed: int = 0) -> dict:
    inp = _fwd_setup_inputs(seed)
    key = _jax.random.fold_in(_jax.random.key(seed), 7919)
    shape, _ = _output_shape()
    out = dict(inp)
    out["loss_target"] = _jax.random.normal(_jax.random.fold_in(key, 0), shape, _jnp.float32)
    for i, name in enumerate(TWIN_WEIGHTS):
        w = inp[name].astype(_jnp.float32)
        if MOMENT_SCALE is None:
            s = _jnp.sqrt(_jnp.mean(_jnp.square(w)) + 1e-30)
        else:
            s = MOMENT_SCALE[name]
        km, kv = _jax.random.split(_jax.random.fold_in(key, i + 1))
        out[name] = w
        out["m_" + name] = s * _jax.random.normal(km, w.shape, _jnp.float32)
        out["v_" + name] = (s * s) * _jax.random.uniform(kv, w.shape, _jnp.float32, 0.5, 1.5)
    if N_MICROBATCH > 1:
        for name, axis in PER_EXAMPLE_BATCH_AXIS.items():
            out[name] = _to_microbatches(out[name], axis)
    return {'x': out['x'], 'norm_mix_g': out['norm_mix_g'], 'w_in': out['w_in'], 's5_lam_re': out['s5_lam_re'], 's5_lam_im': out['s5_lam_im'], 's5_log_step': out['s5_log_step'], 's5_b_re': out['s5_b_re'], 's5_b_im': out['s5_b_im'], 's5_c_re': out['s5_c_re'], 's5_c_im': out['s5_c_im'], 's5_d': out['s5_d'], 's5_w_glu': out['s5_w_glu'], 's5_b_glu': out['s5_b_glu'], 'cv_w_dw': out['cv_w_dw'], 'cv_b_dw': out['cv_b_dw'], 'cv_ln_g': out['cv_ln_g'], 'cv_ln_b': out['cv_ln_b'], 'cv_w_pw': out['cv_w_pw'], 'cv_b_pw': out['cv_b_pw'], 'lru_w_conv': out['lru_w_conv'], 'lru_b_conv': out['lru_b_conv'], 'lru_w_r': out['lru_w_r'], 'lru_b_r': out['lru_b_r'], 'lru_w_i': out['lru_w_i'], 'lru_b_i': out['lru_b_i'], 'lru_lam': out['lru_lam'], 'pool_w': out['pool_w'], 'pool_scale': out['pool_scale'], 'w_out': out['w_out'], 'norm_ffn_g': out['norm_ffn_g'], 'ffn_w_up': out['ffn_w_up'], 'ffn_w_dw': out['ffn_w_dw'], 'ffn_b_dw': out['ffn_b_dw'], 'ffn_w_down': out['ffn_w_down'], 'norm_final_g': out['norm_final_g'], 'loss_target': out['loss_target'], 'm_norm_mix_g': out['m_norm_mix_g'], 'm_w_in': out['m_w_in'], 'm_s5_lam_re': out['m_s5_lam_re'], 'm_s5_lam_im': out['m_s5_lam_im'], 'm_s5_log_step': out['m_s5_log_step'], 'm_s5_b_re': out['m_s5_b_re'], 'm_s5_b_im': out['m_s5_b_im'], 'm_s5_c_re': out['m_s5_c_re'], 'm_s5_c_im': out['m_s5_c_im'], 'm_s5_d': out['m_s5_d'], 'm_s5_w_glu': out['m_s5_w_glu'], 'm_s5_b_glu': out['m_s5_b_glu'], 'm_cv_w_dw': out['m_cv_w_dw'], 'm_cv_b_dw': out['m_cv_b_dw'], 'm_cv_ln_g': out['m_cv_ln_g'], 'm_cv_ln_b': out['m_cv_ln_b'], 'm_cv_w_pw': out['m_cv_w_pw'], 'm_cv_b_pw': out['m_cv_b_pw'], 'm_lru_w_conv': out['m_lru_w_conv'], 'm_lru_b_conv': out['m_lru_b_conv'], 'm_lru_w_r': out['m_lru_w_r'], 'm_lru_b_r': out['m_lru_b_r'], 'm_lru_w_i': out['m_lru_w_i'], 'm_lru_b_i': out['m_lru_b_i'], 'm_lru_lam': out['m_lru_lam'], 'm_pool_w': out['m_pool_w'], 'm_pool_scale': out['m_pool_scale'], 'm_w_out': out['m_w_out'], 'm_norm_ffn_g': out['m_norm_ffn_g'], 'm_ffn_w_up': out['m_ffn_w_up'], 'm_ffn_w_dw': out['m_ffn_w_dw'], 'm_ffn_b_dw': out['m_ffn_b_dw'], 'm_ffn_w_down': out['m_ffn_w_down'], 'm_norm_final_g': out['m_norm_final_g'], 'v_norm_mix_g': out['v_norm_mix_g'], 'v_w_in': out['v_w_in'], 'v_s5_lam_re': out['v_s5_lam_re'], 'v_s5_lam_im': out['v_s5_lam_im'], 'v_s5_log_step': out['v_s5_log_step'], 'v_s5_b_re': out['v_s5_b_re'], 'v_s5_b_im': out['v_s5_b_im'], 'v_s5_c_re': out['v_s5_c_re'], 'v_s5_c_im': out['v_s5_c_im'], 'v_s5_d': out['v_s5_d'], 'v_s5_w_glu': out['v_s5_w_glu'], 'v_s5_b_glu': out['v_s5_b_glu'], 'v_cv_w_dw': out['v_cv_w_dw'], 'v_cv_b_dw': out['v_cv_b_dw'], 'v_cv_ln_g': out['v_cv_ln_g'], 'v_cv_ln_b': out['v_cv_ln_b'], 'v_cv_w_pw': out['v_cv_w_pw'], 'v_cv_b_pw': out['v_cv_b_pw'], 'v_lru_w_conv': out['v_lru_w_conv'], 'v_lru_b_conv': out['v_lru_b_conv'], 'v_lru_w_r': out['v_lru_w_r'], 'v_lru_b_r': out['v_lru_b_r'], 'v_lru_w_i': out['v_lru_w_i'], 'v_lru_b_i': out['v_lru_b_i'], 'v_lru_lam': out['v_lru_lam'], 'v_pool_w': out['v_pool_w'], 'v_pool_scale': out['v_pool_scale'], 'v_w_out': out['v_w_out'], 'v_norm_ffn_g': out['v_norm_ffn_g'], 'v_ffn_w_up': out['v_ffn_w_up'], 'v_ffn_w_dw': out['v_ffn_w_dw'], 'v_ffn_b_dw': out['v_ffn_b_dw'], 'v_ffn_w_down': out['v_ffn_w_down'], 'v_norm_final_g': out['v_norm_final_g']}


def _loss(weights, diff, rest, loss_target):
    with _jax.named_scope("forward"):
        args = {**rest, TWIN_DIFF_INPUT: diff, **{k: w.astype(_WEIGHT_DTYPES[k]) for k, w in weights.items()}}
        y = _forward(args)
    with _jax.named_scope("loss_head"):
        err = _jnp.square(y.astype(_jnp.float32) - loss_target)
        return 0.5 * _jnp.sum(_jnp.mean(err, axis=-1)) if err.ndim else 0.5 * err


def _adamw(w, g, m, v):
    m = ADAM_B1 * m + (1.0 - ADAM_B1) * g
    v = ADAM_B2 * v + (1.0 - ADAM_B2) * _jnp.square(g)
    m_hat = m / (1.0 - ADAM_B1 ** ADAM_STEP)
    v_hat = v / (1.0 - ADAM_B2 ** ADAM_STEP)
    delta = -ADAM_LR * (m_hat / (_jnp.sqrt(v_hat) + ADAM_EPS) + ADAM_WD * w)
    return delta, m, v


def reference(x, norm_mix_g, w_in, s5_lam_re, s5_lam_im, s5_log_step, s5_b_re, s5_b_im, s5_c_re, s5_c_im, s5_d, s5_w_glu, s5_b_glu, cv_w_dw, cv_b_dw, cv_ln_g, cv_ln_b, cv_w_pw, cv_b_pw, lru_w_conv, lru_b_conv, lru_w_r, lru_b_r, lru_w_i, lru_b_i, lru_lam, pool_w, pool_scale, w_out, norm_ffn_g, ffn_w_up, ffn_w_dw, ffn_b_dw, ffn_w_down, norm_final_g, loss_target, m_norm_mix_g, m_w_in, m_s5_lam_re, m_s5_lam_im, m_s5_log_step, m_s5_b_re, m_s5_b_im, m_s5_c_re, m_s5_c_im, m_s5_d, m_s5_w_glu, m_s5_b_glu, m_cv_w_dw, m_cv_b_dw, m_cv_ln_g, m_cv_ln_b, m_cv_w_pw, m_cv_b_pw, m_lru_w_conv, m_lru_b_conv, m_lru_w_r, m_lru_b_r, m_lru_w_i, m_lru_b_i, m_lru_lam, m_pool_w, m_pool_scale, m_w_out, m_norm_ffn_g, m_ffn_w_up, m_ffn_w_dw, m_ffn_b_dw, m_ffn_w_down, m_norm_final_g, v_norm_mix_g, v_w_in, v_s5_lam_re, v_s5_lam_im, v_s5_log_step, v_s5_b_re, v_s5_b_im, v_s5_c_re, v_s5_c_im, v_s5_d, v_s5_w_glu, v_s5_b_glu, v_cv_w_dw, v_cv_b_dw, v_cv_ln_g, v_cv_ln_b, v_cv_w_pw, v_cv_b_pw, v_lru_w_conv, v_lru_b_conv, v_lru_w_r, v_lru_b_r, v_lru_w_i, v_lru_b_i, v_lru_lam, v_pool_w, v_pool_scale, v_w_out, v_norm_ffn_g, v_ffn_w_up, v_ffn_w_dw, v_ffn_b_dw, v_ffn_w_down, v_norm_final_g):
    given = dict(x=x, norm_mix_g=norm_mix_g, w_in=w_in, s5_lam_re=s5_lam_re, s5_lam_im=s5_lam_im, s5_log_step=s5_log_step, s5_b_re=s5_b_re, s5_b_im=s5_b_im, s5_c_re=s5_c_re, s5_c_im=s5_c_im, s5_d=s5_d, s5_w_glu=s5_w_glu, s5_b_glu=s5_b_glu, cv_w_dw=cv_w_dw, cv_b_dw=cv_b_dw, cv_ln_g=cv_ln_g, cv_ln_b=cv_ln_b, cv_w_pw=cv_w_pw, cv_b_pw=cv_b_pw, lru_w_conv=lru_w_conv, lru_b_conv=lru_b_conv, lru_w_r=lru_w_r, lru_b_r=lru_b_r, lru_w_i=lru_w_i, lru_b_i=lru_b_i, lru_lam=lru_lam, pool_w=pool_w, pool_scale=pool_scale, w_out=w_out, norm_ffn_g=norm_ffn_g, ffn_w_up=ffn_w_up, ffn_w_dw=ffn_w_dw, ffn_b_dw=ffn_b_dw, ffn_w_down=ffn_w_down, norm_final_g=norm_final_g, loss_target=loss_target, m_norm_mix_g=m_norm_mix_g, m_w_in=m_w_in, m_s5_lam_re=m_s5_lam_re, m_s5_lam_im=m_s5_lam_im, m_s5_log_step=m_s5_log_step, m_s5_b_re=m_s5_b_re, m_s5_b_im=m_s5_b_im, m_s5_c_re=m_s5_c_re, m_s5_c_im=m_s5_c_im, m_s5_d=m_s5_d, m_s5_w_glu=m_s5_w_glu, m_s5_b_glu=m_s5_b_glu, m_cv_w_dw=m_cv_w_dw, m_cv_b_dw=m_cv_b_dw, m_cv_ln_g=m_cv_ln_g, m_cv_ln_b=m_cv_ln_b, m_cv_w_pw=m_cv_w_pw, m_cv_b_pw=m_cv_b_pw, m_lru_w_conv=m_lru_w_conv, m_lru_b_conv=m_lru_b_conv, m_lru_w_r=m_lru_w_r, m_lru_b_r=m_lru_b_r, m_lru_w_i=m_lru_w_i, m_lru_b_i=m_lru_b_i, m_lru_lam=m_lru_lam, m_pool_w=m_pool_w, m_pool_scale=m_pool_scale, m_w_out=m_w_out, m_norm_ffn_g=m_norm_ffn_g, m_ffn_w_up=m_ffn_w_up, m_ffn_w_dw=m_ffn_w_dw, m_ffn_b_dw=m_ffn_b_dw, m_ffn_w_down=m_ffn_w_down, m_norm_final_g=m_norm_final_g, v_norm_mix_g=v_norm_mix_g, v_w_in=v_w_in, v_s5_lam_re=v_s5_lam_re, v_s5_lam_im=v_s5_lam_im, v_s5_log_step=v_s5_log_step, v_s5_b_re=v_s5_b_re, v_s5_b_im=v_s5_b_im, v_s5_c_re=v_s5_c_re, v_s5_c_im=v_s5_c_im, v_s5_d=v_s5_d, v_s5_w_glu=v_s5_w_glu, v_s5_b_glu=v_s5_b_glu, v_cv_w_dw=v_cv_w_dw, v_cv_b_dw=v_cv_b_dw, v_cv_ln_g=v_cv_ln_g, v_cv_ln_b=v_cv_ln_b, v_cv_w_pw=v_cv_w_pw, v_cv_b_pw=v_cv_b_pw, v_lru_w_conv=v_lru_w_conv, v_lru_b_conv=v_lru_b_conv, v_lru_w_r=v_lru_w_r, v_lru_b_r=v_lru_b_r, v_lru_w_i=v_lru_w_i, v_lru_b_i=v_lru_b_i, v_lru_lam=v_lru_lam, v_pool_w=v_pool_w, v_pool_scale=v_pool_scale, v_w_out=v_w_out, v_norm_ffn_g=v_norm_ffn_g, v_ffn_w_up=v_ffn_w_up, v_ffn_w_dw=v_ffn_w_dw, v_ffn_b_dw=v_ffn_b_dw, v_ffn_w_down=v_ffn_w_down, v_norm_final_g=v_norm_final_g)
    weights = {n: given[n] for n in TWIN_WEIGHTS}
    shared = {n: given[n] for n in SHARED_INPUTS}
    per_example = {n: given[n] for n in ['x']}
    grad_fn = _jax.value_and_grad(_loss, argnums=(0, 1))

    def one_microbatch(ex, loss_target):
        ex = dict(ex)
        diff = ex.pop(TWIN_DIFF_INPUT)
        return grad_fn(weights, diff, {**shared, **ex}, loss_target)

    if N_MICROBATCH == 1:
        loss, (grad_w, grad_x) = one_microbatch(per_example, given["loss_target"])
    else:
        def body(carry, xs):
            loss_sum, grad_sum = carry
            l_k, (gw_k, gx_k) = one_microbatch(xs[0], xs[1])
            with _jax.named_scope("update"):
                return (loss_sum + l_k, _jax.tree.map(_jnp.add, grad_sum, gw_k)), gx_k

        init = (_jnp.zeros((), _jnp.float32), _jax.tree.map(_jnp.zeros_like, weights))
        (loss, grad_w), grad_x = _jax.lax.scan(body, init, (per_example, given["loss_target"]))
    with _jax.named_scope("update"):
        delta_w, new_m, new_v = {}, {}, {}
        for n in TWIN_WEIGHTS:
            delta_w[n], new_m[n], new_v[n] = _adamw(weights[n], grad_w[n], given["m_" + n], given["v_" + n])
    return (loss, grad_x, *[grad_w[n] for n in TWIN_WEIGHTS], *[delta_w[n] for n in TWIN_WEIGHTS],
            *[new_m[n] for n in TWIN_WEIGHTS], *[new_v[n] for n in TWIN_WEIGHTS])
```

```python
import functools
import math

import jax
import jax.numpy as jnp
from jax import lax
from jax.experimental import pallas as pl
from jax.experimental.pallas import tpu as pltpu

F32 = jnp.float32
BF16 = jnp.bfloat16
N_DEV = 8
AXES = ("x", "y", "c")
EPS = 1e-6
S5_GROUP_CH = 16
S5_STATE = 64
LRU_HEADS = 8
LRU_C = 8.0
POOL_WINDOWS = (2, 4, 8, 16)
CONV_HALO = 32
SMALL_HALO = 8
POOL_HALO = 16
ADAM_LR, ADAM_B1, ADAM_B2, ADAM_EPS, ADAM_WD, ADAM_STEP = 0.001, 0.9, 0.999, 1e-08, 0.01, 10
_GELU_K = math.sqrt(2.0 / math.pi)
ARB = pltpu.CompilerParams(dimension_semantics=("arbitrary",))


def _pick(n, pref, mult):
    best = None
    for t in range(mult, min(n, pref) + 1, mult):
        if n % t == 0:
            best = t
    return best if best is not None else n


def _sigmoid(x):
    return 1.0 / (1.0 + jnp.exp(-x))


def _gelu(x):
    return 0.5 * x * (1.0 + jnp.tanh(_GELU_K * (x + 0.044715 * x * x * x)))


def _gelu_grad(x):
    t = jnp.tanh(_GELU_K * (x + 0.044715 * x * x * x))
    return 0.5 * (1.0 + t) + 0.5 * x * (1.0 - t * t) * _GELU_K * (1.0 + 3.0 * 0.044715 * x * x)


def _neg_expm1(x):
    series = -x * (1.0 + x * (0.5 + x * (1.0 / 6.0 + x * (1.0 / 24.0 + x * (1.0 / 120.0 + x * (1.0 / 720.0))))))
    return jnp.where(jnp.abs(x) < 0.25, series, 1.0 - jnp.exp(x))


def _fold8(x):
    return x.reshape(x.shape[0] // 8, 8, x.shape[1]).sum(axis=0)


def _dot(a, b):
    return jnp.dot(a.astype(BF16), b.astype(BF16), preferred_element_type=F32)


def _dot_nt(a, b):
    return lax.dot_general(a.astype(BF16), b.astype(BF16), (((1,), (1,)), ((), ())), preferred_element_type=F32)


def _dot_tn(a, b):
    return lax.dot_general(a.astype(BF16), b.astype(BF16), (((0,), (0,)), ((), ())), preferred_element_type=F32)


def _full(shape):
    nd = len(shape)
    return pl.BlockSpec(shape, lambda *i: (0,) * nd)


def _matmul(a, b, *, mode, tm, tn, tk, out_dtype, name, add=None):
    if mode == "nn":
        (M, K), N = a.shape, b.shape[1]
    elif mode == "nt":
        (M, K), N = a.shape, b.shape[0]
    else:
        (K, M), N = a.shape, b.shape[1]
    tm, tn, tk = _pick(M, tm, 8), _pick(N, tn, 128), _pick(K, tk, 128)
    if mode == "tn":
        tm = _pick(M, tm, 128)
    nk = K // tk
    a_spec = {"nn": pl.BlockSpec((tm, tk), lambda i, j, k: (i, k)), "nt": pl.BlockSpec((tm, tk), lambda i, j, k: (i, k)),
              "tn": pl.BlockSpec((tk, tm), lambda i, j, k: (k, i))}[mode]
    b_spec = {"nn": pl.BlockSpec((tk, tn), lambda i, j, k: (k, j)), "nt": pl.BlockSpec((tn, tk), lambda i, j, k: (j, k)),
              "tn": pl.BlockSpec((tk, tn), lambda i, j, k: (k, j))}[mode]
    o_spec = pl.BlockSpec((tm, tn), lambda i, j, k: (i, j))
    dot = {"nn": _dot, "nt": _dot_nt, "tn": _dot_tn}[mode]
    has_add = add is not None

    def body(*refs):
        if has_add:
            a_ref, b_ref, add_ref, o_ref, acc_ref = refs
        else:
            a_ref, b_ref, o_ref, acc_ref = refs
        k = pl.program_id(2)

        @pl.when(k == 0)
        def _():
            acc_ref[...] = jnp.zeros_like(acc_ref)

        acc_ref[...] += dot(a_ref[...], b_ref[...])

        @pl.when(k == nk - 1)
        def _():
            r = acc_ref[...]
            if has_add:
                r = r + add_ref[...]
            o_ref[...] = r.astype(out_dtype)

    ins = [a, b] + ([add] if has_add else [])
    specs = [a_spec, b_spec] + ([o_spec] if has_add else [])
    return pl.pallas_call(
        body, name=name, grid=(M // tm, N // tn, nk), in_specs=specs, out_specs=o_spec,
        out_shape=jax.ShapeDtypeStruct((M, N), out_dtype), scratch_shapes=[pltpu.VMEM((tm, tn), F32)],
        compiler_params=pltpu.CompilerParams(dimension_semantics=("parallel", "parallel", "arbitrary")),
    )(*ins)


def _norm_matmul(x, g, wt, *, tm, tn, name):
    T, D = x.shape
    N = wt.shape[0]
    tm, tn = _pick(T, tm, 8), _pick(N, tn, 128)

    def body(x_ref, g_ref, w_ref, o_ref, h_ref):
        @pl.when(pl.program_id(1) == 0)
        def _():
            xv = x_ref[...]
            r = lax.rsqrt(jnp.mean(xv * xv, axis=-1, keepdims=True) + EPS)
            h_ref[...] = (xv * r * g_ref[...]).astype(BF16)

        o_ref[...] = _dot_nt(h_ref[...], w_ref[...])

    return pl.pallas_call(
        body, name=name, grid=(T // tm, N // tn),
        in_specs=[pl.BlockSpec((tm, D), lambda i, j: (i, 0)), pl.BlockSpec((1, D), lambda i, j: (0, 0)),
                  pl.BlockSpec((tn, D), lambda i, j: (j, 0))],
        out_specs=[pl.BlockSpec((tm, tn), lambda i, j: (i, j)), pl.BlockSpec((tm, D), lambda i, j: (i, 0))],
        out_shape=[jax.ShapeDtypeStruct((T, N), F32), jax.ShapeDtypeStruct((T, D), BF16)],
        compiler_params=pltpu.CompilerParams(dimension_semantics=("parallel", "arbitrary")),
    )(x, g.reshape(1, D), wt)


def _rms_bwd(dres, dh, x, g, *, name):
    T, D = x.shape
    tc = _pick(T, 256, 8)

    def body(dres_ref, dh_ref, x_ref, g_ref, dx_ref, dxb_ref, dg_ref):
        @pl.when(pl.program_id(0) == 0)
        def _():
            dg_ref[...] = jnp.zeros_like(dg_ref)

        xv, dhv = x_ref[...], dh_ref[...]
        r = lax.rsqrt(jnp.mean(xv * xv, axis=-1, keepdims=True) + EPS)
        dg_ref[...] += _fold8(dhv * xv * r)
        dyg = dhv * g_ref[...]
        dx = dres_ref[...] + r * dyg - xv * (r * r * r) * jnp.mean(dyg * xv, axis=-1, keepdims=True)
        dx_ref[...] = dx
        dxb_ref[...] = dx.astype(BF16)

    row = pl.BlockSpec((tc, D), lambda i: (i, 0))
    return pl.pallas_call(
        body, name=name, grid=(T // tc,), in_specs=[row, row, row, _full((1, D))],
        out_specs=[row, row, _full((8, D))],
        out_shape=[jax.ShapeDtypeStruct((T, D), F32), jax.ShapeDtypeStruct((T, D), BF16), jax.ShapeDtypeStruct((8, D), F32)],
        compiler_params=ARB,
    )(dres, dh, x, g.reshape(1, D))


def _loss_head(x, g, target):
    T, D = x.shape
    tc = _pick(T, 256, 8)

    def body(x_ref, g_ref, t_ref, dx_ref, dxb_ref, dg_ref, se_ref):
        @pl.when(pl.program_id(0) == 0)
        def _():
            dg_ref[...] = jnp.zeros_like(dg_ref)
            se_ref[...] = jnp.zeros_like(se_ref)

        xv = x_ref[...]
        r = lax.rsqrt(jnp.mean(xv * xv, axis=-1, keepdims=True) + EPS)
        gv = g_ref[...]
        err = xv * r * gv - t_ref[...]
        se_ref[...] += _fold8(err * err)
        dy = err * (1.0 / D)
        dg_ref[...] += _fold8(dy * xv * r)
        dyg = dy * gv
        dx = r * dyg - xv * (r * r * r) * jnp.mean(dyg * xv, axis=-1, keepdims=True)
        dx_ref[...] = dx
        dxb_ref[...] = dx.astype(BF16)

    row = pl.BlockSpec((tc, D), lambda i: (i, 0))
    return pl.pallas_call(
        body, name="loss_head", grid=(T // tc,), in_specs=[row, _full((1, D)), row],
        out_specs=[row, row, _full((8, D)), _full((8, D))],
        out_shape=[jax.ShapeDtypeStruct((T, D), F32), jax.ShapeDtypeStruct((T, D), BF16),
                   jax.ShapeDtypeStruct((8, D), F32), jax.ShapeDtypeStruct((8, D), F32)],
        compiler_params=ARB,
    )(x, g.reshape(1, D), target)


def _s5_scan_consts(ar, ai, reverse):
    if reverse:
        ai = -ai
    pr, pi_ = [ar], [ai]
    for _ in range(7):
        pr, pi_ = pr + [pr[-1] * ar - pi_[-1] * ai], pi_ + [pr[-1] * ai + pi_[-1] * ar]
    rows = jnp.arange(8)[:, None]
    planes = []
    for d in (1, 2, 4):
        mask = (rows <= 7 - d) if reverse else (rows >= d)
        planes += [jnp.where(mask, pr[d - 1][None, :], 0.0), jnp.where(mask, pi_[d - 1][None, :], 0.0)]
    order = list(range(7, -1, -1)) if reverse else list(range(8))
    planes += [jnp.stack([pr[k] for k in order]), jnp.stack([pi_[k] for k in order])]
    return jnp.stack(planes).astype(F32)


def _s5_block_scan(xr_ref, xi_ref, cst_ref, car_ref, nblk, reverse, lb, extra=None):
    ns = xr_ref.shape[1]
    crow = 0 if reverse else 7

    def blk(i, _):
        ib = (nblk - 1 - i) if reverse else i
        r0 = pl.multiple_of(ib * 8, 8)
        for k in range(ns // lb):
            sl = slice(k * lb, (k + 1) * lb)
            xr, xi = xr_ref[pl.ds(r0, 8), sl], xi_ref[pl.ds(r0, 8), sl]
            for n, d in enumerate((1, 2, 4)):
                ar, ai = cst_ref[2 * n, :, sl], cst_ref[2 * n + 1, :, sl]
                sh = (8 - d) if reverse else d
                rr, ri = pltpu.roll(xr, sh, 0), pltpu.roll(xi, sh, 0)
                xr, xi = xr + ar * rr - ai * ri, xi + ar * ri + ai * rr
            pr, pi_ = cst_ref[6, :, sl], cst_ref[7, :, sl]
            cr, ci = car_ref[0, crow:crow + 1, sl], car_ref[1, crow:crow + 1, sl]
            xr, xi = xr + pr * cr - pi_ * ci, xi + pr * ci + pi_ * cr
            xr_ref[pl.ds(r0, 8), sl] = xr
            xi_ref[pl.ds(r0, 8), sl] = xi
            if extra is not None:
                extra(r0, sl, xr, xi, cr, ci)
            car_ref[0, :, sl] = xr
            car_ref[1, :, sl] = xi
        return 0

    lax.fori_loop(0, nblk, blk, 0)


def _s5_fwd(proj, col, bre, bim, cre, cim, cst, dvec, wglu, bglu):
    T = proj.shape[0]
    nsb, cw, sw = bre.shape
    M, NS = nsb * cw, nsb * sw
    tc = _pick(T, 256, 8)

    def body(u_ref, bre_ref, bim_ref, cre_ref, cim_ref, cst_ref, d_ref, wg_ref, bg_ref,
             y_ref, sre_ref, sim_ref, yp_ref, z_ref, car_ref):
        @pl.when(pl.program_id(0) == 0)
        def _():
            car_ref[...] = jnp.zeros_like(car_ref)

        u = u_ref[...]
        ub = u.astype(BF16)
        for k in range(nsb):
            sre_ref[:, k * sw:(k + 1) * sw] = _dot(ub[:, k * cw:(k + 1) * cw], bre_ref[k])
            sim_ref[:, k * sw:(k + 1) * sw] = _dot(ub[:, k * cw:(k + 1) * cw], bim_ref[k])
        _s5_block_scan(sre_ref, sim_ref, cst_ref, car_ref, tc // 8, False, sw)
        for k in range(nsb):
            yk = _dot(sre_ref[:, k * sw:(k + 1) * sw], cre_ref[k]) - _dot(sim_ref[:, k * sw:(k + 1) * sw], cim_ref[k])
            yp_ref[:, k * cw:(k + 1) * cw] = yk + d_ref[:, k * cw:(k + 1) * cw] * u[:, k * cw:(k + 1) * cw]
        gl = _gelu(yp_ref[...])
        z = _dot(gl, wg_ref[...]) + bg_ref[...]
        z_ref[...] = z
        y_ref[...] = (gl * _sigmoid(z)).astype(BF16)

    rowm = pl.BlockSpec((tc, M), lambda i: (i, 0))
    rows = pl.BlockSpec((tc, NS), lambda i: (i, 0))
    return pl.pallas_call(
        body, name="s5_fwd", grid=(T // tc,),
        in_specs=[pl.BlockSpec((tc, M), lambda i: (i, col)), _full(bre.shape), _full(bim.shape), _full(cre.shape),
                  _full(cim.shape), _full(cst.shape), _full((1, M)), _full((M, M)), _full((1, M))],
        out_specs=[rowm, rows, rows, rowm, rowm],
        out_shape=[jax.ShapeDtypeStruct((T, M), BF16), jax.ShapeDtypeStruct((T, NS), F32), jax.ShapeDtypeStruct((T, NS), F32),
                   jax.ShapeDtypeStruct((T, M), F32), jax.ShapeDtypeStruct((T, M), F32)],
        scratch_shapes=[pltpu.VMEM((2, 8, NS), F32)], compiler_params=ARB,
    )(proj, bre, bim, cre, cim, cst, dvec.reshape(1, M), wglu, bglu.reshape(1, M))


def _s5_bwd(dmix, dcol, proj, col, sre, sim, yp, z, bre, bim, cre, cim, cst, dvec, wglu):
    T = proj.shape[0]
    nsb, cw, sw = bre.shape
    M, NS = nsb * cw, nsb * sw
    tc = _pick(T, 256, 8)
    nc = T // tc

    def body(do_ref, u_ref, sre_ref, sim_ref, yp_ref, z_ref, bre_ref, bim_ref, cre_ref, cim_ref, cst_ref, d_ref, wg_ref,
             du_ref, dbre_ref, dbim_ref, dcre_ref, dcim_ref, da_ref, dwg_ref, vec_ref, gre_ref, gim_ref, car_ref):
        @pl.when(pl.program_id(0) == 0)
        def _():
            for r in (dbre_ref, dbim_ref, dcre_ref, dcim_ref, da_ref, dwg_ref, vec_ref, car_ref):
                r[...] = jnp.zeros_like(r)

        u, do, ypv = u_ref[...], do_ref[...], yp_ref[...]
        gl = _gelu(ypv)
        sg = _sigmoid(z_ref[...])
        dz = do * gl * sg * (1.0 - sg)
        dgl = do * sg + _dot_nt(dz, wg_ref[...])
        dwg_ref[...] += _dot_tn(gl, dz)
        vec_ref[0] += _fold8(dz)
        dy = dgl * _gelu_grad(ypv)
        vec_ref[1] += _fold8(dy * u)
        dyb = dy.astype(BF16)
        for k in range(nsb):
            cs, ss = slice(k * cw, (k + 1) * cw), slice(k * sw, (k + 1) * sw)
            gre_ref[:, ss] = _dot_nt(dyb[:, cs], cre_ref[k])
            gim_ref[:, ss] = -_dot_nt(dyb[:, cs], cim_ref[k])
            dcre_ref[k] += _dot_tn(sre_ref[:, ss], dyb[:, cs])
            dcim_ref[k] -= _dot_tn(sim_ref[:, ss], dyb[:, cs])

        row = lax.broadcasted_iota(jnp.int32, (8, sw), 0)

        def extra(r0, sl, gr, gi, cr, ci):
            nr = jnp.where(row == 7, cr, pltpu.roll(gr, 7, 0))
            ni = jnp.where(row == 7, ci, pltpu.roll(gi, 7, 0))
            sr, si = sre_ref[pl.ds(r0, 8), sl], sim_ref[pl.ds(r0, 8), sl]
            da_ref[0, :, sl] += nr * sr + ni * si
            da_ref[1, :, sl] += ni * sr - nr * si

        _s5_block_scan(gre_ref, gim_ref, cst_ref, car_ref, tc // 8, True, sw, extra)
        ub = u.astype(BF16)
        for k in range(nsb):
            cs, ss = slice(k * cw, (k + 1) * cw), slice(k * sw, (k + 1) * sw)
            gr, gi = gre_ref[:, ss].astype(BF16), gim_ref[:, ss].astype(BF16)
            duk = dy[:, cs] * d_ref[:, cs] + _dot_nt(gr, bre_ref[k]) + _dot_nt(gi, bim_ref[k])
            du_ref[:, cs] = duk.astype(BF16)
            dbre_ref[k] += _dot_tn(ub[:, cs], gr)
            dbim_ref[k] += _dot_tn(ub[:, cs], gi)

    def rowm(c):
        return pl.BlockSpec((tc, M), lambda i: (nc - 1 - i, c))
    rows = pl.BlockSpec((tc, NS), lambda i: (nc - 1 - i, 0))
    return pl.pallas_call(
        body, name="s5_bwd", grid=(nc,),
        in_specs=[rowm(dcol), rowm(col), rows, rows, rowm(0), rowm(0), _full(bre.shape), _full(bim.shape), _full(cre.shape),
                  _full(cim.shape), _full(cst.shape), _full((1, M)), _full((M, M))],
        out_specs=[rowm(0), _full(bre.shape), _full(bim.shape), _full(cre.shape), _full(cim.shape), _full((2, 8, NS)),
                   _full((M, M)), _full((3, 8, M))],
        out_shape=[jax.ShapeDtypeStruct((T, M), BF16), jax.ShapeDtypeStruct(bre.shape, F32), jax.ShapeDtypeStruct(bim.shape, F32),
                   jax.ShapeDtypeStruct(cre.shape, F32), jax.ShapeDtypeStruct(cim.shape, F32), jax.ShapeDtypeStruct((2, 8, NS), F32),
                   jax.ShapeDtypeStruct((M, M), F32), jax.ShapeDtypeStruct((3, 8, M), F32)],
        scratch_shapes=[pltpu.VMEM((tc, NS), F32), pltpu.VMEM((tc, NS), F32), pltpu.VMEM((2, 8, NS), F32)],
        compiler_params=ARB,
    )(dmix, proj, sre, sim, yp, z, bre, bim, cre, cim, cst, dvec.reshape(1, M), wglu)


def _ln_silu(h1, lg, lb):
    mu = jnp.mean(h1, axis=-1, keepdims=True)
    xc = h1 - mu
    rstd = lax.rsqrt(jnp.mean(xc * xc, axis=-1, keepdims=True) + EPS)
    xhat = xc * rstd
    h2 = xhat * lg + lb
    return xhat, rstd, h2, _sigmoid(h2)


def _cv_fwd(proj, vcol, gcol, wdw, bdw, lg, lb, wpw, bpw):
    T = proj.shape[0]
    M = wpw.shape[0]
    tc = _pick(T, 256, 8)
    H, K = CONV_HALO, 31

    def body(v_ref, g_ref, wdw_ref, bdw_ref, lg_ref, lb_ref, wpw_ref, bpw_ref, y_ref, h1_ref, buf_ref):
        @pl.when(pl.program_id(0) == 0)
        def _():
            buf_ref[0:H, :] = jnp.zeros((H, M), F32)

        buf_ref[H:H + tc, :] = v_ref[...] * _sigmoid(g_ref[...])
        acc = jnp.zeros((tc, M), F32) + bdw_ref[...]
        for k in range(K):
            acc = acc + wdw_ref[k:k + 1, :] * buf_ref[pl.ds(H - (K - 1) + k, tc), :]
        h1_ref[...] = acc
        _, _, h2, sg = _ln_silu(acc, lg_ref[...], lb_ref[...])
        y_ref[...] = (_dot(h2 * sg, wpw_ref[...]) + bpw_ref[...]).astype(BF16)
        buf_ref[0:H, :] = buf_ref[tc:tc + H, :]

    def rowm(c):
        return pl.BlockSpec((tc, M), lambda i: (i, c))
    vec = _full((1, M))
    return pl.pallas_call(
        body, name="cv_fwd", grid=(T // tc,),
        in_specs=[rowm(vcol), rowm(gcol), _full((32, M)), vec, vec, vec, _full((M, M)), vec],
        out_specs=[rowm(0), rowm(0)],
        out_shape=[jax.ShapeDtypeStruct((T, M), BF16), jax.ShapeDtypeStruct((T, M), F32)],
        scratch_shapes=[pltpu.VMEM((tc + H, M), F32)], compiler_params=ARB,
    )(proj, proj, wdw, bdw.reshape(1, M), lg.reshape(1, M), lb.reshape(1, M), wpw, bpw.reshape(1, M))


def _cv_bwd(dmix, dcol, proj, vcol, gcol, h1, wdw, lg, lb, wpw):
    T = proj.shape[0]
    M = wpw.shape[0]
    tc = _pick(T, 256, 8)
    nc = T // tc
    H, K = CONV_HALO, 31
    hb = tc // H

    def body(do_ref, v_ref, g_ref, vh_ref, gh_ref, h1_ref, wdw_ref, lg_ref, lb_ref, wpw_ref,
             dv_ref, dg_ref, dwpw_ref, dwdw_ref, vec_ref, hbuf_ref, dbuf_ref):
        i = pl.program_id(0)

        @pl.when(i == 0)
        def _():
            for r in (dwpw_ref, dwdw_ref, vec_ref):
                r[...] = jnp.zeros_like(r)
            dbuf_ref[tc:tc + H, :] = jnp.zeros((H, M), F32)

        v, g = v_ref[...], g_ref[...]
        sgg = _sigmoid(g)
        halo = vh_ref[...] * _sigmoid(gh_ref[...])
        hbuf_ref[0:H, :] = jnp.where(i == nc - 1, 0.0, halo)
        hbuf_ref[H:H + tc, :] = v * sgg
        do = do_ref[...]
        xhat, rstd, h2, sg = _ln_silu(h1_ref[...], lg_ref[...], lb_ref[...])
        dwpw_ref[...] += _dot_tn(h2 * sg, do)
        vec_ref[0] += _fold8(do)
        dh2 = _dot_nt(do, wpw_ref[...]) * (sg * (1.0 + h2 * (1.0 - sg)))
        vec_ref[1] += _fold8(dh2 * xhat)
        vec_ref[2] += _fold8(dh2)
        dxh = dh2 * lg_ref[...]
        dh1 = rstd * (dxh - jnp.mean(dxh, axis=-1, keepdims=True) - xhat * jnp.mean(dxh * xhat, axis=-1, keepdims=True))
        vec_ref[3] += _fold8(dh1)
        dbuf_ref[0:tc, :] = dh1
        dh0 = jnp.zeros((tc, M), F32)
        for k in range(K):
            dh0 = dh0 + wdw_ref[k:k + 1, :] * dbuf_ref[pl.ds(K - 1 - k, tc), :]
            dwdw_ref[k] += _fold8(dh1 * hbuf_ref[pl.ds(H - (K - 1) + k, tc), :])
        dv_ref[...] = (dh0 * sgg).astype(BF16)
        dg_ref[...] = (dh0 * v * sgg * (1.0 - sgg)).astype(BF16)
        dbuf_ref[tc:tc + H, :] = dbuf_ref[0:H, :]

    def rowm(c):
        return pl.BlockSpec((tc, M), lambda i: (nc - 1 - i, c))

    def halo(c):
        return pl.BlockSpec((H, M), lambda i: (jnp.maximum((nc - 1 - i) * hb - 1, 0), c))
    vec = _full((1, M))
    return pl.pallas_call(
        body, name="cv_bwd", grid=(nc,),
        in_specs=[rowm(dcol), rowm(vcol), rowm(gcol), halo(vcol), halo(gcol), rowm(0), _full((32, M)), vec, vec, _full((M, M))],
        out_specs=[rowm(0), rowm(0), _full((M, M)), _full((32, 8, M)), _full((5, 8, M))],
        out_shape=[jax.ShapeDtypeStruct((T, M), BF16), jax.ShapeDtypeStruct((T, M), BF16), jax.ShapeDtypeStruct((M, M), F32),
                   jax.ShapeDtypeStruct((32, 8, M), F32), jax.ShapeDtypeStruct((5, 8, M), F32)],
        scratch_shapes=[pltpu.VMEM((tc + H, M), F32), pltpu.VMEM((tc + H, M), F32)], compiler_params=ARB,
    )(dmix, proj, proj, proj, proj, h1, wdw, lg.reshape(1, M), lb.reshape(1, M), wpw)


def _lru_gates(xc, r, ig, sp8):
    la = -r * sp8
    a = jnp.exp(la)
    mult = jnp.sqrt(_neg_expm1(2.0 * la))
    return a, mult, mult * (ig * xc)


def _lru_fwd(proj, xcol, gcol, wcv, bcv, wr, br, wi, bi, sp8):
    T = proj.shape[0]
    M = wr.shape[0]
    tc = _pick(T, 256, 8)
    H, K = SMALL_HALO, 4

    def body(x_ref, g_ref, wcv_ref, bcv_ref, wr_ref, br_ref, wi_ref, bi_ref, sp_ref,
             y_ref, xc_ref, r_ref, i_ref, hp_ref, buf_ref, a_ref, car_ref):
        @pl.when(pl.program_id(0) == 0)
        def _():
            buf_ref[0:H, :] = jnp.zeros((H, M), F32)
            car_ref[...] = jnp.zeros_like(car_ref)

        buf_ref[H:H + tc, :] = x_ref[...]
        xc = jnp.zeros((tc, M), F32) + bcv_ref[...]
        for k in range(K):
            xc = xc + wcv_ref[k:k + 1, :] * buf_ref[pl.ds(H - (K - 1) + k, tc), :]
        buf_ref[0:H, :] = buf_ref[tc:tc + H, :]
        r = _sigmoid(_dot(xc, wr_ref[...]) + br_ref[...])
        ig = _sigmoid(_dot(xc, wi_ref[...]) + bi_ref[...])
        xc_ref[...] = xc
        r_ref[...] = r
        i_ref[...] = ig
        a, _, bt = _lru_gates(xc, r, ig, sp_ref[...])
        a_ref[...] = a
        hp_ref[...] = bt
        row = lax.broadcasted_iota(jnp.int32, (8, M), 0)

        def blk(ib, _):
            r0 = pl.multiple_of(ib * 8, 8)
            av, bv = a_ref[pl.ds(r0, 8), :], hp_ref[pl.ds(r0, 8), :]
            for d in (1, 2, 4):
                m = row >= d
                bv = jnp.where(m, bv + av * pltpu.roll(bv, d, 0), bv)
                av = jnp.where(m, av * pltpu.roll(av, d, 0), av)
            cr = car_ref[7:8, :]
            h = bv + av * cr
            a_ref[pl.ds(r0, 8), :] = h
            hp_ref[pl.ds(r0, 8), :] = jnp.where(row == 0, cr, pltpu.roll(h, 1, 0))
            car_ref[...] = h
            return 0

        lax.fori_loop(0, tc // 8, blk, 0)
        y_ref[...] = (a_ref[...] * _gelu(g_ref[...])).astype(BF16)

    def rowm(c):
        return pl.BlockSpec((tc, M), lambda i: (i, c))
    vec = _full((1, M))
    return pl.pallas_call(
        body, name="lru_fwd", grid=(T // tc,),
        in_specs=[rowm(xcol), rowm(gcol), _full((8, M)), vec, _full((M, M)), vec, _full((M, M)), vec, vec],
        out_specs=[rowm(0)] * 5,
        out_shape=[jax.ShapeDtypeStruct((T, M), BF16)] + [jax.ShapeDtypeStruct((T, M), F32)] * 4,
        scratch_shapes=[pltpu.VMEM((tc + H, M), F32), pltpu.VMEM((tc, M), F32), pltpu.VMEM((8, M), F32)], compiler_params=ARB,
    )(proj, proj, wcv, bcv.reshape(1, M), wr, br.reshape(1, M), wi, bi.reshape(1, M), sp8.reshape(1, M))


def _lru_bwd(dmix, dcol, proj, xcol, gcol, xc, r, ig, hprev, wcv, wr, wi, sp8):
    T = proj.shape[0]
    M = wr.shape[0]
    tc = _pick(T, 256, 8)
    nc = T // tc
    H, K = SMALL_HALO, 4
    hb = tc // H

    def body(do_ref, x_ref, g_ref, xh_ref, xc_ref, r_ref, i_ref, hp_ref, wcv_ref, wr_ref, wi_ref, sp_ref,
             dx_ref, dg_ref, dwr_ref, dwi_ref, dwcv_ref, vec_ref, xbuf_ref, dbuf_ref, a_ref, gs_ref, car_ref):
        i = pl.program_id(0)

        @pl.when(i == 0)
        def _():
            for rf in (dwr_ref, dwi_ref, dwcv_ref, vec_ref, car_ref):
                rf[...] = jnp.zeros_like(rf)
            dbuf_ref[tc:tc + H, :] = jnp.zeros((H, M), F32)

        xcv, rv, iv, hpv, gv, do = xc_ref[...], r_ref[...], i_ref[...], hp_ref[...], g_ref[...], do_ref[...]
        sp = sp_ref[...]
        a, mult, bt = _lru_gates(xcv, rv, iv, sp)
        h = a * hpv + bt
        ge = _gelu(gv)
        dg_ref[...] = (do * h * _gelu_grad(gv)).astype(BF16)
        a_ref[...] = a
        gs_ref[...] = do * ge
        row = lax.broadcasted_iota(jnp.int32, (8, M), 0)

        def blk(n, _):
            r0 = pl.multiple_of((tc // 8 - 1 - n) * 8, 8)
            ablk = a_ref[pl.ds(r0, 8), :]
            av = jnp.where(row == 7, car_ref[0, 0:1, :], pltpu.roll(ablk, 7, 0))
            bv = gs_ref[pl.ds(r0, 8), :]
            for d in (1, 2, 4):
                m = row <= 7 - d
                bv = jnp.where(m, bv + av * pltpu.roll(bv, 8 - d, 0), bv)
                av = jnp.where(m, av * pltpu.roll(av, 8 - d, 0), av)
            gsv = bv + av * car_ref[1, 0:1, :]
            gs_ref[pl.ds(r0, 8), :] = gsv
            car_ref[0] = ablk
            car_ref[1] = gsv
            return 0

        lax.fori_loop(0, tc // 8, blk, 0)
        gs = gs_ref[...]
        dixc = gs * mult
        dla = gs * hpv * a - gs * (iv * xcv) * (a * a) / mult
        vec_ref[2] += _fold8(-dla * rv)
        dpr = (-dla * sp) * rv * (1.0 - rv)
        dpi = (dixc * xcv) * iv * (1.0 - iv)
        vec_ref[0] += _fold8(dpr)
        vec_ref[1] += _fold8(dpi)
        dwr_ref[...] += _dot_tn(xcv, dpr)
        dwi_ref[...] += _dot_tn(xcv, dpi)
        dxc = dixc * iv + _dot_nt(dpr, wr_ref[...]) + _dot_nt(dpi, wi_ref[...])
        vec_ref[3] += _fold8(dxc)
        xbuf_ref[0:H, :] = jnp.where(i == nc - 1, 0.0, xh_ref[...])
        xbuf_ref[H:H + tc, :] = x_ref[...]
        dbuf_ref[0:tc, :] = dxc
        dx = jnp.zeros((tc, M), F32)
        for k in range(K):
            dx = dx + wcv_ref[k:k + 1, :] * dbuf_ref[pl.ds(K - 1 - k, tc), :]
            dwcv_ref[k] += _fold8(dxc * xbuf_ref[pl.ds(H - (K - 1) + k, tc), :])
        dx_ref[...] = dx.astype(BF16)
        dbuf_ref[tc:tc + H, :] = dbuf_ref[0:H, :]

    def rowm(c):
        return pl.BlockSpec((tc, M), lambda i: (nc - 1 - i, c))
    halo = pl.BlockSpec((H, M), lambda i: (jnp.maximum((nc - 1 - i) * hb - 1, 0), xcol))
    vec = _full((1, M))
    return pl.pallas_call(
        body, name="lru_bwd", grid=(nc,),
        in_specs=[rowm(dcol), rowm(xcol), rowm(gcol), halo, rowm(0), rowm(0), rowm(0), rowm(0), _full((8, M)),
                  _full((M, M)), _full((M, M)), vec],
        out_specs=[rowm(0), rowm(0), _full((M, M)), _full((M, M)), _full((8, 8, M)), _full((4, 8, M))],
        out_shape=[jax.ShapeDtypeStruct((T, M), BF16), jax.ShapeDtypeStruct((T, M), BF16), jax.ShapeDtypeStruct((M, M), F32),
                   jax.ShapeDtypeStruct((M, M), F32), jax.ShapeDtypeStruct((8, 8, M), F32), jax.ShapeDtypeStruct((4, 8, M), F32)],
        scratch_shapes=[pltpu.VMEM((tc + H, M), F32), pltpu.VMEM((tc + H, M), F32), pltpu.VMEM((tc, M), F32),
                        pltpu.VMEM((tc, M), F32), pltpu.VMEM((2, 8, M), F32)],
        compiler_params=ARB,
    )(dmix, proj, proj, proj, xc, r, ig, hprev, wcv, wr, wi, sp8.reshape(1, M))


def _pool_diffs(buf_ref, x, t0, tc, gw):
    H = POOL_HALO
    pos = (lax.broadcasted_iota(jnp.int32, (tc, gw), 0) + t0 + 1).astype(F32)
    out, inv = [], []
    for gi, win in enumerate(POOL_WINDOWS):
        sl = slice(gi * gw, (gi + 1) * gw)
        s = x[:, sl]
        for j in range(1, win):
            s = s + buf_ref[pl.ds(H - j, tc), sl]
        ic = 1.0 / jnp.minimum(pos, float(win))
        out.append(s * ic - x[:, sl])
        inv.append(ic)
    return out, inv


def _pool_fwd(proj, col, pw, scale):
    T = proj.shape[0]
    ng, gw, _ = pw.shape
    M = ng * gw
    tc = _pick(T, 256, 8)
    H = POOL_HALO

    def body(x_ref, pw_ref, sc_ref, y_ref, buf_ref):
        @pl.when(pl.program_id(0) == 0)
        def _():
            buf_ref[0:H, :] = jnp.zeros((H, M), F32)

        x = x_ref[...]
        buf_ref[H:H + tc, :] = x
        diffs, _ = _pool_diffs(buf_ref, x, pl.program_id(0) * tc, tc, gw)
        for gi in range(ng):
            sl = slice(gi * gw, (gi + 1) * gw)
            y_ref[:, sl] = (_dot(diffs[gi], pw_ref[gi]) * sc_ref[:, sl]).astype(BF16)
        buf_ref[0:H, :] = buf_ref[tc:tc + H, :]

    return pl.pallas_call(
        body, name="pool_fwd", grid=(T // tc,),
        in_specs=[pl.BlockSpec((tc, M), lambda i: (i, col)), _full(pw.shape), _full((1, M))],
        out_specs=pl.BlockSpec((tc, M), lambda i: (i, 0)), out_shape=jax.ShapeDtypeStruct((T, M), BF16),
        scratch_shapes=[pltpu.VMEM((tc + H, M), F32)], compiler_params=ARB,
    )(proj, pw, scale.reshape(1, M))


def _pool_bwd(dmix, dcol, proj, col, pw, scale):
    T = proj.shape[0]
    ng, gw, _ = pw.shape
    M = ng * gw
    tc = _pick(T, 256, 8)
    nc = T // tc
    H = POOL_HALO
    hb = tc // H

    def body(do_ref, x_ref, xh_ref, pw_ref, sc_ref, dx_ref, dpw_ref, dsc_ref, buf_ref, ebuf_ref):
        i = pl.program_id(0)

        @pl.when(i == 0)
        def _():
            dpw_ref[...] = jnp.zeros_like(dpw_ref)
            dsc_ref[...] = jnp.zeros_like(dsc_ref)
            ebuf_ref[tc:tc + H, :] = jnp.zeros((H, M), F32)

        x, do = x_ref[...], do_ref[...]
        buf_ref[0:H, :] = jnp.where(i == nc - 1, 0.0, xh_ref[...])
        buf_ref[H:H + tc, :] = x
        diffs, inv = _pool_diffs(buf_ref, x, (nc - 1 - i) * tc, tc, gw)
        ddg = []
        for gi in range(ng):
            sl = slice(gi * gw, (gi + 1) * gw)
            dyy = do[:, sl] * sc_ref[:, sl]
            dsc_ref[:, sl] += _fold8(do[:, sl] * _dot(diffs[gi], pw_ref[gi]))
            dpw_ref[gi] += _dot_tn(diffs[gi], dyy)
            d = _dot_nt(dyy, pw_ref[gi])
            ddg.append(d)
            ebuf_ref[0:tc, sl] = d * inv[gi]
        for gi, win in enumerate(POOL_WINDOWS):
            sl = slice(gi * gw, (gi + 1) * gw)
            s = -ddg[gi]
            for j in range(win):
                s = s + ebuf_ref[pl.ds(j, tc), sl]
            dx_ref[:, sl] = s.astype(BF16)
        ebuf_ref[tc:tc + H, :] = ebuf_ref[0:H, :]

    return pl.pallas_call(
        body, name="pool_bwd", grid=(nc,),
        in_specs=[pl.BlockSpec((tc, M), lambda i: (nc - 1 - i, dcol)), pl.BlockSpec((tc, M), lambda i: (nc - 1 - i, col)),
                  pl.BlockSpec((H, M), lambda i: (jnp.maximum((nc - 1 - i) * hb - 1, 0), col)), _full(pw.shape), _full((1, M))],
        out_specs=[pl.BlockSpec((tc, M), lambda i: (nc - 1 - i, 0)), _full(pw.shape), _full((8, M))],
        out_shape=[jax.ShapeDtypeStruct((T, M), BF16), jax.ShapeDtypeStruct(pw.shape, F32), jax.ShapeDtypeStruct((8, M), F32)],
        scratch_shapes=[pltpu.VMEM((tc + H, M), F32), pltpu.VMEM((tc + H, M), F32)], compiler_params=ARB,
    )(dmix, proj, proj, pw, scale.reshape(1, M))


def _ffn_mid_fwd(up, wdw, bdw):
    T, F2 = up.shape
    F = F2 // 2
    tc = _pick(T, 256, 8)
    H, K = SMALL_HALO, 3

    def body(gt_ref, val_ref, w_ref, b_ref, act_ref, buf_ref):
        @pl.when(pl.program_id(0) == 0)
        def _():
            buf_ref[0:H, :] = jnp.zeros((H, F), F32)

        buf_ref[H:H + tc, :] = gt_ref[...]
        gc = jnp.zeros((tc, F), F32) + b_ref[...]
        for k in range(K):
            gc = gc + w_ref[k:k + 1, :] * buf_ref[pl.ds(H - (K - 1) + k, tc), :]
        act_ref[...] = (_gelu(gc) * val_ref[...]).astype(BF16)
        buf_ref[0:H, :] = buf_ref[tc:tc + H, :]

    return pl.pallas_call(
        body, name="ffn_mid_fwd", grid=(T // tc,),
        in_specs=[pl.BlockSpec((tc, F), lambda i: (i, 0)), pl.BlockSpec((tc, F), lambda i: (i, 1)), _full((8, F)), _full((1, F))],
        out_specs=pl.BlockSpec((tc, F), lambda i: (i, 0)), out_shape=jax.ShapeDtypeStruct((T, F), BF16),
        scratch_shapes=[pltpu.VMEM((tc + H, F), F32)], compiler_params=ARB,
    )(up, up, wdw, bdw.reshape(1, F))


def _ffn_mid_bwd(dact, up, wdw, bdw):
    T, F2 = up.shape
    F = F2 // 2
    tc = _pick(T, 128, 8)
    nc = T // tc
    H, K = SMALL_HALO, 3
    hb = tc // H

    def body(da_ref, gt_ref, val_ref, gh_ref, w_ref, b_ref, dup_ref, dw_ref, db_ref, gbuf_ref, dbuf_ref):
        i = pl.program_id(0)

        @pl.when(i == 0)
        def _():
            dw_ref[...] = jnp.zeros_like(dw_ref)
            db_ref[...] = jnp.zeros_like(db_ref)
            dbuf_ref[tc:tc + H, :] = jnp.zeros((H, F), F32)

        gbuf_ref[0:H, :] = jnp.where(i == nc - 1, 0.0, gh_ref[...])
        gbuf_ref[H:H + tc, :] = gt_ref[...]
        gc = jnp.zeros((tc, F), F32) + b_ref[...]
        for k in range(K):
            gc = gc + w_ref[k:k + 1, :] * gbuf_ref[pl.ds(H - (K - 1) + k, tc), :]
        da = da_ref[...]
        dup_ref[:, F:2 * F] = (da * _gelu(gc)).astype(BF16)
        dgc = da * val_ref[...] * _gelu_grad(gc)
        db_ref[...] += _fold8(dgc)
        dbuf_ref[0:tc, :] = dgc
        dgt = jnp.zeros((tc, F), F32)
        for k in range(K):
            dgt = dgt + w_ref[k:k + 1, :] * dbuf_ref[pl.ds(K - 1 - k, tc), :]
            dw_ref[k] += _fold8(dgc * gbuf_ref[pl.ds(H - (K - 1) + k, tc), :])
        dup_ref[:, 0:F] = dgt.astype(BF16)
        dbuf_ref[tc:tc + H, :] = dbuf_ref[0:H, :]

    return pl.pallas_call(
        body, name="ffn_mid_bwd", grid=(nc,),
        in_specs=[pl.BlockSpec((tc, F), lambda i: (nc - 1 - i, 0)), pl.BlockSpec((tc, F), lambda i: (nc - 1 - i, 0)),
                  pl.BlockSpec((tc, F), lambda i: (nc - 1 - i, 1)),
                  pl.BlockSpec((H, F), lambda i: (jnp.maximum((nc - 1 - i) * hb - 1, 0), 0)), _full((8, F)), _full((1, F))],
        out_specs=[pl.BlockSpec((tc, F2), lambda i: (nc - 1 - i, 0)), _full((8, 8, F)), _full((8, F))],
        out_shape=[jax.ShapeDtypeStruct((T, F2), BF16), jax.ShapeDtypeStruct((8, 8, F), F32), jax.ShapeDtypeStruct((8, F), F32)],
        scratch_shapes=[pltpu.VMEM((tc + H, F), F32), pltpu.VMEM((tc + H, F), F32)], compiler_params=ARB,
    )(dact, up, up, up, wdw, bdw.reshape(1, F))


def _my_pos():
    return lax.axis_index("x"), lax.axis_index("y"), lax.axis_index("c")


def _flip(pos, k):
    x, y, c = pos
    return ((1 - x) if k & 4 else x, (1 - y) if k & 2 else y, (1 - c) if k & 1 else c)


def _rank(pos):
    return 4 * pos[0] + 2 * pos[1] + pos[2]


def _exchange(xs, name, gather):
    n = len(xs)
    shapes = [((N_DEV,) + tuple(x.shape)) if gather else tuple(x.shape) for x in xs]

    def body(*refs):
        ins, outs = refs[:n], refs[n:2 * n]
        send_sems, recv_sems, local_sems = refs[2 * n:]
        me = _my_pos()
        mine = _rank(me)
        local, sends = [], []
        for a in range(n):
            loc = pltpu.make_async_copy(ins[a] if gather else ins[a].at[mine], outs[a].at[mine], local_sems.at[a])
            loc.start()
            local.append(loc)
            for k in range(1, N_DEV):
                peer = _flip(me, k)
                cp = pltpu.make_async_remote_copy(
                    src_ref=ins[a] if gather else ins[a].at[_rank(peer)], dst_ref=outs[a].at[mine],
                    send_sem=send_sems.at[a, k - 1], recv_sem=recv_sems.at[a, k - 1], device_id=peer,
                    device_id_type=pl.DeviceIdType.MESH)
                cp.start()
                sends.append(cp)
        for a in range(n):
            for k in range(1, N_DEV):
                peer = _flip(me, k)
                pltpu.make_async_remote_copy(
                    src_ref=ins[a] if gather else ins[a].at[_rank(peer)], dst_ref=outs[a].at[_rank(peer)],
                    send_sem=send_sems.at[a, k - 1], recv_sem=recv_sems.at[a, k - 1], device_id=peer,
                    device_id_type=pl.DeviceIdType.MESH).wait_recv()
        for cp in sends:
            cp.wait_send()
        for loc in local:
            loc.wait()

    any_spec = pl.BlockSpec(memory_space=pl.ANY)
    return pl.pallas_call(
        body, name=name, in_specs=[any_spec] * n, out_specs=[any_spec] * n,
        out_shape=[jax.ShapeDtypeStruct(s, x.dtype) for s, x in zip(shapes, xs)],
        scratch_shapes=[pltpu.SemaphoreType.DMA((n, N_DEV - 1)), pltpu.SemaphoreType.DMA((n, N_DEV - 1)),
                        pltpu.SemaphoreType.DMA((n,))],
    )(*xs)


def _adamw_math(w, g, m, v):
    m = ADAM_B1 * m + (1.0 - ADAM_B1) * g
    v = ADAM_B2 * v + (1.0 - ADAM_B2) * (g * g)
    m_hat = m / (1.0 - ADAM_B1 ** ADAM_STEP)
    v_hat = v / (1.0 - ADAM_B2 ** ADAM_STEP)
    return -ADAM_LR * (m_hat / (jnp.sqrt(v_hat) + ADAM_EPS) + ADAM_WD * w), m, v


def _sum8(parts, *, name):
    _, R, C = parts.shape
    tr = _pick(R, 256, 16)

    def body(p_ref, o_ref):
        acc = p_ref[0].astype(F32)
        for r in range(1, N_DEV):
            acc = acc + p_ref[r].astype(F32)
        o_ref[...] = acc

    return pl.pallas_call(
        body, name=name, grid=(R // tr,), in_specs=[pl.BlockSpec((N_DEV, tr, C), lambda i: (0, i, 0))],
        out_specs=pl.BlockSpec((tr, C), lambda i: (i, 0)), out_shape=jax.ShapeDtypeStruct((R, C), F32),
        compiler_params=pltpu.CompilerParams(dimension_semantics=("parallel",)),
    )(parts)


def _adamw(w, g, m, v, *, name):
    R, C = w.shape
    tr = _pick(R, 256, 8)

    def body(w_ref, g_ref, m_ref, v_ref, d_ref, nm_ref, nv_ref):
        d, nm, nv = _adamw_math(w_ref[...], g_ref[...], m_ref[...], v_ref[...])
        d_ref[...] = d
        nm_ref[...] = nm
        nv_ref[...] = nv

    row = pl.BlockSpec((tr, C), lambda i: (i, 0))
    return pl.pallas_call(
        body, name=name, grid=(R // tr,), in_specs=[row] * 4, out_specs=[row] * 3,
        out_shape=[jax.ShapeDtypeStruct((R, C), F32)] * 3,
        compiler_params=pltpu.CompilerParams(dimension_semantics=("parallel",)),
    )(w, g, m, v)


def _sum8_adamw(parts, w, m, v, *, name):
    _, R, C = parts.shape
    tr = _pick(R, 256, 16)

    def body(p_ref, w_ref, m_ref, v_ref, g_ref, d_ref, nm_ref, nv_ref):
        g = p_ref[0].astype(F32)
        for r in range(1, N_DEV):
            g = g + p_ref[r].astype(F32)
        d, nm, nv = _adamw_math(w_ref[...], g, m_ref[...], v_ref[...])
        g_ref[...] = g
        d_ref[...] = d
        nm_ref[...] = nm
        nv_ref[...] = nv

    row = pl.BlockSpec((tr, C), lambda i: (i, 0))
    return pl.pallas_call(
        body, name=name, grid=(R // tr,), in_specs=[pl.BlockSpec((N_DEV, tr, C), lambda i: (0, i, 0)), row, row, row],
        out_specs=[row] * 4, out_shape=[jax.ShapeDtypeStruct((R, C), F32)] * 4,
        compiler_params=pltpu.CompilerParams(dimension_semantics=("parallel",)),
    )(parts, w, m, v)


def _pack(arrs):
    flat = jnp.concatenate([a.reshape(-1).astype(F32) for a in arrs])
    pad = (-flat.shape[0]) % 1024
    return jnp.pad(flat, (0, pad)).reshape(-1, 128)


def _unpack(buf, shapes):
    flat, out, off = buf.reshape(-1), [], 0
    for s in shapes:
        n = math.prod(s)
        out.append(flat[off:off + n].reshape(s))
        off += n
    return out


def _s5_params(lam_re, lam_im, log_step, b_re, b_im):
    G, P, H = b_re.shape
    step = jnp.exp(log_step)[:, None]
    mag = jnp.exp(lam_re * step)
    ar, ai = mag * jnp.cos(lam_im * step), mag * jnp.sin(lam_im * step)
    den = lam_re * lam_re + lam_im * lam_im
    qr = ((ar - 1.0) * lam_re + ai * lam_im) / den
    qi = (ai * lam_re - (ar - 1.0) * lam_im) / den
    bbr = qr[..., None] * b_re - qi[..., None] * b_im
    bbi = qr[..., None] * b_im + qi[..., None] * b_re
    return ar.reshape(-1), ai.reshape(-1), _s5_blocks_in(bbr), _s5_blocks_in(bbi)


def _s5_blocks_in(b):
    G, P, H = b.shape
    eye = jnp.eye(8, dtype=b.dtype)
    return jnp.einsum("kgph,gj->kghjp", b.reshape(G // 8, 8, P, H), eye).reshape(G // 8, 8 * H, 8 * P)


def _s5_blocks_in_diag(m, G, P, H):
    return jnp.einsum("kghgp->kgph", m.reshape(G // 8, 8, H, 8, P)).reshape(G, P, H)


def _s5_blocks_out(c):
    G, H, P = c.shape
    eye = jnp.eye(8, dtype=c.dtype)
    return jnp.einsum("kghp,gj->kgpjh", c.reshape(G // 8, 8, H, P), eye).reshape(G // 8, 8 * P, 8 * H)


def _s5_blocks_out_diag(m, G, H, P):
    return jnp.einsum("kgpgh->kghp", m.reshape(G // 8, 8, P, 8, H)).reshape(G, H, P)


def _head_blocks(w):
    nh, d, e = w.shape
    return jnp.einsum("hde,hk->hdke", w, jnp.eye(nh, dtype=w.dtype)).reshape(nh * d, nh * e)


def _head_blocks_diag(m, nh, d, e):
    return jnp.einsum("hdhe->hde", m.reshape(nh, d, nh, e))


def _pad_rows(w, rows):
    return jnp.pad(w, ((0, rows - w.shape[0]), (0, 0)))


def _unshard_cols(g):
    return jnp.transpose(g, (1, 0, 2)).reshape(g.shape[1], -1)


_NAMES = ['norm_mix_g', 'w_in', 's5_lam_re', 's5_lam_im', 's5_log_step', 's5_b_re', 's5_b_im', 's5_c_re', 's5_c_im', 's5_d',
          's5_w_glu', 's5_b_glu', 'cv_w_dw', 'cv_b_dw', 'cv_ln_g', 'cv_ln_b', 'cv_w_pw', 'cv_b_pw', 'lru_w_conv', 'lru_b_conv',
          'lru_w_r', 'lru_b_r', 'lru_w_i', 'lru_b_i', 'lru_lam', 'pool_w', 'pool_scale', 'w_out', 'norm_ffn_g', 'ffn_w_up',
          'ffn_w_dw', 'ffn_b_dw', 'ffn_w_down', 'norm_final_g']
_MATRIX = ('w_in', 'w_out', 'ffn_w_up', 'ffn_w_down', 's5_w_glu', 'cv_w_pw')
_TRANSPOSED = ('w_in', 'ffn_w_up')
_COLSHARD = ('cv_w_dw', 'lru_w_conv', 'ffn_w_dw')


def kernel(x, norm_mix_g, w_in, s5_lam_re, s5_lam_im, s5_log_step, s5_b_re, s5_b_im, s5_c_re, s5_c_im, s5_d, s5_w_glu, s5_b_glu, cv_w_dw, cv_b_dw, cv_ln_g, cv_ln_b, cv_w_pw, cv_b_pw, lru_w_conv, lru_b_conv, lru_w_r, lru_b_r, lru_w_i, lru_b_i, lru_lam, pool_w, pool_scale, w_out, norm_ffn_g, ffn_w_up, ffn_w_dw, ffn_b_dw, ffn_w_down, norm_final_g, loss_target, m_norm_mix_g, m_w_in, m_s5_lam_re, m_s5_lam_im, m_s5_log_step, m_s5_b_re, m_s5_b_im, m_s5_c_re, m_s5_c_im, m_s5_d, m_s5_w_glu, m_s5_b_glu, m_cv_w_dw, m_cv_b_dw, m_cv_ln_g, m_cv_ln_b, m_cv_w_pw, m_cv_b_pw, m_lru_w_conv, m_lru_b_conv, m_lru_w_r, m_lru_b_r, m_lru_w_i, m_lru_b_i, m_lru_lam, m_pool_w, m_pool_scale, m_w_out, m_norm_ffn_g, m_ffn_w_up, m_ffn_w_dw, m_ffn_b_dw, m_ffn_w_down, m_norm_final_g, v_norm_mix_g, v_w_in, v_s5_lam_re, v_s5_lam_im, v_s5_log_step, v_s5_b_re, v_s5_b_im, v_s5_c_re, v_s5_c_im, v_s5_d, v_s5_w_glu, v_s5_b_glu, v_cv_w_dw, v_cv_b_dw, v_cv_ln_g, v_cv_ln_b, v_cv_w_pw, v_cv_b_pw, v_lru_w_conv, v_lru_b_conv, v_lru_w_r, v_lru_b_r, v_lru_w_i, v_lru_b_i, v_lru_lam, v_pool_w, v_pool_scale, v_w_out, v_norm_ffn_g, v_ffn_w_up, v_ffn_w_dw, v_ffn_b_dw, v_ffn_w_down, v_norm_final_g):
    args = (x, norm_mix_g, w_in, s5_lam_re, s5_lam_im, s5_log_step, s5_b_re, s5_b_im, s5_c_re, s5_c_im, s5_d, s5_w_glu, s5_b_glu, cv_w_dw, cv_b_dw, cv_ln_g, cv_ln_b, cv_w_pw, cv_b_pw, lru_w_conv, lru_b_conv, lru_w_r, lru_b_r, lru_w_i, lru_b_i, lru_lam, pool_w, pool_scale, w_out, norm_ffn_g, ffn_w_up, ffn_w_dw, ffn_b_dw, ffn_w_down, norm_final_g, loss_target, m_norm_mix_g, m_w_in, m_s5_lam_re, m_s5_lam_im, m_s5_log_step, m_s5_b_re, m_s5_b_im, m_s5_c_re, m_s5_c_im, m_s5_d, m_s5_w_glu, m_s5_b_glu, m_cv_w_dw, m_cv_b_dw, m_cv_ln_g, m_cv_ln_b, m_cv_w_pw, m_cv_b_pw, m_lru_w_conv, m_lru_b_conv, m_lru_w_r, m_lru_b_r, m_lru_w_i, m_lru_b_i, m_lru_lam, m_pool_w, m_pool_scale, m_w_out, m_norm_ffn_g, m_ffn_w_up, m_ffn_w_dw, m_ffn_b_dw, m_ffn_w_down, m_norm_final_g, v_norm_mix_g, v_w_in, v_s5_lam_re, v_s5_lam_im, v_s5_log_step, v_s5_b_re, v_s5_b_im, v_s5_c_re, v_s5_c_im, v_s5_d, v_s5_w_glu, v_s5_b_glu, v_cv_w_dw, v_cv_b_dw, v_cv_ln_g, v_cv_ln_b, v_cv_w_pw, v_cv_b_pw, v_lru_w_conv, v_lru_b_conv, v_lru_w_r, v_lru_b_r, v_lru_w_i, v_lru_b_i, v_lru_lam, v_pool_w, v_pool_scale, v_w_out, v_norm_ffn_g, v_ffn_w_up, v_ffn_w_dw, v_ffn_b_dw, v_ffn_w_down, v_norm_final_g)
    x, target = args[0], args[35]
    W = dict(zip(_NAMES, args[1:35]))
    Mo = dict(zip(_NAMES, args[36:70]))
    Vo = dict(zip(_NAMES, args[70:104]))
    x = x[0]
    target = target[0]
    T, D = x.shape
    depth = W['w_in'].shape[0]
    G, P, H = W['s5_b_re'].shape[1:]
    MW = G * H
    nh, hd, _ = W['lru_w_r'].shape[1:]
    me = _rank(_my_pos())

    shards = []
    for l in range(depth):
        for n in _MATRIX:
            w = W[n][l]
            shards.append((w.T if n in _TRANSPOSED else w).astype(BF16))
    gathered = _exchange(shards, "gather_matrices", True)
    full = {}
    for l in range(depth):
        for j, n in enumerate(_MATRIX):
            g8 = gathered[l * len(_MATRIX) + j]
            full[n, l] = g8.reshape(-1, g8.shape[-1])
    tap_shapes = [W[n][l].shape for l in range(depth) for n in _COLSHARD]
    taps8 = _exchange([_pack([W[n][l] for l in range(depth) for n in _COLSHARD])], "gather_taps", True)[0]
    taps = [_unshard_cols(jnp.stack(t)) for t in zip(*[_unpack(taps8[r], tap_shapes) for r in range(N_DEV)])]
    for l in range(depth):
        for j, n in enumerate(_COLSHARD):
            full[n, l] = taps[l * len(_COLSHARD) + j]

    saved = []
    for l in range(depth):
        s5p, s5p_vjp = jax.vjp(_s5_params, W['s5_lam_re'][l], W['s5_lam_im'][l], W['s5_log_step'][l], W['s5_b_re'][l], W['s5_b_im'][l])
        ar, ai, bre, bim = s5p
        cre, cim = _s5_blocks_out(W['s5_c_re'][l]), _s5_blocks_out(W['s5_c_im'][l])
        sp8, sp8_vjp = jax.vjp(lambda lam: LRU_C * jax.nn.softplus(-lam), W['lru_lam'][l])
        wr, wi = _head_blocks(W['lru_w_r'][l]), _head_blocks(W['lru_w_i'][l])
        c = dict(bre=bre.astype(BF16), bim=bim.astype(BF16), cre=cre.astype(BF16), cim=cim.astype(BF16),
                 cst_f=_s5_scan_consts(ar, ai, False), cst_b=_s5_scan_consts(ar, ai, True), sp8=sp8, wr=wr.astype(BF16),
                 wi=wi.astype(BF16), cvw=_pad_rows(full['cv_w_dw', l], 32), lruw=_pad_rows(full['lru_w_conv', l], 8),
                 ffw=_pad_rows(full['ffn_w_dw', l], 8), pw=W['pool_w'][l].astype(BF16), s5p_vjp=s5p_vjp, sp8_vjp=sp8_vjp, x0=x)
        proj, h = _norm_matmul(x, W['norm_mix_g'][l], full['w_in', l], tm=1024, tn=512, name="in_proj")
        y0, sre, sim, yp, z = _s5_fwd(proj, 0, c['bre'], c['bim'], c['cre'], c['cim'], c['cst_f'], W['s5_d'][l],
                                     full['s5_w_glu', l], W['s5_b_glu'][l])
        y1, h1 = _cv_fwd(proj, 1, 2, c['cvw'], W['cv_b_dw'][l], W['cv_ln_g'][l], W['cv_ln_b'][l], full['cv_w_pw', l], W['cv_b_pw'][l])
        y2, xc, rr, ig, hprev = _lru_fwd(proj, 3, 4, c['lruw'], W['lru_b_conv'][l], c['wr'], W['lru_b_r'][l], c['wi'],
                                         W['lru_b_i'][l], sp8)
        y3 = _pool_fwd(proj, 5, c['pw'], W['pool_scale'][l])
        mixed = jnp.concatenate([y0, y1, y2, y3], axis=1)
        x1 = _matmul(mixed, full['w_out', l], mode="nn", tm=1024, tn=512, tk=2048, out_dtype=F32, name="out_proj", add=x)
        up, h2 = _norm_matmul(x1, W['norm_ffn_g'][l], full['ffn_w_up', l], tm=1024, tn=256, name="ffn_up")
        act = _ffn_mid_fwd(up, c['ffw'], W['ffn_b_dw'][l])
        F = act.shape[1]
        x = _matmul(act, full['ffn_w_down', l], mode="nn", tm=512, tn=512, tk=F, out_dtype=F32, name="ffn_down", add=x1)
        c.update(proj=proj, h=h, sre=sre, sim=sim, yp=yp, z=z, h1=h1, xc=xc, rr=rr, ig=ig, hprev=hprev, mixed=mixed, x1=x1,
                 up=up, h2=h2, act=act)
        saved.append(c)

    dx, dxb, dgf, se = _loss_head(x, W['norm_final_g'], target)
    loss = lax.psum(0.5 / D * jnp.sum(se), AXES)
    gsmall = {('norm_final_g', None): dgf.sum(0)}
    gmat = {}
    for l in reversed(range(depth)):
        c = saved[l]
        F = c['act'].shape[1]
        dact = _matmul(dxb, full['ffn_w_down', l], mode="nt", tm=256, tn=F, tk=512, out_dtype=F32, name="d_act")
        gmat['ffn_w_down', l] = _matmul(c['act'], dxb, mode="tn", tm=F, tn=512, tk=256, out_dtype=BF16, name="dw_down")
        dup, dffw, dffb = _ffn_mid_bwd(dact, c['up'], c['ffw'], W['ffn_b_dw'][l])
        gsmall['ffn_w_dw', l] = dffw.sum(1)[:3]
        gsmall['ffn_b_dw', l] = dffb.sum(0)
        dh2 = _matmul(dup, full['ffn_w_up', l], mode="nn", tm=512, tn=512, tk=F, out_dtype=F32, name="d_h2")
        gmat['ffn_w_up', l] = _matmul(dup, c['h2'], mode="tn", tm=256, tn=D, tk=T, out_dtype=BF16, name="dw_up")
        dx1, dx1b, dg2 = _rms_bwd(dx, dh2, c['x1'], W['norm_ffn_g'][l], name="ffn_norm_bwd")
        gsmall['norm_ffn_g', l] = dg2.sum(0)
        dmix = _matmul(dx1b, full['w_out', l], mode="nt", tm=1024, tn=512, tk=2048, out_dtype=F32, name="d_mixed")
        gmat['w_out', l] = _matmul(c['mixed'], dx1b, mode="tn", tm=512, tn=D, tk=2048, out_dtype=BF16, name="dw_out")
        du, dbre, dbim, dcre, dcim, da, dwglu, v3 = _s5_bwd(dmix, 0, c['proj'], 0, c['sre'], c['sim'], c['yp'], c['z'], c['bre'],
                                                            c['bim'], c['cre'], c['cim'], c['cst_b'], W['s5_d'][l], full['s5_w_glu', l])
        da = da.sum(1)
        glr, gli, gls, gbr, gbi = c['s5p_vjp']((da[0], da[1], dbre, dbim))
        gsmall['s5_lam_re', l], gsmall['s5_lam_im', l], gsmall['s5_log_step', l] = glr, gli, gls
        gsmall['s5_b_re', l], gsmall['s5_b_im', l] = gbr, gbi
        gsmall['s5_c_re', l] = _s5_blocks_out_diag(dcre, G, H, P)
        gsmall['s5_c_im', l] = _s5_blocks_out_diag(dcim, G, H, P)
        gsmall['s5_b_glu', l], gsmall['s5_d', l] = v3[0].sum(0), v3[1].sum(0)
        gmat['s5_w_glu', l] = dwglu.astype(BF16)
        dv, dg, dwpw, dcvw, v5 = _cv_bwd(dmix, 1, c['proj'], 1, 2, c['h1'], c['cvw'], W['cv_ln_g'][l], W['cv_ln_b'][l], full['cv_w_pw', l])
        gmat['cv_w_pw', l] = dwpw.astype(BF16)
        gsmall['cv_w_dw', l] = dcvw.sum(1)[:31]
        gsmall['cv_b_pw', l], gsmall['cv_ln_g', l], gsmall['cv_ln_b', l], gsmall['cv_b_dw', l] = [v5[j].sum(0) for j in range(4)]
        dlx, dlg, dwr, dwi, dlw, v4 = _lru_bwd(dmix, 2, c['proj'], 3, 4, c['xc'], c['rr'], c['ig'], c['hprev'], c['lruw'], c['wr'],
                                               c['wi'], c['sp8'])
        gsmall['lru_w_r', l], gsmall['lru_w_i', l] = _head_blocks_diag(dwr, nh, hd, hd), _head_blocks_diag(dwi, nh, hd, hd)
        gsmall['lru_w_conv', l] = dlw.sum(1)[:4]
        gsmall['lru_b_r', l], gsmall['lru_b_i', l] = v4[0].sum(0), v4[1].sum(0)
        gsmall['lru_lam', l] = c['sp8_vjp'](v4[2].sum(0))[0]
        gsmall['lru_b_conv', l] = v4[3].sum(0)
        dpx, dpw, dps = _pool_bwd(dmix, 3, c['proj'], 5, c['pw'], W['pool_scale'][l])
        gsmall['pool_w', l], gsmall['pool_scale', l] = dpw, dps.sum(0)
        dproj = jnp.concatenate([du, dv, dg, dlx, dlg, dpx], axis=1)
        dh = _matmul(dproj, full['w_in', l], mode="nn", tm=1024, tn=512, tk=1024, out_dtype=F32, name="d_h")
        gmat['w_in', l] = _matmul(dproj, c['h'], mode="tn", tm=512, tn=D, tk=2048, out_dtype=BF16, name="dw_in")
        dx, dxb, dg1 = _rms_bwd(dx1, dh, c['x0'], W['norm_mix_g'][l], name="mix_norm_bwd")
        gsmall['norm_mix_g', l] = dg1.sum(0)

    mats = [gmat[n, l] for l in range(depth) for n in _MATRIX]
    parts = _exchange([m.reshape(N_DEV, m.shape[0] // N_DEV, m.shape[1]) for m in mats], "scatter_matrix_grads", False)
    small_names = [n for n in _NAMES if n not in _MATRIX]
    small_keys = [(n, l) for n in small_names if n != 'norm_final_g' for l in range(depth)] + [('norm_final_g', None)]
    small_shapes = [gsmall[k].shape for k in small_keys]
    small8 = _exchange([_pack([gsmall[k] for k in small_keys])], "gather_small_grads", True)[0]
    small_sum = _sum8(small8, name="sum_small_grads")
    gs = dict(zip(small_keys, _unpack(small_sum, small_shapes)))

    out = {}
    for l in range(depth):
        for j, n in enumerate(_MATRIX):
            p8 = parts[l * len(_MATRIX) + j]
            if n in _TRANSPOSED:
                g = _sum8(p8, name="sum_" + n).T
                out[n, l] = (g,) + tuple(_adamw(W[n][l], g, Mo[n][l], Vo[n][l], name="adamw_" + n))
            else:
                out[n, l] = tuple(_sum8_adamw(p8, W[n][l], Mo[n][l], Vo[n][l], name="adamw_" + n))

    def local(n, l, g):
        if n in _COLSHARD:
            cw = W[n].shape[-1]
            return lax.dynamic_slice_in_dim(g, me * cw, cw, axis=1)
        return g

    def pick(src, k):
        return src[k[0]] if k[1] is None else src[k[0]][k[1]]
    gl = [local(k[0], k[1], gs[k]) for k in small_keys]
    lshapes = [g.shape for g in gl]
    upd = _adamw(_pack([pick(W, k) for k in small_keys]), _pack(gl), _pack([pick(Mo, k) for k in small_keys]),
                 _pack([pick(Vo, k) for k in small_keys]), name="adamw_small")
    upd = [_unpack(u, lshapes) for u in upd]
    for j, k in enumerate(small_keys):
        out[k] = (gl[j], upd[0][j], upd[1][j], upd[2][j])

    res = [loss, dx[None]]
    for q in range(4):
        for n in _NAMES:
            if n == 'norm_final_g':
                res.append(out[n, None][q])
            else:
                res.append(jnp.stack([out[n, l][q] for l in range(depth)]))
    return tuple(res)
```

```python
import functools
import math

import jax
import jax.numpy as jnp
from jax import lax
from jax.experimental import pallas as pl
from jax.experimental.pallas import tpu as pltpu

F32 = jnp.float32
BF16 = jnp.bfloat16
N_DEV = 8
AXES = ("x", "y", "c")
EPS = 1e-6
S5_GROUP_CH = 16
S5_STATE = 64
LRU_HEADS = 8
LRU_C = 8.0
POOL_WINDOWS = (2, 4, 8, 16)
CONV_HALO = 32
SMALL_HALO = 8
POOL_HALO = 16
ADAM_LR, ADAM_B1, ADAM_B2, ADAM_EPS, ADAM_WD, ADAM_STEP = 0.001, 0.9, 0.999, 1e-08, 0.01, 10
_GELU_K = math.sqrt(2.0 / math.pi)
ARB = pltpu.CompilerParams(dimension_semantics=("arbitrary",))


def _pick(n, pref, mult):
    best = None
    for t in range(mult, min(n, pref) + 1, mult):
        if n % t == 0:
            best = t
    return best if best is not None else n


def _sigmoid(x):
    return 1.0 / (1.0 + jnp.exp(-x))


def _gelu(x):
    return 0.5 * x * (1.0 + jnp.tanh(_GELU_K * (x + 0.044715 * x * x * x)))


def _gelu_grad(x):
    t = jnp.tanh(_GELU_K * (x + 0.044715 * x * x * x))
    return 0.5 * (1.0 + t) + 0.5 * x * (1.0 - t * t) * _GELU_K * (1.0 + 3.0 * 0.044715 * x * x)


def _neg_expm1(x):
    series = -x * (1.0 + x * (0.5 + x * (1.0 / 6.0 + x * (1.0 / 24.0 + x * (1.0 / 120.0 + x * (1.0 / 720.0))))))
    return jnp.where(jnp.abs(x) < 0.25, series, 1.0 - jnp.exp(x))


def _fold8(x):
    return x.reshape(x.shape[0] // 8, 8, x.shape[1]).sum(axis=0)


def _dot(a, b):
    return jnp.dot(a.astype(BF16), b.astype(BF16), preferred_element_type=F32)


def _dot_nt(a, b):
    return lax.dot_general(a.astype(BF16), b.astype(BF16), (((1,), (1,)), ((), ())), preferred_element_type=F32)


def _dot_tn(a, b):
    return lax.dot_general(a.astype(BF16), b.astype(BF16), (((0,), (0,)), ((), ())), preferred_element_type=F32)


def _full(shape):
    nd = len(shape)
    return pl.BlockSpec(shape, lambda *i: (0,) * nd)


def _matmul(a, b, *, mode, tm, tn, tk, out_dtype, name, add=None):
    if mode == "nn":
        (M, K), N = a.shape, b.shape[1]
    elif mode == "nt":
        (M, K), N = a.shape, b.shape[0]
    else:
        (K, M), N = a.shape, b.shape[1]
    tm, tn, tk = _pick(M, tm, 8), _pick(N, tn, 128), _pick(K, tk, 128)
    if mode == "tn":
        tm = _pick(M, tm, 128)
    nk = K // tk
    a_spec = {"nn": pl.BlockSpec((tm, tk), lambda i, j, k: (i, k)), "nt": pl.BlockSpec((tm, tk), lambda i, j, k: (i, k)),
              "tn": pl.BlockSpec((tk, tm), lambda i, j, k: (k, i))}[mode]
    b_spec = {"nn": pl.BlockSpec((tk, tn), lambda i, j, k: (k, j)), "nt": pl.BlockSpec((tn, tk), lambda i, j, k: (j, k)),
              "tn": pl.BlockSpec((tk, tn), lambda i, j, k: (k, j))}[mode]
    o_spec = pl.BlockSpec((tm, tn), lambda i, j, k: (i, j))
    dot = {"nn": _dot, "nt": _dot_nt, "tn": _dot_tn}[mode]
    has_add = add is not None

    def body(*refs):
        if has_add:
            a_ref, b_ref, add_ref, o_ref, acc_ref = refs
        else:
            a_ref, b_ref, o_ref, acc_ref = refs
        k = pl.program_id(2)

        @pl.when(k == 0)
        def _():
            acc_ref[...] = jnp.zeros_like(acc_ref)

        acc_ref[...] += dot(a_ref[...], b_ref[...])

        @pl.when(k == nk - 1)
        def _():
            r = acc_ref[...]
            if has_add:
                r = r + add_ref[...]
            o_ref[...] = r.astype(out_dtype)

    ins = [a, b] + ([add] if has_add else [])
    specs = [a_spec, b_spec] + ([o_spec] if has_add else [])
    return pl.pallas_call(
        body, name=name, grid=(M // tm, N // tn, nk), in_specs=specs, out_specs=o_spec,
        out_shape=jax.ShapeDtypeStruct((M, N), out_dtype), scratch_shapes=[pltpu.VMEM((tm, tn), F32)],
        compiler_params=pltpu.CompilerParams(dimension_semantics=("parallel", "parallel", "arbitrary")),
    )(*ins)


def _norm_matmul(x, g, wt, *, tm, tn, name):
    T, D = x.shape
    N = wt.shape[0]
    tm, tn = _pick(T, tm, 8), _pick(N, tn, 128)

    def body(x_ref, g_ref, w_ref, o_ref, h_ref):
        @pl.when(pl.program_id(1) == 0)
        def _():
            xv = x_ref[...]
            r = lax.rsqrt(jnp.mean(xv * xv, axis=-1, keepdims=True) + EPS)
            h_ref[...] = (xv * r * g_ref[...]).astype(BF16)

        o_ref[...] = _dot_nt(h_ref[...], w_ref[...])

    return pl.pallas_call(
        body, name=name, grid=(T // tm, N // tn),
        in_specs=[pl.BlockSpec((tm, D), lambda i, j: (i, 0)), pl.BlockSpec((1, D), lambda i, j: (0, 0)),
                  pl.BlockSpec((tn, D), lambda i, j: (j, 0))],
        out_specs=[pl.BlockSpec((tm, tn), lambda i, j: (i, j)), pl.BlockSpec((tm, D), lambda i, j: (i, 0))],
        out_shape=[jax.ShapeDtypeStruct((T, N), F32), jax.ShapeDtypeStruct((T, D), BF16)],
        compiler_params=pltpu.CompilerParams(dimension_semantics=("parallel", "arbitrary")),
    )(x, g.reshape(1, D), wt)


def _rms_bwd(dres, dh, x, g, *, name):
    T, D = x.shape
    tc = _pick(T, 256, 8)

    def body(dres_ref, dh_ref, x_ref, g_ref, dx_ref, dxb_ref, dg_ref):
        @pl.when(pl.program_id(0) == 0)
        def _():
            dg_ref[...] = jnp.zeros_like(dg_ref)

        xv, dhv = x_ref[...], dh_ref[...]
        r = lax.rsqrt(jnp.mean(xv * xv, axis=-1, keepdims=True) + EPS)
        dg_ref[...] += _fold8(dhv * xv * r)
        dyg = dhv * g_ref[...]
        dx = dres_ref[...] + r * dyg - xv * (r * r * r) * jnp.mean(dyg * xv, axis=-1, keepdims=True)
        dx_ref[...] = dx
        dxb_ref[...] = dx.astype(BF16)

    row = pl.BlockSpec((tc, D), lambda i: (i, 0))
    return pl.pallas_call(
        body, name=name, grid=(T // tc,), in_specs=[row, row, row, _full((1, D))],
        out_specs=[row, row, _full((8, D))],
        out_shape=[jax.ShapeDtypeStruct((T, D), F32), jax.ShapeDtypeStruct((T, D), BF16), jax.ShapeDtypeStruct((8, D), F32)],
        compiler_params=ARB,
    )(dres, dh, x, g.reshape(1, D))


def _loss_head(x, g, target):
    T, D = x.shape
    tc = _pick(T, 256, 8)

    def body(x_ref, g_ref, t_ref, dx_ref, dxb_ref, dg_ref, se_ref):
        @pl.when(pl.program_id(0) == 0)
        def _():
            dg_ref[...] = jnp.zeros_like(dg_ref)
            se_ref[...] = jnp.zeros_like(se_ref)

        xv = x_ref[...]
        r = lax.rsqrt(jnp.mean(xv * xv, axis=-1, keepdims=True) + EPS)
        gv = g_ref[...]
        err = xv * r * gv - t_ref[...]
        se_ref[...] += _fold8(err * err)
        dy = err * (1.0 / D)
        dg_ref[...] += _fold8(dy * xv * r)
        dyg = dy * gv
        dx = r * dyg - xv * (r * r * r) * jnp.mean(dyg * xv, axis=-1, keepdims=True)
        dx_ref[...] = dx
        dxb_ref[...] = dx.astype(BF16)

    row = pl.BlockSpec((tc, D), lambda i: (i, 0))
    return pl.pallas_call(
        body, name="loss_head", grid=(T // tc,), in_specs=[row, _full((1, D)), row],
        out_specs=[row, row, _full((8, D)), _full((8, D))],
        out_shape=[jax.ShapeDtypeStruct((T, D), F32), jax.ShapeDtypeStruct((T, D), BF16),
                   jax.ShapeDtypeStruct((8, D), F32), jax.ShapeDtypeStruct((8, D), F32)],
        compiler_params=ARB,
    )(x, g.reshape(1, D), target)


def _s5_scan_consts(ar, ai, reverse):
    if reverse:
        ai = -ai
    pr, pi_ = [ar], [ai]
    for _ in range(7):
        pr, pi_ = pr + [pr[-1] * ar - pi_[-1] * ai], pi_ + [pr[-1] * ai + pi_[-1] * ar]
    rows = jnp.arange(8)[:, None]
    planes = []
    for d in (1, 2, 4):
        mask = (rows <= 7 - d) if reverse else (rows >= d)
        planes += [jnp.where(mask, pr[d - 1][None, :], 0.0), jnp.where(mask, pi_[d - 1][None, :], 0.0)]
    order = list(range(7, -1, -1)) if reverse else list(range(8))
    planes += [jnp.stack([pr[k] for k in order]), jnp.stack([pi_[k] for k in order])]
    return jnp.stack(planes).astype(F32)


def _s5_block_scan(xr_ref, xi_ref, cst_ref, car_ref, nblk, reverse, lb, extra=None):
    ns = xr_ref.shape[1]
    crow = 0 if reverse else 7

    def blk(i, _):
        ib = (nblk - 1 - i) if reverse else i
        r0 = pl.multiple_of(ib * 8, 8)
        for k in range(ns // lb):
            sl = slice(k * lb, (k + 1) * lb)
            xr, xi = xr_ref[pl.ds(r0, 8), sl], xi_ref[pl.ds(r0, 8), sl]
            for n, d in enumerate((1, 2, 4)):
                ar, ai = cst_ref[2 * n, :, sl], cst_ref[2 * n + 1, :, sl]
                sh = (8 - d) if reverse else d
                rr, ri = pltpu.roll(xr, sh, 0), pltpu.roll(xi, sh, 0)
                xr, xi = xr + ar * rr - ai * ri, xi + ar * ri + ai * rr
            pr, pi_ = cst_ref[6, :, sl], cst_ref[7, :, sl]
            cr, ci = car_ref[0, crow:crow + 1, sl], car_ref[1, crow:crow + 1, sl]
            xr, xi = xr + pr * cr - pi_ * ci, xi + pr * ci + pi_ * cr
            xr_ref[pl.ds(r0, 8), sl] = xr
            xi_ref[pl.ds(r0, 8), sl] = xi
            if extra is not None:
                extra(r0, sl, xr, xi, cr, ci)
            car_ref[0, :, sl] = xr
            car_ref[1, :, sl] = xi
        return 0

    lax.fori_loop(0, nblk, blk, 0)


def _s5_fwd(proj, col, bre, bim, cre, cim, cst, dvec, wglu, bglu):
    T = proj.shape[0]
    nsb, cw, sw = bre.shape
    M, NS = nsb * cw, nsb * sw
    tc = _pick(T, 256, 8)

    def body(u_ref, bre_ref, bim_ref, cre_ref, cim_ref, cst_ref, d_ref, wg_ref, bg_ref,
             y_ref, sre_ref, sim_ref, yp_ref, z_ref, car_ref):
        @pl.when(pl.program_id(0) == 0)
        def _():
            car_ref[...] = jnp.zeros_like(car_ref)

        u = u_ref[...]
        ub = u.astype(BF16)
        for k in range(nsb):
            sre_ref[:, k * sw:(k + 1) * sw] = _dot(ub[:, k * cw:(k + 1) * cw], bre_ref[k])
            sim_ref[:, k * sw:(k + 1) * sw] = _dot(ub[:, k * cw:(k + 1) * cw], bim_ref[k])
        _s5_block_scan(sre_ref, sim_ref, cst_ref, car_ref, tc // 8, False, sw)
        for k in range(nsb):
            yk = _dot(sre_ref[:, k * sw:(k + 1) * sw], cre_ref[k]) - _dot(sim_ref[:, k * sw:(k + 1) * sw], cim_ref[k])
            yp_ref[:, k * cw:(k + 1) * cw] = yk + d_ref[:, k * cw:(k + 1) * cw] * u[:, k * cw:(k + 1) * cw]
        gl = _gelu(yp_ref[...])
        z = _dot(gl, wg_ref[...]) + bg_ref[...]
        z_ref[...] = z
        y_ref[...] = (gl * _sigmoid(z)).astype(BF16)

    rowm = pl.BlockSpec((tc, M), lambda i: (i, 0))
    rows = pl.BlockSpec((tc, NS), lambda i: (i, 0))
    return pl.pallas_call(
        body, name="s5_fwd", grid=(T // tc,),
        in_specs=[pl.BlockSpec((tc, M), lambda i: (i, col)), _full(bre.shape), _full(bim.shape), _full(cre.shape),
                  _full(cim.shape), _full(cst.shape), _full((1, M)), _full((M, M)), _full((1, M))],
        out_specs=[rowm, rows, rows, rowm, rowm],
        out_shape=[jax.ShapeDtypeStruct((T, M), BF16), jax.ShapeDtypeStruct((T, NS), F32), jax.ShapeDtypeStruct((T, NS), F32),
                   jax.ShapeDtypeStruct((T, M), F32), jax.ShapeDtypeStruct((T, M), F32)],
        scratch_shapes=[pltpu.VMEM((2, 8, NS), F32)], compiler_params=ARB,
    )(proj, bre, bim, cre, cim, cst, dvec.reshape(1, M), wglu, bglu.reshape(1, M))


def _s5_bwd(dmix, dcol, proj, col, sre, sim, yp, z, bre, bim, cre, cim, cst, dvec, wglu):
    T = proj.shape[0]
    nsb, cw, sw = bre.shape
    M, NS = nsb * cw, nsb * sw
    tc = _pick(T, 256, 8)
    nc = T // tc

    def body(do_ref, u_ref, sre_ref, sim_ref, yp_ref, z_ref, bre_ref, bim_ref, cre_ref, cim_ref, cst_ref, d_ref, wg_ref,
             du_ref, dbre_ref, dbim_ref, dcre_ref, dcim_ref, da_ref, dwg_ref, vec_ref, gre_ref, gim_ref, car_ref):
        @pl.when(pl.program_id(0) == 0)
        def _():
            for r in (dbre_ref, dbim_ref, dcre_ref, dcim_ref, da_ref, dwg_ref, vec_ref, car_ref):
                r[...] = jnp.zeros_like(r)

        u, do, ypv = u_ref[...], do_ref[...], yp_ref[...]
        gl = _gelu(ypv)
        sg = _sigmoid(z_ref[...])
        dz = do * gl * sg * (1.0 - sg)
        dgl = do * sg + _dot_nt(dz, wg_ref[...])
        dwg_ref[...] += _dot_tn(gl, dz)
        vec_ref[0] += _fold8(dz)
        dy = dgl * _gelu_grad(ypv)
        vec_ref[1] += _fold8(dy * u)
        dyb = dy.astype(BF16)
        for k in range(nsb):
            cs, ss = slice(k * cw, (k + 1) * cw), slice(k * sw, (k + 1) * sw)
            gre_ref[:, ss] = _dot_nt(dyb[:, cs], cre_ref[k])
            gim_ref[:, ss] = -_dot_nt(dyb[:, cs], cim_ref[k])
            dcre_ref[k] += _dot_tn(sre_ref[:, ss], dyb[:, cs])
            dcim_ref[k] -= _dot_tn(sim_ref[:, ss], dyb[:, cs])

        row = lax.broadcasted_iota(jnp.int32, (8, sw), 0)

        def extra(r0, sl, gr, gi, cr, ci):
            nr = jnp.where(row == 7, cr, pltpu.roll(gr, 7, 0))
            ni = jnp.where(row == 7, ci, pltpu.roll(gi, 7, 0))
            sr, si = sre_ref[pl.ds(r0, 8), sl], sim_ref[pl.ds(r0, 8), sl]
            da_ref[0, :, sl] += nr * sr + ni * si
            da_ref[1, :, sl] += ni * sr - nr * si

        _s5_block_scan(gre_ref, gim_ref, cst_ref, car_ref, tc // 8, True, sw, extra)
        ub = u.astype(BF16)
        for k in range(nsb):
            cs, ss = slice(k * cw, (k + 1) * cw), slice(k * sw, (k + 1) * sw)
            gr, gi = gre_ref[:, ss].astype(BF16), gim_ref[:, ss].astype(BF16)
            duk = dy[:, cs] * d_ref[:, cs] + _dot_nt(gr, bre_ref[k]) + _dot_nt(gi, bim_ref[k])
            du_ref[:, cs] = duk.astype(BF16)
            dbre_ref[k] += _dot_tn(ub[:, cs], gr)
            dbim_ref[k] += _dot_tn(ub[:, cs], gi)

    def rowm(c):
        return pl.BlockSpec((tc, M), lambda i: (nc - 1 - i, c))
    rows = pl.BlockSpec((tc, NS), lambda i: (nc - 1 - i, 0))
    return pl.pallas_call(
        body, name="s5_bwd", grid=(nc,),
        in_specs=[rowm(dcol), rowm(col), rows, rows, rowm(0), rowm(0), _full(bre.shape), _full(bim.shape), _full(cre.shape),
                  _full(cim.shape), _full(cst.shape), _full((1, M)), _full((M, M))],
        out_specs=[rowm(0), _full(bre.shape), _full(bim.shape), _full(cre.shape), _full(cim.shape), _full((2, 8, NS)),
                   _full((M, M)), _full((3, 8, M))],
        out_shape=[jax.ShapeDtypeStruct((T, M), BF16), jax.ShapeDtypeStruct(bre.shape, F32), jax.ShapeDtypeStruct(bim.shape, F32),
                   jax.ShapeDtypeStruct(cre.shape, F32), jax.ShapeDtypeStruct(cim.shape, F32), jax.ShapeDtypeStruct((2, 8, NS), F32),
                   jax.ShapeDtypeStruct((M, M), F32), jax.ShapeDtypeStruct((3, 8, M), F32)],
        scratch_shapes=[pltpu.VMEM((tc, NS), F32), pltpu.VMEM((tc, NS), F32), pltpu.VMEM((2, 8, NS), F32)],
        compiler_params=ARB,
    )(dmix, proj, sre, sim, yp, z, bre, bim, cre, cim, cst, dvec.reshape(1, M), wglu)


def _ln_silu(h1, lg, lb):
    mu = jnp.mean(h1, axis=-1, keepdims=True)
    xc = h1 - mu
    rstd = lax.rsqrt(jnp.mean(xc * xc, axis=-1, keepdims=True) + EPS)
    xhat = xc * rstd
    h2 = xhat * lg + lb
    return xhat, rstd, h2, _sigmoid(h2)


def _cv_fwd(proj, vcol, gcol, wdw, bdw, lg, lb, wpw, bpw):
    T = proj.shape[0]
    M = wpw.shape[0]
    tc = _pick(T, 256, 8)
    H, K = CONV_HALO, 31

    def body(v_ref, g_ref, wdw_ref, bdw_ref, lg_ref, lb_ref, wpw_ref, bpw_ref, y_ref, h1_ref, buf_ref):
        @pl.when(pl.program_id(0) == 0)
        def _():
            buf_ref[0:H, :] = jnp.zeros((H, M), F32)

        buf_ref[H:H + tc, :] = v_ref[...] * _sigmoid(g_ref[...])
        acc = jnp.zeros((tc, M), F32) + bdw_ref[...]
        for k in range(K):
            acc = acc + wdw_ref[k:k + 1, :] * buf_ref[pl.ds(H - (K - 1) + k, tc), :]
        h1_ref[...] = acc
        _, _, h2, sg = _ln_silu(acc, lg_ref[...], lb_ref[...])
        y_ref[...] = (_dot(h2 * sg, wpw_ref[...]) + bpw_ref[...]).astype(BF16)
        buf_ref[0:H, :] = buf_ref[tc:tc + H, :]

    def rowm(c):
        return pl.BlockSpec((tc, M), lambda i: (i, c))
    vec = _full((1, M))
    return pl.pallas_call(
        body, name="cv_fwd", grid=(T // tc,),
        in_specs=[rowm(vcol), rowm(gcol), _full((32, M)), vec, vec, vec, _full((M, M)), vec],
        out_specs=[rowm(0), rowm(0)],
        out_shape=[jax.ShapeDtypeStruct((T, M), BF16), jax.ShapeDtypeStruct((T, M), F32)],
        scratch_shapes=[pltpu.VMEM((tc + H, M), F32)], compiler_params=ARB,
    )(proj, proj, wdw, bdw.reshape(1, M), lg.reshape(1, M), lb.reshape(1, M), wpw, bpw.reshape(1, M))


def _cv_bwd(dmix, dcol, proj, vcol, gcol, h1, wdw, lg, lb, wpw):
    T = proj.shape[0]
    M = wpw.shape[0]
    tc = _pick(T, 256, 8)
    nc = T // tc
    H, K = CONV_HALO, 31
    hb = tc // H

    def body(do_ref, v_ref, g_ref, vh_ref, gh_ref, h1_ref, wdw_ref, lg_ref, lb_ref, wpw_ref,
             dv_ref, dg_ref, dwpw_ref, dwdw_ref, vec_ref, hbuf_ref, dbuf_ref):
        i = pl.program_id(0)

        @pl.when(i == 0)
        def _():
            for r in (dwpw_ref, dwdw_ref, vec_ref):
                r[...] = jnp.zeros_like(r)
            dbuf_ref[tc:tc + H, :] = jnp.zeros((H, M), F32)

        v, g = v_ref[...], g_ref[...]
        sgg = _sigmoid(g)
        halo = vh_ref[...] * _sigmoid(gh_ref[...])
        hbuf_ref[0:H, :] = jnp.where(i == nc - 1, 0.0, halo)
        hbuf_ref[H:H + tc, :] = v * sgg
        do = do_ref[...]
        xhat, rstd, h2, sg = _ln_silu(h1_ref[...], lg_ref[...], lb_ref[...])
        dwpw_ref[...] += _dot_tn(h2 * sg, do)
        vec_ref[0] += _fold8(do)
        dh2 = _dot_nt(do, wpw_ref[...]) * (sg * (1.0 + h2 * (1.0 - sg)))
        vec_ref[1] += _fold8(dh2 * xhat)
        vec_ref[2] += _fold8(dh2)
        dxh = dh2 * lg_ref[...]
        dh1 = rstd * (dxh - jnp.mean(dxh, axis=-1, keepdims=True) - xhat * jnp.mean(dxh * xhat, axis=-1, keepdims=True))
        vec_ref[3] += _fold8(dh1)
        dbuf_ref[0:tc, :] = dh1
        dh0 = jnp.zeros((tc, M), F32)
        for k in range(K):
            dh0 = dh0 + wdw_ref[k:k + 1, :] * dbuf_ref[pl.ds(K - 1 - k, tc), :]
            dwdw_ref[k] += _fold8(dh1 * hbuf_ref[pl.ds(H - (K - 1) + k, tc), :])
        dv_ref[...] = (dh0 * sgg).astype(BF16)
        dg_ref[...] = (dh0 * v * sgg * (1.0 - sgg)).astype(BF16)
        dbuf_ref[tc:tc + H, :] = dbuf_ref[0:H, :]

    def rowm(c):
        return pl.BlockSpec((tc, M), lambda i: (nc - 1 - i, c))

    def halo(c):
        return pl.BlockSpec((H, M), lambda i: (jnp.maximum((nc - 1 - i) * hb - 1, 0), c))
    vec = _full((1, M))
    return pl.pallas_call(
        body, name="cv_bwd", grid=(nc,),
        in_specs=[rowm(dcol), rowm(vcol), rowm(gcol), halo(vcol), halo(gcol), rowm(0), _full((32, M)), vec, vec, _full((M, M))],
        out_specs=[rowm(0), rowm(0), _full((M, M)), _full((32, 8, M)), _full((5, 8, M))],
        out_shape=[jax.ShapeDtypeStruct((T, M), BF16), jax.ShapeDtypeStruct((T, M), BF16), jax.ShapeDtypeStruct((M, M), F32),
                   jax.ShapeDtypeStruct((32, 8, M), F32), jax.ShapeDtypeStruct((5, 8, M), F32)],
        scratch_shapes=[pltpu.VMEM((tc + H, M), F32), pltpu.VMEM((tc + H, M), F32)], compiler_params=ARB,
    )(dmix, proj, proj, proj, proj, h1, wdw, lg.reshape(1, M), lb.reshape(1, M), wpw)


def _lru_gates(xc, r, ig, sp8):
    la = -r * sp8
    a = jnp.exp(la)
    mult = jnp.sqrt(_neg_expm1(2.0 * la))
    return a, mult, mult * (ig * xc)


def _lru_fwd(proj, xcol, gcol, wcv, bcv, wr, br, wi, bi, sp8):
    T = proj.shape[0]
    M = wr.shape[0]
    tc = _pick(T, 256, 8)
    H, K = SMALL_HALO, 4

    def body(x_ref, g_ref, wcv_ref, bcv_ref, wr_ref, br_ref, wi_ref, bi_ref, sp_ref,
             y_ref, xc_ref, r_ref, i_ref, hp_ref, buf_ref, a_ref, car_ref):
        @pl.when(pl.program_id(0) == 0)
        def _():
            buf_ref[0:H, :] = jnp.zeros((H, M), F32)
            car_ref[...] = jnp.zeros_like(car_ref)

        buf_ref[H:H + tc, :] = x_ref[...]
        xc = jnp.zeros((tc, M), F32) + bcv_ref[...]
        for k in range(K):
            xc = xc + wcv_ref[k:k + 1, :] * buf_ref[pl.ds(H - (K - 1) + k, tc), :]
        buf_ref[0:H, :] = buf_ref[tc:tc + H, :]
        r = _sigmoid(_dot(xc, wr_ref[...]) + br_ref[...])
        ig = _sigmoid(_dot(xc, wi_ref[...]) + bi_ref[...])
        xc_ref[...] = xc
        r_ref[...] = r
        i_ref[...] = ig
        a, _, bt = _lru_gates(xc, r, ig, sp_ref[...])
        a_ref[...] = a
        hp_ref[...] = bt
        row = lax.broadcasted_iota(jnp.int32, (8, M), 0)

        def blk(ib, _):
            r0 = pl.multiple_of(ib * 8, 8)
            av, bv = a_ref[pl.ds(r0, 8), :], hp_ref[pl.ds(r0, 8), :]
            for d in (1, 2, 4):
                m = row >= d
                bv = jnp.where(m, bv + av * pltpu.roll(bv, d, 0), bv)
                av = jnp.where(m, av * pltpu.roll(av, d, 0), av)
            cr = car_ref[7:8, :]
            h = bv + av * cr
            a_ref[pl.ds(r0, 8), :] = h
            hp_ref[pl.ds(r0, 8), :] = jnp.where(row == 0, cr, pltpu.roll(h, 1, 0))
            car_ref[...] = h
            return 0

        lax.fori_loop(0, tc // 8, blk, 0)
        y_ref[...] = (a_ref[...] * _gelu(g_ref[...])).astype(BF16)

    def rowm(c):
        return pl.BlockSpec((tc, M), lambda i: (i, c))
    vec = _full((1, M))
    return pl.pallas_call(
        body, name="lru_fwd", grid=(T // tc,),
        in_specs=[rowm(xcol), rowm(gcol), _full((8, M)), vec, _full((M, M)), vec, _full((M, M)), vec, vec],
        out_specs=[rowm(0)] * 5,
        out_shape=[jax.ShapeDtypeStruct((T, M), BF16)] + [jax.ShapeDtypeStruct((T, M), F32)] * 4,
        scratch_shapes=[pltpu.VMEM((tc + H, M), F32), pltpu.VMEM((tc, M), F32), pltpu.VMEM((8, M), F32)], compiler_params=ARB,
    )(proj, proj, wcv, bcv.reshape(1, M), wr, br.reshape(1, M), wi, bi.reshape(1, M), sp8.reshape(1, M))


def _lru_bwd(dmix, dcol, proj, xcol, gcol, xc, r, ig, hprev, wcv, wr, wi, sp8):
    T = proj.shape[0]
    M = wr.shape[0]
    tc = _pick(T, 256, 8)
    nc = T // tc
    H, K = SMALL_HALO, 4
    hb = tc // H

    def body(do_ref, x_ref, g_ref, xh_ref, xc_ref, r_ref, i_ref, hp_ref, wcv_ref, wr_ref, wi_ref, sp_ref,
             dx_ref, dg_ref, dwr_ref, dwi_ref, dwcv_ref, vec_ref, xbuf_ref, dbuf_ref, a_ref, gs_ref, car_ref):
        i = pl.program_id(0)

        @pl.when(i == 0)
        def _():
            for rf in (dwr_ref, dwi_ref, dwcv_ref, vec_ref, car_ref):
                rf[...] = jnp.zeros_like(rf)
            dbuf_ref[tc:tc + H, :] = jnp.zeros((H, M), F32)

        xcv, rv, iv, hpv, gv, do = xc_ref[...], r_ref[...], i_ref[...], hp_ref[...], g_ref[...], do_ref[...]
        sp = sp_ref[...]
        a, mult, bt = _lru_gates(xcv, rv, iv, sp)
        h = a * hpv + bt
        ge = _gelu(gv)
        dg_ref[...] = (do * h * _gelu_grad(gv)).astype(BF16)
        a_ref[...] = a
        gs_ref[...] = do * ge
        row = lax.broadcasted_iota(jnp.int32, (8, M), 0)

        def blk(n, _):
            r0 = pl.multiple_of((tc // 8 - 1 - n) * 8, 8)
            ablk = a_ref[pl.ds(r0, 8), :]
            av = jnp.where(row == 7, car_ref[0, 0:1, :], pltpu.roll(ablk, 7, 0))
            bv = gs_ref[pl.ds(r0, 8), :]
            for d in (1, 2, 4):
                m = row <= 7 - d
                bv = jnp.where(m, bv + av * pltpu.roll(bv, 8 - d, 0), bv)
                av = jnp.where(m, av * pltpu.roll(av, 8 - d, 0), av)
            gsv = bv + av * car_ref[1, 0:1, :]
            gs_ref[pl.ds(r0, 8), :] = gsv
            car_ref[0] = ablk
            car_ref[1] = gsv
            return 0

        lax.fori_loop(0, tc // 8, blk, 0)
        gs = gs_ref[...]
        dixc = gs * mult
        dla = gs * hpv * a - gs * (iv * xcv) * (a * a) / mult
        vec_ref[2] += _fold8(-dla * rv)
        dpr = (-dla * sp) * rv * (1.0 - rv)
        dpi = (dixc * xcv) * iv * (1.0 - iv)
        vec_ref[0] += _fold8(dpr)
        vec_ref[1] += _fold8(dpi)
        dwr_ref[...] += _dot_tn(xcv, dpr)
        dwi_ref[...] += _dot_tn(xcv, dpi)
        dxc = dixc * iv + _dot_nt(dpr, wr_ref[...]) + _dot_nt(dpi, wi_ref[...])
        vec_ref[3] += _fold8(dxc)
        xbuf_ref[0:H, :] = jnp.where(i == nc - 1, 0.0, xh_ref[...])
        xbuf_ref[H:H + tc, :] = x_ref[...]
        dbuf_ref[0:tc, :] = dxc
        dx = jnp.zeros((tc, M), F32)
        for k in range(K):
            dx = dx + wcv_ref[k:k + 1, :] * dbuf_ref[pl.ds(K - 1 - k, tc), :]
            dwcv_ref[k] += _fold8(dxc * xbuf_ref[pl.ds(H - (K - 1) + k, tc), :])
        dx_ref[...] = dx.astype(BF16)
        dbuf_ref[tc:tc + H, :] = dbuf_ref[0:H, :]

    def rowm(c):
        return pl.BlockSpec((tc, M), lambda i: (nc - 1 - i, c))
    halo = pl.BlockSpec((H, M), lambda i: (jnp.maximum((nc - 1 - i) * hb - 1, 0), xcol))
    vec = _full((1, M))
    return pl.pallas_call(
        body, name="lru_bwd", grid=(nc,),
        in_specs=[rowm(dcol), rowm(xcol), rowm(gcol), halo, rowm(0), rowm(0), rowm(0), rowm(0), _full((8, M)),
                  _full((M, M)), _full((M, M)), vec],
        out_specs=[rowm(0), rowm(0), _full((M, M)), _full((M, M)), _full((8, 8, M)), _full((4, 8, M))],
        out_shape=[jax.ShapeDtypeStruct((T, M), BF16), jax.ShapeDtypeStruct((T, M), BF16), jax.ShapeDtypeStruct((M, M), F32),
                   jax.ShapeDtypeStruct((M, M), F32), jax.ShapeDtypeStruct((8, 8, M), F32), jax.ShapeDtypeStruct((4, 8, M), F32)],
        scratch_shapes=[pltpu.VMEM((tc + H, M), F32), pltpu.VMEM((tc + H, M), F32), pltpu.VMEM((tc, M), F32),
                        pltpu.VMEM((tc, M), F32), pltpu.VMEM((2, 8, M), F32)],
        compiler_params=ARB,
    )(dmix, proj, proj, proj, xc, r, ig, hprev, wcv, wr, wi, sp8.reshape(1, M))


def _pool_diffs(buf_ref, x, t0, tc, gw):
    H = POOL_HALO
    pos = (lax.broadcasted_iota(jnp.int32, (tc, gw), 0) + t0 + 1).astype(F32)
    out, inv = [], []
    for gi, win in enumerate(POOL_WINDOWS):
        sl = slice(gi * gw, (gi + 1) * gw)
        s = x[:, sl]
        for j in range(1, win):
            s = s + buf_ref[pl.ds(H - j, tc), sl]
        ic = 1.0 / jnp.minimum(pos, float(win))
        out.append(s * ic - x[:, sl])
        inv.append(ic)
    return out, inv


def _pool_fwd(proj, col, pw, scale):
    T = proj.shape[0]
    ng, gw, _ = pw.shape
    M = ng * gw
    tc = _pick(T, 256, 8)
    H = POOL_HALO

    def body(x_ref, pw_ref, sc_ref, y_ref, buf_ref):
        @pl.when(pl.program_id(0) == 0)
        def _():
            buf_ref[0:H, :] = jnp.zeros((H, M), F32)

        x = x_ref[...]
        buf_ref[H:H + tc, :] = x
        diffs, _ = _pool_diffs(buf_ref, x, pl.program_id(0) * tc, tc, gw)
        for gi in range(ng):
            sl = slice(gi * gw, (gi + 1) * gw)
            y_ref[:, sl] = (_dot(diffs[gi], pw_ref[gi]) * sc_ref[:, sl]).astype(BF16)
        buf_ref[0:H, :] = buf_ref[tc:tc + H, :]

    return pl.pallas_call(
        body, name="pool_fwd", grid=(T // tc,),
        in_specs=[pl.BlockSpec((tc, M), lambda i: (i, col)), _full(pw.shape), _full((1, M))],
        out_specs=pl.BlockSpec((tc, M), lambda i: (i, 0)), out_shape=jax.ShapeDtypeStruct((T, M), BF16),
        scratch_shapes=[pltpu.VMEM((tc + H, M), F32)], compiler_params=ARB,
    )(proj, pw, scale.reshape(1, M))


def _pool_bwd(dmix, dcol, proj, col, pw, scale):
    T = proj.shape[0]
    ng, gw, _ = pw.shape
    M = ng * gw
    tc = _pick(T, 256, 8)
    nc = T // tc
    H = POOL_HALO
    hb = tc // H

    def body(do_ref, x_ref, xh_ref, pw_ref, sc_ref, dx_ref, dpw_ref, dsc_ref, buf_ref, ebuf_ref):
        i = pl.program_id(0)

        @pl.when(i == 0)
        def _():
            dpw_ref[...] = jnp.zeros_like(dpw_ref)
            dsc_ref[...] = jnp.zeros_like(dsc_ref)
            ebuf_ref[tc:tc + H, :] = jnp.zeros((H, M), F32)

        x, do = x_ref[...], do_ref[...]
        buf_ref[0:H, :] = jnp.where(i == nc - 1, 0.0, xh_ref[...])
        buf_ref[H:H + tc, :] = x
        diffs, inv = _pool_diffs(buf_ref, x, (nc - 1 - i) * tc, tc, gw)
        ddg = []
        for gi in range(ng):
            sl = slice(gi * gw, (gi + 1) * gw)
            dyy = do[:, sl] * sc_ref[:, sl]
            dsc_ref[:, sl] += _fold8(do[:, sl] * _dot(diffs[gi], pw_ref[gi]))
            dpw_ref[gi] += _dot_tn(diffs[gi], dyy)
            d = _dot_nt(dyy, pw_ref[gi])
            ddg.append(d)
            ebuf_ref[0:tc, sl] = d * inv[gi]
        for gi, win in enumerate(POOL_WINDOWS):
            sl = slice(gi * gw, (gi + 1) * gw)
            s = -ddg[gi]
            for j in range(win):
                s = s + ebuf_ref[pl.ds(j, tc), sl]
            dx_ref[:, sl] = s.astype(BF16)
        ebuf_ref[tc:tc + H, :] = ebuf_ref[0:H, :]

    return pl.pallas_call(
        body, name="pool_bwd", grid=(nc,),
        in_specs=[pl.BlockSpec((tc, M), lambda i: (nc - 1 - i, dcol)), pl.BlockSpec((tc, M), lambda i: (nc - 1 - i, col)),
                  pl.BlockSpec((H, M), lambda i: (jnp.maximum((nc - 1 - i) * hb - 1, 0), col)), _full(pw.shape), _full((1, M))],
        out_specs=[pl.BlockSpec((tc, M), lambda i: (nc - 1 - i, 0)), _full(pw.shape), _full((8, M))],
        out_shape=[jax.ShapeDtypeStruct((T, M), BF16), jax.ShapeDtypeStruct(pw.shape, F32), jax.ShapeDtypeStruct((8, M), F32)],
        scratch_shapes=[pltpu.VMEM((tc + H, M), F32), pltpu.VMEM((tc + H, M), F32)], compiler_params=ARB,
    )(dmix, proj, proj, pw, scale.reshape(1, M))


def _ffn_mid_fwd(up, wdw, bdw):
    T, F2 = up.shape
    F = F2 // 2
    tc = _pick(T, 256, 8)
    H, K = SMALL_HALO, 3

    def body(gt_ref, val_ref, w_ref, b_ref, act_ref, buf_ref):
        @pl.when(pl.program_id(0) == 0)
        def _():
            buf_ref[0:H, :] = jnp.zeros((H, F), F32)

        buf_ref[H:H + tc, :] = gt_ref[...]
        gc = jnp.zeros((tc, F), F32) + b_ref[...]
        for k in range(K):
            gc = gc + w_ref[k:k + 1, :] * buf_ref[pl.ds(H - (K - 1) + k, tc), :]
        act_ref[...] = (_gelu(gc) * val_ref[...]).astype(BF16)
        buf_ref[0:H, :] = buf_ref[tc:tc + H, :]

    return pl.pallas_call(
        body, name="ffn_mid_fwd", grid=(T // tc,),
        in_specs=[pl.BlockSpec((tc, F), lambda i: (i, 0)), pl.BlockSpec((tc, F), lambda i: (i, 1)), _full((8, F)), _full((1, F))],
        out_specs=pl.BlockSpec((tc, F), lambda i: (i, 0)), out_shape=jax.ShapeDtypeStruct((T, F), BF16),
        scratch_shapes=[pltpu.VMEM((tc + H, F), F32)], compiler_params=ARB,
    )(up, up, wdw, bdw.reshape(1, F))


def _ffn_mid_bwd(dact, up, wdw, bdw):
    T, F2 = up.shape
    F = F2 // 2
    tc = _pick(T, 128, 8)
    nc = T // tc
    H, K = SMALL_HALO, 3
    hb = tc // H

    def body(da_ref, gt_ref, val_ref, gh_ref, w_ref, b_ref, dup_ref, dw_ref, db_ref, gbuf_ref, dbuf_ref):
        i = pl.program_id(0)

        @pl.when(i == 0)
        def _():
            dw_ref[...] = jnp.zeros_like(dw_ref)
            db_ref[...] = jnp.zeros_like(db_ref)
            dbuf_ref[tc:tc + H, :] = jnp.zeros((H, F), F32)

        gbuf_ref[0:H, :] = jnp.where(i == nc - 1, 0.0, gh_ref[...])
        gbuf_ref[H:H + tc, :] = gt_ref[...]
        gc = jnp.zeros((tc, F), F32) + b_ref[...]
        for k in range(K):
            gc = gc + w_ref[k:k + 1, :] * gbuf_ref[pl.ds(H - (K - 1) + k, tc), :]
        da = da_ref[...]
        dup_ref[:, F:2 * F] = (da * _gelu(gc)).astype(BF16)
        dgc = da * val_ref[...] * _gelu_grad(gc)
        db_ref[...] += _fold8(dgc)
        dbuf_ref[0:tc, :] = dgc
        dgt = jnp.zeros((tc, F), F32)
        for k in range(K):
            dgt = dgt + w_ref[k:k + 1, :] * dbuf_ref[pl.ds(K - 1 - k, tc), :]
            dw_ref[k] += _fold8(dgc * gbuf_ref[pl.ds(H - (K - 1) + k, tc), :])
        dup_ref[:, 0:F] = dgt.astype(BF16)
        dbuf_ref[tc:tc + H, :] = dbuf_ref[0:H, :]

    return pl.pallas_call(
        body, name="ffn_mid_bwd", grid=(nc,),
        in_specs=[pl.BlockSpec((tc, F), lambda i: (nc - 1 - i, 0)), pl.BlockSpec((tc, F), lambda i: (nc - 1 - i, 0)),
                  pl.BlockSpec((tc, F), lambda i: (nc - 1 - i, 1)),
                  pl.BlockSpec((H, F), lambda i: (jnp.maximum((nc - 1 - i) * hb - 1, 0), 0)), _full((8, F)), _full((1, F))],
        out_specs=[pl.BlockSpec((tc, F2), lambda i: (nc - 1 - i, 0)), _full((8, 8, F)), _full((8, F))],
        out_shape=[jax.ShapeDtypeStruct((T, F2), BF16), jax.ShapeDtypeStruct((8, 8, F), F32), jax.ShapeDtypeStruct((8, F), F32)],
        scratch_shapes=[pltpu.VMEM((tc + H, F), F32), pltpu.VMEM((tc + H, F), F32)], compiler_params=ARB,
    )(dact, up, up, up, wdw, bdw.reshape(1, F))


def _my_pos():
    return lax.axis_index("x"), lax.axis_index("y"), lax.axis_index("c")


def _flip(pos, k):
    x, y, c = pos
    return ((1 - x) if k & 4 else x, (1 - y) if k & 2 else y, (1 - c) if k & 1 else c)


def _rank(pos):
    return 4 * pos[0] + 2 * pos[1] + pos[2]


def _exchange(xs, name, gather):
    n = len(xs)
    shapes = [((N_DEV,) + tuple(x.shape)) if gather else tuple(x.shape) for x in xs]

    def body(*refs):
        ins, outs = refs[:n], refs[n:2 * n]
        send_sems, recv_sems, local_sems = refs[2 * n:]
        me = _my_pos()
        mine = _rank(me)
        local, sends = [], []
        for a in range(n):
            loc = pltpu.make_async_copy(ins[a] if gather else ins[a].at[mine], outs[a].at[mine], local_sems.at[a])
            loc.start()
            local.append(loc)
            for k in range(1, N_DEV):
                peer = _flip(me, k)
                cp = pltpu.make_async_remote_copy(
                    src_ref=ins[a] if gather else ins[a].at[_rank(peer)], dst_ref=outs[a].at[mine],
                    send_sem=send_sems.at[a, k - 1], recv_sem=recv_sems.at[a, k - 1], device_id=peer,
                    device_id_type=pl.DeviceIdType.MESH)
                cp.start()
                sends.append(cp)
        for a in range(n):
            for k in range(1, N_DEV):
                peer = _flip(me, k)
                pltpu.make_async_remote_copy(
                    src_ref=ins[a] if gather else ins[a].at[_rank(peer)], dst_ref=outs[a].at[_rank(peer)],
                    send_sem=send_sems.at[a, k - 1], recv_sem=recv_sems.at[a, k - 1], device_id=peer,
                    device_id_type=pl.DeviceIdType.MESH).wait_recv()
        for cp in sends:
            cp.wait_send()
        for loc in local:
            loc.wait()

    any_spec = pl.BlockSpec(memory_space=pl.ANY)
    return pl.pallas_call(
        body, name=name, in_specs=[any_spec] * n, out_specs=[any_spec] * n,
        out_shape=[jax.ShapeDtypeStruct(s, x.dtype) for s, x in zip(shapes, xs)],
        scratch_shapes=[pltpu.SemaphoreType.DMA((n, N_DEV - 1)), pltpu.SemaphoreType.DMA((n, N_DEV - 1)),
                        pltpu.SemaphoreType.DMA((n,))],
    )(*xs)


_HBM = pl.BlockSpec(memory_space=pltpu.HBM)
_SEM = pl.BlockSpec(memory_space=pltpu.SEMAPHORE)
_EFFECT = pltpu.SideEffectType.DATAFLOW_SIDE_EFFECTING


def _landing_copy(ins, lands, send_sems, recv_sems, me, a, k, gather, outgoing):
    peer = _flip(me, k)
    return pltpu.make_async_remote_copy(
        src_ref=ins[a] if gather else ins[a].at[_rank(peer)], dst_ref=lands[a].at[_rank(me) if outgoing else _rank(peer)],
        send_sem=send_sems.at[a * (N_DEV - 1) + k - 1], recv_sem=recv_sems.at[a * (N_DEV - 1) + k - 1], device_id=peer,
        device_id_type=pl.DeviceIdType.MESH)


def _exchange_start(xs, name, gather):
    n = len(xs)
    lands = [lax.empty(((N_DEV,) + tuple(x.shape)) if gather else tuple(x.shape), x.dtype) for x in xs]

    def body(*refs):
        ins, lnd = refs[:n], refs[n:2 * n]
        send_sems, recv_sems = refs[2 * n], refs[2 * n + 1]
        token, local_sems = refs[4 * n + 2], refs[4 * n + 3]
        me = _my_pos()
        mine = _rank(me)
        local = []
        for a in range(n):
            loc = pltpu.make_async_copy(ins[a] if gather else ins[a].at[mine], lnd[a].at[mine], local_sems.at[a])
            loc.start()
            local.append(loc)
            for k in range(1, N_DEV):
                _landing_copy(ins, lnd, send_sems, recv_sems, me, a, k, gather, True).start()
        for loc in local:
            loc.wait()
        token[...] = jnp.zeros_like(token)

    hbm = lambda t: pltpu.HBM(t.shape, t.dtype)
    res = pl.pallas_call(
        body, name=name,
        out_shape=(pltpu.SemaphoreType.DMA((n * (N_DEV - 1),)), pltpu.SemaphoreType.DMA((n * (N_DEV - 1),)),
                   *[hbm(x) for x in xs], *[hbm(t) for t in lands], jax.ShapeDtypeStruct((8, 128), F32)),
        in_specs=[_HBM] * (2 * n), out_specs=(_SEM, _SEM, *([_HBM] * (2 * n)), pl.BlockSpec(memory_space=pltpu.VMEM)),
        input_output_aliases={i: 2 + i for i in range(2 * n)}, scratch_shapes=[pltpu.SemaphoreType.DMA((n,))],
        compiler_params=pltpu.CompilerParams(has_side_effects=_EFFECT),
    )(*[pltpu.with_memory_space_constraint(t, pltpu.HBM) for t in list(xs) + lands])
    return (res[0], res[1], list(res[2:2 + n]), list(res[2 + n:2 + 2 * n])), res[2 + 2 * n]


def _exchange_wait(handle, after, name, gather):
    send_sems, recv_sems, ins, lands = handle
    n = len(ins)

    def body(*refs):
        src, lnd = refs[:n], refs[n:2 * n]
        ssem, rsem = refs[2 * n], refs[2 * n + 1]
        me = _my_pos()
        for a in range(n):
            for k in range(1, N_DEV):
                cp = _landing_copy(src, lnd, ssem, rsem, me, a, k, gather, False)
                cp.wait_send()
                cp.wait_recv()

    hbm = lambda t: pltpu.HBM(t.shape, t.dtype)
    res = pl.pallas_call(
        body, name=name, out_shape=tuple(hbm(t) for t in ins + lands),
        in_specs=[_HBM] * (2 * n) + [_SEM, _SEM, pl.BlockSpec(memory_space=pl.ANY)], out_specs=tuple([_HBM] * (2 * n)),
        input_output_aliases={i: i for i in range(2 * n)},
        compiler_params=pltpu.CompilerParams(has_side_effects=_EFFECT),
    )(*ins, *lands, send_sems, recv_sems, after)
    return list(res[n:])


def _adamw_math(w, g, m, v):
    m = ADAM_B1 * m + (1.0 - ADAM_B1) * g
    v = ADAM_B2 * v + (1.0 - ADAM_B2) * (g * g)
    m_hat = m / (1.0 - ADAM_B1 ** ADAM_STEP)
    v_hat = v / (1.0 - ADAM_B2 ** ADAM_STEP)
    return -ADAM_LR * (m_hat / (jnp.sqrt(v_hat) + ADAM_EPS) + ADAM_WD * w), m, v


def _sum8(parts, *, name):
    _, R, C = parts.shape
    tr = _pick(R, 256, 16)

    def body(p_ref, o_ref):
        acc = p_ref[0].astype(F32)
        for r in range(1, N_DEV):
            acc = acc + p_ref[r].astype(F32)
        o_ref[...] = acc

    return pl.pallas_call(
        body, name=name, grid=(R // tr,), in_specs=[pl.BlockSpec((N_DEV, tr, C), lambda i: (0, i, 0))],
        out_specs=pl.BlockSpec((tr, C), lambda i: (i, 0)), out_shape=jax.ShapeDtypeStruct((R, C), F32),
        compiler_params=pltpu.CompilerParams(dimension_semantics=("parallel",)),
    )(parts)


def _adamw(w, g, m, v, *, name):
    R, C = w.shape
    tr = _pick(R, 256, 8)

    def body(w_ref, g_ref, m_ref, v_ref, d_ref, nm_ref, nv_ref):
        d, nm, nv = _adamw_math(w_ref[...], g_ref[...], m_ref[...], v_ref[...])
        d_ref[...] = d
        nm_ref[...] = nm
        nv_ref[...] = nv

    row = pl.BlockSpec((tr, C), lambda i: (i, 0))
    return pl.pallas_call(
        body, name=name, grid=(R // tr,), in_specs=[row] * 4, out_specs=[row] * 3,
        out_shape=[jax.ShapeDtypeStruct((R, C), F32)] * 3,
        compiler_params=pltpu.CompilerParams(dimension_semantics=("parallel",)),
    )(w, g, m, v)


def _adamw_stacked(gs, w, m, v, *, name):
    depth, R, C = w.shape
    tr = _pick(R, 64, 16)
    summed = gs[0].ndim == 3

    def body(*refs):
        g_refs, (w_ref, m_ref, v_ref), outs = refs[:depth], refs[depth:depth + 3], refs[depth + 3:]
        for l in range(depth):
            if summed:
                g = g_refs[l][0].astype(F32)
                for r in range(1, N_DEV):
                    g = g + g_refs[l][r].astype(F32)
            else:
                g = g_refs[l][...]
            d, nm, nv = _adamw_math(w_ref[l], g, m_ref[l], v_ref[l])
            for o, val in zip(outs, (g, d, nm, nv)):
                o[l] = val

    g_spec = pl.BlockSpec((N_DEV, tr, C), lambda i: (0, i, 0)) if summed else pl.BlockSpec((tr, C), lambda i: (i, 0))
    stk = pl.BlockSpec((depth, tr, C), lambda i: (0, i, 0))
    return pl.pallas_call(
        body, name=name, grid=(R // tr,), in_specs=[g_spec] * depth + [stk] * 3, out_specs=[stk] * 4,
        out_shape=[jax.ShapeDtypeStruct((depth, R, C), F32)] * 4,
        compiler_params=pltpu.CompilerParams(dimension_semantics=("parallel",)),
    )(*gs, w, m, v)


def _pack(arrs):
    flat = jnp.concatenate([a.reshape(-1).astype(F32) for a in arrs])
    pad = (-flat.shape[0]) % 1024
    return jnp.pad(flat, (0, pad)).reshape(-1, 128)


def _unpack(buf, shapes):
    flat, out, off = buf.reshape(-1), [], 0
    for s in shapes:
        n = math.prod(s)
        out.append(flat[off:off + n].reshape(s))
        off += n
    return out


def _s5_params(lam_re, lam_im, log_step, b_re, b_im):
    G, P, H = b_re.shape
    step = jnp.exp(log_step)[:, None]
    mag = jnp.exp(lam_re * step)
    ar, ai = mag * jnp.cos(lam_im * step), mag * jnp.sin(lam_im * step)
    den = lam_re * lam_re + lam_im * lam_im
    qr = ((ar - 1.0) * lam_re + ai * lam_im) / den
    qi = (ai * lam_re - (ar - 1.0) * lam_im) / den
    bbr = qr[..., None] * b_re - qi[..., None] * b_im
    bbi = qr[..., None] * b_im + qi[..., None] * b_re
    return ar.reshape(-1), ai.reshape(-1), _s5_blocks_in(bbr), _s5_blocks_in(bbi)


def _s5_blocks_in(b):
    G, P, H = b.shape
    eye = jnp.eye(8, dtype=b.dtype)
    return jnp.einsum("kgph,gj->kghjp", b.reshape(G // 8, 8, P, H), eye).reshape(G // 8, 8 * H, 8 * P)


def _s5_blocks_in_diag(m, G, P, H):
    return jnp.einsum("kghgp->kgph", m.reshape(G // 8, 8, H, 8, P)).reshape(G, P, H)


def _s5_blocks_out(c):
    G, H, P = c.shape
    eye = jnp.eye(8, dtype=c.dtype)
    return jnp.einsum("kghp,gj->kgpjh", c.reshape(G // 8, 8, H, P), eye).reshape(G // 8, 8 * P, 8 * H)


def _s5_blocks_out_diag(m, G, H, P):
    return jnp.einsum("kgpgh->kghp", m.reshape(G // 8, 8, P, 8, H)).reshape(G, H, P)


def _head_blocks(w):
    nh, d, e = w.shape
    return jnp.einsum("hde,hk->hdke", w, jnp.eye(nh, dtype=w.dtype)).reshape(nh * d, nh * e)


def _head_blocks_diag(m, nh, d, e):
    return jnp.einsum("hdhe->hde", m.reshape(nh, d, nh, e))


def _pad_rows(w, rows):
    return jnp.pad(w, ((0, rows - w.shape[0]), (0, 0)))


def _unshard_cols(g):
    return jnp.transpose(g, (1, 0, 2)).reshape(g.shape[1], -1)


_NAMES = ['norm_mix_g', 'w_in', 's5_lam_re', 's5_lam_im', 's5_log_step', 's5_b_re', 's5_b_im', 's5_c_re', 's5_c_im', 's5_d',
          's5_w_glu', 's5_b_glu', 'cv_w_dw', 'cv_b_dw', 'cv_ln_g', 'cv_ln_b', 'cv_w_pw', 'cv_b_pw', 'lru_w_conv', 'lru_b_conv',
          'lru_w_r', 'lru_b_r', 'lru_w_i', 'lru_b_i', 'lru_lam', 'pool_w', 'pool_scale', 'w_out', 'norm_ffn_g', 'ffn_w_up',
          'ffn_w_dw', 'ffn_b_dw', 'ffn_w_down', 'norm_final_g']
_MATRIX = ('w_in', 'w_out', 'ffn_w_up', 'ffn_w_down', 's5_w_glu', 'cv_w_pw')
_TRANSPOSED = ('w_in', 'ffn_w_up')
_COLSHARD = ('cv_w_dw', 'lru_w_conv', 'ffn_w_dw')
_GROUPS = (('w_in', 's5_w_glu', 'cv_w_pw', 'w_out'), ('ffn_w_up', 'ffn_w_down'))


def kernel(x, norm_mix_g, w_in, s5_lam_re, s5_lam_im, s5_log_step, s5_b_re, s5_b_im, s5_c_re, s5_c_im, s5_d, s5_w_glu, s5_b_glu, cv_w_dw, cv_b_dw, cv_ln_g, cv_ln_b, cv_w_pw, cv_b_pw, lru_w_conv, lru_b_conv, lru_w_r, lru_b_r, lru_w_i, lru_b_i, lru_lam, pool_w, pool_scale, w_out, norm_ffn_g, ffn_w_up, ffn_w_dw, ffn_b_dw, ffn_w_down, norm_final_g, loss_target, m_norm_mix_g, m_w_in, m_s5_lam_re, m_s5_lam_im, m_s5_log_step, m_s5_b_re, m_s5_b_im, m_s5_c_re, m_s5_c_im, m_s5_d, m_s5_w_glu, m_s5_b_glu, m_cv_w_dw, m_cv_b_dw, m_cv_ln_g, m_cv_ln_b, m_cv_w_pw, m_cv_b_pw, m_lru_w_conv, m_lru_b_conv, m_lru_w_r, m_lru_b_r, m_lru_w_i, m_lru_b_i, m_lru_lam, m_pool_w, m_pool_scale, m_w_out, m_norm_ffn_g, m_ffn_w_up, m_ffn_w_dw, m_ffn_b_dw, m_ffn_w_down, m_norm_final_g, v_norm_mix_g, v_w_in, v_s5_lam_re, v_s5_lam_im, v_s5_log_step, v_s5_b_re, v_s5_b_im, v_s5_c_re, v_s5_c_im, v_s5_d, v_s5_w_glu, v_s5_b_glu, v_cv_w_dw, v_cv_b_dw, v_cv_ln_g, v_cv_ln_b, v_cv_w_pw, v_cv_b_pw, v_lru_w_conv, v_lru_b_conv, v_lru_w_r, v_lru_b_r, v_lru_w_i, v_lru_b_i, v_lru_lam, v_pool_w, v_pool_scale, v_w_out, v_norm_ffn_g, v_ffn_w_up, v_ffn_w_dw, v_ffn_b_dw, v_ffn_w_down, v_norm_final_g):
    args = (x, norm_mix_g, w_in, s5_lam_re, s5_lam_im, s5_log_step, s5_b_re, s5_b_im, s5_c_re, s5_c_im, s5_d, s5_w_glu, s5_b_glu, cv_w_dw, cv_b_dw, cv_ln_g, cv_ln_b, cv_w_pw, cv_b_pw, lru_w_conv, lru_b_conv, lru_w_r, lru_b_r, lru_w_i, lru_b_i, lru_lam, pool_w, pool_scale, w_out, norm_ffn_g, ffn_w_up, ffn_w_dw, ffn_b_dw, ffn_w_down, norm_final_g, loss_target, m_norm_mix_g, m_w_in, m_s5_lam_re, m_s5_lam_im, m_s5_log_step, m_s5_b_re, m_s5_b_im, m_s5_c_re, m_s5_c_im, m_s5_d, m_s5_w_glu, m_s5_b_glu, m_cv_w_dw, m_cv_b_dw, m_cv_ln_g, m_cv_ln_b, m_cv_w_pw, m_cv_b_pw, m_lru_w_conv, m_lru_b_conv, m_lru_w_r, m_lru_b_r, m_lru_w_i, m_lru_b_i, m_lru_lam, m_pool_w, m_pool_scale, m_w_out, m_norm_ffn_g, m_ffn_w_up, m_ffn_w_dw, m_ffn_b_dw, m_ffn_w_down, m_norm_final_g, v_norm_mix_g, v_w_in, v_s5_lam_re, v_s5_lam_im, v_s5_log_step, v_s5_b_re, v_s5_b_im, v_s5_c_re, v_s5_c_im, v_s5_d, v_s5_w_glu, v_s5_b_glu, v_cv_w_dw, v_cv_b_dw, v_cv_ln_g, v_cv_ln_b, v_cv_w_pw, v_cv_b_pw, v_lru_w_conv, v_lru_b_conv, v_lru_w_r, v_lru_b_r, v_lru_w_i, v_lru_b_i, v_lru_lam, v_pool_w, v_pool_scale, v_w_out, v_norm_ffn_g, v_ffn_w_up, v_ffn_w_dw, v_ffn_b_dw, v_ffn_w_down, v_norm_final_g)
    x, target = args[0], args[35]
    W = dict(zip(_NAMES, args[1:35]))
    Mo = dict(zip(_NAMES, args[36:70]))
    Vo = dict(zip(_NAMES, args[70:104]))
    x = x[0]
    target = target[0]
    T, D = x.shape
    depth = W['w_in'].shape[0]
    G, P, H = W['s5_b_re'].shape[1:]
    MW = G * H
    nh, hd, _ = W['lru_w_r'].shape[1:]
    me = _rank(_my_pos())

    gather, started = {}, jnp.zeros((), F32)
    for l in range(depth):
        for gi, names in enumerate(_GROUPS):
            shards = [(W[n][l].T if n in _TRANSPOSED else W[n][l]).astype(BF16) for n in names]
            gather[l, gi], tok = _exchange_start(shards, "gather_start_%d%d" % (l, gi), True)
            started = started + tok[0, 0]
    full = {}

    def land_weights(l, gi, after):
        got = _exchange_wait(gather[l, gi], after, "gather_wait_%d%d" % (l, gi), True)
        for n, g8 in zip(_GROUPS[gi], got):
            full[n, l] = g8.reshape(-1, g8.shape[-1])

    tap_shapes = [W[n][l].shape for l in range(depth) for n in _COLSHARD]
    taps8 = _exchange([_pack([W[n][l] for l in range(depth) for n in _COLSHARD])], "gather_taps", True)[0]
    taps = [_unshard_cols(jnp.stack(t)) for t in zip(*[_unpack(taps8[r], tap_shapes) for r in range(N_DEV)])]
    for l in range(depth):
        for j, n in enumerate(_COLSHARD):
            full[n, l] = taps[l * len(_COLSHARD) + j]

    saved = []
    for l in range(depth):
        g_mix = W['norm_mix_g'][l] + started
        land_weights(l, 0, x if l else g_mix)
        s5p, s5p_vjp = jax.vjp(_s5_params, W['s5_lam_re'][l], W['s5_lam_im'][l], W['s5_log_step'][l], W['s5_b_re'][l], W['s5_b_im'][l])
        ar, ai, bre, bim = s5p
        cre, cim = _s5_blocks_out(W['s5_c_re'][l]), _s5_blocks_out(W['s5_c_im'][l])
        sp8, sp8_vjp = jax.vjp(lambda lam: LRU_C * jax.nn.softplus(-lam), W['lru_lam'][l])
        wr, wi = _head_blocks(W['lru_w_r'][l]), _head_blocks(W['lru_w_i'][l])
        c = dict(bre=bre.astype(BF16), bim=bim.astype(BF16), cre=cre.astype(BF16), cim=cim.astype(BF16),
                 cst_f=_s5_scan_consts(ar, ai, False), cst_b=_s5_scan_consts(ar, ai, True), sp8=sp8, wr=wr.astype(BF16),
                 wi=wi.astype(BF16), cvw=_pad_rows(full['cv_w_dw', l], 32), lruw=_pad_rows(full['lru_w_conv', l], 8),
                 ffw=_pad_rows(full['ffn_w_dw', l], 8), pw=W['pool_w'][l].astype(BF16), s5p_vjp=s5p_vjp, sp8_vjp=sp8_vjp, x0=x)
        proj, h = _norm_matmul(x, g_mix, full['w_in', l], tm=1024, tn=512, name="in_proj")
        y0, sre, sim, yp, z = _s5_fwd(proj, 0, c['bre'], c['bim'], c['cre'], c['cim'], c['cst_f'], W['s5_d'][l],
                                     full['s5_w_glu', l], W['s5_b_glu'][l])
        y1, h1 = _cv_fwd(proj, 1, 2, c['cvw'], W['cv_b_dw'][l], W['cv_ln_g'][l], W['cv_ln_b'][l], full['cv_w_pw', l], W['cv_b_pw'][l])
        y2, xc, rr, ig, hprev = _lru_fwd(proj, 3, 4, c['lruw'], W['lru_b_conv'][l], c['wr'], W['lru_b_r'][l], c['wi'],
                                         W['lru_b_i'][l], sp8)
        y3 = _pool_fwd(proj, 5, c['pw'], W['pool_scale'][l])
        mixed = jnp.concatenate([y0, y1, y2, y3], axis=1)
        x1 = _matmul(mixed, full['w_out', l], mode="nn", tm=1024, tn=512, tk=2048, out_dtype=F32, name="out_proj", add=x)
        land_weights(l, 1, x1)
        up, h2 = _norm_matmul(x1, W['norm_ffn_g'][l], full['ffn_w_up', l], tm=1024, tn=256, name="ffn_up")
        act = _ffn_mid_fwd(up, c['ffw'], W['ffn_b_dw'][l])
        F = act.shape[1]
        x = _matmul(act, full['ffn_w_down', l], mode="nn", tm=512, tn=512, tk=F, out_dtype=F32, name="ffn_down", add=x1)
        c.update(proj=proj, h=h, sre=sre, sim=sim, yp=yp, z=z, h1=h1, xc=xc, rr=rr, ig=ig, hprev=hprev, mixed=mixed, x1=x1,
                 up=up, h2=h2, act=act)
        saved.append(c)

    dx, dxb, dgf, se = _loss_head(x, W['norm_final_g'], target)
    loss = lax.psum(0.5 / D * jnp.sum(se), AXES)
    gsmall = {('norm_final_g', None): dgf.sum(0)}
    gmat, scatter = {}, {}

    def parts_of(m):
        return m.reshape(N_DEV, m.shape[0] // N_DEV, m.shape[1])

    for l in reversed(range(depth)):
        c = saved[l]
        F = c['act'].shape[1]
        dact = _matmul(dxb, full['ffn_w_down', l], mode="nt", tm=256, tn=F, tk=512, out_dtype=F32, name="d_act")
        gmat['ffn_w_down', l] = _matmul(c['act'], dxb, mode="tn", tm=F, tn=512, tk=256, out_dtype=BF16, name="dw_down")
        dup, dffw, dffb = _ffn_mid_bwd(dact, c['up'], c['ffw'], W['ffn_b_dw'][l])
        gsmall['ffn_w_dw', l] = dffw.sum(1)[:3]
        gsmall['ffn_b_dw', l] = dffb.sum(0)
        dh2 = _matmul(dup, full['ffn_w_up', l], mode="nn", tm=512, tn=512, tk=F, out_dtype=F32, name="d_h2")
        gmat['ffn_w_up', l] = _matmul(dup, c['h2'], mode="tn", tm=256, tn=D, tk=T, out_dtype=BF16, name="dw_up")
        scatter[l, 1], tok = _exchange_start([parts_of(gmat[n, l]) for n in _GROUPS[1]], "scatter_start_%d1" % l, False)
        dx1, dx1b, dg2 = _rms_bwd(dx, dh2, c['x1'], W['norm_ffn_g'][l] + tok[0, 0], name="ffn_norm_bwd")
        gsmall['norm_ffn_g', l] = dg2.sum(0)
        dmix = _matmul(dx1b, full['w_out', l], mode="nt", tm=1024, tn=512, tk=2048, out_dtype=F32, name="d_mixed")
        gmat['w_out', l] = _matmul(c['mixed'], dx1b, mode="tn", tm=512, tn=D, tk=2048, out_dtype=BF16, name="dw_out")
        du, dbre, dbim, dcre, dcim, da, dwglu, v3 = _s5_bwd(dmix, 0, c['proj'], 0, c['sre'], c['sim'], c['yp'], c['z'], c['bre'],
                                                            c['bim'], c['cre'], c['cim'], c['cst_b'], W['s5_d'][l], full['s5_w_glu', l])
        da = da.sum(1)
        glr, gli, gls, gbr, gbi = c['s5p_vjp']((da[0], da[1], dbre, dbim))
        gsmall['s5_lam_re', l], gsmall['s5_lam_im', l], gsmall['s5_log_step', l] = glr, gli, gls
        gsmall['s5_b_re', l], gsmall['s5_b_im', l] = gbr, gbi
        gsmall['s5_c_re', l] = _s5_blocks_out_diag(dcre, G, H, P)
        gsmall['s5_c_im', l] = _s5_blocks_out_diag(dcim, G, H, P)
        gsmall['s5_b_glu', l], gsmall['s5_d', l] = v3[0].sum(0), v3[1].sum(0)
        gmat['s5_w_glu', l] = dwglu.astype(BF16)
        dv, dg, dwpw, dcvw, v5 = _cv_bwd(dmix, 1, c['proj'], 1, 2, c['h1'], c['cvw'], W['cv_ln_g'][l], W['cv_ln_b'][l], full['cv_w_pw', l])
        gmat['cv_w_pw', l] = dwpw.astype(BF16)
        gsmall['cv_w_dw', l] = dcvw.sum(1)[:31]
        gsmall['cv_b_pw', l], gsmall['cv_ln_g', l], gsmall['cv_ln_b', l], gsmall['cv_b_dw', l] = [v5[j].sum(0) for j in range(4)]
        dlx, dlg, dwr, dwi, dlw, v4 = _lru_bwd(dmix, 2, c['proj'], 3, 4, c['xc'], c['rr'], c['ig'], c['hprev'], c['lruw'], c['wr'],
                                               c['wi'], c['sp8'])
        gsmall['lru_w_r', l], gsmall['lru_w_i', l] = _head_blocks_diag(dwr, nh, hd, hd), _head_blocks_diag(dwi, nh, hd, hd)
        gsmall['lru_w_conv', l] = dlw.sum(1)[:4]
        gsmall['lru_b_r', l], gsmall['lru_b_i', l] = v4[0].sum(0), v4[1].sum(0)
        gsmall['lru_lam', l] = c['sp8_vjp'](v4[2].sum(0))[0]
        gsmall['lru_b_conv', l] = v4[3].sum(0)
        dpx, dpw, dps = _pool_bwd(dmix, 3, c['proj'], 5, c['pw'], W['pool_scale'][l])
        gsmall['pool_w', l], gsmall['pool_scale', l] = dpw, dps.sum(0)
        dproj = jnp.concatenate([du, dv, dg, dlx, dlg, dpx], axis=1)
        dh = _matmul(dproj, full['w_in', l], mode="nn", tm=1024, tn=512, tk=1024, out_dtype=F32, name="d_h")
        gmat['w_in', l] = _matmul(dproj, c['h'], mode="tn", tm=512, tn=D, tk=2048, out_dtype=BF16, name="dw_in")
        scatter[l, 0], tok = _exchange_start([parts_of(gmat[n, l]) for n in _GROUPS[0]], "scatter_start_%d0" % l, False)
        dx, dxb, dg1 = _rms_bwd(dx1, dh, c['x0'], W['norm_mix_g'][l] + tok[0, 0], name="mix_norm_bwd")
        gsmall['norm_mix_g', l] = dg1.sum(0)

    small_names = [n for n in _NAMES if n not in _MATRIX]
    small_parts = [gsmall[n, l] for n in small_names for l in ((None,) if n == 'norm_final_g' else range(depth))]
    full_shapes = [gsmall[n, None].shape if n == 'norm_final_g' else (depth,) + gsmall[n, 0].shape for n in small_names]
    small8 = _exchange([_pack(small_parts)], "gather_small_grads", True)[0]
    small_sum = _sum8(small8, name="sum_small_grads")
    gs = dict(zip(small_names, _unpack(small_sum, full_shapes)))
    parts = {}
    for l in reversed(range(depth)):
        for gi in (1, 0):
            got = _exchange_wait(scatter[l, gi], small_sum, "scatter_wait_%d%d" % (l, gi), False)
            for n, p8 in zip(_GROUPS[gi], got):
                parts[n, l] = p8

    out = {}
    for n in _MATRIX:
        if n in _TRANSPOSED:
            gl = [_sum8(parts[n, l], name="sum_" + n).T for l in range(depth)]
        else:
            gl = [parts[n, l] for l in range(depth)]
        out[n] = _adamw_stacked(gl, W[n], Mo[n], Vo[n], name="adamw_" + n)
    for n in _COLSHARD:
        cw = W[n].shape[-1]
        gs[n] = lax.dynamic_slice_in_dim(gs[n], me * cw, cw, axis=gs[n].ndim - 1)
    upd = _adamw(_pack([W[n] for n in small_names]), _pack([gs[n] for n in small_names]), _pack([Mo[n] for n in small_names]),
                 _pack([Vo[n] for n in small_names]), name="adamw_small")
    upd = [_unpack(u, [gs[n].shape for n in small_names]) for u in upd]
    for j, n in enumerate(small_names):
        out[n] = (gs[n], upd[0][j], upd[1][j], upd[2][j])
    return (loss, dx[None]) + tuple(out[n][q] for q in range(4) for n in _NAMES)
```

```python
import functools
import math

import jax
import jax.numpy as jnp
from jax import lax
from jax.experimental import pallas as pl
from jax.experimental.pallas import tpu as pltpu

F32 = jnp.float32
BF16 = jnp.bfloat16
N_DEV = 8
AXES = ("x", "y", "c")
EPS = 1e-6
S5_GROUP_CH = 16
S5_STATE = 64
LRU_HEADS = 8
LRU_C = 8.0
POOL_WINDOWS = (2, 4, 8, 16)
CONV_HALO = 32
SMALL_HALO = 8
POOL_HALO = 16
LANES = 128
ADAM_LR, ADAM_B1, ADAM_B2, ADAM_EPS, ADAM_WD, ADAM_STEP = 0.001, 0.9, 0.999, 1e-08, 0.01, 10
_GELU_K = math.sqrt(2.0 / math.pi)
ARB = pltpu.CompilerParams(dimension_semantics=("arbitrary",))


def _pick(n, pref, mult):
    best = None
    for t in range(mult, min(n, pref) + 1, mult):
        if n % t == 0:
            best = t
    return best if best is not None else n


def _sigmoid(x):
    return 1.0 / (1.0 + jnp.exp(-x))


def _gelu(x):
    return 0.5 * x * (1.0 + jnp.tanh(_GELU_K * (x + 0.044715 * x * x * x)))


def _gelu_grad(x):
    t = jnp.tanh(_GELU_K * (x + 0.044715 * x * x * x))
    return 0.5 * (1.0 + t) + 0.5 * x * (1.0 - t * t) * _GELU_K * (1.0 + 3.0 * 0.044715 * x * x)


def _gelu_and_grad(x):
    t = jnp.tanh(_GELU_K * (x + 0.044715 * x * x * x))
    half = 0.5 * (1.0 + t)
    return x * half, half + 0.5 * x * (1.0 - t * t) * _GELU_K * (1.0 + 3.0 * 0.044715 * x * x)


def _neg_expm1(x):
    series = -x * (1.0 + x * (0.5 + x * (1.0 / 6.0 + x * (1.0 / 24.0 + x * (1.0 / 120.0 + x * (1.0 / 720.0))))))
    return jnp.where(jnp.abs(x) < 0.25, series, 1.0 - jnp.exp(x))


def _fold8(x):
    return x.reshape(x.shape[0] // 8, 8, x.shape[1]).sum(axis=0)


def _dot(a, b):
    return jnp.dot(a.astype(BF16), b.astype(BF16), preferred_element_type=F32)


def _dot_nt(a, b):
    return lax.dot_general(a.astype(BF16), b.astype(BF16), (((1,), (1,)), ((), ())), preferred_element_type=F32)


def _dot_tn(a, b):
    return lax.dot_general(a.astype(BF16), b.astype(BF16), (((0,), (0,)), ((), ())), preferred_element_type=F32)


def _full(shape):
    nd = len(shape)
    return pl.BlockSpec(shape, lambda *i: (0,) * nd)


def _matmul(a, b, *, mode, tm, tn, tk, out_dtype, name, add=None):
    if mode == "nn":
        (M, K), N = a.shape, b.shape[1]
    elif mode == "nt":
        (M, K), N = a.shape, b.shape[0]
    else:
        (K, M), N = a.shape, b.shape[1]
    tm, tn, tk = _pick(M, tm, 8), _pick(N, tn, 128), _pick(K, tk, 128)
    if mode == "tn":
        tm = _pick(M, tm, 128)
    nk = K // tk
    a_spec = {"nn": pl.BlockSpec((tm, tk), lambda i, j, k: (i, k)), "nt": pl.BlockSpec((tm, tk), lambda i, j, k: (i, k)),
              "tn": pl.BlockSpec((tk, tm), lambda i, j, k: (k, i))}[mode]
    b_spec = {"nn": pl.BlockSpec((tk, tn), lambda i, j, k: (k, j)), "nt": pl.BlockSpec((tn, tk), lambda i, j, k: (j, k)),
              "tn": pl.BlockSpec((tk, tn), lambda i, j, k: (k, j))}[mode]
    o_spec = pl.BlockSpec((tm, tn), lambda i, j, k: (i, j))
    dot = {"nn": _dot, "nt": _dot_nt, "tn": _dot_tn}[mode]
    has_add = add is not None

    def body(*refs):
        if has_add:
            a_ref, b_ref, add_ref, o_ref, acc_ref = refs
        else:
            a_ref, b_ref, o_ref, acc_ref = refs
        k = pl.program_id(2)

        @pl.when(k == 0)
        def _():
            acc_ref[...] = jnp.zeros_like(acc_ref)

        acc_ref[...] += dot(a_ref[...], b_ref[...])

        @pl.when(k == nk - 1)
        def _():
            r = acc_ref[...]
            if has_add:
                r = r + add_ref[...]
            o_ref[...] = r.astype(out_dtype)

    ins = [a, b] + ([add] if has_add else [])
    specs = [a_spec, b_spec] + ([o_spec] if has_add else [])
    return pl.pallas_call(
        body, name=name, grid=(M // tm, N // tn, nk), in_specs=specs, out_specs=o_spec,
        out_shape=jax.ShapeDtypeStruct((M, N), out_dtype), scratch_shapes=[pltpu.VMEM((tm, tn), F32)],
        compiler_params=pltpu.CompilerParams(dimension_semantics=("parallel", "parallel", "arbitrary")),
    )(*ins)


def _norm_matmul(x, g, wt, *, tm, tn, name):
    T, D = x.shape
    N = wt.shape[0]
    tm, tn = _pick(T, tm, 8), _pick(N, tn, 128)

    def body(x_ref, g_ref, w_ref, o_ref, h_ref):
        @pl.when(pl.program_id(1) == 0)
        def _():
            xv = x_ref[...]
            r = lax.rsqrt(jnp.mean(xv * xv, axis=-1, keepdims=True) + EPS)
            h_ref[...] = (xv * r * g_ref[...]).astype(BF16)

        o_ref[...] = _dot_nt(h_ref[...], w_ref[...])

    return pl.pallas_call(
        body, name=name, grid=(T // tm, N // tn),
        in_specs=[pl.BlockSpec((tm, D), lambda i, j: (i, 0)), pl.BlockSpec((1, D), lambda i, j: (0, 0)),
                  pl.BlockSpec((tn, D), lambda i, j: (j, 0))],
        out_specs=[pl.BlockSpec((tm, tn), lambda i, j: (i, j)), pl.BlockSpec((tm, D), lambda i, j: (i, 0))],
        out_shape=[jax.ShapeDtypeStruct((T, N), F32), jax.ShapeDtypeStruct((T, D), BF16)],
        compiler_params=pltpu.CompilerParams(dimension_semantics=("parallel", "arbitrary")),
    )(x, g.reshape(1, D), wt)


def _rms_bwd(dres, dh, x, g, *, name):
    T, D = x.shape
    tc = _pick(T, 256, 8)

    def body(dres_ref, dh_ref, x_ref, g_ref, dx_ref, dxb_ref, dg_ref):
        @pl.when(pl.program_id(0) == 0)
        def _():
            dg_ref[...] = jnp.zeros_like(dg_ref)

        xv, dhv = x_ref[...], dh_ref[...]
        r = lax.rsqrt(jnp.mean(xv * xv, axis=-1, keepdims=True) + EPS)
        dg_ref[...] += _fold8(dhv * xv * r)
        dyg = dhv * g_ref[...]
        dx = dres_ref[...] + r * dyg - xv * (r * r * r) * jnp.mean(dyg * xv, axis=-1, keepdims=True)
        dx_ref[...] = dx
        dxb_ref[...] = dx.astype(BF16)

    row = pl.BlockSpec((tc, D), lambda i: (i, 0))
    return pl.pallas_call(
        body, name=name, grid=(T // tc,), in_specs=[row, row, row, _full((1, D))],
        out_specs=[row, row, _full((8, D))],
        out_shape=[jax.ShapeDtypeStruct((T, D), F32), jax.ShapeDtypeStruct((T, D), BF16), jax.ShapeDtypeStruct((8, D), F32)],
        compiler_params=ARB,
    )(dres, dh, x, g.reshape(1, D))


def _loss_head(x, g, target):
    T, D = x.shape
    tc = _pick(T, 256, 8)

    def body(x_ref, g_ref, t_ref, dx_ref, dxb_ref, dg_ref, se_ref):
        @pl.when(pl.program_id(0) == 0)
        def _():
            dg_ref[...] = jnp.zeros_like(dg_ref)
            se_ref[...] = jnp.zeros_like(se_ref)

        xv = x_ref[...]
        r = lax.rsqrt(jnp.mean(xv * xv, axis=-1, keepdims=True) + EPS)
        gv = g_ref[...]
        err = xv * r * gv - t_ref[...]
        se_ref[...] += _fold8(err * err)
        dy = err * (1.0 / D)
        dg_ref[...] += _fold8(dy * xv * r)
        dyg = dy * gv
        dx = r * dyg - xv * (r * r * r) * jnp.mean(dyg * xv, axis=-1, keepdims=True)
        dx_ref[...] = dx
        dxb_ref[...] = dx.astype(BF16)

    row = pl.BlockSpec((tc, D), lambda i: (i, 0))
    return pl.pallas_call(
        body, name="loss_head", grid=(T // tc,), in_specs=[row, _full((1, D)), row],
        out_specs=[row, row, _full((8, D)), _full((8, D))],
        out_shape=[jax.ShapeDtypeStruct((T, D), F32), jax.ShapeDtypeStruct((T, D), BF16),
                   jax.ShapeDtypeStruct((8, D), F32), jax.ShapeDtypeStruct((8, D), F32)],
        compiler_params=ARB,
    )(x, g.reshape(1, D), target)


def _s5_scan_consts(ar, ai, reverse):
    if reverse:
        ai = -ai
    pr, pi_ = [ar], [ai]
    for _ in range(7):
        pr, pi_ = pr + [pr[-1] * ar - pi_[-1] * ai], pi_ + [pr[-1] * ai + pi_[-1] * ar]
    rows = jnp.arange(8)[:, None]
    planes = []
    for d in (1, 2, 4):
        mask = (rows <= 7 - d) if reverse else (rows >= d)
        planes += [jnp.where(mask, pr[d - 1][None, :], 0.0), jnp.where(mask, pi_[d - 1][None, :], 0.0)]
    order = list(range(7, -1, -1)) if reverse else list(range(8))
    planes += [jnp.stack([pr[k] for k in order]), jnp.stack([pi_[k] for k in order])]
    return jnp.stack(planes).astype(F32)


def _s5_block_scan(xr_ref, xi_ref, cst_ref, car_ref, nblk, reverse, lb, extra=None):
    ns = xr_ref.shape[1]
    crow = 0 if reverse else 7

    def blk(i, _):
        ib = (nblk - 1 - i) if reverse else i
        r0 = pl.multiple_of(ib * 8, 8)
        for k in range(ns // lb):
            sl = slice(k * lb, (k + 1) * lb)
            xr, xi = xr_ref[pl.ds(r0, 8), sl], xi_ref[pl.ds(r0, 8), sl]
            for n, d in enumerate((1, 2, 4)):
                ar, ai = cst_ref[2 * n, :, sl], cst_ref[2 * n + 1, :, sl]
                sh = (8 - d) if reverse else d
                rr, ri = pltpu.roll(xr, sh, 0), pltpu.roll(xi, sh, 0)
                xr, xi = xr + ar * rr - ai * ri, xi + ar * ri + ai * rr
            pr, pi_ = cst_ref[6, :, sl], cst_ref[7, :, sl]
            cr, ci = car_ref[0, crow:crow + 1, sl], car_ref[1, crow:crow + 1, sl]
            xr, xi = xr + pr * cr - pi_ * ci, xi + pr * ci + pi_ * cr
            xr_ref[pl.ds(r0, 8), sl] = xr
            xi_ref[pl.ds(r0, 8), sl] = xi
            if extra is not None:
                extra(r0, sl, xr, xi, cr, ci)
            car_ref[0, :, sl] = xr
            car_ref[1, :, sl] = xi
        return 0

    lax.fori_loop(0, nblk, blk, 0)


def _s5_fwd(proj, col, bre, bim, cre, cim, cst, dvec, wglu, bglu):
    T = proj.shape[0]
    nsb, cw, sw = bre.shape
    M, NS = nsb * cw, nsb * sw
    tc = _pick(T, 256, 8)

    def body(u_ref, bre_ref, bim_ref, cre_ref, cim_ref, cst_ref, d_ref, wg_ref, bg_ref,
             y_ref, sre_ref, sim_ref, yp_ref, z_ref, car_ref):
        @pl.when(pl.program_id(0) == 0)
        def _():
            car_ref[...] = jnp.zeros_like(car_ref)

        u = u_ref[...]
        ub = u.astype(BF16)
        for k in range(nsb):
            sre_ref[:, k * sw:(k + 1) * sw] = _dot(ub[:, k * cw:(k + 1) * cw], bre_ref[k])
            sim_ref[:, k * sw:(k + 1) * sw] = _dot(ub[:, k * cw:(k + 1) * cw], bim_ref[k])
        _s5_block_scan(sre_ref, sim_ref, cst_ref, car_ref, tc // 8, False, sw)
        for k in range(nsb):
            yk = _dot(sre_ref[:, k * sw:(k + 1) * sw], cre_ref[k]) - _dot(sim_ref[:, k * sw:(k + 1) * sw], cim_ref[k])
            yp_ref[:, k * cw:(k + 1) * cw] = yk + d_ref[:, k * cw:(k + 1) * cw] * u[:, k * cw:(k + 1) * cw]
        gl = _gelu(yp_ref[...])
        z = _dot(gl, wg_ref[...]) + bg_ref[...]
        z_ref[...] = z
        y_ref[...] = (gl * _sigmoid(z)).astype(BF16)

    rowm = pl.BlockSpec((tc, M), lambda i: (i, 0))
    rows = pl.BlockSpec((tc, NS), lambda i: (i, 0))
    return pl.pallas_call(
        body, name="s5_fwd", grid=(T // tc,),
        in_specs=[pl.BlockSpec((tc, M), lambda i: (i, col)), _full(bre.shape), _full(bim.shape), _full(cre.shape),
                  _full(cim.shape), _full(cst.shape), _full((1, M)), _full((M, M)), _full((1, M))],
        out_specs=[rowm, rows, rows, rowm, rowm],
        out_shape=[jax.ShapeDtypeStruct((T, M), BF16), jax.ShapeDtypeStruct((T, NS), F32), jax.ShapeDtypeStruct((T, NS), F32),
                   jax.ShapeDtypeStruct((T, M), F32), jax.ShapeDtypeStruct((T, M), F32)],
        scratch_shapes=[pltpu.VMEM((2, 8, NS), F32)], compiler_params=ARB,
    )(proj, bre, bim, cre, cim, cst, dvec.reshape(1, M), wglu, bglu.reshape(1, M))


def _s5_bwd(dmix, dcol, proj, col, sre, sim, yp, z, bre, bim, cre, cim, cst, dvec, wglu):
    T = proj.shape[0]
    nsb, cw, sw = bre.shape
    M, NS = nsb * cw, nsb * sw
    tc = _pick(T, 256, 8)
    nc = T // tc

    def body(do_ref, u_ref, sre_ref, sim_ref, yp_ref, z_ref, bre_ref, bim_ref, cre_ref, cim_ref, cst_ref, d_ref, wg_ref,
             du_ref, dbre_ref, dbim_ref, dcre_ref, dcim_ref, da_ref, dwg_ref, vec_ref, gre_ref, gim_ref, car_ref):
        @pl.when(pl.program_id(0) == 0)
        def _():
            for r in (dbre_ref, dbim_ref, dcre_ref, dcim_ref, da_ref, dwg_ref, vec_ref, car_ref):
                r[...] = jnp.zeros_like(r)

        u, do, ypv = u_ref[...], do_ref[...], yp_ref[...]
        gl = _gelu(ypv)
        sg = _sigmoid(z_ref[...])
        dz = do * gl * sg * (1.0 - sg)
        dgl = do * sg + _dot_nt(dz, wg_ref[...])
        dwg_ref[...] += _dot_tn(gl, dz)
        vec_ref[0] += _fold8(dz)
        dy = dgl * _gelu_grad(ypv)
        vec_ref[1] += _fold8(dy * u)
        dyb = dy.astype(BF16)
        for k in range(nsb):
            cs, ss = slice(k * cw, (k + 1) * cw), slice(k * sw, (k + 1) * sw)
            gre_ref[:, ss] = _dot_nt(dyb[:, cs], cre_ref[k])
            gim_ref[:, ss] = -_dot_nt(dyb[:, cs], cim_ref[k])
            dcre_ref[k] += _dot_tn(sre_ref[:, ss], dyb[:, cs])
            dcim_ref[k] -= _dot_tn(sim_ref[:, ss], dyb[:, cs])

        row = lax.broadcasted_iota(jnp.int32, (8, sw), 0)

        def extra(r0, sl, gr, gi, cr, ci):
            nr = jnp.where(row == 7, cr, pltpu.roll(gr, 7, 0))
            ni = jnp.where(row == 7, ci, pltpu.roll(gi, 7, 0))
            sr, si = sre_ref[pl.ds(r0, 8), sl], sim_ref[pl.ds(r0, 8), sl]
            da_ref[0, :, sl] += nr * sr + ni * si
            da_ref[1, :, sl] += ni * sr - nr * si

        _s5_block_scan(gre_ref, gim_ref, cst_ref, car_ref, tc // 8, True, sw, extra)
        ub = u.astype(BF16)
        for k in range(nsb):
            cs, ss = slice(k * cw, (k + 1) * cw), slice(k * sw, (k + 1) * sw)
            gr, gi = gre_ref[:, ss].astype(BF16), gim_ref[:, ss].astype(BF16)
            duk = dy[:, cs] * d_ref[:, cs] + _dot_nt(gr, bre_ref[k]) + _dot_nt(gi, bim_ref[k])
            du_ref[:, cs] = duk.astype(BF16)
            dbre_ref[k] += _dot_tn(ub[:, cs], gr)
            dbim_ref[k] += _dot_tn(ub[:, cs], gi)

    def rowm(c):
        return pl.BlockSpec((tc, M), lambda i: (nc - 1 - i, c))
    rows = pl.BlockSpec((tc, NS), lambda i: (nc - 1 - i, 0))
    return pl.pallas_call(
        body, name="s5_bwd", grid=(nc,),
        in_specs=[rowm(dcol), rowm(col), rows, rows, rowm(0), rowm(0), _full(bre.shape), _full(bim.shape), _full(cre.shape),
                  _full(cim.shape), _full(cst.shape), _full((1, M)), _full((M, M))],
        out_specs=[rowm(0), _full(bre.shape), _full(bim.shape), _full(cre.shape), _full(cim.shape), _full((2, 8, NS)),
                   _full((M, M)), _full((3, 8, M))],
        out_shape=[jax.ShapeDtypeStruct((T, M), BF16), jax.ShapeDtypeStruct(bre.shape, F32), jax.ShapeDtypeStruct(bim.shape, F32),
                   jax.ShapeDtypeStruct(cre.shape, F32), jax.ShapeDtypeStruct(cim.shape, F32), jax.ShapeDtypeStruct((2, 8, NS), F32),
                   jax.ShapeDtypeStruct((M, M), F32), jax.ShapeDtypeStruct((3, 8, M), F32)],
        scratch_shapes=[pltpu.VMEM((tc, NS), F32), pltpu.VMEM((tc, NS), F32), pltpu.VMEM((2, 8, NS), F32)],
        compiler_params=ARB,
    )(dmix, proj, sre, sim, yp, z, bre, bim, cre, cim, cst, dvec.reshape(1, M), wglu)


def _ln_silu(h1, lg, lb):
    mu = jnp.mean(h1, axis=-1, keepdims=True)
    xc = h1 - mu
    rstd = lax.rsqrt(jnp.mean(xc * xc, axis=-1, keepdims=True) + EPS)
    xhat = xc * rstd
    h2 = xhat * lg + lb
    return xhat, rstd, h2, _sigmoid(h2)


def _cv_fwd(proj, vcol, gcol, wdw, bdw, lg, lb, wpw, bpw):
    T = proj.shape[0]
    M = wpw.shape[0]
    tc = _pick(T, 256, 8)
    H, K = CONV_HALO, 31

    def body(v_ref, g_ref, wdw_ref, bdw_ref, lg_ref, lb_ref, wpw_ref, bpw_ref, y_ref, h1_ref, buf_ref):
        @pl.when(pl.program_id(0) == 0)
        def _():
            buf_ref[0:H, :] = jnp.zeros((H, M), F32)

        buf_ref[H:H + tc, :] = v_ref[...] * _sigmoid(g_ref[...])
        acc = jnp.zeros((tc, M), F32) + bdw_ref[...]
        for k in range(K):
            acc = acc + wdw_ref[k:k + 1, :] * buf_ref[pl.ds(H - (K - 1) + k, tc), :]
        h1_ref[...] = acc
        _, _, h2, sg = _ln_silu(acc, lg_ref[...], lb_ref[...])
        y_ref[...] = (_dot(h2 * sg, wpw_ref[...]) + bpw_ref[...]).astype(BF16)
        buf_ref[0:H, :] = buf_ref[tc:tc + H, :]

    def rowm(c):
        return pl.BlockSpec((tc, M), lambda i: (i, c))
    vec = _full((1, M))
    return pl.pallas_call(
        body, name="cv_fwd", grid=(T // tc,),
        in_specs=[rowm(vcol), rowm(gcol), _full((32, M)), vec, vec, vec, _full((M, M)), vec],
        out_specs=[rowm(0), rowm(0)],
        out_shape=[jax.ShapeDtypeStruct((T, M), BF16), jax.ShapeDtypeStruct((T, M), F32)],
        scratch_shapes=[pltpu.VMEM((tc + H, M), F32)], compiler_params=ARB,
    )(proj, proj, wdw, bdw.reshape(1, M), lg.reshape(1, M), lb.reshape(1, M), wpw, bpw.reshape(1, M))


def _cv_bwd(dmix, dcol, proj, vcol, gcol, h1, wdw, lg, lb, wpw):
    T = proj.shape[0]
    M = wpw.shape[0]
    tc = _pick(T, 256, 8)
    nc = T // tc
    H, K = CONV_HALO, 31
    hb = tc // H

    def body(do_ref, v_ref, g_ref, vh_ref, gh_ref, h1_ref, wdw_ref, lg_ref, lb_ref, wpw_ref,
             dv_ref, dg_ref, dwpw_ref, dwdw_ref, vec_ref, hbuf_ref, dbuf_ref):
        i = pl.program_id(0)

        @pl.when(i == 0)
        def _():
            for r in (dwpw_ref, dwdw_ref, vec_ref):
                r[...] = jnp.zeros_like(r)
            dbuf_ref[tc:tc + H, :] = jnp.zeros((H, M), F32)

        v, g = v_ref[...], g_ref[...]
        sgg = _sigmoid(g)
        halo = vh_ref[...] * _sigmoid(gh_ref[...])
        hbuf_ref[0:H, :] = jnp.where(i == nc - 1, 0.0, halo)
        hbuf_ref[H:H + tc, :] = v * sgg
        do = do_ref[...]
        xhat, rstd, h2, sg = _ln_silu(h1_ref[...], lg_ref[...], lb_ref[...])
        dwpw_ref[...] += _dot_tn(h2 * sg, do)
        vec_ref[0] += _fold8(do)
        dh2 = _dot_nt(do, wpw_ref[...]) * (sg * (1.0 + h2 * (1.0 - sg)))
        vec_ref[1] += _fold8(dh2 * xhat)
        vec_ref[2] += _fold8(dh2)
        dxh = dh2 * lg_ref[...]
        dh1 = rstd * (dxh - jnp.mean(dxh, axis=-1, keepdims=True) - xhat * jnp.mean(dxh * xhat, axis=-1, keepdims=True))
        vec_ref[3] += _fold8(dh1)
        dbuf_ref[0:tc, :] = dh1
        dh0 = jnp.zeros((tc, M), F32)
        for k in range(K):
            dh0 = dh0 + wdw_ref[k:k + 1, :] * dbuf_ref[pl.ds(K - 1 - k, tc), :]
            dwdw_ref[k] += _fold8(dh1 * hbuf_ref[pl.ds(H - (K - 1) + k, tc), :])
        dv_ref[...] = (dh0 * sgg).astype(BF16)
        dg_ref[...] = (dh0 * v * sgg * (1.0 - sgg)).astype(BF16)
        dbuf_ref[tc:tc + H, :] = dbuf_ref[0:H, :]

    def rowm(c):
        return pl.BlockSpec((tc, M), lambda i: (nc - 1 - i, c))

    def halo(c):
        return pl.BlockSpec((H, M), lambda i: (jnp.maximum((nc - 1 - i) * hb - 1, 0), c))
    vec = _full((1, M))
    return pl.pallas_call(
        body, name="cv_bwd", grid=(nc,),
        in_specs=[rowm(dcol), rowm(vcol), rowm(gcol), halo(vcol), halo(gcol), rowm(0), _full((32, M)), vec, vec, _full((M, M))],
        out_specs=[rowm(0), rowm(0), _full((M, M)), _full((32, 8, M)), _full((5, 8, M))],
        out_shape=[jax.ShapeDtypeStruct((T, M), BF16), jax.ShapeDtypeStruct((T, M), BF16), jax.ShapeDtypeStruct((M, M), F32),
                   jax.ShapeDtypeStruct((32, 8, M), F32), jax.ShapeDtypeStruct((5, 8, M), F32)],
        scratch_shapes=[pltpu.VMEM((tc + H, M), F32), pltpu.VMEM((tc + H, M), F32)], compiler_params=ARB,
    )(dmix, proj, proj, proj, proj, h1, wdw, lg.reshape(1, M), lb.reshape(1, M), wpw)


def _lru_gates(xc, r, ig, sp8):
    la = -r * sp8
    a = jnp.exp(la)
    mult = jnp.sqrt(_neg_expm1(2.0 * la))
    return a, mult, mult * (ig * xc)


def _lru_fwd(proj, xcol, gcol, wcv, bcv, wr, br, wi, bi, sp8):
    T = proj.shape[0]
    M = wr.shape[0]
    tc = _pick(T, 256, 8)
    H, K = SMALL_HALO, 4

    def body(x_ref, g_ref, wcv_ref, bcv_ref, wr_ref, br_ref, wi_ref, bi_ref, sp_ref,
             y_ref, xc_ref, r_ref, i_ref, hp_ref, buf_ref, a_ref, car_ref):
        @pl.when(pl.program_id(0) == 0)
        def _():
            buf_ref[0:H, :] = jnp.zeros((H, M), F32)
            car_ref[...] = jnp.zeros_like(car_ref)

        buf_ref[H:H + tc, :] = x_ref[...]
        xc = jnp.zeros((tc, M), F32) + bcv_ref[...]
        for k in range(K):
            xc = xc + wcv_ref[k:k + 1, :] * buf_ref[pl.ds(H - (K - 1) + k, tc), :]
        buf_ref[0:H, :] = buf_ref[tc:tc + H, :]
        r = _sigmoid(_dot(xc, wr_ref[...]) + br_ref[...])
        ig = _sigmoid(_dot(xc, wi_ref[...]) + bi_ref[...])
        xc_ref[...] = xc
        r_ref[...] = r
        i_ref[...] = ig
        a, _, bt = _lru_gates(xc, r, ig, sp_ref[...])
        a_ref[...] = a
        hp_ref[...] = bt
        row = lax.broadcasted_iota(jnp.int32, (8, M), 0)

        def blk(ib, _):
            r0 = pl.multiple_of(ib * 8, 8)
            av, bv = a_ref[pl.ds(r0, 8), :], hp_ref[pl.ds(r0, 8), :]
            for d in (1, 2, 4):
                m = row >= d
                bv = jnp.where(m, bv + av * pltpu.roll(bv, d, 0), bv)
                av = jnp.where(m, av * pltpu.roll(av, d, 0), av)
            cr = car_ref[7:8, :]
            h = bv + av * cr
            a_ref[pl.ds(r0, 8), :] = h
            hp_ref[pl.ds(r0, 8), :] = jnp.where(row == 0, cr, pltpu.roll(h, 1, 0))
            car_ref[...] = h
            return 0

        lax.fori_loop(0, tc // 8, blk, 0)
        y_ref[...] = (a_ref[...] * _gelu(g_ref[...])).astype(BF16)

    def rowm(c):
        return pl.BlockSpec((tc, M), lambda i: (i, c))
    vec = _full((1, M))
    return pl.pallas_call(
        body, name="lru_fwd", grid=(T // tc,),
        in_specs=[rowm(xcol), rowm(gcol), _full((8, M)), vec, _full((M, M)), vec, _full((M, M)), vec, vec],
        out_specs=[rowm(0)] * 5,
        out_shape=[jax.ShapeDtypeStruct((T, M), BF16)] + [jax.ShapeDtypeStruct((T, M), F32)] * 4,
        scratch_shapes=[pltpu.VMEM((tc + H, M), F32), pltpu.VMEM((tc, M), F32), pltpu.VMEM((8, M), F32)], compiler_params=ARB,
    )(proj, proj, wcv, bcv.reshape(1, M), wr, br.reshape(1, M), wi, bi.reshape(1, M), sp8.reshape(1, M))


def _lru_bwd(dmix, dcol, proj, xcol, gcol, xc, r, ig, hprev, wcv, wr, wi, sp8):
    T = proj.shape[0]
    M = wr.shape[0]
    tc = _pick(T, 256, 8)
    nc = T // tc
    H, K = SMALL_HALO, 4
    hb = tc // H

    def body(do_ref, x_ref, g_ref, xh_ref, xc_ref, r_ref, i_ref, hp_ref, wcv_ref, wr_ref, wi_ref, sp_ref,
             dx_ref, dg_ref, dwr_ref, dwi_ref, dwcv_ref, vec_ref, xbuf_ref, dbuf_ref, a_ref, gs_ref, car_ref):
        i = pl.program_id(0)

        @pl.when(i == 0)
        def _():
            for rf in (dwr_ref, dwi_ref, dwcv_ref, vec_ref, car_ref):
                rf[...] = jnp.zeros_like(rf)
            dbuf_ref[tc:tc + H, :] = jnp.zeros((H, M), F32)

        xcv, rv, iv, hpv, gv, do = xc_ref[...], r_ref[...], i_ref[...], hp_ref[...], g_ref[...], do_ref[...]
        sp = sp_ref[...]
        a, mult, bt = _lru_gates(xcv, rv, iv, sp)
        h = a * hpv + bt
        ge = _gelu(gv)
        dg_ref[...] = (do * h * _gelu_grad(gv)).astype(BF16)
        a_ref[...] = a
        gs_ref[...] = do * ge
        row = lax.broadcasted_iota(jnp.int32, (8, M), 0)

        def blk(n, _):
            r0 = pl.multiple_of((tc // 8 - 1 - n) * 8, 8)
            ablk = a_ref[pl.ds(r0, 8), :]
            av = jnp.where(row == 7, car_ref[0, 0:1, :], pltpu.roll(ablk, 7, 0))
            bv = gs_ref[pl.ds(r0, 8), :]
            for d in (1, 2, 4):
                m = row <= 7 - d
                bv = jnp.where(m, bv + av * pltpu.roll(bv, 8 - d, 0), bv)
                av = jnp.where(m, av * pltpu.roll(av, 8 - d, 0), av)
            gsv = bv + av * car_ref[1, 0:1, :]
            gs_ref[pl.ds(r0, 8), :] = gsv
            car_ref[0] = ablk
            car_ref[1] = gsv
            return 0

        lax.fori_loop(0, tc // 8, blk, 0)
        gs = gs_ref[...]
        dixc = gs * mult
        dla = gs * hpv * a - gs * (iv * xcv) * (a * a) / mult
        vec_ref[2] += _fold8(-dla * rv)
        dpr = (-dla * sp) * rv * (1.0 - rv)
        dpi = (dixc * xcv) * iv * (1.0 - iv)
        vec_ref[0] += _fold8(dpr)
        vec_ref[1] += _fold8(dpi)
        dwr_ref[...] += _dot_tn(xcv, dpr)
        dwi_ref[...] += _dot_tn(xcv, dpi)
        dxc = dixc * iv + _dot_nt(dpr, wr_ref[...]) + _dot_nt(dpi, wi_ref[...])
        vec_ref[3] += _fold8(dxc)
        xbuf_ref[0:H, :] = jnp.where(i == nc - 1, 0.0, xh_ref[...])
        xbuf_ref[H:H + tc, :] = x_ref[...]
        dbuf_ref[0:tc, :] = dxc
        dx = jnp.zeros((tc, M), F32)
        for k in range(K):
            dx = dx + wcv_ref[k:k + 1, :] * dbuf_ref[pl.ds(K - 1 - k, tc), :]
            dwcv_ref[k] += _fold8(dxc * xbuf_ref[pl.ds(H - (K - 1) + k, tc), :])
        dx_ref[...] = dx.astype(BF16)
        dbuf_ref[tc:tc + H, :] = dbuf_ref[0:H, :]

    def rowm(c):
        return pl.BlockSpec((tc, M), lambda i: (nc - 1 - i, c))
    halo = pl.BlockSpec((H, M), lambda i: (jnp.maximum((nc - 1 - i) * hb - 1, 0), xcol))
    vec = _full((1, M))
    return pl.pallas_call(
        body, name="lru_bwd", grid=(nc,),
        in_specs=[rowm(dcol), rowm(xcol), rowm(gcol), halo, rowm(0), rowm(0), rowm(0), rowm(0), _full((8, M)),
                  _full((M, M)), _full((M, M)), vec],
        out_specs=[rowm(0), rowm(0), _full((M, M)), _full((M, M)), _full((8, 8, M)), _full((4, 8, M))],
        out_shape=[jax.ShapeDtypeStruct((T, M), BF16), jax.ShapeDtypeStruct((T, M), BF16), jax.ShapeDtypeStruct((M, M), F32),
                   jax.ShapeDtypeStruct((M, M), F32), jax.ShapeDtypeStruct((8, 8, M), F32), jax.ShapeDtypeStruct((4, 8, M), F32)],
        scratch_shapes=[pltpu.VMEM((tc + H, M), F32), pltpu.VMEM((tc + H, M), F32), pltpu.VMEM((tc, M), F32),
                        pltpu.VMEM((tc, M), F32), pltpu.VMEM((2, 8, M), F32)],
        compiler_params=ARB,
    )(dmix, proj, proj, proj, xc, r, ig, hprev, wcv, wr, wi, sp8.reshape(1, M))


def _pool_diffs(buf_ref, x, t0, tc, gw):
    H = POOL_HALO
    pos = (lax.broadcasted_iota(jnp.int32, (tc, gw), 0) + t0 + 1).astype(F32)
    out, inv = [], []
    for gi, win in enumerate(POOL_WINDOWS):
        sl = slice(gi * gw, (gi + 1) * gw)
        s = x[:, sl]
        for j in range(1, win):
            s = s + buf_ref[pl.ds(H - j, tc), sl]
        ic = 1.0 / jnp.minimum(pos, float(win))
        out.append(s * ic - x[:, sl])
        inv.append(ic)
    return out, inv


def _pool_fwd(proj, col, pw, scale):
    T = proj.shape[0]
    ng, gw, _ = pw.shape
    M = ng * gw
    tc = _pick(T, 256, 8)
    H = POOL_HALO

    def body(x_ref, pw_ref, sc_ref, y_ref, buf_ref):
        @pl.when(pl.program_id(0) == 0)
        def _():
            buf_ref[0:H, :] = jnp.zeros((H, M), F32)

        x = x_ref[...]
        buf_ref[H:H + tc, :] = x
        diffs, _ = _pool_diffs(buf_ref, x, pl.program_id(0) * tc, tc, gw)
        for gi in range(ng):
            sl = slice(gi * gw, (gi + 1) * gw)
            y_ref[:, sl] = (_dot(diffs[gi], pw_ref[gi]) * sc_ref[:, sl]).astype(BF16)
        buf_ref[0:H, :] = buf_ref[tc:tc + H, :]

    return pl.pallas_call(
        body, name="pool_fwd", grid=(T // tc,),
        in_specs=[pl.BlockSpec((tc, M), lambda i: (i, col)), _full(pw.shape), _full((1, M))],
        out_specs=pl.BlockSpec((tc, M), lambda i: (i, 0)), out_shape=jax.ShapeDtypeStruct((T, M), BF16),
        scratch_shapes=[pltpu.VMEM((tc + H, M), F32)], compiler_params=ARB,
    )(proj, pw, scale.reshape(1, M))


def _pool_bwd(dmix, dcol, proj, col, pw, scale):
    T = proj.shape[0]
    ng, gw, _ = pw.shape
    M = ng * gw
    tc = _pick(T, 256, 8)
    nc = T // tc
    H = POOL_HALO
    hb = tc // H

    def body(do_ref, x_ref, xh_ref, pw_ref, sc_ref, dx_ref, dpw_ref, dsc_ref, buf_ref, ebuf_ref):
        i = pl.program_id(0)

        @pl.when(i == 0)
        def _():
            dpw_ref[...] = jnp.zeros_like(dpw_ref)
            dsc_ref[...] = jnp.zeros_like(dsc_ref)
            ebuf_ref[tc:tc + H, :] = jnp.zeros((H, M), F32)

        x, do = x_ref[...], do_ref[...]
        buf_ref[0:H, :] = jnp.where(i == nc - 1, 0.0, xh_ref[...])
        buf_ref[H:H + tc, :] = x
        diffs, inv = _pool_diffs(buf_ref, x, (nc - 1 - i) * tc, tc, gw)
        ddg = []
        for gi in range(ng):
            sl = slice(gi * gw, (gi + 1) * gw)
            dyy = do[:, sl] * sc_ref[:, sl]
            dsc_ref[:, sl] += _fold8(do[:, sl] * _dot(diffs[gi], pw_ref[gi]))
            dpw_ref[gi] += _dot_tn(diffs[gi], dyy)
            d = _dot_nt(dyy, pw_ref[gi])
            ddg.append(d)
            ebuf_ref[0:tc, sl] = d * inv[gi]
        for gi, win in enumerate(POOL_WINDOWS):
            sl = slice(gi * gw, (gi + 1) * gw)
            s = -ddg[gi]
            for j in range(win):
                s = s + ebuf_ref[pl.ds(j, tc), sl]
            dx_ref[:, sl] = s.astype(BF16)
        ebuf_ref[tc:tc + H, :] = ebuf_ref[0:H, :]

    return pl.pallas_call(
        body, name="pool_bwd", grid=(nc,),
        in_specs=[pl.BlockSpec((tc, M), lambda i: (nc - 1 - i, dcol)), pl.BlockSpec((tc, M), lambda i: (nc - 1 - i, col)),
                  pl.BlockSpec((H, M), lambda i: (jnp.maximum((nc - 1 - i) * hb - 1, 0), col)), _full(pw.shape), _full((1, M))],
        out_specs=[pl.BlockSpec((tc, M), lambda i: (nc - 1 - i, 0)), _full(pw.shape), _full((8, M))],
        out_shape=[jax.ShapeDtypeStruct((T, M), BF16), jax.ShapeDtypeStruct(pw.shape, F32), jax.ShapeDtypeStruct((8, M), F32)],
        scratch_shapes=[pltpu.VMEM((tc + H, M), F32), pltpu.VMEM((tc + H, M), F32)], compiler_params=ARB,
    )(dmix, proj, proj, pw, scale.reshape(1, M))


def _ffn_mid_fwd(up, wdw, bdw):
    T, F2 = up.shape
    F = F2 // 2
    tc = _pick(T, 256, 8)
    rb = _pick(tc, 64, 16)
    H, K = SMALL_HALO, 3

    def body(gt_ref, val_ref, w_ref, b_ref, act_ref, buf_ref):
        @pl.when(pl.program_id(0) == 0)
        def _():
            buf_ref[0:H, :] = jnp.zeros((H, F), F32)

        buf_ref[H:H + tc, :] = gt_ref[...]

        def strip(s, _):
            cs = pl.ds(pl.multiple_of(s * LANES, LANES), LANES)
            taps = [w_ref[k:k + 1, cs] for k in range(K)]
            for r0 in range(0, tc, rb):
                gc = b_ref[:, cs] + taps[0] * buf_ref[pl.ds(H - (K - 1) + r0, rb), cs]
                for k in range(1, K):
                    gc = gc + taps[k] * buf_ref[pl.ds(H - (K - 1) + k + r0, rb), cs]
                act_ref[pl.ds(r0, rb), cs] = (_gelu(gc) * val_ref[pl.ds(r0, rb), cs]).astype(BF16)
            return 0

        lax.fori_loop(0, F // LANES, strip, 0)
        buf_ref[0:H, :] = buf_ref[tc:tc + H, :]

    return pl.pallas_call(
        body, name="ffn_mid_fwd", grid=(T // tc,),
        in_specs=[pl.BlockSpec((tc, F), lambda i: (i, 0)), pl.BlockSpec((tc, F), lambda i: (i, 1)), _full((8, F)), _full((1, F))],
        out_specs=pl.BlockSpec((tc, F), lambda i: (i, 0)), out_shape=jax.ShapeDtypeStruct((T, F), BF16),
        scratch_shapes=[pltpu.VMEM((tc + H, F), F32)], compiler_params=ARB,
    )(up, up, wdw, bdw.reshape(1, F))


def _ffn_mid_bwd(dact, up, wdw, bdw):
    T, F2 = up.shape
    F = F2 // 2
    tc = _pick(T, 128, 8)
    rb = _pick(tc, 64, 16)
    nc = T // tc
    H, K = SMALL_HALO, 3
    hb = tc // H

    def body(da_ref, gt_ref, val_ref, gh_ref, w_ref, b_ref, dup_ref, dw_ref, db_ref, gbuf_ref, dbuf_ref):
        i = pl.program_id(0)

        @pl.when(i == 0)
        def _():
            dw_ref[...] = jnp.zeros_like(dw_ref)
            db_ref[...] = jnp.zeros_like(db_ref)
            dbuf_ref[tc:tc + H, :] = jnp.zeros((H, F), F32)

        gbuf_ref[0:H, :] = jnp.where(i == nc - 1, 0.0, gh_ref[...])
        gbuf_ref[H:H + tc, :] = gt_ref[...]

        def strip(s, _):
            cs = pl.ds(pl.multiple_of(s * LANES, LANES), LANES)
            cs_val = pl.ds(pl.multiple_of(F + s * LANES, LANES), LANES)
            taps = [w_ref[k:k + 1, cs] for k in range(K)]
            for r0 in range(0, tc, rb):
                shifted = [gbuf_ref[pl.ds(H - (K - 1) + k + r0, rb), cs] for k in range(K)]
                gc = b_ref[:, cs] + taps[0] * shifted[0]
                for k in range(1, K):
                    gc = gc + taps[k] * shifted[k]
                ge, gg = _gelu_and_grad(gc)
                da = da_ref[pl.ds(r0, rb), cs].astype(F32)
                dup_ref[pl.ds(r0, rb), cs_val] = (da * ge).astype(BF16)
                dgc = da * val_ref[pl.ds(r0, rb), cs] * gg
                dbuf_ref[pl.ds(r0, rb), cs] = dgc
                db_ref[:, cs] += _fold8(dgc)
                for k in range(K):
                    dw_ref[k, :, cs] += _fold8(dgc * shifted[k])
            for r0 in range(0, tc, rb):
                dgt = taps[0] * dbuf_ref[pl.ds(K - 1 + r0, rb), cs]
                for k in range(1, K):
                    dgt = dgt + taps[k] * dbuf_ref[pl.ds(K - 1 - k + r0, rb), cs]
                dup_ref[pl.ds(r0, rb), cs] = dgt.astype(BF16)
            return 0

        lax.fori_loop(0, F // LANES, strip, 0)
        dbuf_ref[tc:tc + H, :] = dbuf_ref[0:H, :]

    return pl.pallas_call(
        body, name="ffn_mid_bwd", grid=(nc,),
        in_specs=[pl.BlockSpec((tc, F), lambda i: (nc - 1 - i, 0)), pl.BlockSpec((tc, F), lambda i: (nc - 1 - i, 0)),
                  pl.BlockSpec((tc, F), lambda i: (nc - 1 - i, 1)),
                  pl.BlockSpec((H, F), lambda i: (jnp.maximum((nc - 1 - i) * hb - 1, 0), 0)), _full((8, F)), _full((1, F))],
        out_specs=[pl.BlockSpec((tc, F2), lambda i: (nc - 1 - i, 0)), _full((8, 8, F)), _full((8, F))],
        out_shape=[jax.ShapeDtypeStruct((T, F2), BF16), jax.ShapeDtypeStruct((8, 8, F), F32), jax.ShapeDtypeStruct((8, F), F32)],
        scratch_shapes=[pltpu.VMEM((tc + H, F), F32), pltpu.VMEM((tc + H, F), F32)], compiler_params=ARB,
    )(dact, up, up, up, wdw, bdw.reshape(1, F))


def _my_pos():
    return lax.axis_index("x"), lax.axis_index("y"), lax.axis_index("c")


def _flip(pos, k):
    x, y, c = pos
    return ((1 - x) if k & 4 else x, (1 - y) if k & 2 else y, (1 - c) if k & 1 else c)


def _rank(pos):
    return 4 * pos[0] + 2 * pos[1] + pos[2]


def _exchange(xs, name, gather):
    n = len(xs)
    shapes = [((N_DEV,) + tuple(x.shape)) if gather else tuple(x.shape) for x in xs]

    def body(*refs):
        ins, outs = refs[:n], refs[n:2 * n]
        send_sems, recv_sems, local_sems = refs[2 * n:]
        me = _my_pos()
        mine = _rank(me)
        local, sends = [], []
        for a in range(n):
            loc = pltpu.make_async_copy(ins[a] if gather else ins[a].at[mine], outs[a].at[mine], local_sems.at[a])
            loc.start()
            local.append(loc)
            for k in range(1, N_DEV):
                peer = _flip(me, k)
                cp = pltpu.make_async_remote_copy(
                    src_ref=ins[a] if gather else ins[a].at[_rank(peer)], dst_ref=outs[a].at[mine],
                    send_sem=send_sems.at[a, k - 1], recv_sem=recv_sems.at[a, k - 1], device_id=peer,
                    device_id_type=pl.DeviceIdType.MESH)
                cp.start()
                sends.append(cp)
        for a in range(n):
            for k in range(1, N_DEV):
                peer = _flip(me, k)
                pltpu.make_async_remote_copy(
                    src_ref=ins[a] if gather else ins[a].at[_rank(peer)], dst_ref=outs[a].at[_rank(peer)],
                    send_sem=send_sems.at[a, k - 1], recv_sem=recv_sems.at[a, k - 1], device_id=peer,
                    device_id_type=pl.DeviceIdType.MESH).wait_recv()
        for cp in sends:
            cp.wait_send()
        for loc in local:
            loc.wait()

    any_spec = pl.BlockSpec(memory_space=pl.ANY)
    return pl.pallas_call(
        body, name=name, in_specs=[any_spec] * n, out_specs=[any_spec] * n,
        out_shape=[jax.ShapeDtypeStruct(s, x.dtype) for s, x in zip(shapes, xs)],
        scratch_shapes=[pltpu.SemaphoreType.DMA((n, N_DEV - 1)), pltpu.SemaphoreType.DMA((n, N_DEV - 1)),
                        pltpu.SemaphoreType.DMA((n,))],
    )(*xs)


_HBM = pl.BlockSpec(memory_space=pltpu.HBM)
_SEM = pl.BlockSpec(memory_space=pltpu.SEMAPHORE)
_EFFECT = pltpu.SideEffectType.DATAFLOW_SIDE_EFFECTING


def _landing_copy(ins, lands, send_sems, recv_sems, me, a, k, gather, outgoing):
    peer = _flip(me, k)
    return pltpu.make_async_remote_copy(
        src_ref=ins[a] if gather else ins[a].at[_rank(peer)], dst_ref=lands[a].at[_rank(me) if outgoing else _rank(peer)],
        send_sem=send_sems.at[a * (N_DEV - 1) + k - 1], recv_sem=recv_sems.at[a * (N_DEV - 1) + k - 1], device_id=peer,
        device_id_type=pl.DeviceIdType.MESH)


def _exchange_start(xs, name, gather):
    n = len(xs)
    lands = [lax.empty(((N_DEV,) + tuple(x.shape)) if gather else tuple(x.shape), x.dtype) for x in xs]

    def body(*refs):
        ins, lnd = refs[:n], refs[n:2 * n]
        send_sems, recv_sems = refs[2 * n], refs[2 * n + 1]
        token, local_sems = refs[4 * n + 2], refs[4 * n + 3]
        me = _my_pos()
        mine = _rank(me)
        local = []
        for a in range(n):
            loc = pltpu.make_async_copy(ins[a] if gather else ins[a].at[mine], lnd[a].at[mine], local_sems.at[a])
            loc.start()
            local.append(loc)
            for k in range(1, N_DEV):
                _landing_copy(ins, lnd, send_sems, recv_sems, me, a, k, gather, True).start()
        for loc in local:
            loc.wait()
        token[...] = jnp.zeros_like(token)

    hbm = lambda t: pltpu.HBM(t.shape, t.dtype)
    res = pl.pallas_call(
        body, name=name,
        out_shape=(pltpu.SemaphoreType.DMA((n * (N_DEV - 1),)), pltpu.SemaphoreType.DMA((n * (N_DEV - 1),)),
                   *[hbm(x) for x in xs], *[hbm(t) for t in lands], jax.ShapeDtypeStruct((8, 128), F32)),
        in_specs=[_HBM] * (2 * n), out_specs=(_SEM, _SEM, *([_HBM] * (2 * n)), pl.BlockSpec(memory_space=pltpu.VMEM)),
        input_output_aliases={i: 2 + i for i in range(2 * n)}, scratch_shapes=[pltpu.SemaphoreType.DMA((n,))],
        compiler_params=pltpu.CompilerParams(has_side_effects=_EFFECT),
    )(*[pltpu.with_memory_space_constraint(t, pltpu.HBM) for t in list(xs) + lands])
    return (res[0], res[1], list(res[2:2 + n]), list(res[2 + n:2 + 2 * n])), res[2 + 2 * n]


def _exchange_wait(handle, after, name, gather):
    send_sems, recv_sems, ins, lands = handle
    n = len(ins)

    def body(*refs):
        src, lnd = refs[:n], refs[n:2 * n]
        ssem, rsem = refs[2 * n], refs[2 * n + 1]
        me = _my_pos()
        for a in range(n):
            for k in range(1, N_DEV):
                cp = _landing_copy(src, lnd, ssem, rsem, me, a, k, gather, False)
                cp.wait_send()
                cp.wait_recv()

    hbm = lambda t: pltpu.HBM(t.shape, t.dtype)
    res = pl.pallas_call(
        body, name=name, out_shape=tuple(hbm(t) for t in ins + lands),
        in_specs=[_HBM] * (2 * n) + [_SEM, _SEM, pl.BlockSpec(memory_space=pl.ANY)], out_specs=tuple([_HBM] * (2 * n)),
        input_output_aliases={i: i for i in range(2 * n)},
        compiler_params=pltpu.CompilerParams(has_side_effects=_EFFECT),
    )(*ins, *lands, send_sems, recv_sems, after)
    return list(res[n:])


def _adamw_math(w, g, m, v):
    m = ADAM_B1 * m + (1.0 - ADAM_B1) * g
    v = ADAM_B2 * v + (1.0 - ADAM_B2) * (g * g)
    m_hat = m / (1.0 - ADAM_B1 ** ADAM_STEP)
    v_hat = v / (1.0 - ADAM_B2 ** ADAM_STEP)
    return -ADAM_LR * (m_hat / (jnp.sqrt(v_hat) + ADAM_EPS) + ADAM_WD * w), m, v


def _sum8(parts, *, name):
    _, R, C = parts.shape
    tr = _pick(R, 256, 16)

    def body(p_ref, o_ref):
        acc = p_ref[0].astype(F32)
        for r in range(1, N_DEV):
            acc = acc + p_ref[r].astype(F32)
        o_ref[...] = acc

    return pl.pallas_call(
        body, name=name, grid=(R // tr,), in_specs=[pl.BlockSpec((N_DEV, tr, C), lambda i: (0, i, 0))],
        out_specs=pl.BlockSpec((tr, C), lambda i: (i, 0)), out_shape=jax.ShapeDtypeStruct((R, C), F32),
        compiler_params=pltpu.CompilerParams(dimension_semantics=("parallel",)),
    )(parts)


def _adamw(w, g, m, v, *, name):
    R, C = w.shape
    tr = _pick(R, 256, 8)

    def body(w_ref, g_ref, m_ref, v_ref, d_ref, nm_ref, nv_ref):
        d, nm, nv = _adamw_math(w_ref[...], g_ref[...], m_ref[...], v_ref[...])
        d_ref[...] = d
        nm_ref[...] = nm
        nv_ref[...] = nv

    row = pl.BlockSpec((tr, C), lambda i: (i, 0))
    return pl.pallas_call(
        body, name=name, grid=(R // tr,), in_specs=[row] * 4, out_specs=[row] * 3,
        out_shape=[jax.ShapeDtypeStruct((R, C), F32)] * 3,
        compiler_params=pltpu.CompilerParams(dimension_semantics=("parallel",)),
    )(w, g, m, v)


def _adamw_stacked(gs, w, m, v, *, name):
    depth, R, C = w.shape
    tr = _pick(R, 64, 16)
    summed = gs[0].ndim == 3

    def body(*refs):
        g_refs, (w_ref, m_ref, v_ref), outs = refs[:depth], refs[depth:depth + 3], refs[depth + 3:]
        for l in range(depth):
            if summed:
                g = g_refs[l][0].astype(F32)
                for r in range(1, N_DEV):
                    g = g + g_refs[l][r].astype(F32)
            else:
                g = g_refs[l][...]
            d, nm, nv = _adamw_math(w_ref[l], g, m_ref[l], v_ref[l])
            for o, val in zip(outs, (g, d, nm, nv)):
                o[l] = val

    g_spec = pl.BlockSpec((N_DEV, tr, C), lambda i: (0, i, 0)) if summed else pl.BlockSpec((tr, C), lambda i: (i, 0))
    stk = pl.BlockSpec((depth, tr, C), lambda i: (0, i, 0))
    return pl.pallas_call(
        body, name=name, grid=(R // tr,), in_specs=[g_spec] * depth + [stk] * 3, out_specs=[stk] * 4,
        out_shape=[jax.ShapeDtypeStruct((depth, R, C), F32)] * 4,
        compiler_params=pltpu.CompilerParams(dimension_semantics=("parallel",)),
    )(*gs, w, m, v)


def _pack(arrs):
    flat = jnp.concatenate([a.reshape(-1).astype(F32) for a in arrs])
    pad = (-flat.shape[0]) % 1024
    return jnp.pad(flat, (0, pad)).reshape(-1, 128)


def _unpack(buf, shapes):
    flat, out, off = buf.reshape(-1), [], 0
    for s in shapes:
        n = math.prod(s)
        out.append(flat[off:off + n].reshape(s))
        off += n
    return out


def _s5_params(lam_re, lam_im, log_step, b_re, b_im):
    G, P, H = b_re.shape
    step = jnp.exp(log_step)[:, None]
    mag = jnp.exp(lam_re * step)
    ar, ai = mag * jnp.cos(lam_im * step), mag * jnp.sin(lam_im * step)
    den = lam_re * lam_re + lam_im * lam_im
    qr = ((ar - 1.0) * lam_re + ai * lam_im) / den
    qi = (ai * lam_re - (ar - 1.0) * lam_im) / den
    bbr = qr[..., None] * b_re - qi[..., None] * b_im
    bbi = qr[..., None] * b_im + qi[..., None] * b_re
    return ar.reshape(-1), ai.reshape(-1), _s5_blocks_in(bbr), _s5_blocks_in(bbi)


def _s5_blocks_in(b):
    G, P, H = b.shape
    eye = jnp.eye(8, dtype=b.dtype)
    return jnp.einsum("kgph,gj->kghjp", b.reshape(G // 8, 8, P, H), eye).reshape(G // 8, 8 * H, 8 * P)


def _s5_blocks_in_diag(m, G, P, H):
    return jnp.einsum("kghgp->kgph", m.reshape(G // 8, 8, H, 8, P)).reshape(G, P, H)


def _s5_blocks_out(c):
    G, H, P = c.shape
    eye = jnp.eye(8, dtype=c.dtype)
    return jnp.einsum("kghp,gj->kgpjh", c.reshape(G // 8, 8, H, P), eye).reshape(G // 8, 8 * P, 8 * H)


def _s5_blocks_out_diag(m, G, H, P):
    return jnp.einsum("kgpgh->kghp", m.reshape(G // 8, 8, P, 8, H)).reshape(G, H, P)


def _head_blocks(w):
    nh, d, e = w.shape
    return jnp.einsum("hde,hk->hdke", w, jnp.eye(nh, dtype=w.dtype)).reshape(nh * d, nh * e)


def _head_blocks_diag(m, nh, d, e):
    return jnp.einsum("hdhe->hde", m.reshape(nh, d, nh, e))


def _pad_rows(w, rows):
    return jnp.pad(w, ((0, rows - w.shape[0]), (0, 0)))


def _unshard_cols(g):
    return jnp.transpose(g, (1, 0, 2)).reshape(g.shape[1], -1)


_NAMES = ['norm_mix_g', 'w_in', 's5_lam_re', 's5_lam_im', 's5_log_step', 's5_b_re', 's5_b_im', 's5_c_re', 's5_c_im', 's5_d',
          's5_w_glu', 's5_b_glu', 'cv_w_dw', 'cv_b_dw', 'cv_ln_g', 'cv_ln_b', 'cv_w_pw', 'cv_b_pw', 'lru_w_conv', 'lru_b_conv',
          'lru_w_r', 'lru_b_r', 'lru_w_i', 'lru_b_i', 'lru_lam', 'pool_w', 'pool_scale', 'w_out', 'norm_ffn_g', 'ffn_w_up',
          'ffn_w_dw', 'ffn_b_dw', 'ffn_w_down', 'norm_final_g']
_MATRIX = ('w_in', 'w_out', 'ffn_w_up', 'ffn_w_down', 's5_w_glu', 'cv_w_pw')
_TRANSPOSED = ('w_in', 'ffn_w_up')
_COLSHARD = ('cv_w_dw', 'lru_w_conv', 'ffn_w_dw')
_GROUPS = (('w_in', 's5_w_glu', 'cv_w_pw', 'w_out'), ('ffn_w_up', 'ffn_w_down'))


def kernel(x, norm_mix_g, w_in, s5_lam_re, s5_lam_im, s5_log_step, s5_b_re, s5_b_im, s5_c_re, s5_c_im, s5_d, s5_w_glu, s5_b_glu, cv_w_dw, cv_b_dw, cv_ln_g, cv_ln_b, cv_w_pw, cv_b_pw, lru_w_conv, lru_b_conv, lru_w_r, lru_b_r, lru_w_i, lru_b_i, lru_lam, pool_w, pool_scale, w_out, norm_ffn_g, ffn_w_up, ffn_w_dw, ffn_b_dw, ffn_w_down, norm_final_g, loss_target, m_norm_mix_g, m_w_in, m_s5_lam_re, m_s5_lam_im, m_s5_log_step, m_s5_b_re, m_s5_b_im, m_s5_c_re, m_s5_c_im, m_s5_d, m_s5_w_glu, m_s5_b_glu, m_cv_w_dw, m_cv_b_dw, m_cv_ln_g, m_cv_ln_b, m_cv_w_pw, m_cv_b_pw, m_lru_w_conv, m_lru_b_conv, m_lru_w_r, m_lru_b_r, m_lru_w_i, m_lru_b_i, m_lru_lam, m_pool_w, m_pool_scale, m_w_out, m_norm_ffn_g, m_ffn_w_up, m_ffn_w_dw, m_ffn_b_dw, m_ffn_w_down, m_norm_final_g, v_norm_mix_g, v_w_in, v_s5_lam_re, v_s5_lam_im, v_s5_log_step, v_s5_b_re, v_s5_b_im, v_s5_c_re, v_s5_c_im, v_s5_d, v_s5_w_glu, v_s5_b_glu, v_cv_w_dw, v_cv_b_dw, v_cv_ln_g, v_cv_ln_b, v_cv_w_pw, v_cv_b_pw, v_lru_w_conv, v_lru_b_conv, v_lru_w_r, v_lru_b_r, v_lru_w_i, v_lru_b_i, v_lru_lam, v_pool_w, v_pool_scale, v_w_out, v_norm_ffn_g, v_ffn_w_up, v_ffn_w_dw, v_ffn_b_dw, v_ffn_w_down, v_norm_final_g):
    args = (x, norm_mix_g, w_in, s5_lam_re, s5_lam_im, s5_log_step, s5_b_re, s5_b_im, s5_c_re, s5_c_im, s5_d, s5_w_glu, s5_b_glu, cv_w_dw, cv_b_dw, cv_ln_g, cv_ln_b, cv_w_pw, cv_b_pw, lru_w_conv, lru_b_conv, lru_w_r, lru_b_r, lru_w_i, lru_b_i, lru_lam, pool_w, pool_scale, w_out, norm_ffn_g, ffn_w_up, ffn_w_dw, ffn_b_dw, ffn_w_down, norm_final_g, loss_target, m_norm_mix_g, m_w_in, m_s5_lam_re, m_s5_lam_im, m_s5_log_step, m_s5_b_re, m_s5_b_im, m_s5_c_re, m_s5_c_im, m_s5_d, m_s5_w_glu, m_s5_b_glu, m_cv_w_dw, m_cv_b_dw, m_cv_ln_g, m_cv_ln_b, m_cv_w_pw, m_cv_b_pw, m_lru_w_conv, m_lru_b_conv, m_lru_w_r, m_lru_b_r, m_lru_w_i, m_lru_b_i, m_lru_lam, m_pool_w, m_pool_scale, m_w_out, m_norm_ffn_g, m_ffn_w_up, m_ffn_w_dw, m_ffn_b_dw, m_ffn_w_down, m_norm_final_g, v_norm_mix_g, v_w_in, v_s5_lam_re, v_s5_lam_im, v_s5_log_step, v_s5_b_re, v_s5_b_im, v_s5_c_re, v_s5_c_im, v_s5_d, v_s5_w_glu, v_s5_b_glu, v_cv_w_dw, v_cv_b_dw, v_cv_ln_g, v_cv_ln_b, v_cv_w_pw, v_cv_b_pw, v_lru_w_conv, v_lru_b_conv, v_lru_w_r, v_lru_b_r, v_lru_w_i, v_lru_b_i, v_lru_lam, v_pool_w, v_pool_scale, v_w_out, v_norm_ffn_g, v_ffn_w_up, v_ffn_w_dw, v_ffn_b_dw, v_ffn_w_down, v_norm_final_g)
    x, target = args[0], args[35]
    W = dict(zip(_NAMES, args[1:35]))
    Mo = dict(zip(_NAMES, args[36:70]))
    Vo = dict(zip(_NAMES, args[70:104]))
    x = x[0]
    target = target[0]
    T, D = x.shape
    depth = W['w_in'].shape[0]
    G, P, H = W['s5_b_re'].shape[1:]
    MW = G * H
    nh, hd, _ = W['lru_w_r'].shape[1:]
    me = _rank(_my_pos())

    gather, started = {}, jnp.zeros((), F32)
    for l in range(depth):
        for gi, names in enumerate(_GROUPS):
            shards = [(W[n][l].T if n in _TRANSPOSED else W[n][l]).astype(BF16) for n in names]
            if (l, gi) == (0, 0):
                shards.append(_pack([W[n][k] for k in range(depth) for n in _COLSHARD]))
            gather[l, gi], tok = _exchange_start(shards, "gather_start_%d%d" % (l, gi), True)
            started = started + tok[0, 0]
    full = {}

    def land_weights(l, gi, after):
        got = _exchange_wait(gather[l, gi], after, "gather_wait_%d%d" % (l, gi), True)
        for n, g8 in zip(_GROUPS[gi], got):
            full[n, l] = g8.reshape(-1, g8.shape[-1])
        return got

    taps8 = land_weights(0, 0, W['norm_mix_g'][0] + started)[-1]
    tap_shapes = [W[n][l].shape for l in range(depth) for n in _COLSHARD]
    taps = [_unshard_cols(jnp.stack(t)) for t in zip(*[_unpack(taps8[r], tap_shapes) for r in range(N_DEV)])]
    for l in range(depth):
        for j, n in enumerate(_COLSHARD):
            full[n, l] = taps[l * len(_COLSHARD) + j]

    saved = []
    for l in range(depth):
        g_mix = W['norm_mix_g'][l] + started
        if l:
            land_weights(l, 0, x)
        s5p, s5p_vjp = jax.vjp(_s5_params, W['s5_lam_re'][l], W['s5_lam_im'][l], W['s5_log_step'][l], W['s5_b_re'][l], W['s5_b_im'][l])
        ar, ai, bre, bim = s5p
        cre, cim = _s5_blocks_out(W['s5_c_re'][l]), _s5_blocks_out(W['s5_c_im'][l])
        sp8, sp8_vjp = jax.vjp(lambda lam: LRU_C * jax.nn.softplus(-lam), W['lru_lam'][l])
        wr, wi = _head_blocks(W['lru_w_r'][l]), _head_blocks(W['lru_w_i'][l])
        c = dict(bre=bre.astype(BF16), bim=bim.astype(BF16), cre=cre.astype(BF16), cim=cim.astype(BF16),
                 cst_f=_s5_scan_consts(ar, ai, False), cst_b=_s5_scan_consts(ar, ai, True), sp8=sp8, wr=wr.astype(BF16),
                 wi=wi.astype(BF16), cvw=_pad_rows(full['cv_w_dw', l], 32), lruw=_pad_rows(full['lru_w_conv', l], 8),
                 ffw=_pad_rows(full['ffn_w_dw', l], 8), pw=W['pool_w'][l].astype(BF16), s5p_vjp=s5p_vjp, sp8_vjp=sp8_vjp, x0=x)
        proj, h = _norm_matmul(x, g_mix, full['w_in', l], tm=1024, tn=512, name="in_proj")
        y0, sre, sim, yp, z = _s5_fwd(proj, 0, c['bre'], c['bim'], c['cre'], c['cim'], c['cst_f'], W['s5_d'][l],
                                     full['s5_w_glu', l], W['s5_b_glu'][l])
        y1, h1 = _cv_fwd(proj, 1, 2, c['cvw'], W['cv_b_dw'][l], W['cv_ln_g'][l], W['cv_ln_b'][l], full['cv_w_pw', l], W['cv_b_pw'][l])
        y2, xc, rr, ig, hprev = _lru_fwd(proj, 3, 4, c['lruw'], W['lru_b_conv'][l], c['wr'], W['lru_b_r'][l], c['wi'],
                                         W['lru_b_i'][l], sp8)
        y3 = _pool_fwd(proj, 5, c['pw'], W['pool_scale'][l])
        mixed = jnp.concatenate([y0, y1, y2, y3], axis=1)
        x1 = _matmul(mixed, full['w_out', l], mode="nn", tm=1024, tn=512, tk=2048, out_dtype=F32, name="out_proj", add=x)
        land_weights(l, 1, x1)
        up, h2 = _norm_matmul(x1, W['norm_ffn_g'][l], full['ffn_w_up', l], tm=1024, tn=256, name="ffn_up")
        act = _ffn_mid_fwd(up, c['ffw'], W['ffn_b_dw'][l])
        F = act.shape[1]
        x = _matmul(act, full['ffn_w_down', l], mode="nn", tm=512, tn=512, tk=F, out_dtype=F32, name="ffn_down", add=x1)
        c.update(proj=proj, h=h, sre=sre, sim=sim, yp=yp, z=z, h1=h1, xc=xc, rr=rr, ig=ig, hprev=hprev, mixed=mixed, x1=x1,
                 up=up, h2=h2, act=act)
        saved.append(c)

    dx, dxb, dgf, se = _loss_head(x, W['norm_final_g'], target)
    loss = lax.psum(0.5 / D * jnp.sum(se), AXES)
    gsmall = {('norm_final_g', None): dgf.sum(0)}
    gmat, scatter = {}, {}

    def parts_of(m):
        return m.reshape(N_DEV, m.shape[0] // N_DEV, m.shape[1])

    for l in reversed(range(depth)):
        c = saved[l]
        F = c['act'].shape[1]
        dact = _matmul(dxb, full['ffn_w_down', l], mode="nt", tm=512, tn=F, tk=512, out_dtype=BF16, name="d_act")
        gmat['ffn_w_down', l] = _matmul(c['act'], dxb, mode="tn", tm=F, tn=512, tk=256, out_dtype=BF16, name="dw_down")
        dup, dffw, dffb = _ffn_mid_bwd(dact, c['up'], c['ffw'], W['ffn_b_dw'][l])
        gsmall['ffn_w_dw', l] = dffw.sum(1)[:3]
        gsmall['ffn_b_dw', l] = dffb.sum(0)
        dh2 = _matmul(dup, full['ffn_w_up', l], mode="nn", tm=512, tn=512, tk=F, out_dtype=F32, name="d_h2")
        gmat['ffn_w_up', l] = _matmul(dup, c['h2'], mode="tn", tm=256, tn=D, tk=T, out_dtype=BF16, name="dw_up")
        scatter[l, 1], tok = _exchange_start([parts_of(gmat[n, l]) for n in _GROUPS[1]], "scatter_start_%d1" % l, False)
        dx1, dx1b, dg2 = _rms_bwd(dx, dh2, c['x1'], W['norm_ffn_g'][l] + tok[0, 0], name="ffn_norm_bwd")
        gsmall['norm_ffn_g', l] = dg2.sum(0)
        dmix = _matmul(dx1b, full['w_out', l], mode="nt", tm=1024, tn=512, tk=2048, out_dtype=F32, name="d_mixed")
        gmat['w_out', l] = _matmul(c['mixed'], dx1b, mode="tn", tm=512, tn=D, tk=2048, out_dtype=BF16, name="dw_out")
        du, dbre, dbim, dcre, dcim, da, dwglu, v3 = _s5_bwd(dmix, 0, c['proj'], 0, c['sre'], c['sim'], c['yp'], c['z'], c['bre'],
                                                            c['bim'], c['cre'], c['cim'], c['cst_b'], W['s5_d'][l], full['s5_w_glu', l])
        da = da.sum(1)
        glr, gli, gls, gbr, gbi = c['s5p_vjp']((da[0], da[1], dbre, dbim))
        gsmall['s5_lam_re', l], gsmall['s5_lam_im', l], gsmall['s5_log_step', l] = glr, gli, gls
        gsmall['s5_b_re', l], gsmall['s5_b_im', l] = gbr, gbi
        gsmall['s5_c_re', l] = _s5_blocks_out_diag(dcre, G, H, P)
        gsmall['s5_c_im', l] = _s5_blocks_out_diag(dcim, G, H, P)
        gsmall['s5_b_glu', l], gsmall['s5_d', l] = v3[0].sum(0), v3[1].sum(0)
        gmat['s5_w_glu', l] = dwglu.astype(BF16)
        dv, dg, dwpw, dcvw, v5 = _cv_bwd(dmix, 1, c['proj'], 1, 2, c['h1'], c['cvw'], W['cv_ln_g'][l], W['cv_ln_b'][l], full['cv_w_pw', l])
        gmat['cv_w_pw', l] = dwpw.astype(BF16)
        gsmall['cv_w_dw', l] = dcvw.sum(1)[:31]
        gsmall['cv_b_pw', l], gsmall['cv_ln_g', l], gsmall['cv_ln_b', l], gsmall['cv_b_dw', l] = [v5[j].sum(0) for j in range(4)]
        dlx, dlg, dwr, dwi, dlw, v4 = _lru_bwd(dmix, 2, c['proj'], 3, 4, c['xc'], c['rr'], c['ig'], c['hprev'], c['lruw'], c['wr'],
                                               c['wi'], c['sp8'])
        gsmall['lru_w_r', l], gsmall['lru_w_i', l] = _head_blocks_diag(dwr, nh, hd, hd), _head_blocks_diag(dwi, nh, hd, hd)
        gsmall['lru_w_conv', l] = dlw.sum(1)[:4]
        gsmall['lru_b_r', l], gsmall['lru_b_i', l] = v4[0].sum(0), v4[1].sum(0)
        gsmall['lru_lam', l] = c['sp8_vjp'](v4[2].sum(0))[0]
        gsmall['lru_b_conv', l] = v4[3].sum(0)
        dpx, dpw, dps = _pool_bwd(dmix, 3, c['proj'], 5, c['pw'], W['pool_scale'][l])
        gsmall['pool_w', l], gsmall['pool_scale', l] = dpw, dps.sum(0)
        dproj = jnp.concatenate([du, dv, dg, dlx, dlg, dpx], axis=1)
        dh = _matmul(dproj, full['w_in', l], mode="nn", tm=1024, tn=512, tk=1024, out_dtype=F32, name="d_h")
        gmat['w_in', l] = _matmul(dproj, c['h'], mode="tn", tm=512, tn=D, tk=2048, out_dtype=BF16, name="dw_in")
        scatter[l, 0], tok = _exchange_start([parts_of(gmat[n, l]) for n in _GROUPS[0]], "scatter_start_%d0" % l, False)
        dx, dxb, dg1 = _rms_bwd(dx1, dh, c['x0'], W['norm_mix_g'][l] + tok[0, 0], name="mix_norm_bwd")
        gsmall['norm_mix_g', l] = dg1.sum(0)

    small_names = [n for n in _NAMES if n not in _MATRIX]
    small_parts = [gsmall[n, l] for n in small_names for l in ((None,) if n == 'norm_final_g' else range(depth))]
    full_shapes = [gsmall[n, None].shape if n == 'norm_final_g' else (depth,) + gsmall[n, 0].shape for n in small_names]
    small_handle, tok = _exchange_start([_pack(small_parts)], "small_grads_start", True)
    parts = {}
    for l in reversed(range(depth)):
        for gi in (1, 0):
            got = _exchange_wait(scatter[l, gi], tok, "scatter_wait_%d%d" % (l, gi), False)
            for n, p8 in zip(_GROUPS[gi], got):
                parts[n, l] = p8

    out = {}
    for n in _MATRIX:
        if n in _TRANSPOSED:
            gl = [_sum8(parts[n, l], name="sum_" + n).T for l in range(depth)]
        else:
            gl = [parts[n, l] for l in range(depth)]
        out[n] = _adamw_stacked(gl, W[n], Mo[n], Vo[n], name="adamw_" + n)
    after_matrices = sum(out[n][1][0, 0, 0] for n in _MATRIX).reshape(1, 1)
    small8 = _exchange_wait(small_handle, after_matrices, "small_grads_wait", True)[0]
    small_sum = _sum8(small8, name="sum_small_grads")
    gs = dict(zip(small_names, _unpack(small_sum, full_shapes)))
    for n in _COLSHARD:
        cw = W[n].shape[-1]
        gs[n] = lax.dynamic_slice_in_dim(gs[n], me * cw, cw, axis=gs[n].ndim - 1)
    upd = _adamw(_pack([W[n] for n in small_names]), _pack([gs[n] for n in small_names]), _pack([Mo[n] for n in small_names]),
                 _pack([Vo[n] for n in small_names]), name="adamw_small")
    upd = [_unpack(u, [gs[n].shape for n in small_names]) for u in upd]
    for j, n in enumerate(small_names):
        out[n] = (gs[n], upd[0][j], upd[1][j], upd[2][j])
    return (loss, dx[None]) + tuple(out[n][q] for q in range(4) for n in _NAMES)
```

```python
import functools
import math

import jax
import jax.numpy as jnp
from jax import lax
from jax.experimental import pallas as pl
from jax.experimental.pallas import tpu as pltpu

F32 = jnp.float32
BF16 = jnp.bfloat16
N_DEV = 8
AXES = ("x", "y", "c")
EPS = 1e-6
S5_GROUP_CH = 16
S5_STATE = 64
LRU_HEADS = 8
LRU_C = 8.0
POOL_WINDOWS = (2, 4, 8, 16)
CONV_HALO = 32
SMALL_HALO = 8
POOL_HALO = 16
LANES = 128
ADAM_LR, ADAM_B1, ADAM_B2, ADAM_EPS, ADAM_WD, ADAM_STEP = 0.001, 0.9, 0.999, 1e-08, 0.01, 10
_GELU_K = math.sqrt(2.0 / math.pi)
ARB = pltpu.CompilerParams(dimension_semantics=("arbitrary",))


def _pick(n, pref, mult):
    best = None
    for t in range(mult, min(n, pref) + 1, mult):
        if n % t == 0:
            best = t
    return best if best is not None else n


def _sigmoid(x):
    return 1.0 / (1.0 + jnp.exp(-x))


def _gelu(x):
    return 0.5 * x * (1.0 + jnp.tanh(_GELU_K * (x + 0.044715 * x * x * x)))


def _gelu_grad(x):
    t = jnp.tanh(_GELU_K * (x + 0.044715 * x * x * x))
    return 0.5 * (1.0 + t) + 0.5 * x * (1.0 - t * t) * _GELU_K * (1.0 + 3.0 * 0.044715 * x * x)


def _gelu_and_grad(x):
    t = jnp.tanh(_GELU_K * (x + 0.044715 * x * x * x))
    half = 0.5 * (1.0 + t)
    return x * half, half + 0.5 * x * (1.0 - t * t) * _GELU_K * (1.0 + 3.0 * 0.044715 * x * x)


def _neg_expm1(x):
    series = -x * (1.0 + x * (0.5 + x * (1.0 / 6.0 + x * (1.0 / 24.0 + x * (1.0 / 120.0 + x * (1.0 / 720.0))))))
    return jnp.where(jnp.abs(x) < 0.25, series, 1.0 - jnp.exp(x))


def _fold8(x):
    return x.reshape(x.shape[0] // 8, 8, x.shape[1]).sum(axis=0)


def _dot(a, b):
    return jnp.dot(a.astype(BF16), b.astype(BF16), preferred_element_type=F32)


def _dot_nt(a, b):
    return lax.dot_general(a.astype(BF16), b.astype(BF16), (((1,), (1,)), ((), ())), preferred_element_type=F32)


def _dot_tn(a, b):
    return lax.dot_general(a.astype(BF16), b.astype(BF16), (((0,), (0,)), ((), ())), preferred_element_type=F32)


def _full(shape):
    nd = len(shape)
    return pl.BlockSpec(shape, lambda *i: (0,) * nd)


def _matmul(a, b, *, mode, tm, tn, tk, out_dtype, name, add=None):
    if mode == "nn":
        (M, K), N = a.shape, b.shape[1]
    elif mode == "nt":
        (M, K), N = a.shape, b.shape[0]
    else:
        (K, M), N = a.shape, b.shape[1]
    tm, tn, tk = _pick(M, tm, 8), _pick(N, tn, 128), _pick(K, tk, 128)
    if mode == "tn":
        tm = _pick(M, tm, 128)
    nk = K // tk
    a_spec = {"nn": pl.BlockSpec((tm, tk), lambda i, j, k: (i, k)), "nt": pl.BlockSpec((tm, tk), lambda i, j, k: (i, k)),
              "tn": pl.BlockSpec((tk, tm), lambda i, j, k: (k, i))}[mode]
    b_spec = {"nn": pl.BlockSpec((tk, tn), lambda i, j, k: (k, j)), "nt": pl.BlockSpec((tn, tk), lambda i, j, k: (j, k)),
              "tn": pl.BlockSpec((tk, tn), lambda i, j, k: (k, j))}[mode]
    o_spec = pl.BlockSpec((tm, tn), lambda i, j, k: (i, j))
    dot = {"nn": _dot, "nt": _dot_nt, "tn": _dot_tn}[mode]
    has_add = add is not None

    def body(*refs):
        if has_add:
            a_ref, b_ref, add_ref, o_ref, acc_ref = refs
        else:
            a_ref, b_ref, o_ref, acc_ref = refs
        k = pl.program_id(2)

        @pl.when(k == 0)
        def _():
            acc_ref[...] = jnp.zeros_like(acc_ref)

        acc_ref[...] += dot(a_ref[...], b_ref[...])

        @pl.when(k == nk - 1)
        def _():
            r = acc_ref[...]
            if has_add:
                r = r + add_ref[...]
            o_ref[...] = r.astype(out_dtype)

    ins = [a, b] + ([add] if has_add else [])
    specs = [a_spec, b_spec] + ([o_spec] if has_add else [])
    return pl.pallas_call(
        body, name=name, grid=(M // tm, N // tn, nk), in_specs=specs, out_specs=o_spec,
        out_shape=jax.ShapeDtypeStruct((M, N), out_dtype), scratch_shapes=[pltpu.VMEM((tm, tn), F32)],
        compiler_params=pltpu.CompilerParams(dimension_semantics=("parallel", "parallel", "arbitrary")),
    )(*ins)


def _norm_matmul(x, g, wt, *, tm, tn, name):
    T, D = x.shape
    N = wt.shape[0]
    tm, tn = _pick(T, tm, 8), _pick(N, tn, 128)

    def body(x_ref, g_ref, w_ref, o_ref, h_ref):
        @pl.when(pl.program_id(1) == 0)
        def _():
            xv = x_ref[...]
            r = lax.rsqrt(jnp.mean(xv * xv, axis=-1, keepdims=True) + EPS)
            h_ref[...] = (xv * r * g_ref[...]).astype(BF16)

        o_ref[...] = _dot_nt(h_ref[...], w_ref[...])

    return pl.pallas_call(
        body, name=name, grid=(T // tm, N // tn),
        in_specs=[pl.BlockSpec((tm, D), lambda i, j: (i, 0)), pl.BlockSpec((1, D), lambda i, j: (0, 0)),
                  pl.BlockSpec((tn, D), lambda i, j: (j, 0))],
        out_specs=[pl.BlockSpec((tm, tn), lambda i, j: (i, j)), pl.BlockSpec((tm, D), lambda i, j: (i, 0))],
        out_shape=[jax.ShapeDtypeStruct((T, N), F32), jax.ShapeDtypeStruct((T, D), BF16)],
        compiler_params=pltpu.CompilerParams(dimension_semantics=("parallel", "arbitrary")),
    )(x, g.reshape(1, D), wt)


def _rms_bwd(dres, dh, x, g, *, name):
    T, D = x.shape
    tc = _pick(T, 256, 8)

    def body(dres_ref, dh_ref, x_ref, g_ref, dx_ref, dxb_ref, dg_ref):
        @pl.when(pl.program_id(0) == 0)
        def _():
            dg_ref[...] = jnp.zeros_like(dg_ref)

        xv, dhv = x_ref[...], dh_ref[...]
        r = lax.rsqrt(jnp.mean(xv * xv, axis=-1, keepdims=True) + EPS)
        dg_ref[...] += _fold8(dhv * xv * r)
        dyg = dhv * g_ref[...]
        dx = dres_ref[...] + r * dyg - xv * (r * r * r) * jnp.mean(dyg * xv, axis=-1, keepdims=True)
        dx_ref[...] = dx
        dxb_ref[...] = dx.astype(BF16)

    row = pl.BlockSpec((tc, D), lambda i: (i, 0))
    return pl.pallas_call(
        body, name=name, grid=(T // tc,), in_specs=[row, row, row, _full((1, D))],
        out_specs=[row, row, _full((8, D))],
        out_shape=[jax.ShapeDtypeStruct((T, D), F32), jax.ShapeDtypeStruct((T, D), BF16), jax.ShapeDtypeStruct((8, D), F32)],
        compiler_params=ARB,
    )(dres, dh, x, g.reshape(1, D))


def _loss_head(x, g, target):
    T, D = x.shape
    tc = _pick(T, 256, 8)

    def body(x_ref, g_ref, t_ref, dx_ref, dxb_ref, dg_ref, se_ref):
        @pl.when(pl.program_id(0) == 0)
        def _():
            dg_ref[...] = jnp.zeros_like(dg_ref)
            se_ref[...] = jnp.zeros_like(se_ref)

        xv = x_ref[...]
        r = lax.rsqrt(jnp.mean(xv * xv, axis=-1, keepdims=True) + EPS)
        gv = g_ref[...]
        err = xv * r * gv - t_ref[...]
        se_ref[...] += _fold8(err * err)
        dy = err * (1.0 / D)
        dg_ref[...] += _fold8(dy * xv * r)
        dyg = dy * gv
        dx = r * dyg - xv * (r * r * r) * jnp.mean(dyg * xv, axis=-1, keepdims=True)
        dx_ref[...] = dx
        dxb_ref[...] = dx.astype(BF16)

    row = pl.BlockSpec((tc, D), lambda i: (i, 0))
    return pl.pallas_call(
        body, name="loss_head", grid=(T // tc,), in_specs=[row, _full((1, D)), row],
        out_specs=[row, row, _full((8, D)), _full((8, D))],
        out_shape=[jax.ShapeDtypeStruct((T, D), F32), jax.ShapeDtypeStruct((T, D), BF16),
                   jax.ShapeDtypeStruct((8, D), F32), jax.ShapeDtypeStruct((8, D), F32)],
        compiler_params=ARB,
    )(x, g.reshape(1, D), target)


def _s5_scan_consts(ar, ai, reverse):
    if reverse:
        ai = -ai
    pr, pi_ = [ar], [ai]
    for _ in range(7):
        pr, pi_ = pr + [pr[-1] * ar - pi_[-1] * ai], pi_ + [pr[-1] * ai + pi_[-1] * ar]
    rows = jnp.arange(8)[:, None]
    planes = []
    for d in (1, 2, 4):
        mask = (rows <= 7 - d) if reverse else (rows >= d)
        planes += [jnp.where(mask, pr[d - 1][None, :], 0.0), jnp.where(mask, pi_[d - 1][None, :], 0.0)]
    order = list(range(7, -1, -1)) if reverse else list(range(8))
    planes += [jnp.stack([pr[k] for k in order]), jnp.stack([pi_[k] for k in order])]
    return jnp.stack(planes).astype(F32)


def _s5_block_scan(xr_ref, xi_ref, cst_ref, car_ref, nblk, reverse, lb, extra=None):
    ns = xr_ref.shape[1]
    crow = 0 if reverse else 7

    def blk(i, _):
        ib = (nblk - 1 - i) if reverse else i
        r0 = pl.multiple_of(ib * 8, 8)
        for k in range(ns // lb):
            sl = slice(k * lb, (k + 1) * lb)
            xr, xi = xr_ref[pl.ds(r0, 8), sl], xi_ref[pl.ds(r0, 8), sl]
            for n, d in enumerate((1, 2, 4)):
                ar, ai = cst_ref[2 * n, :, sl], cst_ref[2 * n + 1, :, sl]
                sh = (8 - d) if reverse else d
                rr, ri = pltpu.roll(xr, sh, 0), pltpu.roll(xi, sh, 0)
                xr, xi = xr + ar * rr - ai * ri, xi + ar * ri + ai * rr
            pr, pi_ = cst_ref[6, :, sl], cst_ref[7, :, sl]
            cr, ci = car_ref[0, crow:crow + 1, sl], car_ref[1, crow:crow + 1, sl]
            xr, xi = xr + pr * cr - pi_ * ci, xi + pr * ci + pi_ * cr
            xr_ref[pl.ds(r0, 8), sl] = xr
            xi_ref[pl.ds(r0, 8), sl] = xi
            if extra is not None:
                extra(r0, sl, xr, xi, cr, ci)
            car_ref[0, :, sl] = xr
            car_ref[1, :, sl] = xi
        return 0

    lax.fori_loop(0, nblk, blk, 0)


def _s5_fwd(proj, col, bre, bim, cre, cim, cst, dvec, wglu, bglu):
    T = proj.shape[0]
    nsb, cw, sw = bre.shape
    M, NS = nsb * cw, nsb * sw
    tc = _pick(T, 256, 8)

    def body(u_ref, bre_ref, bim_ref, cre_ref, cim_ref, cst_ref, d_ref, wg_ref, bg_ref,
             y_ref, sre_ref, sim_ref, yp_ref, z_ref, car_ref):
        @pl.when(pl.program_id(0) == 0)
        def _():
            car_ref[...] = jnp.zeros_like(car_ref)

        u = u_ref[...]
        ub = u.astype(BF16)
        for k in range(nsb):
            sre_ref[:, k * sw:(k + 1) * sw] = _dot(ub[:, k * cw:(k + 1) * cw], bre_ref[k])
            sim_ref[:, k * sw:(k + 1) * sw] = _dot(ub[:, k * cw:(k + 1) * cw], bim_ref[k])
        _s5_block_scan(sre_ref, sim_ref, cst_ref, car_ref, tc // 8, False, sw)
        for k in range(nsb):
            yk = _dot(sre_ref[:, k * sw:(k + 1) * sw], cre_ref[k]) - _dot(sim_ref[:, k * sw:(k + 1) * sw], cim_ref[k])
            yp_ref[:, k * cw:(k + 1) * cw] = yk + d_ref[:, k * cw:(k + 1) * cw] * u[:, k * cw:(k + 1) * cw]
        gl = _gelu(yp_ref[...])
        z = _dot(gl, wg_ref[...]) + bg_ref[...]
        z_ref[...] = z
        y_ref[...] = (gl * _sigmoid(z)).astype(BF16)

    rowm = pl.BlockSpec((tc, M), lambda i: (i, 0))
    rows = pl.BlockSpec((tc, NS), lambda i: (i, 0))
    return pl.pallas_call(
        body, name="s5_fwd", grid=(T // tc,),
        in_specs=[pl.BlockSpec((tc, M), lambda i: (i, col)), _full(bre.shape), _full(bim.shape), _full(cre.shape),
                  _full(cim.shape), _full(cst.shape), _full((1, M)), _full((M, M)), _full((1, M))],
        out_specs=[rowm, rows, rows, rowm, rowm],
        out_shape=[jax.ShapeDtypeStruct((T, M), BF16), jax.ShapeDtypeStruct((T, NS), F32), jax.ShapeDtypeStruct((T, NS), F32),
                   jax.ShapeDtypeStruct((T, M), F32), jax.ShapeDtypeStruct((T, M), F32)],
        scratch_shapes=[pltpu.VMEM((2, 8, NS), F32)], compiler_params=ARB,
    )(proj, bre, bim, cre, cim, cst, dvec.reshape(1, M), wglu, bglu.reshape(1, M))


def _s5_bwd(dmix, dcol, proj, col, sre, sim, yp, z, bre, bim, cre, cim, cst, dvec, wglu):
    T = proj.shape[0]
    nsb, cw, sw = bre.shape
    M, NS = nsb * cw, nsb * sw
    tc = _pick(T, 256, 8)
    nc = T // tc

    def body(do_ref, u_ref, sre_ref, sim_ref, yp_ref, z_ref, bre_ref, bim_ref, cre_ref, cim_ref, cst_ref, d_ref, wg_ref,
             du_ref, dbre_ref, dbim_ref, dcre_ref, dcim_ref, da_ref, dwg_ref, vec_ref, gre_ref, gim_ref, car_ref):
        @pl.when(pl.program_id(0) == 0)
        def _():
            for r in (dbre_ref, dbim_ref, dcre_ref, dcim_ref, da_ref, dwg_ref, vec_ref, car_ref):
                r[...] = jnp.zeros_like(r)

        u, do, ypv = u_ref[...], do_ref[...], yp_ref[...]
        gl = _gelu(ypv)
        sg = _sigmoid(z_ref[...])
        dz = do * gl * sg * (1.0 - sg)
        dgl = do * sg + _dot_nt(dz, wg_ref[...])
        dwg_ref[...] += _dot_tn(gl, dz)
        vec_ref[0] += _fold8(dz)
        dy = dgl * _gelu_grad(ypv)
        vec_ref[1] += _fold8(dy * u)
        dyb = dy.astype(BF16)
        for k in range(nsb):
            cs, ss = slice(k * cw, (k + 1) * cw), slice(k * sw, (k + 1) * sw)
            gre_ref[:, ss] = _dot_nt(dyb[:, cs], cre_ref[k])
            gim_ref[:, ss] = -_dot_nt(dyb[:, cs], cim_ref[k])
            dcre_ref[k] += _dot_tn(sre_ref[:, ss], dyb[:, cs])
            dcim_ref[k] -= _dot_tn(sim_ref[:, ss], dyb[:, cs])

        row = lax.broadcasted_iota(jnp.int32, (8, sw), 0)

        def extra(r0, sl, gr, gi, cr, ci):
            nr = jnp.where(row == 7, cr, pltpu.roll(gr, 7, 0))
            ni = jnp.where(row == 7, ci, pltpu.roll(gi, 7, 0))
            sr, si = sre_ref[pl.ds(r0, 8), sl], sim_ref[pl.ds(r0, 8), sl]
            da_ref[0, :, sl] += nr * sr + ni * si
            da_ref[1, :, sl] += ni * sr - nr * si

        _s5_block_scan(gre_ref, gim_ref, cst_ref, car_ref, tc // 8, True, sw, extra)
        ub = u.astype(BF16)
        for k in range(nsb):
            cs, ss = slice(k * cw, (k + 1) * cw), slice(k * sw, (k + 1) * sw)
            gr, gi = gre_ref[:, ss].astype(BF16), gim_ref[:, ss].astype(BF16)
            duk = dy[:, cs] * d_ref[:, cs] + _dot_nt(gr, bre_ref[k]) + _dot_nt(gi, bim_ref[k])
            du_ref[:, cs] = duk.astype(BF16)
            dbre_ref[k] += _dot_tn(ub[:, cs], gr)
            dbim_ref[k] += _dot_tn(ub[:, cs], gi)

    def rowm(c):
        return pl.BlockSpec((tc, M), lambda i: (nc - 1 - i, c))
    rows = pl.BlockSpec((tc, NS), lambda i: (nc - 1 - i, 0))
    return pl.pallas_call(
        body, name="s5_bwd", grid=(nc,),
        in_specs=[rowm(dcol), rowm(col), rows, rows, rowm(0), rowm(0), _full(bre.shape), _full(bim.shape), _full(cre.shape),
                  _full(cim.shape), _full(cst.shape), _full((1, M)), _full((M, M))],
        out_specs=[rowm(0), _full(bre.shape), _full(bim.shape), _full(cre.shape), _full(cim.shape), _full((2, 8, NS)),
                   _full((M, M)), _full((3, 8, M))],
        out_shape=[jax.ShapeDtypeStruct((T, M), BF16), jax.ShapeDtypeStruct(bre.shape, F32), jax.ShapeDtypeStruct(bim.shape, F32),
                   jax.ShapeDtypeStruct(cre.shape, F32), jax.ShapeDtypeStruct(cim.shape, F32), jax.ShapeDtypeStruct((2, 8, NS), F32),
                   jax.ShapeDtypeStruct((M, M), F32), jax.ShapeDtypeStruct((3, 8, M), F32)],
        scratch_shapes=[pltpu.VMEM((tc, NS), F32), pltpu.VMEM((tc, NS), F32), pltpu.VMEM((2, 8, NS), F32)],
        compiler_params=ARB,
    )(dmix, proj, sre, sim, yp, z, bre, bim, cre, cim, cst, dvec.reshape(1, M), wglu)


def _ln_silu(h1, lg, lb):
    mu = jnp.mean(h1, axis=-1, keepdims=True)
    xc = h1 - mu
    rstd = lax.rsqrt(jnp.mean(xc * xc, axis=-1, keepdims=True) + EPS)
    xhat = xc * rstd
    h2 = xhat * lg + lb
    return xhat, rstd, h2, _sigmoid(h2)


def _cv_fwd(proj, vcol, gcol, wdw, bdw, lg, lb, wpw, bpw):
    T = proj.shape[0]
    M = wpw.shape[0]
    tc = _pick(T, 256, 8)
    H, K = CONV_HALO, 31

    def body(v_ref, g_ref, wdw_ref, bdw_ref, lg_ref, lb_ref, wpw_ref, bpw_ref, y_ref, h1_ref, buf_ref):
        @pl.when(pl.program_id(0) == 0)
        def _():
            buf_ref[0:H, :] = jnp.zeros((H, M), F32)

        buf_ref[H:H + tc, :] = v_ref[...] * _sigmoid(g_ref[...])
        acc = jnp.zeros((tc, M), F32) + bdw_ref[...]
        for k in range(K):
            acc = acc + wdw_ref[k:k + 1, :] * buf_ref[pl.ds(H - (K - 1) + k, tc), :]
        h1_ref[...] = acc
        _, _, h2, sg = _ln_silu(acc, lg_ref[...], lb_ref[...])
        y_ref[...] = (_dot(h2 * sg, wpw_ref[...]) + bpw_ref[...]).astype(BF16)
        buf_ref[0:H, :] = buf_ref[tc:tc + H, :]

    def rowm(c):
        return pl.BlockSpec((tc, M), lambda i: (i, c))
    vec = _full((1, M))
    return pl.pallas_call(
        body, name="cv_fwd", grid=(T // tc,),
        in_specs=[rowm(vcol), rowm(gcol), _full((32, M)), vec, vec, vec, _full((M, M)), vec],
        out_specs=[rowm(0), rowm(0)],
        out_shape=[jax.ShapeDtypeStruct((T, M), BF16), jax.ShapeDtypeStruct((T, M), F32)],
        scratch_shapes=[pltpu.VMEM((tc + H, M), F32)], compiler_params=ARB,
    )(proj, proj, wdw, bdw.reshape(1, M), lg.reshape(1, M), lb.reshape(1, M), wpw, bpw.reshape(1, M))


def _cv_bwd(dmix, dcol, proj, vcol, gcol, h1, wdw, lg, lb, wpw):
    T = proj.shape[0]
    M = wpw.shape[0]
    tc = _pick(T, 256, 8)
    nc = T // tc
    H, K = CONV_HALO, 31
    hb = tc // H

    def body(do_ref, v_ref, g_ref, vh_ref, gh_ref, h1_ref, wdw_ref, lg_ref, lb_ref, wpw_ref,
             dv_ref, dg_ref, dwpw_ref, dwdw_ref, vec_ref, hbuf_ref, dbuf_ref):
        i = pl.program_id(0)

        @pl.when(i == 0)
        def _():
            for r in (dwpw_ref, dwdw_ref, vec_ref):
                r[...] = jnp.zeros_like(r)
            dbuf_ref[tc:tc + H, :] = jnp.zeros((H, M), F32)

        v, g = v_ref[...], g_ref[...]
        sgg = _sigmoid(g)
        halo = vh_ref[...] * _sigmoid(gh_ref[...])
        hbuf_ref[0:H, :] = jnp.where(i == nc - 1, 0.0, halo)
        hbuf_ref[H:H + tc, :] = v * sgg
        do = do_ref[...]
        xhat, rstd, h2, sg = _ln_silu(h1_ref[...], lg_ref[...], lb_ref[...])
        dwpw_ref[...] += _dot_tn(h2 * sg, do)
        vec_ref[0] += _fold8(do)
        dh2 = _dot_nt(do, wpw_ref[...]) * (sg * (1.0 + h2 * (1.0 - sg)))
        vec_ref[1] += _fold8(dh2 * xhat)
        vec_ref[2] += _fold8(dh2)
        dxh = dh2 * lg_ref[...]
        dh1 = rstd * (dxh - jnp.mean(dxh, axis=-1, keepdims=True) - xhat * jnp.mean(dxh * xhat, axis=-1, keepdims=True))
        vec_ref[3] += _fold8(dh1)
        dbuf_ref[0:tc, :] = dh1
        dh0 = jnp.zeros((tc, M), F32)
        for k in range(K):
            dh0 = dh0 + wdw_ref[k:k + 1, :] * dbuf_ref[pl.ds(K - 1 - k, tc), :]
            dwdw_ref[k] += _fold8(dh1 * hbuf_ref[pl.ds(H - (K - 1) + k, tc), :])
        dv_ref[...] = (dh0 * sgg).astype(BF16)
        dg_ref[...] = (dh0 * v * sgg * (1.0 - sgg)).astype(BF16)
        dbuf_ref[tc:tc + H, :] = dbuf_ref[0:H, :]

    def rowm(c):
        return pl.BlockSpec((tc, M), lambda i: (nc - 1 - i, c))

    def halo(c):
        return pl.BlockSpec((H, M), lambda i: (jnp.maximum((nc - 1 - i) * hb - 1, 0), c))
    vec = _full((1, M))
    return pl.pallas_call(
        body, name="cv_bwd", grid=(nc,),
        in_specs=[rowm(dcol), rowm(vcol), rowm(gcol), halo(vcol), halo(gcol), rowm(0), _full((32, M)), vec, vec, _full((M, M))],
        out_specs=[rowm(0), rowm(0), _full((M, M)), _full((32, 8, M)), _full((5, 8, M))],
        out_shape=[jax.ShapeDtypeStruct((T, M), BF16), jax.ShapeDtypeStruct((T, M), BF16), jax.ShapeDtypeStruct((M, M), F32),
                   jax.ShapeDtypeStruct((32, 8, M), F32), jax.ShapeDtypeStruct((5, 8, M), F32)],
        scratch_shapes=[pltpu.VMEM((tc + H, M), F32), pltpu.VMEM((tc + H, M), F32)], compiler_params=ARB,
    )(dmix, proj, proj, proj, proj, h1, wdw, lg.reshape(1, M), lb.reshape(1, M), wpw)


def _lru_gates(xc, r, ig, sp8):
    la = -r * sp8
    a = jnp.exp(la)
    mult = jnp.sqrt(_neg_expm1(2.0 * la))
    return a, mult, mult * (ig * xc)


def _lru_fwd(proj, xcol, gcol, wcv, bcv, wr, br, wi, bi, sp8):
    T = proj.shape[0]
    M = wr.shape[0]
    tc = _pick(T, 256, 8)
    H, K = SMALL_HALO, 4

    def body(x_ref, g_ref, wcv_ref, bcv_ref, wr_ref, br_ref, wi_ref, bi_ref, sp_ref,
             y_ref, xc_ref, r_ref, i_ref, hp_ref, buf_ref, a_ref, car_ref):
        @pl.when(pl.program_id(0) == 0)
        def _():
            buf_ref[0:H, :] = jnp.zeros((H, M), F32)
            car_ref[...] = jnp.zeros_like(car_ref)

        buf_ref[H:H + tc, :] = x_ref[...]
        xc = jnp.zeros((tc, M), F32) + bcv_ref[...]
        for k in range(K):
            xc = xc + wcv_ref[k:k + 1, :] * buf_ref[pl.ds(H - (K - 1) + k, tc), :]
        buf_ref[0:H, :] = buf_ref[tc:tc + H, :]
        r = _sigmoid(_dot(xc, wr_ref[...]) + br_ref[...])
        ig = _sigmoid(_dot(xc, wi_ref[...]) + bi_ref[...])
        xc_ref[...] = xc
        r_ref[...] = r
        i_ref[...] = ig
        a, _, bt = _lru_gates(xc, r, ig, sp_ref[...])
        a_ref[...] = a
        hp_ref[...] = bt
        row = lax.broadcasted_iota(jnp.int32, (8, M), 0)

        def blk(ib, _):
            r0 = pl.multiple_of(ib * 8, 8)
            av, bv = a_ref[pl.ds(r0, 8), :], hp_ref[pl.ds(r0, 8), :]
            for d in (1, 2, 4):
                m = row >= d
                bv = jnp.where(m, bv + av * pltpu.roll(bv, d, 0), bv)
                av = jnp.where(m, av * pltpu.roll(av, d, 0), av)
            cr = car_ref[7:8, :]
            h = bv + av * cr
            a_ref[pl.ds(r0, 8), :] = h
            hp_ref[pl.ds(r0, 8), :] = jnp.where(row == 0, cr, pltpu.roll(h, 1, 0))
            car_ref[...] = h
            return 0

        lax.fori_loop(0, tc // 8, blk, 0)
        y_ref[...] = (a_ref[...] * _gelu(g_ref[...])).astype(BF16)

    def rowm(c):
        return pl.BlockSpec((tc, M), lambda i: (i, c))
    vec = _full((1, M))
    return pl.pallas_call(
        body, name="lru_fwd", grid=(T // tc,),
        in_specs=[rowm(xcol), rowm(gcol), _full((8, M)), vec, _full((M, M)), vec, _full((M, M)), vec, vec],
        out_specs=[rowm(0)] * 5,
        out_shape=[jax.ShapeDtypeStruct((T, M), BF16)] + [jax.ShapeDtypeStruct((T, M), F32)] * 4,
        scratch_shapes=[pltpu.VMEM((tc + H, M), F32), pltpu.VMEM((tc, M), F32), pltpu.VMEM((8, M), F32)], compiler_params=ARB,
    )(proj, proj, wcv, bcv.reshape(1, M), wr, br.reshape(1, M), wi, bi.reshape(1, M), sp8.reshape(1, M))


def _lru_bwd(dmix, dcol, proj, xcol, gcol, xc, r, ig, hprev, wcv, wr, wi, sp8):
    T = proj.shape[0]
    M = wr.shape[0]
    tc = _pick(T, 256, 8)
    nc = T // tc
    H, K = SMALL_HALO, 4
    hb = tc // H

    def body(do_ref, x_ref, g_ref, xh_ref, xc_ref, r_ref, i_ref, hp_ref, wcv_ref, wr_ref, wi_ref, sp_ref,
             dx_ref, dg_ref, dwr_ref, dwi_ref, dwcv_ref, vec_ref, xbuf_ref, dbuf_ref, a_ref, gs_ref, car_ref):
        i = pl.program_id(0)

        @pl.when(i == 0)
        def _():
            for rf in (dwr_ref, dwi_ref, dwcv_ref, vec_ref, car_ref):
                rf[...] = jnp.zeros_like(rf)
            dbuf_ref[tc:tc + H, :] = jnp.zeros((H, M), F32)

        xcv, rv, iv, hpv, gv, do = xc_ref[...], r_ref[...], i_ref[...], hp_ref[...], g_ref[...], do_ref[...]
        sp = sp_ref[...]
        a, mult, bt = _lru_gates(xcv, rv, iv, sp)
        h = a * hpv + bt
        ge = _gelu(gv)
        dg_ref[...] = (do * h * _gelu_grad(gv)).astype(BF16)
        a_ref[...] = a
        gs_ref[...] = do * ge
        row = lax.broadcasted_iota(jnp.int32, (8, M), 0)

        def blk(n, _):
            r0 = pl.multiple_of((tc // 8 - 1 - n) * 8, 8)
            ablk = a_ref[pl.ds(r0, 8), :]
            av = jnp.where(row == 7, car_ref[0, 0:1, :], pltpu.roll(ablk, 7, 0))
            bv = gs_ref[pl.ds(r0, 8), :]
            for d in (1, 2, 4):
                m = row <= 7 - d
                bv = jnp.where(m, bv + av * pltpu.roll(bv, 8 - d, 0), bv)
                av = jnp.where(m, av * pltpu.roll(av, 8 - d, 0), av)
            gsv = bv + av * car_ref[1, 0:1, :]
            gs_ref[pl.ds(r0, 8), :] = gsv
            car_ref[0] = ablk
            car_ref[1] = gsv
            return 0

        lax.fori_loop(0, tc // 8, blk, 0)
        gs = gs_ref[...]
        dixc = gs * mult
        dla = gs * hpv * a - gs * (iv * xcv) * (a * a) / mult
        vec_ref[2] += _fold8(-dla * rv)
        dpr = (-dla * sp) * rv * (1.0 - rv)
        dpi = (dixc * xcv) * iv * (1.0 - iv)
        vec_ref[0] += _fold8(dpr)
        vec_ref[1] += _fold8(dpi)
        dwr_ref[...] += _dot_tn(xcv, dpr)
        dwi_ref[...] += _dot_tn(xcv, dpi)
        dxc = dixc * iv + _dot_nt(dpr, wr_ref[...]) + _dot_nt(dpi, wi_ref[...])
        vec_ref[3] += _fold8(dxc)
        xbuf_ref[0:H, :] = jnp.where(i == nc - 1, 0.0, xh_ref[...])
        xbuf_ref[H:H + tc, :] = x_ref[...]
        dbuf_ref[0:tc, :] = dxc
        dx = jnp.zeros((tc, M), F32)
        for k in range(K):
            dx = dx + wcv_ref[k:k + 1, :] * dbuf_ref[pl.ds(K - 1 - k, tc), :]
            dwcv_ref[k] += _fold8(dxc * xbuf_ref[pl.ds(H - (K - 1) + k, tc), :])
        dx_ref[...] = dx.astype(BF16)
        dbuf_ref[tc:tc + H, :] = dbuf_ref[0:H, :]

    def rowm(c):
        return pl.BlockSpec((tc, M), lambda i: (nc - 1 - i, c))
    halo = pl.BlockSpec((H, M), lambda i: (jnp.maximum((nc - 1 - i) * hb - 1, 0), xcol))
    vec = _full((1, M))
    return pl.pallas_call(
        body, name="lru_bwd", grid=(nc,),
        in_specs=[rowm(dcol), rowm(xcol), rowm(gcol), halo, rowm(0), rowm(0), rowm(0), rowm(0), _full((8, M)),
                  _full((M, M)), _full((M, M)), vec],
        out_specs=[rowm(0), rowm(0), _full((M, M)), _full((M, M)), _full((8, 8, M)), _full((4, 8, M))],
        out_shape=[jax.ShapeDtypeStruct((T, M), BF16), jax.ShapeDtypeStruct((T, M), BF16), jax.ShapeDtypeStruct((M, M), F32),
                   jax.ShapeDtypeStruct((M, M), F32), jax.ShapeDtypeStruct((8, 8, M), F32), jax.ShapeDtypeStruct((4, 8, M), F32)],
        scratch_shapes=[pltpu.VMEM((tc + H, M), F32), pltpu.VMEM((tc + H, M), F32), pltpu.VMEM((tc, M), F32),
                        pltpu.VMEM((tc, M), F32), pltpu.VMEM((2, 8, M), F32)],
        compiler_params=ARB,
    )(dmix, proj, proj, proj, xc, r, ig, hprev, wcv, wr, wi, sp8.reshape(1, M))


def _pool_diffs(buf_ref, x, t0, tc, gw):
    H = POOL_HALO
    pos = (lax.broadcasted_iota(jnp.int32, (tc, gw), 0) + t0 + 1).astype(F32)
    out, inv = [], []
    for gi, win in enumerate(POOL_WINDOWS):
        sl = slice(gi * gw, (gi + 1) * gw)
        s = x[:, sl]
        for j in range(1, win):
            s = s + buf_ref[pl.ds(H - j, tc), sl]
        ic = 1.0 / jnp.minimum(pos, float(win))
        out.append(s * ic - x[:, sl])
        inv.append(ic)
    return out, inv


def _pool_fwd(proj, col, pw, scale):
    T = proj.shape[0]
    ng, gw, _ = pw.shape
    M = ng * gw
    tc = _pick(T, 256, 8)
    H = POOL_HALO

    def body(x_ref, pw_ref, sc_ref, y_ref, buf_ref):
        @pl.when(pl.program_id(0) == 0)
        def _():
            buf_ref[0:H, :] = jnp.zeros((H, M), F32)

        x = x_ref[...]
        buf_ref[H:H + tc, :] = x
        diffs, _ = _pool_diffs(buf_ref, x, pl.program_id(0) * tc, tc, gw)
        for gi in range(ng):
            sl = slice(gi * gw, (gi + 1) * gw)
            y_ref[:, sl] = (_dot(diffs[gi], pw_ref[gi]) * sc_ref[:, sl]).astype(BF16)
        buf_ref[0:H, :] = buf_ref[tc:tc + H, :]

    return pl.pallas_call(
        body, name="pool_fwd", grid=(T // tc,),
        in_specs=[pl.BlockSpec((tc, M), lambda i: (i, col)), _full(pw.shape), _full((1, M))],
        out_specs=pl.BlockSpec((tc, M), lambda i: (i, 0)), out_shape=jax.ShapeDtypeStruct((T, M), BF16),
        scratch_shapes=[pltpu.VMEM((tc + H, M), F32)], compiler_params=ARB,
    )(proj, pw, scale.reshape(1, M))


def _pool_bwd(dmix, dcol, proj, col, pw, scale):
    T = proj.shape[0]
    ng, gw, _ = pw.shape
    M = ng * gw
    tc = _pick(T, 256, 8)
    nc = T // tc
    H = POOL_HALO
    hb = tc // H

    def body(do_ref, x_ref, xh_ref, pw_ref, sc_ref, dx_ref, dpw_ref, dsc_ref, buf_ref, ebuf_ref):
        i = pl.program_id(0)

        @pl.when(i == 0)
        def _():
            dpw_ref[...] = jnp.zeros_like(dpw_ref)
            dsc_ref[...] = jnp.zeros_like(dsc_ref)
            ebuf_ref[tc:tc + H, :] = jnp.zeros((H, M), F32)

        x, do = x_ref[...], do_ref[...]
        buf_ref[0:H, :] = jnp.where(i == nc - 1, 0.0, xh_ref[...])
        buf_ref[H:H + tc, :] = x
        diffs, inv = _pool_diffs(buf_ref, x, (nc - 1 - i) * tc, tc, gw)
        ddg = []
        for gi in range(ng):
            sl = slice(gi * gw, (gi + 1) * gw)
            dyy = do[:, sl] * sc_ref[:, sl]
            dsc_ref[:, sl] += _fold8(do[:, sl] * _dot(diffs[gi], pw_ref[gi]))
            dpw_ref[gi] += _dot_tn(diffs[gi], dyy)
            d = _dot_nt(dyy, pw_ref[gi])
            ddg.append(d)
            ebuf_ref[0:tc, sl] = d * inv[gi]
        for gi, win in enumerate(POOL_WINDOWS):
            sl = slice(gi * gw, (gi + 1) * gw)
            s = -ddg[gi]
            for j in range(win):
                s = s + ebuf_ref[pl.ds(j, tc), sl]
            dx_ref[:, sl] = s.astype(BF16)
        ebuf_ref[tc:tc + H, :] = ebuf_ref[0:H, :]

    return pl.pallas_call(
        body, name="pool_bwd", grid=(nc,),
        in_specs=[pl.BlockSpec((tc, M), lambda i: (nc - 1 - i, dcol)), pl.BlockSpec((tc, M), lambda i: (nc - 1 - i, col)),
                  pl.BlockSpec((H, M), lambda i: (jnp.maximum((nc - 1 - i) * hb - 1, 0), col)), _full(pw.shape), _full((1, M))],
        out_specs=[pl.BlockSpec((tc, M), lambda i: (nc - 1 - i, 0)), _full(pw.shape), _full((8, M))],
        out_shape=[jax.ShapeDtypeStruct((T, M), BF16), jax.ShapeDtypeStruct(pw.shape, F32), jax.ShapeDtypeStruct((8, M), F32)],
        scratch_shapes=[pltpu.VMEM((tc + H, M), F32), pltpu.VMEM((tc + H, M), F32)], compiler_params=ARB,
    )(dmix, proj, proj, pw, scale.reshape(1, M))


def _ffn_mid_fwd(up, wdw, bdw):
    T, F2 = up.shape
    F = F2 // 2
    tc = _pick(T, 256, 8)
    rb = _pick(tc, 64, 16)
    H, K = SMALL_HALO, 3

    def body(gt_ref, val_ref, w_ref, b_ref, act_ref, buf_ref):
        @pl.when(pl.program_id(0) == 0)
        def _():
            buf_ref[0:H, :] = jnp.zeros((H, F), F32)

        buf_ref[H:H + tc, :] = gt_ref[...]

        def strip(s, _):
            cs = pl.ds(pl.multiple_of(s * LANES, LANES), LANES)
            taps = [w_ref[k:k + 1, cs] for k in range(K)]
            for r0 in range(0, tc, rb):
                gc = b_ref[:, cs] + taps[0] * buf_ref[pl.ds(H - (K - 1) + r0, rb), cs]
                for k in range(1, K):
                    gc = gc + taps[k] * buf_ref[pl.ds(H - (K - 1) + k + r0, rb), cs]
                act_ref[pl.ds(r0, rb), cs] = (_gelu(gc) * val_ref[pl.ds(r0, rb), cs]).astype(BF16)
            return 0

        lax.fori_loop(0, F // LANES, strip, 0)
        buf_ref[0:H, :] = buf_ref[tc:tc + H, :]

    return pl.pallas_call(
        body, name="ffn_mid_fwd", grid=(T // tc,),
        in_specs=[pl.BlockSpec((tc, F), lambda i: (i, 0)), pl.BlockSpec((tc, F), lambda i: (i, 1)), _full((8, F)), _full((1, F))],
        out_specs=pl.BlockSpec((tc, F), lambda i: (i, 0)), out_shape=jax.ShapeDtypeStruct((T, F), BF16),
        scratch_shapes=[pltpu.VMEM((tc + H, F), F32)], compiler_params=ARB,
    )(up, up, wdw, bdw.reshape(1, F))


def _ffn_mid_bwd(dact, up, wdw, bdw):
    T, F2 = up.shape
    F = F2 // 2
    tc = _pick(T, 128, 8)
    rb = _pick(tc, 64, 16)
    nc = T // tc
    H, K = SMALL_HALO, 3
    hb = tc // H

    def body(da_ref, gt_ref, val_ref, gh_ref, w_ref, b_ref, dup_ref, dw_ref, db_ref, gbuf_ref, dbuf_ref):
        i = pl.program_id(0)

        @pl.when(i == 0)
        def _():
            dw_ref[...] = jnp.zeros_like(dw_ref)
            db_ref[...] = jnp.zeros_like(db_ref)
            dbuf_ref[tc:tc + H, :] = jnp.zeros((H, F), F32)

        gbuf_ref[0:H, :] = jnp.where(i == nc - 1, 0.0, gh_ref[...])
        gbuf_ref[H:H + tc, :] = gt_ref[...]

        def strip(s, _):
            cs = pl.ds(pl.multiple_of(s * LANES, LANES), LANES)
            cs_val = pl.ds(pl.multiple_of(F + s * LANES, LANES), LANES)
            taps = [w_ref[k:k + 1, cs] for k in range(K)]
            for r0 in range(0, tc, rb):
                shifted = [gbuf_ref[pl.ds(H - (K - 1) + k + r0, rb), cs] for k in range(K)]
                gc = b_ref[:, cs] + taps[0] * shifted[0]
                for k in range(1, K):
                    gc = gc + taps[k] * shifted[k]
                ge, gg = _gelu_and_grad(gc)
                da = da_ref[pl.ds(r0, rb), cs].astype(F32)
                dup_ref[pl.ds(r0, rb), cs_val] = (da * ge).astype(BF16)
                dgc = da * val_ref[pl.ds(r0, rb), cs] * gg
                dbuf_ref[pl.ds(r0, rb), cs] = dgc
                db_ref[:, cs] += _fold8(dgc)
                for k in range(K):
                    dw_ref[k, :, cs] += _fold8(dgc * shifted[k])
            for r0 in range(0, tc, rb):
                dgt = taps[0] * dbuf_ref[pl.ds(K - 1 + r0, rb), cs]
                for k in range(1, K):
                    dgt = dgt + taps[k] * dbuf_ref[pl.ds(K - 1 - k + r0, rb), cs]
                dup_ref[pl.ds(r0, rb), cs] = dgt.astype(BF16)
            return 0

        lax.fori_loop(0, F // LANES, strip, 0)
        dbuf_ref[tc:tc + H, :] = dbuf_ref[0:H, :]

    return pl.pallas_call(
        body, name="ffn_mid_bwd", grid=(nc,),
        in_specs=[pl.BlockSpec((tc, F), lambda i: (nc - 1 - i, 0)), pl.BlockSpec((tc, F), lambda i: (nc - 1 - i, 0)),
                  pl.BlockSpec((tc, F), lambda i: (nc - 1 - i, 1)),
                  pl.BlockSpec((H, F), lambda i: (jnp.maximum((nc - 1 - i) * hb - 1, 0), 0)), _full((8, F)), _full((1, F))],
        out_specs=[pl.BlockSpec((tc, F2), lambda i: (nc - 1 - i, 0)), _full((8, 8, F)), _full((8, F))],
        out_shape=[jax.ShapeDtypeStruct((T, F2), BF16), jax.ShapeDtypeStruct((8, 8, F), F32), jax.ShapeDtypeStruct((8, F), F32)],
        scratch_shapes=[pltpu.VMEM((tc + H, F), F32), pltpu.VMEM((tc + H, F), F32)], compiler_params=ARB,
    )(dact, up, up, up, wdw, bdw.reshape(1, F))


def _my_pos():
    return lax.axis_index("x"), lax.axis_index("y"), lax.axis_index("c")


def _flip(pos, k):
    x, y, c = pos
    return ((1 - x) if k & 4 else x, (1 - y) if k & 2 else y, (1 - c) if k & 1 else c)


def _rank(pos):
    return 4 * pos[0] + 2 * pos[1] + pos[2]


def _exchange(xs, name, gather):
    n = len(xs)
    shapes = [((N_DEV,) + tuple(x.shape)) if gather else tuple(x.shape) for x in xs]

    def body(*refs):
        ins, outs = refs[:n], refs[n:2 * n]
        send_sems, recv_sems, local_sems = refs[2 * n:]
        me = _my_pos()
        mine = _rank(me)
        local, sends = [], []
        for a in range(n):
            loc = pltpu.make_async_copy(ins[a] if gather else ins[a].at[mine], outs[a].at[mine], local_sems.at[a])
            loc.start()
            local.append(loc)
            for k in range(1, N_DEV):
                peer = _flip(me, k)
                cp = pltpu.make_async_remote_copy(
                    src_ref=ins[a] if gather else ins[a].at[_rank(peer)], dst_ref=outs[a].at[mine],
                    send_sem=send_sems.at[a, k - 1], recv_sem=recv_sems.at[a, k - 1], device_id=peer,
                    device_id_type=pl.DeviceIdType.MESH)
                cp.start()
                sends.append(cp)
        for a in range(n):
            for k in range(1, N_DEV):
                peer = _flip(me, k)
                pltpu.make_async_remote_copy(
                    src_ref=ins[a] if gather else ins[a].at[_rank(peer)], dst_ref=outs[a].at[_rank(peer)],
                    send_sem=send_sems.at[a, k - 1], recv_sem=recv_sems.at[a, k - 1], device_id=peer,
                    device_id_type=pl.DeviceIdType.MESH).wait_recv()
        for cp in sends:
            cp.wait_send()
        for loc in local:
            loc.wait()

    any_spec = pl.BlockSpec(memory_space=pl.ANY)
    return pl.pallas_call(
        body, name=name, in_specs=[any_spec] * n, out_specs=[any_spec] * n,
        out_shape=[jax.ShapeDtypeStruct(s, x.dtype) for s, x in zip(shapes, xs)],
        scratch_shapes=[pltpu.SemaphoreType.DMA((n, N_DEV - 1)), pltpu.SemaphoreType.DMA((n, N_DEV - 1)),
                        pltpu.SemaphoreType.DMA((n,))],
    )(*xs)


_HBM = pl.BlockSpec(memory_space=pltpu.HBM)
_SEM = pl.BlockSpec(memory_space=pltpu.SEMAPHORE)
_EFFECT = pltpu.SideEffectType.DATAFLOW_SIDE_EFFECTING


def _landing_copy(ins, lands, send_sems, recv_sems, me, a, k, gather, outgoing):
    peer = _flip(me, k)
    return pltpu.make_async_remote_copy(
        src_ref=ins[a] if gather else ins[a].at[_rank(peer)], dst_ref=lands[a].at[_rank(me) if outgoing else _rank(peer)],
        send_sem=send_sems.at[a * (N_DEV - 1) + k - 1], recv_sem=recv_sems.at[a * (N_DEV - 1) + k - 1], device_id=peer,
        device_id_type=pl.DeviceIdType.MESH)


def _exchange_start(xs, name, gather):
    n = len(xs)
    mine = _rank(_my_pos())
    lands = []
    for x in xs:
        own = x if gather else lax.dynamic_index_in_dim(x, mine, 0, keepdims=False)
        land = lax.empty((N_DEV,) + tuple(own.shape), x.dtype)
        lands.append(lax.dynamic_update_index_in_dim(land, own, mine, 0))

    def body(*refs):
        ins, lnd = refs[:n], refs[n:2 * n]
        send_sems, recv_sems = refs[2 * n], refs[2 * n + 1]
        token = refs[4 * n + 2]
        me = _my_pos()
        for a in range(n):
            for k in range(1, N_DEV):
                _landing_copy(ins, lnd, send_sems, recv_sems, me, a, k, gather, True).start()
        token[...] = jnp.zeros_like(token)

    hbm = lambda t: pltpu.HBM(t.shape, t.dtype)
    res = pl.pallas_call(
        body, name=name,
        out_shape=(pltpu.SemaphoreType.DMA((n * (N_DEV - 1),)), pltpu.SemaphoreType.DMA((n * (N_DEV - 1),)),
                   *[hbm(x) for x in xs], *[hbm(t) for t in lands], jax.ShapeDtypeStruct((8, 128), F32)),
        in_specs=[_HBM] * (2 * n), out_specs=(_SEM, _SEM, *([_HBM] * (2 * n)), pl.BlockSpec(memory_space=pltpu.VMEM)),
        input_output_aliases={i: 2 + i for i in range(2 * n)},
        compiler_params=pltpu.CompilerParams(has_side_effects=_EFFECT),
    )(*[pltpu.with_memory_space_constraint(t, pltpu.HBM) for t in list(xs) + lands])
    return (res[0], res[1], list(res[2:2 + n]), list(res[2 + n:2 + 2 * n])), res[2 + 2 * n]


def _exchange_wait(handle, after, name, gather):
    send_sems, recv_sems, ins, lands = handle
    n = len(ins)

    def body(*refs):
        src, lnd = refs[:n], refs[n:2 * n]
        ssem, rsem = refs[2 * n], refs[2 * n + 1]
        me = _my_pos()
        for a in range(n):
            for k in range(1, N_DEV):
                cp = _landing_copy(src, lnd, ssem, rsem, me, a, k, gather, False)
                cp.wait_send()
                cp.wait_recv()

    hbm = lambda t: pltpu.HBM(t.shape, t.dtype)
    res = pl.pallas_call(
        body, name=name, out_shape=tuple(hbm(t) for t in ins + lands),
        in_specs=[_HBM] * (2 * n) + [_SEM, _SEM, pl.BlockSpec(memory_space=pl.ANY)], out_specs=tuple([_HBM] * (2 * n)),
        input_output_aliases={i: i for i in range(2 * n)},
        compiler_params=pltpu.CompilerParams(has_side_effects=_EFFECT),
    )(*ins, *lands, send_sems, recv_sems, after)
    return list(res[n:])


def _adamw_math(w, g, m, v):
    m = ADAM_B1 * m + (1.0 - ADAM_B1) * g
    v = ADAM_B2 * v + (1.0 - ADAM_B2) * (g * g)
    m_hat = m / (1.0 - ADAM_B1 ** ADAM_STEP)
    v_hat = v / (1.0 - ADAM_B2 ** ADAM_STEP)
    return -ADAM_LR * (m_hat / (jnp.sqrt(v_hat) + ADAM_EPS) + ADAM_WD * w), m, v


def _sum8(parts, *, name):
    _, R, C = parts.shape
    tr = _pick(R, 256, 16)

    def body(p_ref, o_ref):
        acc = p_ref[0].astype(F32)
        for r in range(1, N_DEV):
            acc = acc + p_ref[r].astype(F32)
        o_ref[...] = acc

    return pl.pallas_call(
        body, name=name, grid=(R // tr,), in_specs=[pl.BlockSpec((N_DEV, tr, C), lambda i: (0, i, 0))],
        out_specs=pl.BlockSpec((tr, C), lambda i: (i, 0)), out_shape=jax.ShapeDtypeStruct((R, C), F32),
        compiler_params=pltpu.CompilerParams(dimension_semantics=("parallel",)),
    )(parts)


def _adamw(w, g, m, v, *, name):
    R, C = w.shape
    tr = _pick(R, 256, 8)

    def body(w_ref, g_ref, m_ref, v_ref, d_ref, nm_ref, nv_ref):
        d, nm, nv = _adamw_math(w_ref[...], g_ref[...], m_ref[...], v_ref[...])
        d_ref[...] = d
        nm_ref[...] = nm
        nv_ref[...] = nv

    row = pl.BlockSpec((tr, C), lambda i: (i, 0))
    return pl.pallas_call(
        body, name=name, grid=(R // tr,), in_specs=[row] * 4, out_specs=[row] * 3,
        out_shape=[jax.ShapeDtypeStruct((R, C), F32)] * 3,
        compiler_params=pltpu.CompilerParams(dimension_semantics=("parallel",)),
    )(w, g, m, v)


def _adamw_stacked(gs, w, m, v, *, name):
    depth, R, C = w.shape
    tr = _pick(R, 64, 16)
    summed = gs[0].ndim == 3

    def body(*refs):
        g_refs, (w_ref, m_ref, v_ref), outs = refs[:depth], refs[depth:depth + 3], refs[depth + 3:]
        for l in range(depth):
            if summed:
                g = g_refs[l][0].astype(F32)
                for r in range(1, N_DEV):
                    g = g + g_refs[l][r].astype(F32)
            else:
                g = g_refs[l][...]
            d, nm, nv = _adamw_math(w_ref[l], g, m_ref[l], v_ref[l])
            for o, val in zip(outs, (g, d, nm, nv)):
                o[l] = val

    g_spec = pl.BlockSpec((N_DEV, tr, C), lambda i: (0, i, 0)) if summed else pl.BlockSpec((tr, C), lambda i: (i, 0))
    stk = pl.BlockSpec((depth, tr, C), lambda i: (0, i, 0))
    return pl.pallas_call(
        body, name=name, grid=(R // tr,), in_specs=[g_spec] * depth + [stk] * 3, out_specs=[stk] * 4,
        out_shape=[jax.ShapeDtypeStruct((depth, R, C), F32)] * 4,
        compiler_params=pltpu.CompilerParams(dimension_semantics=("parallel",)),
    )(*gs, w, m, v)


def _pack(arrs):
    flat = jnp.concatenate([a.reshape(-1).astype(F32) for a in arrs])
    pad = (-flat.shape[0]) % 1024
    return jnp.pad(flat, (0, pad)).reshape(-1, 128)


def _unpack(buf, shapes):
    flat, out, off = buf.reshape(-1), [], 0
    for s in shapes:
        n = math.prod(s)
        out.append(flat[off:off + n].reshape(s))
        off += n
    return out


def _s5_params(lam_re, lam_im, log_step, b_re, b_im):
    G, P, H = b_re.shape
    step = jnp.exp(log_step)[:, None]
    mag = jnp.exp(lam_re * step)
    ar, ai = mag * jnp.cos(lam_im * step), mag * jnp.sin(lam_im * step)
    den = lam_re * lam_re + lam_im * lam_im
    qr = ((ar - 1.0) * lam_re + ai * lam_im) / den
    qi = (ai * lam_re - (ar - 1.0) * lam_im) / den
    bbr = qr[..., None] * b_re - qi[..., None] * b_im
    bbi = qr[..., None] * b_im + qi[..., None] * b_re
    return ar.reshape(-1), ai.reshape(-1), _s5_blocks_in(bbr), _s5_blocks_in(bbi)


def _s5_blocks_in(b):
    G, P, H = b.shape
    eye = jnp.eye(8, dtype=b.dtype)
    return jnp.einsum("kgph,gj->kghjp", b.reshape(G // 8, 8, P, H), eye).reshape(G // 8, 8 * H, 8 * P)


def _s5_blocks_in_diag(m, G, P, H):
    return jnp.einsum("kghgp->kgph", m.reshape(G // 8, 8, H, 8, P)).reshape(G, P, H)


def _s5_blocks_out(c):
    G, H, P = c.shape
    eye = jnp.eye(8, dtype=c.dtype)
    return jnp.einsum("kghp,gj->kgpjh", c.reshape(G // 8, 8, H, P), eye).reshape(G // 8, 8 * P, 8 * H)


def _s5_blocks_out_diag(m, G, H, P):
    return jnp.einsum("kgpgh->kghp", m.reshape(G // 8, 8, P, 8, H)).reshape(G, H, P)


def _head_blocks(w):
    nh, d, e = w.shape
    return jnp.einsum("hde,hk->hdke", w, jnp.eye(nh, dtype=w.dtype)).reshape(nh * d, nh * e)


def _head_blocks_diag(m, nh, d, e):
    return jnp.einsum("hdhe->hde", m.reshape(nh, d, nh, e))


def _pad_rows(w, rows):
    return jnp.pad(w, ((0, rows - w.shape[0]), (0, 0)))


def _unshard_cols(g):
    return jnp.transpose(g, (1, 0, 2)).reshape(g.shape[1], -1)


_NAMES = ['norm_mix_g', 'w_in', 's5_lam_re', 's5_lam_im', 's5_log_step', 's5_b_re', 's5_b_im', 's5_c_re', 's5_c_im', 's5_d',
          's5_w_glu', 's5_b_glu', 'cv_w_dw', 'cv_b_dw', 'cv_ln_g', 'cv_ln_b', 'cv_w_pw', 'cv_b_pw', 'lru_w_conv', 'lru_b_conv',
          'lru_w_r', 'lru_b_r', 'lru_w_i', 'lru_b_i', 'lru_lam', 'pool_w', 'pool_scale', 'w_out', 'norm_ffn_g', 'ffn_w_up',
          'ffn_w_dw', 'ffn_b_dw', 'ffn_w_down', 'norm_final_g']
_MATRIX = ('w_in', 'w_out', 'ffn_w_up', 'ffn_w_down', 's5_w_glu', 'cv_w_pw')
_TRANSPOSED = ('w_in', 'ffn_w_up')
_COLSHARD = ('cv_w_dw', 'lru_w_conv', 'ffn_w_dw')
_GROUPS = (('w_in', 's5_w_glu', 'cv_w_pw', 'w_out'), ('ffn_w_up', 'ffn_w_down'))


def kernel(x, norm_mix_g, w_in, s5_lam_re, s5_lam_im, s5_log_step, s5_b_re, s5_b_im, s5_c_re, s5_c_im, s5_d, s5_w_glu, s5_b_glu, cv_w_dw, cv_b_dw, cv_ln_g, cv_ln_b, cv_w_pw, cv_b_pw, lru_w_conv, lru_b_conv, lru_w_r, lru_b_r, lru_w_i, lru_b_i, lru_lam, pool_w, pool_scale, w_out, norm_ffn_g, ffn_w_up, ffn_w_dw, ffn_b_dw, ffn_w_down, norm_final_g, loss_target, m_norm_mix_g, m_w_in, m_s5_lam_re, m_s5_lam_im, m_s5_log_step, m_s5_b_re, m_s5_b_im, m_s5_c_re, m_s5_c_im, m_s5_d, m_s5_w_glu, m_s5_b_glu, m_cv_w_dw, m_cv_b_dw, m_cv_ln_g, m_cv_ln_b, m_cv_w_pw, m_cv_b_pw, m_lru_w_conv, m_lru_b_conv, m_lru_w_r, m_lru_b_r, m_lru_w_i, m_lru_b_i, m_lru_lam, m_pool_w, m_pool_scale, m_w_out, m_norm_ffn_g, m_ffn_w_up, m_ffn_w_dw, m_ffn_b_dw, m_ffn_w_down, m_norm_final_g, v_norm_mix_g, v_w_in, v_s5_lam_re, v_s5_lam_im, v_s5_log_step, v_s5_b_re, v_s5_b_im, v_s5_c_re, v_s5_c_im, v_s5_d, v_s5_w_glu, v_s5_b_glu, v_cv_w_dw, v_cv_b_dw, v_cv_ln_g, v_cv_ln_b, v_cv_w_pw, v_cv_b_pw, v_lru_w_conv, v_lru_b_conv, v_lru_w_r, v_lru_b_r, v_lru_w_i, v_lru_b_i, v_lru_lam, v_pool_w, v_pool_scale, v_w_out, v_norm_ffn_g, v_ffn_w_up, v_ffn_w_dw, v_ffn_b_dw, v_ffn_w_down, v_norm_final_g):
    args = (x, norm_mix_g, w_in, s5_lam_re, s5_lam_im, s5_log_step, s5_b_re, s5_b_im, s5_c_re, s5_c_im, s5_d, s5_w_glu, s5_b_glu, cv_w_dw, cv_b_dw, cv_ln_g, cv_ln_b, cv_w_pw, cv_b_pw, lru_w_conv, lru_b_conv, lru_w_r, lru_b_r, lru_w_i, lru_b_i, lru_lam, pool_w, pool_scale, w_out, norm_ffn_g, ffn_w_up, ffn_w_dw, ffn_b_dw, ffn_w_down, norm_final_g, loss_target, m_norm_mix_g, m_w_in, m_s5_lam_re, m_s5_lam_im, m_s5_log_step, m_s5_b_re, m_s5_b_im, m_s5_c_re, m_s5_c_im, m_s5_d, m_s5_w_glu, m_s5_b_glu, m_cv_w_dw, m_cv_b_dw, m_cv_ln_g, m_cv_ln_b, m_cv_w_pw, m_cv_b_pw, m_lru_w_conv, m_lru_b_conv, m_lru_w_r, m_lru_b_r, m_lru_w_i, m_lru_b_i, m_lru_lam, m_pool_w, m_pool_scale, m_w_out, m_norm_ffn_g, m_ffn_w_up, m_ffn_w_dw, m_ffn_b_dw, m_ffn_w_down, m_norm_final_g, v_norm_mix_g, v_w_in, v_s5_lam_re, v_s5_lam_im, v_s5_log_step, v_s5_b_re, v_s5_b_im, v_s5_c_re, v_s5_c_im, v_s5_d, v_s5_w_glu, v_s5_b_glu, v_cv_w_dw, v_cv_b_dw, v_cv_ln_g, v_cv_ln_b, v_cv_w_pw, v_cv_b_pw, v_lru_w_conv, v_lru_b_conv, v_lru_w_r, v_lru_b_r, v_lru_w_i, v_lru_b_i, v_lru_lam, v_pool_w, v_pool_scale, v_w_out, v_norm_ffn_g, v_ffn_w_up, v_ffn_w_dw, v_ffn_b_dw, v_ffn_w_down, v_norm_final_g)
    x, target = args[0], args[35]
    W = dict(zip(_NAMES, args[1:35]))
    Mo = dict(zip(_NAMES, args[36:70]))
    Vo = dict(zip(_NAMES, args[70:104]))
    x = x[0]
    target = target[0]
    T, D = x.shape
    depth = W['w_in'].shape[0]
    G, P, H = W['s5_b_re'].shape[1:]
    MW = G * H
    nh, hd, _ = W['lru_w_r'].shape[1:]
    me = _rank(_my_pos())

    gather, started = {}, jnp.zeros((), F32)
    for l in range(depth):
        for gi, names in enumerate(_GROUPS):
            shards = [(W[n][l].T if n in _TRANSPOSED else W[n][l]).astype(BF16) for n in names]
            if (l, gi) == (0, 0):
                shards.append(_pack([W[n][k] for k in range(depth) for n in _COLSHARD]))
            gather[l, gi], tok = _exchange_start(shards, "gather_start_%d%d" % (l, gi), True)
            started = started + tok[0, 0]
    full = {}

    def land_weights(l, gi, after):
        got = _exchange_wait(gather[l, gi], after, "gather_wait_%d%d" % (l, gi), True)
        for n, g8 in zip(_GROUPS[gi], got):
            full[n, l] = g8.reshape(-1, g8.shape[-1])
        return got

    taps8 = land_weights(0, 0, W['norm_mix_g'][0] + started)[-1]
    tap_shapes = [W[n][l].shape for l in range(depth) for n in _COLSHARD]
    taps = [_unshard_cols(jnp.stack(t)) for t in zip(*[_unpack(taps8[r], tap_shapes) for r in range(N_DEV)])]
    for l in range(depth):
        for j, n in enumerate(_COLSHARD):
            full[n, l] = taps[l * len(_COLSHARD) + j]

    saved = []
    for l in range(depth):
        g_mix = W['norm_mix_g'][l] + started
        if l:
            land_weights(l, 0, x)
        s5p, s5p_vjp = jax.vjp(_s5_params, W['s5_lam_re'][l], W['s5_lam_im'][l], W['s5_log_step'][l], W['s5_b_re'][l], W['s5_b_im'][l])
        ar, ai, bre, bim = s5p
        cre, cim = _s5_blocks_out(W['s5_c_re'][l]), _s5_blocks_out(W['s5_c_im'][l])
        sp8, sp8_vjp = jax.vjp(lambda lam: LRU_C * jax.nn.softplus(-lam), W['lru_lam'][l])
        wr, wi = _head_blocks(W['lru_w_r'][l]), _head_blocks(W['lru_w_i'][l])
        c = dict(bre=bre.astype(BF16), bim=bim.astype(BF16), cre=cre.astype(BF16), cim=cim.astype(BF16),
                 cst_f=_s5_scan_consts(ar, ai, False), cst_b=_s5_scan_consts(ar, ai, True), sp8=sp8, wr=wr.astype(BF16),
                 wi=wi.astype(BF16), cvw=_pad_rows(full['cv_w_dw', l], 32), lruw=_pad_rows(full['lru_w_conv', l], 8),
                 ffw=_pad_rows(full['ffn_w_dw', l], 8), pw=W['pool_w'][l].astype(BF16), s5p_vjp=s5p_vjp, sp8_vjp=sp8_vjp, x0=x)
        proj, h = _norm_matmul(x, g_mix, full['w_in', l], tm=1024, tn=512, name="in_proj")
        y0, sre, sim, yp, z = _s5_fwd(proj, 0, c['bre'], c['bim'], c['cre'], c['cim'], c['cst_f'], W['s5_d'][l],
                                     full['s5_w_glu', l], W['s5_b_glu'][l])
        y1, h1 = _cv_fwd(proj, 1, 2, c['cvw'], W['cv_b_dw'][l], W['cv_ln_g'][l], W['cv_ln_b'][l], full['cv_w_pw', l], W['cv_b_pw'][l])
        y2, xc, rr, ig, hprev = _lru_fwd(proj, 3, 4, c['lruw'], W['lru_b_conv'][l], c['wr'], W['lru_b_r'][l], c['wi'],
                                         W['lru_b_i'][l], sp8)
        y3 = _pool_fwd(proj, 5, c['pw'], W['pool_scale'][l])
        mixed = jnp.concatenate([y0, y1, y2, y3], axis=1)
        x1 = _matmul(mixed, full['w_out', l], mode="nn", tm=1024, tn=512, tk=2048, out_dtype=F32, name="out_proj", add=x)
        land_weights(l, 1, x1)
        up, h2 = _norm_matmul(x1, W['norm_ffn_g'][l], full['ffn_w_up', l], tm=1024, tn=256, name="ffn_up")
        act = _ffn_mid_fwd(up, c['ffw'], W['ffn_b_dw'][l])
        F = act.shape[1]
        x = _matmul(act, full['ffn_w_down', l], mode="nn", tm=512, tn=512, tk=F, out_dtype=F32, name="ffn_down", add=x1)
        c.update(proj=proj, h=h, sre=sre, sim=sim, yp=yp, z=z, h1=h1, xc=xc, rr=rr, ig=ig, hprev=hprev, mixed=mixed, x1=x1,
                 up=up, h2=h2, act=act)
        saved.append(c)

    dx, dxb, dgf, se = _loss_head(x, W['norm_final_g'], target)
    loss = lax.psum(0.5 / D * jnp.sum(se), AXES)
    gsmall = {('norm_final_g', None): dgf.sum(0)}
    gmat, scatter = {}, {}

    def parts_of(m):
        return m.reshape(N_DEV, m.shape[0] // N_DEV, m.shape[1])

    for l in reversed(range(depth)):
        c = saved[l]
        F = c['act'].shape[1]
        dact = _matmul(dxb, full['ffn_w_down', l], mode="nt", tm=512, tn=F, tk=512, out_dtype=BF16, name="d_act")
        gmat['ffn_w_down', l] = _matmul(c['act'], dxb, mode="tn", tm=F, tn=512, tk=256, out_dtype=BF16, name="dw_down")
        dup, dffw, dffb = _ffn_mid_bwd(dact, c['up'], c['ffw'], W['ffn_b_dw'][l])
        gsmall['ffn_w_dw', l] = dffw.sum(1)[:3]
        gsmall['ffn_b_dw', l] = dffb.sum(0)
        dh2 = _matmul(dup, full['ffn_w_up', l], mode="nn", tm=512, tn=512, tk=F, out_dtype=F32, name="d_h2")
        gmat['ffn_w_up', l] = _matmul(dup, c['h2'], mode="tn", tm=256, tn=D, tk=T, out_dtype=BF16, name="dw_up")
        scatter[l, 1], tok = _exchange_start([parts_of(gmat[n, l]) for n in _GROUPS[1]], "scatter_start_%d1" % l, False)
        dx1, dx1b, dg2 = _rms_bwd(dx, dh2, c['x1'], W['norm_ffn_g'][l] + tok[0, 0], name="ffn_norm_bwd")
        gsmall['norm_ffn_g', l] = dg2.sum(0)
        dmix = _matmul(dx1b, full['w_out', l], mode="nt", tm=1024, tn=512, tk=2048, out_dtype=F32, name="d_mixed")
        gmat['w_out', l] = _matmul(c['mixed'], dx1b, mode="tn", tm=512, tn=D, tk=2048, out_dtype=BF16, name="dw_out")
        du, dbre, dbim, dcre, dcim, da, dwglu, v3 = _s5_bwd(dmix, 0, c['proj'], 0, c['sre'], c['sim'], c['yp'], c['z'], c['bre'],
                                                            c['bim'], c['cre'], c['cim'], c['cst_b'], W['s5_d'][l], full['s5_w_glu', l])
        da = da.sum(1)
        glr, gli, gls, gbr, gbi = c['s5p_vjp']((da[0], da[1], dbre, dbim))
        gsmall['s5_lam_re', l], gsmall['s5_lam_im', l], gsmall['s5_log_step', l] = glr, gli, gls
        gsmall['s5_b_re', l], gsmall['s5_b_im', l] = gbr, gbi
        gsmall['s5_c_re', l] = _s5_blocks_out_diag(dcre, G, H, P)
        gsmall['s5_c_im', l] = _s5_blocks_out_diag(dcim, G, H, P)
        gsmall['s5_b_glu', l], gsmall['s5_d', l] = v3[0].sum(0), v3[1].sum(0)
        gmat['s5_w_glu', l] = dwglu.astype(BF16)
        dv, dg, dwpw, dcvw, v5 = _cv_bwd(dmix, 1, c['proj'], 1, 2, c['h1'], c['cvw'], W['cv_ln_g'][l], W['cv_ln_b'][l], full['cv_w_pw', l])
        gmat['cv_w_pw', l] = dwpw.astype(BF16)
        gsmall['cv_w_dw', l] = dcvw.sum(1)[:31]
        gsmall['cv_b_pw', l], gsmall['cv_ln_g', l], gsmall['cv_ln_b', l], gsmall['cv_b_dw', l] = [v5[j].sum(0) for j in range(4)]
        dlx, dlg, dwr, dwi, dlw, v4 = _lru_bwd(dmix, 2, c['proj'], 3, 4, c['xc'], c['rr'], c['ig'], c['hprev'], c['lruw'], c['wr'],
                                               c['wi'], c['sp8'])
        gsmall['lru_w_r', l], gsmall['lru_w_i', l] = _head_blocks_diag(dwr, nh, hd, hd), _head_blocks_diag(dwi, nh, hd, hd)
        gsmall['lru_w_conv', l] = dlw.sum(1)[:4]
        gsmall['lru_b_r', l], gsmall['lru_b_i', l] = v4[0].sum(0), v4[1].sum(0)
        gsmall['lru_lam', l] = c['sp8_vjp'](v4[2].sum(0))[0]
        gsmall['lru_b_conv', l] = v4[3].sum(0)
        dpx, dpw, dps = _pool_bwd(dmix, 3, c['proj'], 5, c['pw'], W['pool_scale'][l])
        gsmall['pool_w', l], gsmall['pool_scale', l] = dpw, dps.sum(0)
        dproj = jnp.concatenate([du, dv, dg, dlx, dlg, dpx], axis=1)
        dh = _matmul(dproj, full['w_in', l], mode="nn", tm=1024, tn=512, tk=1024, out_dtype=F32, name="d_h")
        gmat['w_in', l] = _matmul(dproj, c['h'], mode="tn", tm=512, tn=D, tk=2048, out_dtype=BF16, name="dw_in")
        scatter[l, 0], tok = _exchange_start([parts_of(gmat[n, l]) for n in _GROUPS[0]], "scatter_start_%d0" % l, False)
        dx, dxb, dg1 = _rms_bwd(dx1, dh, c['x0'], W['norm_mix_g'][l] + tok[0, 0], name="mix_norm_bwd")
        gsmall['norm_mix_g', l] = dg1.sum(0)

    small_names = [n for n in _NAMES if n not in _MATRIX]
    small_parts = [gsmall[n, l] for n in small_names for l in ((None,) if n == 'norm_final_g' else range(depth))]
    full_shapes = [gsmall[n, None].shape if n == 'norm_final_g' else (depth,) + gsmall[n, 0].shape for n in small_names]
    small_handle, tok = _exchange_start([_pack(small_parts)], "small_grads_start", True)
    parts = {}
    for l in reversed(range(depth)):
        for gi in (1, 0):
            got = _exchange_wait(scatter[l, gi], tok, "scatter_wait_%d%d" % (l, gi), False)
            for n, p8 in zip(_GROUPS[gi], got):
                parts[n, l] = p8

    out = {}
    for n in _MATRIX:
        if n in _TRANSPOSED and W[n].shape[-1] % LANES:
            wt, mt, vt = (jnp.swapaxes(t, 1, 2) for t in (W[n], Mo[n], Vo[n]))
            res = _adamw_stacked([parts[n, l] for l in range(depth)], wt, mt, vt, name="adamw_" + n)
            out[n] = tuple(jnp.swapaxes(t, 1, 2) for t in res)
            continue
        if n in _TRANSPOSED:
            gl = [_sum8(parts[n, l], name="sum_" + n).T for l in range(depth)]
        else:
            gl = [parts[n, l] for l in range(depth)]
        out[n] = _adamw_stacked(gl, W[n], Mo[n], Vo[n], name="adamw_" + n)
    after_matrices = sum(out[n][1][0, 0, 0] for n in _MATRIX).reshape(1, 1)
    small8 = _exchange_wait(small_handle, after_matrices, "small_grads_wait", True)[0]
    small_sum = _sum8(small8, name="sum_small_grads")
    gs = dict(zip(small_names, _unpack(small_sum, full_shapes)))
    for n in _COLSHARD:
        cw = W[n].shape[-1]
        gs[n] = lax.dynamic_slice_in_dim(gs[n], me * cw, cw, axis=gs[n].ndim - 1)
    upd = _adamw(_pack([W[n] for n in small_names]), _pack([gs[n] for n in small_names]), _pack([Mo[n] for n in small_names]),
                 _pack([Vo[n] for n in small_names]), name="adamw_small")
    upd = [_unpack(u, [gs[n].shape for n in small_names]) for u in upd]
    for j, n in enumerate(small_names):
        out[n] = (gs[n], upd[0][j], upd[1][j], upd[2][j])
    return (loss, dx[None]) + tuple(out[n][q] for q in range(4) for n in _NAMES)
```

```python
import functools
import math

import jax
import jax.numpy as jnp
from jax import lax
from jax.experimental import pallas as pl
from jax.experimental.pallas import tpu as pltpu

F32 = jnp.float32
BF16 = jnp.bfloat16
N_DEV = 8
AXES = ("x", "y", "c")
EPS = 1e-6
S5_GROUP_CH = 16
S5_STATE = 64
LRU_HEADS = 8
LRU_C = 8.0
POOL_WINDOWS = (2, 4, 8, 16)
CONV_HALO = 32
SMALL_HALO = 8
POOL_HALO = 16
LANES = 128
ADAM_LR, ADAM_B1, ADAM_B2, ADAM_EPS, ADAM_WD, ADAM_STEP = 0.001, 0.9, 0.999, 1e-08, 0.01, 10
_GELU_K = math.sqrt(2.0 / math.pi)
ARB = pltpu.CompilerParams(dimension_semantics=("arbitrary",))


def _pick(n, pref, mult):
    best = None
    for t in range(mult, min(n, pref) + 1, mult):
        if n % t == 0:
            best = t
    return best if best is not None else n


def _sigmoid(x):
    return 1.0 / (1.0 + jnp.exp(-x))


def _gelu(x):
    return 0.5 * x * (1.0 + jnp.tanh(_GELU_K * (x + 0.044715 * x * x * x)))


def _gelu_grad(x):
    t = jnp.tanh(_GELU_K * (x + 0.044715 * x * x * x))
    return 0.5 * (1.0 + t) + 0.5 * x * (1.0 - t * t) * _GELU_K * (1.0 + 3.0 * 0.044715 * x * x)


def _gelu_and_grad(x):
    t = jnp.tanh(_GELU_K * (x + 0.044715 * x * x * x))
    half = 0.5 * (1.0 + t)
    return x * half, half + 0.5 * x * (1.0 - t * t) * _GELU_K * (1.0 + 3.0 * 0.044715 * x * x)


def _neg_expm1(x):
    series = -x * (1.0 + x * (0.5 + x * (1.0 / 6.0 + x * (1.0 / 24.0 + x * (1.0 / 120.0 + x * (1.0 / 720.0))))))
    return jnp.where(jnp.abs(x) < 0.25, series, 1.0 - jnp.exp(x))


def _fold8(x):
    return x.reshape(x.shape[0] // 8, 8, x.shape[1]).sum(axis=0)


def _dot(a, b):
    return jnp.dot(a.astype(BF16), b.astype(BF16), preferred_element_type=F32)


def _dot_nt(a, b):
    return lax.dot_general(a.astype(BF16), b.astype(BF16), (((1,), (1,)), ((), ())), preferred_element_type=F32)


def _dot_tn(a, b):
    return lax.dot_general(a.astype(BF16), b.astype(BF16), (((0,), (0,)), ((), ())), preferred_element_type=F32)


def _full(shape):
    nd = len(shape)
    return pl.BlockSpec(shape, lambda *i: (0,) * nd)


def _matmul(a, b, *, mode, tm, tn, tk, out_dtype, name, add=None):
    if mode == "nn":
        (M, K), N = a.shape, b.shape[1]
    elif mode == "nt":
        (M, K), N = a.shape, b.shape[0]
    else:
        (K, M), N = a.shape, b.shape[1]
    tm, tn, tk = _pick(M, tm, 8), _pick(N, tn, 128), _pick(K, tk, 128)
    if mode == "tn":
        tm = _pick(M, tm, 128)
    nk = K // tk
    a_spec = {"nn": pl.BlockSpec((tm, tk), lambda i, j, k: (i, k)), "nt": pl.BlockSpec((tm, tk), lambda i, j, k: (i, k)),
              "tn": pl.BlockSpec((tk, tm), lambda i, j, k: (k, i))}[mode]
    b_spec = {"nn": pl.BlockSpec((tk, tn), lambda i, j, k: (k, j)), "nt": pl.BlockSpec((tn, tk), lambda i, j, k: (j, k)),
              "tn": pl.BlockSpec((tk, tn), lambda i, j, k: (k, j))}[mode]
    o_spec = pl.BlockSpec((tm, tn), lambda i, j, k: (i, j))
    dot = {"nn": _dot, "nt": _dot_nt, "tn": _dot_tn}[mode]
    has_add = add is not None

    def body(*refs):
        if has_add:
            a_ref, b_ref, add_ref, o_ref, acc_ref = refs
        else:
            a_ref, b_ref, o_ref, acc_ref = refs
        k = pl.program_id(2)

        @pl.when(k == 0)
        def _():
            acc_ref[...] = jnp.zeros_like(acc_ref)

        acc_ref[...] += dot(a_ref[...], b_ref[...])

        @pl.when(k == nk - 1)
        def _():
            r = acc_ref[...]
            if has_add:
                r = r + add_ref[...]
            o_ref[...] = r.astype(out_dtype)

    ins = [a, b] + ([add] if has_add else [])
    specs = [a_spec, b_spec] + ([o_spec] if has_add else [])
    return pl.pallas_call(
        body, name=name, grid=(M // tm, N // tn, nk), in_specs=specs, out_specs=o_spec,
        out_shape=jax.ShapeDtypeStruct((M, N), out_dtype), scratch_shapes=[pltpu.VMEM((tm, tn), F32)],
        compiler_params=pltpu.CompilerParams(dimension_semantics=("parallel", "parallel", "arbitrary")),
    )(*ins)


def _norm_matmul(x, g, wt, *, tm, tn, name):
    T, D = x.shape
    N = wt.shape[0]
    tm, tn = _pick(T, tm, 8), _pick(N, tn, 128)

    def body(x_ref, g_ref, w_ref, o_ref, h_ref):
        @pl.when(pl.program_id(1) == 0)
        def _():
            xv = x_ref[...]
            r = lax.rsqrt(jnp.mean(xv * xv, axis=-1, keepdims=True) + EPS)
            h_ref[...] = (xv * r * g_ref[...]).astype(BF16)

        o_ref[...] = _dot_nt(h_ref[...], w_ref[...])

    return pl.pallas_call(
        body, name=name, grid=(T // tm, N // tn),
        in_specs=[pl.BlockSpec((tm, D), lambda i, j: (i, 0)), pl.BlockSpec((1, D), lambda i, j: (0, 0)),
                  pl.BlockSpec((tn, D), lambda i, j: (j, 0))],
        out_specs=[pl.BlockSpec((tm, tn), lambda i, j: (i, j)), pl.BlockSpec((tm, D), lambda i, j: (i, 0))],
        out_shape=[jax.ShapeDtypeStruct((T, N), F32), jax.ShapeDtypeStruct((T, D), BF16)],
        compiler_params=pltpu.CompilerParams(dimension_semantics=("parallel", "arbitrary")),
    )(x, g.reshape(1, D), wt)


def _rms_bwd(dres, dh, x, g, *, name):
    T, D = x.shape
    tc = _pick(T, 256, 8)

    def body(dres_ref, dh_ref, x_ref, g_ref, dx_ref, dxb_ref, dg_ref):
        @pl.when(pl.program_id(0) == 0)
        def _():
            dg_ref[...] = jnp.zeros_like(dg_ref)

        xv, dhv = x_ref[...], dh_ref[...]
        r = lax.rsqrt(jnp.mean(xv * xv, axis=-1, keepdims=True) + EPS)
        dg_ref[...] += _fold8(dhv * xv * r)
        dyg = dhv * g_ref[...]
        dx = dres_ref[...] + r * dyg - xv * (r * r * r) * jnp.mean(dyg * xv, axis=-1, keepdims=True)
        dx_ref[...] = dx
        dxb_ref[...] = dx.astype(BF16)

    row = pl.BlockSpec((tc, D), lambda i: (i, 0))
    return pl.pallas_call(
        body, name=name, grid=(T // tc,), in_specs=[row, row, row, _full((1, D))],
        out_specs=[row, row, _full((8, D))],
        out_shape=[jax.ShapeDtypeStruct((T, D), F32), jax.ShapeDtypeStruct((T, D), BF16), jax.ShapeDtypeStruct((8, D), F32)],
        compiler_params=ARB,
    )(dres, dh, x, g.reshape(1, D))


def _loss_head(x, g, target):
    T, D = x.shape
    tc = _pick(T, 256, 8)

    def body(x_ref, g_ref, t_ref, dx_ref, dxb_ref, dg_ref, se_ref):
        @pl.when(pl.program_id(0) == 0)
        def _():
            dg_ref[...] = jnp.zeros_like(dg_ref)
            se_ref[...] = jnp.zeros_like(se_ref)

        xv = x_ref[...]
        r = lax.rsqrt(jnp.mean(xv * xv, axis=-1, keepdims=True) + EPS)
        gv = g_ref[...]
        err = xv * r * gv - t_ref[...]
        se_ref[...] += _fold8(err * err)
        dy = err * (1.0 / D)
        dg_ref[...] += _fold8(dy * xv * r)
        dyg = dy * gv
        dx = r * dyg - xv * (r * r * r) * jnp.mean(dyg * xv, axis=-1, keepdims=True)
        dx_ref[...] = dx
        dxb_ref[...] = dx.astype(BF16)

    row = pl.BlockSpec((tc, D), lambda i: (i, 0))
    return pl.pallas_call(
        body, name="loss_head", grid=(T // tc,), in_specs=[row, _full((1, D)), row],
        out_specs=[row, row, _full((8, D)), _full((8, D))],
        out_shape=[jax.ShapeDtypeStruct((T, D), F32), jax.ShapeDtypeStruct((T, D), BF16),
                   jax.ShapeDtypeStruct((8, D), F32), jax.ShapeDtypeStruct((8, D), F32)],
        compiler_params=ARB,
    )(x, g.reshape(1, D), target)


def _s5_scan_consts(ar, ai, reverse):
    if reverse:
        ai = -ai
    pr, pi_ = [ar], [ai]
    for _ in range(7):
        pr, pi_ = pr + [pr[-1] * ar - pi_[-1] * ai], pi_ + [pr[-1] * ai + pi_[-1] * ar]
    rows = jnp.arange(8)[:, None]
    planes = []
    for d in (1, 2, 4):
        mask = (rows <= 7 - d) if reverse else (rows >= d)
        planes += [jnp.where(mask, pr[d - 1][None, :], 0.0), jnp.where(mask, pi_[d - 1][None, :], 0.0)]
    order = list(range(7, -1, -1)) if reverse else list(range(8))
    planes += [jnp.stack([pr[k] for k in order]), jnp.stack([pi_[k] for k in order])]
    return jnp.stack(planes).astype(F32)


def _s5_block_scan(xr_ref, xi_ref, cst_ref, car_ref, nblk, reverse, lb, extra=None):
    ns = xr_ref.shape[1]
    crow = 0 if reverse else 7

    def blk(i, _):
        ib = (nblk - 1 - i) if reverse else i
        r0 = pl.multiple_of(ib * 8, 8)
        for k in range(ns // lb):
            sl = slice(k * lb, (k + 1) * lb)
            xr, xi = xr_ref[pl.ds(r0, 8), sl], xi_ref[pl.ds(r0, 8), sl]
            for n, d in enumerate((1, 2, 4)):
                ar, ai = cst_ref[2 * n, :, sl], cst_ref[2 * n + 1, :, sl]
                sh = (8 - d) if reverse else d
                rr, ri = pltpu.roll(xr, sh, 0), pltpu.roll(xi, sh, 0)
                xr, xi = xr + ar * rr - ai * ri, xi + ar * ri + ai * rr
            pr, pi_ = cst_ref[6, :, sl], cst_ref[7, :, sl]
            cr, ci = car_ref[0, crow:crow + 1, sl], car_ref[1, crow:crow + 1, sl]
            xr, xi = xr + pr * cr - pi_ * ci, xi + pr * ci + pi_ * cr
            xr_ref[pl.ds(r0, 8), sl] = xr
            xi_ref[pl.ds(r0, 8), sl] = xi
            if extra is not None:
                extra(r0, sl, xr, xi, cr, ci)
            car_ref[0, :, sl] = xr
            car_ref[1, :, sl] = xi
        return 0

    lax.fori_loop(0, nblk, blk, 0)


def _s5_fwd(proj, col, bre, bim, cre, cim, cst, dvec, wglu, bglu):
    T = proj.shape[0]
    nsb, cw, sw = bre.shape
    M, NS = nsb * cw, nsb * sw
    tc = _pick(T, 256, 8)

    def body(u_ref, bre_ref, bim_ref, cre_ref, cim_ref, cst_ref, d_ref, wg_ref, bg_ref,
             y_ref, sre_ref, sim_ref, yp_ref, z_ref, car_ref):
        @pl.when(pl.program_id(0) == 0)
        def _():
            car_ref[...] = jnp.zeros_like(car_ref)

        u = u_ref[...]
        ub = u.astype(BF16)
        for k in range(nsb):
            sre_ref[:, k * sw:(k + 1) * sw] = _dot(ub[:, k * cw:(k + 1) * cw], bre_ref[k])
            sim_ref[:, k * sw:(k + 1) * sw] = _dot(ub[:, k * cw:(k + 1) * cw], bim_ref[k])
        _s5_block_scan(sre_ref, sim_ref, cst_ref, car_ref, tc // 8, False, sw)
        for k in range(nsb):
            yk = _dot(sre_ref[:, k * sw:(k + 1) * sw], cre_ref[k]) - _dot(sim_ref[:, k * sw:(k + 1) * sw], cim_ref[k])
            yp_ref[:, k * cw:(k + 1) * cw] = yk + d_ref[:, k * cw:(k + 1) * cw] * u[:, k * cw:(k + 1) * cw]
        gl = _gelu(yp_ref[...])
        z = _dot(gl, wg_ref[...]) + bg_ref[...]
        z_ref[...] = z
        y_ref[...] = (gl * _sigmoid(z)).astype(BF16)

    rowm = pl.BlockSpec((tc, M), lambda i: (i, 0))
    rows = pl.BlockSpec((tc, NS), lambda i: (i, 0))
    return pl.pallas_call(
        body, name="s5_fwd", grid=(T // tc,),
        in_specs=[pl.BlockSpec((tc, M), lambda i: (i, col)), _full(bre.shape), _full(bim.shape), _full(cre.shape),
                  _full(cim.shape), _full(cst.shape), _full((1, M)), _full((M, M)), _full((1, M))],
        out_specs=[rowm, rows, rows, rowm, rowm],
        out_shape=[jax.ShapeDtypeStruct((T, M), BF16), jax.ShapeDtypeStruct((T, NS), F32), jax.ShapeDtypeStruct((T, NS), F32),
                   jax.ShapeDtypeStruct((T, M), F32), jax.ShapeDtypeStruct((T, M), F32)],
        scratch_shapes=[pltpu.VMEM((2, 8, NS), F32)], compiler_params=ARB,
    )(proj, bre, bim, cre, cim, cst, dvec.reshape(1, M), wglu, bglu.reshape(1, M))


def _s5_bwd(dmix, dcol, proj, col, sre, sim, yp, z, bre, bim, cre, cim, cst, dvec, wglu):
    T = proj.shape[0]
    nsb, cw, sw = bre.shape
    M, NS = nsb * cw, nsb * sw
    tc = _pick(T, 256, 8)
    nc = T // tc

    def body(do_ref, u_ref, sre_ref, sim_ref, yp_ref, z_ref, bre_ref, bim_ref, cre_ref, cim_ref, cst_ref, d_ref, wg_ref,
             du_ref, dbre_ref, dbim_ref, dcre_ref, dcim_ref, da_ref, dwg_ref, vec_ref, gre_ref, gim_ref, car_ref):
        @pl.when(pl.program_id(0) == 0)
        def _():
            for r in (dbre_ref, dbim_ref, dcre_ref, dcim_ref, da_ref, dwg_ref, vec_ref, car_ref):
                r[...] = jnp.zeros_like(r)

        u, do, ypv = u_ref[...], do_ref[...], yp_ref[...]
        gl = _gelu(ypv)
        sg = _sigmoid(z_ref[...])
        dz = do * gl * sg * (1.0 - sg)
        dgl = do * sg + _dot_nt(dz, wg_ref[...])
        dwg_ref[...] += _dot_tn(gl, dz)
        vec_ref[0] += _fold8(dz)
        dy = dgl * _gelu_grad(ypv)
        vec_ref[1] += _fold8(dy * u)
        dyb = dy.astype(BF16)
        for k in range(nsb):
            cs, ss = slice(k * cw, (k + 1) * cw), slice(k * sw, (k + 1) * sw)
            gre_ref[:, ss] = _dot_nt(dyb[:, cs], cre_ref[k])
            gim_ref[:, ss] = -_dot_nt(dyb[:, cs], cim_ref[k])
            dcre_ref[k] += _dot_tn(sre_ref[:, ss], dyb[:, cs])
            dcim_ref[k] -= _dot_tn(sim_ref[:, ss], dyb[:, cs])

        row = lax.broadcasted_iota(jnp.int32, (8, sw), 0)

        def extra(r0, sl, gr, gi, cr, ci):
            nr = jnp.where(row == 7, cr, pltpu.roll(gr, 7, 0))
            ni = jnp.where(row == 7, ci, pltpu.roll(gi, 7, 0))
            sr, si = sre_ref[pl.ds(r0, 8), sl], sim_ref[pl.ds(r0, 8), sl]
            da_ref[0, :, sl] += nr * sr + ni * si
            da_ref[1, :, sl] += ni * sr - nr * si

        _s5_block_scan(gre_ref, gim_ref, cst_ref, car_ref, tc // 8, True, sw, extra)
        ub = u.astype(BF16)
        for k in range(nsb):
            cs, ss = slice(k * cw, (k + 1) * cw), slice(k * sw, (k + 1) * sw)
            gr, gi = gre_ref[:, ss].astype(BF16), gim_ref[:, ss].astype(BF16)
            duk = dy[:, cs] * d_ref[:, cs] + _dot_nt(gr, bre_ref[k]) + _dot_nt(gi, bim_ref[k])
            du_ref[:, cs] = duk.astype(BF16)
            dbre_ref[k] += _dot_tn(ub[:, cs], gr)
            dbim_ref[k] += _dot_tn(ub[:, cs], gi)

    def rowm(c):
        return pl.BlockSpec((tc, M), lambda i: (nc - 1 - i, c))
    rows = pl.BlockSpec((tc, NS), lambda i: (nc - 1 - i, 0))
    return pl.pallas_call(
        body, name="s5_bwd", grid=(nc,),
        in_specs=[rowm(dcol), rowm(col), rows, rows, rowm(0), rowm(0), _full(bre.shape), _full(bim.shape), _full(cre.shape),
                  _full(cim.shape), _full(cst.shape), _full((1, M)), _full((M, M))],
        out_specs=[rowm(0), _full(bre.shape), _full(bim.shape), _full(cre.shape), _full(cim.shape), _full((2, 8, NS)),
                   _full((M, M)), _full((3, 8, M))],
        out_shape=[jax.ShapeDtypeStruct((T, M), BF16), jax.ShapeDtypeStruct(bre.shape, F32), jax.ShapeDtypeStruct(bim.shape, F32),
                   jax.ShapeDtypeStruct(cre.shape, F32), jax.ShapeDtypeStruct(cim.shape, F32), jax.ShapeDtypeStruct((2, 8, NS), F32),
                   jax.ShapeDtypeStruct((M, M), F32), jax.ShapeDtypeStruct((3, 8, M), F32)],
        scratch_shapes=[pltpu.VMEM((tc, NS), F32), pltpu.VMEM((tc, NS), F32), pltpu.VMEM((2, 8, NS), F32)],
        compiler_params=ARB,
    )(dmix, proj, sre, sim, yp, z, bre, bim, cre, cim, cst, dvec.reshape(1, M), wglu)


def _ln_silu(h1, lg, lb):
    mu = jnp.mean(h1, axis=-1, keepdims=True)
    xc = h1 - mu
    rstd = lax.rsqrt(jnp.mean(xc * xc, axis=-1, keepdims=True) + EPS)
    xhat = xc * rstd
    h2 = xhat * lg + lb
    return xhat, rstd, h2, _sigmoid(h2)


def _cv_fwd(proj, vcol, gcol, wdw, bdw, lg, lb, wpw, bpw):
    T = proj.shape[0]
    M = wpw.shape[0]
    tc = _pick(T, 256, 8)
    H, K = CONV_HALO, 31

    def body(v_ref, g_ref, wdw_ref, bdw_ref, lg_ref, lb_ref, wpw_ref, bpw_ref, y_ref, h1_ref, buf_ref):
        @pl.when(pl.program_id(0) == 0)
        def _():
            buf_ref[0:H, :] = jnp.zeros((H, M), F32)

        buf_ref[H:H + tc, :] = v_ref[...] * _sigmoid(g_ref[...])
        acc = jnp.zeros((tc, M), F32) + bdw_ref[...]
        for k in range(K):
            acc = acc + wdw_ref[k:k + 1, :] * buf_ref[pl.ds(H - (K - 1) + k, tc), :]
        h1_ref[...] = acc
        _, _, h2, sg = _ln_silu(acc, lg_ref[...], lb_ref[...])
        y_ref[...] = (_dot(h2 * sg, wpw_ref[...]) + bpw_ref[...]).astype(BF16)
        buf_ref[0:H, :] = buf_ref[tc:tc + H, :]

    def rowm(c):
        return pl.BlockSpec((tc, M), lambda i: (i, c))
    vec = _full((1, M))
    return pl.pallas_call(
        body, name="cv_fwd", grid=(T // tc,),
        in_specs=[rowm(vcol), rowm(gcol), _full((32, M)), vec, vec, vec, _full((M, M)), vec],
        out_specs=[rowm(0), rowm(0)],
        out_shape=[jax.ShapeDtypeStruct((T, M), BF16), jax.ShapeDtypeStruct((T, M), F32)],
        scratch_shapes=[pltpu.VMEM((tc + H, M), F32)], compiler_params=ARB,
    )(proj, proj, wdw, bdw.reshape(1, M), lg.reshape(1, M), lb.reshape(1, M), wpw, bpw.reshape(1, M))


def _cv_bwd(dmix, dcol, proj, vcol, gcol, h1, wdw, lg, lb, wpw):
    T = proj.shape[0]
    M = wpw.shape[0]
    tc = _pick(T, 256, 8)
    nc = T // tc
    H, K = CONV_HALO, 31
    hb = tc // H

    def body(do_ref, v_ref, g_ref, vh_ref, gh_ref, h1_ref, wdw_ref, lg_ref, lb_ref, wpw_ref,
             dv_ref, dg_ref, dwpw_ref, dwdw_ref, vec_ref, hbuf_ref, dbuf_ref):
        i = pl.program_id(0)

        @pl.when(i == 0)
        def _():
            for r in (dwpw_ref, dwdw_ref, vec_ref):
                r[...] = jnp.zeros_like(r)
            dbuf_ref[tc:tc + H, :] = jnp.zeros((H, M), F32)

        v, g = v_ref[...], g_ref[...]
        sgg = _sigmoid(g)
        halo = vh_ref[...] * _sigmoid(gh_ref[...])
        hbuf_ref[0:H, :] = jnp.where(i == nc - 1, 0.0, halo)
        hbuf_ref[H:H + tc, :] = v * sgg
        do = do_ref[...]
        xhat, rstd, h2, sg = _ln_silu(h1_ref[...], lg_ref[...], lb_ref[...])
        dwpw_ref[...] += _dot_tn(h2 * sg, do)
        vec_ref[0] += _fold8(do)
        dh2 = _dot_nt(do, wpw_ref[...]) * (sg * (1.0 + h2 * (1.0 - sg)))
        vec_ref[1] += _fold8(dh2 * xhat)
        vec_ref[2] += _fold8(dh2)
        dxh = dh2 * lg_ref[...]
        dh1 = rstd * (dxh - jnp.mean(dxh, axis=-1, keepdims=True) - xhat * jnp.mean(dxh * xhat, axis=-1, keepdims=True))
        vec_ref[3] += _fold8(dh1)
        dbuf_ref[0:tc, :] = dh1
        dh0 = jnp.zeros((tc, M), F32)
        for k in range(K):
            dh0 = dh0 + wdw_ref[k:k + 1, :] * dbuf_ref[pl.ds(K - 1 - k, tc), :]
            dwdw_ref[k] += _fold8(dh1 * hbuf_ref[pl.ds(H - (K - 1) + k, tc), :])
        dv_ref[...] = (dh0 * sgg).astype(BF16)
        dg_ref[...] = (dh0 * v * sgg * (1.0 - sgg)).astype(BF16)
        dbuf_ref[tc:tc + H, :] = dbuf_ref[0:H, :]

    def rowm(c):
        return pl.BlockSpec((tc, M), lambda i: (nc - 1 - i, c))

    def halo(c):
        return pl.BlockSpec((H, M), lambda i: (jnp.maximum((nc - 1 - i) * hb - 1, 0), c))
    vec = _full((1, M))
    return pl.pallas_call(
        body, name="cv_bwd", grid=(nc,),
        in_specs=[rowm(dcol), rowm(vcol), rowm(gcol), halo(vcol), halo(gcol), rowm(0), _full((32, M)), vec, vec, _full((M, M))],
        out_specs=[rowm(0), rowm(0), _full((M, M)), _full((32, 8, M)), _full((5, 8, M))],
        out_shape=[jax.ShapeDtypeStruct((T, M), BF16), jax.ShapeDtypeStruct((T, M), BF16), jax.ShapeDtypeStruct((M, M), F32),
                   jax.ShapeDtypeStruct((32, 8, M), F32), jax.ShapeDtypeStruct((5, 8, M), F32)],
        scratch_shapes=[pltpu.VMEM((tc + H, M), F32), pltpu.VMEM((tc + H, M), F32)], compiler_params=ARB,
    )(dmix, proj, proj, proj, proj, h1, wdw, lg.reshape(1, M), lb.reshape(1, M), wpw)


def _lru_gates(xc, r, ig, sp8):
    la = -r * sp8
    a = jnp.exp(la)
    mult = jnp.sqrt(_neg_expm1(2.0 * la))
    return a, mult, mult * (ig * xc)


def _lru_fwd(proj, xcol, gcol, wcv, bcv, wr, br, wi, bi, sp8):
    T = proj.shape[0]
    M = wr.shape[0]
    tc = _pick(T, 256, 8)
    H, K = SMALL_HALO, 4

    def body(x_ref, g_ref, wcv_ref, bcv_ref, wr_ref, br_ref, wi_ref, bi_ref, sp_ref,
             y_ref, xc_ref, r_ref, i_ref, hp_ref, buf_ref, a_ref, car_ref):
        @pl.when(pl.program_id(0) == 0)
        def _():
            buf_ref[0:H, :] = jnp.zeros((H, M), F32)
            car_ref[...] = jnp.zeros_like(car_ref)

        buf_ref[H:H + tc, :] = x_ref[...]
        xc = jnp.zeros((tc, M), F32) + bcv_ref[...]
        for k in range(K):
            xc = xc + wcv_ref[k:k + 1, :] * buf_ref[pl.ds(H - (K - 1) + k, tc), :]
        buf_ref[0:H, :] = buf_ref[tc:tc + H, :]
        r = _sigmoid(_dot(xc, wr_ref[...]) + br_ref[...])
        ig = _sigmoid(_dot(xc, wi_ref[...]) + bi_ref[...])
        xc_ref[...] = xc
        r_ref[...] = r
        i_ref[...] = ig
        a, _, bt = _lru_gates(xc, r, ig, sp_ref[...])
        a_ref[...] = a
        hp_ref[...] = bt
        row = lax.broadcasted_iota(jnp.int32, (8, M), 0)

        def blk(ib, _):
            r0 = pl.multiple_of(ib * 8, 8)
            av, bv = a_ref[pl.ds(r0, 8), :], hp_ref[pl.ds(r0, 8), :]
            for d in (1, 2, 4):
                m = row >= d
                bv = jnp.where(m, bv + av * pltpu.roll(bv, d, 0), bv)
                av = jnp.where(m, av * pltpu.roll(av, d, 0), av)
            cr = car_ref[7:8, :]
            h = bv + av * cr
            a_ref[pl.ds(r0, 8), :] = h
            hp_ref[pl.ds(r0, 8), :] = jnp.where(row == 0, cr, pltpu.roll(h, 1, 0))
            car_ref[...] = h
            return 0

        lax.fori_loop(0, tc // 8, blk, 0)
        y_ref[...] = (a_ref[...] * _gelu(g_ref[...])).astype(BF16)

    def rowm(c):
        return pl.BlockSpec((tc, M), lambda i: (i, c))
    vec = _full((1, M))
    return pl.pallas_call(
        body, name="lru_fwd", grid=(T // tc,),
        in_specs=[rowm(xcol), rowm(gcol), _full((8, M)), vec, _full((M, M)), vec, _full((M, M)), vec, vec],
        out_specs=[rowm(0)] * 5,
        out_shape=[jax.ShapeDtypeStruct((T, M), BF16)] + [jax.ShapeDtypeStruct((T, M), F32)] * 4,
        scratch_shapes=[pltpu.VMEM((tc + H, M), F32), pltpu.VMEM((tc, M), F32), pltpu.VMEM((8, M), F32)], compiler_params=ARB,
    )(proj, proj, wcv, bcv.reshape(1, M), wr, br.reshape(1, M), wi, bi.reshape(1, M), sp8.reshape(1, M))


def _lru_bwd(dmix, dcol, proj, xcol, gcol, xc, r, ig, hprev, wcv, wr, wi, sp8):
    T = proj.shape[0]
    M = wr.shape[0]
    tc = _pick(T, 256, 8)
    nc = T // tc
    H, K = SMALL_HALO, 4
    hb = tc // H

    def body(do_ref, x_ref, g_ref, xh_ref, xc_ref, r_ref, i_ref, hp_ref, wcv_ref, wr_ref, wi_ref, sp_ref,
             dx_ref, dg_ref, dwr_ref, dwi_ref, dwcv_ref, vec_ref, xbuf_ref, dbuf_ref, a_ref, gs_ref, car_ref):
        i = pl.program_id(0)

        @pl.when(i == 0)
        def _():
            for rf in (dwr_ref, dwi_ref, dwcv_ref, vec_ref, car_ref):
                rf[...] = jnp.zeros_like(rf)
            dbuf_ref[tc:tc + H, :] = jnp.zeros((H, M), F32)

        xcv, rv, iv, hpv, gv, do = xc_ref[...], r_ref[...], i_ref[...], hp_ref[...], g_ref[...], do_ref[...]
        sp = sp_ref[...]
        a, mult, bt = _lru_gates(xcv, rv, iv, sp)
        h = a * hpv + bt
        ge = _gelu(gv)
        dg_ref[...] = (do * h * _gelu_grad(gv)).astype(BF16)
        a_ref[...] = a
        gs_ref[...] = do * ge
        row = lax.broadcasted_iota(jnp.int32, (8, M), 0)

        def blk(n, _):
            r0 = pl.multiple_of((tc // 8 - 1 - n) * 8, 8)
            ablk = a_ref[pl.ds(r0, 8), :]
            av = jnp.where(row == 7, car_ref[0, 0:1, :], pltpu.roll(ablk, 7, 0))
            bv = gs_ref[pl.ds(r0, 8), :]
            for d in (1, 2, 4):
                m = row <= 7 - d
                bv = jnp.where(m, bv + av * pltpu.roll(bv, 8 - d, 0), bv)
                av = jnp.where(m, av * pltpu.roll(av, 8 - d, 0), av)
            gsv = bv + av * car_ref[1, 0:1, :]
            gs_ref[pl.ds(r0, 8), :] = gsv
            car_ref[0] = ablk
            car_ref[1] = gsv
            return 0

        lax.fori_loop(0, tc // 8, blk, 0)
        gs = gs_ref[...]
        dixc = gs * mult
        dla = gs * hpv * a - gs * (iv * xcv) * (a * a) / mult
        vec_ref[2] += _fold8(-dla * rv)
        dpr = (-dla * sp) * rv * (1.0 - rv)
        dpi = (dixc * xcv) * iv * (1.0 - iv)
        vec_ref[0] += _fold8(dpr)
        vec_ref[1] += _fold8(dpi)
        dwr_ref[...] += _dot_tn(xcv, dpr)
        dwi_ref[...] += _dot_tn(xcv, dpi)
        dxc = dixc * iv + _dot_nt(dpr, wr_ref[...]) + _dot_nt(dpi, wi_ref[...])
        vec_ref[3] += _fold8(dxc)
        xbuf_ref[0:H, :] = jnp.where(i == nc - 1, 0.0, xh_ref[...])
        xbuf_ref[H:H + tc, :] = x_ref[...]
        dbuf_ref[0:tc, :] = dxc
        dx = jnp.zeros((tc, M), F32)
        for k in range(K):
            dx = dx + wcv_ref[k:k + 1, :] * dbuf_ref[pl.ds(K - 1 - k, tc), :]
            dwcv_ref[k] += _fold8(dxc * xbuf_ref[pl.ds(H - (K - 1) + k, tc), :])
        dx_ref[...] = dx.astype(BF16)
        dbuf_ref[tc:tc + H, :] = dbuf_ref[0:H, :]

    def rowm(c):
        return pl.BlockSpec((tc, M), lambda i: (nc - 1 - i, c))
    halo = pl.BlockSpec((H, M), lambda i: (jnp.maximum((nc - 1 - i) * hb - 1, 0), xcol))
    vec = _full((1, M))
    return pl.pallas_call(
        body, name="lru_bwd", grid=(nc,),
        in_specs=[rowm(dcol), rowm(xcol), rowm(gcol), halo, rowm(0), rowm(0), rowm(0), rowm(0), _full((8, M)),
                  _full((M, M)), _full((M, M)), vec],
        out_specs=[rowm(0), rowm(0), _full((M, M)), _full((M, M)), _full((8, 8, M)), _full((4, 8, M))],
        out_shape=[jax.ShapeDtypeStruct((T, M), BF16), jax.ShapeDtypeStruct((T, M), BF16), jax.ShapeDtypeStruct((M, M), F32),
                   jax.ShapeDtypeStruct((M, M), F32), jax.ShapeDtypeStruct((8, 8, M), F32), jax.ShapeDtypeStruct((4, 8, M), F32)],
        scratch_shapes=[pltpu.VMEM((tc + H, M), F32), pltpu.VMEM((tc + H, M), F32), pltpu.VMEM((tc, M), F32),
                        pltpu.VMEM((tc, M), F32), pltpu.VMEM((2, 8, M), F32)],
        compiler_params=ARB,
    )(dmix, proj, proj, proj, xc, r, ig, hprev, wcv, wr, wi, sp8.reshape(1, M))


def _pool_diffs(buf_ref, x, t0, tc, gw):
    H = POOL_HALO
    pos = (lax.broadcasted_iota(jnp.int32, (tc, gw), 0) + t0 + 1).astype(F32)
    out, inv = [], []
    for gi, win in enumerate(POOL_WINDOWS):
        sl = slice(gi * gw, (gi + 1) * gw)
        s = x[:, sl]
        for j in range(1, win):
            s = s + buf_ref[pl.ds(H - j, tc), sl]
        ic = 1.0 / jnp.minimum(pos, float(win))
        out.append(s * ic - x[:, sl])
        inv.append(ic)
    return out, inv


def _pool_fwd(proj, col, pw, scale):
    T = proj.shape[0]
    ng, gw, _ = pw.shape
    M = ng * gw
    tc = _pick(T, 256, 8)
    H = POOL_HALO

    def body(x_ref, pw_ref, sc_ref, y_ref, buf_ref):
        @pl.when(pl.program_id(0) == 0)
        def _():
            buf_ref[0:H, :] = jnp.zeros((H, M), F32)

        x = x_ref[...]
        buf_ref[H:H + tc, :] = x
        diffs, _ = _pool_diffs(buf_ref, x, pl.program_id(0) * tc, tc, gw)
        for gi in range(ng):
            sl = slice(gi * gw, (gi + 1) * gw)
            y_ref[:, sl] = (_dot(diffs[gi], pw_ref[gi]) * sc_ref[:, sl]).astype(BF16)
        buf_ref[0:H, :] = buf_ref[tc:tc + H, :]

    return pl.pallas_call(
        body, name="pool_fwd", grid=(T // tc,),
        in_specs=[pl.BlockSpec((tc, M), lambda i: (i, col)), _full(pw.shape), _full((1, M))],
        out_specs=pl.BlockSpec((tc, M), lambda i: (i, 0)), out_shape=jax.ShapeDtypeStruct((T, M), BF16),
        scratch_shapes=[pltpu.VMEM((tc + H, M), F32)], compiler_params=ARB,
    )(proj, pw, scale.reshape(1, M))


def _pool_bwd(dmix, dcol, proj, col, pw, scale):
    T = proj.shape[0]
    ng, gw, _ = pw.shape
    M = ng * gw
    tc = _pick(T, 256, 8)
    nc = T // tc
    H = POOL_HALO
    hb = tc // H

    def body(do_ref, x_ref, xh_ref, pw_ref, sc_ref, dx_ref, dpw_ref, dsc_ref, buf_ref, ebuf_ref):
        i = pl.program_id(0)

        @pl.when(i == 0)
        def _():
            dpw_ref[...] = jnp.zeros_like(dpw_ref)
            dsc_ref[...] = jnp.zeros_like(dsc_ref)
            ebuf_ref[tc:tc + H, :] = jnp.zeros((H, M), F32)

        x, do = x_ref[...], do_ref[...]
        buf_ref[0:H, :] = jnp.where(i == nc - 1, 0.0, xh_ref[...])
        buf_ref[H:H + tc, :] = x
        diffs, inv = _pool_diffs(buf_ref, x, (nc - 1 - i) * tc, tc, gw)
        ddg = []
        for gi in range(ng):
            sl = slice(gi * gw, (gi + 1) * gw)
            dyy = do[:, sl] * sc_ref[:, sl]
            dsc_ref[:, sl] += _fold8(do[:, sl] * _dot(diffs[gi], pw_ref[gi]))
            dpw_ref[gi] += _dot_tn(diffs[gi], dyy)
            d = _dot_nt(dyy, pw_ref[gi])
            ddg.append(d)
            ebuf_ref[0:tc, sl] = d * inv[gi]
        for gi, win in enumerate(POOL_WINDOWS):
            sl = slice(gi * gw, (gi + 1) * gw)
            s = -ddg[gi]
            for j in range(win):
                s = s + ebuf_ref[pl.ds(j, tc), sl]
            dx_ref[:, sl] = s.astype(BF16)
        ebuf_ref[tc:tc + H, :] = ebuf_ref[0:H, :]

    return pl.pallas_call(
        body, name="pool_bwd", grid=(nc,),
        in_specs=[pl.BlockSpec((tc, M), lambda i: (nc - 1 - i, dcol)), pl.BlockSpec((tc, M), lambda i: (nc - 1 - i, col)),
                  pl.BlockSpec((H, M), lambda i: (jnp.maximum((nc - 1 - i) * hb - 1, 0), col)), _full(pw.shape), _full((1, M))],
        out_specs=[pl.BlockSpec((tc, M), lambda i: (nc - 1 - i, 0)), _full(pw.shape), _full((8, M))],
        out_shape=[jax.ShapeDtypeStruct((T, M), BF16), jax.ShapeDtypeStruct(pw.shape, F32), jax.ShapeDtypeStruct((8, M), F32)],
        scratch_shapes=[pltpu.VMEM((tc + H, M), F32), pltpu.VMEM((tc + H, M), F32)], compiler_params=ARB,
    )(dmix, proj, proj, pw, scale.reshape(1, M))


def _ffn_mid_fwd(up, wdw, bdw):
    T, F2 = up.shape
    F = F2 // 2
    tc = _pick(T, 256, 8)
    rb = _pick(tc, 64, 16)
    H, K = SMALL_HALO, 3

    def body(gt_ref, val_ref, w_ref, b_ref, act_ref, buf_ref):
        @pl.when(pl.program_id(0) == 0)
        def _():
            buf_ref[0:H, :] = jnp.zeros((H, F), F32)

        buf_ref[H:H + tc, :] = gt_ref[...]

        def strip(s, _):
            cs = pl.ds(pl.multiple_of(s * LANES, LANES), LANES)
            taps = [w_ref[k:k + 1, cs] for k in range(K)]
            for r0 in range(0, tc, rb):
                gc = b_ref[:, cs] + taps[0] * buf_ref[pl.ds(H - (K - 1) + r0, rb), cs]
                for k in range(1, K):
                    gc = gc + taps[k] * buf_ref[pl.ds(H - (K - 1) + k + r0, rb), cs]
                act_ref[pl.ds(r0, rb), cs] = (_gelu(gc) * val_ref[pl.ds(r0, rb), cs]).astype(BF16)
            return 0

        lax.fori_loop(0, F // LANES, strip, 0)
        buf_ref[0:H, :] = buf_ref[tc:tc + H, :]

    return pl.pallas_call(
        body, name="ffn_mid_fwd", grid=(T // tc,),
        in_specs=[pl.BlockSpec((tc, F), lambda i: (i, 0)), pl.BlockSpec((tc, F), lambda i: (i, 1)), _full((8, F)), _full((1, F))],
        out_specs=pl.BlockSpec((tc, F), lambda i: (i, 0)), out_shape=jax.ShapeDtypeStruct((T, F), BF16),
        scratch_shapes=[pltpu.VMEM((tc + H, F), F32)], compiler_params=ARB,
    )(up, up, wdw, bdw.reshape(1, F))


def _ffn_mid_bwd(dact, up, wdw, bdw):
    T, F2 = up.shape
    F = F2 // 2
    tc = _pick(T, 128, 8)
    rb = _pick(tc, 64, 16)
    nc = T // tc
    H, K = SMALL_HALO, 3
    hb = tc // H

    def body(da_ref, gt_ref, val_ref, gh_ref, w_ref, b_ref, dup_ref, dw_ref, db_ref, gbuf_ref, dbuf_ref):
        i = pl.program_id(0)

        @pl.when(i == 0)
        def _():
            dw_ref[...] = jnp.zeros_like(dw_ref)
            db_ref[...] = jnp.zeros_like(db_ref)
            dbuf_ref[tc:tc + H, :] = jnp.zeros((H, F), F32)

        gbuf_ref[0:H, :] = jnp.where(i == nc - 1, 0.0, gh_ref[...])
        gbuf_ref[H:H + tc, :] = gt_ref[...]

        def strip(s, _):
            cs = pl.ds(pl.multiple_of(s * LANES, LANES), LANES)
            cs_val = pl.ds(pl.multiple_of(F + s * LANES, LANES), LANES)
            taps = [w_ref[k:k + 1, cs] for k in range(K)]
            for r0 in range(0, tc, rb):
                shifted = [gbuf_ref[pl.ds(H - (K - 1) + k + r0, rb), cs] for k in range(K)]
                gc = b_ref[:, cs] + taps[0] * shifted[0]
                for k in range(1, K):
                    gc = gc + taps[k] * shifted[k]
                ge, gg = _gelu_and_grad(gc)
                da = da_ref[pl.ds(r0, rb), cs].astype(F32)
                dup_ref[pl.ds(r0, rb), cs_val] = (da * ge).astype(BF16)
                dgc = da * val_ref[pl.ds(r0, rb), cs] * gg
                dbuf_ref[pl.ds(r0, rb), cs] = dgc
                db_ref[:, cs] += _fold8(dgc)
                for k in range(K):
                    dw_ref[k, :, cs] += _fold8(dgc * shifted[k])
            for r0 in range(0, tc, rb):
                dgt = taps[0] * dbuf_ref[pl.ds(K - 1 + r0, rb), cs]
                for k in range(1, K):
                    dgt = dgt + taps[k] * dbuf_ref[pl.ds(K - 1 - k + r0, rb), cs]
                dup_ref[pl.ds(r0, rb), cs] = dgt.astype(BF16)
            return 0

        lax.fori_loop(0, F // LANES, strip, 0)
        dbuf_ref[tc:tc + H, :] = dbuf_ref[0:H, :]

    return pl.pallas_call(
        body, name="ffn_mid_bwd", grid=(nc,),
        in_specs=[pl.BlockSpec((tc, F), lambda i: (nc - 1 - i, 0)), pl.BlockSpec((tc, F), lambda i: (nc - 1 - i, 0)),
                  pl.BlockSpec((tc, F), lambda i: (nc - 1 - i, 1)),
                  pl.BlockSpec((H, F), lambda i: (jnp.maximum((nc - 1 - i) * hb - 1, 0), 0)), _full((8, F)), _full((1, F))],
        out_specs=[pl.BlockSpec((tc, F2), lambda i: (nc - 1 - i, 0)), _full((8, 8, F)), _full((8, F))],
        out_shape=[jax.ShapeDtypeStruct((T, F2), BF16), jax.ShapeDtypeStruct((8, 8, F), F32), jax.ShapeDtypeStruct((8, F), F32)],
        scratch_shapes=[pltpu.VMEM((tc + H, F), F32), pltpu.VMEM((tc + H, F), F32)], compiler_params=ARB,
    )(dact, up, up, up, wdw, bdw.reshape(1, F))


def _my_pos():
    return lax.axis_index("x"), lax.axis_index("y"), lax.axis_index("c")


def _flip(pos, k):
    x, y, c = pos
    return ((1 - x) if k & 4 else x, (1 - y) if k & 2 else y, (1 - c) if k & 1 else c)


def _rank(pos):
    return 4 * pos[0] + 2 * pos[1] + pos[2]


def _exchange(xs, name, gather):
    n = len(xs)
    shapes = [((N_DEV,) + tuple(x.shape)) if gather else tuple(x.shape) for x in xs]

    def body(*refs):
        ins, outs = refs[:n], refs[n:2 * n]
        send_sems, recv_sems, local_sems = refs[2 * n:]
        me = _my_pos()
        mine = _rank(me)
        local, sends = [], []
        for a in range(n):
            loc = pltpu.make_async_copy(ins[a] if gather else ins[a].at[mine], outs[a].at[mine], local_sems.at[a])
            loc.start()
            local.append(loc)
            for k in range(1, N_DEV):
                peer = _flip(me, k)
                cp = pltpu.make_async_remote_copy(
                    src_ref=ins[a] if gather else ins[a].at[_rank(peer)], dst_ref=outs[a].at[mine],
                    send_sem=send_sems.at[a, k - 1], recv_sem=recv_sems.at[a, k - 1], device_id=peer,
                    device_id_type=pl.DeviceIdType.MESH)
                cp.start()
                sends.append(cp)
        for a in range(n):
            for k in range(1, N_DEV):
                peer = _flip(me, k)
                pltpu.make_async_remote_copy(
                    src_ref=ins[a] if gather else ins[a].at[_rank(peer)], dst_ref=outs[a].at[_rank(peer)],
                    send_sem=send_sems.at[a, k - 1], recv_sem=recv_sems.at[a, k - 1], device_id=peer,
                    device_id_type=pl.DeviceIdType.MESH).wait_recv()
        for cp in sends:
            cp.wait_send()
        for loc in local:
            loc.wait()

    any_spec = pl.BlockSpec(memory_space=pl.ANY)
    return pl.pallas_call(
        body, name=name, in_specs=[any_spec] * n, out_specs=[any_spec] * n,
        out_shape=[jax.ShapeDtypeStruct(s, x.dtype) for s, x in zip(shapes, xs)],
        scratch_shapes=[pltpu.SemaphoreType.DMA((n, N_DEV - 1)), pltpu.SemaphoreType.DMA((n, N_DEV - 1)),
                        pltpu.SemaphoreType.DMA((n,))],
    )(*xs)


_HBM = pl.BlockSpec(memory_space=pltpu.HBM)
_SEM = pl.BlockSpec(memory_space=pltpu.SEMAPHORE)
_EFFECT = pltpu.SideEffectType.DATAFLOW_SIDE_EFFECTING


def _plan_direct(gather):
    def plan(bufs, ssem, rsem, me, outgoing):
        n = len(bufs) // 2
        out = []
        for a in range(n):
            for k in range(1, N_DEV):
                peer = _flip(me, k)
                out.append(pltpu.make_async_remote_copy(
                    src_ref=bufs[a] if gather else bufs[a].at[_rank(peer)],
                    dst_ref=bufs[n + a].at[_rank(me) if outgoing else _rank(peer)],
                    send_sem=ssem.at[a * 7 + k - 1], recv_sem=rsem.at[a * 7 + k - 1], device_id=peer,
                    device_id_type=pl.DeviceIdType.MESH))
        return out
    return plan, 7


_SAME_CORE = (2, 4, 6)


def _plan_own_block(bufs, ssem, rsem, me, outgoing):
    n = len(bufs) // 2
    out = []
    for a in range(n):
        for j, k in enumerate((1,) + _SAME_CORE):
            peer = _flip(me, k)
            out.append(pltpu.make_async_remote_copy(
                src_ref=bufs[a], dst_ref=bufs[n + a].at[_rank(me) if outgoing else _rank(peer)],
                send_sem=ssem.at[a * 4 + j], recv_sem=rsem.at[a * 4 + j], device_id=peer, device_id_type=pl.DeviceIdType.MESH))
    return out


def _plan_pass_on(bufs, ssem, rsem, me, outgoing):
    sibling = _flip(me, 1)
    out = []
    for a in range(len(bufs)):
        for j, k in enumerate(_SAME_CORE):
            slot = _rank(_flip(me, k) if outgoing else _flip(sibling, k))
            out.append(pltpu.make_async_remote_copy(
                src_ref=bufs[a].at[slot], dst_ref=bufs[a].at[slot], send_sem=ssem.at[a * 3 + j], recv_sem=rsem.at[a * 3 + j],
                device_id=sibling, device_id_type=pl.DeviceIdType.MESH))
    return out


def _split_start(bufs, plan, n_sems, name, after=None):
    nb = len(bufs)
    extra = [] if after is None else [after]

    def body(*refs):
        ssem, rsem = refs[nb + len(extra)], refs[nb + len(extra) + 1]
        for cp in plan(refs[:nb], ssem, rsem, _my_pos(), True):
            cp.start()
        token = refs[2 * nb + len(extra) + 2]
        token[...] = jnp.zeros_like(token)

    res = pl.pallas_call(
        body, name=name,
        out_shape=(pltpu.SemaphoreType.DMA((n_sems,)), pltpu.SemaphoreType.DMA((n_sems,)),
                   *[pltpu.HBM(t.shape, t.dtype) for t in bufs], jax.ShapeDtypeStruct((8, 128), F32)),
        in_specs=[_HBM] * nb + [pl.BlockSpec(memory_space=pl.ANY)] * len(extra),
        out_specs=(_SEM, _SEM, *([_HBM] * nb), pl.BlockSpec(memory_space=pltpu.VMEM)),
        input_output_aliases={i: 2 + i for i in range(nb)},
        compiler_params=pltpu.CompilerParams(has_side_effects=_EFFECT),
    )(*[pltpu.with_memory_space_constraint(t, pltpu.HBM) for t in bufs], *extra)
    return (res[0], res[1], list(res[2:2 + nb])), res[2 + nb]


def _split_wait(handle, plan, after, name):
    ssem, rsem, bufs = handle
    nb = len(bufs)

    def body(*refs):
        for cp in plan(refs[:nb], refs[nb], refs[nb + 1], _my_pos(), False):
            cp.wait_send()
            cp.wait_recv()

    res = pl.pallas_call(
        body, name=name, out_shape=tuple(pltpu.HBM(t.shape, t.dtype) for t in bufs),
        in_specs=[_HBM] * nb + [_SEM, _SEM, pl.BlockSpec(memory_space=pl.ANY)], out_specs=tuple([_HBM] * nb),
        input_output_aliases={i: i for i in range(nb)},
        compiler_params=pltpu.CompilerParams(has_side_effects=_EFFECT),
    )(*bufs, ssem, rsem, after)
    return list(res)


def _landing_zones(xs, gather):
    mine = _rank(_my_pos())
    lands = []
    for x in xs:
        own = x if gather else lax.dynamic_index_in_dim(x, mine, 0, keepdims=False)
        lands.append(lax.dynamic_update_index_in_dim(lax.empty((N_DEV,) + tuple(own.shape), x.dtype), own, mine, 0))
    return lands


def _gather_start(xs, name, after=None, two_level=True):
    bufs = list(xs) + _landing_zones(xs, True)
    if two_level:
        handle, token = _split_start(bufs, _plan_own_block, 4 * len(xs), name + "_start", after)
    else:
        plan, per = _plan_direct(True)
        handle, token = _split_start(bufs, plan, per * len(xs), name + "_start", after)
    return (handle, two_level, name), token


def _gather_wait(state, after):
    handle, two_level, name = state
    n = len(handle[2]) // 2
    if not two_level:
        return _split_wait(handle, _plan_direct(True)[0], after, name + "_wait")[n:]
    lands = _split_wait(handle, _plan_own_block, after, name + "_wait")[n:]
    handle, _ = _split_start(lands, _plan_pass_on, 3 * n, name + "_pass")
    return _split_wait(handle, _plan_pass_on, after, name + "_passed")


def _scatter_start(xs, name):
    plan, per = _plan_direct(False)
    handle, token = _split_start(list(xs) + _landing_zones(xs, False), plan, per * len(xs), name + "_start")
    return (handle, name), token


def _scatter_wait(state, after):
    handle, name = state
    return _split_wait(handle, _plan_direct(False)[0], after, name + "_wait")[len(handle[2]) // 2:]


def _adamw_math(w, g, m, v):
    m = ADAM_B1 * m + (1.0 - ADAM_B1) * g
    v = ADAM_B2 * v + (1.0 - ADAM_B2) * (g * g)
    m_hat = m / (1.0 - ADAM_B1 ** ADAM_STEP)
    v_hat = v / (1.0 - ADAM_B2 ** ADAM_STEP)
    return -ADAM_LR * (m_hat / (jnp.sqrt(v_hat) + ADAM_EPS) + ADAM_WD * w), m, v


def _sum8(parts, *, name):
    _, R, C = parts.shape
    tr = _pick(R, 256, 16)

    def body(p_ref, o_ref):
        acc = p_ref[0].astype(F32)
        for r in range(1, N_DEV):
            acc = acc + p_ref[r].astype(F32)
        o_ref[...] = acc

    return pl.pallas_call(
        body, name=name, grid=(R // tr,), in_specs=[pl.BlockSpec((N_DEV, tr, C), lambda i: (0, i, 0))],
        out_specs=pl.BlockSpec((tr, C), lambda i: (i, 0)), out_shape=jax.ShapeDtypeStruct((R, C), F32),
        compiler_params=pltpu.CompilerParams(dimension_semantics=("parallel",)),
    )(parts)


def _adamw(w, g, m, v, *, name):
    R, C = w.shape
    tr = _pick(R, 256, 8)

    def body(w_ref, g_ref, m_ref, v_ref, d_ref, nm_ref, nv_ref):
        d, nm, nv = _adamw_math(w_ref[...], g_ref[...], m_ref[...], v_ref[...])
        d_ref[...] = d
        nm_ref[...] = nm
        nv_ref[...] = nv

    row = pl.BlockSpec((tr, C), lambda i: (i, 0))
    return pl.pallas_call(
        body, name=name, grid=(R // tr,), in_specs=[row] * 4, out_specs=[row] * 3,
        out_shape=[jax.ShapeDtypeStruct((R, C), F32)] * 3,
        compiler_params=pltpu.CompilerParams(dimension_semantics=("parallel",)),
    )(w, g, m, v)


def _adamw_stacked(gs, w, m, v, *, name):
    depth, R, C = w.shape
    tr = _pick(R, 64, 16)
    summed = gs[0].ndim == 3

    def body(*refs):
        g_refs, (w_ref, m_ref, v_ref), outs = refs[:depth], refs[depth:depth + 3], refs[depth + 3:]
        for l in range(depth):
            if summed:
                g = g_refs[l][0].astype(F32)
                for r in range(1, N_DEV):
                    g = g + g_refs[l][r].astype(F32)
            else:
                g = g_refs[l][...]
            d, nm, nv = _adamw_math(w_ref[l], g, m_ref[l], v_ref[l])
            for o, val in zip(outs, (g, d, nm, nv)):
                o[l] = val

    g_spec = pl.BlockSpec((N_DEV, tr, C), lambda i: (0, i, 0)) if summed else pl.BlockSpec((tr, C), lambda i: (i, 0))
    stk = pl.BlockSpec((depth, tr, C), lambda i: (0, i, 0))
    return pl.pallas_call(
        body, name=name, grid=(R // tr,), in_specs=[g_spec] * depth + [stk] * 3, out_specs=[stk] * 4,
        out_shape=[jax.ShapeDtypeStruct((depth, R, C), F32)] * 4,
        compiler_params=pltpu.CompilerParams(dimension_semantics=("parallel",)),
    )(*gs, w, m, v)


def _pack(arrs):
    flat = jnp.concatenate([a.reshape(-1).astype(F32) for a in arrs])
    pad = (-flat.shape[0]) % 1024
    return jnp.pad(flat, (0, pad)).reshape(-1, 128)


def _unpack(buf, shapes):
    flat, out, off = buf.reshape(-1), [], 0
    for s in shapes:
        n = math.prod(s)
        out.append(flat[off:off + n].reshape(s))
        off += n
    return out


def _s5_params(lam_re, lam_im, log_step, b_re, b_im):
    G, P, H = b_re.shape
    step = jnp.exp(log_step)[:, None]
    mag = jnp.exp(lam_re * step)
    ar, ai = mag * jnp.cos(lam_im * step), mag * jnp.sin(lam_im * step)
    den = lam_re * lam_re + lam_im * lam_im
    qr = ((ar - 1.0) * lam_re + ai * lam_im) / den
    qi = (ai * lam_re - (ar - 1.0) * lam_im) / den
    bbr = qr[..., None] * b_re - qi[..., None] * b_im
    bbi = qr[..., None] * b_im + qi[..., None] * b_re
    return ar.reshape(-1), ai.reshape(-1), _s5_blocks_in(bbr), _s5_blocks_in(bbi)


def _s5_blocks_in(b):
    G, P, H = b.shape
    eye = jnp.eye(8, dtype=b.dtype)
    return jnp.einsum("kgph,gj->kghjp", b.reshape(G // 8, 8, P, H), eye).reshape(G // 8, 8 * H, 8 * P)


def _s5_blocks_in_diag(m, G, P, H):
    return jnp.einsum("kghgp->kgph", m.reshape(G // 8, 8, H, 8, P)).reshape(G, P, H)


def _s5_blocks_out(c):
    G, H, P = c.shape
    eye = jnp.eye(8, dtype=c.dtype)
    return jnp.einsum("kghp,gj->kgpjh", c.reshape(G // 8, 8, H, P), eye).reshape(G // 8, 8 * P, 8 * H)


def _s5_blocks_out_diag(m, G, H, P):
    return jnp.einsum("kgpgh->kghp", m.reshape(G // 8, 8, P, 8, H)).reshape(G, H, P)


def _head_blocks(w):
    nh, d, e = w.shape
    return jnp.einsum("hde,hk->hdke", w, jnp.eye(nh, dtype=w.dtype)).reshape(nh * d, nh * e)


def _head_blocks_diag(m, nh, d, e):
    return jnp.einsum("hdhe->hde", m.reshape(nh, d, nh, e))


def _pad_rows(w, rows):
    return jnp.pad(w, ((0, rows - w.shape[0]), (0, 0)))


def _unshard_cols(g):
    return jnp.transpose(g, (1, 0, 2)).reshape(g.shape[1], -1)


_NAMES = ['norm_mix_g', 'w_in', 's5_lam_re', 's5_lam_im', 's5_log_step', 's5_b_re', 's5_b_im', 's5_c_re', 's5_c_im', 's5_d',
          's5_w_glu', 's5_b_glu', 'cv_w_dw', 'cv_b_dw', 'cv_ln_g', 'cv_ln_b', 'cv_w_pw', 'cv_b_pw', 'lru_w_conv', 'lru_b_conv',
          'lru_w_r', 'lru_b_r', 'lru_w_i', 'lru_b_i', 'lru_lam', 'pool_w', 'pool_scale', 'w_out', 'norm_ffn_g', 'ffn_w_up',
          'ffn_w_dw', 'ffn_b_dw', 'ffn_w_down', 'norm_final_g']
_MATRIX = ('w_in', 'w_out', 'ffn_w_up', 'ffn_w_down', 's5_w_glu', 'cv_w_pw')
_TRANSPOSED = ('w_in', 'ffn_w_up')
_COLSHARD = ('cv_w_dw', 'lru_w_conv', 'ffn_w_dw')
_GROUPS = (('w_in', 's5_w_glu', 'cv_w_pw', 'w_out'), ('ffn_w_up', 'ffn_w_down'))


def kernel(x, norm_mix_g, w_in, s5_lam_re, s5_lam_im, s5_log_step, s5_b_re, s5_b_im, s5_c_re, s5_c_im, s5_d, s5_w_glu, s5_b_glu, cv_w_dw, cv_b_dw, cv_ln_g, cv_ln_b, cv_w_pw, cv_b_pw, lru_w_conv, lru_b_conv, lru_w_r, lru_b_r, lru_w_i, lru_b_i, lru_lam, pool_w, pool_scale, w_out, norm_ffn_g, ffn_w_up, ffn_w_dw, ffn_b_dw, ffn_w_down, norm_final_g, loss_target, m_norm_mix_g, m_w_in, m_s5_lam_re, m_s5_lam_im, m_s5_log_step, m_s5_b_re, m_s5_b_im, m_s5_c_re, m_s5_c_im, m_s5_d, m_s5_w_glu, m_s5_b_glu, m_cv_w_dw, m_cv_b_dw, m_cv_ln_g, m_cv_ln_b, m_cv_w_pw, m_cv_b_pw, m_lru_w_conv, m_lru_b_conv, m_lru_w_r, m_lru_b_r, m_lru_w_i, m_lru_b_i, m_lru_lam, m_pool_w, m_pool_scale, m_w_out, m_norm_ffn_g, m_ffn_w_up, m_ffn_w_dw, m_ffn_b_dw, m_ffn_w_down, m_norm_final_g, v_norm_mix_g, v_w_in, v_s5_lam_re, v_s5_lam_im, v_s5_log_step, v_s5_b_re, v_s5_b_im, v_s5_c_re, v_s5_c_im, v_s5_d, v_s5_w_glu, v_s5_b_glu, v_cv_w_dw, v_cv_b_dw, v_cv_ln_g, v_cv_ln_b, v_cv_w_pw, v_cv_b_pw, v_lru_w_conv, v_lru_b_conv, v_lru_w_r, v_lru_b_r, v_lru_w_i, v_lru_b_i, v_lru_lam, v_pool_w, v_pool_scale, v_w_out, v_norm_ffn_g, v_ffn_w_up, v_ffn_w_dw, v_ffn_b_dw, v_ffn_w_down, v_norm_final_g):
    args = (x, norm_mix_g, w_in, s5_lam_re, s5_lam_im, s5_log_step, s5_b_re, s5_b_im, s5_c_re, s5_c_im, s5_d, s5_w_glu, s5_b_glu, cv_w_dw, cv_b_dw, cv_ln_g, cv_ln_b, cv_w_pw, cv_b_pw, lru_w_conv, lru_b_conv, lru_w_r, lru_b_r, lru_w_i, lru_b_i, lru_lam, pool_w, pool_scale, w_out, norm_ffn_g, ffn_w_up, ffn_w_dw, ffn_b_dw, ffn_w_down, norm_final_g, loss_target, m_norm_mix_g, m_w_in, m_s5_lam_re, m_s5_lam_im, m_s5_log_step, m_s5_b_re, m_s5_b_im, m_s5_c_re, m_s5_c_im, m_s5_d, m_s5_w_glu, m_s5_b_glu, m_cv_w_dw, m_cv_b_dw, m_cv_ln_g, m_cv_ln_b, m_cv_w_pw, m_cv_b_pw, m_lru_w_conv, m_lru_b_conv, m_lru_w_r, m_lru_b_r, m_lru_w_i, m_lru_b_i, m_lru_lam, m_pool_w, m_pool_scale, m_w_out, m_norm_ffn_g, m_ffn_w_up, m_ffn_w_dw, m_ffn_b_dw, m_ffn_w_down, m_norm_final_g, v_norm_mix_g, v_w_in, v_s5_lam_re, v_s5_lam_im, v_s5_log_step, v_s5_b_re, v_s5_b_im, v_s5_c_re, v_s5_c_im, v_s5_d, v_s5_w_glu, v_s5_b_glu, v_cv_w_dw, v_cv_b_dw, v_cv_ln_g, v_cv_ln_b, v_cv_w_pw, v_cv_b_pw, v_lru_w_conv, v_lru_b_conv, v_lru_w_r, v_lru_b_r, v_lru_w_i, v_lru_b_i, v_lru_lam, v_pool_w, v_pool_scale, v_w_out, v_norm_ffn_g, v_ffn_w_up, v_ffn_w_dw, v_ffn_b_dw, v_ffn_w_down, v_norm_final_g)
    x, target = args[0], args[35]
    W = dict(zip(_NAMES, args[1:35]))
    Mo = dict(zip(_NAMES, args[36:70]))
    Vo = dict(zip(_NAMES, args[70:104]))
    x = x[0]
    target = target[0]
    T, D = x.shape
    depth = W['w_in'].shape[0]
    G, P, H = W['s5_b_re'].shape[1:]
    MW = G * H
    nh, hd, _ = W['lru_w_r'].shape[1:]
    me = _rank(_my_pos())

    gather, full = {}, {}

    def start_group(l, gi, after):
        shards = [(W[n][l].T if n in _TRANSPOSED else W[n][l]).astype(BF16) for n in _GROUPS[gi]]
        if (l, gi) == (0, 0):
            shards.append(_pack([W[n][k] for k in range(depth) for n in _COLSHARD]))
        gather[l, gi], tok = _gather_start(shards, "gather_%d%d" % (l, gi), after)
        return tok

    def land_weights(l, gi, after):
        got = _gather_wait(gather[l, gi], after)
        for n, g8 in zip(_GROUPS[gi], got):
            full[n, l] = g8.reshape(-1, g8.shape[-1])
        return got

    started = start_group(0, 0, None)[0, 0]
    got = land_weights(0, 0, W['norm_mix_g'][0] + started)
    start_group(0, 1, got[0])
    taps8 = got[-1]
    tap_shapes = [W[n][l].shape for l in range(depth) for n in _COLSHARD]
    taps = [_unshard_cols(jnp.stack(t)) for t in zip(*[_unpack(taps8[r], tap_shapes) for r in range(N_DEV)])]
    for l in range(depth):
        for j, n in enumerate(_COLSHARD):
            full[n, l] = taps[l * len(_COLSHARD) + j]

    saved = []
    for l in range(depth):
        g_mix = W['norm_mix_g'][l]
        if l:
            land_weights(l, 0, x)
        s5p, s5p_vjp = jax.vjp(_s5_params, W['s5_lam_re'][l], W['s5_lam_im'][l], W['s5_log_step'][l], W['s5_b_re'][l], W['s5_b_im'][l])
        ar, ai, bre, bim = s5p
        cre, cim = _s5_blocks_out(W['s5_c_re'][l]), _s5_blocks_out(W['s5_c_im'][l])
        sp8, sp8_vjp = jax.vjp(lambda lam: LRU_C * jax.nn.softplus(-lam), W['lru_lam'][l])
        wr, wi = _head_blocks(W['lru_w_r'][l]), _head_blocks(W['lru_w_i'][l])
        c = dict(bre=bre.astype(BF16), bim=bim.astype(BF16), cre=cre.astype(BF16), cim=cim.astype(BF16),
                 cst_f=_s5_scan_consts(ar, ai, False), cst_b=_s5_scan_consts(ar, ai, True), sp8=sp8, wr=wr.astype(BF16),
                 wi=wi.astype(BF16), cvw=_pad_rows(full['cv_w_dw', l], 32), lruw=_pad_rows(full['lru_w_conv', l], 8),
                 ffw=_pad_rows(full['ffn_w_dw', l], 8), pw=W['pool_w'][l].astype(BF16), s5p_vjp=s5p_vjp, sp8_vjp=sp8_vjp, x0=x)
        proj, h = _norm_matmul(x, g_mix, full['w_in', l], tm=1024, tn=512, name="in_proj")
        y0, sre, sim, yp, z = _s5_fwd(proj, 0, c['bre'], c['bim'], c['cre'], c['cim'], c['cst_f'], W['s5_d'][l],
                                     full['s5_w_glu', l], W['s5_b_glu'][l])
        y1, h1 = _cv_fwd(proj, 1, 2, c['cvw'], W['cv_b_dw'][l], W['cv_ln_g'][l], W['cv_ln_b'][l], full['cv_w_pw', l], W['cv_b_pw'][l])
        y2, xc, rr, ig, hprev = _lru_fwd(proj, 3, 4, c['lruw'], W['lru_b_conv'][l], c['wr'], W['lru_b_r'][l], c['wi'],
                                         W['lru_b_i'][l], sp8)
        y3 = _pool_fwd(proj, 5, c['pw'], W['pool_scale'][l])
        mixed = jnp.concatenate([y0, y1, y2, y3], axis=1)
        x1 = _matmul(mixed, full['w_out', l], mode="nn", tm=1024, tn=512, tk=2048, out_dtype=F32, name="out_proj", add=x)
        got = land_weights(l, 1, x1)
        if l + 1 < depth:
            start_group(l + 1, 0, got[0])
            start_group(l + 1, 1, got[0])
        up, h2 = _norm_matmul(x1, W['norm_ffn_g'][l], full['ffn_w_up', l], tm=1024, tn=256, name="ffn_up")
        act = _ffn_mid_fwd(up, c['ffw'], W['ffn_b_dw'][l])
        F = act.shape[1]
        x = _matmul(act, full['ffn_w_down', l], mode="nn", tm=512, tn=512, tk=F, out_dtype=F32, name="ffn_down", add=x1)
        c.update(proj=proj, h=h, sre=sre, sim=sim, yp=yp, z=z, h1=h1, xc=xc, rr=rr, ig=ig, hprev=hprev, mixed=mixed, x1=x1,
                 up=up, h2=h2, act=act)
        saved.append(c)

    dx, dxb, dgf, se = _loss_head(x, W['norm_final_g'], target)
    loss = lax.psum(0.5 / D * jnp.sum(se), AXES)
    gsmall = {('norm_final_g', None): dgf.sum(0)}
    gmat, scatter = {}, {}

    def parts_of(m):
        return m.reshape(N_DEV, m.shape[0] // N_DEV, m.shape[1])

    for l in reversed(range(depth)):
        c = saved[l]
        F = c['act'].shape[1]
        dact = _matmul(dxb, full['ffn_w_down', l], mode="nt", tm=512, tn=F, tk=512, out_dtype=BF16, name="d_act")
        gmat['ffn_w_down', l] = _matmul(c['act'], dxb, mode="tn", tm=F, tn=512, tk=256, out_dtype=BF16, name="dw_down")
        dup, dffw, dffb = _ffn_mid_bwd(dact, c['up'], c['ffw'], W['ffn_b_dw'][l])
        gsmall['ffn_w_dw', l] = dffw.sum(1)[:3]
        gsmall['ffn_b_dw', l] = dffb.sum(0)
        dh2 = _matmul(dup, full['ffn_w_up', l], mode="nn", tm=1024, tn=512, tk=F, out_dtype=F32, name="d_h2")
        gmat['ffn_w_up', l] = _matmul(dup, c['h2'], mode="tn", tm=256, tn=D, tk=T, out_dtype=BF16, name="dw_up")
        scatter[l, 1], tok = _scatter_start([parts_of(gmat[n, l]) for n in _GROUPS[1]], "scatter_%d1" % l)
        dx1, dx1b, dg2 = _rms_bwd(dx, dh2, c['x1'], W['norm_ffn_g'][l] + tok[0, 0], name="ffn_norm_bwd")
        gsmall['norm_ffn_g', l] = dg2.sum(0)
        dmix = _matmul(dx1b, full['w_out', l], mode="nt", tm=1024, tn=512, tk=2048, out_dtype=F32, name="d_mixed")
        gmat['w_out', l] = _matmul(c['mixed'], dx1b, mode="tn", tm=512, tn=D, tk=2048, out_dtype=BF16, name="dw_out")
        du, dbre, dbim, dcre, dcim, da, dwglu, v3 = _s5_bwd(dmix, 0, c['proj'], 0, c['sre'], c['sim'], c['yp'], c['z'], c['bre'],
                                                            c['bim'], c['cre'], c['cim'], c['cst_b'], W['s5_d'][l], full['s5_w_glu', l])
        da = da.sum(1)
        glr, gli, gls, gbr, gbi = c['s5p_vjp']((da[0], da[1], dbre, dbim))
        gsmall['s5_lam_re', l], gsmall['s5_lam_im', l], gsmall['s5_log_step', l] = glr, gli, gls
        gsmall['s5_b_re', l], gsmall['s5_b_im', l] = gbr, gbi
        gsmall['s5_c_re', l] = _s5_blocks_out_diag(dcre, G, H, P)
        gsmall['s5_c_im', l] = _s5_blocks_out_diag(dcim, G, H, P)
        gsmall['s5_b_glu', l], gsmall['s5_d', l] = v3[0].sum(0), v3[1].sum(0)
        gmat['s5_w_glu', l] = dwglu.astype(BF16)
        dv, dg, dwpw, dcvw, v5 = _cv_bwd(dmix, 1, c['proj'], 1, 2, c['h1'], c['cvw'], W['cv_ln_g'][l], W['cv_ln_b'][l], full['cv_w_pw', l])
        gmat['cv_w_pw', l] = dwpw.astype(BF16)
        gsmall['cv_w_dw', l] = dcvw.sum(1)[:31]
        gsmall['cv_b_pw', l], gsmall['cv_ln_g', l], gsmall['cv_ln_b', l], gsmall['cv_b_dw', l] = [v5[j].sum(0) for j in range(4)]
        dlx, dlg, dwr, dwi, dlw, v4 = _lru_bwd(dmix, 2, c['proj'], 3, 4, c['xc'], c['rr'], c['ig'], c['hprev'], c['lruw'], c['wr'],
                                               c['wi'], c['sp8'])
        gsmall['lru_w_r', l], gsmall['lru_w_i', l] = _head_blocks_diag(dwr, nh, hd, hd), _head_blocks_diag(dwi, nh, hd, hd)
        gsmall['lru_w_conv', l] = dlw.sum(1)[:4]
        gsmall['lru_b_r', l], gsmall['lru_b_i', l] = v4[0].sum(0), v4[1].sum(0)
        gsmall['lru_lam', l] = c['sp8_vjp'](v4[2].sum(0))[0]
        gsmall['lru_b_conv', l] = v4[3].sum(0)
        dpx, dpw, dps = _pool_bwd(dmix, 3, c['proj'], 5, c['pw'], W['pool_scale'][l])
        gsmall['pool_w', l], gsmall['pool_scale', l] = dpw, dps.sum(0)
        dproj = jnp.concatenate([du, dv, dg, dlx, dlg, dpx], axis=1)
        dh = _matmul(dproj, full['w_in', l], mode="nn", tm=1024, tn=512, tk=1024, out_dtype=F32, name="d_h")
        gmat['w_in', l] = _matmul(dproj, c['h'], mode="tn", tm=512, tn=D, tk=2048, out_dtype=BF16, name="dw_in")
        scatter[l, 0], tok = _scatter_start([parts_of(gmat[n, l]) for n in _GROUPS[0]], "scatter_%d0" % l)
        dx, dxb, dg1 = _rms_bwd(dx1, dh, c['x0'], W['norm_mix_g'][l] + tok[0, 0], name="mix_norm_bwd")
        gsmall['norm_mix_g', l] = dg1.sum(0)

    small_names = [n for n in _NAMES if n not in _MATRIX]
    small_parts = [gsmall[n, l] for n in small_names for l in ((None,) if n == 'norm_final_g' else range(depth))]
    full_shapes = [gsmall[n, None].shape if n == 'norm_final_g' else (depth,) + gsmall[n, 0].shape for n in small_names]
    small_state, tok = _gather_start([_pack(small_parts)], "small_grads", two_level=False)
    parts = {}
    for l in reversed(range(depth)):
        for gi in (1, 0):
            got = _scatter_wait(scatter[l, gi], tok)
            for n, p8 in zip(_GROUPS[gi], got):
                parts[n, l] = p8

    out = {}
    for n in _MATRIX:
        if n in _TRANSPOSED and W[n].shape[-1] % LANES:
            wt, mt, vt = (jnp.swapaxes(t, 1, 2) for t in (W[n], Mo[n], Vo[n]))
            res = _adamw_stacked([parts[n, l] for l in range(depth)], wt, mt, vt, name="adamw_" + n)
            out[n] = tuple(jnp.swapaxes(t, 1, 2) for t in res)
            continue
        if n in _TRANSPOSED:
            gl = [_sum8(parts[n, l], name="sum_" + n).T for l in range(depth)]
        else:
            gl = [parts[n, l] for l in range(depth)]
        out[n] = _adamw_stacked(gl, W[n], Mo[n], Vo[n], name="adamw_" + n)
    after_matrices = sum(out[n][1][0, 0, 0] for n in _MATRIX).reshape(1, 1)
    small8 = _gather_wait(small_state, after_matrices)[0]
    small_sum = _sum8(small8, name="sum_small_grads")
    gs = dict(zip(small_names, _unpack(small_sum, full_shapes)))
    for n in _COLSHARD:
        cw = W[n].shape[-1]
        gs[n] = lax.dynamic_slice_in_dim(gs[n], me * cw, cw, axis=gs[n].ndim - 1)
    upd = _adamw(_pack([W[n] for n in small_names]), _pack([gs[n] for n in small_names]), _pack([Mo[n] for n in small_names]),
                 _pack([Vo[n] for n in small_names]), name="adamw_small")
    upd = [_unpack(u, [gs[n].shape for n in small_names]) for u in upd]
    for j, n in enumerate(small_names):
        out[n] = (gs[n], upd[0][j], upd[1][j], upd[2][j])
    return (loss, dx[None]) + tuple(out[n][q] for q in range(4) for n in _NAMES)
```

```python
import functools
import math

import jax
import jax.numpy as jnp
from jax import lax
from jax.experimental import pallas as pl
from jax.experimental.pallas import tpu as pltpu

F32 = jnp.float32
BF16 = jnp.bfloat16
N_DEV = 8
AXES = ("x", "y", "c")
EPS = 1e-6
S5_GROUP_CH = 16
S5_STATE = 64
LRU_HEADS = 8
LRU_C = 8.0
POOL_WINDOWS = (2, 4, 8, 16)
CONV_HALO = 32
SMALL_HALO = 8
FFN_HALO = 16
POOL_HALO = 16
LANES = 128
ADAM_LR, ADAM_B1, ADAM_B2, ADAM_EPS, ADAM_WD, ADAM_STEP = 0.001, 0.9, 0.999, 1e-08, 0.01, 10
_GELU_K = math.sqrt(2.0 / math.pi)
ARB = pltpu.CompilerParams(dimension_semantics=("arbitrary",))


def _pick(n, pref, mult):
    best = None
    for t in range(mult, min(n, pref) + 1, mult):
        if n % t == 0:
            best = t
    return best if best is not None else n


def _sigmoid(x):
    return 1.0 / (1.0 + jnp.exp(-x))


def _gelu(x):
    return 0.5 * x * (1.0 + jnp.tanh(_GELU_K * (x + 0.044715 * x * x * x)))


def _gelu_grad(x):
    t = jnp.tanh(_GELU_K * (x + 0.044715 * x * x * x))
    return 0.5 * (1.0 + t) + 0.5 * x * (1.0 - t * t) * _GELU_K * (1.0 + 3.0 * 0.044715 * x * x)


def _gelu_and_grad(x):
    t = jnp.tanh(_GELU_K * (x + 0.044715 * x * x * x))
    half = 0.5 * (1.0 + t)
    return x * half, half + 0.5 * x * (1.0 - t * t) * _GELU_K * (1.0 + 3.0 * 0.044715 * x * x)


def _neg_expm1(x):
    series = -x * (1.0 + x * (0.5 + x * (1.0 / 6.0 + x * (1.0 / 24.0 + x * (1.0 / 120.0 + x * (1.0 / 720.0))))))
    return jnp.where(jnp.abs(x) < 0.25, series, 1.0 - jnp.exp(x))


def _fold8(x):
    return x.reshape(x.shape[0] // 8, 8, x.shape[1]).sum(axis=0)


def _dot(a, b):
    return jnp.dot(a.astype(BF16), b.astype(BF16), preferred_element_type=F32)


def _dot_nt(a, b):
    return lax.dot_general(a.astype(BF16), b.astype(BF16), (((1,), (1,)), ((), ())), preferred_element_type=F32)


def _dot_tn(a, b):
    return lax.dot_general(a.astype(BF16), b.astype(BF16), (((0,), (0,)), ((), ())), preferred_element_type=F32)


def _full(shape):
    nd = len(shape)
    return pl.BlockSpec(shape, lambda *i: (0,) * nd)


def _matmul(a, b, *, mode, tm, tn, tk, out_dtype, name, add=None):
    if mode == "nn":
        (M, K), N = a.shape, b.shape[1]
    elif mode == "nt":
        (M, K), N = a.shape, b.shape[0]
    else:
        (K, M), N = a.shape, b.shape[1]
    tm, tn, tk = _pick(M, tm, 8), _pick(N, tn, 128), _pick(K, tk, 128)
    if mode == "tn":
        tm = _pick(M, tm, 128)
    nk = K // tk
    a_spec = {"nn": pl.BlockSpec((tm, tk), lambda i, j, k: (i, k)), "nt": pl.BlockSpec((tm, tk), lambda i, j, k: (i, k)),
              "tn": pl.BlockSpec((tk, tm), lambda i, j, k: (k, i))}[mode]
    b_spec = {"nn": pl.BlockSpec((tk, tn), lambda i, j, k: (k, j)), "nt": pl.BlockSpec((tn, tk), lambda i, j, k: (j, k)),
              "tn": pl.BlockSpec((tk, tn), lambda i, j, k: (k, j))}[mode]
    o_spec = pl.BlockSpec((tm, tn), lambda i, j, k: (i, j))
    dot = {"nn": _dot, "nt": _dot_nt, "tn": _dot_tn}[mode]
    has_add = add is not None

    def body(*refs):
        if has_add:
            a_ref, b_ref, add_ref, o_ref, acc_ref = refs
        else:
            a_ref, b_ref, o_ref, acc_ref = refs
        k = pl.program_id(2)

        @pl.when(k == 0)
        def _():
            acc_ref[...] = jnp.zeros_like(acc_ref)

        acc_ref[...] += dot(a_ref[...], b_ref[...])

        @pl.when(k == nk - 1)
        def _():
            r = acc_ref[...]
            if has_add:
                r = r + add_ref[...]
            o_ref[...] = r.astype(out_dtype)

    ins = [a, b] + ([add] if has_add else [])
    specs = [a_spec, b_spec] + ([o_spec] if has_add else [])
    return pl.pallas_call(
        body, name=name, grid=(M // tm, N // tn, nk), in_specs=specs, out_specs=o_spec,
        out_shape=jax.ShapeDtypeStruct((M, N), out_dtype), scratch_shapes=[pltpu.VMEM((tm, tn), F32)],
        compiler_params=pltpu.CompilerParams(dimension_semantics=("parallel", "parallel", "arbitrary")),
    )(*ins)


def _norm_matmul(x, g, wt, *, tm, tn, name, out_dtype=F32):
    T, D = x.shape
    N = wt.shape[0]
    tm, tn = _pick(T, tm, 8), _pick(N, tn, 128)

    def body(x_ref, g_ref, w_ref, o_ref, h_ref):
        @pl.when(pl.program_id(1) == 0)
        def _():
            xv = x_ref[...]
            r = lax.rsqrt(jnp.mean(xv * xv, axis=-1, keepdims=True) + EPS)
            h_ref[...] = (xv * r * g_ref[...]).astype(BF16)

        o_ref[...] = _dot_nt(h_ref[...], w_ref[...]).astype(out_dtype)

    return pl.pallas_call(
        body, name=name, grid=(T // tm, N // tn),
        in_specs=[pl.BlockSpec((tm, D), lambda i, j: (i, 0)), pl.BlockSpec((1, D), lambda i, j: (0, 0)),
                  pl.BlockSpec((tn, D), lambda i, j: (j, 0))],
        out_specs=[pl.BlockSpec((tm, tn), lambda i, j: (i, j)), pl.BlockSpec((tm, D), lambda i, j: (i, 0))],
        out_shape=[jax.ShapeDtypeStruct((T, N), out_dtype), jax.ShapeDtypeStruct((T, D), BF16)],
        compiler_params=pltpu.CompilerParams(dimension_semantics=("parallel", "arbitrary")),
    )(x, g.reshape(1, D), wt)


def _rms_bwd(dres, dh, x, g, *, name):
    T, D = x.shape
    tc = _pick(T, 256, 8)

    def body(dres_ref, dh_ref, x_ref, g_ref, dx_ref, dxb_ref, dg_ref):
        @pl.when(pl.program_id(0) == 0)
        def _():
            dg_ref[...] = jnp.zeros_like(dg_ref)

        xv, dhv = x_ref[...], dh_ref[...]
        r = lax.rsqrt(jnp.mean(xv * xv, axis=-1, keepdims=True) + EPS)
        dg_ref[...] += _fold8(dhv * xv * r)
        dyg = dhv * g_ref[...]
        dx = dres_ref[...] + r * dyg - xv * (r * r * r) * jnp.mean(dyg * xv, axis=-1, keepdims=True)
        dx_ref[...] = dx
        dxb_ref[...] = dx.astype(BF16)

    row = pl.BlockSpec((tc, D), lambda i: (i, 0))
    return pl.pallas_call(
        body, name=name, grid=(T // tc,), in_specs=[row, row, row, _full((1, D))],
        out_specs=[row, row, _full((8, D))],
        out_shape=[jax.ShapeDtypeStruct((T, D), F32), jax.ShapeDtypeStruct((T, D), BF16), jax.ShapeDtypeStruct((8, D), F32)],
        compiler_params=ARB,
    )(dres, dh, x, g.reshape(1, D))


def _loss_head(x, g, target):
    T, D = x.shape
    tc = _pick(T, 256, 8)

    def body(x_ref, g_ref, t_ref, dx_ref, dxb_ref, dg_ref, se_ref):
        @pl.when(pl.program_id(0) == 0)
        def _():
            dg_ref[...] = jnp.zeros_like(dg_ref)
            se_ref[...] = jnp.zeros_like(se_ref)

        xv = x_ref[...]
        r = lax.rsqrt(jnp.mean(xv * xv, axis=-1, keepdims=True) + EPS)
        gv = g_ref[...]
        err = xv * r * gv - t_ref[...]
        se_ref[...] += _fold8(err * err)
        dy = err * (1.0 / D)
        dg_ref[...] += _fold8(dy * xv * r)
        dyg = dy * gv
        dx = r * dyg - xv * (r * r * r) * jnp.mean(dyg * xv, axis=-1, keepdims=True)
        dx_ref[...] = dx
        dxb_ref[...] = dx.astype(BF16)

    row = pl.BlockSpec((tc, D), lambda i: (i, 0))
    return pl.pallas_call(
        body, name="loss_head", grid=(T // tc,), in_specs=[row, _full((1, D)), row],
        out_specs=[row, row, _full((8, D)), _full((8, D))],
        out_shape=[jax.ShapeDtypeStruct((T, D), F32), jax.ShapeDtypeStruct((T, D), BF16),
                   jax.ShapeDtypeStruct((8, D), F32), jax.ShapeDtypeStruct((8, D), F32)],
        compiler_params=ARB,
    )(x, g.reshape(1, D), target)


def _s5_scan_consts(ar, ai, reverse):
    if reverse:
        ai = -ai
    pr, pi_ = [ar], [ai]
    for _ in range(7):
        pr, pi_ = pr + [pr[-1] * ar - pi_[-1] * ai], pi_ + [pr[-1] * ai + pi_[-1] * ar]
    rows = jnp.arange(8)[:, None]
    planes = []
    for d in (1, 2, 4):
        mask = (rows <= 7 - d) if reverse else (rows >= d)
        planes += [jnp.where(mask, pr[d - 1][None, :], 0.0), jnp.where(mask, pi_[d - 1][None, :], 0.0)]
    order = list(range(7, -1, -1)) if reverse else list(range(8))
    planes += [jnp.stack([pr[k] for k in order]), jnp.stack([pi_[k] for k in order])]
    return jnp.stack(planes).astype(F32)


def _s5_block_scan(xr_ref, xi_ref, cst_ref, car_ref, nblk, reverse, lb, extra=None):
    ns = xr_ref.shape[1]
    crow = 0 if reverse else 7

    def blk(i, _):
        ib = (nblk - 1 - i) if reverse else i
        r0 = pl.multiple_of(ib * 8, 8)
        for k in range(ns // lb):
            sl = slice(k * lb, (k + 1) * lb)
            xr, xi = xr_ref[pl.ds(r0, 8), sl], xi_ref[pl.ds(r0, 8), sl]
            for n, d in enumerate((1, 2, 4)):
                ar, ai = cst_ref[2 * n, :, sl], cst_ref[2 * n + 1, :, sl]
                sh = (8 - d) if reverse else d
                rr, ri = pltpu.roll(xr, sh, 0), pltpu.roll(xi, sh, 0)
                xr, xi = xr + ar * rr - ai * ri, xi + ar * ri + ai * rr
            pr, pi_ = cst_ref[6, :, sl], cst_ref[7, :, sl]
            cr, ci = car_ref[0, crow:crow + 1, sl], car_ref[1, crow:crow + 1, sl]
            xr, xi = xr + pr * cr - pi_ * ci, xi + pr * ci + pi_ * cr
            xr_ref[pl.ds(r0, 8), sl] = xr
            xi_ref[pl.ds(r0, 8), sl] = xi
            if extra is not None:
                extra(r0, sl, xr, xi, cr, ci)
            car_ref[0, :, sl] = xr
            car_ref[1, :, sl] = xi
        return 0

    lax.fori_loop(0, nblk, blk, 0)


def _s5_fwd(proj, col, bre, bim, cre, cim, cst, dvec, wglu, bglu):
    T = proj.shape[0]
    nsb, cw, sw = bre.shape
    M, NS = nsb * cw, nsb * sw
    tc = _pick(T, 256, 8)

    def body(u_ref, bre_ref, bim_ref, cre_ref, cim_ref, cst_ref, d_ref, wg_ref, bg_ref,
             y_ref, sre_ref, sim_ref, yp_ref, z_ref, car_ref):
        @pl.when(pl.program_id(0) == 0)
        def _():
            car_ref[...] = jnp.zeros_like(car_ref)

        u = u_ref[...]
        ub = u.astype(BF16)
        for k in range(nsb):
            sre_ref[:, k * sw:(k + 1) * sw] = _dot(ub[:, k * cw:(k + 1) * cw], bre_ref[k])
            sim_ref[:, k * sw:(k + 1) * sw] = _dot(ub[:, k * cw:(k + 1) * cw], bim_ref[k])
        _s5_block_scan(sre_ref, sim_ref, cst_ref, car_ref, tc // 8, False, sw)
        for k in range(nsb):
            yk = _dot(sre_ref[:, k * sw:(k + 1) * sw], cre_ref[k]) - _dot(sim_ref[:, k * sw:(k + 1) * sw], cim_ref[k])
            yp_ref[:, k * cw:(k + 1) * cw] = yk + d_ref[:, k * cw:(k + 1) * cw] * u[:, k * cw:(k + 1) * cw]
        gl = _gelu(yp_ref[...])
        z = _dot(gl, wg_ref[...]) + bg_ref[...]
        z_ref[...] = z
        y_ref[...] = (gl * _sigmoid(z)).astype(BF16)

    rowm = pl.BlockSpec((tc, M), lambda i: (i, 0))
    rows = pl.BlockSpec((tc, NS), lambda i: (i, 0))
    return pl.pallas_call(
        body, name="s5_fwd", grid=(T // tc,),
        in_specs=[pl.BlockSpec((tc, M), lambda i: (i, col)), _full(bre.shape), _full(bim.shape), _full(cre.shape),
                  _full(cim.shape), _full(cst.shape), _full((1, M)), _full((M, M)), _full((1, M))],
        out_specs=[rowm, rows, rows, rowm, rowm],
        out_shape=[jax.ShapeDtypeStruct((T, M), BF16), jax.ShapeDtypeStruct((T, NS), F32), jax.ShapeDtypeStruct((T, NS), F32),
                   jax.ShapeDtypeStruct((T, M), F32), jax.ShapeDtypeStruct((T, M), F32)],
        scratch_shapes=[pltpu.VMEM((2, 8, NS), F32)], compiler_params=ARB,
    )(proj, bre, bim, cre, cim, cst, dvec.reshape(1, M), wglu, bglu.reshape(1, M))


def _s5_bwd(dmix, dcol, proj, col, sre, sim, yp, z, bre, bim, cre, cim, cst, dvec, wglu):
    T = proj.shape[0]
    nsb, cw, sw = bre.shape
    M, NS = nsb * cw, nsb * sw
    tc = _pick(T, 256, 8)
    nc = T // tc

    def body(do_ref, u_ref, sre_ref, sim_ref, yp_ref, z_ref, bre_ref, bim_ref, cre_ref, cim_ref, cst_ref, d_ref, wg_ref,
             du_ref, dbre_ref, dbim_ref, dcre_ref, dcim_ref, da_ref, dwg_ref, vec_ref, gre_ref, gim_ref, car_ref):
        @pl.when(pl.program_id(0) == 0)
        def _():
            for r in (dbre_ref, dbim_ref, dcre_ref, dcim_ref, da_ref, dwg_ref, vec_ref, car_ref):
                r[...] = jnp.zeros_like(r)

        u, do, ypv = u_ref[...], do_ref[...], yp_ref[...]
        gl = _gelu(ypv)
        sg = _sigmoid(z_ref[...])
        dz = do * gl * sg * (1.0 - sg)
        dgl = do * sg + _dot_nt(dz, wg_ref[...])
        dwg_ref[...] += _dot_tn(gl, dz)
        vec_ref[0] += _fold8(dz)
        dy = dgl * _gelu_grad(ypv)
        vec_ref[1] += _fold8(dy * u)
        dyb = dy.astype(BF16)
        for k in range(nsb):
            cs, ss = slice(k * cw, (k + 1) * cw), slice(k * sw, (k + 1) * sw)
            gre_ref[:, ss] = _dot_nt(dyb[:, cs], cre_ref[k])
            gim_ref[:, ss] = -_dot_nt(dyb[:, cs], cim_ref[k])
            dcre_ref[k] += _dot_tn(sre_ref[:, ss], dyb[:, cs])
            dcim_ref[k] -= _dot_tn(sim_ref[:, ss], dyb[:, cs])

        row = lax.broadcasted_iota(jnp.int32, (8, sw), 0)

        def extra(r0, sl, gr, gi, cr, ci):
            nr = jnp.where(row == 7, cr, pltpu.roll(gr, 7, 0))
            ni = jnp.where(row == 7, ci, pltpu.roll(gi, 7, 0))
            sr, si = sre_ref[pl.ds(r0, 8), sl], sim_ref[pl.ds(r0, 8), sl]
            da_ref[0, :, sl] += nr * sr + ni * si
            da_ref[1, :, sl] += ni * sr - nr * si

        _s5_block_scan(gre_ref, gim_ref, cst_ref, car_ref, tc // 8, True, sw, extra)
        ub = u.astype(BF16)
        for k in range(nsb):
            cs, ss = slice(k * cw, (k + 1) * cw), slice(k * sw, (k + 1) * sw)
            gr, gi = gre_ref[:, ss].astype(BF16), gim_ref[:, ss].astype(BF16)
            duk = dy[:, cs] * d_ref[:, cs] + _dot_nt(gr, bre_ref[k]) + _dot_nt(gi, bim_ref[k])
            du_ref[:, cs] = duk.astype(BF16)
            dbre_ref[k] += _dot_tn(ub[:, cs], gr)
            dbim_ref[k] += _dot_tn(ub[:, cs], gi)

    def rowm(c):
        return pl.BlockSpec((tc, M), lambda i: (nc - 1 - i, c))
    rows = pl.BlockSpec((tc, NS), lambda i: (nc - 1 - i, 0))
    return pl.pallas_call(
        body, name="s5_bwd", grid=(nc,),
        in_specs=[rowm(dcol), rowm(col), rows, rows, rowm(0), rowm(0), _full(bre.shape), _full(bim.shape), _full(cre.shape),
                  _full(cim.shape), _full(cst.shape), _full((1, M)), _full((M, M))],
        out_specs=[rowm(0), _full(bre.shape), _full(bim.shape), _full(cre.shape), _full(cim.shape), _full((2, 8, NS)),
                   _full((M, M)), _full((3, 8, M))],
        out_shape=[jax.ShapeDtypeStruct((T, M), BF16), jax.ShapeDtypeStruct(bre.shape, F32), jax.ShapeDtypeStruct(bim.shape, F32),
                   jax.ShapeDtypeStruct(cre.shape, F32), jax.ShapeDtypeStruct(cim.shape, F32), jax.ShapeDtypeStruct((2, 8, NS), F32),
                   jax.ShapeDtypeStruct((M, M), F32), jax.ShapeDtypeStruct((3, 8, M), F32)],
        scratch_shapes=[pltpu.VMEM((tc, NS), F32), pltpu.VMEM((tc, NS), F32), pltpu.VMEM((2, 8, NS), F32)],
        compiler_params=ARB,
    )(dmix, proj, sre, sim, yp, z, bre, bim, cre, cim, cst, dvec.reshape(1, M), wglu)


def _ln_silu(h1, lg, lb):
    mu = jnp.mean(h1, axis=-1, keepdims=True)
    xc = h1 - mu
    rstd = lax.rsqrt(jnp.mean(xc * xc, axis=-1, keepdims=True) + EPS)
    xhat = xc * rstd
    h2 = xhat * lg + lb
    return xhat, rstd, h2, _sigmoid(h2)


def _cv_fwd(proj, vcol, gcol, wdw, bdw, lg, lb, wpw, bpw):
    T = proj.shape[0]
    M = wpw.shape[0]
    tc = _pick(T, 256, 8)
    H, K = CONV_HALO, 31

    def body(v_ref, g_ref, wdw_ref, bdw_ref, lg_ref, lb_ref, wpw_ref, bpw_ref, y_ref, h1_ref, buf_ref):
        @pl.when(pl.program_id(0) == 0)
        def _():
            buf_ref[0:H, :] = jnp.zeros((H, M), F32)

        buf_ref[H:H + tc, :] = v_ref[...] * _sigmoid(g_ref[...])
        acc = jnp.zeros((tc, M), F32) + bdw_ref[...]
        for k in range(K):
            acc = acc + wdw_ref[k:k + 1, :] * buf_ref[pl.ds(H - (K - 1) + k, tc), :]
        h1_ref[...] = acc
        _, _, h2, sg = _ln_silu(acc, lg_ref[...], lb_ref[...])
        y_ref[...] = (_dot(h2 * sg, wpw_ref[...]) + bpw_ref[...]).astype(BF16)
        buf_ref[0:H, :] = buf_ref[tc:tc + H, :]

    def rowm(c):
        return pl.BlockSpec((tc, M), lambda i: (i, c))
    vec = _full((1, M))
    return pl.pallas_call(
        body, name="cv_fwd", grid=(T // tc,),
        in_specs=[rowm(vcol), rowm(gcol), _full((32, M)), vec, vec, vec, _full((M, M)), vec],
        out_specs=[rowm(0), rowm(0)],
        out_shape=[jax.ShapeDtypeStruct((T, M), BF16), jax.ShapeDtypeStruct((T, M), F32)],
        scratch_shapes=[pltpu.VMEM((tc + H, M), F32)], compiler_params=ARB,
    )(proj, proj, wdw, bdw.reshape(1, M), lg.reshape(1, M), lb.reshape(1, M), wpw, bpw.reshape(1, M))


def _cv_bwd(dmix, dcol, proj, vcol, gcol, h1, wdw, lg, lb, wpw):
    T = proj.shape[0]
    M = wpw.shape[0]
    tc = _pick(T, 256, 8)
    nc = T // tc
    H, K = CONV_HALO, 31
    hb = tc // H

    def body(do_ref, v_ref, g_ref, vh_ref, gh_ref, h1_ref, wdw_ref, lg_ref, lb_ref, wpw_ref,
             dv_ref, dg_ref, dwpw_ref, dwdw_ref, vec_ref, hbuf_ref, dbuf_ref):
        i = pl.program_id(0)

        @pl.when(i == 0)
        def _():
            for r in (dwpw_ref, dwdw_ref, vec_ref):
                r[...] = jnp.zeros_like(r)
            dbuf_ref[tc:tc + H, :] = jnp.zeros((H, M), F32)

        v, g = v_ref[...], g_ref[...]
        sgg = _sigmoid(g)
        halo = vh_ref[...] * _sigmoid(gh_ref[...])
        hbuf_ref[0:H, :] = jnp.where(i == nc - 1, 0.0, halo)
        hbuf_ref[H:H + tc, :] = v * sgg
        do = do_ref[...]
        xhat, rstd, h2, sg = _ln_silu(h1_ref[...], lg_ref[...], lb_ref[...])
        dwpw_ref[...] += _dot_tn(h2 * sg, do)
        vec_ref[0] += _fold8(do)
        dh2 = _dot_nt(do, wpw_ref[...]) * (sg * (1.0 + h2 * (1.0 - sg)))
        vec_ref[1] += _fold8(dh2 * xhat)
        vec_ref[2] += _fold8(dh2)
        dxh = dh2 * lg_ref[...]
        dh1 = rstd * (dxh - jnp.mean(dxh, axis=-1, keepdims=True) - xhat * jnp.mean(dxh * xhat, axis=-1, keepdims=True))
        vec_ref[3] += _fold8(dh1)
        dbuf_ref[0:tc, :] = dh1
        dh0 = jnp.zeros((tc, M), F32)
        for k in range(K):
            dh0 = dh0 + wdw_ref[k:k + 1, :] * dbuf_ref[pl.ds(K - 1 - k, tc), :]
            dwdw_ref[k] += _fold8(dh1 * hbuf_ref[pl.ds(H - (K - 1) + k, tc), :])
        dv_ref[...] = (dh0 * sgg).astype(BF16)
        dg_ref[...] = (dh0 * v * sgg * (1.0 - sgg)).astype(BF16)
        dbuf_ref[tc:tc + H, :] = dbuf_ref[0:H, :]

    def rowm(c):
        return pl.BlockSpec((tc, M), lambda i: (nc - 1 - i, c))

    def halo(c):
        return pl.BlockSpec((H, M), lambda i: (jnp.maximum((nc - 1 - i) * hb - 1, 0), c))
    vec = _full((1, M))
    return pl.pallas_call(
        body, name="cv_bwd", grid=(nc,),
        in_specs=[rowm(dcol), rowm(vcol), rowm(gcol), halo(vcol), halo(gcol), rowm(0), _full((32, M)), vec, vec, _full((M, M))],
        out_specs=[rowm(0), rowm(0), _full((M, M)), _full((32, 8, M)), _full((5, 8, M))],
        out_shape=[jax.ShapeDtypeStruct((T, M), BF16), jax.ShapeDtypeStruct((T, M), BF16), jax.ShapeDtypeStruct((M, M), F32),
                   jax.ShapeDtypeStruct((32, 8, M), F32), jax.ShapeDtypeStruct((5, 8, M), F32)],
        scratch_shapes=[pltpu.VMEM((tc + H, M), F32), pltpu.VMEM((tc + H, M), F32)], compiler_params=ARB,
    )(dmix, proj, proj, proj, proj, h1, wdw, lg.reshape(1, M), lb.reshape(1, M), wpw)


def _lru_gates(xc, r, ig, sp8):
    la = -r * sp8
    a = jnp.exp(la)
    mult = jnp.sqrt(_neg_expm1(2.0 * la))
    return a, mult, mult * (ig * xc)


def _lru_fwd(proj, xcol, gcol, wcv, bcv, wr, br, wi, bi, sp8):
    T = proj.shape[0]
    M = wr.shape[0]
    tc = _pick(T, 256, 8)
    H, K = SMALL_HALO, 4

    def body(x_ref, g_ref, wcv_ref, bcv_ref, wr_ref, br_ref, wi_ref, bi_ref, sp_ref,
             y_ref, xc_ref, r_ref, i_ref, hp_ref, buf_ref, a_ref, car_ref):
        @pl.when(pl.program_id(0) == 0)
        def _():
            buf_ref[0:H, :] = jnp.zeros((H, M), F32)
            car_ref[...] = jnp.zeros_like(car_ref)

        buf_ref[H:H + tc, :] = x_ref[...]
        xc = jnp.zeros((tc, M), F32) + bcv_ref[...]
        for k in range(K):
            xc = xc + wcv_ref[k:k + 1, :] * buf_ref[pl.ds(H - (K - 1) + k, tc), :]
        buf_ref[0:H, :] = buf_ref[tc:tc + H, :]
        r = _sigmoid(_dot(xc, wr_ref[...]) + br_ref[...])
        ig = _sigmoid(_dot(xc, wi_ref[...]) + bi_ref[...])
        xc_ref[...] = xc
        r_ref[...] = r
        i_ref[...] = ig
        a, _, bt = _lru_gates(xc, r, ig, sp_ref[...])
        a_ref[...] = a
        hp_ref[...] = bt
        row = lax.broadcasted_iota(jnp.int32, (8, M), 0)

        def blk(ib, _):
            r0 = pl.multiple_of(ib * 8, 8)
            av, bv = a_ref[pl.ds(r0, 8), :], hp_ref[pl.ds(r0, 8), :]
            for d in (1, 2, 4):
                m = row >= d
                bv = jnp.where(m, bv + av * pltpu.roll(bv, d, 0), bv)
                av = jnp.where(m, av * pltpu.roll(av, d, 0), av)
            cr = car_ref[7:8, :]
            h = bv + av * cr
            a_ref[pl.ds(r0, 8), :] = h
            hp_ref[pl.ds(r0, 8), :] = jnp.where(row == 0, cr, pltpu.roll(h, 1, 0))
            car_ref[...] = h
            return 0

        lax.fori_loop(0, tc // 8, blk, 0)
        y_ref[...] = (a_ref[...] * _gelu(g_ref[...])).astype(BF16)

    def rowm(c):
        return pl.BlockSpec((tc, M), lambda i: (i, c))
    vec = _full((1, M))
    return pl.pallas_call(
        body, name="lru_fwd", grid=(T // tc,),
        in_specs=[rowm(xcol), rowm(gcol), _full((8, M)), vec, _full((M, M)), vec, _full((M, M)), vec, vec],
        out_specs=[rowm(0)] * 5,
        out_shape=[jax.ShapeDtypeStruct((T, M), BF16)] + [jax.ShapeDtypeStruct((T, M), F32)] * 4,
        scratch_shapes=[pltpu.VMEM((tc + H, M), F32), pltpu.VMEM((tc, M), F32), pltpu.VMEM((8, M), F32)], compiler_params=ARB,
    )(proj, proj, wcv, bcv.reshape(1, M), wr, br.reshape(1, M), wi, bi.reshape(1, M), sp8.reshape(1, M))


def _lru_bwd(dmix, dcol, proj, xcol, gcol, xc, r, ig, hprev, wcv, wr, wi, sp8):
    T = proj.shape[0]
    M = wr.shape[0]
    tc = _pick(T, 256, 8)
    nc = T // tc
    H, K = SMALL_HALO, 4
    hb = tc // H

    def body(do_ref, x_ref, g_ref, xh_ref, xc_ref, r_ref, i_ref, hp_ref, wcv_ref, wr_ref, wi_ref, sp_ref,
             dx_ref, dg_ref, dwr_ref, dwi_ref, dwcv_ref, vec_ref, xbuf_ref, dbuf_ref, a_ref, gs_ref, car_ref):
        i = pl.program_id(0)

        @pl.when(i == 0)
        def _():
            for rf in (dwr_ref, dwi_ref, dwcv_ref, vec_ref, car_ref):
                rf[...] = jnp.zeros_like(rf)
            dbuf_ref[tc:tc + H, :] = jnp.zeros((H, M), F32)

        xcv, rv, iv, hpv, gv, do = xc_ref[...], r_ref[...], i_ref[...], hp_ref[...], g_ref[...], do_ref[...]
        sp = sp_ref[...]
        a, mult, bt = _lru_gates(xcv, rv, iv, sp)
        h = a * hpv + bt
        ge = _gelu(gv)
        dg_ref[...] = (do * h * _gelu_grad(gv)).astype(BF16)
        a_ref[...] = a
        gs_ref[...] = do * ge
        row = lax.broadcasted_iota(jnp.int32, (8, M), 0)

        def blk(n, _):
            r0 = pl.multiple_of((tc // 8 - 1 - n) * 8, 8)
            ablk = a_ref[pl.ds(r0, 8), :]
            av = jnp.where(row == 7, car_ref[0, 0:1, :], pltpu.roll(ablk, 7, 0))
            bv = gs_ref[pl.ds(r0, 8), :]
            for d in (1, 2, 4):
                m = row <= 7 - d
                bv = jnp.where(m, bv + av * pltpu.roll(bv, 8 - d, 0), bv)
                av = jnp.where(m, av * pltpu.roll(av, 8 - d, 0), av)
            gsv = bv + av * car_ref[1, 0:1, :]
            gs_ref[pl.ds(r0, 8), :] = gsv
            car_ref[0] = ablk
            car_ref[1] = gsv
            return 0

        lax.fori_loop(0, tc // 8, blk, 0)
        gs = gs_ref[...]
        dixc = gs * mult
        dla = gs * hpv * a - gs * (iv * xcv) * (a * a) / mult
        vec_ref[2] += _fold8(-dla * rv)
        dpr = (-dla * sp) * rv * (1.0 - rv)
        dpi = (dixc * xcv) * iv * (1.0 - iv)
        vec_ref[0] += _fold8(dpr)
        vec_ref[1] += _fold8(dpi)
        dwr_ref[...] += _dot_tn(xcv, dpr)
        dwi_ref[...] += _dot_tn(xcv, dpi)
        dxc = dixc * iv + _dot_nt(dpr, wr_ref[...]) + _dot_nt(dpi, wi_ref[...])
        vec_ref[3] += _fold8(dxc)
        xbuf_ref[0:H, :] = jnp.where(i == nc - 1, 0.0, xh_ref[...])
        xbuf_ref[H:H + tc, :] = x_ref[...]
        dbuf_ref[0:tc, :] = dxc
        dx = jnp.zeros((tc, M), F32)
        for k in range(K):
            dx = dx + wcv_ref[k:k + 1, :] * dbuf_ref[pl.ds(K - 1 - k, tc), :]
            dwcv_ref[k] += _fold8(dxc * xbuf_ref[pl.ds(H - (K - 1) + k, tc), :])
        dx_ref[...] = dx.astype(BF16)
        dbuf_ref[tc:tc + H, :] = dbuf_ref[0:H, :]

    def rowm(c):
        return pl.BlockSpec((tc, M), lambda i: (nc - 1 - i, c))
    halo = pl.BlockSpec((H, M), lambda i: (jnp.maximum((nc - 1 - i) * hb - 1, 0), xcol))
    vec = _full((1, M))
    return pl.pallas_call(
        body, name="lru_bwd", grid=(nc,),
        in_specs=[rowm(dcol), rowm(xcol), rowm(gcol), halo, rowm(0), rowm(0), rowm(0), rowm(0), _full((8, M)),
                  _full((M, M)), _full((M, M)), vec],
        out_specs=[rowm(0), rowm(0), _full((M, M)), _full((M, M)), _full((8, 8, M)), _full((4, 8, M))],
        out_shape=[jax.ShapeDtypeStruct((T, M), BF16), jax.ShapeDtypeStruct((T, M), BF16), jax.ShapeDtypeStruct((M, M), F32),
                   jax.ShapeDtypeStruct((M, M), F32), jax.ShapeDtypeStruct((8, 8, M), F32), jax.ShapeDtypeStruct((4, 8, M), F32)],
        scratch_shapes=[pltpu.VMEM((tc + H, M), F32), pltpu.VMEM((tc + H, M), F32), pltpu.VMEM((tc, M), F32),
                        pltpu.VMEM((tc, M), F32), pltpu.VMEM((2, 8, M), F32)],
        compiler_params=ARB,
    )(dmix, proj, proj, proj, xc, r, ig, hprev, wcv, wr, wi, sp8.reshape(1, M))


def _pool_diffs(buf_ref, x, t0, tc, gw):
    H = POOL_HALO
    pos = (lax.broadcasted_iota(jnp.int32, (tc, gw), 0) + t0 + 1).astype(F32)
    out, inv = [], []
    for gi, win in enumerate(POOL_WINDOWS):
        sl = slice(gi * gw, (gi + 1) * gw)
        s = x[:, sl]
        for j in range(1, win):
            s = s + buf_ref[pl.ds(H - j, tc), sl]
        ic = 1.0 / jnp.minimum(pos, float(win))
        out.append(s * ic - x[:, sl])
        inv.append(ic)
    return out, inv


def _pool_fwd(proj, col, pw, scale):
    T = proj.shape[0]
    ng, gw, _ = pw.shape
    M = ng * gw
    tc = _pick(T, 256, 8)
    H = POOL_HALO

    def body(x_ref, pw_ref, sc_ref, y_ref, buf_ref):
        @pl.when(pl.program_id(0) == 0)
        def _():
            buf_ref[0:H, :] = jnp.zeros((H, M), F32)

        x = x_ref[...]
        buf_ref[H:H + tc, :] = x
        diffs, _ = _pool_diffs(buf_ref, x, pl.program_id(0) * tc, tc, gw)
        for gi in range(ng):
            sl = slice(gi * gw, (gi + 1) * gw)
            y_ref[:, sl] = (_dot(diffs[gi], pw_ref[gi]) * sc_ref[:, sl]).astype(BF16)
        buf_ref[0:H, :] = buf_ref[tc:tc + H, :]

    return pl.pallas_call(
        body, name="pool_fwd", grid=(T // tc,),
        in_specs=[pl.BlockSpec((tc, M), lambda i: (i, col)), _full(pw.shape), _full((1, M))],
        out_specs=pl.BlockSpec((tc, M), lambda i: (i, 0)), out_shape=jax.ShapeDtypeStruct((T, M), BF16),
        scratch_shapes=[pltpu.VMEM((tc + H, M), F32)], compiler_params=ARB,
    )(proj, pw, scale.reshape(1, M))


def _pool_bwd(dmix, dcol, proj, col, pw, scale):
    T = proj.shape[0]
    ng, gw, _ = pw.shape
    M = ng * gw
    tc = _pick(T, 256, 8)
    nc = T // tc
    H = POOL_HALO
    hb = tc // H

    def body(do_ref, x_ref, xh_ref, pw_ref, sc_ref, dx_ref, dpw_ref, dsc_ref, buf_ref, ebuf_ref):
        i = pl.program_id(0)

        @pl.when(i == 0)
        def _():
            dpw_ref[...] = jnp.zeros_like(dpw_ref)
            dsc_ref[...] = jnp.zeros_like(dsc_ref)
            ebuf_ref[tc:tc + H, :] = jnp.zeros((H, M), F32)

        x, do = x_ref[...], do_ref[...]
        buf_ref[0:H, :] = jnp.where(i == nc - 1, 0.0, xh_ref[...])
        buf_ref[H:H + tc, :] = x
        diffs, inv = _pool_diffs(buf_ref, x, (nc - 1 - i) * tc, tc, gw)
        ddg = []
        for gi in range(ng):
            sl = slice(gi * gw, (gi + 1) * gw)
            dyy = do[:, sl] * sc_ref[:, sl]
            dsc_ref[:, sl] += _fold8(do[:, sl] * _dot(diffs[gi], pw_ref[gi]))
            dpw_ref[gi] += _dot_tn(diffs[gi], dyy)
            d = _dot_nt(dyy, pw_ref[gi])
            ddg.append(d)
            ebuf_ref[0:tc, sl] = d * inv[gi]
        for gi, win in enumerate(POOL_WINDOWS):
            sl = slice(gi * gw, (gi + 1) * gw)
            s = -ddg[gi]
            for j in range(win):
                s = s + ebuf_ref[pl.ds(j, tc), sl]
            dx_ref[:, sl] = s.astype(BF16)
        ebuf_ref[tc:tc + H, :] = ebuf_ref[0:H, :]

    return pl.pallas_call(
        body, name="pool_bwd", grid=(nc,),
        in_specs=[pl.BlockSpec((tc, M), lambda i: (nc - 1 - i, dcol)), pl.BlockSpec((tc, M), lambda i: (nc - 1 - i, col)),
                  pl.BlockSpec((H, M), lambda i: (jnp.maximum((nc - 1 - i) * hb - 1, 0), col)), _full(pw.shape), _full((1, M))],
        out_specs=[pl.BlockSpec((tc, M), lambda i: (nc - 1 - i, 0)), _full(pw.shape), _full((8, M))],
        out_shape=[jax.ShapeDtypeStruct((T, M), BF16), jax.ShapeDtypeStruct(pw.shape, F32), jax.ShapeDtypeStruct((8, M), F32)],
        scratch_shapes=[pltpu.VMEM((tc + H, M), F32), pltpu.VMEM((tc + H, M), F32)], compiler_params=ARB,
    )(dmix, proj, proj, pw, scale.reshape(1, M))


def _ffn_mid_fwd(up, wdw, bdw):
    T, F2 = up.shape
    F = F2 // 2
    tc = _pick(T, 256, 8)
    rb = _pick(tc, 64, 16)
    H, K = FFN_HALO, 3

    def body(gt_ref, val_ref, w_ref, b_ref, act_ref, buf_ref):
        @pl.when(pl.program_id(0) == 0)
        def _():
            buf_ref[0:H, :] = jnp.zeros((H, F), F32)

        buf_ref[H:H + tc, :] = gt_ref[...].astype(F32)

        def strip(s, _):
            cs = pl.ds(pl.multiple_of(s * LANES, LANES), LANES)
            taps = [w_ref[k:k + 1, cs] for k in range(K)]
            for r0 in range(0, tc, rb):
                gc = b_ref[:, cs] + taps[0] * buf_ref[pl.ds(H - (K - 1) + r0, rb), cs]
                for k in range(1, K):
                    gc = gc + taps[k] * buf_ref[pl.ds(H - (K - 1) + k + r0, rb), cs]
                act_ref[pl.ds(r0, rb), cs] = (_gelu(gc) * val_ref[pl.ds(r0, rb), cs].astype(F32)).astype(BF16)
            return 0

        lax.fori_loop(0, F // LANES, strip, 0)
        buf_ref[0:H, :] = buf_ref[tc:tc + H, :]

    return pl.pallas_call(
        body, name="ffn_mid_fwd", grid=(T // tc,),
        in_specs=[pl.BlockSpec((tc, F), lambda i: (i, 0)), pl.BlockSpec((tc, F), lambda i: (i, 1)), _full((8, F)), _full((1, F))],
        out_specs=pl.BlockSpec((tc, F), lambda i: (i, 0)), out_shape=jax.ShapeDtypeStruct((T, F), BF16),
        scratch_shapes=[pltpu.VMEM((tc + H, F), F32)], compiler_params=ARB,
    )(up, up, wdw, bdw.reshape(1, F))


def _ffn_mid_bwd(dact, up, wdw, bdw):
    T, F2 = up.shape
    F = F2 // 2
    tc = _pick(T, 128, 8)
    rb = _pick(tc, 64, 16)
    nc = T // tc
    H, K = FFN_HALO, 3
    hb = tc // H

    def body(da_ref, gt_ref, val_ref, gh_ref, w_ref, b_ref, dup_ref, dw_ref, db_ref, gbuf_ref, dbuf_ref):
        i = pl.program_id(0)

        @pl.when(i == 0)
        def _():
            dw_ref[...] = jnp.zeros_like(dw_ref)
            db_ref[...] = jnp.zeros_like(db_ref)
            dbuf_ref[tc:tc + H, :] = jnp.zeros((H, F), F32)

        gbuf_ref[0:H, :] = jnp.where(i == nc - 1, 0.0, gh_ref[...].astype(F32))
        gbuf_ref[H:H + tc, :] = gt_ref[...].astype(F32)

        def strip(s, _):
            cs = pl.ds(pl.multiple_of(s * LANES, LANES), LANES)
            cs_val = pl.ds(pl.multiple_of(F + s * LANES, LANES), LANES)
            taps = [w_ref[k:k + 1, cs] for k in range(K)]
            for r0 in range(0, tc, rb):
                shifted = [gbuf_ref[pl.ds(H - (K - 1) + k + r0, rb), cs] for k in range(K)]
                gc = b_ref[:, cs] + taps[0] * shifted[0]
                for k in range(1, K):
                    gc = gc + taps[k] * shifted[k]
                ge, gg = _gelu_and_grad(gc)
                da = da_ref[pl.ds(r0, rb), cs].astype(F32)
                dup_ref[pl.ds(r0, rb), cs_val] = (da * ge).astype(BF16)
                dgc = da * val_ref[pl.ds(r0, rb), cs].astype(F32) * gg
                dbuf_ref[pl.ds(r0, rb), cs] = dgc
                db_ref[:, cs] += _fold8(dgc)
                for k in range(K):
                    dw_ref[k, :, cs] += _fold8(dgc * shifted[k])
            for r0 in range(0, tc, rb):
                dgt = taps[0] * dbuf_ref[pl.ds(K - 1 + r0, rb), cs]
                for k in range(1, K):
                    dgt = dgt + taps[k] * dbuf_ref[pl.ds(K - 1 - k + r0, rb), cs]
                dup_ref[pl.ds(r0, rb), cs] = dgt.astype(BF16)
            return 0

        lax.fori_loop(0, F // LANES, strip, 0)
        dbuf_ref[tc:tc + H, :] = dbuf_ref[0:H, :]

    return pl.pallas_call(
        body, name="ffn_mid_bwd", grid=(nc,),
        in_specs=[pl.BlockSpec((tc, F), lambda i: (nc - 1 - i, 0)), pl.BlockSpec((tc, F), lambda i: (nc - 1 - i, 0)),
                  pl.BlockSpec((tc, F), lambda i: (nc - 1 - i, 1)),
                  pl.BlockSpec((H, F), lambda i: (jnp.maximum((nc - 1 - i) * hb - 1, 0), 0)), _full((8, F)), _full((1, F))],
        out_specs=[pl.BlockSpec((tc, F2), lambda i: (nc - 1 - i, 0)), _full((8, 8, F)), _full((8, F))],
        out_shape=[jax.ShapeDtypeStruct((T, F2), BF16), jax.ShapeDtypeStruct((8, 8, F), F32), jax.ShapeDtypeStruct((8, F), F32)],
        scratch_shapes=[pltpu.VMEM((tc + H, F), F32), pltpu.VMEM((tc + H, F), F32)], compiler_params=ARB,
    )(dact, up, up, up, wdw, bdw.reshape(1, F))


def _my_pos():
    return lax.axis_index("x"), lax.axis_index("y"), lax.axis_index("c")


def _flip(pos, k):
    x, y, c = pos
    return ((1 - x) if k & 4 else x, (1 - y) if k & 2 else y, (1 - c) if k & 1 else c)


def _rank(pos):
    return 4 * pos[0] + 2 * pos[1] + pos[2]


def _exchange(xs, name, gather):
    n = len(xs)
    shapes = [((N_DEV,) + tuple(x.shape)) if gather else tuple(x.shape) for x in xs]

    def body(*refs):
        ins, outs = refs[:n], refs[n:2 * n]
        send_sems, recv_sems, local_sems = refs[2 * n:]
        me = _my_pos()
        mine = _rank(me)
        local, sends = [], []
        for a in range(n):
            loc = pltpu.make_async_copy(ins[a] if gather else ins[a].at[mine], outs[a].at[mine], local_sems.at[a])
            loc.start()
            local.append(loc)
            for k in range(1, N_DEV):
                peer = _flip(me, k)
                cp = pltpu.make_async_remote_copy(
                    src_ref=ins[a] if gather else ins[a].at[_rank(peer)], dst_ref=outs[a].at[mine],
                    send_sem=send_sems.at[a, k - 1], recv_sem=recv_sems.at[a, k - 1], device_id=peer,
                    device_id_type=pl.DeviceIdType.MESH)
                cp.start()
                sends.append(cp)
        for a in range(n):
            for k in range(1, N_DEV):
                peer = _flip(me, k)
                pltpu.make_async_remote_copy(
                    src_ref=ins[a] if gather else ins[a].at[_rank(peer)], dst_ref=outs[a].at[_rank(peer)],
                    send_sem=send_sems.at[a, k - 1], recv_sem=recv_sems.at[a, k - 1], device_id=peer,
                    device_id_type=pl.DeviceIdType.MESH).wait_recv()
        for cp in sends:
            cp.wait_send()
        for loc in local:
            loc.wait()

    any_spec = pl.BlockSpec(memory_space=pl.ANY)
    return pl.pallas_call(
        body, name=name, in_specs=[any_spec] * n, out_specs=[any_spec] * n,
        out_shape=[jax.ShapeDtypeStruct(s, x.dtype) for s, x in zip(shapes, xs)],
        scratch_shapes=[pltpu.SemaphoreType.DMA((n, N_DEV - 1)), pltpu.SemaphoreType.DMA((n, N_DEV - 1)),
                        pltpu.SemaphoreType.DMA((n,))],
    )(*xs)


_HBM = pl.BlockSpec(memory_space=pltpu.HBM)
_SEM = pl.BlockSpec(memory_space=pltpu.SEMAPHORE)
_EFFECT = pltpu.SideEffectType.DATAFLOW_SIDE_EFFECTING


def _plan_direct(gather):
    def plan(bufs, ssem, rsem, me, outgoing):
        n = len(bufs) // 2
        out = []
        for a in range(n):
            for k in range(1, N_DEV):
                peer = _flip(me, k)
                out.append(pltpu.make_async_remote_copy(
                    src_ref=bufs[a] if gather else bufs[a].at[_rank(peer)],
                    dst_ref=bufs[n + a].at[_rank(me) if outgoing else _rank(peer)],
                    send_sem=ssem.at[a * 7 + k - 1], recv_sem=rsem.at[a * 7 + k - 1], device_id=peer,
                    device_id_type=pl.DeviceIdType.MESH))
        return out
    return plan, 7


_SAME_CORE = (2, 4, 6)


def _plan_own_block(bufs, ssem, rsem, me, outgoing):
    n = len(bufs) // 2
    out = []
    for a in range(n):
        for j, k in enumerate((1,) + _SAME_CORE):
            peer = _flip(me, k)
            out.append(pltpu.make_async_remote_copy(
                src_ref=bufs[a], dst_ref=bufs[n + a].at[_rank(me) if outgoing else _rank(peer)],
                send_sem=ssem.at[a * 4 + j], recv_sem=rsem.at[a * 4 + j], device_id=peer, device_id_type=pl.DeviceIdType.MESH))
    return out


def _plan_pass_on(bufs, ssem, rsem, me, outgoing):
    sibling = _flip(me, 1)
    out = []
    for a in range(len(bufs)):
        for j, k in enumerate(_SAME_CORE):
            slot = _rank(_flip(me, k) if outgoing else _flip(sibling, k))
            out.append(pltpu.make_async_remote_copy(
                src_ref=bufs[a].at[slot], dst_ref=bufs[a].at[slot], send_sem=ssem.at[a * 3 + j], recv_sem=rsem.at[a * 3 + j],
                device_id=sibling, device_id_type=pl.DeviceIdType.MESH))
    return out


def _split_start(bufs, plan, n_sems, name, after=None):
    nb = len(bufs)
    extra = [] if after is None else [after]

    def body(*refs):
        ssem, rsem = refs[nb + len(extra)], refs[nb + len(extra) + 1]
        for cp in plan(refs[:nb], ssem, rsem, _my_pos(), True):
            cp.start()
        token = refs[2 * nb + len(extra) + 2]
        token[...] = jnp.zeros_like(token)

    res = pl.pallas_call(
        body, name=name,
        out_shape=(pltpu.SemaphoreType.DMA((n_sems,)), pltpu.SemaphoreType.DMA((n_sems,)),
                   *[pltpu.HBM(t.shape, t.dtype) for t in bufs], jax.ShapeDtypeStruct((8, 128), F32)),
        in_specs=[_HBM] * nb + [pl.BlockSpec(memory_space=pl.ANY)] * len(extra),
        out_specs=(_SEM, _SEM, *([_HBM] * nb), pl.BlockSpec(memory_space=pltpu.VMEM)),
        input_output_aliases={i: 2 + i for i in range(nb)},
        compiler_params=pltpu.CompilerParams(has_side_effects=_EFFECT),
    )(*[pltpu.with_memory_space_constraint(t, pltpu.HBM) for t in bufs], *extra)
    return (res[0], res[1], list(res[2:2 + nb])), res[2 + nb]


def _split_wait(handle, plan, after, name):
    ssem, rsem, bufs = handle
    nb = len(bufs)

    def body(*refs):
        for cp in plan(refs[:nb], refs[nb], refs[nb + 1], _my_pos(), False):
            cp.wait_send()
            cp.wait_recv()

    res = pl.pallas_call(
        body, name=name, out_shape=tuple(pltpu.HBM(t.shape, t.dtype) for t in bufs),
        in_specs=[_HBM] * nb + [_SEM, _SEM, pl.BlockSpec(memory_space=pl.ANY)], out_specs=tuple([_HBM] * nb),
        input_output_aliases={i: i for i in range(nb)},
        compiler_params=pltpu.CompilerParams(has_side_effects=_EFFECT),
    )(*bufs, ssem, rsem, after)
    return list(res)


def _landing_zones(xs, gather):
    mine = _rank(_my_pos())
    lands = []
    for x in xs:
        own = x if gather else lax.dynamic_index_in_dim(x, mine, 0, keepdims=False)
        lands.append(lax.dynamic_update_index_in_dim(lax.empty((N_DEV,) + tuple(own.shape), x.dtype), own, mine, 0))
    return lands


def _gather_start(xs, name, after=None, two_level=True):
    bufs = list(xs) + _landing_zones(xs, True)
    if two_level:
        handle, token = _split_start(bufs, _plan_own_block, 4 * len(xs), name + "_start", after)
    else:
        plan, per = _plan_direct(True)
        handle, token = _split_start(bufs, plan, per * len(xs), name + "_start", after)
    return (handle, two_level, name), token


def _gather_wait(state, after):
    handle, two_level, name = state
    n = len(handle[2]) // 2
    if not two_level:
        return _split_wait(handle, _plan_direct(True)[0], after, name + "_wait")[n:]
    lands = _split_wait(handle, _plan_own_block, after, name + "_wait")[n:]
    handle, _ = _split_start(lands, _plan_pass_on, 3 * n, name + "_pass")
    return _split_wait(handle, _plan_pass_on, after, name + "_passed")


def _scatter_start(xs, name):
    plan, per = _plan_direct(False)
    handle, token = _split_start(list(xs) + _landing_zones(xs, False), plan, per * len(xs), name + "_start")
    return (handle, name), token


def _scatter_wait(state, after):
    handle, name = state
    return _split_wait(handle, _plan_direct(False)[0], after, name + "_wait")[len(handle[2]) // 2:]


def _adamw_math(w, g, m, v):
    m = ADAM_B1 * m + (1.0 - ADAM_B1) * g
    v = ADAM_B2 * v + (1.0 - ADAM_B2) * (g * g)
    m_hat = m / (1.0 - ADAM_B1 ** ADAM_STEP)
    v_hat = v / (1.0 - ADAM_B2 ** ADAM_STEP)
    return -ADAM_LR * (m_hat / (jnp.sqrt(v_hat) + ADAM_EPS) + ADAM_WD * w), m, v


def _sum8(parts, *, name):
    _, R, C = parts.shape
    tr = _pick(R, 256, 16)

    def body(p_ref, o_ref):
        acc = p_ref[0].astype(F32)
        for r in range(1, N_DEV):
            acc = acc + p_ref[r].astype(F32)
        o_ref[...] = acc

    return pl.pallas_call(
        body, name=name, grid=(R // tr,), in_specs=[pl.BlockSpec((N_DEV, tr, C), lambda i: (0, i, 0))],
        out_specs=pl.BlockSpec((tr, C), lambda i: (i, 0)), out_shape=jax.ShapeDtypeStruct((R, C), F32),
        compiler_params=pltpu.CompilerParams(dimension_semantics=("parallel",)),
    )(parts)


def _adamw_stacked(gs, w, m, v, *, name):
    depth, R, C = w.shape
    tr = _pick(R, 64, 16)
    summed = gs[0].ndim == 3

    def body(*refs):
        g_refs, (w_ref, m_ref, v_ref), outs = refs[:depth], refs[depth:depth + 3], refs[depth + 3:]
        for l in range(depth):
            if summed:
                g = g_refs[l][0].astype(F32)
                for r in range(1, N_DEV):
                    g = g + g_refs[l][r].astype(F32)
            else:
                g = g_refs[l][...]
            d, nm, nv = _adamw_math(w_ref[l], g, m_ref[l], v_ref[l])
            for o, val in zip(outs, (g, d, nm, nv)):
                o[l] = val

    g_spec = pl.BlockSpec((N_DEV, tr, C), lambda i: (0, i, 0)) if summed else pl.BlockSpec((tr, C), lambda i: (i, 0))
    stk = pl.BlockSpec((depth, tr, C), lambda i: (0, i, 0))
    return pl.pallas_call(
        body, name=name, grid=(R // tr,), in_specs=[g_spec] * depth + [stk] * 3, out_specs=[stk] * 4,
        out_shape=[jax.ShapeDtypeStruct((depth, R, C), F32)] * 4,
        compiler_params=pltpu.CompilerParams(dimension_semantics=("parallel",)),
    )(*gs, w, m, v)


def _lane_view(shape):
    if len(shape) == 1:
        return (1, shape[0])
    size = math.prod(shape)
    if shape[-1] < 64 and size % LANES == 0:
        return (size // LANES, LANES)
    if len(shape) > 2 and shape[-2] % 8 == 0:
        return (size // shape[-1], shape[-1])
    return tuple(shape)


def _adamw_many(ws, gs, ms, vs, *, name):
    n = len(ws)
    views = [_lane_view(w.shape) for w in ws]

    def body(*refs):
        for i in range(n):
            d, nm, nv = _adamw_math(refs[i][...], refs[n + i][...], refs[2 * n + i][...], refs[3 * n + i][...])
            refs[4 * n + i][...] = d
            refs[5 * n + i][...] = nm
            refs[6 * n + i][...] = nv

    outs = pl.pallas_call(
        body, name=name, out_shape=[jax.ShapeDtypeStruct(s, F32) for s in views] * 3,
    )(*[a.reshape(s) for arrs in (ws, gs, ms, vs) for a, s in zip(arrs, views)])
    return [[o.reshape(w.shape) for o, w in zip(outs[q * n:(q + 1) * n], ws)] for q in range(3)]


def _pack(arrs):
    flat = jnp.concatenate([a.reshape(-1).astype(F32) for a in arrs])
    pad = (-flat.shape[0]) % 1024
    return jnp.pad(flat, (0, pad)).reshape(-1, 128)


def _unpack(buf, shapes):
    flat, out, off = buf.reshape(-1), [], 0
    for s in shapes:
        n = math.prod(s)
        out.append(flat[off:off + n].reshape(s))
        off += n
    return out


def _s5_params(lam_re, lam_im, log_step, b_re, b_im):
    G, P, H = b_re.shape
    step = jnp.exp(log_step)[:, None]
    mag = jnp.exp(lam_re * step)
    ar, ai = mag * jnp.cos(lam_im * step), mag * jnp.sin(lam_im * step)
    den = lam_re * lam_re + lam_im * lam_im
    qr = ((ar - 1.0) * lam_re + ai * lam_im) / den
    qi = (ai * lam_re - (ar - 1.0) * lam_im) / den
    bbr = qr[..., None] * b_re - qi[..., None] * b_im
    bbi = qr[..., None] * b_im + qi[..., None] * b_re
    return ar.reshape(-1), ai.reshape(-1), _s5_blocks_in(bbr), _s5_blocks_in(bbi)


def _s5_blocks_in(b):
    G, P, H = b.shape
    eye = jnp.eye(8, dtype=b.dtype)
    return jnp.einsum("kgph,gj->kghjp", b.reshape(G // 8, 8, P, H), eye).reshape(G // 8, 8 * H, 8 * P)


def _s5_blocks_in_diag(m, G, P, H):
    return jnp.einsum("kghgp->kgph", m.reshape(G // 8, 8, H, 8, P)).reshape(G, P, H)


def _s5_blocks_out(c):
    G, H, P = c.shape
    eye = jnp.eye(8, dtype=c.dtype)
    return jnp.einsum("kghp,gj->kgpjh", c.reshape(G // 8, 8, H, P), eye).reshape(G // 8, 8 * P, 8 * H)


def _s5_blocks_out_diag(m, G, H, P):
    return jnp.einsum("kgpgh->kghp", m.reshape(G // 8, 8, P, 8, H)).reshape(G, H, P)


def _head_blocks(w):
    nh, d, e = w.shape
    return jnp.einsum("hde,hk->hdke", w, jnp.eye(nh, dtype=w.dtype)).reshape(nh * d, nh * e)


def _head_blocks_diag(m, nh, d, e):
    return jnp.einsum("hdhe->hde", m.reshape(nh, d, nh, e))


def _pad_rows(w, rows):
    return jnp.pad(w, ((0, rows - w.shape[0]), (0, 0)))


def _unshard_cols(g):
    return jnp.transpose(g, (1, 0, 2)).reshape(g.shape[1], -1)


_NAMES = ['norm_mix_g', 'w_in', 's5_lam_re', 's5_lam_im', 's5_log_step', 's5_b_re', 's5_b_im', 's5_c_re', 's5_c_im', 's5_d',
          's5_w_glu', 's5_b_glu', 'cv_w_dw', 'cv_b_dw', 'cv_ln_g', 'cv_ln_b', 'cv_w_pw', 'cv_b_pw', 'lru_w_conv', 'lru_b_conv',
          'lru_w_r', 'lru_b_r', 'lru_w_i', 'lru_b_i', 'lru_lam', 'pool_w', 'pool_scale', 'w_out', 'norm_ffn_g', 'ffn_w_up',
          'ffn_w_dw', 'ffn_b_dw', 'ffn_w_down', 'norm_final_g']
_MATRIX = ('w_in', 'w_out', 'ffn_w_up', 'ffn_w_down', 's5_w_glu', 'cv_w_pw')
_TRANSPOSED = ('w_in', 'ffn_w_up')
_COLSHARD = ('cv_w_dw', 'lru_w_conv', 'ffn_w_dw')
_GROUPS = (('w_in', 's5_w_glu', 'cv_w_pw', 'w_out'), ('ffn_w_up', 'ffn_w_down'))


def kernel(x, norm_mix_g, w_in, s5_lam_re, s5_lam_im, s5_log_step, s5_b_re, s5_b_im, s5_c_re, s5_c_im, s5_d, s5_w_glu, s5_b_glu, cv_w_dw, cv_b_dw, cv_ln_g, cv_ln_b, cv_w_pw, cv_b_pw, lru_w_conv, lru_b_conv, lru_w_r, lru_b_r, lru_w_i, lru_b_i, lru_lam, pool_w, pool_scale, w_out, norm_ffn_g, ffn_w_up, ffn_w_dw, ffn_b_dw, ffn_w_down, norm_final_g, loss_target, m_norm_mix_g, m_w_in, m_s5_lam_re, m_s5_lam_im, m_s5_log_step, m_s5_b_re, m_s5_b_im, m_s5_c_re, m_s5_c_im, m_s5_d, m_s5_w_glu, m_s5_b_glu, m_cv_w_dw, m_cv_b_dw, m_cv_ln_g, m_cv_ln_b, m_cv_w_pw, m_cv_b_pw, m_lru_w_conv, m_lru_b_conv, m_lru_w_r, m_lru_b_r, m_lru_w_i, m_lru_b_i, m_lru_lam, m_pool_w, m_pool_scale, m_w_out, m_norm_ffn_g, m_ffn_w_up, m_ffn_w_dw, m_ffn_b_dw, m_ffn_w_down, m_norm_final_g, v_norm_mix_g, v_w_in, v_s5_lam_re, v_s5_lam_im, v_s5_log_step, v_s5_b_re, v_s5_b_im, v_s5_c_re, v_s5_c_im, v_s5_d, v_s5_w_glu, v_s5_b_glu, v_cv_w_dw, v_cv_b_dw, v_cv_ln_g, v_cv_ln_b, v_cv_w_pw, v_cv_b_pw, v_lru_w_conv, v_lru_b_conv, v_lru_w_r, v_lru_b_r, v_lru_w_i, v_lru_b_i, v_lru_lam, v_pool_w, v_pool_scale, v_w_out, v_norm_ffn_g, v_ffn_w_up, v_ffn_w_dw, v_ffn_b_dw, v_ffn_w_down, v_norm_final_g):
    args = (x, norm_mix_g, w_in, s5_lam_re, s5_lam_im, s5_log_step, s5_b_re, s5_b_im, s5_c_re, s5_c_im, s5_d, s5_w_glu, s5_b_glu, cv_w_dw, cv_b_dw, cv_ln_g, cv_ln_b, cv_w_pw, cv_b_pw, lru_w_conv, lru_b_conv, lru_w_r, lru_b_r, lru_w_i, lru_b_i, lru_lam, pool_w, pool_scale, w_out, norm_ffn_g, ffn_w_up, ffn_w_dw, ffn_b_dw, ffn_w_down, norm_final_g, loss_target, m_norm_mix_g, m_w_in, m_s5_lam_re, m_s5_lam_im, m_s5_log_step, m_s5_b_re, m_s5_b_im, m_s5_c_re, m_s5_c_im, m_s5_d, m_s5_w_glu, m_s5_b_glu, m_cv_w_dw, m_cv_b_dw, m_cv_ln_g, m_cv_ln_b, m_cv_w_pw, m_cv_b_pw, m_lru_w_conv, m_lru_b_conv, m_lru_w_r, m_lru_b_r, m_lru_w_i, m_lru_b_i, m_lru_lam, m_pool_w, m_pool_scale, m_w_out, m_norm_ffn_g, m_ffn_w_up, m_ffn_w_dw, m_ffn_b_dw, m_ffn_w_down, m_norm_final_g, v_norm_mix_g, v_w_in, v_s5_lam_re, v_s5_lam_im, v_s5_log_step, v_s5_b_re, v_s5_b_im, v_s5_c_re, v_s5_c_im, v_s5_d, v_s5_w_glu, v_s5_b_glu, v_cv_w_dw, v_cv_b_dw, v_cv_ln_g, v_cv_ln_b, v_cv_w_pw, v_cv_b_pw, v_lru_w_conv, v_lru_b_conv, v_lru_w_r, v_lru_b_r, v_lru_w_i, v_lru_b_i, v_lru_lam, v_pool_w, v_pool_scale, v_w_out, v_norm_ffn_g, v_ffn_w_up, v_ffn_w_dw, v_ffn_b_dw, v_ffn_w_down, v_norm_final_g)
    x, target = args[0], args[35]
    W = dict(zip(_NAMES, args[1:35]))
    Mo = dict(zip(_NAMES, args[36:70]))
    Vo = dict(zip(_NAMES, args[70:104]))
    x = x[0]
    target = target[0]
    T, D = x.shape
    depth = W['w_in'].shape[0]
    G, P, H = W['s5_b_re'].shape[1:]
    MW = G * H
    nh, hd, _ = W['lru_w_r'].shape[1:]
    me = _rank(_my_pos())

    gather, full = {}, {}

    def start_group(l, gi, after):
        shards = [(W[n][l].T if n in _TRANSPOSED else W[n][l]).astype(BF16) for n in _GROUPS[gi]]
        if (l, gi) == (0, 0):
            shards.append(_pack([W[n][k] for k in range(depth) for n in _COLSHARD]))
        gather[l, gi], tok = _gather_start(shards, "gather_%d%d" % (l, gi), after)
        return tok

    def land_weights(l, gi, after):
        got = _gather_wait(gather[l, gi], after)
        for n, g8 in zip(_GROUPS[gi], got):
            full[n, l] = g8.reshape(-1, g8.shape[-1])
        return got

    started = start_group(0, 0, None)[0, 0]
    got = land_weights(0, 0, W['norm_mix_g'][0] + started)
    started = start_group(0, 1, got[0])[0, 0]
    taps8 = got[-1]
    tap_shapes = [W[n][l].shape for l in range(depth) for n in _COLSHARD]
    taps = [_unshard_cols(jnp.stack(t)) for t in zip(*[_unpack(taps8[r], tap_shapes) for r in range(N_DEV)])]
    for l in range(depth):
        for j, n in enumerate(_COLSHARD):
            full[n, l] = taps[l * len(_COLSHARD) + j]

    saved = []
    for l in range(depth):
        g_mix = W['norm_mix_g'][l] + (0.0 if l else started)
        if l:
            land_weights(l, 0, x)
        s5p, s5p_vjp = jax.vjp(_s5_params, W['s5_lam_re'][l], W['s5_lam_im'][l], W['s5_log_step'][l], W['s5_b_re'][l], W['s5_b_im'][l])
        ar, ai, bre, bim = s5p
        cre, cim = _s5_blocks_out(W['s5_c_re'][l]), _s5_blocks_out(W['s5_c_im'][l])
        sp8, sp8_vjp = jax.vjp(lambda lam: LRU_C * jax.nn.softplus(-lam), W['lru_lam'][l])
        wr, wi = _head_blocks(W['lru_w_r'][l]), _head_blocks(W['lru_w_i'][l])
        c = dict(bre=bre.astype(BF16), bim=bim.astype(BF16), cre=cre.astype(BF16), cim=cim.astype(BF16),
                 cst_f=_s5_scan_consts(ar, ai, False), cst_b=_s5_scan_consts(ar, ai, True), sp8=sp8, wr=wr.astype(BF16),
                 wi=wi.astype(BF16), cvw=_pad_rows(full['cv_w_dw', l], 32), lruw=_pad_rows(full['lru_w_conv', l], 8),
                 ffw=_pad_rows(full['ffn_w_dw', l], 8), pw=W['pool_w'][l].astype(BF16), s5p_vjp=s5p_vjp, sp8_vjp=sp8_vjp, x0=x)
        proj, h = _norm_matmul(x, g_mix, full['w_in', l], tm=1024, tn=512, name="in_proj")
        y0, sre, sim, yp, z = _s5_fwd(proj, 0, c['bre'], c['bim'], c['cre'], c['cim'], c['cst_f'], W['s5_d'][l],
                                     full['s5_w_glu', l], W['s5_b_glu'][l])
        y1, h1 = _cv_fwd(proj, 1, 2, c['cvw'], W['cv_b_dw'][l], W['cv_ln_g'][l], W['cv_ln_b'][l], full['cv_w_pw', l], W['cv_b_pw'][l])
        y2, xc, rr, ig, hprev = _lru_fwd(proj, 3, 4, c['lruw'], W['lru_b_conv'][l], c['wr'], W['lru_b_r'][l], c['wi'],
                                         W['lru_b_i'][l], sp8)
        y3 = _pool_fwd(proj, 5, c['pw'], W['pool_scale'][l])
        mixed = jnp.concatenate([y0, y1, y2, y3], axis=1)
        x1 = _matmul(mixed, full['w_out', l], mode="nn", tm=1024, tn=512, tk=2048, out_dtype=F32, name="out_proj", add=x)
        got = land_weights(l, 1, x1)
        if l + 1 < depth:
            start_group(l + 1, 0, got[0])
            start_group(l + 1, 1, got[0])
        up, h2 = _norm_matmul(x1, W['norm_ffn_g'][l], full['ffn_w_up', l], tm=1024, tn=256, name="ffn_up", out_dtype=BF16)
        act = _ffn_mid_fwd(up, c['ffw'], W['ffn_b_dw'][l])
        F = act.shape[1]
        x = _matmul(act, full['ffn_w_down', l], mode="nn", tm=512, tn=512, tk=F, out_dtype=F32, name="ffn_down", add=x1)
        c.update(proj=proj, h=h, sre=sre, sim=sim, yp=yp, z=z, h1=h1, xc=xc, rr=rr, ig=ig, hprev=hprev, mixed=mixed, x1=x1,
                 up=up, h2=h2, act=act)
        saved.append(c)

    dx, dxb, dgf, se = _loss_head(x, W['norm_final_g'], target)
    loss = lax.psum(0.5 / D * jnp.sum(se), AXES)
    gsmall = {('norm_final_g', None): dgf.sum(0)}
    gmat, scatter = {}, {}

    def parts_of(m):
        return m.reshape(N_DEV, m.shape[0] // N_DEV, m.shape[1])

    for l in reversed(range(depth)):
        c = saved[l]
        F = c['act'].shape[1]
        dact = _matmul(dxb, full['ffn_w_down', l], mode="nt", tm=512, tn=F, tk=512, out_dtype=BF16, name="d_act")
        gmat['ffn_w_down', l] = _matmul(c['act'], dxb, mode="tn", tm=F, tn=512, tk=256, out_dtype=BF16, name="dw_down")
        dup, dffw, dffb = _ffn_mid_bwd(dact, c['up'], c['ffw'], W['ffn_b_dw'][l])
        gsmall['ffn_w_dw', l] = dffw.sum(1)[:3]
        gsmall['ffn_b_dw', l] = dffb.sum(0)
        dh2 = _matmul(dup, full['ffn_w_up', l], mode="nn", tm=1024, tn=512, tk=F, out_dtype=F32, name="d_h2")
        gmat['ffn_w_up', l] = _matmul(dup, c['h2'], mode="tn", tm=256, tn=D, tk=T, out_dtype=BF16, name="dw_up")
        scatter[l, 1], tok = _scatter_start([parts_of(gmat[n, l]) for n in _GROUPS[1]], "scatter_%d1" % l)
        dx1, dx1b, dg2 = _rms_bwd(dx, dh2, c['x1'], W['norm_ffn_g'][l] + tok[0, 0], name="ffn_norm_bwd")
        gsmall['norm_ffn_g', l] = dg2.sum(0)
        dmix = _matmul(dx1b, full['w_out', l], mode="nt", tm=1024, tn=512, tk=2048, out_dtype=F32, name="d_mixed")
        gmat['w_out', l] = _matmul(c['mixed'], dx1b, mode="tn", tm=512, tn=D, tk=2048, out_dtype=BF16, name="dw_out")
        du, dbre, dbim, dcre, dcim, da, dwglu, v3 = _s5_bwd(dmix, 0, c['proj'], 0, c['sre'], c['sim'], c['yp'], c['z'], c['bre'],
                                                            c['bim'], c['cre'], c['cim'], c['cst_b'], W['s5_d'][l], full['s5_w_glu', l])
        da = da.sum(1)
        glr, gli, gls, gbr, gbi = c['s5p_vjp']((da[0], da[1], dbre, dbim))
        gsmall['s5_lam_re', l], gsmall['s5_lam_im', l], gsmall['s5_log_step', l] = glr, gli, gls
        gsmall['s5_b_re', l], gsmall['s5_b_im', l] = gbr, gbi
        gsmall['s5_c_re', l] = _s5_blocks_out_diag(dcre, G, H, P)
        gsmall['s5_c_im', l] = _s5_blocks_out_diag(dcim, G, H, P)
        gsmall['s5_b_glu', l], gsmall['s5_d', l] = v3[0].sum(0), v3[1].sum(0)
        gmat['s5_w_glu', l] = dwglu.astype(BF16)
        dv, dg, dwpw, dcvw, v5 = _cv_bwd(dmix, 1, c['proj'], 1, 2, c['h1'], c['cvw'], W['cv_ln_g'][l], W['cv_ln_b'][l], full['cv_w_pw', l])
        gmat['cv_w_pw', l] = dwpw.astype(BF16)
        gsmall['cv_w_dw', l] = dcvw.sum(1)[:31]
        gsmall['cv_b_pw', l], gsmall['cv_ln_g', l], gsmall['cv_ln_b', l], gsmall['cv_b_dw', l] = [v5[j].sum(0) for j in range(4)]
        dlx, dlg, dwr, dwi, dlw, v4 = _lru_bwd(dmix, 2, c['proj'], 3, 4, c['xc'], c['rr'], c['ig'], c['hprev'], c['lruw'], c['wr'],
                                               c['wi'], c['sp8'])
        gsmall['lru_w_r', l], gsmall['lru_w_i', l] = _head_blocks_diag(dwr, nh, hd, hd), _head_blocks_diag(dwi, nh, hd, hd)
        gsmall['lru_w_conv', l] = dlw.sum(1)[:4]
        gsmall['lru_b_r', l], gsmall['lru_b_i', l] = v4[0].sum(0), v4[1].sum(0)
        gsmall['lru_lam', l] = c['sp8_vjp'](v4[2].sum(0))[0]
        gsmall['lru_b_conv', l] = v4[3].sum(0)
        dpx, dpw, dps = _pool_bwd(dmix, 3, c['proj'], 5, c['pw'], W['pool_scale'][l])
        gsmall['pool_w', l], gsmall['pool_scale', l] = dpw, dps.sum(0)
        dproj = jnp.concatenate([du, dv, dg, dlx, dlg, dpx], axis=1)
        dh = _matmul(dproj, full['w_in', l], mode="nn", tm=1024, tn=512, tk=1024, out_dtype=F32, name="d_h")
        gmat['w_in', l] = _matmul(dproj, c['h'], mode="tn", tm=512, tn=D, tk=2048, out_dtype=BF16, name="dw_in")
        scatter[l, 0], tok = _scatter_start([parts_of(gmat[n, l]) for n in _GROUPS[0]], "scatter_%d0" % l)
        dx, dxb, dg1 = _rms_bwd(dx1, dh, c['x0'], W['norm_mix_g'][l] + tok[0, 0], name="mix_norm_bwd")
        gsmall['norm_mix_g', l] = dg1.sum(0)

    small_names = [n for n in _NAMES if n not in _MATRIX]
    small_parts = [gsmall[n, l] for n in small_names for l in ((None,) if n == 'norm_final_g' else range(depth))]
    full_shapes = [gsmall[n, None].shape if n == 'norm_final_g' else (depth,) + gsmall[n, 0].shape for n in small_names]
    small_state, tok = _gather_start([_pack(small_parts)], "small_grads", two_level=False)
    parts = {}
    for l in reversed(range(depth)):
        for gi in (1, 0):
            got = _scatter_wait(scatter[l, gi], tok)
            for n, p8 in zip(_GROUPS[gi], got):
                parts[n, l] = p8

    out = {}
    for n in _MATRIX:
        if n in _TRANSPOSED and W[n].shape[-1] % LANES:
            wt, mt, vt = (jnp.swapaxes(t, 1, 2) for t in (W[n], Mo[n], Vo[n]))
            res = _adamw_stacked([parts[n, l] for l in range(depth)], wt, mt, vt, name="adamw_" + n)
            out[n] = tuple(jnp.swapaxes(t, 1, 2) for t in res)
            continue
        if n in _TRANSPOSED:
            gl = [_sum8(parts[n, l], name="sum_" + n).T for l in range(depth)]
        else:
            gl = [parts[n, l] for l in range(depth)]
        out[n] = _adamw_stacked(gl, W[n], Mo[n], Vo[n], name="adamw_" + n)
    after_matrices = sum(out[n][1][0, 0, 0] for n in _MATRIX).reshape(1, 1)
    small8 = _gather_wait(small_state, after_matrices)[0]
    small_sum = _sum8(small8, name="sum_small_grads")
    gs = dict(zip(small_names, _unpack(small_sum, full_shapes)))
    for n in _COLSHARD:
        cw = W[n].shape[-1]
        gs[n] = lax.dynamic_slice_in_dim(gs[n], me * cw, cw, axis=gs[n].ndim - 1)
    upd = _adamw_many([W[n] for n in small_names], [gs[n] for n in small_names], [Mo[n] for n in small_names],
                      [Vo[n] for n in small_names], name="adamw_small")
    for j, n in enumerate(small_names):
        out[n] = (gs[n], upd[0][j], upd[1][j], upd[2][j])
    return (loss, dx[None]) + tuple(out[n][q] for q in range(4) for n in _NAMES)
```

```python
import functools
import math

import jax
import jax.numpy as jnp
from jax import lax
from jax.experimental import pallas as pl
from jax.experimental.pallas import tpu as pltpu

F32 = jnp.float32
BF16 = jnp.bfloat16
N_DEV = 8
AXES = ("x", "y", "c")
EPS = 1e-6
S5_GROUP_CH = 16
S5_STATE = 64
LRU_HEADS = 8
LRU_C = 8.0
POOL_WINDOWS = (2, 4, 8, 16)
CONV_HALO = 32
SMALL_HALO = 8
FFN_HALO = 16
POOL_HALO = 16
LANES = 128
ADAM_LR, ADAM_B1, ADAM_B2, ADAM_EPS, ADAM_WD, ADAM_STEP = 0.001, 0.9, 0.999, 1e-08, 0.01, 10
_GELU_K = math.sqrt(2.0 / math.pi)
ARB = pltpu.CompilerParams(dimension_semantics=("arbitrary",))


def _pick(n, pref, mult):
    best = None
    for t in range(mult, min(n, pref) + 1, mult):
        if n % t == 0:
            best = t
    return best if best is not None else n


def _sigmoid(x):
    return 1.0 / (1.0 + jnp.exp(-x))


def _gelu(x):
    return 0.5 * x * (1.0 + jnp.tanh(_GELU_K * (x + 0.044715 * x * x * x)))


def _gelu_grad(x):
    t = jnp.tanh(_GELU_K * (x + 0.044715 * x * x * x))
    return 0.5 * (1.0 + t) + 0.5 * x * (1.0 - t * t) * _GELU_K * (1.0 + 3.0 * 0.044715 * x * x)


def _gelu_and_grad(x):
    t = jnp.tanh(_GELU_K * (x + 0.044715 * x * x * x))
    half = 0.5 * (1.0 + t)
    return x * half, half + 0.5 * x * (1.0 - t * t) * _GELU_K * (1.0 + 3.0 * 0.044715 * x * x)


def _neg_expm1(x):
    series = -x * (1.0 + x * (0.5 + x * (1.0 / 6.0 + x * (1.0 / 24.0 + x * (1.0 / 120.0 + x * (1.0 / 720.0))))))
    return jnp.where(jnp.abs(x) < 0.25, series, 1.0 - jnp.exp(x))


def _fold8(x):
    return x.reshape(x.shape[0] // 8, 8, x.shape[1]).sum(axis=0)


def _dot(a, b):
    return jnp.dot(a.astype(BF16), b.astype(BF16), preferred_element_type=F32)


def _dot_nt(a, b):
    return lax.dot_general(a.astype(BF16), b.astype(BF16), (((1,), (1,)), ((), ())), preferred_element_type=F32)


def _dot_tn(a, b):
    return lax.dot_general(a.astype(BF16), b.astype(BF16), (((0,), (0,)), ((), ())), preferred_element_type=F32)


def _full(shape):
    nd = len(shape)
    return pl.BlockSpec(shape, lambda *i: (0,) * nd)


def _matmul(a, b, *, mode, tm, tn, tk, out_dtype, name, add=None):
    if mode == "nn":
        (M, K), N = a.shape, b.shape[1]
    elif mode == "nt":
        (M, K), N = a.shape, b.shape[0]
    else:
        (K, M), N = a.shape, b.shape[1]
    tm, tn, tk = _pick(M, tm, 8), _pick(N, tn, 128), _pick(K, tk, 128)
    if mode == "tn":
        tm = _pick(M, tm, 128)
    nk = K // tk
    a_spec = {"nn": pl.BlockSpec((tm, tk), lambda i, j, k: (i, k)), "nt": pl.BlockSpec((tm, tk), lambda i, j, k: (i, k)),
              "tn": pl.BlockSpec((tk, tm), lambda i, j, k: (k, i))}[mode]
    b_spec = {"nn": pl.BlockSpec((tk, tn), lambda i, j, k: (k, j)), "nt": pl.BlockSpec((tn, tk), lambda i, j, k: (j, k)),
              "tn": pl.BlockSpec((tk, tn), lambda i, j, k: (k, j))}[mode]
    o_spec = pl.BlockSpec((tm, tn), lambda i, j, k: (i, j))
    dot = {"nn": _dot, "nt": _dot_nt, "tn": _dot_tn}[mode]
    has_add = add is not None

    def body(*refs):
        if has_add:
            a_ref, b_ref, add_ref, o_ref, acc_ref = refs
        else:
            a_ref, b_ref, o_ref, acc_ref = refs
        k = pl.program_id(2)

        @pl.when(k == 0)
        def _():
            acc_ref[...] = jnp.zeros_like(acc_ref)

        acc_ref[...] += dot(a_ref[...], b_ref[...])

        @pl.when(k == nk - 1)
        def _():
            r = acc_ref[...]
            if has_add:
                r = r + add_ref[...]
            o_ref[...] = r.astype(out_dtype)

    ins = [a, b] + ([add] if has_add else [])
    specs = [a_spec, b_spec] + ([o_spec] if has_add else [])
    return pl.pallas_call(
        body, name=name, grid=(M // tm, N // tn, nk), in_specs=specs, out_specs=o_spec,
        out_shape=jax.ShapeDtypeStruct((M, N), out_dtype), scratch_shapes=[pltpu.VMEM((tm, tn), F32)],
        compiler_params=pltpu.CompilerParams(dimension_semantics=("parallel", "parallel", "arbitrary")),
    )(*ins)


def _norm_matmul(x, g, wt, *, tm, tn, name, tk=None, out_dtype=F32):
    T, D = x.shape
    N = wt.shape[0]
    tm, tn, tk = _pick(T, tm, 8), _pick(N, tn, 128), _pick(D, tk or D, 128)
    nk = D // tk

    def body(x_ref, g_ref, w_ref, o_ref, h_ref, acc_ref):
        j, k = pl.program_id(1), pl.program_id(2)

        @pl.when((j == 0) & (k == 0))
        def _():
            xv = x_ref[...]
            r = lax.rsqrt(jnp.mean(xv * xv, axis=-1, keepdims=True) + EPS)
            h_ref[...] = (xv * r * g_ref[...]).astype(BF16)

        part = _dot_nt(h_ref[:, pl.ds(pl.multiple_of(k * tk, LANES), tk)], w_ref[...])
        if nk == 1:
            o_ref[...] = part.astype(out_dtype)
        else:
            @pl.when(k == 0)
            def _():
                acc_ref[...] = part

            @pl.when(k > 0)
            def _():
                acc_ref[...] += part

            @pl.when(k == nk - 1)
            def _():
                o_ref[...] = acc_ref[...].astype(out_dtype)

    return pl.pallas_call(
        body, name=name, grid=(T // tm, N // tn, nk),
        in_specs=[pl.BlockSpec((tm, D), lambda i, j, k: (i, 0)), pl.BlockSpec((1, D), lambda i, j, k: (0, 0)),
                  pl.BlockSpec((tn, tk), lambda i, j, k: (j, k))],
        out_specs=[pl.BlockSpec((tm, tn), lambda i, j, k: (i, j)), pl.BlockSpec((tm, D), lambda i, j, k: (i, 0))],
        out_shape=[jax.ShapeDtypeStruct((T, N), out_dtype), jax.ShapeDtypeStruct((T, D), BF16)],
        scratch_shapes=[pltpu.VMEM((tm, tn) if nk > 1 else (8, LANES), F32)],
        compiler_params=pltpu.CompilerParams(dimension_semantics=("parallel", "arbitrary", "arbitrary")),
    )(x, g.reshape(1, D), wt)


def _rms_bwd(dres, dh, x, g, *, name):
    T, D = x.shape
    tc = _pick(T, 256, 8)

    def body(dres_ref, dh_ref, x_ref, g_ref, dx_ref, dxb_ref, dg_ref):
        @pl.when(pl.program_id(0) == 0)
        def _():
            dg_ref[...] = jnp.zeros_like(dg_ref)

        xv, dhv = x_ref[...], dh_ref[...]
        r = lax.rsqrt(jnp.mean(xv * xv, axis=-1, keepdims=True) + EPS)
        dg_ref[...] += _fold8(dhv * xv * r)
        dyg = dhv * g_ref[...]
        dx = dres_ref[...] + r * dyg - xv * (r * r * r) * jnp.mean(dyg * xv, axis=-1, keepdims=True)
        dx_ref[...] = dx
        dxb_ref[...] = dx.astype(BF16)

    row = pl.BlockSpec((tc, D), lambda i: (i, 0))
    return pl.pallas_call(
        body, name=name, grid=(T // tc,), in_specs=[row, row, row, _full((1, D))],
        out_specs=[row, row, _full((8, D))],
        out_shape=[jax.ShapeDtypeStruct((T, D), F32), jax.ShapeDtypeStruct((T, D), BF16), jax.ShapeDtypeStruct((8, D), F32)],
        compiler_params=ARB,
    )(dres, dh, x, g.reshape(1, D))


def _loss_head(x, g, target):
    T, D = x.shape
    tc = _pick(T, 256, 8)

    def body(x_ref, g_ref, t_ref, dx_ref, dxb_ref, dg_ref, se_ref):
        @pl.when(pl.program_id(0) == 0)
        def _():
            dg_ref[...] = jnp.zeros_like(dg_ref)
            se_ref[...] = jnp.zeros_like(se_ref)

        xv = x_ref[...]
        r = lax.rsqrt(jnp.mean(xv * xv, axis=-1, keepdims=True) + EPS)
        gv = g_ref[...]
        err = xv * r * gv - t_ref[...]
        se_ref[...] += _fold8(err * err)
        dy = err * (1.0 / D)
        dg_ref[...] += _fold8(dy * xv * r)
        dyg = dy * gv
        dx = r * dyg - xv * (r * r * r) * jnp.mean(dyg * xv, axis=-1, keepdims=True)
        dx_ref[...] = dx
        dxb_ref[...] = dx.astype(BF16)

    row = pl.BlockSpec((tc, D), lambda i: (i, 0))
    return pl.pallas_call(
        body, name="loss_head", grid=(T // tc,), in_specs=[row, _full((1, D)), row],
        out_specs=[row, row, _full((8, D)), _full((8, D))],
        out_shape=[jax.ShapeDtypeStruct((T, D), F32), jax.ShapeDtypeStruct((T, D), BF16),
                   jax.ShapeDtypeStruct((8, D), F32), jax.ShapeDtypeStruct((8, D), F32)],
        compiler_params=ARB,
    )(x, g.reshape(1, D), target)


def _s5_scan_consts(ar, ai, reverse):
    if reverse:
        ai = -ai
    pr, pi_ = [ar], [ai]
    for _ in range(7):
        pr, pi_ = pr + [pr[-1] * ar - pi_[-1] * ai], pi_ + [pr[-1] * ai + pi_[-1] * ar]
    rows = jnp.arange(8)[:, None]
    planes = []
    for d in (1, 2, 4):
        mask = (rows <= 7 - d) if reverse else (rows >= d)
        planes += [jnp.where(mask, pr[d - 1][None, :], 0.0), jnp.where(mask, pi_[d - 1][None, :], 0.0)]
    order = list(range(7, -1, -1)) if reverse else list(range(8))
    planes += [jnp.stack([pr[k] for k in order]), jnp.stack([pi_[k] for k in order])]
    return jnp.stack(planes).astype(F32)


def _s5_block_scan(xr_ref, xi_ref, cst_ref, car_ref, nblk, reverse, lb, extra=None):
    ns = xr_ref.shape[1]
    crow = 0 if reverse else 7

    def blk(i, _):
        ib = (nblk - 1 - i) if reverse else i
        r0 = pl.multiple_of(ib * 8, 8)
        for k in range(ns // lb):
            sl = slice(k * lb, (k + 1) * lb)
            xr, xi = xr_ref[pl.ds(r0, 8), sl], xi_ref[pl.ds(r0, 8), sl]
            for n, d in enumerate((1, 2, 4)):
                ar, ai = cst_ref[2 * n, :, sl], cst_ref[2 * n + 1, :, sl]
                sh = (8 - d) if reverse else d
                rr, ri = pltpu.roll(xr, sh, 0), pltpu.roll(xi, sh, 0)
                xr, xi = xr + ar * rr - ai * ri, xi + ar * ri + ai * rr
            pr, pi_ = cst_ref[6, :, sl], cst_ref[7, :, sl]
            cr, ci = car_ref[0, crow:crow + 1, sl], car_ref[1, crow:crow + 1, sl]
            xr, xi = xr + pr * cr - pi_ * ci, xi + pr * ci + pi_ * cr
            xr_ref[pl.ds(r0, 8), sl] = xr
            xi_ref[pl.ds(r0, 8), sl] = xi
            if extra is not None:
                extra(r0, sl, xr, xi, cr, ci)
            car_ref[0, :, sl] = xr
            car_ref[1, :, sl] = xi
        return 0

    lax.fori_loop(0, nblk, blk, 0)


def _s5_fwd(proj, col, bre, bim, cre, cim, cst, dvec, wglu, bglu):
    T = proj.shape[0]
    nsb, cw, sw = bre.shape
    M, NS = nsb * cw, nsb * sw
    tc = _pick(T, 256, 8)

    def body(u_ref, bre_ref, bim_ref, cre_ref, cim_ref, cst_ref, d_ref, wg_ref, bg_ref,
             y_ref, sre_ref, sim_ref, yp_ref, z_ref, car_ref):
        @pl.when(pl.program_id(0) == 0)
        def _():
            car_ref[...] = jnp.zeros_like(car_ref)

        u = u_ref[...]
        ub = u.astype(BF16)
        for k in range(nsb):
            sre_ref[:, k * sw:(k + 1) * sw] = _dot(ub[:, k * cw:(k + 1) * cw], bre_ref[k])
            sim_ref[:, k * sw:(k + 1) * sw] = _dot(ub[:, k * cw:(k + 1) * cw], bim_ref[k])
        _s5_block_scan(sre_ref, sim_ref, cst_ref, car_ref, tc // 8, False, sw)
        for k in range(nsb):
            yk = _dot(sre_ref[:, k * sw:(k + 1) * sw], cre_ref[k]) - _dot(sim_ref[:, k * sw:(k + 1) * sw], cim_ref[k])
            yp_ref[:, k * cw:(k + 1) * cw] = yk + d_ref[:, k * cw:(k + 1) * cw] * u[:, k * cw:(k + 1) * cw]
        gl = _gelu(yp_ref[...])
        z = _dot(gl, wg_ref[...]) + bg_ref[...]
        z_ref[...] = z
        y_ref[...] = (gl * _sigmoid(z)).astype(BF16)

    rowm = pl.BlockSpec((tc, M), lambda i: (i, 0))
    rows = pl.BlockSpec((tc, NS), lambda i: (i, 0))
    return pl.pallas_call(
        body, name="s5_fwd", grid=(T // tc,),
        in_specs=[pl.BlockSpec((tc, M), lambda i: (i, col)), _full(bre.shape), _full(bim.shape), _full(cre.shape),
                  _full(cim.shape), _full(cst.shape), _full((1, M)), _full((M, M)), _full((1, M))],
        out_specs=[rowm, rows, rows, rowm, rowm],
        out_shape=[jax.ShapeDtypeStruct((T, M), BF16), jax.ShapeDtypeStruct((T, NS), F32), jax.ShapeDtypeStruct((T, NS), F32),
                   jax.ShapeDtypeStruct((T, M), F32), jax.ShapeDtypeStruct((T, M), F32)],
        scratch_shapes=[pltpu.VMEM((2, 8, NS), F32)], compiler_params=ARB,
    )(proj, bre, bim, cre, cim, cst, dvec.reshape(1, M), wglu, bglu.reshape(1, M))


def _s5_bwd(dmix, dcol, proj, col, sre, sim, yp, z, bre, bim, cre, cim, cst, dvec, wglu):
    T = proj.shape[0]
    nsb, cw, sw = bre.shape
    M, NS = nsb * cw, nsb * sw
    tc = _pick(T, 256, 8)
    nc = T // tc

    def body(do_ref, u_ref, sre_ref, sim_ref, yp_ref, z_ref, bre_ref, bim_ref, cre_ref, cim_ref, cst_ref, d_ref, wg_ref,
             du_ref, dbre_ref, dbim_ref, dcre_ref, dcim_ref, da_ref, dwg_ref, vec_ref, gre_ref, gim_ref, car_ref):
        @pl.when(pl.program_id(0) == 0)
        def _():
            for r in (dbre_ref, dbim_ref, dcre_ref, dcim_ref, da_ref, dwg_ref, vec_ref, car_ref):
                r[...] = jnp.zeros_like(r)

        u, do, ypv = u_ref[...], do_ref[...], yp_ref[...]
        gl = _gelu(ypv)
        sg = _sigmoid(z_ref[...])
        dz = do * gl * sg * (1.0 - sg)
        dgl = do * sg + _dot_nt(dz, wg_ref[...])
        dwg_ref[...] += _dot_tn(gl, dz)
        vec_ref[0] += _fold8(dz)
        dy = dgl * _gelu_grad(ypv)
        vec_ref[1] += _fold8(dy * u)
        dyb = dy.astype(BF16)
        for k in range(nsb):
            cs, ss = slice(k * cw, (k + 1) * cw), slice(k * sw, (k + 1) * sw)
            gre_ref[:, ss] = _dot_nt(dyb[:, cs], cre_ref[k])
            gim_ref[:, ss] = -_dot_nt(dyb[:, cs], cim_ref[k])
            dcre_ref[k] += _dot_tn(sre_ref[:, ss], dyb[:, cs])
            dcim_ref[k] -= _dot_tn(sim_ref[:, ss], dyb[:, cs])

        row = lax.broadcasted_iota(jnp.int32, (8, sw), 0)

        def extra(r0, sl, gr, gi, cr, ci):
            nr = jnp.where(row == 7, cr, pltpu.roll(gr, 7, 0))
            ni = jnp.where(row == 7, ci, pltpu.roll(gi, 7, 0))
            sr, si = sre_ref[pl.ds(r0, 8), sl], sim_ref[pl.ds(r0, 8), sl]
            da_ref[0, :, sl] += nr * sr + ni * si
            da_ref[1, :, sl] += ni * sr - nr * si

        _s5_block_scan(gre_ref, gim_ref, cst_ref, car_ref, tc // 8, True, sw, extra)
        ub = u.astype(BF16)
        for k in range(nsb):
            cs, ss = slice(k * cw, (k + 1) * cw), slice(k * sw, (k + 1) * sw)
            gr, gi = gre_ref[:, ss].astype(BF16), gim_ref[:, ss].astype(BF16)
            duk = dy[:, cs] * d_ref[:, cs] + _dot_nt(gr, bre_ref[k]) + _dot_nt(gi, bim_ref[k])
            du_ref[:, cs] = duk.astype(BF16)
            dbre_ref[k] += _dot_tn(ub[:, cs], gr)
            dbim_ref[k] += _dot_tn(ub[:, cs], gi)

    def rowm(c):
        return pl.BlockSpec((tc, M), lambda i: (nc - 1 - i, c))
    rows = pl.BlockSpec((tc, NS), lambda i: (nc - 1 - i, 0))
    return pl.pallas_call(
        body, name="s5_bwd", grid=(nc,),
        in_specs=[rowm(dcol), rowm(col), rows, rows, rowm(0), rowm(0), _full(bre.shape), _full(bim.shape), _full(cre.shape),
                  _full(cim.shape), _full(cst.shape), _full((1, M)), _full((M, M))],
        out_specs=[rowm(0), _full(bre.shape), _full(bim.shape), _full(cre.shape), _full(cim.shape), _full((2, 8, NS)),
                   _full((M, M)), _full((3, 8, M))],
        out_shape=[jax.ShapeDtypeStruct((T, M), BF16), jax.ShapeDtypeStruct(bre.shape, F32), jax.ShapeDtypeStruct(bim.shape, F32),
                   jax.ShapeDtypeStruct(cre.shape, F32), jax.ShapeDtypeStruct(cim.shape, F32), jax.ShapeDtypeStruct((2, 8, NS), F32),
                   jax.ShapeDtypeStruct((M, M), F32), jax.ShapeDtypeStruct((3, 8, M), F32)],
        scratch_shapes=[pltpu.VMEM((tc, NS), F32), pltpu.VMEM((tc, NS), F32), pltpu.VMEM((2, 8, NS), F32)],
        compiler_params=ARB,
    )(dmix, proj, sre, sim, yp, z, bre, bim, cre, cim, cst, dvec.reshape(1, M), wglu)


def _ln_silu(h1, lg, lb):
    mu = jnp.mean(h1, axis=-1, keepdims=True)
    xc = h1 - mu
    rstd = lax.rsqrt(jnp.mean(xc * xc, axis=-1, keepdims=True) + EPS)
    xhat = xc * rstd
    h2 = xhat * lg + lb
    return xhat, rstd, h2, _sigmoid(h2)


def _cv_fwd(proj, vcol, gcol, wdw, bdw, lg, lb, wpw, bpw):
    T = proj.shape[0]
    M = wpw.shape[0]
    tc = _pick(T, 256, 8)
    H, K = CONV_HALO, 31

    def body(v_ref, g_ref, wdw_ref, bdw_ref, lg_ref, lb_ref, wpw_ref, bpw_ref, y_ref, h1_ref, buf_ref):
        @pl.when(pl.program_id(0) == 0)
        def _():
            buf_ref[0:H, :] = jnp.zeros((H, M), F32)

        buf_ref[H:H + tc, :] = v_ref[...] * _sigmoid(g_ref[...])
        acc = jnp.zeros((tc, M), F32) + bdw_ref[...]
        for k in range(K):
            acc = acc + wdw_ref[k:k + 1, :] * buf_ref[pl.ds(H - (K - 1) + k, tc), :]
        h1_ref[...] = acc
        _, _, h2, sg = _ln_silu(acc, lg_ref[...], lb_ref[...])
        y_ref[...] = (_dot(h2 * sg, wpw_ref[...]) + bpw_ref[...]).astype(BF16)
        buf_ref[0:H, :] = buf_ref[tc:tc + H, :]

    def rowm(c):
        return pl.BlockSpec((tc, M), lambda i: (i, c))
    vec = _full((1, M))
    return pl.pallas_call(
        body, name="cv_fwd", grid=(T // tc,),
        in_specs=[rowm(vcol), rowm(gcol), _full((32, M)), vec, vec, vec, _full((M, M)), vec],
        out_specs=[rowm(0), rowm(0)],
        out_shape=[jax.ShapeDtypeStruct((T, M), BF16), jax.ShapeDtypeStruct((T, M), F32)],
        scratch_shapes=[pltpu.VMEM((tc + H, M), F32)], compiler_params=ARB,
    )(proj, proj, wdw, bdw.reshape(1, M), lg.reshape(1, M), lb.reshape(1, M), wpw, bpw.reshape(1, M))


def _cv_bwd(dmix, dcol, proj, vcol, gcol, h1, wdw, lg, lb, wpw):
    T = proj.shape[0]
    M = wpw.shape[0]
    tc = _pick(T, 256, 8)
    nc = T // tc
    H, K = CONV_HALO, 31
    hb = tc // H

    def body(do_ref, v_ref, g_ref, vh_ref, gh_ref, h1_ref, wdw_ref, lg_ref, lb_ref, wpw_ref,
             dv_ref, dg_ref, dwpw_ref, dwdw_ref, vec_ref, hbuf_ref, dbuf_ref):
        i = pl.program_id(0)

        @pl.when(i == 0)
        def _():
            for r in (dwpw_ref, dwdw_ref, vec_ref):
                r[...] = jnp.zeros_like(r)
            dbuf_ref[tc:tc + H, :] = jnp.zeros((H, M), F32)

        v, g = v_ref[...], g_ref[...]
        sgg = _sigmoid(g)
        halo = vh_ref[...] * _sigmoid(gh_ref[...])
        hbuf_ref[0:H, :] = jnp.where(i == nc - 1, 0.0, halo)
        hbuf_ref[H:H + tc, :] = v * sgg
        do = do_ref[...]
        xhat, rstd, h2, sg = _ln_silu(h1_ref[...], lg_ref[...], lb_ref[...])
        dwpw_ref[...] += _dot_tn(h2 * sg, do)
        vec_ref[0] += _fold8(do)
        dh2 = _dot_nt(do, wpw_ref[...]) * (sg * (1.0 + h2 * (1.0 - sg)))
        vec_ref[1] += _fold8(dh2 * xhat)
        vec_ref[2] += _fold8(dh2)
        dxh = dh2 * lg_ref[...]
        dh1 = rstd * (dxh - jnp.mean(dxh, axis=-1, keepdims=True) - xhat * jnp.mean(dxh * xhat, axis=-1, keepdims=True))
        vec_ref[3] += _fold8(dh1)
        dbuf_ref[0:tc, :] = dh1
        dh0 = jnp.zeros((tc, M), F32)
        for k in range(K):
            dh0 = dh0 + wdw_ref[k:k + 1, :] * dbuf_ref[pl.ds(K - 1 - k, tc), :]
            dwdw_ref[k] += _fold8(dh1 * hbuf_ref[pl.ds(H - (K - 1) + k, tc), :])
        dv_ref[...] = (dh0 * sgg).astype(BF16)
        dg_ref[...] = (dh0 * v * sgg * (1.0 - sgg)).astype(BF16)
        dbuf_ref[tc:tc + H, :] = dbuf_ref[0:H, :]

    def rowm(c):
        return pl.BlockSpec((tc, M), lambda i: (nc - 1 - i, c))

    def halo(c):
        return pl.BlockSpec((H, M), lambda i: (jnp.maximum((nc - 1 - i) * hb - 1, 0), c))
    vec = _full((1, M))
    return pl.pallas_call(
        body, name="cv_bwd", grid=(nc,),
        in_specs=[rowm(dcol), rowm(vcol), rowm(gcol), halo(vcol), halo(gcol), rowm(0), _full((32, M)), vec, vec, _full((M, M))],
        out_specs=[rowm(0), rowm(0), _full((M, M)), _full((32, 8, M)), _full((5, 8, M))],
        out_shape=[jax.ShapeDtypeStruct((T, M), BF16), jax.ShapeDtypeStruct((T, M), BF16), jax.ShapeDtypeStruct((M, M), F32),
                   jax.ShapeDtypeStruct((32, 8, M), F32), jax.ShapeDtypeStruct((5, 8, M), F32)],
        scratch_shapes=[pltpu.VMEM((tc + H, M), F32), pltpu.VMEM((tc + H, M), F32)], compiler_params=ARB,
    )(dmix, proj, proj, proj, proj, h1, wdw, lg.reshape(1, M), lb.reshape(1, M), wpw)


def _lru_gates(xc, r, ig, sp8):
    la = -r * sp8
    a = jnp.exp(la)
    mult = jnp.sqrt(_neg_expm1(2.0 * la))
    return a, mult, mult * (ig * xc)


def _lru_fwd(proj, xcol, gcol, wcv, bcv, wr, br, wi, bi, sp8):
    T = proj.shape[0]
    M = wr.shape[0]
    tc = _pick(T, 256, 8)
    H, K = SMALL_HALO, 4

    def body(x_ref, g_ref, wcv_ref, bcv_ref, wr_ref, br_ref, wi_ref, bi_ref, sp_ref,
             y_ref, xc_ref, r_ref, i_ref, hp_ref, buf_ref, a_ref, car_ref):
        @pl.when(pl.program_id(0) == 0)
        def _():
            buf_ref[0:H, :] = jnp.zeros((H, M), F32)
            car_ref[...] = jnp.zeros_like(car_ref)

        buf_ref[H:H + tc, :] = x_ref[...]
        xc = jnp.zeros((tc, M), F32) + bcv_ref[...]
        for k in range(K):
            xc = xc + wcv_ref[k:k + 1, :] * buf_ref[pl.ds(H - (K - 1) + k, tc), :]
        buf_ref[0:H, :] = buf_ref[tc:tc + H, :]
        r = _sigmoid(_dot(xc, wr_ref[...]) + br_ref[...])
        ig = _sigmoid(_dot(xc, wi_ref[...]) + bi_ref[...])
        xc_ref[...] = xc
        r_ref[...] = r
        i_ref[...] = ig
        a, _, bt = _lru_gates(xc, r, ig, sp_ref[...])
        a_ref[...] = a
        hp_ref[...] = bt
        row = lax.broadcasted_iota(jnp.int32, (8, M), 0)

        def blk(ib, _):
            r0 = pl.multiple_of(ib * 8, 8)
            av, bv = a_ref[pl.ds(r0, 8), :], hp_ref[pl.ds(r0, 8), :]
            for d in (1, 2, 4):
                m = row >= d
                bv = jnp.where(m, bv + av * pltpu.roll(bv, d, 0), bv)
                av = jnp.where(m, av * pltpu.roll(av, d, 0), av)
            cr = car_ref[7:8, :]
            h = bv + av * cr
            a_ref[pl.ds(r0, 8), :] = h
            hp_ref[pl.ds(r0, 8), :] = jnp.where(row == 0, cr, pltpu.roll(h, 1, 0))
            car_ref[...] = h
            return 0

        lax.fori_loop(0, tc // 8, blk, 0)
        y_ref[...] = (a_ref[...] * _gelu(g_ref[...])).astype(BF16)

    def rowm(c):
        return pl.BlockSpec((tc, M), lambda i: (i, c))
    vec = _full((1, M))
    return pl.pallas_call(
        body, name="lru_fwd", grid=(T // tc,),
        in_specs=[rowm(xcol), rowm(gcol), _full((8, M)), vec, _full((M, M)), vec, _full((M, M)), vec, vec],
        out_specs=[rowm(0)] * 5,
        out_shape=[jax.ShapeDtypeStruct((T, M), BF16)] + [jax.ShapeDtypeStruct((T, M), F32)] * 4,
        scratch_shapes=[pltpu.VMEM((tc + H, M), F32), pltpu.VMEM((tc, M), F32), pltpu.VMEM((8, M), F32)], compiler_params=ARB,
    )(proj, proj, wcv, bcv.reshape(1, M), wr, br.reshape(1, M), wi, bi.reshape(1, M), sp8.reshape(1, M))


def _lru_bwd(dmix, dcol, proj, xcol, gcol, xc, r, ig, hprev, wcv, wr, wi, sp8):
    T = proj.shape[0]
    M = wr.shape[0]
    tc = _pick(T, 256, 8)
    nc = T // tc
    H, K = SMALL_HALO, 4
    hb = tc // H

    def body(do_ref, x_ref, g_ref, xh_ref, xc_ref, r_ref, i_ref, hp_ref, wcv_ref, wr_ref, wi_ref, sp_ref,
             dx_ref, dg_ref, dwr_ref, dwi_ref, dwcv_ref, vec_ref, xbuf_ref, dbuf_ref, a_ref, gs_ref, car_ref):
        i = pl.program_id(0)

        @pl.when(i == 0)
        def _():
            for rf in (dwr_ref, dwi_ref, dwcv_ref, vec_ref, car_ref):
                rf[...] = jnp.zeros_like(rf)
            dbuf_ref[tc:tc + H, :] = jnp.zeros((H, M), F32)

        xcv, rv, iv, hpv, gv, do = xc_ref[...], r_ref[...], i_ref[...], hp_ref[...], g_ref[...], do_ref[...]
        sp = sp_ref[...]
        a, mult, bt = _lru_gates(xcv, rv, iv, sp)
        h = a * hpv + bt
        ge = _gelu(gv)
        dg_ref[...] = (do * h * _gelu_grad(gv)).astype(BF16)
        a_ref[...] = a
        gs_ref[...] = do * ge
        row = lax.broadcasted_iota(jnp.int32, (8, M), 0)

        def blk(n, _):
            r0 = pl.multiple_of((tc // 8 - 1 - n) * 8, 8)
            ablk = a_ref[pl.ds(r0, 8), :]
            av = jnp.where(row == 7, car_ref[0, 0:1, :], pltpu.roll(ablk, 7, 0))
            bv = gs_ref[pl.ds(r0, 8), :]
            for d in (1, 2, 4):
                m = row <= 7 - d
                bv = jnp.where(m, bv + av * pltpu.roll(bv, 8 - d, 0), bv)
                av = jnp.where(m, av * pltpu.roll(av, 8 - d, 0), av)
            gsv = bv + av * car_ref[1, 0:1, :]
            gs_ref[pl.ds(r0, 8), :] = gsv
            car_ref[0] = ablk
            car_ref[1] = gsv
            return 0

        lax.fori_loop(0, tc // 8, blk, 0)
        gs = gs_ref[...]
        dixc = gs * mult
        dla = gs * hpv * a - gs * (iv * xcv) * (a * a) / mult
        vec_ref[2] += _fold8(-dla * rv)
        dpr = (-dla * sp) * rv * (1.0 - rv)
        dpi = (dixc * xcv) * iv * (1.0 - iv)
        vec_ref[0] += _fold8(dpr)
        vec_ref[1] += _fold8(dpi)
        dwr_ref[...] += _dot_tn(xcv, dpr)
        dwi_ref[...] += _dot_tn(xcv, dpi)
        dxc = dixc * iv + _dot_nt(dpr, wr_ref[...]) + _dot_nt(dpi, wi_ref[...])
        vec_ref[3] += _fold8(dxc)
        xbuf_ref[0:H, :] = jnp.where(i == nc - 1, 0.0, xh_ref[...])
        xbuf_ref[H:H + tc, :] = x_ref[...]
        dbuf_ref[0:tc, :] = dxc
        dx = jnp.zeros((tc, M), F32)
        for k in range(K):
            dx = dx + wcv_ref[k:k + 1, :] * dbuf_ref[pl.ds(K - 1 - k, tc), :]
            dwcv_ref[k] += _fold8(dxc * xbuf_ref[pl.ds(H - (K - 1) + k, tc), :])
        dx_ref[...] = dx.astype(BF16)
        dbuf_ref[tc:tc + H, :] = dbuf_ref[0:H, :]

    def rowm(c):
        return pl.BlockSpec((tc, M), lambda i: (nc - 1 - i, c))
    halo = pl.BlockSpec((H, M), lambda i: (jnp.maximum((nc - 1 - i) * hb - 1, 0), xcol))
    vec = _full((1, M))
    return pl.pallas_call(
        body, name="lru_bwd", grid=(nc,),
        in_specs=[rowm(dcol), rowm(xcol), rowm(gcol), halo, rowm(0), rowm(0), rowm(0), rowm(0), _full((8, M)),
                  _full((M, M)), _full((M, M)), vec],
        out_specs=[rowm(0), rowm(0), _full((M, M)), _full((M, M)), _full((8, 8, M)), _full((4, 8, M))],
        out_shape=[jax.ShapeDtypeStruct((T, M), BF16), jax.ShapeDtypeStruct((T, M), BF16), jax.ShapeDtypeStruct((M, M), F32),
                   jax.ShapeDtypeStruct((M, M), F32), jax.ShapeDtypeStruct((8, 8, M), F32), jax.ShapeDtypeStruct((4, 8, M), F32)],
        scratch_shapes=[pltpu.VMEM((tc + H, M), F32), pltpu.VMEM((tc + H, M), F32), pltpu.VMEM((tc, M), F32),
                        pltpu.VMEM((tc, M), F32), pltpu.VMEM((2, 8, M), F32)],
        compiler_params=ARB,
    )(dmix, proj, proj, proj, xc, r, ig, hprev, wcv, wr, wi, sp8.reshape(1, M))


def _pool_diffs(buf_ref, x, t0, tc, gw):
    H = POOL_HALO
    pos = (lax.broadcasted_iota(jnp.int32, (tc, gw), 0) + t0 + 1).astype(F32)
    out, inv = [], []
    for gi, win in enumerate(POOL_WINDOWS):
        sl = slice(gi * gw, (gi + 1) * gw)
        s = x[:, sl]
        for j in range(1, win):
            s = s + buf_ref[pl.ds(H - j, tc), sl]
        ic = 1.0 / jnp.minimum(pos, float(win))
        out.append(s * ic - x[:, sl])
        inv.append(ic)
    return out, inv


def _pool_fwd(proj, col, pw, scale):
    T = proj.shape[0]
    ng, gw, _ = pw.shape
    M = ng * gw
    tc = _pick(T, 256, 8)
    H = POOL_HALO

    def body(x_ref, pw_ref, sc_ref, y_ref, buf_ref):
        @pl.when(pl.program_id(0) == 0)
        def _():
            buf_ref[0:H, :] = jnp.zeros((H, M), F32)

        x = x_ref[...]
        buf_ref[H:H + tc, :] = x
        diffs, _ = _pool_diffs(buf_ref, x, pl.program_id(0) * tc, tc, gw)
        for gi in range(ng):
            sl = slice(gi * gw, (gi + 1) * gw)
            y_ref[:, sl] = (_dot(diffs[gi], pw_ref[gi]) * sc_ref[:, sl]).astype(BF16)
        buf_ref[0:H, :] = buf_ref[tc:tc + H, :]

    return pl.pallas_call(
        body, name="pool_fwd", grid=(T // tc,),
        in_specs=[pl.BlockSpec((tc, M), lambda i: (i, col)), _full(pw.shape), _full((1, M))],
        out_specs=pl.BlockSpec((tc, M), lambda i: (i, 0)), out_shape=jax.ShapeDtypeStruct((T, M), BF16),
        scratch_shapes=[pltpu.VMEM((tc + H, M), F32)], compiler_params=ARB,
    )(proj, pw, scale.reshape(1, M))


def _pool_bwd(dmix, dcol, proj, col, pw, scale):
    T = proj.shape[0]
    ng, gw, _ = pw.shape
    M = ng * gw
    tc = _pick(T, 256, 8)
    nc = T // tc
    H = POOL_HALO
    hb = tc // H

    def body(do_ref, x_ref, xh_ref, pw_ref, sc_ref, dx_ref, dpw_ref, dsc_ref, buf_ref, ebuf_ref):
        i = pl.program_id(0)

        @pl.when(i == 0)
        def _():
            dpw_ref[...] = jnp.zeros_like(dpw_ref)
            dsc_ref[...] = jnp.zeros_like(dsc_ref)
            ebuf_ref[tc:tc + H, :] = jnp.zeros((H, M), F32)

        x, do = x_ref[...], do_ref[...]
        buf_ref[0:H, :] = jnp.where(i == nc - 1, 0.0, xh_ref[...])
        buf_ref[H:H + tc, :] = x
        diffs, inv = _pool_diffs(buf_ref, x, (nc - 1 - i) * tc, tc, gw)
        ddg = []
        for gi in range(ng):
            sl = slice(gi * gw, (gi + 1) * gw)
            dyy = do[:, sl] * sc_ref[:, sl]
            dsc_ref[:, sl] += _fold8(do[:, sl] * _dot(diffs[gi], pw_ref[gi]))
            dpw_ref[gi] += _dot_tn(diffs[gi], dyy)
            d = _dot_nt(dyy, pw_ref[gi])
            ddg.append(d)
            ebuf_ref[0:tc, sl] = d * inv[gi]
        for gi, win in enumerate(POOL_WINDOWS):
            sl = slice(gi * gw, (gi + 1) * gw)
            s = -ddg[gi]
            for j in range(win):
                s = s + ebuf_ref[pl.ds(j, tc), sl]
            dx_ref[:, sl] = s.astype(BF16)
        ebuf_ref[tc:tc + H, :] = ebuf_ref[0:H, :]

    return pl.pallas_call(
        body, name="pool_bwd", grid=(nc,),
        in_specs=[pl.BlockSpec((tc, M), lambda i: (nc - 1 - i, dcol)), pl.BlockSpec((tc, M), lambda i: (nc - 1 - i, col)),
                  pl.BlockSpec((H, M), lambda i: (jnp.maximum((nc - 1 - i) * hb - 1, 0), col)), _full(pw.shape), _full((1, M))],
        out_specs=[pl.BlockSpec((tc, M), lambda i: (nc - 1 - i, 0)), _full(pw.shape), _full((8, M))],
        out_shape=[jax.ShapeDtypeStruct((T, M), BF16), jax.ShapeDtypeStruct(pw.shape, F32), jax.ShapeDtypeStruct((8, M), F32)],
        scratch_shapes=[pltpu.VMEM((tc + H, M), F32), pltpu.VMEM((tc + H, M), F32)], compiler_params=ARB,
    )(dmix, proj, proj, pw, scale.reshape(1, M))


def _ffn_mid_fwd(up, wdw, bdw):
    T, F2 = up.shape
    F = F2 // 2
    tc = _pick(T, 256, 8)
    rb = _pick(tc, 64, 16)
    H, K = FFN_HALO, 3

    def body(gt_ref, val_ref, w_ref, b_ref, act_ref, buf_ref):
        @pl.when(pl.program_id(0) == 0)
        def _():
            buf_ref[0:H, :] = jnp.zeros((H, F), F32)

        buf_ref[H:H + tc, :] = gt_ref[...].astype(F32)

        def strip(s, _):
            cs = pl.ds(pl.multiple_of(s * LANES, LANES), LANES)
            taps = [w_ref[k:k + 1, cs] for k in range(K)]
            for r0 in range(0, tc, rb):
                gc = b_ref[:, cs] + taps[0] * buf_ref[pl.ds(H - (K - 1) + r0, rb), cs]
                for k in range(1, K):
                    gc = gc + taps[k] * buf_ref[pl.ds(H - (K - 1) + k + r0, rb), cs]
                act_ref[pl.ds(r0, rb), cs] = (_gelu(gc) * val_ref[pl.ds(r0, rb), cs].astype(F32)).astype(BF16)
            return 0

        lax.fori_loop(0, F // LANES, strip, 0)
        buf_ref[0:H, :] = buf_ref[tc:tc + H, :]

    return pl.pallas_call(
        body, name="ffn_mid_fwd", grid=(T // tc,),
        in_specs=[pl.BlockSpec((tc, F), lambda i: (i, 0)), pl.BlockSpec((tc, F), lambda i: (i, 1)), _full((8, F)), _full((1, F))],
        out_specs=pl.BlockSpec((tc, F), lambda i: (i, 0)), out_shape=jax.ShapeDtypeStruct((T, F), BF16),
        scratch_shapes=[pltpu.VMEM((tc + H, F), F32)], compiler_params=ARB,
    )(up, up, wdw, bdw.reshape(1, F))


def _ffn_mid_bwd(dact, up, wdw, bdw):
    T, F2 = up.shape
    F = F2 // 2
    tc = _pick(T, 128, 8)
    rb = _pick(tc, 64, 16)
    nc = T // tc
    H, K = FFN_HALO, 3
    hb = tc // H

    def body(da_ref, gt_ref, val_ref, gh_ref, w_ref, b_ref, dup_ref, dw_ref, db_ref, gbuf_ref, dbuf_ref):
        i = pl.program_id(0)

        @pl.when(i == 0)
        def _():
            dw_ref[...] = jnp.zeros_like(dw_ref)
            db_ref[...] = jnp.zeros_like(db_ref)
            dbuf_ref[tc:tc + H, :] = jnp.zeros((H, F), F32)

        gbuf_ref[0:H, :] = jnp.where(i == nc - 1, 0.0, gh_ref[...].astype(F32))
        gbuf_ref[H:H + tc, :] = gt_ref[...].astype(F32)

        def strip(s, _):
            cs = pl.ds(pl.multiple_of(s * LANES, LANES), LANES)
            cs_val = pl.ds(pl.multiple_of(F + s * LANES, LANES), LANES)
            taps = [w_ref[k:k + 1, cs] for k in range(K)]
            for r0 in range(0, tc, rb):
                shifted = [gbuf_ref[pl.ds(H - (K - 1) + k + r0, rb), cs] for k in range(K)]
                gc = b_ref[:, cs] + taps[0] * shifted[0]
                for k in range(1, K):
                    gc = gc + taps[k] * shifted[k]
                ge, gg = _gelu_and_grad(gc)
                da = da_ref[pl.ds(r0, rb), cs].astype(F32)
                dup_ref[pl.ds(r0, rb), cs_val] = (da * ge).astype(BF16)
                dgc = da * val_ref[pl.ds(r0, rb), cs].astype(F32) * gg
                dbuf_ref[pl.ds(r0, rb), cs] = dgc
                db_ref[:, cs] += _fold8(dgc)
                for k in range(K):
                    dw_ref[k, :, cs] += _fold8(dgc * shifted[k])
            for r0 in range(0, tc, rb):
                dgt = taps[0] * dbuf_ref[pl.ds(K - 1 + r0, rb), cs]
                for k in range(1, K):
                    dgt = dgt + taps[k] * dbuf_ref[pl.ds(K - 1 - k + r0, rb), cs]
                dup_ref[pl.ds(r0, rb), cs] = dgt.astype(BF16)
            return 0

        lax.fori_loop(0, F // LANES, strip, 0)
        dbuf_ref[tc:tc + H, :] = dbuf_ref[0:H, :]

    return pl.pallas_call(
        body, name="ffn_mid_bwd", grid=(nc,),
        in_specs=[pl.BlockSpec((tc, F), lambda i: (nc - 1 - i, 0)), pl.BlockSpec((tc, F), lambda i: (nc - 1 - i, 0)),
                  pl.BlockSpec((tc, F), lambda i: (nc - 1 - i, 1)),
                  pl.BlockSpec((H, F), lambda i: (jnp.maximum((nc - 1 - i) * hb - 1, 0), 0)), _full((8, F)), _full((1, F))],
        out_specs=[pl.BlockSpec((tc, F2), lambda i: (nc - 1 - i, 0)), _full((8, 8, F)), _full((8, F))],
        out_shape=[jax.ShapeDtypeStruct((T, F2), BF16), jax.ShapeDtypeStruct((8, 8, F), F32), jax.ShapeDtypeStruct((8, F), F32)],
        scratch_shapes=[pltpu.VMEM((tc + H, F), F32), pltpu.VMEM((tc + H, F), F32)], compiler_params=ARB,
    )(dact, up, up, up, wdw, bdw.reshape(1, F))


def _my_pos():
    return lax.axis_index("x"), lax.axis_index("y"), lax.axis_index("c")


def _flip(pos, k):
    x, y, c = pos
    return ((1 - x) if k & 4 else x, (1 - y) if k & 2 else y, (1 - c) if k & 1 else c)


def _rank(pos):
    return 4 * pos[0] + 2 * pos[1] + pos[2]


def _exchange(xs, name, gather):
    n = len(xs)
    shapes = [((N_DEV,) + tuple(x.shape)) if gather else tuple(x.shape) for x in xs]

    def body(*refs):
        ins, outs = refs[:n], refs[n:2 * n]
        send_sems, recv_sems, local_sems = refs[2 * n:]
        me = _my_pos()
        mine = _rank(me)
        local, sends = [], []
        for a in range(n):
            loc = pltpu.make_async_copy(ins[a] if gather else ins[a].at[mine], outs[a].at[mine], local_sems.at[a])
            loc.start()
            local.append(loc)
            for k in range(1, N_DEV):
                peer = _flip(me, k)
                cp = pltpu.make_async_remote_copy(
                    src_ref=ins[a] if gather else ins[a].at[_rank(peer)], dst_ref=outs[a].at[mine],
                    send_sem=send_sems.at[a, k - 1], recv_sem=recv_sems.at[a, k - 1], device_id=peer,
                    device_id_type=pl.DeviceIdType.MESH)
                cp.start()
                sends.append(cp)
        for a in range(n):
            for k in range(1, N_DEV):
                peer = _flip(me, k)
                pltpu.make_async_remote_copy(
                    src_ref=ins[a] if gather else ins[a].at[_rank(peer)], dst_ref=outs[a].at[_rank(peer)],
                    send_sem=send_sems.at[a, k - 1], recv_sem=recv_sems.at[a, k - 1], device_id=peer,
                    device_id_type=pl.DeviceIdType.MESH).wait_recv()
        for cp in sends:
            cp.wait_send()
        for loc in local:
            loc.wait()

    any_spec = pl.BlockSpec(memory_space=pl.ANY)
    return pl.pallas_call(
        body, name=name, in_specs=[any_spec] * n, out_specs=[any_spec] * n,
        out_shape=[jax.ShapeDtypeStruct(s, x.dtype) for s, x in zip(shapes, xs)],
        scratch_shapes=[pltpu.SemaphoreType.DMA((n, N_DEV - 1)), pltpu.SemaphoreType.DMA((n, N_DEV - 1)),
                        pltpu.SemaphoreType.DMA((n,))],
    )(*xs)


_HBM = pl.BlockSpec(memory_space=pltpu.HBM)
_SEM = pl.BlockSpec(memory_space=pltpu.SEMAPHORE)
_EFFECT = pltpu.SideEffectType.DATAFLOW_SIDE_EFFECTING


def _plan_direct(gather):
    def plan(bufs, ssem, rsem, me, outgoing):
        n = len(bufs) // 2
        out = []
        for a in range(n):
            for k in range(1, N_DEV):
                peer = _flip(me, k)
                out.append(pltpu.make_async_remote_copy(
                    src_ref=bufs[a] if gather else bufs[a].at[_rank(peer)],
                    dst_ref=bufs[n + a].at[_rank(me) if outgoing else _rank(peer)],
                    send_sem=ssem.at[a * 7 + k - 1], recv_sem=rsem.at[a * 7 + k - 1], device_id=peer,
                    device_id_type=pl.DeviceIdType.MESH))
        return out
    return plan, 7


_SAME_CORE = (2, 4, 6)


def _plan_own_block(bufs, ssem, rsem, me, outgoing):
    n = len(bufs) // 2
    out = []
    for a in range(n):
        for j, k in enumerate((1,) + _SAME_CORE):
            peer = _flip(me, k)
            out.append(pltpu.make_async_remote_copy(
                src_ref=bufs[a], dst_ref=bufs[n + a].at[_rank(me) if outgoing else _rank(peer)],
                send_sem=ssem.at[a * 4 + j], recv_sem=rsem.at[a * 4 + j], device_id=peer, device_id_type=pl.DeviceIdType.MESH))
    return out


def _plan_pass_on(bufs, ssem, rsem, me, outgoing):
    sibling = _flip(me, 1)
    out = []
    for a in range(len(bufs)):
        for j, k in enumerate(_SAME_CORE):
            slot = _rank(_flip(me, k) if outgoing else _flip(sibling, k))
            out.append(pltpu.make_async_remote_copy(
                src_ref=bufs[a].at[slot], dst_ref=bufs[a].at[slot], send_sem=ssem.at[a * 3 + j], recv_sem=rsem.at[a * 3 + j],
                device_id=sibling, device_id_type=pl.DeviceIdType.MESH))
    return out


def _split_start(bufs, plan, n_sems, name, after=None):
    nb = len(bufs)
    extra = [] if after is None else [after]

    def body(*refs):
        ssem, rsem = refs[nb + len(extra)], refs[nb + len(extra) + 1]
        for cp in plan(refs[:nb], ssem, rsem, _my_pos(), True):
            cp.start()
        token = refs[2 * nb + len(extra) + 2]
        token[...] = jnp.zeros_like(token)

    res = pl.pallas_call(
        body, name=name,
        out_shape=(pltpu.SemaphoreType.DMA((n_sems,)), pltpu.SemaphoreType.DMA((n_sems,)),
                   *[pltpu.HBM(t.shape, t.dtype) for t in bufs], jax.ShapeDtypeStruct((8, 128), F32)),
        in_specs=[_HBM] * nb + [pl.BlockSpec(memory_space=pl.ANY)] * len(extra),
        out_specs=(_SEM, _SEM, *([_HBM] * nb), pl.BlockSpec(memory_space=pltpu.VMEM)),
        input_output_aliases={i: 2 + i for i in range(nb)},
        compiler_params=pltpu.CompilerParams(has_side_effects=_EFFECT),
    )(*[pltpu.with_memory_space_constraint(t, pltpu.HBM) for t in bufs], *extra)
    return (res[0], res[1], list(res[2:2 + nb])), res[2 + nb]


def _split_wait(handle, plan, after, name):
    ssem, rsem, bufs = handle
    nb = len(bufs)

    def body(*refs):
        for cp in plan(refs[:nb], refs[nb], refs[nb + 1], _my_pos(), False):
            cp.wait_send()
            cp.wait_recv()

    res = pl.pallas_call(
        body, name=name, out_shape=tuple(pltpu.HBM(t.shape, t.dtype) for t in bufs),
        in_specs=[_HBM] * nb + [_SEM, _SEM, pl.BlockSpec(memory_space=pl.ANY)], out_specs=tuple([_HBM] * nb),
        input_output_aliases={i: i for i in range(nb)},
        compiler_params=pltpu.CompilerParams(has_side_effects=_EFFECT),
    )(*bufs, ssem, rsem, after)
    return list(res)


def _landing_zones(xs, gather):
    mine = _rank(_my_pos())
    lands = []
    for x in xs:
        own = x if gather else lax.dynamic_index_in_dim(x, mine, 0, keepdims=False)
        lands.append(lax.dynamic_update_index_in_dim(lax.empty((N_DEV,) + tuple(own.shape), x.dtype), own, mine, 0))
    return lands


def _gather_start(xs, name, after=None, two_level=True):
    bufs = list(xs) + _landing_zones(xs, True)
    if two_level:
        handle, token = _split_start(bufs, _plan_own_block, 4 * len(xs), name + "_start", after)
    else:
        plan, per = _plan_direct(True)
        handle, token = _split_start(bufs, plan, per * len(xs), name + "_start", after)
    return (handle, two_level, name), token


def _gather_wait(state, after):
    handle, two_level, name = state
    n = len(handle[2]) // 2
    if not two_level:
        return _split_wait(handle, _plan_direct(True)[0], after, name + "_wait")[n:]
    lands = _split_wait(handle, _plan_own_block, after, name + "_wait")[n:]
    handle, _ = _split_start(lands, _plan_pass_on, 3 * n, name + "_pass")
    return _split_wait(handle, _plan_pass_on, after, name + "_passed")


def _scatter_start(xs, name):
    plan, per = _plan_direct(False)
    handle, token = _split_start(list(xs) + _landing_zones(xs, False), plan, per * len(xs), name + "_start")
    return (handle, name), token


def _scatter_wait(state, after):
    handle, name = state
    return _split_wait(handle, _plan_direct(False)[0], after, name + "_wait")[len(handle[2]) // 2:]


def _adamw_math(w, g, m, v):
    m = ADAM_B1 * m + (1.0 - ADAM_B1) * g
    v = ADAM_B2 * v + (1.0 - ADAM_B2) * (g * g)
    m_hat = m / (1.0 - ADAM_B1 ** ADAM_STEP)
    v_hat = v / (1.0 - ADAM_B2 ** ADAM_STEP)
    return -ADAM_LR * (m_hat / (jnp.sqrt(v_hat) + ADAM_EPS) + ADAM_WD * w), m, v


def _sum8(parts, *, name):
    _, R, C = parts.shape
    tr = _pick(R, 256, 16)

    def body(p_ref, o_ref):
        acc = p_ref[0].astype(F32)
        for r in range(1, N_DEV):
            acc = acc + p_ref[r].astype(F32)
        o_ref[...] = acc

    return pl.pallas_call(
        body, name=name, grid=(R // tr,), in_specs=[pl.BlockSpec((N_DEV, tr, C), lambda i: (0, i, 0))],
        out_specs=pl.BlockSpec((tr, C), lambda i: (i, 0)), out_shape=jax.ShapeDtypeStruct((R, C), F32),
        compiler_params=pltpu.CompilerParams(dimension_semantics=("parallel",)),
    )(parts)


def _adamw_stacked(gs, w, m, v, *, name):
    depth, R, C = w.shape
    tr = _pick(R, 64, 16)
    summed = gs[0].ndim == 3

    def body(*refs):
        g_refs, (w_ref, m_ref, v_ref), outs = refs[:depth], refs[depth:depth + 3], refs[depth + 3:]
        for l in range(depth):
            if summed:
                g = g_refs[l][0].astype(F32)
                for r in range(1, N_DEV):
                    g = g + g_refs[l][r].astype(F32)
            else:
                g = g_refs[l][...]
            d, nm, nv = _adamw_math(w_ref[l], g, m_ref[l], v_ref[l])
            for o, val in zip(outs, (g, d, nm, nv)):
                o[l] = val

    g_spec = pl.BlockSpec((N_DEV, tr, C), lambda i: (0, i, 0)) if summed else pl.BlockSpec((tr, C), lambda i: (i, 0))
    stk = pl.BlockSpec((depth, tr, C), lambda i: (0, i, 0))
    return pl.pallas_call(
        body, name=name, grid=(R // tr,), in_specs=[g_spec] * depth + [stk] * 3, out_specs=[stk] * 4,
        out_shape=[jax.ShapeDtypeStruct((depth, R, C), F32)] * 4,
        compiler_params=pltpu.CompilerParams(dimension_semantics=("parallel",)),
    )(*gs, w, m, v)


def _lane_view(shape):
    if len(shape) == 1:
        return (1, shape[0])
    size = math.prod(shape)
    if shape[-1] < 64 and size % LANES == 0:
        return (size // LANES, LANES)
    if len(shape) > 2 and shape[-2] % 8 == 0:
        return (size // shape[-1], shape[-1])
    return tuple(shape)


def _adamw_many(ws, gs, ms, vs, *, name):
    n = len(ws)
    views = [_lane_view(w.shape) for w in ws]

    def body(*refs):
        for i in range(n):
            d, nm, nv = _adamw_math(refs[i][...], refs[n + i][...], refs[2 * n + i][...], refs[3 * n + i][...])
            refs[4 * n + i][...] = d
            refs[5 * n + i][...] = nm
            refs[6 * n + i][...] = nv

    outs = pl.pallas_call(
        body, name=name, out_shape=[jax.ShapeDtypeStruct(s, F32) for s in views] * 3,
    )(*[a.reshape(s) for arrs in (ws, gs, ms, vs) for a, s in zip(arrs, views)])
    return [[o.reshape(w.shape) for o, w in zip(outs[q * n:(q + 1) * n], ws)] for q in range(3)]


def _pack(arrs):
    flat = jnp.concatenate([a.reshape(-1).astype(F32) for a in arrs])
    pad = (-flat.shape[0]) % 1024
    return jnp.pad(flat, (0, pad)).reshape(-1, 128)


def _unpack(buf, shapes):
    flat, out, off = buf.reshape(-1), [], 0
    for s in shapes:
        n = math.prod(s)
        out.append(flat[off:off + n].reshape(s))
        off += n
    return out


def _s5_params(lam_re, lam_im, log_step, b_re, b_im):
    G, P, H = b_re.shape
    step = jnp.exp(log_step)[:, None]
    mag = jnp.exp(lam_re * step)
    ar, ai = mag * jnp.cos(lam_im * step), mag * jnp.sin(lam_im * step)
    den = lam_re * lam_re + lam_im * lam_im
    qr = ((ar - 1.0) * lam_re + ai * lam_im) / den
    qi = (ai * lam_re - (ar - 1.0) * lam_im) / den
    bbr = qr[..., None] * b_re - qi[..., None] * b_im
    bbi = qr[..., None] * b_im + qi[..., None] * b_re
    return ar.reshape(-1), ai.reshape(-1), _s5_blocks_in(bbr), _s5_blocks_in(bbi)


def _s5_blocks_in(b):
    G, P, H = b.shape
    eye = jnp.eye(8, dtype=b.dtype)
    return jnp.einsum("kgph,gj->kghjp", b.reshape(G // 8, 8, P, H), eye).reshape(G // 8, 8 * H, 8 * P)


def _s5_blocks_in_diag(m, G, P, H):
    return jnp.einsum("kghgp->kgph", m.reshape(G // 8, 8, H, 8, P)).reshape(G, P, H)


def _s5_blocks_out(c):
    G, H, P = c.shape
    eye = jnp.eye(8, dtype=c.dtype)
    return jnp.einsum("kghp,gj->kgpjh", c.reshape(G // 8, 8, H, P), eye).reshape(G // 8, 8 * P, 8 * H)


def _s5_blocks_out_diag(m, G, H, P):
    return jnp.einsum("kgpgh->kghp", m.reshape(G // 8, 8, P, 8, H)).reshape(G, H, P)


def _head_blocks(w):
    nh, d, e = w.shape
    return jnp.einsum("hde,hk->hdke", w, jnp.eye(nh, dtype=w.dtype)).reshape(nh * d, nh * e)


def _head_blocks_diag(m, nh, d, e):
    return jnp.einsum("hdhe->hde", m.reshape(nh, d, nh, e))


def _pad_rows(w, rows):
    return jnp.pad(w, ((0, rows - w.shape[0]), (0, 0)))


def _unshard_cols(g):
    return jnp.transpose(g, (1, 0, 2)).reshape(g.shape[1], -1)


_NAMES = ['norm_mix_g', 'w_in', 's5_lam_re', 's5_lam_im', 's5_log_step', 's5_b_re', 's5_b_im', 's5_c_re', 's5_c_im', 's5_d',
          's5_w_glu', 's5_b_glu', 'cv_w_dw', 'cv_b_dw', 'cv_ln_g', 'cv_ln_b', 'cv_w_pw', 'cv_b_pw', 'lru_w_conv', 'lru_b_conv',
          'lru_w_r', 'lru_b_r', 'lru_w_i', 'lru_b_i', 'lru_lam', 'pool_w', 'pool_scale', 'w_out', 'norm_ffn_g', 'ffn_w_up',
          'ffn_w_dw', 'ffn_b_dw', 'ffn_w_down', 'norm_final_g']
_MATRIX = ('w_in', 'w_out', 'ffn_w_up', 'ffn_w_down', 's5_w_glu', 'cv_w_pw')
_TRANSPOSED = ('w_in', 'ffn_w_up')
_COLSHARD = ('cv_w_dw', 'lru_w_conv', 'ffn_w_dw')
_GROUPS = (('w_in', 's5_w_glu', 'cv_w_pw', 'w_out'), ('ffn_w_up', 'ffn_w_down'))
_SCATTER_GROUPS = (('w_out', 's5_w_glu', 'cv_w_pw'), ('w_in',), ('ffn_w_up', 'ffn_w_down'))


def kernel(x, norm_mix_g, w_in, s5_lam_re, s5_lam_im, s5_log_step, s5_b_re, s5_b_im, s5_c_re, s5_c_im, s5_d, s5_w_glu, s5_b_glu, cv_w_dw, cv_b_dw, cv_ln_g, cv_ln_b, cv_w_pw, cv_b_pw, lru_w_conv, lru_b_conv, lru_w_r, lru_b_r, lru_w_i, lru_b_i, lru_lam, pool_w, pool_scale, w_out, norm_ffn_g, ffn_w_up, ffn_w_dw, ffn_b_dw, ffn_w_down, norm_final_g, loss_target, m_norm_mix_g, m_w_in, m_s5_lam_re, m_s5_lam_im, m_s5_log_step, m_s5_b_re, m_s5_b_im, m_s5_c_re, m_s5_c_im, m_s5_d, m_s5_w_glu, m_s5_b_glu, m_cv_w_dw, m_cv_b_dw, m_cv_ln_g, m_cv_ln_b, m_cv_w_pw, m_cv_b_pw, m_lru_w_conv, m_lru_b_conv, m_lru_w_r, m_lru_b_r, m_lru_w_i, m_lru_b_i, m_lru_lam, m_pool_w, m_pool_scale, m_w_out, m_norm_ffn_g, m_ffn_w_up, m_ffn_w_dw, m_ffn_b_dw, m_ffn_w_down, m_norm_final_g, v_norm_mix_g, v_w_in, v_s5_lam_re, v_s5_lam_im, v_s5_log_step, v_s5_b_re, v_s5_b_im, v_s5_c_re, v_s5_c_im, v_s5_d, v_s5_w_glu, v_s5_b_glu, v_cv_w_dw, v_cv_b_dw, v_cv_ln_g, v_cv_ln_b, v_cv_w_pw, v_cv_b_pw, v_lru_w_conv, v_lru_b_conv, v_lru_w_r, v_lru_b_r, v_lru_w_i, v_lru_b_i, v_lru_lam, v_pool_w, v_pool_scale, v_w_out, v_norm_ffn_g, v_ffn_w_up, v_ffn_w_dw, v_ffn_b_dw, v_ffn_w_down, v_norm_final_g):
    args = (x, norm_mix_g, w_in, s5_lam_re, s5_lam_im, s5_log_step, s5_b_re, s5_b_im, s5_c_re, s5_c_im, s5_d, s5_w_glu, s5_b_glu, cv_w_dw, cv_b_dw, cv_ln_g, cv_ln_b, cv_w_pw, cv_b_pw, lru_w_conv, lru_b_conv, lru_w_r, lru_b_r, lru_w_i, lru_b_i, lru_lam, pool_w, pool_scale, w_out, norm_ffn_g, ffn_w_up, ffn_w_dw, ffn_b_dw, ffn_w_down, norm_final_g, loss_target, m_norm_mix_g, m_w_in, m_s5_lam_re, m_s5_lam_im, m_s5_log_step, m_s5_b_re, m_s5_b_im, m_s5_c_re, m_s5_c_im, m_s5_d, m_s5_w_glu, m_s5_b_glu, m_cv_w_dw, m_cv_b_dw, m_cv_ln_g, m_cv_ln_b, m_cv_w_pw, m_cv_b_pw, m_lru_w_conv, m_lru_b_conv, m_lru_w_r, m_lru_b_r, m_lru_w_i, m_lru_b_i, m_lru_lam, m_pool_w, m_pool_scale, m_w_out, m_norm_ffn_g, m_ffn_w_up, m_ffn_w_dw, m_ffn_b_dw, m_ffn_w_down, m_norm_final_g, v_norm_mix_g, v_w_in, v_s5_lam_re, v_s5_lam_im, v_s5_log_step, v_s5_b_re, v_s5_b_im, v_s5_c_re, v_s5_c_im, v_s5_d, v_s5_w_glu, v_s5_b_glu, v_cv_w_dw, v_cv_b_dw, v_cv_ln_g, v_cv_ln_b, v_cv_w_pw, v_cv_b_pw, v_lru_w_conv, v_lru_b_conv, v_lru_w_r, v_lru_b_r, v_lru_w_i, v_lru_b_i, v_lru_lam, v_pool_w, v_pool_scale, v_w_out, v_norm_ffn_g, v_ffn_w_up, v_ffn_w_dw, v_ffn_b_dw, v_ffn_w_down, v_norm_final_g)
    x, target = args[0], args[35]
    W = dict(zip(_NAMES, args[1:35]))
    Mo = dict(zip(_NAMES, args[36:70]))
    Vo = dict(zip(_NAMES, args[70:104]))
    x = x[0]
    target = target[0]
    T, D = x.shape
    depth = W['w_in'].shape[0]
    F = W['ffn_w_down'].shape[1] * N_DEV
    G, P, H = W['s5_b_re'].shape[1:]
    MW = G * H
    nh, hd, _ = W['lru_w_r'].shape[1:]
    me = _rank(_my_pos())

    gather, full = {}, {}

    def start_group(l, gi, after):
        shards = [(W[n][l].T if n in _TRANSPOSED else W[n][l]).astype(BF16) for n in _GROUPS[gi]]
        if (l, gi) == (0, 0):
            shards.append(_pack([W[n][k] for k in range(depth) for n in _COLSHARD]))
        gather[l, gi], tok = _gather_start(shards, "gather_%d%d" % (l, gi), after)
        return tok

    def land_weights(l, gi, after):
        got = _gather_wait(gather[l, gi], after)
        for n, g8 in zip(_GROUPS[gi], got):
            full[n, l] = g8.reshape(-1, g8.shape[-1])
        return got

    started = start_group(0, 0, None)[0, 0]
    got = land_weights(0, 0, W['norm_mix_g'][0] + started)
    started = start_group(0, 1, got[0])[0, 0]
    taps8 = got[-1]
    tap_shapes = [W[n][l].shape for l in range(depth) for n in _COLSHARD]
    taps = [_unshard_cols(jnp.stack(t)) for t in zip(*[_unpack(taps8[r], tap_shapes) for r in range(N_DEV)])]
    for l in range(depth):
        for j, n in enumerate(_COLSHARD):
            full[n, l] = taps[l * len(_COLSHARD) + j]

    saved = []
    for l in range(depth):
        g_mix = W['norm_mix_g'][l] + (0.0 if l else started)
        if l:
            land_weights(l, 0, x)
        s5p, s5p_vjp = jax.vjp(_s5_params, W['s5_lam_re'][l], W['s5_lam_im'][l], W['s5_log_step'][l], W['s5_b_re'][l], W['s5_b_im'][l])
        ar, ai, bre, bim = s5p
        cre, cim = _s5_blocks_out(W['s5_c_re'][l]), _s5_blocks_out(W['s5_c_im'][l])
        sp8, sp8_vjp = jax.vjp(lambda lam: LRU_C * jax.nn.softplus(-lam), W['lru_lam'][l])
        wr, wi = _head_blocks(W['lru_w_r'][l]), _head_blocks(W['lru_w_i'][l])
        c = dict(bre=bre.astype(BF16), bim=bim.astype(BF16), cre=cre.astype(BF16), cim=cim.astype(BF16),
                 cst_f=_s5_scan_consts(ar, ai, False), cst_b=_s5_scan_consts(ar, ai, True), sp8=sp8, wr=wr.astype(BF16),
                 wi=wi.astype(BF16), cvw=_pad_rows(full['cv_w_dw', l], 32), lruw=_pad_rows(full['lru_w_conv', l], 8),
                 ffw=_pad_rows(full['ffn_w_dw', l], 8), pw=W['pool_w'][l].astype(BF16), s5p_vjp=s5p_vjp, sp8_vjp=sp8_vjp, x0=x)
        proj, h = _norm_matmul(x, g_mix, full['w_in', l], tm=1024, tn=512, name="in_proj")
        y0, sre, sim, yp, z = _s5_fwd(proj, 0, c['bre'], c['bim'], c['cre'], c['cim'], c['cst_f'], W['s5_d'][l],
                                     full['s5_w_glu', l], W['s5_b_glu'][l])
        y1, h1 = _cv_fwd(proj, 1, 2, c['cvw'], W['cv_b_dw'][l], W['cv_ln_g'][l], W['cv_ln_b'][l], full['cv_w_pw', l], W['cv_b_pw'][l])
        y2, xc, rr, ig, hprev = _lru_fwd(proj, 3, 4, c['lruw'], W['lru_b_conv'][l], c['wr'], W['lru_b_r'][l], c['wi'],
                                         W['lru_b_i'][l], sp8)
        y3 = _pool_fwd(proj, 5, c['pw'], W['pool_scale'][l])
        mixed = jnp.concatenate([y0, y1, y2, y3], axis=1)
        x1 = _matmul(mixed, full['w_out', l], mode="nn", tm=1024, tn=512, tk=2048, out_dtype=F32, name="out_proj", add=x)
        got = land_weights(l, 1, x1)
        if l + 1 < depth:
            start_group(l + 1, 0, got[0])
            start_group(l + 1, 1, got[0])
        up, h2 = _norm_matmul(x1, W['norm_ffn_g'][l], full['ffn_w_up', l], tm=512, tn=F, tk=512, name="ffn_up", out_dtype=BF16)
        act = _ffn_mid_fwd(up, c['ffw'], W['ffn_b_dw'][l])
        x = _matmul(act, full['ffn_w_down', l], mode="nn", tm=1024, tn=512, tk=F, out_dtype=F32, name="ffn_down", add=x1)
        c.update(proj=proj, h=h, sre=sre, sim=sim, yp=yp, z=z, h1=h1, xc=xc, rr=rr, ig=ig, hprev=hprev, mixed=mixed, x1=x1,
                 up=up, h2=h2, act=act)
        saved.append(c)

    dx, dxb, dgf, se = _loss_head(x, W['norm_final_g'], target)
    loss = lax.psum(0.5 / D * jnp.sum(se), AXES)
    gsmall = {('norm_final_g', None): dgf.sum(0)}
    gmat, scatter = {}, {}

    def parts_of(m):
        return m.reshape(N_DEV, m.shape[0] // N_DEV, m.shape[1])

    for l in reversed(range(depth)):
        c = saved[l]
        dact = _matmul(dxb, full['ffn_w_down', l], mode="nt", tm=512, tn=F, tk=512, out_dtype=BF16, name="d_act")
        gmat['ffn_w_down', l] = _matmul(c['act'], dxb, mode="tn", tm=F, tn=512, tk=256, out_dtype=BF16, name="dw_down")
        dup, dffw, dffb = _ffn_mid_bwd(dact, c['up'], c['ffw'], W['ffn_b_dw'][l])
        gsmall['ffn_w_dw', l] = dffw.sum(1)[:3]
        gsmall['ffn_b_dw', l] = dffb.sum(0)
        dh2 = _matmul(dup, full['ffn_w_up', l], mode="nn", tm=1024, tn=512, tk=F, out_dtype=F32, name="d_h2")
        gmat['ffn_w_up', l] = _matmul(dup, c['h2'], mode="tn", tm=256, tn=D, tk=T, out_dtype=BF16, name="dw_up")
        scatter[l, 2], tok = _scatter_start([parts_of(gmat[n, l]) for n in _SCATTER_GROUPS[2]], "scatter_%d2" % l)
        dx1, dx1b, dg2 = _rms_bwd(dx, dh2, c['x1'], W['norm_ffn_g'][l] + tok[0, 0], name="ffn_norm_bwd")
        gsmall['norm_ffn_g', l] = dg2.sum(0)
        dmix = _matmul(dx1b, full['w_out', l], mode="nt", tm=1024, tn=512, tk=2048, out_dtype=F32, name="d_mixed")
        gmat['w_out', l] = _matmul(c['mixed'], dx1b, mode="tn", tm=512, tn=D, tk=2048, out_dtype=BF16, name="dw_out")
        du, dbre, dbim, dcre, dcim, da, dwglu, v3 = _s5_bwd(dmix, 0, c['proj'], 0, c['sre'], c['sim'], c['yp'], c['z'], c['bre'],
                                                            c['bim'], c['cre'], c['cim'], c['cst_b'], W['s5_d'][l], full['s5_w_glu', l])
        da = da.sum(1)
        glr, gli, gls, gbr, gbi = c['s5p_vjp']((da[0], da[1], dbre, dbim))
        gsmall['s5_lam_re', l], gsmall['s5_lam_im', l], gsmall['s5_log_step', l] = glr, gli, gls
        gsmall['s5_b_re', l], gsmall['s5_b_im', l] = gbr, gbi
        gsmall['s5_c_re', l] = _s5_blocks_out_diag(dcre, G, H, P)
        gsmall['s5_c_im', l] = _s5_blocks_out_diag(dcim, G, H, P)
        gsmall['s5_b_glu', l], gsmall['s5_d', l] = v3[0].sum(0), v3[1].sum(0)
        gmat['s5_w_glu', l] = dwglu.astype(BF16)
        dv, dg, dwpw, dcvw, v5 = _cv_bwd(dmix, 1, c['proj'], 1, 2, c['h1'], c['cvw'], W['cv_ln_g'][l], W['cv_ln_b'][l], full['cv_w_pw', l])
        gmat['cv_w_pw', l] = dwpw.astype(BF16)
        gsmall['cv_w_dw', l] = dcvw.sum(1)[:31]
        gsmall['cv_b_pw', l], gsmall['cv_ln_g', l], gsmall['cv_ln_b', l], gsmall['cv_b_dw', l] = [v5[j].sum(0) for j in range(4)]
        scatter[l, 0], tok = _scatter_start([parts_of(gmat[n, l]) for n in _SCATTER_GROUPS[0]], "scatter_%d0" % l)
        tok = tok[0, 0]
        dlx, dlg, dwr, dwi, dlw, v4 = _lru_bwd(dmix, 2, c['proj'], 3, 4, c['xc'], c['rr'], c['ig'], c['hprev'], c['lruw'], c['wr'],
                                               c['wi'], c['sp8'] + tok)
        gsmall['lru_w_r', l], gsmall['lru_w_i', l] = _head_blocks_diag(dwr, nh, hd, hd), _head_blocks_diag(dwi, nh, hd, hd)
        gsmall['lru_w_conv', l] = dlw.sum(1)[:4]
        gsmall['lru_b_r', l], gsmall['lru_b_i', l] = v4[0].sum(0), v4[1].sum(0)
        gsmall['lru_lam', l] = c['sp8_vjp'](v4[2].sum(0))[0]
        gsmall['lru_b_conv', l] = v4[3].sum(0)
        dpx, dpw, dps = _pool_bwd(dmix, 3, c['proj'], 5, c['pw'], W['pool_scale'][l] + tok)
        gsmall['pool_w', l], gsmall['pool_scale', l] = dpw, dps.sum(0)
        dproj = jnp.concatenate([du, dv, dg, dlx, dlg, dpx], axis=1)
        dh = _matmul(dproj, full['w_in', l], mode="nn", tm=1024, tn=512, tk=1024, out_dtype=F32, name="d_h")
        gmat['w_in', l] = _matmul(dproj, c['h'], mode="tn", tm=512, tn=D, tk=2048, out_dtype=BF16, name="dw_in")
        scatter[l, 1], tok = _scatter_start([parts_of(gmat[n, l]) for n in _SCATTER_GROUPS[1]], "scatter_%d1" % l)
        dx, dxb, dg1 = _rms_bwd(dx1, dh, c['x0'], W['norm_mix_g'][l] + tok[0, 0], name="mix_norm_bwd")
        gsmall['norm_mix_g', l] = dg1.sum(0)

    small_names = [n for n in _NAMES if n not in _MATRIX]
    small_parts = [gsmall[n, l] for n in small_names for l in ((None,) if n == 'norm_final_g' else range(depth))]
    full_shapes = [gsmall[n, None].shape if n == 'norm_final_g' else (depth,) + gsmall[n, 0].shape for n in small_names]
    small_state, tok = _gather_start([_pack(small_parts)], "small_grads", two_level=False)
    parts = {}
    for l in reversed(range(depth)):
        for gi in (2, 0, 1):
            got = _scatter_wait(scatter[l, gi], tok)
            for n, p8 in zip(_SCATTER_GROUPS[gi], got):
                parts[n, l] = p8

    out = {}
    for n in _MATRIX:
        if n in _TRANSPOSED and W[n].shape[-1] % LANES:
            wt, mt, vt = (jnp.swapaxes(t, 1, 2) for t in (W[n], Mo[n], Vo[n]))
            res = _adamw_stacked([parts[n, l] for l in range(depth)], wt, mt, vt, name="adamw_" + n)
            out[n] = tuple(jnp.swapaxes(t, 1, 2) for t in res)
            continue
        if n in _TRANSPOSED:
            gl = [_sum8(parts[n, l], name="sum_" + n).T for l in range(depth)]
        else:
            gl = [parts[n, l] for l in range(depth)]
        out[n] = _adamw_stacked(gl, W[n], Mo[n], Vo[n], name="adamw_" + n)
    after_matrices = sum(out[n][1][0, 0, 0] for n in _MATRIX).reshape(1, 1)
    small8 = _gather_wait(small_state, after_matrices)[0]
    small_sum = _sum8(small8, name="sum_small_grads")
    gs = dict(zip(small_names, _unpack(small_sum, full_shapes)))
    for n in _COLSHARD:
        cw = W[n].shape[-1]
        gs[n] = lax.dynamic_slice_in_dim(gs[n], me * cw, cw, axis=gs[n].ndim - 1)
    upd = _adamw_many([W[n] for n in small_names], [gs[n] for n in small_names], [Mo[n] for n in small_names],
                      [Vo[n] for n in small_names], name="adamw_small")
    for j, n in enumerate(small_names):
        out[n] = (gs[n], upd[0][j], upd[1][j], upd[2][j])
    return (loss, dx[None]) + tuple(out[n][q] for q in range(4) for n in _NAMES)
```

```python
import functools
import math

import jax
import jax.numpy as jnp
from jax import lax
from jax.experimental import pallas as pl
from jax.experimental.pallas import tpu as pltpu

F32 = jnp.float32
BF16 = jnp.bfloat16
N_DEV = 8
AXES = ("x", "y", "c")
EPS = 1e-6
S5_GROUP_CH = 16
S5_STATE = 64
LRU_HEADS = 8
LRU_C = 8.0
POOL_WINDOWS = (2, 4, 8, 16)
CONV_HALO = 32
SMALL_HALO = 8
FFN_HALO = 16
POOL_HALO = 16
LANES = 128
ADAM_LR, ADAM_B1, ADAM_B2, ADAM_EPS, ADAM_WD, ADAM_STEP = 0.001, 0.9, 0.999, 1e-08, 0.01, 10
_GELU_K = math.sqrt(2.0 / math.pi)
ARB = pltpu.CompilerParams(dimension_semantics=("arbitrary",))


def _pick(n, pref, mult):
    best = None
    for t in range(mult, min(n, pref) + 1, mult):
        if n % t == 0:
            best = t
    return best if best is not None else n


def _sigmoid(x):
    return 1.0 / (1.0 + jnp.exp(-x))


def _gelu(x):
    return 0.5 * x * (1.0 + jnp.tanh(_GELU_K * (x + 0.044715 * x * x * x)))


def _gelu_grad(x):
    t = jnp.tanh(_GELU_K * (x + 0.044715 * x * x * x))
    return 0.5 * (1.0 + t) + 0.5 * x * (1.0 - t * t) * _GELU_K * (1.0 + 3.0 * 0.044715 * x * x)


def _gelu_and_grad(x):
    t = jnp.tanh(_GELU_K * (x + 0.044715 * x * x * x))
    half = 0.5 * (1.0 + t)
    return x * half, half + 0.5 * x * (1.0 - t * t) * _GELU_K * (1.0 + 3.0 * 0.044715 * x * x)


def _neg_expm1(x):
    series = -x * (1.0 + x * (0.5 + x * (1.0 / 6.0 + x * (1.0 / 24.0 + x * (1.0 / 120.0 + x * (1.0 / 720.0))))))
    return jnp.where(jnp.abs(x) < 0.25, series, 1.0 - jnp.exp(x))


def _fold8(x):
    return x.reshape(x.shape[0] // 8, 8, x.shape[1]).sum(axis=0)


def _dot(a, b):
    return jnp.dot(a.astype(BF16), b.astype(BF16), preferred_element_type=F32)


def _dot_nt(a, b):
    return lax.dot_general(a.astype(BF16), b.astype(BF16), (((1,), (1,)), ((), ())), preferred_element_type=F32)


def _dot_tn(a, b):
    return lax.dot_general(a.astype(BF16), b.astype(BF16), (((0,), (0,)), ((), ())), preferred_element_type=F32)


def _full(shape):
    nd = len(shape)
    return pl.BlockSpec(shape, lambda *i: (0,) * nd)


def _matmul(a, b, *, mode, tm, tn, tk, out_dtype, name, add=None):
    if mode == "nn":
        (M, K), N = a.shape, b.shape[1]
    elif mode == "nt":
        (M, K), N = a.shape, b.shape[0]
    else:
        (K, M), N = a.shape, b.shape[1]
    tm, tn, tk = _pick(M, tm, 8), _pick(N, tn, 128), _pick(K, tk, 128)
    if mode == "tn":
        tm = _pick(M, tm, 128)
    nk = K // tk
    a_spec = {"nn": pl.BlockSpec((tm, tk), lambda i, j, k: (i, k)), "nt": pl.BlockSpec((tm, tk), lambda i, j, k: (i, k)),
              "tn": pl.BlockSpec((tk, tm), lambda i, j, k: (k, i))}[mode]
    b_spec = {"nn": pl.BlockSpec((tk, tn), lambda i, j, k: (k, j)), "nt": pl.BlockSpec((tn, tk), lambda i, j, k: (j, k)),
              "tn": pl.BlockSpec((tk, tn), lambda i, j, k: (k, j))}[mode]
    o_spec = pl.BlockSpec((tm, tn), lambda i, j, k: (i, j))
    dot = {"nn": _dot, "nt": _dot_nt, "tn": _dot_tn}[mode]
    has_add = add is not None

    def body(*refs):
        if has_add:
            a_ref, b_ref, add_ref, o_ref, acc_ref = refs
        else:
            a_ref, b_ref, o_ref, acc_ref = refs
        k = pl.program_id(2)

        @pl.when(k == 0)
        def _():
            acc_ref[...] = jnp.zeros_like(acc_ref)

        acc_ref[...] += dot(a_ref[...], b_ref[...])

        @pl.when(k == nk - 1)
        def _():
            r = acc_ref[...]
            if has_add:
                r = r + add_ref[...]
            o_ref[...] = r.astype(out_dtype)

    ins = [a, b] + ([add] if has_add else [])
    specs = [a_spec, b_spec] + ([o_spec] if has_add else [])
    return pl.pallas_call(
        body, name=name, grid=(M // tm, N // tn, nk), in_specs=specs, out_specs=o_spec,
        out_shape=jax.ShapeDtypeStruct((M, N), out_dtype), scratch_shapes=[pltpu.VMEM((tm, tn), F32)],
        compiler_params=pltpu.CompilerParams(dimension_semantics=("parallel", "parallel", "arbitrary")),
    )(*ins)


def _norm_matmul(x, g, wt, *, tm, tn, name, tk=None, out_dtype=F32):
    T, D = x.shape
    N = wt.shape[0]
    tm, tn, tk = _pick(T, tm, 8), _pick(N, tn, 128), _pick(D, tk or D, 128)
    nk = D // tk

    def body(x_ref, g_ref, w_ref, o_ref, h_ref, acc_ref):
        j, k = pl.program_id(1), pl.program_id(2)

        @pl.when((j == 0) & (k == 0))
        def _():
            xv = x_ref[...]
            r = lax.rsqrt(jnp.mean(xv * xv, axis=-1, keepdims=True) + EPS)
            h_ref[...] = (xv * r * g_ref[...]).astype(BF16)

        part = _dot_nt(h_ref[:, pl.ds(pl.multiple_of(k * tk, LANES), tk)], w_ref[...])
        if nk == 1:
            o_ref[...] = part.astype(out_dtype)
        else:
            @pl.when(k == 0)
            def _():
                acc_ref[...] = part

            @pl.when(k > 0)
            def _():
                acc_ref[...] += part

            @pl.when(k == nk - 1)
            def _():
                o_ref[...] = acc_ref[...].astype(out_dtype)

    return pl.pallas_call(
        body, name=name, grid=(T // tm, N // tn, nk),
        in_specs=[pl.BlockSpec((tm, D), lambda i, j, k: (i, 0)), pl.BlockSpec((1, D), lambda i, j, k: (0, 0)),
                  pl.BlockSpec((tn, tk), lambda i, j, k: (j, k))],
        out_specs=[pl.BlockSpec((tm, tn), lambda i, j, k: (i, j)), pl.BlockSpec((tm, D), lambda i, j, k: (i, 0))],
        out_shape=[jax.ShapeDtypeStruct((T, N), out_dtype), jax.ShapeDtypeStruct((T, D), BF16)],
        scratch_shapes=[pltpu.VMEM((tm, tn) if nk > 1 else (8, LANES), F32)],
        compiler_params=pltpu.CompilerParams(dimension_semantics=("parallel", "arbitrary", "arbitrary")),
    )(x, g.reshape(1, D), wt)


def _rms_bwd(dres, dh, x, g, *, name):
    T, D = x.shape
    tc = _pick(T, 256, 8)

    def body(dres_ref, dh_ref, x_ref, g_ref, dx_ref, dxb_ref, dg_ref):
        @pl.when(pl.program_id(0) == 0)
        def _():
            dg_ref[...] = jnp.zeros_like(dg_ref)

        xv, dhv = x_ref[...], dh_ref[...]
        r = lax.rsqrt(jnp.mean(xv * xv, axis=-1, keepdims=True) + EPS)
        dg_ref[...] += _fold8(dhv * xv * r)
        dyg = dhv * g_ref[...]
        dx = dres_ref[...] + r * dyg - xv * (r * r * r) * jnp.mean(dyg * xv, axis=-1, keepdims=True)
        dx_ref[...] = dx
        dxb_ref[...] = dx.astype(BF16)

    row = pl.BlockSpec((tc, D), lambda i: (i, 0))
    return pl.pallas_call(
        body, name=name, grid=(T // tc,), in_specs=[row, row, row, _full((1, D))],
        out_specs=[row, row, _full((8, D))],
        out_shape=[jax.ShapeDtypeStruct((T, D), F32), jax.ShapeDtypeStruct((T, D), BF16), jax.ShapeDtypeStruct((8, D), F32)],
        compiler_params=ARB,
    )(dres, dh, x, g.reshape(1, D))


def _loss_head(x, g, target):
    T, D = x.shape
    tc = _pick(T, 256, 8)

    def body(x_ref, g_ref, t_ref, dx_ref, dxb_ref, dg_ref, se_ref):
        @pl.when(pl.program_id(0) == 0)
        def _():
            dg_ref[...] = jnp.zeros_like(dg_ref)
            se_ref[...] = jnp.zeros_like(se_ref)

        xv = x_ref[...]
        r = lax.rsqrt(jnp.mean(xv * xv, axis=-1, keepdims=True) + EPS)
        gv = g_ref[...]
        err = xv * r * gv - t_ref[...]
        se_ref[...] += _fold8(err * err)
        dy = err * (1.0 / D)
        dg_ref[...] += _fold8(dy * xv * r)
        dyg = dy * gv
        dx = r * dyg - xv * (r * r * r) * jnp.mean(dyg * xv, axis=-1, keepdims=True)
        dx_ref[...] = dx
        dxb_ref[...] = dx.astype(BF16)

    row = pl.BlockSpec((tc, D), lambda i: (i, 0))
    return pl.pallas_call(
        body, name="loss_head", grid=(T // tc,), in_specs=[row, _full((1, D)), row],
        out_specs=[row, row, _full((8, D)), _full((8, D))],
        out_shape=[jax.ShapeDtypeStruct((T, D), F32), jax.ShapeDtypeStruct((T, D), BF16),
                   jax.ShapeDtypeStruct((8, D), F32), jax.ShapeDtypeStruct((8, D), F32)],
        compiler_params=ARB,
    )(x, g.reshape(1, D), target)


def _s5_scan_consts(ar, ai, reverse):
    if reverse:
        ai = -ai
    pr, pi_ = [ar], [ai]
    for _ in range(7):
        pr, pi_ = pr + [pr[-1] * ar - pi_[-1] * ai], pi_ + [pr[-1] * ai + pi_[-1] * ar]
    rows = jnp.arange(8)[:, None]
    planes = []
    for d in (1, 2, 4):
        mask = (rows <= 7 - d) if reverse else (rows >= d)
        planes += [jnp.where(mask, pr[d - 1][None, :], 0.0), jnp.where(mask, pi_[d - 1][None, :], 0.0)]
    order = list(range(7, -1, -1)) if reverse else list(range(8))
    planes += [jnp.stack([pr[k] for k in order]), jnp.stack([pi_[k] for k in order])]
    return jnp.stack(planes).astype(F32)


def _s5_block_scan(xr_ref, xi_ref, cst_ref, car_ref, nblk, reverse, lb, extra=None):
    ns = xr_ref.shape[1]
    crow = 0 if reverse else 7

    def blk(i, _):
        ib = (nblk - 1 - i) if reverse else i
        r0 = pl.multiple_of(ib * 8, 8)
        for k in range(ns // lb):
            sl = slice(k * lb, (k + 1) * lb)
            xr, xi = xr_ref[pl.ds(r0, 8), sl], xi_ref[pl.ds(r0, 8), sl]
            for n, d in enumerate((1, 2, 4)):
                ar, ai = cst_ref[2 * n, :, sl], cst_ref[2 * n + 1, :, sl]
                sh = (8 - d) if reverse else d
                rr, ri = pltpu.roll(xr, sh, 0), pltpu.roll(xi, sh, 0)
                xr, xi = xr + ar * rr - ai * ri, xi + ar * ri + ai * rr
            pr, pi_ = cst_ref[6, :, sl], cst_ref[7, :, sl]
            cr, ci = car_ref[0, crow:crow + 1, sl], car_ref[1, crow:crow + 1, sl]
            xr, xi = xr + pr * cr - pi_ * ci, xi + pr * ci + pi_ * cr
            xr_ref[pl.ds(r0, 8), sl] = xr
            xi_ref[pl.ds(r0, 8), sl] = xi
            if extra is not None:
                extra(r0, sl, xr, xi, cr, ci)
            car_ref[0, :, sl] = xr
            car_ref[1, :, sl] = xi
        return 0

    lax.fori_loop(0, nblk, blk, 0)


def _s5_fwd(proj, col, bre, bim, cre, cim, cst, dvec, wglu, bglu):
    T = proj.shape[0]
    nsb, cw, sw = bre.shape
    M, NS = nsb * cw, nsb * sw
    tc = _pick(T, 256, 8)

    def body(u_ref, bre_ref, bim_ref, cre_ref, cim_ref, cst_ref, d_ref, wg_ref, bg_ref,
             y_ref, sre_ref, sim_ref, yp_ref, z_ref, car_ref):
        @pl.when(pl.program_id(0) == 0)
        def _():
            car_ref[...] = jnp.zeros_like(car_ref)

        u = u_ref[...]
        ub = u.astype(BF16)
        for k in range(nsb):
            sre_ref[:, k * sw:(k + 1) * sw] = _dot(ub[:, k * cw:(k + 1) * cw], bre_ref[k])
            sim_ref[:, k * sw:(k + 1) * sw] = _dot(ub[:, k * cw:(k + 1) * cw], bim_ref[k])
        _s5_block_scan(sre_ref, sim_ref, cst_ref, car_ref, tc // 8, False, sw)
        for k in range(nsb):
            yk = _dot(sre_ref[:, k * sw:(k + 1) * sw], cre_ref[k]) - _dot(sim_ref[:, k * sw:(k + 1) * sw], cim_ref[k])
            yp_ref[:, k * cw:(k + 1) * cw] = yk + d_ref[:, k * cw:(k + 1) * cw] * u[:, k * cw:(k + 1) * cw]
        gl = _gelu(yp_ref[...])
        z = _dot(gl, wg_ref[...]) + bg_ref[...]
        z_ref[...] = z
        y_ref[...] = (gl * _sigmoid(z)).astype(BF16)

    rowm = pl.BlockSpec((tc, M), lambda i: (i, 0))
    rows = pl.BlockSpec((tc, NS), lambda i: (i, 0))
    return pl.pallas_call(
        body, name="s5_fwd", grid=(T // tc,),
        in_specs=[pl.BlockSpec((tc, M), lambda i: (i, col)), _full(bre.shape), _full(bim.shape), _full(cre.shape),
                  _full(cim.shape), _full(cst.shape), _full((1, M)), _full((M, M)), _full((1, M))],
        out_specs=[rowm, rows, rows, rowm, rowm],
        out_shape=[jax.ShapeDtypeStruct((T, M), BF16), jax.ShapeDtypeStruct((T, NS), F32), jax.ShapeDtypeStruct((T, NS), F32),
                   jax.ShapeDtypeStruct((T, M), F32), jax.ShapeDtypeStruct((T, M), F32)],
        scratch_shapes=[pltpu.VMEM((2, 8, NS), F32)], compiler_params=ARB,
    )(proj, bre, bim, cre, cim, cst, dvec.reshape(1, M), wglu, bglu.reshape(1, M))


def _s5_bwd(dmix, dcol, proj, col, sre, sim, yp, z, bre, bim, cre, cim, cst, dvec, wglu):
    T = proj.shape[0]
    nsb, cw, sw = bre.shape
    M, NS = nsb * cw, nsb * sw
    tc = _pick(T, 256, 8)
    nc = T // tc

    def body(do_ref, u_ref, sre_ref, sim_ref, yp_ref, z_ref, bre_ref, bim_ref, cre_ref, cim_ref, cst_ref, d_ref, wg_ref,
             du_ref, dbre_ref, dbim_ref, dcre_ref, dcim_ref, da_ref, dwg_ref, vec_ref, gre_ref, gim_ref, car_ref):
        @pl.when(pl.program_id(0) == 0)
        def _():
            for r in (dbre_ref, dbim_ref, dcre_ref, dcim_ref, da_ref, dwg_ref, vec_ref, car_ref):
                r[...] = jnp.zeros_like(r)

        u, do, ypv = u_ref[...], do_ref[...], yp_ref[...]
        gl = _gelu(ypv)
        sg = _sigmoid(z_ref[...])
        dz = do * gl * sg * (1.0 - sg)
        dgl = do * sg + _dot_nt(dz, wg_ref[...])
        dwg_ref[...] += _dot_tn(gl, dz)
        vec_ref[0] += _fold8(dz)
        dy = dgl * _gelu_grad(ypv)
        vec_ref[1] += _fold8(dy * u)
        dyb = dy.astype(BF16)
        for k in range(nsb):
            cs, ss = slice(k * cw, (k + 1) * cw), slice(k * sw, (k + 1) * sw)
            gre_ref[:, ss] = _dot_nt(dyb[:, cs], cre_ref[k])
            gim_ref[:, ss] = -_dot_nt(dyb[:, cs], cim_ref[k])
            dcre_ref[k] += _dot_tn(sre_ref[:, ss], dyb[:, cs])
            dcim_ref[k] -= _dot_tn(sim_ref[:, ss], dyb[:, cs])

        row = lax.broadcasted_iota(jnp.int32, (8, sw), 0)

        def extra(r0, sl, gr, gi, cr, ci):
            nr = jnp.where(row == 7, cr, pltpu.roll(gr, 7, 0))
            ni = jnp.where(row == 7, ci, pltpu.roll(gi, 7, 0))
            sr, si = sre_ref[pl.ds(r0, 8), sl], sim_ref[pl.ds(r0, 8), sl]
            da_ref[0, :, sl] += nr * sr + ni * si
            da_ref[1, :, sl] += ni * sr - nr * si

        _s5_block_scan(gre_ref, gim_ref, cst_ref, car_ref, tc // 8, True, sw, extra)
        ub = u.astype(BF16)
        for k in range(nsb):
            cs, ss = slice(k * cw, (k + 1) * cw), slice(k * sw, (k + 1) * sw)
            gr, gi = gre_ref[:, ss].astype(BF16), gim_ref[:, ss].astype(BF16)
            duk = dy[:, cs] * d_ref[:, cs] + _dot_nt(gr, bre_ref[k]) + _dot_nt(gi, bim_ref[k])
            du_ref[:, cs] = duk.astype(BF16)
            dbre_ref[k] += _dot_tn(ub[:, cs], gr)
            dbim_ref[k] += _dot_tn(ub[:, cs], gi)

    def rowm(c):
        return pl.BlockSpec((tc, M), lambda i: (nc - 1 - i, c))
    rows = pl.BlockSpec((tc, NS), lambda i: (nc - 1 - i, 0))
    return pl.pallas_call(
        body, name="s5_bwd", grid=(nc,),
        in_specs=[rowm(dcol), rowm(col), rows, rows, rowm(0), rowm(0), _full(bre.shape), _full(bim.shape), _full(cre.shape),
                  _full(cim.shape), _full(cst.shape), _full((1, M)), _full((M, M))],
        out_specs=[rowm(0), _full(bre.shape), _full(bim.shape), _full(cre.shape), _full(cim.shape), _full((2, 8, NS)),
                   _full((M, M)), _full((3, 8, M))],
        out_shape=[jax.ShapeDtypeStruct((T, M), BF16), jax.ShapeDtypeStruct(bre.shape, F32), jax.ShapeDtypeStruct(bim.shape, F32),
                   jax.ShapeDtypeStruct(cre.shape, F32), jax.ShapeDtypeStruct(cim.shape, F32), jax.ShapeDtypeStruct((2, 8, NS), F32),
                   jax.ShapeDtypeStruct((M, M), F32), jax.ShapeDtypeStruct((3, 8, M), F32)],
        scratch_shapes=[pltpu.VMEM((tc, NS), F32), pltpu.VMEM((tc, NS), F32), pltpu.VMEM((2, 8, NS), F32)],
        compiler_params=ARB,
    )(dmix, proj, sre, sim, yp, z, bre, bim, cre, cim, cst, dvec.reshape(1, M), wglu)


def _ln_silu(h1, lg, lb):
    mu = jnp.mean(h1, axis=-1, keepdims=True)
    xc = h1 - mu
    rstd = lax.rsqrt(jnp.mean(xc * xc, axis=-1, keepdims=True) + EPS)
    xhat = xc * rstd
    h2 = xhat * lg + lb
    return xhat, rstd, h2, _sigmoid(h2)


def _shifted_copies(buf_ref, sh_ref, cs, n):
    for s in range(1, 8):
        sh_ref[s, 0:n, :] = buf_ref[pl.ds(s, n), cs]


def _tap(buf_ref, sh_ref, cs, off, r0, rows):
    q, s = divmod(off, 8)
    return buf_ref[pl.ds(r0 + 8 * q, rows), cs] if s == 0 else sh_ref[s, pl.ds(r0 + 8 * q, rows), :]


def _cv_fwd(proj, vcol, gcol, wdw, bdw, lg, lb, wpw, bpw):
    T = proj.shape[0]
    M = wpw.shape[0]
    tc = _pick(T, 256, 8)
    rb = _pick(tc, 64, 8)
    H, K = CONV_HALO, 31

    def body(v_ref, g_ref, wdw_ref, bdw_ref, lg_ref, lb_ref, wpw_ref, bpw_ref, y_ref, h1_ref, buf_ref, sh_ref):
        @pl.when(pl.program_id(0) == 0)
        def _():
            buf_ref[0:H, :] = jnp.zeros((H, M), F32)

        buf_ref[H:H + tc, :] = v_ref[...] * _sigmoid(g_ref[...])

        def strip(s, _):
            cs = pl.ds(pl.multiple_of(s * LANES, LANES), LANES)
            _shifted_copies(buf_ref, sh_ref, cs, tc + H - 8)
            for r0 in range(0, tc, rb):
                acc = jnp.zeros((rb, LANES), F32) + bdw_ref[:, cs]
                for k in range(K):
                    acc = acc + wdw_ref[k:k + 1, cs] * _tap(buf_ref, sh_ref, cs, H - (K - 1) + k, r0, rb)
                h1_ref[pl.ds(r0, rb), cs] = acc
            return 0

        lax.fori_loop(0, M // LANES, strip, 0)
        _, _, h2, sg = _ln_silu(h1_ref[...], lg_ref[...], lb_ref[...])
        y_ref[...] = (_dot(h2 * sg, wpw_ref[...]) + bpw_ref[...]).astype(BF16)
        buf_ref[0:H, :] = buf_ref[tc:tc + H, :]

    def rowm(c):
        return pl.BlockSpec((tc, M), lambda i: (i, c))
    vec = _full((1, M))
    return pl.pallas_call(
        body, name="cv_fwd", grid=(T // tc,),
        in_specs=[rowm(vcol), rowm(gcol), _full((32, M)), vec, vec, vec, _full((M, M)), vec],
        out_specs=[rowm(0), rowm(0)],
        out_shape=[jax.ShapeDtypeStruct((T, M), BF16), jax.ShapeDtypeStruct((T, M), F32)],
        scratch_shapes=[pltpu.VMEM((tc + H, M), F32), pltpu.VMEM((8, tc + H, LANES), F32)], compiler_params=ARB,
    )(proj, proj, wdw, bdw.reshape(1, M), lg.reshape(1, M), lb.reshape(1, M), wpw, bpw.reshape(1, M))


def _cv_bwd(dmix, dcol, proj, vcol, gcol, h1, wdw, lg, lb, wpw):
    T = proj.shape[0]
    M = wpw.shape[0]
    tc = _pick(T, 256, 8)
    nc = T // tc
    rb = _pick(tc, 32, 16)
    H, K = CONV_HALO, 31
    hb = tc // H

    def body(do_ref, v_ref, g_ref, vh_ref, gh_ref, h1_ref, wdw_ref, lg_ref, lb_ref, wpw_ref,
             dv_ref, dg_ref, dwpw_ref, dwdw_ref, vec_ref, hbuf_ref, dbuf_ref, hsh_ref, dsh_ref):
        i = pl.program_id(0)

        @pl.when(i == 0)
        def _():
            for r in (dwpw_ref, dwdw_ref, vec_ref):
                r[...] = jnp.zeros_like(r)
            dbuf_ref[tc:tc + H, :] = jnp.zeros((H, M), F32)

        halo = vh_ref[...] * _sigmoid(gh_ref[...])
        hbuf_ref[0:H, :] = jnp.where(i == nc - 1, 0.0, halo)
        hbuf_ref[H:H + tc, :] = v_ref[...] * _sigmoid(g_ref[...])
        do = do_ref[...]
        xhat, rstd, h2, sg = _ln_silu(h1_ref[...], lg_ref[...], lb_ref[...])
        dwpw_ref[...] += _dot_tn(h2 * sg, do)
        vec_ref[0] += _fold8(do)
        dh2 = _dot_nt(do, wpw_ref[...]) * (sg * (1.0 + h2 * (1.0 - sg)))
        vec_ref[1] += _fold8(dh2 * xhat)
        vec_ref[2] += _fold8(dh2)
        dxh = dh2 * lg_ref[...]
        dh1 = rstd * (dxh - jnp.mean(dxh, axis=-1, keepdims=True) - xhat * jnp.mean(dxh * xhat, axis=-1, keepdims=True))
        vec_ref[3] += _fold8(dh1)
        dbuf_ref[0:tc, :] = dh1

        def strip(s, _):
            cs = pl.ds(pl.multiple_of(s * LANES, LANES), LANES)
            _shifted_copies(dbuf_ref, dsh_ref, cs, tc + H - 8)
            _shifted_copies(hbuf_ref, hsh_ref, cs, tc + H - 8)
            for r0 in range(0, tc, rb):
                d1 = dbuf_ref[pl.ds(r0, rb), cs]
                dh0 = jnp.zeros((rb, LANES), F32)
                for k in range(K):
                    dh0 = dh0 + wdw_ref[k:k + 1, cs] * _tap(dbuf_ref, dsh_ref, cs, K - 1 - k, r0, rb)
                    dwdw_ref[k, :, cs] += _fold8(d1 * _tap(hbuf_ref, hsh_ref, cs, H - (K - 1) + k, r0, rb))
                sg0 = _sigmoid(g_ref[pl.ds(r0, rb), cs])
                dv_ref[pl.ds(r0, rb), cs] = (dh0 * sg0).astype(BF16)
                dg_ref[pl.ds(r0, rb), cs] = (dh0 * v_ref[pl.ds(r0, rb), cs] * sg0 * (1.0 - sg0)).astype(BF16)
            return 0

        lax.fori_loop(0, M // LANES, strip, 0)
        dbuf_ref[tc:tc + H, :] = dbuf_ref[0:H, :]

    def rowm(c):
        return pl.BlockSpec((tc, M), lambda i: (nc - 1 - i, c))

    def halo(c):
        return pl.BlockSpec((H, M), lambda i: (jnp.maximum((nc - 1 - i) * hb - 1, 0), c))
    vec = _full((1, M))
    return pl.pallas_call(
        body, name="cv_bwd", grid=(nc,),
        in_specs=[rowm(dcol), rowm(vcol), rowm(gcol), halo(vcol), halo(gcol), rowm(0), _full((32, M)), vec, vec, _full((M, M))],
        out_specs=[rowm(0), rowm(0), _full((M, M)), _full((32, 8, M)), _full((5, 8, M))],
        out_shape=[jax.ShapeDtypeStruct((T, M), BF16), jax.ShapeDtypeStruct((T, M), BF16), jax.ShapeDtypeStruct((M, M), F32),
                   jax.ShapeDtypeStruct((32, 8, M), F32), jax.ShapeDtypeStruct((5, 8, M), F32)],
        scratch_shapes=[pltpu.VMEM((tc + H, M), F32), pltpu.VMEM((tc + H, M), F32), pltpu.VMEM((8, tc + H, LANES), F32),
                        pltpu.VMEM((8, tc + H, LANES), F32)], compiler_params=ARB,
    )(dmix, proj, proj, proj, proj, h1, wdw, lg.reshape(1, M), lb.reshape(1, M), wpw)


def _lru_gates(xc, r, ig, sp8):
    la = -r * sp8
    a = jnp.exp(la)
    mult = jnp.sqrt(_neg_expm1(2.0 * la))
    return a, mult, mult * (ig * xc)


def _lru_fwd(proj, xcol, gcol, wcv, bcv, wr, br, wi, bi, sp8):
    T = proj.shape[0]
    M = wr.shape[0]
    tc = _pick(T, 256, 8)
    H, K = SMALL_HALO, 4

    def body(x_ref, g_ref, wcv_ref, bcv_ref, wr_ref, br_ref, wi_ref, bi_ref, sp_ref,
             y_ref, xc_ref, r_ref, i_ref, hp_ref, buf_ref, a_ref, car_ref):
        @pl.when(pl.program_id(0) == 0)
        def _():
            buf_ref[0:H, :] = jnp.zeros((H, M), F32)
            car_ref[...] = jnp.zeros_like(car_ref)

        buf_ref[H:H + tc, :] = x_ref[...]
        xc = jnp.zeros((tc, M), F32) + bcv_ref[...]
        for k in range(K):
            xc = xc + wcv_ref[k:k + 1, :] * buf_ref[pl.ds(H - (K - 1) + k, tc), :]
        buf_ref[0:H, :] = buf_ref[tc:tc + H, :]
        r = _sigmoid(_dot(xc, wr_ref[...]) + br_ref[...])
        ig = _sigmoid(_dot(xc, wi_ref[...]) + bi_ref[...])
        xc_ref[...] = xc
        r_ref[...] = r
        i_ref[...] = ig
        a, _, bt = _lru_gates(xc, r, ig, sp_ref[...])
        a_ref[...] = a
        hp_ref[...] = bt
        row = lax.broadcasted_iota(jnp.int32, (8, M), 0)

        def blk(ib, _):
            r0 = pl.multiple_of(ib * 8, 8)
            av, bv = a_ref[pl.ds(r0, 8), :], hp_ref[pl.ds(r0, 8), :]
            for d in (1, 2, 4):
                m = row >= d
                bv = jnp.where(m, bv + av * pltpu.roll(bv, d, 0), bv)
                av = jnp.where(m, av * pltpu.roll(av, d, 0), av)
            cr = car_ref[7:8, :]
            h = bv + av * cr
            a_ref[pl.ds(r0, 8), :] = h
            hp_ref[pl.ds(r0, 8), :] = jnp.where(row == 0, cr, pltpu.roll(h, 1, 0))
            car_ref[...] = h
            return 0

        lax.fori_loop(0, tc // 8, blk, 0)
        y_ref[...] = (a_ref[...] * _gelu(g_ref[...])).astype(BF16)

    def rowm(c):
        return pl.BlockSpec((tc, M), lambda i: (i, c))
    vec = _full((1, M))
    return pl.pallas_call(
        body, name="lru_fwd", grid=(T // tc,),
        in_specs=[rowm(xcol), rowm(gcol), _full((8, M)), vec, _full((M, M)), vec, _full((M, M)), vec, vec],
        out_specs=[rowm(0)] * 5,
        out_shape=[jax.ShapeDtypeStruct((T, M), BF16)] + [jax.ShapeDtypeStruct((T, M), F32)] * 4,
        scratch_shapes=[pltpu.VMEM((tc + H, M), F32), pltpu.VMEM((tc, M), F32), pltpu.VMEM((8, M), F32)], compiler_params=ARB,
    )(proj, proj, wcv, bcv.reshape(1, M), wr, br.reshape(1, M), wi, bi.reshape(1, M), sp8.reshape(1, M))


def _lru_bwd(dmix, dcol, proj, xcol, gcol, xc, r, ig, hprev, wcv, wr, wi, sp8):
    T = proj.shape[0]
    M = wr.shape[0]
    tc = _pick(T, 256, 8)
    nc = T // tc
    H, K = SMALL_HALO, 4
    hb = tc // H

    def body(do_ref, x_ref, g_ref, xh_ref, xc_ref, r_ref, i_ref, hp_ref, wcv_ref, wr_ref, wi_ref, sp_ref,
             dx_ref, dg_ref, dwr_ref, dwi_ref, dwcv_ref, vec_ref, xbuf_ref, dbuf_ref, a_ref, gs_ref, car_ref):
        i = pl.program_id(0)

        @pl.when(i == 0)
        def _():
            for rf in (dwr_ref, dwi_ref, dwcv_ref, vec_ref, car_ref):
                rf[...] = jnp.zeros_like(rf)
            dbuf_ref[tc:tc + H, :] = jnp.zeros((H, M), F32)

        xcv, rv, iv, hpv, gv, do = xc_ref[...], r_ref[...], i_ref[...], hp_ref[...], g_ref[...], do_ref[...]
        sp = sp_ref[...]
        a, mult, bt = _lru_gates(xcv, rv, iv, sp)
        h = a * hpv + bt
        ge = _gelu(gv)
        dg_ref[...] = (do * h * _gelu_grad(gv)).astype(BF16)
        a_ref[...] = a
        gs_ref[...] = do * ge
        row = lax.broadcasted_iota(jnp.int32, (8, M), 0)

        def blk(n, _):
            r0 = pl.multiple_of((tc // 8 - 1 - n) * 8, 8)
            ablk = a_ref[pl.ds(r0, 8), :]
            av = jnp.where(row == 7, car_ref[0, 0:1, :], pltpu.roll(ablk, 7, 0))
            bv = gs_ref[pl.ds(r0, 8), :]
            for d in (1, 2, 4):
                m = row <= 7 - d
                bv = jnp.where(m, bv + av * pltpu.roll(bv, 8 - d, 0), bv)
                av = jnp.where(m, av * pltpu.roll(av, 8 - d, 0), av)
            gsv = bv + av * car_ref[1, 0:1, :]
            gs_ref[pl.ds(r0, 8), :] = gsv
            car_ref[0] = ablk
            car_ref[1] = gsv
            return 0

        lax.fori_loop(0, tc // 8, blk, 0)
        gs = gs_ref[...]
        dixc = gs * mult
        dla = gs * hpv * a - gs * (iv * xcv) * (a * a) / mult
        vec_ref[2] += _fold8(-dla * rv)
        dpr = (-dla * sp) * rv * (1.0 - rv)
        dpi = (dixc * xcv) * iv * (1.0 - iv)
        vec_ref[0] += _fold8(dpr)
        vec_ref[1] += _fold8(dpi)
        dwr_ref[...] += _dot_tn(xcv, dpr)
        dwi_ref[...] += _dot_tn(xcv, dpi)
        dxc = dixc * iv + _dot_nt(dpr, wr_ref[...]) + _dot_nt(dpi, wi_ref[...])
        vec_ref[3] += _fold8(dxc)
        xbuf_ref[0:H, :] = jnp.where(i == nc - 1, 0.0, xh_ref[...])
        xbuf_ref[H:H + tc, :] = x_ref[...]
        dbuf_ref[0:tc, :] = dxc
        dx = jnp.zeros((tc, M), F32)
        for k in range(K):
            dx = dx + wcv_ref[k:k + 1, :] * dbuf_ref[pl.ds(K - 1 - k, tc), :]
            dwcv_ref[k] += _fold8(dxc * xbuf_ref[pl.ds(H - (K - 1) + k, tc), :])
        dx_ref[...] = dx.astype(BF16)
        dbuf_ref[tc:tc + H, :] = dbuf_ref[0:H, :]

    def rowm(c):
        return pl.BlockSpec((tc, M), lambda i: (nc - 1 - i, c))
    halo = pl.BlockSpec((H, M), lambda i: (jnp.maximum((nc - 1 - i) * hb - 1, 0), xcol))
    vec = _full((1, M))
    return pl.pallas_call(
        body, name="lru_bwd", grid=(nc,),
        in_specs=[rowm(dcol), rowm(xcol), rowm(gcol), halo, rowm(0), rowm(0), rowm(0), rowm(0), _full((8, M)),
                  _full((M, M)), _full((M, M)), vec],
        out_specs=[rowm(0), rowm(0), _full((M, M)), _full((M, M)), _full((8, 8, M)), _full((4, 8, M))],
        out_shape=[jax.ShapeDtypeStruct((T, M), BF16), jax.ShapeDtypeStruct((T, M), BF16), jax.ShapeDtypeStruct((M, M), F32),
                   jax.ShapeDtypeStruct((M, M), F32), jax.ShapeDtypeStruct((8, 8, M), F32), jax.ShapeDtypeStruct((4, 8, M), F32)],
        scratch_shapes=[pltpu.VMEM((tc + H, M), F32), pltpu.VMEM((tc + H, M), F32), pltpu.VMEM((tc, M), F32),
                        pltpu.VMEM((tc, M), F32), pltpu.VMEM((2, 8, M), F32)],
        compiler_params=ARB,
    )(dmix, proj, proj, proj, xc, r, ig, hprev, wcv, wr, wi, sp8.reshape(1, M))


def _pool_diffs(buf_ref, x, t0, tc, gw):
    H = POOL_HALO
    pos = (lax.broadcasted_iota(jnp.int32, (tc, gw), 0) + t0 + 1).astype(F32)
    out, inv = [], []
    for gi, win in enumerate(POOL_WINDOWS):
        sl = slice(gi * gw, (gi + 1) * gw)
        s = x[:, sl]
        for j in range(1, win):
            s = s + buf_ref[pl.ds(H - j, tc), sl]
        ic = 1.0 / jnp.minimum(pos, float(win))
        out.append(s * ic - x[:, sl])
        inv.append(ic)
    return out, inv


def _pool_fwd(proj, col, pw, scale):
    T = proj.shape[0]
    ng, gw, _ = pw.shape
    M = ng * gw
    tc = _pick(T, 256, 8)
    H = POOL_HALO

    def body(x_ref, pw_ref, sc_ref, y_ref, buf_ref):
        @pl.when(pl.program_id(0) == 0)
        def _():
            buf_ref[0:H, :] = jnp.zeros((H, M), F32)

        x = x_ref[...]
        buf_ref[H:H + tc, :] = x
        diffs, _ = _pool_diffs(buf_ref, x, pl.program_id(0) * tc, tc, gw)
        for gi in range(ng):
            sl = slice(gi * gw, (gi + 1) * gw)
            y_ref[:, sl] = (_dot(diffs[gi], pw_ref[gi]) * sc_ref[:, sl]).astype(BF16)
        buf_ref[0:H, :] = buf_ref[tc:tc + H, :]

    return pl.pallas_call(
        body, name="pool_fwd", grid=(T // tc,),
        in_specs=[pl.BlockSpec((tc, M), lambda i: (i, col)), _full(pw.shape), _full((1, M))],
        out_specs=pl.BlockSpec((tc, M), lambda i: (i, 0)), out_shape=jax.ShapeDtypeStruct((T, M), BF16),
        scratch_shapes=[pltpu.VMEM((tc + H, M), F32)], compiler_params=ARB,
    )(proj, pw, scale.reshape(1, M))


def _pool_bwd(dmix, dcol, proj, col, pw, scale):
    T = proj.shape[0]
    ng, gw, _ = pw.shape
    M = ng * gw
    tc = _pick(T, 256, 8)
    nc = T // tc
    H = POOL_HALO
    hb = tc // H

    def body(do_ref, x_ref, xh_ref, pw_ref, sc_ref, dx_ref, dpw_ref, dsc_ref, buf_ref, ebuf_ref):
        i = pl.program_id(0)

        @pl.when(i == 0)
        def _():
            dpw_ref[...] = jnp.zeros_like(dpw_ref)
            dsc_ref[...] = jnp.zeros_like(dsc_ref)
            ebuf_ref[tc:tc + H, :] = jnp.zeros((H, M), F32)

        x, do = x_ref[...], do_ref[...]
        buf_ref[0:H, :] = jnp.where(i == nc - 1, 0.0, xh_ref[...])
        buf_ref[H:H + tc, :] = x
        diffs, inv = _pool_diffs(buf_ref, x, (nc - 1 - i) * tc, tc, gw)
        ddg = []
        for gi in range(ng):
            sl = slice(gi * gw, (gi + 1) * gw)
            dyy = do[:, sl] * sc_ref[:, sl]
            dsc_ref[:, sl] += _fold8(do[:, sl] * _dot(diffs[gi], pw_ref[gi]))
            dpw_ref[gi] += _dot_tn(diffs[gi], dyy)
            d = _dot_nt(dyy, pw_ref[gi])
            ddg.append(d)
            ebuf_ref[0:tc, sl] = d * inv[gi]
        for gi, win in enumerate(POOL_WINDOWS):
            sl = slice(gi * gw, (gi + 1) * gw)
            s = -ddg[gi]
            for j in range(win):
                s = s + ebuf_ref[pl.ds(j, tc), sl]
            dx_ref[:, sl] = s.astype(BF16)
        ebuf_ref[tc:tc + H, :] = ebuf_ref[0:H, :]

    return pl.pallas_call(
        body, name="pool_bwd", grid=(nc,),
        in_specs=[pl.BlockSpec((tc, M), lambda i: (nc - 1 - i, dcol)), pl.BlockSpec((tc, M), lambda i: (nc - 1 - i, col)),
                  pl.BlockSpec((H, M), lambda i: (jnp.maximum((nc - 1 - i) * hb - 1, 0), col)), _full(pw.shape), _full((1, M))],
        out_specs=[pl.BlockSpec((tc, M), lambda i: (nc - 1 - i, 0)), _full(pw.shape), _full((8, M))],
        out_shape=[jax.ShapeDtypeStruct((T, M), BF16), jax.ShapeDtypeStruct(pw.shape, F32), jax.ShapeDtypeStruct((8, M), F32)],
        scratch_shapes=[pltpu.VMEM((tc + H, M), F32), pltpu.VMEM((tc + H, M), F32)], compiler_params=ARB,
    )(dmix, proj, proj, pw, scale.reshape(1, M))


def _ffn_mid_fwd(up, wdw, bdw):
    T, F2 = up.shape
    F = F2 // 2
    tc = _pick(T, 256, 8)
    rb = _pick(tc, 64, 16)
    H, K = FFN_HALO, 3

    def body(gt_ref, val_ref, w_ref, b_ref, act_ref, buf_ref):
        @pl.when(pl.program_id(0) == 0)
        def _():
            buf_ref[0:H, :] = jnp.zeros((H, F), F32)

        buf_ref[H:H + tc, :] = gt_ref[...].astype(F32)

        def strip(s, _):
            cs = pl.ds(pl.multiple_of(s * LANES, LANES), LANES)
            taps = [w_ref[k:k + 1, cs] for k in range(K)]
            for r0 in range(0, tc, rb):
                gc = b_ref[:, cs] + taps[0] * buf_ref[pl.ds(H - (K - 1) + r0, rb), cs]
                for k in range(1, K):
                    gc = gc + taps[k] * buf_ref[pl.ds(H - (K - 1) + k + r0, rb), cs]
                act_ref[pl.ds(r0, rb), cs] = (_gelu(gc) * val_ref[pl.ds(r0, rb), cs].astype(F32)).astype(BF16)
            return 0

        lax.fori_loop(0, F // LANES, strip, 0)
        buf_ref[0:H, :] = buf_ref[tc:tc + H, :]

    return pl.pallas_call(
        body, name="ffn_mid_fwd", grid=(T // tc,),
        in_specs=[pl.BlockSpec((tc, F), lambda i: (i, 0)), pl.BlockSpec((tc, F), lambda i: (i, 1)), _full((8, F)), _full((1, F))],
        out_specs=pl.BlockSpec((tc, F), lambda i: (i, 0)), out_shape=jax.ShapeDtypeStruct((T, F), BF16),
        scratch_shapes=[pltpu.VMEM((tc + H, F), F32)], compiler_params=ARB,
    )(up, up, wdw, bdw.reshape(1, F))


def _ffn_mid_bwd(dact, up, wdw, bdw):
    T, F2 = up.shape
    F = F2 // 2
    tc = _pick(T, 128, 8)
    rb = _pick(tc, 64, 16)
    nc = T // tc
    H, K = FFN_HALO, 3
    hb = tc // H

    def body(da_ref, gt_ref, val_ref, gh_ref, w_ref, b_ref, dup_ref, dw_ref, db_ref, gbuf_ref, dbuf_ref):
        i = pl.program_id(0)

        @pl.when(i == 0)
        def _():
            dw_ref[...] = jnp.zeros_like(dw_ref)
            db_ref[...] = jnp.zeros_like(db_ref)
            dbuf_ref[tc:tc + H, :] = jnp.zeros((H, F), F32)

        gbuf_ref[0:H, :] = jnp.where(i == nc - 1, 0.0, gh_ref[...].astype(F32))
        gbuf_ref[H:H + tc, :] = gt_ref[...].astype(F32)

        def strip(s, _):
            cs = pl.ds(pl.multiple_of(s * LANES, LANES), LANES)
            cs_val = pl.ds(pl.multiple_of(F + s * LANES, LANES), LANES)
            taps = [w_ref[k:k + 1, cs] for k in range(K)]
            for r0 in range(0, tc, rb):
                shifted = [gbuf_ref[pl.ds(H - (K - 1) + k + r0, rb), cs] for k in range(K)]
                gc = b_ref[:, cs] + taps[0] * shifted[0]
                for k in range(1, K):
                    gc = gc + taps[k] * shifted[k]
                ge, gg = _gelu_and_grad(gc)
                da = da_ref[pl.ds(r0, rb), cs].astype(F32)
                dup_ref[pl.ds(r0, rb), cs_val] = (da * ge).astype(BF16)
                dgc = da * val_ref[pl.ds(r0, rb), cs].astype(F32) * gg
                dbuf_ref[pl.ds(r0, rb), cs] = dgc
                db_ref[:, cs] += _fold8(dgc)
                for k in range(K):
                    dw_ref[k, :, cs] += _fold8(dgc * shifted[k])
            for r0 in range(0, tc, rb):
                dgt = taps[0] * dbuf_ref[pl.ds(K - 1 + r0, rb), cs]
                for k in range(1, K):
                    dgt = dgt + taps[k] * dbuf_ref[pl.ds(K - 1 - k + r0, rb), cs]
                dup_ref[pl.ds(r0, rb), cs] = dgt.astype(BF16)
            return 0

        lax.fori_loop(0, F // LANES, strip, 0)
        dbuf_ref[tc:tc + H, :] = dbuf_ref[0:H, :]

    return pl.pallas_call(
        body, name="ffn_mid_bwd", grid=(nc,),
        in_specs=[pl.BlockSpec((tc, F), lambda i: (nc - 1 - i, 0)), pl.BlockSpec((tc, F), lambda i: (nc - 1 - i, 0)),
                  pl.BlockSpec((tc, F), lambda i: (nc - 1 - i, 1)),
                  pl.BlockSpec((H, F), lambda i: (jnp.maximum((nc - 1 - i) * hb - 1, 0), 0)), _full((8, F)), _full((1, F))],
        out_specs=[pl.BlockSpec((tc, F2), lambda i: (nc - 1 - i, 0)), _full((8, 8, F)), _full((8, F))],
        out_shape=[jax.ShapeDtypeStruct((T, F2), BF16), jax.ShapeDtypeStruct((8, 8, F), F32), jax.ShapeDtypeStruct((8, F), F32)],
        scratch_shapes=[pltpu.VMEM((tc + H, F), F32), pltpu.VMEM((tc + H, F), F32)], compiler_params=ARB,
    )(dact, up, up, up, wdw, bdw.reshape(1, F))


def _my_pos():
    return lax.axis_index("x"), lax.axis_index("y"), lax.axis_index("c")


def _flip(pos, k):
    x, y, c = pos
    return ((1 - x) if k & 4 else x, (1 - y) if k & 2 else y, (1 - c) if k & 1 else c)


def _rank(pos):
    return 4 * pos[0] + 2 * pos[1] + pos[2]


def _exchange(xs, name, gather):
    n = len(xs)
    shapes = [((N_DEV,) + tuple(x.shape)) if gather else tuple(x.shape) for x in xs]

    def body(*refs):
        ins, outs = refs[:n], refs[n:2 * n]
        send_sems, recv_sems, local_sems = refs[2 * n:]
        me = _my_pos()
        mine = _rank(me)
        local, sends = [], []
        for a in range(n):
            loc = pltpu.make_async_copy(ins[a] if gather else ins[a].at[mine], outs[a].at[mine], local_sems.at[a])
            loc.start()
            local.append(loc)
            for k in range(1, N_DEV):
                peer = _flip(me, k)
                cp = pltpu.make_async_remote_copy(
                    src_ref=ins[a] if gather else ins[a].at[_rank(peer)], dst_ref=outs[a].at[mine],
                    send_sem=send_sems.at[a, k - 1], recv_sem=recv_sems.at[a, k - 1], device_id=peer,
                    device_id_type=pl.DeviceIdType.MESH)
                cp.start()
                sends.append(cp)
        for a in range(n):
            for k in range(1, N_DEV):
                peer = _flip(me, k)
                pltpu.make_async_remote_copy(
                    src_ref=ins[a] if gather else ins[a].at[_rank(peer)], dst_ref=outs[a].at[_rank(peer)],
                    send_sem=send_sems.at[a, k - 1], recv_sem=recv_sems.at[a, k - 1], device_id=peer,
                    device_id_type=pl.DeviceIdType.MESH).wait_recv()
        for cp in sends:
            cp.wait_send()
        for loc in local:
            loc.wait()

    any_spec = pl.BlockSpec(memory_space=pl.ANY)
    return pl.pallas_call(
        body, name=name, in_specs=[any_spec] * n, out_specs=[any_spec] * n,
        out_shape=[jax.ShapeDtypeStruct(s, x.dtype) for s, x in zip(shapes, xs)],
        scratch_shapes=[pltpu.SemaphoreType.DMA((n, N_DEV - 1)), pltpu.SemaphoreType.DMA((n, N_DEV - 1)),
                        pltpu.SemaphoreType.DMA((n,))],
    )(*xs)


_HBM = pl.BlockSpec(memory_space=pltpu.HBM)
_SEM = pl.BlockSpec(memory_space=pltpu.SEMAPHORE)
_EFFECT = pltpu.SideEffectType.DATAFLOW_SIDE_EFFECTING


def _plan_direct(gather):
    def plan(bufs, ssem, rsem, me, outgoing):
        n = len(bufs) // 2
        out = []
        for a in range(n):
            for k in range(1, N_DEV):
                peer = _flip(me, k)
                out.append(pltpu.make_async_remote_copy(
                    src_ref=bufs[a] if gather else bufs[a].at[_rank(peer)],
                    dst_ref=bufs[n + a].at[_rank(me) if outgoing else _rank(peer)],
                    send_sem=ssem.at[a * 7 + k - 1], recv_sem=rsem.at[a * 7 + k - 1], device_id=peer,
                    device_id_type=pl.DeviceIdType.MESH))
        return out
    return plan, 7


_SAME_CORE = (2, 4, 6)


def _plan_own_block(bufs, ssem, rsem, me, outgoing):
    n = len(bufs) // 2
    out = []
    for a in range(n):
        for j, k in enumerate((1,) + _SAME_CORE):
            peer = _flip(me, k)
            out.append(pltpu.make_async_remote_copy(
                src_ref=bufs[a], dst_ref=bufs[n + a].at[_rank(me) if outgoing else _rank(peer)],
                send_sem=ssem.at[a * 4 + j], recv_sem=rsem.at[a * 4 + j], device_id=peer, device_id_type=pl.DeviceIdType.MESH))
    return out


def _plan_pass_on(bufs, ssem, rsem, me, outgoing):
    sibling = _flip(me, 1)
    out = []
    for a in range(len(bufs)):
        for j, k in enumerate(_SAME_CORE):
            slot = _rank(_flip(me, k) if outgoing else _flip(sibling, k))
            out.append(pltpu.make_async_remote_copy(
                src_ref=bufs[a].at[slot], dst_ref=bufs[a].at[slot], send_sem=ssem.at[a * 3 + j], recv_sem=rsem.at[a * 3 + j],
                device_id=sibling, device_id_type=pl.DeviceIdType.MESH))
    return out


def _split_start(bufs, plan, n_sems, name, after=None):
    nb = len(bufs)
    extra = [] if after is None else [after]

    def body(*refs):
        ssem, rsem = refs[nb + len(extra)], refs[nb + len(extra) + 1]
        for cp in plan(refs[:nb], ssem, rsem, _my_pos(), True):
            cp.start()
        token = refs[2 * nb + len(extra) + 2]
        token[...] = jnp.zeros_like(token)

    res = pl.pallas_call(
        body, name=name,
        out_shape=(pltpu.SemaphoreType.DMA((n_sems,)), pltpu.SemaphoreType.DMA((n_sems,)),
                   *[pltpu.HBM(t.shape, t.dtype) for t in bufs], jax.ShapeDtypeStruct((8, 128), F32)),
        in_specs=[_HBM] * nb + [pl.BlockSpec(memory_space=pl.ANY)] * len(extra),
        out_specs=(_SEM, _SEM, *([_HBM] * nb), pl.BlockSpec(memory_space=pltpu.VMEM)),
        input_output_aliases={i: 2 + i for i in range(nb)},
        compiler_params=pltpu.CompilerParams(has_side_effects=_EFFECT),
    )(*[pltpu.with_memory_space_constraint(t, pltpu.HBM) for t in bufs], *extra)
    return (res[0], res[1], list(res[2:2 + nb])), res[2 + nb]


def _split_wait(handle, plan, after, name):
    ssem, rsem, bufs = handle
    nb = len(bufs)

    def body(*refs):
        for cp in plan(refs[:nb], refs[nb], refs[nb + 1], _my_pos(), False):
            cp.wait_send()
            cp.wait_recv()

    res = pl.pallas_call(
        body, name=name, out_shape=tuple(pltpu.HBM(t.shape, t.dtype) for t in bufs),
        in_specs=[_HBM] * nb + [_SEM, _SEM, pl.BlockSpec(memory_space=pl.ANY)], out_specs=tuple([_HBM] * nb),
        input_output_aliases={i: i for i in range(nb)},
        compiler_params=pltpu.CompilerParams(has_side_effects=_EFFECT),
    )(*bufs, ssem, rsem, after)
    return list(res)


def _landing_zones(xs, gather):
    mine = _rank(_my_pos())
    lands = []
    for x in xs:
        own = x if gather else lax.dynamic_index_in_dim(x, mine, 0, keepdims=False)
        lands.append(lax.dynamic_update_index_in_dim(lax.empty((N_DEV,) + tuple(own.shape), x.dtype), own, mine, 0))
    return lands


def _gather_start(xs, name, after=None, two_level=True):
    bufs = list(xs) + _landing_zones(xs, True)
    if two_level:
        handle, token = _split_start(bufs, _plan_own_block, 4 * len(xs), name + "_start", after)
    else:
        plan, per = _plan_direct(True)
        handle, token = _split_start(bufs, plan, per * len(xs), name + "_start", after)
    return (handle, two_level, name), token


def _gather_wait(state, after):
    handle, two_level, name = state
    n = len(handle[2]) // 2
    if not two_level:
        return _split_wait(handle, _plan_direct(True)[0], after, name + "_wait")[n:]
    lands = _split_wait(handle, _plan_own_block, after, name + "_wait")[n:]
    handle, _ = _split_start(lands, _plan_pass_on, 3 * n, name + "_pass")
    return _split_wait(handle, _plan_pass_on, after, name + "_passed")


def _scatter_start(xs, name):
    plan, per = _plan_direct(False)
    handle, token = _split_start(list(xs) + _landing_zones(xs, False), plan, per * len(xs), name + "_start")
    return (handle, name), token


def _scatter_wait(state, after):
    handle, name = state
    return _split_wait(handle, _plan_direct(False)[0], after, name + "_wait")[len(handle[2]) // 2:]


def _adamw_math(w, g, m, v):
    m = ADAM_B1 * m + (1.0 - ADAM_B1) * g
    v = ADAM_B2 * v + (1.0 - ADAM_B2) * (g * g)
    m_hat = m / (1.0 - ADAM_B1 ** ADAM_STEP)
    v_hat = v / (1.0 - ADAM_B2 ** ADAM_STEP)
    return -ADAM_LR * (m_hat / (jnp.sqrt(v_hat) + ADAM_EPS) + ADAM_WD * w), m, v


def _sum8(parts, *, name):
    _, R, C = parts.shape
    tr = _pick(R, 256, 16)

    def body(p_ref, o_ref):
        acc = p_ref[0].astype(F32)
        for r in range(1, N_DEV):
            acc = acc + p_ref[r].astype(F32)
        o_ref[...] = acc

    return pl.pallas_call(
        body, name=name, grid=(R // tr,), in_specs=[pl.BlockSpec((N_DEV, tr, C), lambda i: (0, i, 0))],
        out_specs=pl.BlockSpec((tr, C), lambda i: (i, 0)), out_shape=jax.ShapeDtypeStruct((R, C), F32),
        compiler_params=pltpu.CompilerParams(dimension_semantics=("parallel",)),
    )(parts)


def _adamw_stacked(gs, w, m, v, *, name):
    depth, R, C = w.shape
    tr = _pick(R, 64, 16)
    summed = gs[0].ndim == 3

    def body(*refs):
        g_refs, (w_ref, m_ref, v_ref), outs = refs[:depth], refs[depth:depth + 3], refs[depth + 3:]
        for l in range(depth):
            if summed:
                g = g_refs[l][0].astype(F32)
                for r in range(1, N_DEV):
                    g = g + g_refs[l][r].astype(F32)
            else:
                g = g_refs[l][...]
            d, nm, nv = _adamw_math(w_ref[l], g, m_ref[l], v_ref[l])
            for o, val in zip(outs, (g, d, nm, nv)):
                o[l] = val

    g_spec = pl.BlockSpec((N_DEV, tr, C), lambda i: (0, i, 0)) if summed else pl.BlockSpec((tr, C), lambda i: (i, 0))
    stk = pl.BlockSpec((depth, tr, C), lambda i: (0, i, 0))
    return pl.pallas_call(
        body, name=name, grid=(R // tr,), in_specs=[g_spec] * depth + [stk] * 3, out_specs=[stk] * 4,
        out_shape=[jax.ShapeDtypeStruct((depth, R, C), F32)] * 4,
        compiler_params=pltpu.CompilerParams(dimension_semantics=("parallel",)),
    )(*gs, w, m, v)


def _lane_view(shape):
    if len(shape) == 1:
        return (1, shape[0])
    size = math.prod(shape)
    if shape[-1] < 64 and size % LANES == 0:
        return (size // LANES, LANES)
    if len(shape) > 2 and shape[-2] % 8 == 0:
        return (size // shape[-1], shape[-1])
    return tuple(shape)


def _adamw_many(ws, gs, ms, vs, *, name):
    n = len(ws)
    views = [_lane_view(w.shape) for w in ws]

    def body(*refs):
        for i in range(n):
            d, nm, nv = _adamw_math(refs[i][...], refs[n + i][...], refs[2 * n + i][...], refs[3 * n + i][...])
            refs[4 * n + i][...] = d
            refs[5 * n + i][...] = nm
            refs[6 * n + i][...] = nv

    outs = pl.pallas_call(
        body, name=name, out_shape=[jax.ShapeDtypeStruct(s, F32) for s in views] * 3,
    )(*[a.reshape(s) for arrs in (ws, gs, ms, vs) for a, s in zip(arrs, views)])
    return [[o.reshape(w.shape) for o, w in zip(outs[q * n:(q + 1) * n], ws)] for q in range(3)]


def _pack(arrs):
    flat = jnp.concatenate([a.reshape(-1).astype(F32) for a in arrs])
    pad = (-flat.shape[0]) % 1024
    return jnp.pad(flat, (0, pad)).reshape(-1, 128)


def _unpack(buf, shapes):
    flat, out, off = buf.reshape(-1), [], 0
    for s in shapes:
        n = math.prod(s)
        out.append(flat[off:off + n].reshape(s))
        off += n
    return out


def _s5_params(lam_re, lam_im, log_step, b_re, b_im):
    G, P, H = b_re.shape
    step = jnp.exp(log_step)[:, None]
    mag = jnp.exp(lam_re * step)
    ar, ai = mag * jnp.cos(lam_im * step), mag * jnp.sin(lam_im * step)
    den = lam_re * lam_re + lam_im * lam_im
    qr = ((ar - 1.0) * lam_re + ai * lam_im) / den
    qi = (ai * lam_re - (ar - 1.0) * lam_im) / den
    bbr = qr[..., None] * b_re - qi[..., None] * b_im
    bbi = qr[..., None] * b_im + qi[..., None] * b_re
    return ar.reshape(-1), ai.reshape(-1), _s5_blocks_in(bbr), _s5_blocks_in(bbi)


def _s5_blocks_in(b):
    G, P, H = b.shape
    eye = jnp.eye(8, dtype=b.dtype)
    return jnp.einsum("kgph,gj->kghjp", b.reshape(G // 8, 8, P, H), eye).reshape(G // 8, 8 * H, 8 * P)


def _s5_blocks_in_diag(m, G, P, H):
    return jnp.einsum("kghgp->kgph", m.reshape(G // 8, 8, H, 8, P)).reshape(G, P, H)


def _s5_blocks_out(c):
    G, H, P = c.shape
    eye = jnp.eye(8, dtype=c.dtype)
    return jnp.einsum("kghp,gj->kgpjh", c.reshape(G // 8, 8, H, P), eye).reshape(G // 8, 8 * P, 8 * H)


def _s5_blocks_out_diag(m, G, H, P):
    return jnp.einsum("kgpgh->kghp", m.reshape(G // 8, 8, P, 8, H)).reshape(G, H, P)


def _head_blocks(w):
    nh, d, e = w.shape
    return jnp.einsum("hde,hk->hdke", w, jnp.eye(nh, dtype=w.dtype)).reshape(nh * d, nh * e)


def _head_blocks_diag(m, nh, d, e):
    return jnp.einsum("hdhe->hde", m.reshape(nh, d, nh, e))


def _pad_rows(w, rows):
    return jnp.pad(w, ((0, rows - w.shape[0]), (0, 0)))


def _unshard_cols(g):
    return jnp.transpose(g, (1, 0, 2)).reshape(g.shape[1], -1)


_NAMES = ['norm_mix_g', 'w_in', 's5_lam_re', 's5_lam_im', 's5_log_step', 's5_b_re', 's5_b_im', 's5_c_re', 's5_c_im', 's5_d',
          's5_w_glu', 's5_b_glu', 'cv_w_dw', 'cv_b_dw', 'cv_ln_g', 'cv_ln_b', 'cv_w_pw', 'cv_b_pw', 'lru_w_conv', 'lru_b_conv',
          'lru_w_r', 'lru_b_r', 'lru_w_i', 'lru_b_i', 'lru_lam', 'pool_w', 'pool_scale', 'w_out', 'norm_ffn_g', 'ffn_w_up',
          'ffn_w_dw', 'ffn_b_dw', 'ffn_w_down', 'norm_final_g']
_MATRIX = ('w_in', 'w_out', 'ffn_w_up', 'ffn_w_down', 's5_w_glu', 'cv_w_pw')
_TRANSPOSED = ('w_in', 'ffn_w_up')
_COLSHARD = ('cv_w_dw', 'lru_w_conv', 'ffn_w_dw')
_GROUPS = (('w_in', 's5_w_glu', 'cv_w_pw', 'w_out'), ('ffn_w_up', 'ffn_w_down'))
_SCATTER_GROUPS = (('w_out', 's5_w_glu', 'cv_w_pw'), ('w_in',), ('ffn_w_up', 'ffn_w_down'))


def kernel(x, norm_mix_g, w_in, s5_lam_re, s5_lam_im, s5_log_step, s5_b_re, s5_b_im, s5_c_re, s5_c_im, s5_d, s5_w_glu, s5_b_glu, cv_w_dw, cv_b_dw, cv_ln_g, cv_ln_b, cv_w_pw, cv_b_pw, lru_w_conv, lru_b_conv, lru_w_r, lru_b_r, lru_w_i, lru_b_i, lru_lam, pool_w, pool_scale, w_out, norm_ffn_g, ffn_w_up, ffn_w_dw, ffn_b_dw, ffn_w_down, norm_final_g, loss_target, m_norm_mix_g, m_w_in, m_s5_lam_re, m_s5_lam_im, m_s5_log_step, m_s5_b_re, m_s5_b_im, m_s5_c_re, m_s5_c_im, m_s5_d, m_s5_w_glu, m_s5_b_glu, m_cv_w_dw, m_cv_b_dw, m_cv_ln_g, m_cv_ln_b, m_cv_w_pw, m_cv_b_pw, m_lru_w_conv, m_lru_b_conv, m_lru_w_r, m_lru_b_r, m_lru_w_i, m_lru_b_i, m_lru_lam, m_pool_w, m_pool_scale, m_w_out, m_norm_ffn_g, m_ffn_w_up, m_ffn_w_dw, m_ffn_b_dw, m_ffn_w_down, m_norm_final_g, v_norm_mix_g, v_w_in, v_s5_lam_re, v_s5_lam_im, v_s5_log_step, v_s5_b_re, v_s5_b_im, v_s5_c_re, v_s5_c_im, v_s5_d, v_s5_w_glu, v_s5_b_glu, v_cv_w_dw, v_cv_b_dw, v_cv_ln_g, v_cv_ln_b, v_cv_w_pw, v_cv_b_pw, v_lru_w_conv, v_lru_b_conv, v_lru_w_r, v_lru_b_r, v_lru_w_i, v_lru_b_i, v_lru_lam, v_pool_w, v_pool_scale, v_w_out, v_norm_ffn_g, v_ffn_w_up, v_ffn_w_dw, v_ffn_b_dw, v_ffn_w_down, v_norm_final_g):
    args = (x, norm_mix_g, w_in, s5_lam_re, s5_lam_im, s5_log_step, s5_b_re, s5_b_im, s5_c_re, s5_c_im, s5_d, s5_w_glu, s5_b_glu, cv_w_dw, cv_b_dw, cv_ln_g, cv_ln_b, cv_w_pw, cv_b_pw, lru_w_conv, lru_b_conv, lru_w_r, lru_b_r, lru_w_i, lru_b_i, lru_lam, pool_w, pool_scale, w_out, norm_ffn_g, ffn_w_up, ffn_w_dw, ffn_b_dw, ffn_w_down, norm_final_g, loss_target, m_norm_mix_g, m_w_in, m_s5_lam_re, m_s5_lam_im, m_s5_log_step, m_s5_b_re, m_s5_b_im, m_s5_c_re, m_s5_c_im, m_s5_d, m_s5_w_glu, m_s5_b_glu, m_cv_w_dw, m_cv_b_dw, m_cv_ln_g, m_cv_ln_b, m_cv_w_pw, m_cv_b_pw, m_lru_w_conv, m_lru_b_conv, m_lru_w_r, m_lru_b_r, m_lru_w_i, m_lru_b_i, m_lru_lam, m_pool_w, m_pool_scale, m_w_out, m_norm_ffn_g, m_ffn_w_up, m_ffn_w_dw, m_ffn_b_dw, m_ffn_w_down, m_norm_final_g, v_norm_mix_g, v_w_in, v_s5_lam_re, v_s5_lam_im, v_s5_log_step, v_s5_b_re, v_s5_b_im, v_s5_c_re, v_s5_c_im, v_s5_d, v_s5_w_glu, v_s5_b_glu, v_cv_w_dw, v_cv_b_dw, v_cv_ln_g, v_cv_ln_b, v_cv_w_pw, v_cv_b_pw, v_lru_w_conv, v_lru_b_conv, v_lru_w_r, v_lru_b_r, v_lru_w_i, v_lru_b_i, v_lru_lam, v_pool_w, v_pool_scale, v_w_out, v_norm_ffn_g, v_ffn_w_up, v_ffn_w_dw, v_ffn_b_dw, v_ffn_w_down, v_norm_final_g)
    x, target = args[0], args[35]
    W = dict(zip(_NAMES, args[1:35]))
    Mo = dict(zip(_NAMES, args[36:70]))
    Vo = dict(zip(_NAMES, args[70:104]))
    x = x[0]
    target = target[0]
    T, D = x.shape
    depth = W['w_in'].shape[0]
    F = W['ffn_w_down'].shape[1] * N_DEV
    G, P, H = W['s5_b_re'].shape[1:]
    MW = G * H
    nh, hd, _ = W['lru_w_r'].shape[1:]
    me = _rank(_my_pos())

    gather, full = {}, {}

    def start_group(l, gi, after):
        shards = [(W[n][l].T if n in _TRANSPOSED else W[n][l]).astype(BF16) for n in _GROUPS[gi]]
        if (l, gi) == (0, 0):
            shards.append(_pack([W[n][k] for k in range(depth) for n in _COLSHARD]))
        gather[l, gi], tok = _gather_start(shards, "gather_%d%d" % (l, gi), after)
        return tok

    def land_weights(l, gi, after):
        got = _gather_wait(gather[l, gi], after)
        for n, g8 in zip(_GROUPS[gi], got):
            full[n, l] = g8.reshape(-1, g8.shape[-1])
        return got

    started = start_group(0, 0, None)[0, 0]
    got = land_weights(0, 0, W['norm_mix_g'][0] + started)
    started = start_group(0, 1, got[0])[0, 0]
    taps8 = got[-1]
    tap_shapes = [W[n][l].shape for l in range(depth) for n in _COLSHARD]
    taps = [_unshard_cols(jnp.stack(t)) for t in zip(*[_unpack(taps8[r], tap_shapes) for r in range(N_DEV)])]
    for l in range(depth):
        for j, n in enumerate(_COLSHARD):
            full[n, l] = taps[l * len(_COLSHARD) + j]

    saved = []
    for l in range(depth):
        g_mix = W['norm_mix_g'][l] + (0.0 if l else started)
        if l:
            land_weights(l, 0, x)
        s5p, s5p_vjp = jax.vjp(_s5_params, W['s5_lam_re'][l], W['s5_lam_im'][l], W['s5_log_step'][l], W['s5_b_re'][l], W['s5_b_im'][l])
        ar, ai, bre, bim = s5p
        cre, cim = _s5_blocks_out(W['s5_c_re'][l]), _s5_blocks_out(W['s5_c_im'][l])
        sp8, sp8_vjp = jax.vjp(lambda lam: LRU_C * jax.nn.softplus(-lam), W['lru_lam'][l])
        wr, wi = _head_blocks(W['lru_w_r'][l]), _head_blocks(W['lru_w_i'][l])
        c = dict(bre=bre.astype(BF16), bim=bim.astype(BF16), cre=cre.astype(BF16), cim=cim.astype(BF16),
                 cst_f=_s5_scan_consts(ar, ai, False), cst_b=_s5_scan_consts(ar, ai, True), sp8=sp8, wr=wr.astype(BF16),
                 wi=wi.astype(BF16), cvw=_pad_rows(full['cv_w_dw', l], 32), lruw=_pad_rows(full['lru_w_conv', l], 8),
                 ffw=_pad_rows(full['ffn_w_dw', l], 8), pw=W['pool_w'][l].astype(BF16), s5p_vjp=s5p_vjp, sp8_vjp=sp8_vjp, x0=x)
        proj, h = _norm_matmul(x, g_mix, full['w_in', l], tm=1024, tn=512, name="in_proj")
        y0, sre, sim, yp, z = _s5_fwd(proj, 0, c['bre'], c['bim'], c['cre'], c['cim'], c['cst_f'], W['s5_d'][l],
                                     full['s5_w_glu', l], W['s5_b_glu'][l])
        y1, h1 = _cv_fwd(proj, 1, 2, c['cvw'], W['cv_b_dw'][l], W['cv_ln_g'][l], W['cv_ln_b'][l], full['cv_w_pw', l], W['cv_b_pw'][l])
        y2, xc, rr, ig, hprev = _lru_fwd(proj, 3, 4, c['lruw'], W['lru_b_conv'][l], c['wr'], W['lru_b_r'][l], c['wi'],
                                         W['lru_b_i'][l], sp8)
        y3 = _pool_fwd(proj, 5, c['pw'], W['pool_scale'][l])
        mixed = jnp.concatenate([y0, y1, y2, y3], axis=1)
        x1 = _matmul(mixed, full['w_out', l], mode="nn", tm=1024, tn=512, tk=2048, out_dtype=F32, name="out_proj", add=x)
        got = land_weights(l, 1, x1)
        if l + 1 < depth:
            start_group(l + 1, 0, got[0])
            start_group(l + 1, 1, got[0])
        up, h2 = _norm_matmul(x1, W['norm_ffn_g'][l], full['ffn_w_up', l], tm=1024, tn=256, name="ffn_up", out_dtype=BF16)
        act = _ffn_mid_fwd(up, c['ffw'], W['ffn_b_dw'][l])
        x = _matmul(act, full['ffn_w_down', l], mode="nn", tm=1024, tn=512, tk=F, out_dtype=F32, name="ffn_down", add=x1)
        c.update(proj=proj, h=h, sre=sre, sim=sim, yp=yp, z=z, h1=h1, xc=xc, rr=rr, ig=ig, hprev=hprev, mixed=mixed, x1=x1,
                 up=up, h2=h2, act=act)
        saved.append(c)

    dx, dxb, dgf, se = _loss_head(x, W['norm_final_g'], target)
    loss = lax.psum(0.5 / D * jnp.sum(se), AXES)
    gsmall = {('norm_final_g', None): dgf.sum(0)}
    gmat, scatter = {}, {}

    def parts_of(m):
        return m.reshape(N_DEV, m.shape[0] // N_DEV, m.shape[1])

    for l in reversed(range(depth)):
        c = saved[l]
        dact = _matmul(dxb, full['ffn_w_down', l], mode="nt", tm=512, tn=F, tk=512, out_dtype=BF16, name="d_act")
        gmat['ffn_w_down', l] = _matmul(c['act'], dxb, mode="tn", tm=F, tn=512, tk=512, out_dtype=BF16, name="dw_down")
        dup, dffw, dffb = _ffn_mid_bwd(dact, c['up'], c['ffw'], W['ffn_b_dw'][l])
        gsmall['ffn_w_dw', l] = dffw.sum(1)[:3]
        gsmall['ffn_b_dw', l] = dffb.sum(0)
        dh2 = _matmul(dup, full['ffn_w_up', l], mode="nn", tm=1024, tn=512, tk=F, out_dtype=F32, name="d_h2")
        gmat['ffn_w_up', l] = _matmul(dup, c['h2'], mode="tn", tm=256, tn=D, tk=T, out_dtype=BF16, name="dw_up")
        scatter[l, 2], tok = _scatter_start([parts_of(gmat[n, l]) for n in _SCATTER_GROUPS[2]], "scatter_%d2" % l)
        dx1, dx1b, dg2 = _rms_bwd(dx, dh2, c['x1'], W['norm_ffn_g'][l] + tok[0, 0], name="ffn_norm_bwd")
        gsmall['norm_ffn_g', l] = dg2.sum(0)
        dmix = _matmul(dx1b, full['w_out', l], mode="nt", tm=1024, tn=512, tk=2048, out_dtype=F32, name="d_mixed")
        gmat['w_out', l] = _matmul(c['mixed'], dx1b, mode="tn", tm=512, tn=D, tk=2048, out_dtype=BF16, name="dw_out")
        du, dbre, dbim, dcre, dcim, da, dwglu, v3 = _s5_bwd(dmix, 0, c['proj'], 0, c['sre'], c['sim'], c['yp'], c['z'], c['bre'],
                                                            c['bim'], c['cre'], c['cim'], c['cst_b'], W['s5_d'][l], full['s5_w_glu', l])
        da = da.sum(1)
        glr, gli, gls, gbr, gbi = c['s5p_vjp']((da[0], da[1], dbre, dbim))
        gsmall['s5_lam_re', l], gsmall['s5_lam_im', l], gsmall['s5_log_step', l] = glr, gli, gls
        gsmall['s5_b_re', l], gsmall['s5_b_im', l] = gbr, gbi
        gsmall['s5_c_re', l] = _s5_blocks_out_diag(dcre, G, H, P)
        gsmall['s5_c_im', l] = _s5_blocks_out_diag(dcim, G, H, P)
        gsmall['s5_b_glu', l], gsmall['s5_d', l] = v3[0].sum(0), v3[1].sum(0)
        gmat['s5_w_glu', l] = dwglu.astype(BF16)
        dv, dg, dwpw, dcvw, v5 = _cv_bwd(dmix, 1, c['proj'], 1, 2, c['h1'], c['cvw'], W['cv_ln_g'][l], W['cv_ln_b'][l], full['cv_w_pw', l])
        gmat['cv_w_pw', l] = dwpw.astype(BF16)
        gsmall['cv_w_dw', l] = dcvw.sum(1)[:31]
        gsmall['cv_b_pw', l], gsmall['cv_ln_g', l], gsmall['cv_ln_b', l], gsmall['cv_b_dw', l] = [v5[j].sum(0) for j in range(4)]
        scatter[l, 0], tok = _scatter_start([parts_of(gmat[n, l]) for n in _SCATTER_GROUPS[0]], "scatter_%d0" % l)
        tok = tok[0, 0]
        dlx, dlg, dwr, dwi, dlw, v4 = _lru_bwd(dmix, 2, c['proj'], 3, 4, c['xc'], c['rr'], c['ig'], c['hprev'], c['lruw'], c['wr'],
                                               c['wi'], c['sp8'] + tok)
        gsmall['lru_w_r', l], gsmall['lru_w_i', l] = _head_blocks_diag(dwr, nh, hd, hd), _head_blocks_diag(dwi, nh, hd, hd)
        gsmall['lru_w_conv', l] = dlw.sum(1)[:4]
        gsmall['lru_b_r', l], gsmall['lru_b_i', l] = v4[0].sum(0), v4[1].sum(0)
        gsmall['lru_lam', l] = c['sp8_vjp'](v4[2].sum(0))[0]
        gsmall['lru_b_conv', l] = v4[3].sum(0)
        dpx, dpw, dps = _pool_bwd(dmix, 3, c['proj'], 5, c['pw'], W['pool_scale'][l] + tok)
        gsmall['pool_w', l], gsmall['pool_scale', l] = dpw, dps.sum(0)
        dproj = jnp.concatenate([du, dv, dg, dlx, dlg, dpx], axis=1)
        dh = _matmul(dproj, full['w_in', l], mode="nn", tm=1024, tn=512, tk=1024, out_dtype=F32, name="d_h")
        gmat['w_in', l] = _matmul(dproj, c['h'], mode="tn", tm=512, tn=D, tk=2048, out_dtype=BF16, name="dw_in")
        scatter[l, 1], tok = _scatter_start([parts_of(gmat[n, l]) for n in _SCATTER_GROUPS[1]], "scatter_%d1" % l)
        dx, dxb, dg1 = _rms_bwd(dx1, dh, c['x0'], W['norm_mix_g'][l] + tok[0, 0], name="mix_norm_bwd")
        gsmall['norm_mix_g', l] = dg1.sum(0)

    small_names = [n for n in _NAMES if n not in _MATRIX]
    small_parts = [gsmall[n, l] for n in small_names for l in ((None,) if n == 'norm_final_g' else range(depth))]
    full_shapes = [gsmall[n, None].shape if n == 'norm_final_g' else (depth,) + gsmall[n, 0].shape for n in small_names]
    small_state, tok = _gather_start([_pack(small_parts)], "small_grads", two_level=False)
    parts = {}
    for l in reversed(range(depth)):
        for gi in (2, 0, 1):
            got = _scatter_wait(scatter[l, gi], tok)
            for n, p8 in zip(_SCATTER_GROUPS[gi], got):
                parts[n, l] = p8

    out = {}
    for n in _MATRIX:
        if n in _TRANSPOSED and W[n].shape[-1] % LANES:
            wt, mt, vt = (jnp.swapaxes(t, 1, 2) for t in (W[n], Mo[n], Vo[n]))
            res = _adamw_stacked([parts[n, l] for l in range(depth)], wt, mt, vt, name="adamw_" + n)
            out[n] = tuple(jnp.swapaxes(t, 1, 2) for t in res)
            continue
        if n in _TRANSPOSED:
            gl = [_sum8(parts[n, l], name="sum_" + n).T for l in range(depth)]
        else:
            gl = [parts[n, l] for l in range(depth)]
        out[n] = _adamw_stacked(gl, W[n], Mo[n], Vo[n], name="adamw_" + n)
    after_matrices = sum(out[n][1][0, 0, 0] for n in _MATRIX).reshape(1, 1)
    small8 = _gather_wait(small_state, after_matrices)[0]
    small_sum = _sum8(small8, name="sum_small_grads")
    gs = dict(zip(small_names, _unpack(small_sum, full_shapes)))
    for n in _COLSHARD:
        cw = W[n].shape[-1]
        gs[n] = lax.dynamic_slice_in_dim(gs[n], me * cw, cw, axis=gs[n].ndim - 1)
    upd = _adamw_many([W[n] for n in small_names], [gs[n] for n in small_names], [Mo[n] for n in small_names],
                      [Vo[n] for n in small_names], name="adamw_small")
    for j, n in enumerate(small_names):
        out[n] = (gs[n], upd[0][j], upd[1][j], upd[2][j])
    return (loss, dx[None]) + tuple(out[n][q] for q in range(4) for n in _NAMES)
```

```python
import functools
import math

import jax
import jax.numpy as jnp
from jax import lax
from jax.experimental import pallas as pl
from jax.experimental.pallas import tpu as pltpu

F32 = jnp.float32
BF16 = jnp.bfloat16
N_DEV = 8
AXES = ("x", "y", "c")
EPS = 1e-6
S5_GROUP_CH = 16
S5_STATE = 64
LRU_HEADS = 8
LRU_C = 8.0
POOL_WINDOWS = (2, 4, 8, 16)
CONV_HALO = 32
SMALL_HALO = 8
FFN_HALO = 16
POOL_HALO = 16
LANES = 128
ADAM_LR, ADAM_B1, ADAM_B2, ADAM_EPS, ADAM_WD, ADAM_STEP = 0.001, 0.9, 0.999, 1e-08, 0.01, 10
_GELU_K = math.sqrt(2.0 / math.pi)
ARB = pltpu.CompilerParams(dimension_semantics=("arbitrary",))


def _pick(n, pref, mult):
    best = None
    for t in range(mult, min(n, pref) + 1, mult):
        if n % t == 0:
            best = t
    return best if best is not None else n


def _sigmoid(x):
    return 1.0 / (1.0 + jnp.exp(-x))


def _gelu(x):
    return 0.5 * x * (1.0 + jnp.tanh(_GELU_K * (x + 0.044715 * x * x * x)))


def _gelu_grad(x):
    t = jnp.tanh(_GELU_K * (x + 0.044715 * x * x * x))
    return 0.5 * (1.0 + t) + 0.5 * x * (1.0 - t * t) * _GELU_K * (1.0 + 3.0 * 0.044715 * x * x)


def _gelu_and_grad(x):
    t = jnp.tanh(_GELU_K * (x + 0.044715 * x * x * x))
    half = 0.5 * (1.0 + t)
    return x * half, half + 0.5 * x * (1.0 - t * t) * _GELU_K * (1.0 + 3.0 * 0.044715 * x * x)


def _neg_expm1(x):
    series = -x * (1.0 + x * (0.5 + x * (1.0 / 6.0 + x * (1.0 / 24.0 + x * (1.0 / 120.0 + x * (1.0 / 720.0))))))
    return jnp.where(jnp.abs(x) < 0.25, series, 1.0 - jnp.exp(x))


def _fold8(x):
    return x.reshape(x.shape[0] // 8, 8, x.shape[1]).sum(axis=0)


def _dot(a, b):
    return jnp.dot(a.astype(BF16), b.astype(BF16), preferred_element_type=F32)


def _dot_nt(a, b):
    return lax.dot_general(a.astype(BF16), b.astype(BF16), (((1,), (1,)), ((), ())), preferred_element_type=F32)


def _dot_tn(a, b):
    return lax.dot_general(a.astype(BF16), b.astype(BF16), (((0,), (0,)), ((), ())), preferred_element_type=F32)


def _full(shape):
    nd = len(shape)
    return pl.BlockSpec(shape, lambda *i: (0,) * nd)


def _matmul(a, b, *, mode, tm, tn, tk, out_dtype, name, add=None):
    if mode == "nn":
        (M, K), N = a.shape, b.shape[1]
    elif mode == "nt":
        (M, K), N = a.shape, b.shape[0]
    else:
        (K, M), N = a.shape, b.shape[1]
    tm, tn, tk = _pick(M, tm, 8), _pick(N, tn, 128), _pick(K, tk, 128)
    if mode == "tn":
        tm = _pick(M, tm, 128)
    nk = K // tk
    a_spec = {"nn": pl.BlockSpec((tm, tk), lambda i, j, k: (i, k)), "nt": pl.BlockSpec((tm, tk), lambda i, j, k: (i, k)),
              "tn": pl.BlockSpec((tk, tm), lambda i, j, k: (k, i))}[mode]
    b_spec = {"nn": pl.BlockSpec((tk, tn), lambda i, j, k: (k, j)), "nt": pl.BlockSpec((tn, tk), lambda i, j, k: (j, k)),
              "tn": pl.BlockSpec((tk, tn), lambda i, j, k: (k, j))}[mode]
    o_spec = pl.BlockSpec((tm, tn), lambda i, j, k: (i, j))
    dot = {"nn": _dot, "nt": _dot_nt, "tn": _dot_tn}[mode]
    has_add = add is not None

    def body(*refs):
        if has_add:
            a_ref, b_ref, add_ref, o_ref, acc_ref = refs
        else:
            a_ref, b_ref, o_ref, acc_ref = refs
        part = dot(a_ref[...], b_ref[...])
        if nk == 1:
            o_ref[...] = (part + add_ref[...] if has_add else part).astype(out_dtype)
            return
        k = pl.program_id(2)

        @pl.when(k == 0)
        def _():
            acc_ref[...] = part

        @pl.when(k > 0)
        def _():
            acc_ref[...] += part

        @pl.when(k == nk - 1)
        def _():
            r = acc_ref[...]
            if has_add:
                r = r + add_ref[...]
            o_ref[...] = r.astype(out_dtype)

    ins = [a, b] + ([add] if has_add else [])
    specs = [a_spec, b_spec] + ([o_spec] if has_add else [])
    return pl.pallas_call(
        body, name=name, grid=(M // tm, N // tn, nk), in_specs=specs, out_specs=o_spec,
        out_shape=jax.ShapeDtypeStruct((M, N), out_dtype),
        scratch_shapes=[pltpu.VMEM((tm, tn) if nk > 1 else (8, LANES), F32)],
        compiler_params=pltpu.CompilerParams(dimension_semantics=("parallel", "parallel", "arbitrary")),
    )(*ins)


def _rmsnorm(x, g, *, name):
    T, D = x.shape
    tc = _pick(T, 256, 16)

    def body(x_ref, g_ref, h_ref):
        xv = x_ref[...]
        r = lax.rsqrt(jnp.mean(xv * xv, axis=-1, keepdims=True) + EPS)
        h_ref[...] = (xv * r * g_ref[...]).astype(BF16)

    row = pl.BlockSpec((tc, D), lambda i: (i, 0))
    return pl.pallas_call(
        body, name=name, grid=(T // tc,), in_specs=[row, _full((1, D))], out_specs=row,
        out_shape=jax.ShapeDtypeStruct((T, D), BF16), compiler_params=pltpu.CompilerParams(dimension_semantics=("parallel",)),
    )(x, g.reshape(1, D))


def _rms_bwd(dres, dh, x, g, *, name):
    T, D = x.shape
    tc = _pick(T, 256, 8)

    def body(dres_ref, dh_ref, x_ref, g_ref, dx_ref, dxb_ref, dg_ref):
        @pl.when(pl.program_id(0) == 0)
        def _():
            dg_ref[...] = jnp.zeros_like(dg_ref)

        xv, dhv = x_ref[...], dh_ref[...]
        r = lax.rsqrt(jnp.mean(xv * xv, axis=-1, keepdims=True) + EPS)
        dg_ref[...] += _fold8(dhv * xv * r)
        dyg = dhv * g_ref[...]
        dx = dres_ref[...] + r * dyg - xv * (r * r * r) * jnp.mean(dyg * xv, axis=-1, keepdims=True)
        dx_ref[...] = dx
        dxb_ref[...] = dx.astype(BF16)

    row = pl.BlockSpec((tc, D), lambda i: (i, 0))
    return pl.pallas_call(
        body, name=name, grid=(T // tc,), in_specs=[row, row, row, _full((1, D))],
        out_specs=[row, row, _full((8, D))],
        out_shape=[jax.ShapeDtypeStruct((T, D), F32), jax.ShapeDtypeStruct((T, D), BF16), jax.ShapeDtypeStruct((8, D), F32)],
        compiler_params=ARB,
    )(dres, dh, x, g.reshape(1, D))


def _loss_head(x, g, target):
    T, D = x.shape
    tc = _pick(T, 256, 8)

    def body(x_ref, g_ref, t_ref, dx_ref, dxb_ref, dg_ref, se_ref):
        @pl.when(pl.program_id(0) == 0)
        def _():
            dg_ref[...] = jnp.zeros_like(dg_ref)
            se_ref[...] = jnp.zeros_like(se_ref)

        xv = x_ref[...]
        r = lax.rsqrt(jnp.mean(xv * xv, axis=-1, keepdims=True) + EPS)
        gv = g_ref[...]
        err = xv * r * gv - t_ref[...]
        se_ref[...] += _fold8(err * err)
        dy = err * (1.0 / D)
        dg_ref[...] += _fold8(dy * xv * r)
        dyg = dy * gv
        dx = r * dyg - xv * (r * r * r) * jnp.mean(dyg * xv, axis=-1, keepdims=True)
        dx_ref[...] = dx
        dxb_ref[...] = dx.astype(BF16)

    row = pl.BlockSpec((tc, D), lambda i: (i, 0))
    return pl.pallas_call(
        body, name="loss_head", grid=(T // tc,), in_specs=[row, _full((1, D)), row],
        out_specs=[row, row, _full((8, D)), _full((8, D))],
        out_shape=[jax.ShapeDtypeStruct((T, D), F32), jax.ShapeDtypeStruct((T, D), BF16),
                   jax.ShapeDtypeStruct((8, D), F32), jax.ShapeDtypeStruct((8, D), F32)],
        compiler_params=ARB,
    )(x, g.reshape(1, D), target)


def _s5_scan_consts(ar, ai, reverse):
    if reverse:
        ai = -ai
    pr, pi_ = [ar], [ai]
    for _ in range(7):
        pr, pi_ = pr + [pr[-1] * ar - pi_[-1] * ai], pi_ + [pr[-1] * ai + pi_[-1] * ar]
    rows = jnp.arange(8)[:, None]
    planes = []
    for d in (1, 2, 4):
        mask = (rows <= 7 - d) if reverse else (rows >= d)
        planes += [jnp.where(mask, pr[d - 1][None, :], 0.0), jnp.where(mask, pi_[d - 1][None, :], 0.0)]
    order = list(range(7, -1, -1)) if reverse else list(range(8))
    planes += [jnp.stack([pr[k] for k in order]), jnp.stack([pi_[k] for k in order])]
    return jnp.stack(planes).astype(F32)


def _s5_block_scan(xr_ref, xi_ref, cst_ref, car_ref, nblk, reverse, lb, extra=None):
    ns = xr_ref.shape[1]
    crow = 0 if reverse else 7

    def blk(i, _):
        ib = (nblk - 1 - i) if reverse else i
        r0 = pl.multiple_of(ib * 8, 8)
        for k in range(ns // lb):
            sl = slice(k * lb, (k + 1) * lb)
            xr, xi = xr_ref[pl.ds(r0, 8), sl], xi_ref[pl.ds(r0, 8), sl]
            for n, d in enumerate((1, 2, 4)):
                ar, ai = cst_ref[2 * n, :, sl], cst_ref[2 * n + 1, :, sl]
                sh = (8 - d) if reverse else d
                rr, ri = pltpu.roll(xr, sh, 0), pltpu.roll(xi, sh, 0)
                xr, xi = xr + ar * rr - ai * ri, xi + ar * ri + ai * rr
            pr, pi_ = cst_ref[6, :, sl], cst_ref[7, :, sl]
            cr, ci = car_ref[0, crow:crow + 1, sl], car_ref[1, crow:crow + 1, sl]
            xr, xi = xr + pr * cr - pi_ * ci, xi + pr * ci + pi_ * cr
            xr_ref[pl.ds(r0, 8), sl] = xr
            xi_ref[pl.ds(r0, 8), sl] = xi
            if extra is not None:
                extra(r0, sl, xr, xi, cr, ci)
            car_ref[0, :, sl] = xr
            car_ref[1, :, sl] = xi
        return 0

    lax.fori_loop(0, nblk, blk, 0)


def _s5_fwd(proj, col, bre, bim, cre, cim, cst, dvec, wglu, bglu):
    T = proj.shape[0]
    nsb, cw, sw = bre.shape
    M, NS = nsb * cw, nsb * sw
    tc = _pick(T, 256, 8)

    def body(u_ref, bre_ref, bim_ref, cre_ref, cim_ref, cst_ref, d_ref, wg_ref, bg_ref,
             y_ref, sre_ref, sim_ref, yp_ref, z_ref, car_ref):
        @pl.when(pl.program_id(0) == 0)
        def _():
            car_ref[...] = jnp.zeros_like(car_ref)

        u = u_ref[...]
        ub = u.astype(BF16)
        for k in range(nsb):
            sre_ref[:, k * sw:(k + 1) * sw] = _dot(ub[:, k * cw:(k + 1) * cw], bre_ref[k])
            sim_ref[:, k * sw:(k + 1) * sw] = _dot(ub[:, k * cw:(k + 1) * cw], bim_ref[k])
        _s5_block_scan(sre_ref, sim_ref, cst_ref, car_ref, tc // 8, False, sw)
        for k in range(nsb):
            yk = _dot(sre_ref[:, k * sw:(k + 1) * sw], cre_ref[k]) - _dot(sim_ref[:, k * sw:(k + 1) * sw], cim_ref[k])
            yp_ref[:, k * cw:(k + 1) * cw] = yk + d_ref[:, k * cw:(k + 1) * cw] * u[:, k * cw:(k + 1) * cw]
        gl = _gelu(yp_ref[...])
        z = _dot(gl, wg_ref[...]) + bg_ref[...]
        z_ref[...] = z
        y_ref[...] = (gl * _sigmoid(z)).astype(BF16)

    rowm = pl.BlockSpec((tc, M), lambda i: (i, 0))
    rows = pl.BlockSpec((tc, NS), lambda i: (i, 0))
    return pl.pallas_call(
        body, name="s5_fwd", grid=(T // tc,),
        in_specs=[pl.BlockSpec((tc, M), lambda i: (i, col)), _full(bre.shape), _full(bim.shape), _full(cre.shape),
                  _full(cim.shape), _full(cst.shape), _full((1, M)), _full((M, M)), _full((1, M))],
        out_specs=[rowm, rows, rows, rowm, rowm],
        out_shape=[jax.ShapeDtypeStruct((T, M), BF16), jax.ShapeDtypeStruct((T, NS), F32), jax.ShapeDtypeStruct((T, NS), F32),
                   jax.ShapeDtypeStruct((T, M), F32), jax.ShapeDtypeStruct((T, M), F32)],
        scratch_shapes=[pltpu.VMEM((2, 8, NS), F32)], compiler_params=ARB,
    )(proj, bre, bim, cre, cim, cst, dvec.reshape(1, M), wglu, bglu.reshape(1, M))


def _s5_bwd(dmix, dcol, proj, col, sre, sim, yp, z, bre, bim, cre, cim, cst, dvec, wglu):
    T = proj.shape[0]
    nsb, cw, sw = bre.shape
    M, NS = nsb * cw, nsb * sw
    tc = _pick(T, 256, 8)
    nc = T // tc

    def body(do_ref, u_ref, sre_ref, sim_ref, yp_ref, z_ref, bre_ref, bim_ref, cre_ref, cim_ref, cst_ref, d_ref, wg_ref,
             du_ref, dbre_ref, dbim_ref, dcre_ref, dcim_ref, da_ref, dwg_ref, vec_ref, gre_ref, gim_ref, car_ref):
        @pl.when(pl.program_id(0) == 0)
        def _():
            for r in (dbre_ref, dbim_ref, dcre_ref, dcim_ref, da_ref, dwg_ref, vec_ref, car_ref):
                r[...] = jnp.zeros_like(r)

        u, do, ypv = u_ref[...], do_ref[...], yp_ref[...]
        gl = _gelu(ypv)
        sg = _sigmoid(z_ref[...])
        dz = do * gl * sg * (1.0 - sg)
        dgl = do * sg + _dot_nt(dz, wg_ref[...])
        dwg_ref[...] += _dot_tn(gl, dz)
        vec_ref[0] += _fold8(dz)
        dy = dgl * _gelu_grad(ypv)
        vec_ref[1] += _fold8(dy * u)
        dyb = dy.astype(BF16)
        for k in range(nsb):
            cs, ss = slice(k * cw, (k + 1) * cw), slice(k * sw, (k + 1) * sw)
            gre_ref[:, ss] = _dot_nt(dyb[:, cs], cre_ref[k])
            gim_ref[:, ss] = -_dot_nt(dyb[:, cs], cim_ref[k])
            dcre_ref[k] += _dot_tn(sre_ref[:, ss], dyb[:, cs])
            dcim_ref[k] -= _dot_tn(sim_ref[:, ss], dyb[:, cs])

        row = lax.broadcasted_iota(jnp.int32, (8, sw), 0)

        def extra(r0, sl, gr, gi, cr, ci):
            nr = jnp.where(row == 7, cr, pltpu.roll(gr, 7, 0))
            ni = jnp.where(row == 7, ci, pltpu.roll(gi, 7, 0))
            sr, si = sre_ref[pl.ds(r0, 8), sl], sim_ref[pl.ds(r0, 8), sl]
            da_ref[0, :, sl] += nr * sr + ni * si
            da_ref[1, :, sl] += ni * sr - nr * si

        _s5_block_scan(gre_ref, gim_ref, cst_ref, car_ref, tc // 8, True, sw, extra)
        ub = u.astype(BF16)
        for k in range(nsb):
            cs, ss = slice(k * cw, (k + 1) * cw), slice(k * sw, (k + 1) * sw)
            gr, gi = gre_ref[:, ss].astype(BF16), gim_ref[:, ss].astype(BF16)
            duk = dy[:, cs] * d_ref[:, cs] + _dot_nt(gr, bre_ref[k]) + _dot_nt(gi, bim_ref[k])
            du_ref[:, cs] = duk.astype(BF16)
            dbre_ref[k] += _dot_tn(ub[:, cs], gr)
            dbim_ref[k] += _dot_tn(ub[:, cs], gi)

    def rowm(c):
        return pl.BlockSpec((tc, M), lambda i: (nc - 1 - i, c))
    rows = pl.BlockSpec((tc, NS), lambda i: (nc - 1 - i, 0))
    return pl.pallas_call(
        body, name="s5_bwd", grid=(nc,),
        in_specs=[rowm(dcol), rowm(col), rows, rows, rowm(0), rowm(0), _full(bre.shape), _full(bim.shape), _full(cre.shape),
                  _full(cim.shape), _full(cst.shape), _full((1, M)), _full((M, M))],
        out_specs=[rowm(0), _full(bre.shape), _full(bim.shape), _full(cre.shape), _full(cim.shape), _full((2, 8, NS)),
                   _full((M, M)), _full((3, 8, M))],
        out_shape=[jax.ShapeDtypeStruct((T, M), BF16), jax.ShapeDtypeStruct(bre.shape, F32), jax.ShapeDtypeStruct(bim.shape, F32),
                   jax.ShapeDtypeStruct(cre.shape, F32), jax.ShapeDtypeStruct(cim.shape, F32), jax.ShapeDtypeStruct((2, 8, NS), F32),
                   jax.ShapeDtypeStruct((M, M), F32), jax.ShapeDtypeStruct((3, 8, M), F32)],
        scratch_shapes=[pltpu.VMEM((tc, NS), F32), pltpu.VMEM((tc, NS), F32), pltpu.VMEM((2, 8, NS), F32)],
        compiler_params=ARB,
    )(dmix, proj, sre, sim, yp, z, bre, bim, cre, cim, cst, dvec.reshape(1, M), wglu)


def _ln_silu(h1, lg, lb):
    mu = jnp.mean(h1, axis=-1, keepdims=True)
    xc = h1 - mu
    rstd = lax.rsqrt(jnp.mean(xc * xc, axis=-1, keepdims=True) + EPS)
    xhat = xc * rstd
    h2 = xhat * lg + lb
    return xhat, rstd, h2, _sigmoid(h2)


def _shifted_copies(buf_ref, sh_ref, cs, n):
    for s in range(1, 8):
        sh_ref[s, 0:n, :] = buf_ref[pl.ds(s, n), cs]


def _tap(buf_ref, sh_ref, cs, off, r0, rows):
    q, s = divmod(off, 8)
    return buf_ref[pl.ds(r0 + 8 * q, rows), cs] if s == 0 else sh_ref[s, pl.ds(r0 + 8 * q, rows), :]


def _cv_fwd(proj, vcol, gcol, wdw, bdw, lg, lb, wpw, bpw):
    T = proj.shape[0]
    M = wpw.shape[0]
    tc = _pick(T, 256, 8)
    rb = _pick(tc, 64, 8)
    H, K = CONV_HALO, 31

    def body(v_ref, g_ref, wdw_ref, bdw_ref, lg_ref, lb_ref, wpw_ref, bpw_ref, y_ref, h1_ref, buf_ref, sh_ref):
        @pl.when(pl.program_id(0) == 0)
        def _():
            buf_ref[0:H, :] = jnp.zeros((H, M), F32)

        buf_ref[H:H + tc, :] = v_ref[...] * _sigmoid(g_ref[...])

        def strip(s, _):
            cs = pl.ds(pl.multiple_of(s * LANES, LANES), LANES)
            _shifted_copies(buf_ref, sh_ref, cs, tc + H - 8)
            for r0 in range(0, tc, rb):
                acc = jnp.zeros((rb, LANES), F32) + bdw_ref[:, cs]
                for k in range(K):
                    acc = acc + wdw_ref[k:k + 1, cs] * _tap(buf_ref, sh_ref, cs, H - (K - 1) + k, r0, rb)
                h1_ref[pl.ds(r0, rb), cs] = acc
            return 0

        lax.fori_loop(0, M // LANES, strip, 0)
        _, _, h2, sg = _ln_silu(h1_ref[...], lg_ref[...], lb_ref[...])
        y_ref[...] = (_dot(h2 * sg, wpw_ref[...]) + bpw_ref[...]).astype(BF16)
        buf_ref[0:H, :] = buf_ref[tc:tc + H, :]

    def rowm(c):
        return pl.BlockSpec((tc, M), lambda i: (i, c))
    vec = _full((1, M))
    return pl.pallas_call(
        body, name="cv_fwd", grid=(T // tc,),
        in_specs=[rowm(vcol), rowm(gcol), _full((32, M)), vec, vec, vec, _full((M, M)), vec],
        out_specs=[rowm(0), rowm(0)],
        out_shape=[jax.ShapeDtypeStruct((T, M), BF16), jax.ShapeDtypeStruct((T, M), F32)],
        scratch_shapes=[pltpu.VMEM((tc + H, M), F32), pltpu.VMEM((8, tc + H, LANES), F32)], compiler_params=ARB,
    )(proj, proj, wdw, bdw.reshape(1, M), lg.reshape(1, M), lb.reshape(1, M), wpw, bpw.reshape(1, M))


def _cv_bwd(dmix, dcol, proj, vcol, gcol, h1, wdw, lg, lb, wpw):
    T = proj.shape[0]
    M = wpw.shape[0]
    tc = _pick(T, 256, 8)
    nc = T // tc
    rb = _pick(tc, 32, 16)
    H, K = CONV_HALO, 31
    hb = tc // H

    def body(do_ref, v_ref, g_ref, vh_ref, gh_ref, h1_ref, wdw_ref, lg_ref, lb_ref, wpw_ref,
             dv_ref, dg_ref, dwpw_ref, dwdw_ref, vec_ref, hbuf_ref, dbuf_ref, hsh_ref, dsh_ref):
        i = pl.program_id(0)

        @pl.when(i == 0)
        def _():
            for r in (dwpw_ref, dwdw_ref, vec_ref):
                r[...] = jnp.zeros_like(r)
            dbuf_ref[tc:tc + H, :] = jnp.zeros((H, M), F32)

        halo = vh_ref[...] * _sigmoid(gh_ref[...])
        hbuf_ref[0:H, :] = jnp.where(i == nc - 1, 0.0, halo)
        hbuf_ref[H:H + tc, :] = v_ref[...] * _sigmoid(g_ref[...])
        do = do_ref[...]
        xhat, rstd, h2, sg = _ln_silu(h1_ref[...], lg_ref[...], lb_ref[...])
        dwpw_ref[...] += _dot_tn(h2 * sg, do)
        vec_ref[0] += _fold8(do)
        dh2 = _dot_nt(do, wpw_ref[...]) * (sg * (1.0 + h2 * (1.0 - sg)))
        vec_ref[1] += _fold8(dh2 * xhat)
        vec_ref[2] += _fold8(dh2)
        dxh = dh2 * lg_ref[...]
        dh1 = rstd * (dxh - jnp.mean(dxh, axis=-1, keepdims=True) - xhat * jnp.mean(dxh * xhat, axis=-1, keepdims=True))
        vec_ref[3] += _fold8(dh1)
        dbuf_ref[0:tc, :] = dh1

        def strip(s, _):
            cs = pl.ds(pl.multiple_of(s * LANES, LANES), LANES)
            _shifted_copies(dbuf_ref, dsh_ref, cs, tc + H - 8)
            _shifted_copies(hbuf_ref, hsh_ref, cs, tc + H - 8)
            for r0 in range(0, tc, rb):
                d1 = dbuf_ref[pl.ds(r0, rb), cs]
                dh0 = jnp.zeros((rb, LANES), F32)
                for k in range(K):
                    dh0 = dh0 + wdw_ref[k:k + 1, cs] * _tap(dbuf_ref, dsh_ref, cs, K - 1 - k, r0, rb)
                    dwdw_ref[k, :, cs] += _fold8(d1 * _tap(hbuf_ref, hsh_ref, cs, H - (K - 1) + k, r0, rb))
                sg0 = _sigmoid(g_ref[pl.ds(r0, rb), cs])
                dv_ref[pl.ds(r0, rb), cs] = (dh0 * sg0).astype(BF16)
                dg_ref[pl.ds(r0, rb), cs] = (dh0 * v_ref[pl.ds(r0, rb), cs] * sg0 * (1.0 - sg0)).astype(BF16)
            return 0

        lax.fori_loop(0, M // LANES, strip, 0)
        dbuf_ref[tc:tc + H, :] = dbuf_ref[0:H, :]

    def rowm(c):
        return pl.BlockSpec((tc, M), lambda i: (nc - 1 - i, c))

    def halo(c):
        return pl.BlockSpec((H, M), lambda i: (jnp.maximum((nc - 1 - i) * hb - 1, 0), c))
    vec = _full((1, M))
    return pl.pallas_call(
        body, name="cv_bwd", grid=(nc,),
        in_specs=[rowm(dcol), rowm(vcol), rowm(gcol), halo(vcol), halo(gcol), rowm(0), _full((32, M)), vec, vec, _full((M, M))],
        out_specs=[rowm(0), rowm(0), _full((M, M)), _full((32, 8, M)), _full((5, 8, M))],
        out_shape=[jax.ShapeDtypeStruct((T, M), BF16), jax.ShapeDtypeStruct((T, M), BF16), jax.ShapeDtypeStruct((M, M), F32),
                   jax.ShapeDtypeStruct((32, 8, M), F32), jax.ShapeDtypeStruct((5, 8, M), F32)],
        scratch_shapes=[pltpu.VMEM((tc + H, M), F32), pltpu.VMEM((tc + H, M), F32), pltpu.VMEM((8, tc + H, LANES), F32),
                        pltpu.VMEM((8, tc + H, LANES), F32)], compiler_params=ARB,
    )(dmix, proj, proj, proj, proj, h1, wdw, lg.reshape(1, M), lb.reshape(1, M), wpw)


def _lru_gates(xc, r, ig, sp8):
    la = -r * sp8
    a = jnp.exp(la)
    mult = jnp.sqrt(_neg_expm1(2.0 * la))
    return a, mult, mult * (ig * xc)


def _lru_fwd(proj, xcol, gcol, wcv, bcv, wr, br, wi, bi, sp8):
    T = proj.shape[0]
    M = wr.shape[0]
    tc = _pick(T, 256, 8)
    H, K = SMALL_HALO, 4

    def body(x_ref, g_ref, wcv_ref, bcv_ref, wr_ref, br_ref, wi_ref, bi_ref, sp_ref,
             y_ref, xc_ref, r_ref, i_ref, hp_ref, buf_ref, a_ref, car_ref):
        @pl.when(pl.program_id(0) == 0)
        def _():
            buf_ref[0:H, :] = jnp.zeros((H, M), F32)
            car_ref[...] = jnp.zeros_like(car_ref)

        buf_ref[H:H + tc, :] = x_ref[...]
        xc = jnp.zeros((tc, M), F32) + bcv_ref[...]
        for k in range(K):
            xc = xc + wcv_ref[k:k + 1, :] * buf_ref[pl.ds(H - (K - 1) + k, tc), :]
        buf_ref[0:H, :] = buf_ref[tc:tc + H, :]
        r = _sigmoid(_dot(xc, wr_ref[...]) + br_ref[...])
        ig = _sigmoid(_dot(xc, wi_ref[...]) + bi_ref[...])
        xc_ref[...] = xc
        r_ref[...] = r
        i_ref[...] = ig
        a, _, bt = _lru_gates(xc, r, ig, sp_ref[...])
        a_ref[...] = a
        hp_ref[...] = bt
        row = lax.broadcasted_iota(jnp.int32, (8, M), 0)

        def blk(ib, _):
            r0 = pl.multiple_of(ib * 8, 8)
            av, bv = a_ref[pl.ds(r0, 8), :], hp_ref[pl.ds(r0, 8), :]
            for d in (1, 2, 4):
                m = row >= d
                bv = jnp.where(m, bv + av * pltpu.roll(bv, d, 0), bv)
                av = jnp.where(m, av * pltpu.roll(av, d, 0), av)
            cr = car_ref[7:8, :]
            h = bv + av * cr
            a_ref[pl.ds(r0, 8), :] = h
            hp_ref[pl.ds(r0, 8), :] = jnp.where(row == 0, cr, pltpu.roll(h, 1, 0))
            car_ref[...] = h
            return 0

        lax.fori_loop(0, tc // 8, blk, 0)
        y_ref[...] = (a_ref[...] * _gelu(g_ref[...])).astype(BF16)

    def rowm(c):
        return pl.BlockSpec((tc, M), lambda i: (i, c))
    vec = _full((1, M))
    return pl.pallas_call(
        body, name="lru_fwd", grid=(T // tc,),
        in_specs=[rowm(xcol), rowm(gcol), _full((8, M)), vec, _full((M, M)), vec, _full((M, M)), vec, vec],
        out_specs=[rowm(0)] * 5,
        out_shape=[jax.ShapeDtypeStruct((T, M), BF16)] + [jax.ShapeDtypeStruct((T, M), F32)] * 4,
        scratch_shapes=[pltpu.VMEM((tc + H, M), F32), pltpu.VMEM((tc, M), F32), pltpu.VMEM((8, M), F32)], compiler_params=ARB,
    )(proj, proj, wcv, bcv.reshape(1, M), wr, br.reshape(1, M), wi, bi.reshape(1, M), sp8.reshape(1, M))


def _lru_bwd(dmix, dcol, proj, xcol, gcol, xc, r, ig, hprev, wcv, wr, wi, sp8):
    T = proj.shape[0]
    M = wr.shape[0]
    tc = _pick(T, 256, 8)
    nc = T // tc
    H, K = SMALL_HALO, 4
    hb = tc // H

    def body(do_ref, x_ref, g_ref, xh_ref, xc_ref, r_ref, i_ref, hp_ref, wcv_ref, wr_ref, wi_ref, sp_ref,
             dx_ref, dg_ref, dwr_ref, dwi_ref, dwcv_ref, vec_ref, xbuf_ref, dbuf_ref, a_ref, gs_ref, car_ref):
        i = pl.program_id(0)

        @pl.when(i == 0)
        def _():
            for rf in (dwr_ref, dwi_ref, dwcv_ref, vec_ref, car_ref):
                rf[...] = jnp.zeros_like(rf)
            dbuf_ref[tc:tc + H, :] = jnp.zeros((H, M), F32)

        xcv, rv, iv, hpv, gv, do = xc_ref[...], r_ref[...], i_ref[...], hp_ref[...], g_ref[...], do_ref[...]
        sp = sp_ref[...]
        a, mult, bt = _lru_gates(xcv, rv, iv, sp)
        h = a * hpv + bt
        ge = _gelu(gv)
        dg_ref[...] = (do * h * _gelu_grad(gv)).astype(BF16)
        a_ref[...] = a
        gs_ref[...] = do * ge
        row = lax.broadcasted_iota(jnp.int32, (8, M), 0)

        def blk(n, _):
            r0 = pl.multiple_of((tc // 8 - 1 - n) * 8, 8)
            ablk = a_ref[pl.ds(r0, 8), :]
            av = jnp.where(row == 7, car_ref[0, 0:1, :], pltpu.roll(ablk, 7, 0))
            bv = gs_ref[pl.ds(r0, 8), :]
            for d in (1, 2, 4):
                m = row <= 7 - d
                bv = jnp.where(m, bv + av * pltpu.roll(bv, 8 - d, 0), bv)
                av = jnp.where(m, av * pltpu.roll(av, 8 - d, 0), av)
            gsv = bv + av * car_ref[1, 0:1, :]
            gs_ref[pl.ds(r0, 8), :] = gsv
            car_ref[0] = ablk
            car_ref[1] = gsv
            return 0

        lax.fori_loop(0, tc // 8, blk, 0)
        gs = gs_ref[...]
        dixc = gs * mult
        dla = gs * hpv * a - gs * (iv * xcv) * (a * a) / mult
        vec_ref[2] += _fold8(-dla * rv)
        dpr = (-dla * sp) * rv * (1.0 - rv)
        dpi = (dixc * xcv) * iv * (1.0 - iv)
        vec_ref[0] += _fold8(dpr)
        vec_ref[1] += _fold8(dpi)
        dwr_ref[...] += _dot_tn(xcv, dpr)
        dwi_ref[...] += _dot_tn(xcv, dpi)
        dxc = dixc * iv + _dot_nt(dpr, wr_ref[...]) + _dot_nt(dpi, wi_ref[...])
        vec_ref[3] += _fold8(dxc)
        xbuf_ref[0:H, :] = jnp.where(i == nc - 1, 0.0, xh_ref[...])
        xbuf_ref[H:H + tc, :] = x_ref[...]
        dbuf_ref[0:tc, :] = dxc
        dx = jnp.zeros((tc, M), F32)
        for k in range(K):
            dx = dx + wcv_ref[k:k + 1, :] * dbuf_ref[pl.ds(K - 1 - k, tc), :]
            dwcv_ref[k] += _fold8(dxc * xbuf_ref[pl.ds(H - (K - 1) + k, tc), :])
        dx_ref[...] = dx.astype(BF16)
        dbuf_ref[tc:tc + H, :] = dbuf_ref[0:H, :]

    def rowm(c):
        return pl.BlockSpec((tc, M), lambda i: (nc - 1 - i, c))
    halo = pl.BlockSpec((H, M), lambda i: (jnp.maximum((nc - 1 - i) * hb - 1, 0), xcol))
    vec = _full((1, M))
    return pl.pallas_call(
        body, name="lru_bwd", grid=(nc,),
        in_specs=[rowm(dcol), rowm(xcol), rowm(gcol), halo, rowm(0), rowm(0), rowm(0), rowm(0), _full((8, M)),
                  _full((M, M)), _full((M, M)), vec],
        out_specs=[rowm(0), rowm(0), _full((M, M)), _full((M, M)), _full((8, 8, M)), _full((4, 8, M))],
        out_shape=[jax.ShapeDtypeStruct((T, M), BF16), jax.ShapeDtypeStruct((T, M), BF16), jax.ShapeDtypeStruct((M, M), F32),
                   jax.ShapeDtypeStruct((M, M), F32), jax.ShapeDtypeStruct((8, 8, M), F32), jax.ShapeDtypeStruct((4, 8, M), F32)],
        scratch_shapes=[pltpu.VMEM((tc + H, M), F32), pltpu.VMEM((tc + H, M), F32), pltpu.VMEM((tc, M), F32),
                        pltpu.VMEM((tc, M), F32), pltpu.VMEM((2, 8, M), F32)],
        compiler_params=ARB,
    )(dmix, proj, proj, proj, xc, r, ig, hprev, wcv, wr, wi, sp8.reshape(1, M))


def _pool_diffs(buf_ref, x, t0, tc, gw):
    H = POOL_HALO
    pos = (lax.broadcasted_iota(jnp.int32, (tc, gw), 0) + t0 + 1).astype(F32)
    out, inv = [], []
    for gi, win in enumerate(POOL_WINDOWS):
        sl = slice(gi * gw, (gi + 1) * gw)
        s = x[:, sl]
        for j in range(1, win):
            s = s + buf_ref[pl.ds(H - j, tc), sl]
        ic = 1.0 / jnp.minimum(pos, float(win))
        out.append(s * ic - x[:, sl])
        inv.append(ic)
    return out, inv


def _pool_fwd(proj, col, pw, scale):
    T = proj.shape[0]
    ng, gw, _ = pw.shape
    M = ng * gw
    tc = _pick(T, 256, 8)
    H = POOL_HALO

    def body(x_ref, pw_ref, sc_ref, y_ref, buf_ref):
        @pl.when(pl.program_id(0) == 0)
        def _():
            buf_ref[0:H, :] = jnp.zeros((H, M), F32)

        x = x_ref[...]
        buf_ref[H:H + tc, :] = x
        diffs, _ = _pool_diffs(buf_ref, x, pl.program_id(0) * tc, tc, gw)
        for gi in range(ng):
            sl = slice(gi * gw, (gi + 1) * gw)
            y_ref[:, sl] = (_dot(diffs[gi], pw_ref[gi]) * sc_ref[:, sl]).astype(BF16)
        buf_ref[0:H, :] = buf_ref[tc:tc + H, :]

    return pl.pallas_call(
        body, name="pool_fwd", grid=(T // tc,),
        in_specs=[pl.BlockSpec((tc, M), lambda i: (i, col)), _full(pw.shape), _full((1, M))],
        out_specs=pl.BlockSpec((tc, M), lambda i: (i, 0)), out_shape=jax.ShapeDtypeStruct((T, M), BF16),
        scratch_shapes=[pltpu.VMEM((tc + H, M), F32)], compiler_params=ARB,
    )(proj, pw, scale.reshape(1, M))


def _pool_bwd(dmix, dcol, proj, col, pw, scale):
    T = proj.shape[0]
    ng, gw, _ = pw.shape
    M = ng * gw
    tc = _pick(T, 256, 8)
    nc = T // tc
    H = POOL_HALO
    hb = tc // H

    def body(do_ref, x_ref, xh_ref, pw_ref, sc_ref, dx_ref, dpw_ref, dsc_ref, buf_ref, ebuf_ref):
        i = pl.program_id(0)

        @pl.when(i == 0)
        def _():
            dpw_ref[...] = jnp.zeros_like(dpw_ref)
            dsc_ref[...] = jnp.zeros_like(dsc_ref)
            ebuf_ref[tc:tc + H, :] = jnp.zeros((H, M), F32)

        x, do = x_ref[...], do_ref[...]
        buf_ref[0:H, :] = jnp.where(i == nc - 1, 0.0, xh_ref[...])
        buf_ref[H:H + tc, :] = x
        diffs, inv = _pool_diffs(buf_ref, x, (nc - 1 - i) * tc, tc, gw)
        ddg = []
        for gi in range(ng):
            sl = slice(gi * gw, (gi + 1) * gw)
            dyy = do[:, sl] * sc_ref[:, sl]
            dsc_ref[:, sl] += _fold8(do[:, sl] * _dot(diffs[gi], pw_ref[gi]))
            dpw_ref[gi] += _dot_tn(diffs[gi], dyy)
            d = _dot_nt(dyy, pw_ref[gi])
            ddg.append(d)
            ebuf_ref[0:tc, sl] = d * inv[gi]
        for gi, win in enumerate(POOL_WINDOWS):
            sl = slice(gi * gw, (gi + 1) * gw)
            s = -ddg[gi]
            for j in range(win):
                s = s + ebuf_ref[pl.ds(j, tc), sl]
            dx_ref[:, sl] = s.astype(BF16)
        ebuf_ref[tc:tc + H, :] = ebuf_ref[0:H, :]

    return pl.pallas_call(
        body, name="pool_bwd", grid=(nc,),
        in_specs=[pl.BlockSpec((tc, M), lambda i: (nc - 1 - i, dcol)), pl.BlockSpec((tc, M), lambda i: (nc - 1 - i, col)),
                  pl.BlockSpec((H, M), lambda i: (jnp.maximum((nc - 1 - i) * hb - 1, 0), col)), _full(pw.shape), _full((1, M))],
        out_specs=[pl.BlockSpec((tc, M), lambda i: (nc - 1 - i, 0)), _full(pw.shape), _full((8, M))],
        out_shape=[jax.ShapeDtypeStruct((T, M), BF16), jax.ShapeDtypeStruct(pw.shape, F32), jax.ShapeDtypeStruct((8, M), F32)],
        scratch_shapes=[pltpu.VMEM((tc + H, M), F32), pltpu.VMEM((tc + H, M), F32)], compiler_params=ARB,
    )(dmix, proj, proj, pw, scale.reshape(1, M))


def _ffn_mid_fwd(up, wdw, bdw):
    T, F2 = up.shape
    F = F2 // 2
    tc = _pick(T, 256, 8)
    rb = _pick(tc, 64, 16)
    H, K = FFN_HALO, 3

    def body(gt_ref, val_ref, w_ref, b_ref, act_ref, buf_ref):
        @pl.when(pl.program_id(0) == 0)
        def _():
            buf_ref[0:H, :] = jnp.zeros((H, F), F32)

        buf_ref[H:H + tc, :] = gt_ref[...].astype(F32)

        def strip(s, _):
            cs = pl.ds(pl.multiple_of(s * LANES, LANES), LANES)
            taps = [w_ref[k:k + 1, cs] for k in range(K)]
            for r0 in range(0, tc, rb):
                gc = b_ref[:, cs] + taps[0] * buf_ref[pl.ds(H - (K - 1) + r0, rb), cs]
                for k in range(1, K):
                    gc = gc + taps[k] * buf_ref[pl.ds(H - (K - 1) + k + r0, rb), cs]
                act_ref[pl.ds(r0, rb), cs] = (_gelu(gc) * val_ref[pl.ds(r0, rb), cs].astype(F32)).astype(BF16)
            return 0

        lax.fori_loop(0, F // LANES, strip, 0)
        buf_ref[0:H, :] = buf_ref[tc:tc + H, :]

    return pl.pallas_call(
        body, name="ffn_mid_fwd", grid=(T // tc,),
        in_specs=[pl.BlockSpec((tc, F), lambda i: (i, 0)), pl.BlockSpec((tc, F), lambda i: (i, 1)), _full((8, F)), _full((1, F))],
        out_specs=pl.BlockSpec((tc, F), lambda i: (i, 0)), out_shape=jax.ShapeDtypeStruct((T, F), BF16),
        scratch_shapes=[pltpu.VMEM((tc + H, F), F32)], compiler_params=ARB,
    )(up, up, wdw, bdw.reshape(1, F))


def _ffn_mid_bwd(dact, up, wdw, bdw):
    T, F2 = up.shape
    F = F2 // 2
    tc = _pick(T, 128, 8)
    rb = _pick(tc, 64, 16)
    nc = T // tc
    H, K = FFN_HALO, 3
    hb = tc // H

    def body(da_ref, gt_ref, val_ref, gh_ref, w_ref, b_ref, dup_ref, dw_ref, db_ref, gbuf_ref, dbuf_ref):
        i = pl.program_id(0)

        @pl.when(i == 0)
        def _():
            dw_ref[...] = jnp.zeros_like(dw_ref)
            db_ref[...] = jnp.zeros_like(db_ref)
            dbuf_ref[tc:tc + H, :] = jnp.zeros((H, F), F32)

        gbuf_ref[0:H, :] = jnp.where(i == nc - 1, 0.0, gh_ref[...].astype(F32))
        gbuf_ref[H:H + tc, :] = gt_ref[...].astype(F32)

        def strip(s, _):
            cs = pl.ds(pl.multiple_of(s * LANES, LANES), LANES)
            cs_val = pl.ds(pl.multiple_of(F + s * LANES, LANES), LANES)
            taps = [w_ref[k:k + 1, cs] for k in range(K)]
            for r0 in range(0, tc, rb):
                shifted = [gbuf_ref[pl.ds(H - (K - 1) + k + r0, rb), cs] for k in range(K)]
                gc = b_ref[:, cs] + taps[0] * shifted[0]
                for k in range(1, K):
                    gc = gc + taps[k] * shifted[k]
                ge, gg = _gelu_and_grad(gc)
                da = da_ref[pl.ds(r0, rb), cs].astype(F32)
                dup_ref[pl.ds(r0, rb), cs_val] = (da * ge).astype(BF16)
                dgc = da * val_ref[pl.ds(r0, rb), cs].astype(F32) * gg
                dbuf_ref[pl.ds(r0, rb), cs] = dgc
                db_ref[:, cs] += _fold8(dgc)
                for k in range(K):
                    dw_ref[k, :, cs] += _fold8(dgc * shifted[k])
            for r0 in range(0, tc, rb):
                dgt = taps[0] * dbuf_ref[pl.ds(K - 1 + r0, rb), cs]
                for k in range(1, K):
                    dgt = dgt + taps[k] * dbuf_ref[pl.ds(K - 1 - k + r0, rb), cs]
                dup_ref[pl.ds(r0, rb), cs] = dgt.astype(BF16)
            return 0

        lax.fori_loop(0, F // LANES, strip, 0)
        dbuf_ref[tc:tc + H, :] = dbuf_ref[0:H, :]

    return pl.pallas_call(
        body, name="ffn_mid_bwd", grid=(nc,),
        in_specs=[pl.BlockSpec((tc, F), lambda i: (nc - 1 - i, 0)), pl.BlockSpec((tc, F), lambda i: (nc - 1 - i, 0)),
                  pl.BlockSpec((tc, F), lambda i: (nc - 1 - i, 1)),
                  pl.BlockSpec((H, F), lambda i: (jnp.maximum((nc - 1 - i) * hb - 1, 0), 0)), _full((8, F)), _full((1, F))],
        out_specs=[pl.BlockSpec((tc, F2), lambda i: (nc - 1 - i, 0)), _full((8, 8, F)), _full((8, F))],
        out_shape=[jax.ShapeDtypeStruct((T, F2), BF16), jax.ShapeDtypeStruct((8, 8, F), F32), jax.ShapeDtypeStruct((8, F), F32)],
        scratch_shapes=[pltpu.VMEM((tc + H, F), F32), pltpu.VMEM((tc + H, F), F32)], compiler_params=ARB,
    )(dact, up, up, up, wdw, bdw.reshape(1, F))


def _my_pos():
    return lax.axis_index("x"), lax.axis_index("y"), lax.axis_index("c")


def _flip(pos, k):
    x, y, c = pos
    return ((1 - x) if k & 4 else x, (1 - y) if k & 2 else y, (1 - c) if k & 1 else c)


def _rank(pos):
    return 4 * pos[0] + 2 * pos[1] + pos[2]


def _exchange(xs, name, gather):
    n = len(xs)
    shapes = [((N_DEV,) + tuple(x.shape)) if gather else tuple(x.shape) for x in xs]

    def body(*refs):
        ins, outs = refs[:n], refs[n:2 * n]
        send_sems, recv_sems, local_sems = refs[2 * n:]
        me = _my_pos()
        mine = _rank(me)
        local, sends = [], []
        for a in range(n):
            loc = pltpu.make_async_copy(ins[a] if gather else ins[a].at[mine], outs[a].at[mine], local_sems.at[a])
            loc.start()
            local.append(loc)
            for k in range(1, N_DEV):
                peer = _flip(me, k)
                cp = pltpu.make_async_remote_copy(
                    src_ref=ins[a] if gather else ins[a].at[_rank(peer)], dst_ref=outs[a].at[mine],
                    send_sem=send_sems.at[a, k - 1], recv_sem=recv_sems.at[a, k - 1], device_id=peer,
                    device_id_type=pl.DeviceIdType.MESH)
                cp.start()
                sends.append(cp)
        for a in range(n):
            for k in range(1, N_DEV):
                peer = _flip(me, k)
                pltpu.make_async_remote_copy(
                    src_ref=ins[a] if gather else ins[a].at[_rank(peer)], dst_ref=outs[a].at[_rank(peer)],
                    send_sem=send_sems.at[a, k - 1], recv_sem=recv_sems.at[a, k - 1], device_id=peer,
                    device_id_type=pl.DeviceIdType.MESH).wait_recv()
        for cp in sends:
            cp.wait_send()
        for loc in local:
            loc.wait()

    any_spec = pl.BlockSpec(memory_space=pl.ANY)
    return pl.pallas_call(
        body, name=name, in_specs=[any_spec] * n, out_specs=[any_spec] * n,
        out_shape=[jax.ShapeDtypeStruct(s, x.dtype) for s, x in zip(shapes, xs)],
        scratch_shapes=[pltpu.SemaphoreType.DMA((n, N_DEV - 1)), pltpu.SemaphoreType.DMA((n, N_DEV - 1)),
                        pltpu.SemaphoreType.DMA((n,))],
    )(*xs)


_HBM = pl.BlockSpec(memory_space=pltpu.HBM)
_SEM = pl.BlockSpec(memory_space=pltpu.SEMAPHORE)
_EFFECT = pltpu.SideEffectType.DATAFLOW_SIDE_EFFECTING


def _plan_direct(gather):
    def plan(bufs, ssem, rsem, me, outgoing):
        n = len(bufs) // 2
        out = []
        for a in range(n):
            for k in range(1, N_DEV):
                peer = _flip(me, k)
                out.append(pltpu.make_async_remote_copy(
                    src_ref=bufs[a] if gather else bufs[a].at[_rank(peer)],
                    dst_ref=bufs[n + a].at[_rank(me) if outgoing else _rank(peer)],
                    send_sem=ssem.at[a * 7 + k - 1], recv_sem=rsem.at[a * 7 + k - 1], device_id=peer,
                    device_id_type=pl.DeviceIdType.MESH))
        return out
    return plan, 7


_SAME_CORE = (2, 4, 6)


def _plan_own_block(bufs, ssem, rsem, me, outgoing):
    n = len(bufs) // 2
    out = []
    for a in range(n):
        for j, k in enumerate((1,) + _SAME_CORE):
            peer = _flip(me, k)
            out.append(pltpu.make_async_remote_copy(
                src_ref=bufs[a], dst_ref=bufs[n + a].at[_rank(me) if outgoing else _rank(peer)],
                send_sem=ssem.at[a * 4 + j], recv_sem=rsem.at[a * 4 + j], device_id=peer, device_id_type=pl.DeviceIdType.MESH))
    return out


def _plan_pass_on(bufs, ssem, rsem, me, outgoing):
    sibling = _flip(me, 1)
    out = []
    for a in range(len(bufs)):
        for j, k in enumerate(_SAME_CORE):
            slot = _rank(_flip(me, k) if outgoing else _flip(sibling, k))
            out.append(pltpu.make_async_remote_copy(
                src_ref=bufs[a].at[slot], dst_ref=bufs[a].at[slot], send_sem=ssem.at[a * 3 + j], recv_sem=rsem.at[a * 3 + j],
                device_id=sibling, device_id_type=pl.DeviceIdType.MESH))
    return out


def _split_start(bufs, plan, n_sems, name, after=None):
    nb = len(bufs)
    extra = [] if after is None else [after]

    def body(*refs):
        ssem, rsem = refs[nb + len(extra)], refs[nb + len(extra) + 1]
        for cp in plan(refs[:nb], ssem, rsem, _my_pos(), True):
            cp.start()
        token = refs[2 * nb + len(extra) + 2]
        token[...] = jnp.zeros_like(token)

    res = pl.pallas_call(
        body, name=name,
        out_shape=(pltpu.SemaphoreType.DMA((n_sems,)), pltpu.SemaphoreType.DMA((n_sems,)),
                   *[pltpu.HBM(t.shape, t.dtype) for t in bufs], jax.ShapeDtypeStruct((8, 128), F32)),
        in_specs=[_HBM] * nb + [pl.BlockSpec(memory_space=pl.ANY)] * len(extra),
        out_specs=(_SEM, _SEM, *([_HBM] * nb), pl.BlockSpec(memory_space=pltpu.VMEM)),
        input_output_aliases={i: 2 + i for i in range(nb)},
        compiler_params=pltpu.CompilerParams(has_side_effects=_EFFECT),
    )(*[pltpu.with_memory_space_constraint(t, pltpu.HBM) for t in bufs], *extra)
    return (res[0], res[1], list(res[2:2 + nb])), res[2 + nb]


def _split_wait(handle, plan, after, name):
    ssem, rsem, bufs = handle
    nb = len(bufs)

    def body(*refs):
        for cp in plan(refs[:nb], refs[nb], refs[nb + 1], _my_pos(), False):
            cp.wait_send()
            cp.wait_recv()

    res = pl.pallas_call(
        body, name=name, out_shape=tuple(pltpu.HBM(t.shape, t.dtype) for t in bufs),
        in_specs=[_HBM] * nb + [_SEM, _SEM, pl.BlockSpec(memory_space=pl.ANY)], out_specs=tuple([_HBM] * nb),
        input_output_aliases={i: i for i in range(nb)},
        compiler_params=pltpu.CompilerParams(has_side_effects=_EFFECT),
    )(*bufs, ssem, rsem, after)
    return list(res)


def _landing_zones(xs, gather):
    mine = _rank(_my_pos())
    lands = []
    for x in xs:
        own = x if gather else lax.dynamic_index_in_dim(x, mine, 0, keepdims=False)
        lands.append(lax.dynamic_update_index_in_dim(lax.empty((N_DEV,) + tuple(own.shape), x.dtype), own, mine, 0))
    return lands


def _gather_start(xs, name, after=None, two_level=True):
    bufs = list(xs) + _landing_zones(xs, True)
    if two_level:
        handle, token = _split_start(bufs, _plan_own_block, 4 * len(xs), name + "_start", after)
    else:
        plan, per = _plan_direct(True)
        handle, token = _split_start(bufs, plan, per * len(xs), name + "_start", after)
    return (handle, two_level, name), token


def _gather_wait(state, after):
    handle, two_level, name = state
    n = len(handle[2]) // 2
    if not two_level:
        return _split_wait(handle, _plan_direct(True)[0], after, name + "_wait")[n:]
    lands = _split_wait(handle, _plan_own_block, after, name + "_wait")[n:]
    handle, _ = _split_start(lands, _plan_pass_on, 3 * n, name + "_pass")
    return _split_wait(handle, _plan_pass_on, after, name + "_passed")


def _scatter_start(xs, name):
    plan, per = _plan_direct(False)
    handle, token = _split_start(list(xs) + _landing_zones(xs, False), plan, per * len(xs), name + "_start")
    return (handle, name), token


def _scatter_wait(state, after):
    handle, name = state
    return _split_wait(handle, _plan_direct(False)[0], after, name + "_wait")[len(handle[2]) // 2:]


def _adamw_math(w, g, m, v):
    m = ADAM_B1 * m + (1.0 - ADAM_B1) * g
    v = ADAM_B2 * v + (1.0 - ADAM_B2) * (g * g)
    m_hat = m / (1.0 - ADAM_B1 ** ADAM_STEP)
    v_hat = v / (1.0 - ADAM_B2 ** ADAM_STEP)
    return -ADAM_LR * (m_hat / (jnp.sqrt(v_hat) + ADAM_EPS) + ADAM_WD * w), m, v


def _sum8(parts, *, name):
    _, R, C = parts.shape
    tr = _pick(R, 256, 16)

    def body(p_ref, o_ref):
        acc = p_ref[0].astype(F32)
        for r in range(1, N_DEV):
            acc = acc + p_ref[r].astype(F32)
        o_ref[...] = acc

    return pl.pallas_call(
        body, name=name, grid=(R // tr,), in_specs=[pl.BlockSpec((N_DEV, tr, C), lambda i: (0, i, 0))],
        out_specs=pl.BlockSpec((tr, C), lambda i: (i, 0)), out_shape=jax.ShapeDtypeStruct((R, C), F32),
        compiler_params=pltpu.CompilerParams(dimension_semantics=("parallel",)),
    )(parts)


def _adamw_stacked(gs, w, m, v, *, name):
    depth, R, C = w.shape
    tr = _pick(R, 64, 16)
    summed = gs[0].ndim == 3

    def body(*refs):
        g_refs, (w_ref, m_ref, v_ref), outs = refs[:depth], refs[depth:depth + 3], refs[depth + 3:]
        for l in range(depth):
            if summed:
                g = g_refs[l][0].astype(F32)
                for r in range(1, N_DEV):
                    g = g + g_refs[l][r].astype(F32)
            else:
                g = g_refs[l][...]
            d, nm, nv = _adamw_math(w_ref[l], g, m_ref[l], v_ref[l])
            for o, val in zip(outs, (g, d, nm, nv)):
                o[l] = val

    g_spec = pl.BlockSpec((N_DEV, tr, C), lambda i: (0, i, 0)) if summed else pl.BlockSpec((tr, C), lambda i: (i, 0))
    stk = pl.BlockSpec((depth, tr, C), lambda i: (0, i, 0))
    return pl.pallas_call(
        body, name=name, grid=(R // tr,), in_specs=[g_spec] * depth + [stk] * 3, out_specs=[stk] * 4,
        out_shape=[jax.ShapeDtypeStruct((depth, R, C), F32)] * 4,
        compiler_params=pltpu.CompilerParams(dimension_semantics=("parallel",)),
    )(*gs, w, m, v)


def _lane_view(shape):
    if len(shape) == 1:
        return (1, shape[0])
    size = math.prod(shape)
    if shape[-1] < 64 and size % LANES == 0:
        return (size // LANES, LANES)
    if len(shape) > 2 and shape[-2] % 8 == 0:
        return (size // shape[-1], shape[-1])
    return tuple(shape)


def _adamw_many(ws, gs, ms, vs, *, name):
    n = len(ws)
    views = [_lane_view(w.shape) for w in ws]

    def body(*refs):
        for i in range(n):
            d, nm, nv = _adamw_math(refs[i][...], refs[n + i][...], refs[2 * n + i][...], refs[3 * n + i][...])
            refs[4 * n + i][...] = d
            refs[5 * n + i][...] = nm
            refs[6 * n + i][...] = nv

    outs = pl.pallas_call(
        body, name=name, out_shape=[jax.ShapeDtypeStruct(s, F32) for s in views] * 3,
    )(*[a.reshape(s) for arrs in (ws, gs, ms, vs) for a, s in zip(arrs, views)])
    return [[o.reshape(w.shape) for o, w in zip(outs[q * n:(q + 1) * n], ws)] for q in range(3)]


def _pack(arrs, dtype=F32):
    flat = jnp.concatenate([a.reshape(-1).astype(dtype) for a in arrs])
    pad = (-flat.shape[0]) % 2048
    return jnp.pad(flat, (0, pad)).reshape(-1, 128)


def _unpack(buf, shapes):
    flat, out, off = buf.reshape(-1), [], 0
    for s in shapes:
        n = math.prod(s)
        out.append(flat[off:off + n].reshape(s))
        off += n
    return out


def _s5_params(lam_re, lam_im, log_step, b_re, b_im):
    G, P, H = b_re.shape
    step = jnp.exp(log_step)[:, None]
    mag = jnp.exp(lam_re * step)
    ar, ai = mag * jnp.cos(lam_im * step), mag * jnp.sin(lam_im * step)
    den = lam_re * lam_re + lam_im * lam_im
    qr = ((ar - 1.0) * lam_re + ai * lam_im) / den
    qi = (ai * lam_re - (ar - 1.0) * lam_im) / den
    bbr = qr[..., None] * b_re - qi[..., None] * b_im
    bbi = qr[..., None] * b_im + qi[..., None] * b_re
    return ar.reshape(-1), ai.reshape(-1), _s5_blocks_in(bbr), _s5_blocks_in(bbi)


def _s5_blocks_in(b):
    G, P, H = b.shape
    eye = jnp.eye(8, dtype=b.dtype)
    return jnp.einsum("kgph,gj->kghjp", b.reshape(G // 8, 8, P, H), eye).reshape(G // 8, 8 * H, 8 * P)


def _s5_blocks_in_diag(m, G, P, H):
    return jnp.einsum("kghgp->kgph", m.reshape(G // 8, 8, H, 8, P)).reshape(G, P, H)


def _s5_blocks_out(c):
    G, H, P = c.shape
    eye = jnp.eye(8, dtype=c.dtype)
    return jnp.einsum("kghp,gj->kgpjh", c.reshape(G // 8, 8, H, P), eye).reshape(G // 8, 8 * P, 8 * H)


def _s5_blocks_out_diag(m, G, H, P):
    return jnp.einsum("kgpgh->kghp", m.reshape(G // 8, 8, P, 8, H)).reshape(G, H, P)


def _head_blocks(w):
    nh, d, e = w.shape
    return jnp.einsum("hde,hk->hdke", w, jnp.eye(nh, dtype=w.dtype)).reshape(nh * d, nh * e)


def _head_blocks_diag(m, nh, d, e):
    return jnp.einsum("hdhe->hde", m.reshape(nh, d, nh, e))


def _pad_rows(w, rows):
    return jnp.pad(w, ((0, rows - w.shape[0]), (0, 0)))


def _unshard_cols(g):
    return jnp.transpose(g, (1, 0, 2)).reshape(g.shape[1], -1)


_NAMES = ['norm_mix_g', 'w_in', 's5_lam_re', 's5_lam_im', 's5_log_step', 's5_b_re', 's5_b_im', 's5_c_re', 's5_c_im', 's5_d',
          's5_w_glu', 's5_b_glu', 'cv_w_dw', 'cv_b_dw', 'cv_ln_g', 'cv_ln_b', 'cv_w_pw', 'cv_b_pw', 'lru_w_conv', 'lru_b_conv',
          'lru_w_r', 'lru_b_r', 'lru_w_i', 'lru_b_i', 'lru_lam', 'pool_w', 'pool_scale', 'w_out', 'norm_ffn_g', 'ffn_w_up',
          'ffn_w_dw', 'ffn_b_dw', 'ffn_w_down', 'norm_final_g']
_MATRIX = ('w_in', 'w_out', 'ffn_w_up', 'ffn_w_down', 's5_w_glu', 'cv_w_pw')
_TRANSPOSED = ('w_in', 'ffn_w_up')
_COLSHARD = ('cv_w_dw', 'lru_w_conv', 'ffn_w_dw')
_GROUPS = (('w_in', 's5_w_glu', 'cv_w_pw', 'w_out'), ('ffn_w_up', 'ffn_w_down'))
_SMALL_BF16 = ('s5_b_re', 's5_b_im', 's5_c_re', 's5_c_im', 'lru_w_r', 'lru_w_i', 'pool_w')
_SCATTER_GROUPS = (('w_out', 's5_w_glu', 'cv_w_pw'), ('w_in',), ('ffn_w_up', 'ffn_w_down'))


def kernel(x, norm_mix_g, w_in, s5_lam_re, s5_lam_im, s5_log_step, s5_b_re, s5_b_im, s5_c_re, s5_c_im, s5_d, s5_w_glu, s5_b_glu, cv_w_dw, cv_b_dw, cv_ln_g, cv_ln_b, cv_w_pw, cv_b_pw, lru_w_conv, lru_b_conv, lru_w_r, lru_b_r, lru_w_i, lru_b_i, lru_lam, pool_w, pool_scale, w_out, norm_ffn_g, ffn_w_up, ffn_w_dw, ffn_b_dw, ffn_w_down, norm_final_g, loss_target, m_norm_mix_g, m_w_in, m_s5_lam_re, m_s5_lam_im, m_s5_log_step, m_s5_b_re, m_s5_b_im, m_s5_c_re, m_s5_c_im, m_s5_d, m_s5_w_glu, m_s5_b_glu, m_cv_w_dw, m_cv_b_dw, m_cv_ln_g, m_cv_ln_b, m_cv_w_pw, m_cv_b_pw, m_lru_w_conv, m_lru_b_conv, m_lru_w_r, m_lru_b_r, m_lru_w_i, m_lru_b_i, m_lru_lam, m_pool_w, m_pool_scale, m_w_out, m_norm_ffn_g, m_ffn_w_up, m_ffn_w_dw, m_ffn_b_dw, m_ffn_w_down, m_norm_final_g, v_norm_mix_g, v_w_in, v_s5_lam_re, v_s5_lam_im, v_s5_log_step, v_s5_b_re, v_s5_b_im, v_s5_c_re, v_s5_c_im, v_s5_d, v_s5_w_glu, v_s5_b_glu, v_cv_w_dw, v_cv_b_dw, v_cv_ln_g, v_cv_ln_b, v_cv_w_pw, v_cv_b_pw, v_lru_w_conv, v_lru_b_conv, v_lru_w_r, v_lru_b_r, v_lru_w_i, v_lru_b_i, v_lru_lam, v_pool_w, v_pool_scale, v_w_out, v_norm_ffn_g, v_ffn_w_up, v_ffn_w_dw, v_ffn_b_dw, v_ffn_w_down, v_norm_final_g):
    args = (x, norm_mix_g, w_in, s5_lam_re, s5_lam_im, s5_log_step, s5_b_re, s5_b_im, s5_c_re, s5_c_im, s5_d, s5_w_glu, s5_b_glu, cv_w_dw, cv_b_dw, cv_ln_g, cv_ln_b, cv_w_pw, cv_b_pw, lru_w_conv, lru_b_conv, lru_w_r, lru_b_r, lru_w_i, lru_b_i, lru_lam, pool_w, pool_scale, w_out, norm_ffn_g, ffn_w_up, ffn_w_dw, ffn_b_dw, ffn_w_down, norm_final_g, loss_target, m_norm_mix_g, m_w_in, m_s5_lam_re, m_s5_lam_im, m_s5_log_step, m_s5_b_re, m_s5_b_im, m_s5_c_re, m_s5_c_im, m_s5_d, m_s5_w_glu, m_s5_b_glu, m_cv_w_dw, m_cv_b_dw, m_cv_ln_g, m_cv_ln_b, m_cv_w_pw, m_cv_b_pw, m_lru_w_conv, m_lru_b_conv, m_lru_w_r, m_lru_b_r, m_lru_w_i, m_lru_b_i, m_lru_lam, m_pool_w, m_pool_scale, m_w_out, m_norm_ffn_g, m_ffn_w_up, m_ffn_w_dw, m_ffn_b_dw, m_ffn_w_down, m_norm_final_g, v_norm_mix_g, v_w_in, v_s5_lam_re, v_s5_lam_im, v_s5_log_step, v_s5_b_re, v_s5_b_im, v_s5_c_re, v_s5_c_im, v_s5_d, v_s5_w_glu, v_s5_b_glu, v_cv_w_dw, v_cv_b_dw, v_cv_ln_g, v_cv_ln_b, v_cv_w_pw, v_cv_b_pw, v_lru_w_conv, v_lru_b_conv, v_lru_w_r, v_lru_b_r, v_lru_w_i, v_lru_b_i, v_lru_lam, v_pool_w, v_pool_scale, v_w_out, v_norm_ffn_g, v_ffn_w_up, v_ffn_w_dw, v_ffn_b_dw, v_ffn_w_down, v_norm_final_g)
    x, target = args[0], args[35]
    W = dict(zip(_NAMES, args[1:35]))
    Mo = dict(zip(_NAMES, args[36:70]))
    Vo = dict(zip(_NAMES, args[70:104]))
    x = x[0]
    target = target[0]
    T, D = x.shape
    depth = W['w_in'].shape[0]
    F = W['ffn_w_down'].shape[1] * N_DEV
    G, P, H = W['s5_b_re'].shape[1:]
    MW = G * H
    nh, hd, _ = W['lru_w_r'].shape[1:]
    me = _rank(_my_pos())

    gather, full = {}, {}

    def start_group(l, gi, after):
        shards = [(W[n][l].T if n in _TRANSPOSED else W[n][l]).astype(BF16) for n in _GROUPS[gi]]
        if (l, gi) == (0, 0):
            shards.append(_pack([W[n][k] for k in range(depth) for n in _COLSHARD]))
        gather[l, gi], tok = _gather_start(shards, "gather_%d%d" % (l, gi), after)
        return tok

    def land_weights(l, gi, after):
        got = _gather_wait(gather[l, gi], after)
        for n, g8 in zip(_GROUPS[gi], got):
            full[n, l] = g8.reshape(-1, g8.shape[-1])
        return got

    started = start_group(0, 0, None)[0, 0]
    got = land_weights(0, 0, W['norm_mix_g'][0] + started)
    started = start_group(0, 1, got[0])[0, 0]
    taps8 = got[-1]
    tap_shapes = [W[n][l].shape for l in range(depth) for n in _COLSHARD]
    taps = [_unshard_cols(jnp.stack(t)) for t in zip(*[_unpack(taps8[r], tap_shapes) for r in range(N_DEV)])]
    for l in range(depth):
        for j, n in enumerate(_COLSHARD):
            full[n, l] = taps[l * len(_COLSHARD) + j]

    saved = []
    for l in range(depth):
        g_mix = W['norm_mix_g'][l] + (0.0 if l else started)
        if l:
            land_weights(l, 0, x)
        s5p, s5p_vjp = jax.vjp(_s5_params, W['s5_lam_re'][l], W['s5_lam_im'][l], W['s5_log_step'][l], W['s5_b_re'][l], W['s5_b_im'][l])
        ar, ai, bre, bim = s5p
        cre, cim = _s5_blocks_out(W['s5_c_re'][l]), _s5_blocks_out(W['s5_c_im'][l])
        sp8, sp8_vjp = jax.vjp(lambda lam: LRU_C * jax.nn.softplus(-lam), W['lru_lam'][l])
        wr, wi = _head_blocks(W['lru_w_r'][l]), _head_blocks(W['lru_w_i'][l])
        c = dict(bre=bre.astype(BF16), bim=bim.astype(BF16), cre=cre.astype(BF16), cim=cim.astype(BF16),
                 cst_f=_s5_scan_consts(ar, ai, False), cst_b=_s5_scan_consts(ar, ai, True), sp8=sp8, wr=wr.astype(BF16),
                 wi=wi.astype(BF16), cvw=_pad_rows(full['cv_w_dw', l], 32), lruw=_pad_rows(full['lru_w_conv', l], 8),
                 ffw=_pad_rows(full['ffn_w_dw', l], 8), pw=W['pool_w'][l].astype(BF16), s5p_vjp=s5p_vjp, sp8_vjp=sp8_vjp, x0=x)
        h = _rmsnorm(x, g_mix, name="mix_norm")
        proj = _matmul(h, full['w_in', l], mode="nt", tm=1024, tn=512, tk=D, out_dtype=F32, name="in_proj")
        y0, sre, sim, yp, z = _s5_fwd(proj, 0, c['bre'], c['bim'], c['cre'], c['cim'], c['cst_f'], W['s5_d'][l],
                                     full['s5_w_glu', l], W['s5_b_glu'][l])
        y1, h1 = _cv_fwd(proj, 1, 2, c['cvw'], W['cv_b_dw'][l], W['cv_ln_g'][l], W['cv_ln_b'][l], full['cv_w_pw', l], W['cv_b_pw'][l])
        y2, xc, rr, ig, hprev = _lru_fwd(proj, 3, 4, c['lruw'], W['lru_b_conv'][l], c['wr'], W['lru_b_r'][l], c['wi'],
                                         W['lru_b_i'][l], sp8)
        y3 = _pool_fwd(proj, 5, c['pw'], W['pool_scale'][l])
        mixed = jnp.concatenate([y0, y1, y2, y3], axis=1)
        x1 = _matmul(mixed, full['w_out', l], mode="nn", tm=1024, tn=512, tk=2048, out_dtype=F32, name="out_proj", add=x)
        got = land_weights(l, 1, x1)
        if l + 1 < depth:
            start_group(l + 1, 0, got[0])
            start_group(l + 1, 1, got[0])
        h2 = _rmsnorm(x1, W['norm_ffn_g'][l], name="ffn_norm")
        up = _matmul(h2, full['ffn_w_up', l], mode="nt", tm=2048, tn=256, tk=D, out_dtype=BF16, name="ffn_up")
        act = _ffn_mid_fwd(up, c['ffw'], W['ffn_b_dw'][l])
        x = _matmul(act, full['ffn_w_down', l], mode="nn", tm=1024, tn=512, tk=F, out_dtype=F32, name="ffn_down", add=x1)
        c.update(proj=proj, h=h, sre=sre, sim=sim, yp=yp, z=z, h1=h1, xc=xc, rr=rr, ig=ig, hprev=hprev, mixed=mixed, x1=x1,
                 up=up, h2=h2, act=act)
        saved.append(c)

    dx, dxb, dgf, se = _loss_head(x, W['norm_final_g'], target)
    loss = lax.psum(0.5 / D * jnp.sum(se), AXES)
    gsmall = {('norm_final_g', None): dgf.sum(0)}
    gmat, scatter = {}, {}

    def parts_of(m):
        return m.reshape(N_DEV, m.shape[0] // N_DEV, m.shape[1])

    for l in reversed(range(depth)):
        c = saved[l]
        dact = _matmul(dxb, full['ffn_w_down', l], mode="nt", tm=512, tn=F, tk=512, out_dtype=BF16, name="d_act")
        gmat['ffn_w_down', l] = _matmul(c['act'], dxb, mode="tn", tm=F, tn=512, tk=512, out_dtype=BF16, name="dw_down")
        dup, dffw, dffb = _ffn_mid_bwd(dact, c['up'], c['ffw'], W['ffn_b_dw'][l])
        gsmall['ffn_w_dw', l] = dffw.sum(1)[:3]
        gsmall['ffn_b_dw', l] = dffb.sum(0)
        dh2 = _matmul(dup, full['ffn_w_up', l], mode="nn", tm=1024, tn=512, tk=F, out_dtype=F32, name="d_h2")
        gmat['ffn_w_up', l] = _matmul(dup, c['h2'], mode="tn", tm=256, tn=D, tk=T, out_dtype=BF16, name="dw_up")
        scatter[l, 2], tok = _scatter_start([parts_of(gmat[n, l]) for n in _SCATTER_GROUPS[2]], "scatter_%d2" % l)
        dx1, dx1b, dg2 = _rms_bwd(dx, dh2, c['x1'], W['norm_ffn_g'][l] + tok[0, 0], name="ffn_norm_bwd")
        gsmall['norm_ffn_g', l] = dg2.sum(0)
        dmix = _matmul(dx1b, full['w_out', l], mode="nt", tm=1024, tn=512, tk=2048, out_dtype=F32, name="d_mixed")
        gmat['w_out', l] = _matmul(c['mixed'], dx1b, mode="tn", tm=512, tn=D, tk=T, out_dtype=BF16, name="dw_out")
        du, dbre, dbim, dcre, dcim, da, dwglu, v3 = _s5_bwd(dmix, 0, c['proj'], 0, c['sre'], c['sim'], c['yp'], c['z'], c['bre'],
                                                            c['bim'], c['cre'], c['cim'], c['cst_b'], W['s5_d'][l], full['s5_w_glu', l])
        da = da.sum(1)
        glr, gli, gls, gbr, gbi = c['s5p_vjp']((da[0], da[1], dbre, dbim))
        gsmall['s5_lam_re', l], gsmall['s5_lam_im', l], gsmall['s5_log_step', l] = glr, gli, gls
        gsmall['s5_b_re', l], gsmall['s5_b_im', l] = gbr, gbi
        gsmall['s5_c_re', l] = _s5_blocks_out_diag(dcre, G, H, P)
        gsmall['s5_c_im', l] = _s5_blocks_out_diag(dcim, G, H, P)
        gsmall['s5_b_glu', l], gsmall['s5_d', l] = v3[0].sum(0), v3[1].sum(0)
        gmat['s5_w_glu', l] = dwglu.astype(BF16)
        dv, dg, dwpw, dcvw, v5 = _cv_bwd(dmix, 1, c['proj'], 1, 2, c['h1'], c['cvw'], W['cv_ln_g'][l], W['cv_ln_b'][l], full['cv_w_pw', l])
        gmat['cv_w_pw', l] = dwpw.astype(BF16)
        gsmall['cv_w_dw', l] = dcvw.sum(1)[:31]
        gsmall['cv_b_pw', l], gsmall['cv_ln_g', l], gsmall['cv_ln_b', l], gsmall['cv_b_dw', l] = [v5[j].sum(0) for j in range(4)]
        scatter[l, 0], tok = _scatter_start([parts_of(gmat[n, l]) for n in _SCATTER_GROUPS[0]], "scatter_%d0" % l)
        tok = tok[0, 0]
        dlx, dlg, dwr, dwi, dlw, v4 = _lru_bwd(dmix, 2, c['proj'], 3, 4, c['xc'], c['rr'], c['ig'], c['hprev'], c['lruw'], c['wr'],
                                               c['wi'], c['sp8'] + tok)
        gsmall['lru_w_r', l], gsmall['lru_w_i', l] = _head_blocks_diag(dwr, nh, hd, hd), _head_blocks_diag(dwi, nh, hd, hd)
        gsmall['lru_w_conv', l] = dlw.sum(1)[:4]
        gsmall['lru_b_r', l], gsmall['lru_b_i', l] = v4[0].sum(0), v4[1].sum(0)
        gsmall['lru_lam', l] = c['sp8_vjp'](v4[2].sum(0))[0]
        gsmall['lru_b_conv', l] = v4[3].sum(0)
        dpx, dpw, dps = _pool_bwd(dmix, 3, c['proj'], 5, c['pw'], W['pool_scale'][l] + tok)
        gsmall['pool_w', l], gsmall['pool_scale', l] = dpw, dps.sum(0)
        dproj = jnp.concatenate([du, dv, dg, dlx, dlg, dpx], axis=1)
        dh = _matmul(dproj, full['w_in', l], mode="nn", tm=1024, tn=512, tk=dproj.shape[1], out_dtype=F32, name="d_h")
        gmat['w_in', l] = _matmul(dproj, c['h'], mode="tn", tm=512, tn=D, tk=T, out_dtype=BF16, name="dw_in")
        scatter[l, 1], tok = _scatter_start([parts_of(gmat[n, l]) for n in _SCATTER_GROUPS[1]], "scatter_%d1" % l)
        dx, dxb, dg1 = _rms_bwd(dx1, dh, c['x0'], W['norm_mix_g'][l] + tok[0, 0], name="mix_norm_bwd")
        gsmall['norm_mix_g', l] = dg1.sum(0)

    small_names = [n for n in _NAMES if n not in _MATRIX]
    small_split = ([n for n in small_names if n not in _SMALL_BF16], [n for n in small_names if n in _SMALL_BF16])
    small_packs, full_shapes = [], []
    for names, dtype in zip(small_split, (F32, BF16)):
        layers = [[None] if n == 'norm_final_g' else range(depth) for n in names]
        small_packs.append(_pack([gsmall[n, l] for n, ls in zip(names, layers) for l in ls], dtype))
        full_shapes.append([gsmall[n, None].shape if n == 'norm_final_g' else (depth,) + gsmall[n, 0].shape for n in names])
    small_state, tok = _gather_start(small_packs, "small_grads")
    parts = {}
    for l in reversed(range(depth)):
        for gi in (2, 0, 1):
            got = _scatter_wait(scatter[l, gi], tok)
            for n, p8 in zip(_SCATTER_GROUPS[gi], got):
                parts[n, l] = p8

    out = {}
    for n in _MATRIX:
        if n in _TRANSPOSED and W[n].shape[-1] % LANES:
            wt, mt, vt = (jnp.swapaxes(t, 1, 2) for t in (W[n], Mo[n], Vo[n]))
            res = _adamw_stacked([parts[n, l] for l in range(depth)], wt, mt, vt, name="adamw_" + n)
            out[n] = tuple(jnp.swapaxes(t, 1, 2) for t in res)
            continue
        if n in _TRANSPOSED:
            gl = [_sum8(parts[n, l], name="sum_" + n).T for l in range(depth)]
        else:
            gl = [parts[n, l] for l in range(depth)]
        out[n] = _adamw_stacked(gl, W[n], Mo[n], Vo[n], name="adamw_" + n)
    after_matrices = sum(out[n][1][0, 0, 0] for n in _MATRIX).reshape(1, 1)
    gs = {}
    for names, shapes, small8 in zip(small_split, full_shapes, _gather_wait(small_state, after_matrices)):
        gs.update(zip(names, _unpack(_sum8(small8, name="sum_small_" + str(small8.dtype)), shapes)))
    for n in _COLSHARD:
        cw = W[n].shape[-1]
        gs[n] = lax.dynamic_slice_in_dim(gs[n], me * cw, cw, axis=gs[n].ndim - 1)
    upd = _adamw_many([W[n] for n in small_names], [gs[n] for n in small_names], [Mo[n] for n in small_names],
                      [Vo[n] for n in small_names], name="adamw_small")
    for j, n in enumerate(small_names):
        out[n] = (gs[n], upd[0][j], upd[1][j], upd[2][j])
    return (loss, dx[None]) + tuple(out[n][q] for q in range(4) for n in _NAMES)
```

```python
import functools
import math

import jax
import jax.numpy as jnp
from jax import lax
from jax.experimental import pallas as pl
from jax.experimental.pallas import tpu as pltpu

F32 = jnp.float32
BF16 = jnp.bfloat16
N_DEV = 8
AXES = ("x", "y", "c")
EPS = 1e-6
S5_GROUP_CH = 16
S5_STATE = 64
LRU_HEADS = 8
LRU_C = 8.0
POOL_WINDOWS = (2, 4, 8, 16)
CONV_HALO = 32
SMALL_HALO = 8
FFN_HALO = 16
POOL_HALO = 16
LANES = 128
ADAM_LR, ADAM_B1, ADAM_B2, ADAM_EPS, ADAM_WD, ADAM_STEP = 0.001, 0.9, 0.999, 1e-08, 0.01, 10
_GELU_K = math.sqrt(2.0 / math.pi)
ARB = pltpu.CompilerParams(dimension_semantics=("arbitrary",))


def _pick(n, pref, mult):
    best = None
    for t in range(mult, min(n, pref) + 1, mult):
        if n % t == 0:
            best = t
    return best if best is not None else n


def _sigmoid(x):
    return 1.0 / (1.0 + jnp.exp(-x))


def _gelu(x):
    return 0.5 * x * (1.0 + jnp.tanh(_GELU_K * (x + 0.044715 * x * x * x)))


def _gelu_grad(x):
    t = jnp.tanh(_GELU_K * (x + 0.044715 * x * x * x))
    return 0.5 * (1.0 + t) + 0.5 * x * (1.0 - t * t) * _GELU_K * (1.0 + 3.0 * 0.044715 * x * x)


def _gelu_and_grad(x):
    t = jnp.tanh(_GELU_K * (x + 0.044715 * x * x * x))
    half = 0.5 * (1.0 + t)
    return x * half, half + 0.5 * x * (1.0 - t * t) * _GELU_K * (1.0 + 3.0 * 0.044715 * x * x)


def _neg_expm1(x):
    series = -x * (1.0 + x * (0.5 + x * (1.0 / 6.0 + x * (1.0 / 24.0 + x * (1.0 / 120.0 + x * (1.0 / 720.0))))))
    return jnp.where(jnp.abs(x) < 0.25, series, 1.0 - jnp.exp(x))


def _fold8(x):
    return x.reshape(x.shape[0] // 8, 8, x.shape[1]).sum(axis=0)


def _dot(a, b):
    return jnp.dot(a.astype(BF16), b.astype(BF16), preferred_element_type=F32)


def _dot_nt(a, b):
    return lax.dot_general(a.astype(BF16), b.astype(BF16), (((1,), (1,)), ((), ())), preferred_element_type=F32)


def _dot_tn(a, b):
    return lax.dot_general(a.astype(BF16), b.astype(BF16), (((0,), (0,)), ((), ())), preferred_element_type=F32)


def _full(shape):
    nd = len(shape)
    return pl.BlockSpec(shape, lambda *i: (0,) * nd)


def _matmul(a, b, *, mode, tm, tn, tk, out_dtype, name, add=None):
    if mode == "nn":
        (M, K), N = a.shape, b.shape[1]
    elif mode == "nt":
        (M, K), N = a.shape, b.shape[0]
    else:
        (K, M), N = a.shape, b.shape[1]
    tm, tn, tk = _pick(M, tm, 8), _pick(N, tn, 128), _pick(K, tk, 128)
    if mode == "tn":
        tm = _pick(M, tm, 128)
    nk = K // tk
    a_spec = {"nn": pl.BlockSpec((tm, tk), lambda i, j, k: (i, k)), "nt": pl.BlockSpec((tm, tk), lambda i, j, k: (i, k)),
              "tn": pl.BlockSpec((tk, tm), lambda i, j, k: (k, i))}[mode]
    b_spec = {"nn": pl.BlockSpec((tk, tn), lambda i, j, k: (k, j)), "nt": pl.BlockSpec((tn, tk), lambda i, j, k: (j, k)),
              "tn": pl.BlockSpec((tk, tn), lambda i, j, k: (k, j))}[mode]
    o_spec = pl.BlockSpec((tm, tn), lambda i, j, k: (i, j))
    dot = {"nn": _dot, "nt": _dot_nt, "tn": _dot_tn}[mode]
    has_add = add is not None

    def body(*refs):
        if has_add:
            a_ref, b_ref, add_ref, o_ref, acc_ref = refs
        else:
            a_ref, b_ref, o_ref, acc_ref = refs
        part = dot(a_ref[...], b_ref[...])
        if nk == 1:
            o_ref[...] = (part + add_ref[...] if has_add else part).astype(out_dtype)
            return
        k = pl.program_id(2)

        @pl.when(k == 0)
        def _():
            acc_ref[...] = jnp.zeros_like(acc_ref)

        acc_ref[...] += part

        @pl.when(k == nk - 1)
        def _():
            r = acc_ref[...]
            if has_add:
                r = r + add_ref[...]
            o_ref[...] = r.astype(out_dtype)

    ins = [a, b] + ([add] if has_add else [])
    specs = [a_spec, b_spec] + ([o_spec] if has_add else [])
    return pl.pallas_call(
        body, name=name, grid=(M // tm, N // tn, nk), in_specs=specs, out_specs=o_spec,
        out_shape=jax.ShapeDtypeStruct((M, N), out_dtype),
        scratch_shapes=[pltpu.VMEM((tm, tn) if nk > 1 else (8, LANES), F32)],
        compiler_params=pltpu.CompilerParams(dimension_semantics=("parallel", "parallel", "arbitrary")),
    )(*ins)


def _rmsnorm(x, g, *, name):
    T, D = x.shape
    tc = _pick(T, 256, 16)

    def body(x_ref, g_ref, h_ref):
        xv = x_ref[...]
        r = lax.rsqrt(jnp.mean(xv * xv, axis=-1, keepdims=True) + EPS)
        h_ref[...] = (xv * r * g_ref[...]).astype(BF16)

    row = pl.BlockSpec((tc, D), lambda i: (i, 0))
    return pl.pallas_call(
        body, name=name, grid=(T // tc,), in_specs=[row, _full((1, D))], out_specs=row,
        out_shape=jax.ShapeDtypeStruct((T, D), BF16), compiler_params=pltpu.CompilerParams(dimension_semantics=("parallel",)),
    )(x, g.reshape(1, D))


def _rms_bwd(dres, dh, x, g, *, name):
    T, D = x.shape
    tc = _pick(T, 256, 8)

    def body(dres_ref, dh_ref, x_ref, g_ref, dx_ref, dxb_ref, dg_ref):
        @pl.when(pl.program_id(0) == 0)
        def _():
            dg_ref[...] = jnp.zeros_like(dg_ref)

        xv, dhv = x_ref[...], dh_ref[...]
        r = lax.rsqrt(jnp.mean(xv * xv, axis=-1, keepdims=True) + EPS)
        dg_ref[...] += _fold8(dhv * xv * r)
        dyg = dhv * g_ref[...]
        dx = dres_ref[...] + r * dyg - xv * (r * r * r) * jnp.mean(dyg * xv, axis=-1, keepdims=True)
        dx_ref[...] = dx
        dxb_ref[...] = dx.astype(BF16)

    row = pl.BlockSpec((tc, D), lambda i: (i, 0))
    return pl.pallas_call(
        body, name=name, grid=(T // tc,), in_specs=[row, row, row, _full((1, D))],
        out_specs=[row, row, _full((8, D))],
        out_shape=[jax.ShapeDtypeStruct((T, D), F32), jax.ShapeDtypeStruct((T, D), BF16), jax.ShapeDtypeStruct((8, D), F32)],
        compiler_params=ARB,
    )(dres, dh, x, g.reshape(1, D))


def _loss_head(x, g, target):
    T, D = x.shape
    tc = _pick(T, 256, 8)

    def body(x_ref, g_ref, t_ref, dx_ref, dxb_ref, dg_ref, se_ref):
        @pl.when(pl.program_id(0) == 0)
        def _():
            dg_ref[...] = jnp.zeros_like(dg_ref)
            se_ref[...] = jnp.zeros_like(se_ref)

        xv = x_ref[...]
        r = lax.rsqrt(jnp.mean(xv * xv, axis=-1, keepdims=True) + EPS)
        gv = g_ref[...]
        err = xv * r * gv - t_ref[...]
        se_ref[...] += _fold8(err * err)
        dy = err * (1.0 / D)
        dg_ref[...] += _fold8(dy * xv * r)
        dyg = dy * gv
        dx = r * dyg - xv * (r * r * r) * jnp.mean(dyg * xv, axis=-1, keepdims=True)
        dx_ref[...] = dx
        dxb_ref[...] = dx.astype(BF16)

    row = pl.BlockSpec((tc, D), lambda i: (i, 0))
    return pl.pallas_call(
        body, name="loss_head", grid=(T // tc,), in_specs=[row, _full((1, D)), row],
        out_specs=[row, row, _full((8, D)), _full((8, D))],
        out_shape=[jax.ShapeDtypeStruct((T, D), F32), jax.ShapeDtypeStruct((T, D), BF16),
                   jax.ShapeDtypeStruct((8, D), F32), jax.ShapeDtypeStruct((8, D), F32)],
        compiler_params=ARB,
    )(x, g.reshape(1, D), target)


def _s5_scan_consts(ar, ai, reverse):
    if reverse:
        ai = -ai
    pr, pi_ = [ar], [ai]
    for _ in range(7):
        pr, pi_ = pr + [pr[-1] * ar - pi_[-1] * ai], pi_ + [pr[-1] * ai + pi_[-1] * ar]
    rows = jnp.arange(8)[:, None]
    planes = []
    for d in (1, 2, 4):
        mask = (rows <= 7 - d) if reverse else (rows >= d)
        planes += [jnp.where(mask, pr[d - 1][None, :], 0.0), jnp.where(mask, pi_[d - 1][None, :], 0.0)]
    order = list(range(7, -1, -1)) if reverse else list(range(8))
    planes += [jnp.stack([pr[k] for k in order]), jnp.stack([pi_[k] for k in order])]
    return jnp.stack(planes).astype(F32)


def _s5_block_scan(xr_ref, xi_ref, cst_ref, car_ref, nblk, reverse, lb, extra=None):
    ns = xr_ref.shape[1]
    crow = 0 if reverse else 7

    def blk(i, _):
        ib = (nblk - 1 - i) if reverse else i
        r0 = pl.multiple_of(ib * 8, 8)
        for k in range(ns // lb):
            sl = slice(k * lb, (k + 1) * lb)
            xr, xi = xr_ref[pl.ds(r0, 8), sl], xi_ref[pl.ds(r0, 8), sl]
            for n, d in enumerate((1, 2, 4)):
                ar, ai = cst_ref[2 * n, :, sl], cst_ref[2 * n + 1, :, sl]
                sh = (8 - d) if reverse else d
                rr, ri = pltpu.roll(xr, sh, 0), pltpu.roll(xi, sh, 0)
                xr, xi = xr + ar * rr - ai * ri, xi + ar * ri + ai * rr
            pr, pi_ = cst_ref[6, :, sl], cst_ref[7, :, sl]
            cr, ci = car_ref[0, crow:crow + 1, sl], car_ref[1, crow:crow + 1, sl]
            xr, xi = xr + pr * cr - pi_ * ci, xi + pr * ci + pi_ * cr
            xr_ref[pl.ds(r0, 8), sl] = xr
            xi_ref[pl.ds(r0, 8), sl] = xi
            if extra is not None:
                extra(r0, sl, xr, xi, cr, ci)
            car_ref[0, :, sl] = xr
            car_ref[1, :, sl] = xi
        return 0

    lax.fori_loop(0, nblk, blk, 0)


def _s5_fwd(proj, col, bre, bim, cre, cim, cst, dvec, wglu, bglu):
    T = proj.shape[0]
    nsb, cw, sw = bre.shape
    M, NS = nsb * cw, nsb * sw
    tc = _pick(T, 256, 8)

    def body(u_ref, bre_ref, bim_ref, cre_ref, cim_ref, cst_ref, d_ref, wg_ref, bg_ref,
             y_ref, sre_ref, sim_ref, yp_ref, z_ref, car_ref):
        @pl.when(pl.program_id(0) == 0)
        def _():
            car_ref[...] = jnp.zeros_like(car_ref)

        u = u_ref[...]
        ub = u.astype(BF16)
        for k in range(nsb):
            sre_ref[:, k * sw:(k + 1) * sw] = _dot(ub[:, k * cw:(k + 1) * cw], bre_ref[k])
            sim_ref[:, k * sw:(k + 1) * sw] = _dot(ub[:, k * cw:(k + 1) * cw], bim_ref[k])
        _s5_block_scan(sre_ref, sim_ref, cst_ref, car_ref, tc // 8, False, sw)
        for k in range(nsb):
            yk = _dot(sre_ref[:, k * sw:(k + 1) * sw], cre_ref[k]) - _dot(sim_ref[:, k * sw:(k + 1) * sw], cim_ref[k])
            yp_ref[:, k * cw:(k + 1) * cw] = yk + d_ref[:, k * cw:(k + 1) * cw] * u[:, k * cw:(k + 1) * cw]
        gl = _gelu(yp_ref[...])
        z = _dot(gl, wg_ref[...]) + bg_ref[...]
        z_ref[...] = z
        y_ref[...] = (gl * _sigmoid(z)).astype(BF16)

    rowm = pl.BlockSpec((tc, M), lambda i: (i, 0))
    rows = pl.BlockSpec((tc, NS), lambda i: (i, 0))
    return pl.pallas_call(
        body, name="s5_fwd", grid=(T // tc,),
        in_specs=[pl.BlockSpec((tc, M), lambda i: (i, col)), _full(bre.shape), _full(bim.shape), _full(cre.shape),
                  _full(cim.shape), _full(cst.shape), _full((1, M)), _full((M, M)), _full((1, M))],
        out_specs=[rowm, rows, rows, rowm, rowm],
        out_shape=[jax.ShapeDtypeStruct((T, M), BF16), jax.ShapeDtypeStruct((T, NS), F32), jax.ShapeDtypeStruct((T, NS), F32),
                   jax.ShapeDtypeStruct((T, M), F32), jax.ShapeDtypeStruct((T, M), F32)],
        scratch_shapes=[pltpu.VMEM((2, 8, NS), F32)], compiler_params=ARB,
    )(proj, bre, bim, cre, cim, cst, dvec.reshape(1, M), wglu, bglu.reshape(1, M))


def _s5_bwd(dmix, dcol, proj, col, sre, sim, yp, z, bre, bim, cre, cim, cst, dvec, wglu):
    T = proj.shape[0]
    nsb, cw, sw = bre.shape
    M, NS = nsb * cw, nsb * sw
    tc = _pick(T, 256, 8)
    nc = T // tc

    def body(do_ref, u_ref, sre_ref, sim_ref, yp_ref, z_ref, bre_ref, bim_ref, cre_ref, cim_ref, cst_ref, d_ref, wg_ref,
             du_ref, dbre_ref, dbim_ref, dcre_ref, dcim_ref, da_ref, dwg_ref, vec_ref, gre_ref, gim_ref, car_ref):
        @pl.when(pl.program_id(0) == 0)
        def _():
            for r in (dbre_ref, dbim_ref, dcre_ref, dcim_ref, da_ref, dwg_ref, vec_ref, car_ref):
                r[...] = jnp.zeros_like(r)

        u, do, ypv = u_ref[...], do_ref[...], yp_ref[...]
        gl = _gelu(ypv)
        sg = _sigmoid(z_ref[...])
        dz = do * gl * sg * (1.0 - sg)
        dgl = do * sg + _dot_nt(dz, wg_ref[...])
        dwg_ref[...] += _dot_tn(gl, dz)
        vec_ref[0] += _fold8(dz)
        dy = dgl * _gelu_grad(ypv)
        vec_ref[1] += _fold8(dy * u)
        dyb = dy.astype(BF16)
        for k in range(nsb):
            cs, ss = slice(k * cw, (k + 1) * cw), slice(k * sw, (k + 1) * sw)
            gre_ref[:, ss] = _dot_nt(dyb[:, cs], cre_ref[k])
            gim_ref[:, ss] = -_dot_nt(dyb[:, cs], cim_ref[k])
            dcre_ref[k] += _dot_tn(sre_ref[:, ss], dyb[:, cs])
            dcim_ref[k] -= _dot_tn(sim_ref[:, ss], dyb[:, cs])

        row = lax.broadcasted_iota(jnp.int32, (8, sw), 0)

        def extra(r0, sl, gr, gi, cr, ci):
            nr = jnp.where(row == 7, cr, pltpu.roll(gr, 7, 0))
            ni = jnp.where(row == 7, ci, pltpu.roll(gi, 7, 0))
            sr, si = sre_ref[pl.ds(r0, 8), sl], sim_ref[pl.ds(r0, 8), sl]
            da_ref[0, :, sl] += nr * sr + ni * si
            da_ref[1, :, sl] += ni * sr - nr * si

        _s5_block_scan(gre_ref, gim_ref, cst_ref, car_ref, tc // 8, True, sw, extra)
        ub = u.astype(BF16)
        for k in range(nsb):
            cs, ss = slice(k * cw, (k + 1) * cw), slice(k * sw, (k + 1) * sw)
            gr, gi = gre_ref[:, ss].astype(BF16), gim_ref[:, ss].astype(BF16)
            duk = dy[:, cs] * d_ref[:, cs] + _dot_nt(gr, bre_ref[k]) + _dot_nt(gi, bim_ref[k])
            du_ref[:, cs] = duk.astype(BF16)
            dbre_ref[k] += _dot_tn(ub[:, cs], gr)
            dbim_ref[k] += _dot_tn(ub[:, cs], gi)

    def rowm(c):
        return pl.BlockSpec((tc, M), lambda i: (nc - 1 - i, c))
    rows = pl.BlockSpec((tc, NS), lambda i: (nc - 1 - i, 0))
    return pl.pallas_call(
        body, name="s5_bwd", grid=(nc,),
        in_specs=[rowm(dcol), rowm(col), rows, rows, rowm(0), rowm(0), _full(bre.shape), _full(bim.shape), _full(cre.shape),
                  _full(cim.shape), _full(cst.shape), _full((1, M)), _full((M, M))],
        out_specs=[rowm(0), _full(bre.shape), _full(bim.shape), _full(cre.shape), _full(cim.shape), _full((2, 8, NS)),
                   _full((M, M)), _full((3, 8, M))],
        out_shape=[jax.ShapeDtypeStruct((T, M), BF16), jax.ShapeDtypeStruct(bre.shape, F32), jax.ShapeDtypeStruct(bim.shape, F32),
                   jax.ShapeDtypeStruct(cre.shape, F32), jax.ShapeDtypeStruct(cim.shape, F32), jax.ShapeDtypeStruct((2, 8, NS), F32),
                   jax.ShapeDtypeStruct((M, M), F32), jax.ShapeDtypeStruct((3, 8, M), F32)],
        scratch_shapes=[pltpu.VMEM((tc, NS), F32), pltpu.VMEM((tc, NS), F32), pltpu.VMEM((2, 8, NS), F32)],
        compiler_params=ARB,
    )(dmix, proj, sre, sim, yp, z, bre, bim, cre, cim, cst, dvec.reshape(1, M), wglu)


def _ln_silu(h1, lg, lb):
    mu = jnp.mean(h1, axis=-1, keepdims=True)
    xc = h1 - mu
    rstd = lax.rsqrt(jnp.mean(xc * xc, axis=-1, keepdims=True) + EPS)
    xhat = xc * rstd
    h2 = xhat * lg + lb
    return xhat, rstd, h2, _sigmoid(h2)


def _shifted_copies(buf_ref, sh_ref, cs, n):
    for s in range(1, 8):
        sh_ref[s, 0:n, :] = buf_ref[pl.ds(s, n), cs]


def _tap(buf_ref, sh_ref, cs, off, r0, rows):
    q, s = divmod(off, 8)
    return buf_ref[pl.ds(r0 + 8 * q, rows), cs] if s == 0 else sh_ref[s, pl.ds(r0 + 8 * q, rows), :]


def _cv_fwd(proj, vcol, gcol, wdw, bdw, lg, lb, wpw, bpw):
    T = proj.shape[0]
    M = wpw.shape[0]
    tc = _pick(T, 256, 8)
    rb = _pick(tc, 64, 8)
    H, K = CONV_HALO, 31

    def body(v_ref, g_ref, wdw_ref, bdw_ref, lg_ref, lb_ref, wpw_ref, bpw_ref, y_ref, h1_ref, buf_ref, sh_ref):
        @pl.when(pl.program_id(0) == 0)
        def _():
            buf_ref[0:H, :] = jnp.zeros((H, M), F32)

        buf_ref[H:H + tc, :] = v_ref[...] * _sigmoid(g_ref[...])

        def strip(s, _):
            cs = pl.ds(pl.multiple_of(s * LANES, LANES), LANES)
            _shifted_copies(buf_ref, sh_ref, cs, tc + H - 8)
            for r0 in range(0, tc, rb):
                acc = jnp.zeros((rb, LANES), F32) + bdw_ref[:, cs]
                for k in range(K):
                    acc = acc + wdw_ref[k:k + 1, cs] * _tap(buf_ref, sh_ref, cs, H - (K - 1) + k, r0, rb)
                h1_ref[pl.ds(r0, rb), cs] = acc
            return 0

        lax.fori_loop(0, M // LANES, strip, 0)
        _, _, h2, sg = _ln_silu(h1_ref[...], lg_ref[...], lb_ref[...])
        y_ref[...] = (_dot(h2 * sg, wpw_ref[...]) + bpw_ref[...]).astype(BF16)
        buf_ref[0:H, :] = buf_ref[tc:tc + H, :]

    def rowm(c):
        return pl.BlockSpec((tc, M), lambda i: (i, c))
    vec = _full((1, M))
    return pl.pallas_call(
        body, name="cv_fwd", grid=(T // tc,),
        in_specs=[rowm(vcol), rowm(gcol), _full((32, M)), vec, vec, vec, _full((M, M)), vec],
        out_specs=[rowm(0), rowm(0)],
        out_shape=[jax.ShapeDtypeStruct((T, M), BF16), jax.ShapeDtypeStruct((T, M), F32)],
        scratch_shapes=[pltpu.VMEM((tc + H, M), F32), pltpu.VMEM((8, tc + H, LANES), F32)], compiler_params=ARB,
    )(proj, proj, wdw, bdw.reshape(1, M), lg.reshape(1, M), lb.reshape(1, M), wpw, bpw.reshape(1, M))


def _cv_bwd(dmix, dcol, proj, vcol, gcol, h1, wdw, lg, lb, wpw):
    T = proj.shape[0]
    M = wpw.shape[0]
    tc = _pick(T, 256, 8)
    nc = T // tc
    rb = _pick(tc, 32, 16)
    H, K = CONV_HALO, 31
    hb = tc // H

    def body(do_ref, v_ref, g_ref, vh_ref, gh_ref, h1_ref, wdw_ref, lg_ref, lb_ref, wpw_ref,
             dv_ref, dg_ref, dwpw_ref, dwdw_ref, vec_ref, hbuf_ref, dbuf_ref, hsh_ref, dsh_ref):
        i = pl.program_id(0)

        @pl.when(i == 0)
        def _():
            for r in (dwpw_ref, dwdw_ref, vec_ref):
                r[...] = jnp.zeros_like(r)
            dbuf_ref[tc:tc + H, :] = jnp.zeros((H, M), F32)

        halo = vh_ref[...] * _sigmoid(gh_ref[...])
        hbuf_ref[0:H, :] = jnp.where(i == nc - 1, 0.0, halo)
        hbuf_ref[H:H + tc, :] = v_ref[...] * _sigmoid(g_ref[...])
        do = do_ref[...]
        xhat, rstd, h2, sg = _ln_silu(h1_ref[...], lg_ref[...], lb_ref[...])
        dwpw_ref[...] += _dot_tn(h2 * sg, do)
        vec_ref[0] += _fold8(do)
        dh2 = _dot_nt(do, wpw_ref[...]) * (sg * (1.0 + h2 * (1.0 - sg)))
        vec_ref[1] += _fold8(dh2 * xhat)
        vec_ref[2] += _fold8(dh2)
        dxh = dh2 * lg_ref[...]
        dh1 = rstd * (dxh - jnp.mean(dxh, axis=-1, keepdims=True) - xhat * jnp.mean(dxh * xhat, axis=-1, keepdims=True))
        vec_ref[3] += _fold8(dh1)
        dbuf_ref[0:tc, :] = dh1

        def strip(s, _):
            cs = pl.ds(pl.multiple_of(s * LANES, LANES), LANES)
            _shifted_copies(dbuf_ref, dsh_ref, cs, tc + H - 8)
            _shifted_copies(hbuf_ref, hsh_ref, cs, tc + H - 8)
            for r0 in range(0, tc, rb):
                d1 = dbuf_ref[pl.ds(r0, rb), cs]
                dh0 = jnp.zeros((rb, LANES), F32)
                for k in range(K):
                    dh0 = dh0 + wdw_ref[k:k + 1, cs] * _tap(dbuf_ref, dsh_ref, cs, K - 1 - k, r0, rb)
                    dwdw_ref[k, :, cs] += _fold8(d1 * _tap(hbuf_ref, hsh_ref, cs, H - (K - 1) + k, r0, rb))
                sg0 = _sigmoid(g_ref[pl.ds(r0, rb), cs])
                dv_ref[pl.ds(r0, rb), cs] = (dh0 * sg0).astype(BF16)
                dg_ref[pl.ds(r0, rb), cs] = (dh0 * v_ref[pl.ds(r0, rb), cs] * sg0 * (1.0 - sg0)).astype(BF16)
            return 0

        lax.fori_loop(0, M // LANES, strip, 0)
        dbuf_ref[tc:tc + H, :] = dbuf_ref[0:H, :]

    def rowm(c):
        return pl.BlockSpec((tc, M), lambda i: (nc - 1 - i, c))

    def halo(c):
        return pl.BlockSpec((H, M), lambda i: (jnp.maximum((nc - 1 - i) * hb - 1, 0), c))
    vec = _full((1, M))
    return pl.pallas_call(
        body, name="cv_bwd", grid=(nc,),
        in_specs=[rowm(dcol), rowm(vcol), rowm(gcol), halo(vcol), halo(gcol), rowm(0), _full((32, M)), vec, vec, _full((M, M))],
        out_specs=[rowm(0), rowm(0), _full((M, M)), _full((32, 8, M)), _full((5, 8, M))],
        out_shape=[jax.ShapeDtypeStruct((T, M), BF16), jax.ShapeDtypeStruct((T, M), BF16), jax.ShapeDtypeStruct((M, M), F32),
                   jax.ShapeDtypeStruct((32, 8, M), F32), jax.ShapeDtypeStruct((5, 8, M), F32)],
        scratch_shapes=[pltpu.VMEM((tc + H, M), F32), pltpu.VMEM((tc + H, M), F32), pltpu.VMEM((8, tc + H, LANES), F32),
                        pltpu.VMEM((8, tc + H, LANES), F32)], compiler_params=ARB,
    )(dmix, proj, proj, proj, proj, h1, wdw, lg.reshape(1, M), lb.reshape(1, M), wpw)


def _lru_gates(xc, r, ig, sp8):
    la = -r * sp8
    a = jnp.exp(la)
    mult = jnp.sqrt(_neg_expm1(2.0 * la))
    return a, mult, mult * (ig * xc)


def _lru_fwd(proj, xcol, gcol, wcv, bcv, wr, br, wi, bi, sp8):
    T = proj.shape[0]
    M = wr.shape[0]
    tc = _pick(T, 256, 8)
    H, K = SMALL_HALO, 4

    def body(x_ref, g_ref, wcv_ref, bcv_ref, wr_ref, br_ref, wi_ref, bi_ref, sp_ref,
             y_ref, xc_ref, r_ref, i_ref, hp_ref, buf_ref, a_ref, car_ref):
        @pl.when(pl.program_id(0) == 0)
        def _():
            buf_ref[0:H, :] = jnp.zeros((H, M), F32)
            car_ref[...] = jnp.zeros_like(car_ref)

        buf_ref[H:H + tc, :] = x_ref[...]
        xc = jnp.zeros((tc, M), F32) + bcv_ref[...]
        for k in range(K):
            xc = xc + wcv_ref[k:k + 1, :] * buf_ref[pl.ds(H - (K - 1) + k, tc), :]
        buf_ref[0:H, :] = buf_ref[tc:tc + H, :]
        r = _sigmoid(_dot(xc, wr_ref[...]) + br_ref[...])
        ig = _sigmoid(_dot(xc, wi_ref[...]) + bi_ref[...])
        xc_ref[...] = xc
        r_ref[...] = r
        i_ref[...] = ig
        a, _, bt = _lru_gates(xc, r, ig, sp_ref[...])
        a_ref[...] = a
        hp_ref[...] = bt
        row = lax.broadcasted_iota(jnp.int32, (8, M), 0)

        def blk(ib, _):
            r0 = pl.multiple_of(ib * 8, 8)
            av, bv = a_ref[pl.ds(r0, 8), :], hp_ref[pl.ds(r0, 8), :]
            for d in (1, 2, 4):
                m = row >= d
                bv = jnp.where(m, bv + av * pltpu.roll(bv, d, 0), bv)
                av = jnp.where(m, av * pltpu.roll(av, d, 0), av)
            cr = car_ref[7:8, :]
            h = bv + av * cr
            a_ref[pl.ds(r0, 8), :] = h
            hp_ref[pl.ds(r0, 8), :] = jnp.where(row == 0, cr, pltpu.roll(h, 1, 0))
            car_ref[...] = h
            return 0

        lax.fori_loop(0, tc // 8, blk, 0)
        y_ref[...] = (a_ref[...] * _gelu(g_ref[...])).astype(BF16)

    def rowm(c):
        return pl.BlockSpec((tc, M), lambda i: (i, c))
    vec = _full((1, M))
    return pl.pallas_call(
        body, name="lru_fwd", grid=(T // tc,),
        in_specs=[rowm(xcol), rowm(gcol), _full((8, M)), vec, _full((M, M)), vec, _full((M, M)), vec, vec],
        out_specs=[rowm(0)] * 5,
        out_shape=[jax.ShapeDtypeStruct((T, M), BF16)] + [jax.ShapeDtypeStruct((T, M), F32)] * 4,
        scratch_shapes=[pltpu.VMEM((tc + H, M), F32), pltpu.VMEM((tc, M), F32), pltpu.VMEM((8, M), F32)], compiler_params=ARB,
    )(proj, proj, wcv, bcv.reshape(1, M), wr, br.reshape(1, M), wi, bi.reshape(1, M), sp8.reshape(1, M))


def _lru_bwd(dmix, dcol, proj, xcol, gcol, xc, r, ig, hprev, wcv, wr, wi, sp8):
    T = proj.shape[0]
    M = wr.shape[0]
    tc = _pick(T, 256, 8)
    nc = T // tc
    H, K = SMALL_HALO, 4
    hb = tc // H

    def body(do_ref, x_ref, g_ref, xh_ref, xc_ref, r_ref, i_ref, hp_ref, wcv_ref, wr_ref, wi_ref, sp_ref,
             dx_ref, dg_ref, dwr_ref, dwi_ref, dwcv_ref, vec_ref, xbuf_ref, dbuf_ref, a_ref, gs_ref, car_ref):
        i = pl.program_id(0)

        @pl.when(i == 0)
        def _():
            for rf in (dwr_ref, dwi_ref, dwcv_ref, vec_ref, car_ref):
                rf[...] = jnp.zeros_like(rf)
            dbuf_ref[tc:tc + H, :] = jnp.zeros((H, M), F32)

        xcv, rv, iv, hpv, gv, do = xc_ref[...], r_ref[...], i_ref[...], hp_ref[...], g_ref[...], do_ref[...]
        sp = sp_ref[...]
        a, mult, bt = _lru_gates(xcv, rv, iv, sp)
        h = a * hpv + bt
        ge = _gelu(gv)
        dg_ref[...] = (do * h * _gelu_grad(gv)).astype(BF16)
        a_ref[...] = a
        gs_ref[...] = do * ge
        row = lax.broadcasted_iota(jnp.int32, (8, M), 0)

        def blk(n, _):
            r0 = pl.multiple_of((tc // 8 - 1 - n) * 8, 8)
            ablk = a_ref[pl.ds(r0, 8), :]
            av = jnp.where(row == 7, car_ref[0, 0:1, :], pltpu.roll(ablk, 7, 0))
            bv = gs_ref[pl.ds(r0, 8), :]
            for d in (1, 2, 4):
                m = row <= 7 - d
                bv = jnp.where(m, bv + av * pltpu.roll(bv, 8 - d, 0), bv)
                av = jnp.where(m, av * pltpu.roll(av, 8 - d, 0), av)
            gsv = bv + av * car_ref[1, 0:1, :]
            gs_ref[pl.ds(r0, 8), :] = gsv
            car_ref[0] = ablk
            car_ref[1] = gsv
            return 0

        lax.fori_loop(0, tc // 8, blk, 0)
        gs = gs_ref[...]
        dixc = gs * mult
        dla = gs * hpv * a - gs * (iv * xcv) * (a * a) / mult
        vec_ref[2] += _fold8(-dla * rv)
        dpr = (-dla * sp) * rv * (1.0 - rv)
        dpi = (dixc * xcv) * iv * (1.0 - iv)
        vec_ref[0] += _fold8(dpr)
        vec_ref[1] += _fold8(dpi)
        dwr_ref[...] += _dot_tn(xcv, dpr)
        dwi_ref[...] += _dot_tn(xcv, dpi)
        dxc = dixc * iv + _dot_nt(dpr, wr_ref[...]) + _dot_nt(dpi, wi_ref[...])
        vec_ref[3] += _fold8(dxc)
        xbuf_ref[0:H, :] = jnp.where(i == nc - 1, 0.0, xh_ref[...])
        xbuf_ref[H:H + tc, :] = x_ref[...]
        dbuf_ref[0:tc, :] = dxc
        dx = jnp.zeros((tc, M), F32)
        for k in range(K):
            dx = dx + wcv_ref[k:k + 1, :] * dbuf_ref[pl.ds(K - 1 - k, tc), :]
            dwcv_ref[k] += _fold8(dxc * xbuf_ref[pl.ds(H - (K - 1) + k, tc), :])
        dx_ref[...] = dx.astype(BF16)
        dbuf_ref[tc:tc + H, :] = dbuf_ref[0:H, :]

    def rowm(c):
        return pl.BlockSpec((tc, M), lambda i: (nc - 1 - i, c))
    halo = pl.BlockSpec((H, M), lambda i: (jnp.maximum((nc - 1 - i) * hb - 1, 0), xcol))
    vec = _full((1, M))
    return pl.pallas_call(
        body, name="lru_bwd", grid=(nc,),
        in_specs=[rowm(dcol), rowm(xcol), rowm(gcol), halo, rowm(0), rowm(0), rowm(0), rowm(0), _full((8, M)),
                  _full((M, M)), _full((M, M)), vec],
        out_specs=[rowm(0), rowm(0), _full((M, M)), _full((M, M)), _full((8, 8, M)), _full((4, 8, M))],
        out_shape=[jax.ShapeDtypeStruct((T, M), BF16), jax.ShapeDtypeStruct((T, M), BF16), jax.ShapeDtypeStruct((M, M), F32),
                   jax.ShapeDtypeStruct((M, M), F32), jax.ShapeDtypeStruct((8, 8, M), F32), jax.ShapeDtypeStruct((4, 8, M), F32)],
        scratch_shapes=[pltpu.VMEM((tc + H, M), F32), pltpu.VMEM((tc + H, M), F32), pltpu.VMEM((tc, M), F32),
                        pltpu.VMEM((tc, M), F32), pltpu.VMEM((2, 8, M), F32)],
        compiler_params=ARB,
    )(dmix, proj, proj, proj, xc, r, ig, hprev, wcv, wr, wi, sp8.reshape(1, M))


def _pool_diffs(buf_ref, x, t0, tc, gw):
    H = POOL_HALO
    pos = (lax.broadcasted_iota(jnp.int32, (tc, gw), 0) + t0 + 1).astype(F32)
    out, inv = [], []
    for gi, win in enumerate(POOL_WINDOWS):
        sl = slice(gi * gw, (gi + 1) * gw)
        s = x[:, sl]
        for j in range(1, win):
            s = s + buf_ref[pl.ds(H - j, tc), sl]
        ic = 1.0 / jnp.minimum(pos, float(win))
        out.append(s * ic - x[:, sl])
        inv.append(ic)
    return out, inv


def _pool_fwd(proj, col, pw, scale):
    T = proj.shape[0]
    ng, gw, _ = pw.shape
    M = ng * gw
    tc = _pick(T, 256, 8)
    H = POOL_HALO

    def body(x_ref, pw_ref, sc_ref, y_ref, buf_ref):
        @pl.when(pl.program_id(0) == 0)
        def _():
            buf_ref[0:H, :] = jnp.zeros((H, M), F32)

        x = x_ref[...]
        buf_ref[H:H + tc, :] = x
        diffs, _ = _pool_diffs(buf_ref, x, pl.program_id(0) * tc, tc, gw)
        for gi in range(ng):
            sl = slice(gi * gw, (gi + 1) * gw)
            y_ref[:, sl] = (_dot(diffs[gi], pw_ref[gi]) * sc_ref[:, sl]).astype(BF16)
        buf_ref[0:H, :] = buf_ref[tc:tc + H, :]

    return pl.pallas_call(
        body, name="pool_fwd", grid=(T // tc,),
        in_specs=[pl.BlockSpec((tc, M), lambda i: (i, col)), _full(pw.shape), _full((1, M))],
        out_specs=pl.BlockSpec((tc, M), lambda i: (i, 0)), out_shape=jax.ShapeDtypeStruct((T, M), BF16),
        scratch_shapes=[pltpu.VMEM((tc + H, M), F32)], compiler_params=ARB,
    )(proj, pw, scale.reshape(1, M))


def _pool_bwd(dmix, dcol, proj, col, pw, scale):
    T = proj.shape[0]
    ng, gw, _ = pw.shape
    M = ng * gw
    tc = _pick(T, 256, 8)
    nc = T // tc
    H = POOL_HALO
    hb = tc // H

    def body(do_ref, x_ref, xh_ref, pw_ref, sc_ref, dx_ref, dpw_ref, dsc_ref, buf_ref, ebuf_ref):
        i = pl.program_id(0)

        @pl.when(i == 0)
        def _():
            dpw_ref[...] = jnp.zeros_like(dpw_ref)
            dsc_ref[...] = jnp.zeros_like(dsc_ref)
            ebuf_ref[tc:tc + H, :] = jnp.zeros((H, M), F32)

        x, do = x_ref[...], do_ref[...]
        buf_ref[0:H, :] = jnp.where(i == nc - 1, 0.0, xh_ref[...])
        buf_ref[H:H + tc, :] = x
        diffs, inv = _pool_diffs(buf_ref, x, (nc - 1 - i) * tc, tc, gw)
        ddg = []
        for gi in range(ng):
            sl = slice(gi * gw, (gi + 1) * gw)
            dyy = do[:, sl] * sc_ref[:, sl]
            dsc_ref[:, sl] += _fold8(do[:, sl] * _dot(diffs[gi], pw_ref[gi]))
            dpw_ref[gi] += _dot_tn(diffs[gi], dyy)
            d = _dot_nt(dyy, pw_ref[gi])
            ddg.append(d)
            ebuf_ref[0:tc, sl] = d * inv[gi]
        for gi, win in enumerate(POOL_WINDOWS):
            sl = slice(gi * gw, (gi + 1) * gw)
            s = -ddg[gi]
            for j in range(win):
                s = s + ebuf_ref[pl.ds(j, tc), sl]
            dx_ref[:, sl] = s.astype(BF16)
        ebuf_ref[tc:tc + H, :] = ebuf_ref[0:H, :]

    return pl.pallas_call(
        body, name="pool_bwd", grid=(nc,),
        in_specs=[pl.BlockSpec((tc, M), lambda i: (nc - 1 - i, dcol)), pl.BlockSpec((tc, M), lambda i: (nc - 1 - i, col)),
                  pl.BlockSpec((H, M), lambda i: (jnp.maximum((nc - 1 - i) * hb - 1, 0), col)), _full(pw.shape), _full((1, M))],
        out_specs=[pl.BlockSpec((tc, M), lambda i: (nc - 1 - i, 0)), _full(pw.shape), _full((8, M))],
        out_shape=[jax.ShapeDtypeStruct((T, M), BF16), jax.ShapeDtypeStruct(pw.shape, F32), jax.ShapeDtypeStruct((8, M), F32)],
        scratch_shapes=[pltpu.VMEM((tc + H, M), F32), pltpu.VMEM((tc + H, M), F32)], compiler_params=ARB,
    )(dmix, proj, proj, pw, scale.reshape(1, M))


def _ffn_mid_fwd(up, wdw, bdw):
    T, F2 = up.shape
    F = F2 // 2
    tc = _pick(T, 256, 8)
    rb = _pick(tc, 64, 16)
    H, K = FFN_HALO, 3

    def body(gt_ref, val_ref, w_ref, b_ref, act_ref, buf_ref):
        @pl.when(pl.program_id(0) == 0)
        def _():
            buf_ref[0:H, :] = jnp.zeros((H, F), F32)

        buf_ref[H:H + tc, :] = gt_ref[...].astype(F32)

        def strip(s, _):
            cs = pl.ds(pl.multiple_of(s * LANES, LANES), LANES)
            taps = [w_ref[k:k + 1, cs] for k in range(K)]
            for r0 in range(0, tc, rb):
                gc = b_ref[:, cs] + taps[0] * buf_ref[pl.ds(H - (K - 1) + r0, rb), cs]
                for k in range(1, K):
                    gc = gc + taps[k] * buf_ref[pl.ds(H - (K - 1) + k + r0, rb), cs]
                act_ref[pl.ds(r0, rb), cs] = (_gelu(gc) * val_ref[pl.ds(r0, rb), cs].astype(F32)).astype(BF16)
            return 0

        lax.fori_loop(0, F // LANES, strip, 0)
        buf_ref[0:H, :] = buf_ref[tc:tc + H, :]

    return pl.pallas_call(
        body, name="ffn_mid_fwd", grid=(T // tc,),
        in_specs=[pl.BlockSpec((tc, F), lambda i: (i, 0)), pl.BlockSpec((tc, F), lambda i: (i, 1)), _full((8, F)), _full((1, F))],
        out_specs=pl.BlockSpec((tc, F), lambda i: (i, 0)), out_shape=jax.ShapeDtypeStruct((T, F), BF16),
        scratch_shapes=[pltpu.VMEM((tc + H, F), F32)], compiler_params=ARB,
    )(up, up, wdw, bdw.reshape(1, F))


def _ffn_mid_bwd(dact, up, wdw, bdw):
    T, F2 = up.shape
    F = F2 // 2
    tc = _pick(T, 128, 8)
    rb = _pick(tc, 64, 16)
    nc = T // tc
    H, K = FFN_HALO, 3
    hb = tc // H

    def body(da_ref, gt_ref, val_ref, gh_ref, w_ref, b_ref, dup_ref, dw_ref, db_ref, gbuf_ref, dbuf_ref):
        i = pl.program_id(0)

        @pl.when(i == 0)
        def _():
            dw_ref[...] = jnp.zeros_like(dw_ref)
            db_ref[...] = jnp.zeros_like(db_ref)
            dbuf_ref[tc:tc + H, :] = jnp.zeros((H, F), F32)

        gbuf_ref[0:H, :] = jnp.where(i == nc - 1, 0.0, gh_ref[...].astype(F32))
        gbuf_ref[H:H + tc, :] = gt_ref[...].astype(F32)

        def strip(s, _):
            cs = pl.ds(pl.multiple_of(s * LANES, LANES), LANES)
            cs_val = pl.ds(pl.multiple_of(F + s * LANES, LANES), LANES)
            taps = [w_ref[k:k + 1, cs] for k in range(K)]
            for r0 in range(0, tc, rb):
                shifted = [gbuf_ref[pl.ds(H - (K - 1) + k + r0, rb), cs] for k in range(K)]
                gc = b_ref[:, cs] + taps[0] * shifted[0]
                for k in range(1, K):
                    gc = gc + taps[k] * shifted[k]
                ge, gg = _gelu_and_grad(gc)
                da = da_ref[pl.ds(r0, rb), cs].astype(F32)
                dup_ref[pl.ds(r0, rb), cs_val] = (da * ge).astype(BF16)
                dgc = da * val_ref[pl.ds(r0, rb), cs].astype(F32) * gg
                dbuf_ref[pl.ds(r0, rb), cs] = dgc
                db_ref[:, cs] += _fold8(dgc)
                for k in range(K):
                    dw_ref[k, :, cs] += _fold8(dgc * shifted[k])
            for r0 in range(0, tc, rb):
                dgt = taps[0] * dbuf_ref[pl.ds(K - 1 + r0, rb), cs]
                for k in range(1, K):
                    dgt = dgt + taps[k] * dbuf_ref[pl.ds(K - 1 - k + r0, rb), cs]
                dup_ref[pl.ds(r0, rb), cs] = dgt.astype(BF16)
            return 0

        lax.fori_loop(0, F // LANES, strip, 0)
        dbuf_ref[tc:tc + H, :] = dbuf_ref[0:H, :]

    return pl.pallas_call(
        body, name="ffn_mid_bwd", grid=(nc,),
        in_specs=[pl.BlockSpec((tc, F), lambda i: (nc - 1 - i, 0)), pl.BlockSpec((tc, F), lambda i: (nc - 1 - i, 0)),
                  pl.BlockSpec((tc, F), lambda i: (nc - 1 - i, 1)),
                  pl.BlockSpec((H, F), lambda i: (jnp.maximum((nc - 1 - i) * hb - 1, 0), 0)), _full((8, F)), _full((1, F))],
        out_specs=[pl.BlockSpec((tc, F2), lambda i: (nc - 1 - i, 0)), _full((8, 8, F)), _full((8, F))],
        out_shape=[jax.ShapeDtypeStruct((T, F2), BF16), jax.ShapeDtypeStruct((8, 8, F), F32), jax.ShapeDtypeStruct((8, F), F32)],
        scratch_shapes=[pltpu.VMEM((tc + H, F), F32), pltpu.VMEM((tc + H, F), F32)], compiler_params=ARB,
    )(dact, up, up, up, wdw, bdw.reshape(1, F))


def _my_pos():
    return lax.axis_index("x"), lax.axis_index("y"), lax.axis_index("c")


def _flip(pos, k):
    x, y, c = pos
    return ((1 - x) if k & 4 else x, (1 - y) if k & 2 else y, (1 - c) if k & 1 else c)


def _rank(pos):
    return 4 * pos[0] + 2 * pos[1] + pos[2]


def _exchange(xs, name, gather):
    n = len(xs)
    shapes = [((N_DEV,) + tuple(x.shape)) if gather else tuple(x.shape) for x in xs]

    def body(*refs):
        ins, outs = refs[:n], refs[n:2 * n]
        send_sems, recv_sems, local_sems = refs[2 * n:]
        me = _my_pos()
        mine = _rank(me)
        local, sends = [], []
        for a in range(n):
            loc = pltpu.make_async_copy(ins[a] if gather else ins[a].at[mine], outs[a].at[mine], local_sems.at[a])
            loc.start()
            local.append(loc)
            for k in range(1, N_DEV):
                peer = _flip(me, k)
                cp = pltpu.make_async_remote_copy(
                    src_ref=ins[a] if gather else ins[a].at[_rank(peer)], dst_ref=outs[a].at[mine],
                    send_sem=send_sems.at[a, k - 1], recv_sem=recv_sems.at[a, k - 1], device_id=peer,
                    device_id_type=pl.DeviceIdType.MESH)
                cp.start()
                sends.append(cp)
        for a in range(n):
            for k in range(1, N_DEV):
                peer = _flip(me, k)
                pltpu.make_async_remote_copy(
                    src_ref=ins[a] if gather else ins[a].at[_rank(peer)], dst_ref=outs[a].at[_rank(peer)],
                    send_sem=send_sems.at[a, k - 1], recv_sem=recv_sems.at[a, k - 1], device_id=peer,
                    device_id_type=pl.DeviceIdType.MESH).wait_recv()
        for cp in sends:
            cp.wait_send()
        for loc in local:
            loc.wait()

    any_spec = pl.BlockSpec(memory_space=pl.ANY)
    return pl.pallas_call(
        body, name=name, in_specs=[any_spec] * n, out_specs=[any_spec] * n,
        out_shape=[jax.ShapeDtypeStruct(s, x.dtype) for s, x in zip(shapes, xs)],
        scratch_shapes=[pltpu.SemaphoreType.DMA((n, N_DEV - 1)), pltpu.SemaphoreType.DMA((n, N_DEV - 1)),
                        pltpu.SemaphoreType.DMA((n,))],
    )(*xs)


_HBM = pl.BlockSpec(memory_space=pltpu.HBM)
_SEM = pl.BlockSpec(memory_space=pltpu.SEMAPHORE)
_EFFECT = pltpu.SideEffectType.DATAFLOW_SIDE_EFFECTING


def _plan_direct(gather):
    def plan(bufs, ssem, rsem, me, outgoing):
        n = len(bufs) // 2
        out = []
        for a in range(n):
            for k in range(1, N_DEV):
                peer = _flip(me, k)
                out.append(pltpu.make_async_remote_copy(
                    src_ref=bufs[a] if gather else bufs[a].at[_rank(peer)],
                    dst_ref=bufs[n + a].at[_rank(me) if outgoing else _rank(peer)],
                    send_sem=ssem.at[a * 7 + k - 1], recv_sem=rsem.at[a * 7 + k - 1], device_id=peer,
                    device_id_type=pl.DeviceIdType.MESH))
        return out
    return plan, 7


_SAME_CORE = (2, 4, 6)


def _plan_own_block(bufs, ssem, rsem, me, outgoing):
    n = len(bufs) // 2
    out = []
    for a in range(n):
        for j, k in enumerate((1,) + _SAME_CORE):
            peer = _flip(me, k)
            out.append(pltpu.make_async_remote_copy(
                src_ref=bufs[a], dst_ref=bufs[n + a].at[_rank(me) if outgoing else _rank(peer)],
                send_sem=ssem.at[a * 4 + j], recv_sem=rsem.at[a * 4 + j], device_id=peer, device_id_type=pl.DeviceIdType.MESH))
    return out


def _plan_pass_on(bufs, ssem, rsem, me, outgoing):
    sibling = _flip(me, 1)
    out = []
    for a in range(len(bufs)):
        for j, k in enumerate(_SAME_CORE):
            slot = _rank(_flip(me, k) if outgoing else _flip(sibling, k))
            out.append(pltpu.make_async_remote_copy(
                src_ref=bufs[a].at[slot], dst_ref=bufs[a].at[slot], send_sem=ssem.at[a * 3 + j], recv_sem=rsem.at[a * 3 + j],
                device_id=sibling, device_id_type=pl.DeviceIdType.MESH))
    return out


def _split_start(bufs, plan, n_sems, name, after=None):
    nb = len(bufs)
    extra = [] if after is None else [after]

    def body(*refs):
        ssem, rsem = refs[nb + len(extra)], refs[nb + len(extra) + 1]
        for cp in plan(refs[:nb], ssem, rsem, _my_pos(), True):
            cp.start()
        token = refs[2 * nb + len(extra) + 2]
        token[...] = jnp.zeros_like(token)

    res = pl.pallas_call(
        body, name=name,
        out_shape=(pltpu.SemaphoreType.DMA((n_sems,)), pltpu.SemaphoreType.DMA((n_sems,)),
                   *[pltpu.HBM(t.shape, t.dtype) for t in bufs], jax.ShapeDtypeStruct((8, 128), F32)),
        in_specs=[_HBM] * nb + [pl.BlockSpec(memory_space=pl.ANY)] * len(extra),
        out_specs=(_SEM, _SEM, *([_HBM] * nb), pl.BlockSpec(memory_space=pltpu.VMEM)),
        input_output_aliases={i: 2 + i for i in range(nb)},
        compiler_params=pltpu.CompilerParams(has_side_effects=_EFFECT),
    )(*[pltpu.with_memory_space_constraint(t, pltpu.HBM) for t in bufs], *extra)
    return (res[0], res[1], list(res[2:2 + nb])), res[2 + nb]


def _split_wait(handle, plan, after, name):
    ssem, rsem, bufs = handle
    nb = len(bufs)

    def body(*refs):
        for cp in plan(refs[:nb], refs[nb], refs[nb + 1], _my_pos(), False):
            cp.wait_send()
            cp.wait_recv()

    res = pl.pallas_call(
        body, name=name, out_shape=tuple(pltpu.HBM(t.shape, t.dtype) for t in bufs),
        in_specs=[_HBM] * nb + [_SEM, _SEM, pl.BlockSpec(memory_space=pl.ANY)], out_specs=tuple([_HBM] * nb),
        input_output_aliases={i: i for i in range(nb)},
        compiler_params=pltpu.CompilerParams(has_side_effects=_EFFECT),
    )(*bufs, ssem, rsem, after)
    return list(res)


def _landing_zones(xs, gather):
    mine = _rank(_my_pos())
    lands = []
    for x in xs:
        own = x if gather else lax.dynamic_index_in_dim(x, mine, 0, keepdims=False)
        lands.append(lax.dynamic_update_index_in_dim(lax.empty((N_DEV,) + tuple(own.shape), x.dtype), own, mine, 0))
    return lands


def _gather_start(xs, name, after=None, two_level=True):
    bufs = list(xs) + _landing_zones(xs, True)
    if two_level:
        handle, token = _split_start(bufs, _plan_own_block, 4 * len(xs), name + "_start", after)
    else:
        plan, per = _plan_direct(True)
        handle, token = _split_start(bufs, plan, per * len(xs), name + "_start", after)
    return (handle, two_level, name), token


def _gather_wait(state, after):
    handle, two_level, name = state
    n = len(handle[2]) // 2
    if not two_level:
        return _split_wait(handle, _plan_direct(True)[0], after, name + "_wait")[n:]
    lands = _split_wait(handle, _plan_own_block, after, name + "_wait")[n:]
    handle, _ = _split_start(lands, _plan_pass_on, 3 * n, name + "_pass")
    return _split_wait(handle, _plan_pass_on, after, name + "_passed")


def _scatter_start(xs, name):
    plan, per = _plan_direct(False)
    handle, token = _split_start(list(xs) + _landing_zones(xs, False), plan, per * len(xs), name + "_start")
    return (handle, name), token


def _scatter_wait(state, after):
    handle, name = state
    return _split_wait(handle, _plan_direct(False)[0], after, name + "_wait")[len(handle[2]) // 2:]


def _adamw_math(w, g, m, v):
    m = ADAM_B1 * m + (1.0 - ADAM_B1) * g
    v = ADAM_B2 * v + (1.0 - ADAM_B2) * (g * g)
    m_hat = m / (1.0 - ADAM_B1 ** ADAM_STEP)
    v_hat = v / (1.0 - ADAM_B2 ** ADAM_STEP)
    return -ADAM_LR * (m_hat / (jnp.sqrt(v_hat) + ADAM_EPS) + ADAM_WD * w), m, v


def _sum8(parts, *, name):
    _, R, C = parts.shape
    tr = _pick(R, 256, 16)

    def body(p_ref, o_ref):
        acc = p_ref[0].astype(F32)
        for r in range(1, N_DEV):
            acc = acc + p_ref[r].astype(F32)
        o_ref[...] = acc

    return pl.pallas_call(
        body, name=name, grid=(R // tr,), in_specs=[pl.BlockSpec((N_DEV, tr, C), lambda i: (0, i, 0))],
        out_specs=pl.BlockSpec((tr, C), lambda i: (i, 0)), out_shape=jax.ShapeDtypeStruct((R, C), F32),
        compiler_params=pltpu.CompilerParams(dimension_semantics=("parallel",)),
    )(parts)


def _adamw_layer(g, w, m, v, layer, prev, *, name):
    depth, R, C = w.shape
    tr = _pick(R, 64, 16)
    summed = g.ndim == 3
    if prev is None:
        prev = [lax.empty((depth, R, C), F32) for _ in range(4)]

    def body(g_ref, w_ref, m_ref, v_ref, *rest):
        outs = rest[4:]
        if summed:
            gv = g_ref[0].astype(F32)
            for r in range(1, N_DEV):
                gv = gv + g_ref[r].astype(F32)
        else:
            gv = g_ref[...]
        d, nm, nv = _adamw_math(w_ref[...], gv, m_ref[...], v_ref[...])
        for o, val in zip(outs, (gv, d, nm, nv)):
            o[...] = val

    g_spec = pl.BlockSpec((N_DEV, tr, C), lambda i: (0, i, 0)) if summed else pl.BlockSpec((tr, C), lambda i: (i, 0))
    slab = pl.BlockSpec((None, tr, C), lambda i: (layer, i, 0))
    return pl.pallas_call(
        body, name=name, grid=(R // tr,), in_specs=[g_spec] + [slab] * 3 + [pl.BlockSpec(memory_space=pl.ANY)] * 4,
        out_specs=[slab] * 4, out_shape=[jax.ShapeDtypeStruct((depth, R, C), F32)] * 4,
        input_output_aliases={4 + q: q for q in range(4)},
        compiler_params=pltpu.CompilerParams(dimension_semantics=("parallel",)),
    )(g, w, m, v, *prev)


def _lane_view(shape):
    if len(shape) == 1:
        return (1, shape[0])
    size = math.prod(shape)
    if shape[-1] < 64 and size % LANES == 0:
        return (size // LANES, LANES)
    if len(shape) > 2 and shape[-2] % 8 == 0:
        return (size // shape[-1], shape[-1])
    return tuple(shape)


def _adamw_many(ws, gs, ms, vs, *, name):
    n = len(ws)
    views = [_lane_view(w.shape) for w in ws]

    def body(*refs):
        for i in range(n):
            d, nm, nv = _adamw_math(refs[i][...], refs[n + i][...], refs[2 * n + i][...], refs[3 * n + i][...])
            refs[4 * n + i][...] = d
            refs[5 * n + i][...] = nm
            refs[6 * n + i][...] = nv

    outs = pl.pallas_call(
        body, name=name, out_shape=[jax.ShapeDtypeStruct(s, F32) for s in views] * 3,
    )(*[a.reshape(s) for arrs in (ws, gs, ms, vs) for a, s in zip(arrs, views)])
    return [[o.reshape(w.shape) for o, w in zip(outs[q * n:(q + 1) * n], ws)] for q in range(3)]


def _pack(arrs, dtype=F32):
    flat = jnp.concatenate([a.reshape(-1).astype(dtype) for a in arrs])
    pad = (-flat.shape[0]) % 2048
    return jnp.pad(flat, (0, pad)).reshape(-1, 128)


def _unpack(buf, shapes):
    flat, out, off = buf.reshape(-1), [], 0
    for s in shapes:
        n = math.prod(s)
        out.append(flat[off:off + n].reshape(s))
        off += n
    return out


def _s5_params(lam_re, lam_im, log_step, b_re, b_im):
    G, P, H = b_re.shape
    step = jnp.exp(log_step)[:, None]
    mag = jnp.exp(lam_re * step)
    ar, ai = mag * jnp.cos(lam_im * step), mag * jnp.sin(lam_im * step)
    den = lam_re * lam_re + lam_im * lam_im
    qr = ((ar - 1.0) * lam_re + ai * lam_im) / den
    qi = (ai * lam_re - (ar - 1.0) * lam_im) / den
    bbr = qr[..., None] * b_re - qi[..., None] * b_im
    bbi = qr[..., None] * b_im + qi[..., None] * b_re
    return ar.reshape(-1), ai.reshape(-1), _s5_blocks_in(bbr), _s5_blocks_in(bbi)


def _s5_blocks_in(b):
    G, P, H = b.shape
    eye = jnp.eye(8, dtype=b.dtype)
    return jnp.einsum("kgph,gj->kghjp", b.reshape(G // 8, 8, P, H), eye).reshape(G // 8, 8 * H, 8 * P)


def _s5_blocks_in_diag(m, G, P, H):
    return jnp.einsum("kghgp->kgph", m.reshape(G // 8, 8, H, 8, P)).reshape(G, P, H)


def _s5_blocks_out(c):
    G, H, P = c.shape
    eye = jnp.eye(8, dtype=c.dtype)
    return jnp.einsum("kghp,gj->kgpjh", c.reshape(G // 8, 8, H, P), eye).reshape(G // 8, 8 * P, 8 * H)


def _s5_blocks_out_diag(m, G, H, P):
    return jnp.einsum("kgpgh->kghp", m.reshape(G // 8, 8, P, 8, H)).reshape(G, H, P)


def _head_blocks(w):
    nh, d, e = w.shape
    return jnp.einsum("hde,hk->hdke", w, jnp.eye(nh, dtype=w.dtype)).reshape(nh * d, nh * e)


def _head_blocks_diag(m, nh, d, e):
    return jnp.einsum("hdhe->hde", m.reshape(nh, d, nh, e))


def _pad_rows(w, rows):
    return jnp.pad(w, ((0, rows - w.shape[0]), (0, 0)))


def _unshard_cols(g):
    return jnp.transpose(g, (1, 0, 2)).reshape(g.shape[1], -1)


_NAMES = ['norm_mix_g', 'w_in', 's5_lam_re', 's5_lam_im', 's5_log_step', 's5_b_re', 's5_b_im', 's5_c_re', 's5_c_im', 's5_d',
          's5_w_glu', 's5_b_glu', 'cv_w_dw', 'cv_b_dw', 'cv_ln_g', 'cv_ln_b', 'cv_w_pw', 'cv_b_pw', 'lru_w_conv', 'lru_b_conv',
          'lru_w_r', 'lru_b_r', 'lru_w_i', 'lru_b_i', 'lru_lam', 'pool_w', 'pool_scale', 'w_out', 'norm_ffn_g', 'ffn_w_up',
          'ffn_w_dw', 'ffn_b_dw', 'ffn_w_down', 'norm_final_g']
_MATRIX = ('w_in', 'w_out', 'ffn_w_up', 'ffn_w_down', 's5_w_glu', 'cv_w_pw')
_TRANSPOSED = ('w_in', 'ffn_w_up')
_COLSHARD = ('cv_w_dw', 'lru_w_conv', 'ffn_w_dw')
_GROUPS = (('w_in', 's5_w_glu', 'cv_w_pw', 'w_out'), ('ffn_w_up', 'ffn_w_down'))
_SMALL_BF16 = ('s5_b_re', 's5_b_im', 's5_c_re', 's5_c_im', 'lru_w_r', 'lru_w_i', 'pool_w')
_SCATTER_GROUPS = (('w_out', 's5_w_glu', 'cv_w_pw'), ('w_in',), ('ffn_w_up', 'ffn_w_down'))


def kernel(x, norm_mix_g, w_in, s5_lam_re, s5_lam_im, s5_log_step, s5_b_re, s5_b_im, s5_c_re, s5_c_im, s5_d, s5_w_glu, s5_b_glu, cv_w_dw, cv_b_dw, cv_ln_g, cv_ln_b, cv_w_pw, cv_b_pw, lru_w_conv, lru_b_conv, lru_w_r, lru_b_r, lru_w_i, lru_b_i, lru_lam, pool_w, pool_scale, w_out, norm_ffn_g, ffn_w_up, ffn_w_dw, ffn_b_dw, ffn_w_down, norm_final_g, loss_target, m_norm_mix_g, m_w_in, m_s5_lam_re, m_s5_lam_im, m_s5_log_step, m_s5_b_re, m_s5_b_im, m_s5_c_re, m_s5_c_im, m_s5_d, m_s5_w_glu, m_s5_b_glu, m_cv_w_dw, m_cv_b_dw, m_cv_ln_g, m_cv_ln_b, m_cv_w_pw, m_cv_b_pw, m_lru_w_conv, m_lru_b_conv, m_lru_w_r, m_lru_b_r, m_lru_w_i, m_lru_b_i, m_lru_lam, m_pool_w, m_pool_scale, m_w_out, m_norm_ffn_g, m_ffn_w_up, m_ffn_w_dw, m_ffn_b_dw, m_ffn_w_down, m_norm_final_g, v_norm_mix_g, v_w_in, v_s5_lam_re, v_s5_lam_im, v_s5_log_step, v_s5_b_re, v_s5_b_im, v_s5_c_re, v_s5_c_im, v_s5_d, v_s5_w_glu, v_s5_b_glu, v_cv_w_dw, v_cv_b_dw, v_cv_ln_g, v_cv_ln_b, v_cv_w_pw, v_cv_b_pw, v_lru_w_conv, v_lru_b_conv, v_lru_w_r, v_lru_b_r, v_lru_w_i, v_lru_b_i, v_lru_lam, v_pool_w, v_pool_scale, v_w_out, v_norm_ffn_g, v_ffn_w_up, v_ffn_w_dw, v_ffn_b_dw, v_ffn_w_down, v_norm_final_g):
    args = (x, norm_mix_g, w_in, s5_lam_re, s5_lam_im, s5_log_step, s5_b_re, s5_b_im, s5_c_re, s5_c_im, s5_d, s5_w_glu, s5_b_glu, cv_w_dw, cv_b_dw, cv_ln_g, cv_ln_b, cv_w_pw, cv_b_pw, lru_w_conv, lru_b_conv, lru_w_r, lru_b_r, lru_w_i, lru_b_i, lru_lam, pool_w, pool_scale, w_out, norm_ffn_g, ffn_w_up, ffn_w_dw, ffn_b_dw, ffn_w_down, norm_final_g, loss_target, m_norm_mix_g, m_w_in, m_s5_lam_re, m_s5_lam_im, m_s5_log_step, m_s5_b_re, m_s5_b_im, m_s5_c_re, m_s5_c_im, m_s5_d, m_s5_w_glu, m_s5_b_glu, m_cv_w_dw, m_cv_b_dw, m_cv_ln_g, m_cv_ln_b, m_cv_w_pw, m_cv_b_pw, m_lru_w_conv, m_lru_b_conv, m_lru_w_r, m_lru_b_r, m_lru_w_i, m_lru_b_i, m_lru_lam, m_pool_w, m_pool_scale, m_w_out, m_norm_ffn_g, m_ffn_w_up, m_ffn_w_dw, m_ffn_b_dw, m_ffn_w_down, m_norm_final_g, v_norm_mix_g, v_w_in, v_s5_lam_re, v_s5_lam_im, v_s5_log_step, v_s5_b_re, v_s5_b_im, v_s5_c_re, v_s5_c_im, v_s5_d, v_s5_w_glu, v_s5_b_glu, v_cv_w_dw, v_cv_b_dw, v_cv_ln_g, v_cv_ln_b, v_cv_w_pw, v_cv_b_pw, v_lru_w_conv, v_lru_b_conv, v_lru_w_r, v_lru_b_r, v_lru_w_i, v_lru_b_i, v_lru_lam, v_pool_w, v_pool_scale, v_w_out, v_norm_ffn_g, v_ffn_w_up, v_ffn_w_dw, v_ffn_b_dw, v_ffn_w_down, v_norm_final_g)
    x, target = args[0], args[35]
    W = dict(zip(_NAMES, args[1:35]))
    Mo = dict(zip(_NAMES, args[36:70]))
    Vo = dict(zip(_NAMES, args[70:104]))
    x = x[0]
    target = target[0]
    T, D = x.shape
    depth = W['w_in'].shape[0]
    F = W['ffn_w_down'].shape[1] * N_DEV
    G, P, H = W['s5_b_re'].shape[1:]
    MW = G * H
    nh, hd, _ = W['lru_w_r'].shape[1:]
    me = _rank(_my_pos())

    gather, full = {}, {}

    def start_group(l, gi, after):
        shards = [(W[n][l].T if n in _TRANSPOSED else W[n][l]).astype(BF16) for n in _GROUPS[gi]]
        if (l, gi) == (0, 0):
            shards.append(_pack([W[n][k] for k in range(depth) for n in _COLSHARD]))
        gather[l, gi], tok = _gather_start(shards, "gather_%d%d" % (l, gi), after)
        return tok

    def land_weights(l, gi, after):
        got = _gather_wait(gather[l, gi], after)
        for n, g8 in zip(_GROUPS[gi], got):
            full[n, l] = g8.reshape(-1, g8.shape[-1])
        return got

    started = start_group(0, 0, None)[0, 0]
    got = land_weights(0, 0, W['norm_mix_g'][0] + started)
    started = start_group(0, 1, got[0])[0, 0]
    taps8 = got[-1]
    tap_shapes = [W[n][l].shape for l in range(depth) for n in _COLSHARD]
    taps = [_unshard_cols(jnp.stack(t)) for t in zip(*[_unpack(taps8[r], tap_shapes) for r in range(N_DEV)])]
    for l in range(depth):
        for j, n in enumerate(_COLSHARD):
            full[n, l] = taps[l * len(_COLSHARD) + j]

    saved = []
    for l in range(depth):
        g_mix = W['norm_mix_g'][l] + (0.0 if l else started)
        if l:
            land_weights(l, 0, x)
        s5p, s5p_vjp = jax.vjp(_s5_params, W['s5_lam_re'][l], W['s5_lam_im'][l], W['s5_log_step'][l], W['s5_b_re'][l], W['s5_b_im'][l])
        ar, ai, bre, bim = s5p
        cre, cim = _s5_blocks_out(W['s5_c_re'][l]), _s5_blocks_out(W['s5_c_im'][l])
        sp8, sp8_vjp = jax.vjp(lambda lam: LRU_C * jax.nn.softplus(-lam), W['lru_lam'][l])
        wr, wi = _head_blocks(W['lru_w_r'][l]), _head_blocks(W['lru_w_i'][l])
        c = dict(bre=bre.astype(BF16), bim=bim.astype(BF16), cre=cre.astype(BF16), cim=cim.astype(BF16),
                 cst_f=_s5_scan_consts(ar, ai, False), cst_b=_s5_scan_consts(ar, ai, True), sp8=sp8, wr=wr.astype(BF16),
                 wi=wi.astype(BF16), cvw=_pad_rows(full['cv_w_dw', l], 32), lruw=_pad_rows(full['lru_w_conv', l], 8),
                 ffw=_pad_rows(full['ffn_w_dw', l], 8), pw=W['pool_w'][l].astype(BF16), s5p_vjp=s5p_vjp, sp8_vjp=sp8_vjp, x0=x)
        h = _rmsnorm(x, g_mix, name="mix_norm")
        proj = _matmul(h, full['w_in', l], mode="nt", tm=1024, tn=512, tk=D, out_dtype=F32, name="in_proj")
        y0, sre, sim, yp, z = _s5_fwd(proj, 0, c['bre'], c['bim'], c['cre'], c['cim'], c['cst_f'], W['s5_d'][l],
                                     full['s5_w_glu', l], W['s5_b_glu'][l])
        y1, h1 = _cv_fwd(proj, 1, 2, c['cvw'], W['cv_b_dw'][l], W['cv_ln_g'][l], W['cv_ln_b'][l], full['cv_w_pw', l], W['cv_b_pw'][l])
        y2, xc, rr, ig, hprev = _lru_fwd(proj, 3, 4, c['lruw'], W['lru_b_conv'][l], c['wr'], W['lru_b_r'][l], c['wi'],
                                         W['lru_b_i'][l], sp8)
        y3 = _pool_fwd(proj, 5, c['pw'], W['pool_scale'][l])
        mixed = jnp.concatenate([y0, y1, y2, y3], axis=1)
        x1 = _matmul(mixed, full['w_out', l], mode="nn", tm=1024, tn=512, tk=2048, out_dtype=F32, name="out_proj", add=x)
        got = land_weights(l, 1, x1)
        if l + 1 < depth:
            start_group(l + 1, 0, got[0])
            start_group(l + 1, 1, got[0])
        h2 = _rmsnorm(x1, W['norm_ffn_g'][l], name="ffn_norm")
        up = _matmul(h2, full['ffn_w_up', l], mode="nt", tm=2048, tn=256, tk=D, out_dtype=BF16, name="ffn_up")
        act = _ffn_mid_fwd(up, c['ffw'], W['ffn_b_dw'][l])
        x = _matmul(act, full['ffn_w_down', l], mode="nn", tm=1024, tn=512, tk=F, out_dtype=F32, name="ffn_down", add=x1)
        c.update(proj=proj, h=h, sre=sre, sim=sim, yp=yp, z=z, h1=h1, xc=xc, rr=rr, ig=ig, hprev=hprev, mixed=mixed, x1=x1,
                 up=up, h2=h2, act=act)
        saved.append(c)

    dx, dxb, dgf, se = _loss_head(x, W['norm_final_g'], target)
    loss = lax.psum(0.5 / D * jnp.sum(se), AXES)
    gsmall = {('norm_final_g', None): dgf.sum(0)}
    gmat, scatter = {}, {}

    def parts_of(m):
        return m.reshape(N_DEV, m.shape[0] // N_DEV, m.shape[1])

    for l in reversed(range(depth)):
        c = saved[l]
        dact = _matmul(dxb, full['ffn_w_down', l], mode="nt", tm=512, tn=F, tk=512, out_dtype=BF16, name="d_act")
        gmat['ffn_w_down', l] = _matmul(c['act'], dxb, mode="tn", tm=F, tn=512, tk=512, out_dtype=BF16, name="dw_down")
        dup, dffw, dffb = _ffn_mid_bwd(dact, c['up'], c['ffw'], W['ffn_b_dw'][l])
        gsmall['ffn_w_dw', l] = dffw.sum(1)[:3]
        gsmall['ffn_b_dw', l] = dffb.sum(0)
        dh2 = _matmul(dup, full['ffn_w_up', l], mode="nn", tm=1024, tn=512, tk=F, out_dtype=F32, name="d_h2")
        gmat['ffn_w_up', l] = _matmul(dup, c['h2'], mode="tn", tm=256, tn=D, tk=T, out_dtype=BF16, name="dw_up")
        scatter[l, 2], tok = _scatter_start([parts_of(gmat[n, l]) for n in _SCATTER_GROUPS[2]], "scatter_%d2" % l)
        dx1, dx1b, dg2 = _rms_bwd(dx, dh2, c['x1'], W['norm_ffn_g'][l] + tok[0, 0], name="ffn_norm_bwd")
        gsmall['norm_ffn_g', l] = dg2.sum(0)
        dmix = _matmul(dx1b, full['w_out', l], mode="nt", tm=1024, tn=512, tk=2048, out_dtype=F32, name="d_mixed")
        gmat['w_out', l] = _matmul(c['mixed'], dx1b, mode="tn", tm=512, tn=D, tk=T, out_dtype=BF16, name="dw_out")
        du, dbre, dbim, dcre, dcim, da, dwglu, v3 = _s5_bwd(dmix, 0, c['proj'], 0, c['sre'], c['sim'], c['yp'], c['z'], c['bre'],
                                                            c['bim'], c['cre'], c['cim'], c['cst_b'], W['s5_d'][l], full['s5_w_glu', l])
        da = da.sum(1)
        glr, gli, gls, gbr, gbi = c['s5p_vjp']((da[0], da[1], dbre, dbim))
        gsmall['s5_lam_re', l], gsmall['s5_lam_im', l], gsmall['s5_log_step', l] = glr, gli, gls
        gsmall['s5_b_re', l], gsmall['s5_b_im', l] = gbr, gbi
        gsmall['s5_c_re', l] = _s5_blocks_out_diag(dcre, G, H, P)
        gsmall['s5_c_im', l] = _s5_blocks_out_diag(dcim, G, H, P)
        gsmall['s5_b_glu', l], gsmall['s5_d', l] = v3[0].sum(0), v3[1].sum(0)
        gmat['s5_w_glu', l] = dwglu.astype(BF16)
        dv, dg, dwpw, dcvw, v5 = _cv_bwd(dmix, 1, c['proj'], 1, 2, c['h1'], c['cvw'], W['cv_ln_g'][l], W['cv_ln_b'][l], full['cv_w_pw', l])
        gmat['cv_w_pw', l] = dwpw.astype(BF16)
        gsmall['cv_w_dw', l] = dcvw.sum(1)[:31]
        gsmall['cv_b_pw', l], gsmall['cv_ln_g', l], gsmall['cv_ln_b', l], gsmall['cv_b_dw', l] = [v5[j].sum(0) for j in range(4)]
        scatter[l, 0], tok = _scatter_start([parts_of(gmat[n, l]) for n in _SCATTER_GROUPS[0]], "scatter_%d0" % l)
        tok = tok[0, 0]
        dlx, dlg, dwr, dwi, dlw, v4 = _lru_bwd(dmix, 2, c['proj'], 3, 4, c['xc'], c['rr'], c['ig'], c['hprev'], c['lruw'], c['wr'],
                                               c['wi'], c['sp8'] + tok)
        gsmall['lru_w_r', l], gsmall['lru_w_i', l] = _head_blocks_diag(dwr, nh, hd, hd), _head_blocks_diag(dwi, nh, hd, hd)
        gsmall['lru_w_conv', l] = dlw.sum(1)[:4]
        gsmall['lru_b_r', l], gsmall['lru_b_i', l] = v4[0].sum(0), v4[1].sum(0)
        gsmall['lru_lam', l] = c['sp8_vjp'](v4[2].sum(0))[0]
        gsmall['lru_b_conv', l] = v4[3].sum(0)
        dpx, dpw, dps = _pool_bwd(dmix, 3, c['proj'], 5, c['pw'], W['pool_scale'][l] + tok)
        gsmall['pool_w', l], gsmall['pool_scale', l] = dpw, dps.sum(0)
        dproj = jnp.concatenate([du, dv, dg, dlx, dlg, dpx], axis=1)
        dh = _matmul(dproj, full['w_in', l], mode="nn", tm=1024, tn=512, tk=dproj.shape[1], out_dtype=F32, name="d_h")
        gmat['w_in', l] = _matmul(dproj, c['h'], mode="tn", tm=512, tn=D, tk=T, out_dtype=BF16, name="dw_in")
        scatter[l, 1], tok = _scatter_start([parts_of(gmat[n, l]) for n in _SCATTER_GROUPS[1]], "scatter_%d1" % l)
        dx, dxb, dg1 = _rms_bwd(dx1, dh, c['x0'], W['norm_mix_g'][l] + tok[0, 0], name="mix_norm_bwd")
        gsmall['norm_mix_g', l] = dg1.sum(0)

    small_names = [n for n in _NAMES if n not in _MATRIX]
    small_split = ([n for n in small_names if n not in _SMALL_BF16], [n for n in small_names if n in _SMALL_BF16])
    small_packs, full_shapes = [], []
    for names, dtype in zip(small_split, (F32, BF16)):
        layers = [[None] if n == 'norm_final_g' else range(depth) for n in names]
        small_packs.append(_pack([gsmall[n, l] for n, ls in zip(names, layers) for l in ls], dtype))
        full_shapes.append([gsmall[n, None].shape if n == 'norm_final_g' else (depth,) + gsmall[n, 0].shape for n in names])
    small_state, tok = _gather_start(small_packs, "small_grads")

    stored_t = [n for n in _TRANSPOSED if W[n].shape[-1] % LANES]
    wmv = {n: [jnp.swapaxes(t[n], 1, 2) if n in stored_t else t[n] for t in (W, Mo, Vo)] for n in _MATRIX}
    stacked = {n: None for n in _MATRIX}

    def update_matrices(l, after):
        for gi in (2, 0, 1):
            for n, p8 in zip(_SCATTER_GROUPS[gi], _scatter_wait(scatter[l, gi], after)):
                g = _sum8(p8, name="sum_%s_%d" % (n, l)).T if n in _TRANSPOSED and n not in stored_t else p8
                stacked[n] = _adamw_layer(g, *wmv[n], l, stacked[n], name="adamw_%s_%d" % (n, l))

    for l in reversed(range(1, depth)):
        update_matrices(l, tok)
    after_matrices = tok if depth == 1 else sum(stacked[n][1][depth - 1, 0, 0] for n in _MATRIX).reshape(1, 1)
    gs = {}
    for names, shapes, small8 in zip(small_split, full_shapes, _gather_wait(small_state, after_matrices)):
        gs.update(zip(names, _unpack(_sum8(small8, name="sum_small_" + str(small8.dtype)), shapes)))
    for n in _COLSHARD:
        cw = W[n].shape[-1]
        gs[n] = lax.dynamic_slice_in_dim(gs[n], me * cw, cw, axis=gs[n].ndim - 1)
    upd = _adamw_many([W[n] for n in small_names], [gs[n] for n in small_names], [Mo[n] for n in small_names],
                      [Vo[n] for n in small_names], name="adamw_small")
    out = {n: (gs[n], upd[0][j], upd[1][j], upd[2][j]) for j, n in enumerate(small_names)}
    update_matrices(0, upd[0][0])
    for n in _MATRIX:
        out[n] = tuple(jnp.swapaxes(t, 1, 2) for t in stacked[n]) if n in stored_t else tuple(stacked[n])
    return (loss, dx[None]) + tuple(out[n][q] for q in range(4) for n in _NAMES)
```

```python
import functools
import math

import jax
import jax.numpy as jnp
from jax import lax
from jax.experimental import pallas as pl
from jax.experimental.pallas import tpu as pltpu

F32 = jnp.float32
BF16 = jnp.bfloat16
N_DEV = 8
AXES = ("x", "y", "c")
EPS = 1e-6
S5_GROUP_CH = 16
S5_STATE = 64
LRU_HEADS = 8
LRU_C = 8.0
POOL_WINDOWS = (2, 4, 8, 16)
CONV_HALO = 32
SMALL_HALO = 8
FFN_HALO = 16
POOL_HALO = 16
LANES = 128
ADAM_LR, ADAM_B1, ADAM_B2, ADAM_EPS, ADAM_WD, ADAM_STEP = 0.001, 0.9, 0.999, 1e-08, 0.01, 10
_GELU_K = math.sqrt(2.0 / math.pi)
ARB = pltpu.CompilerParams(dimension_semantics=("arbitrary",))


def _pick(n, pref, mult):
    best = None
    for t in range(mult, min(n, pref) + 1, mult):
        if n % t == 0:
            best = t
    return best if best is not None else n


def _sigmoid(x):
    return 1.0 / (1.0 + jnp.exp(-x))


def _gelu(x):
    return 0.5 * x * (1.0 + jnp.tanh(_GELU_K * (x + 0.044715 * x * x * x)))


def _gelu_grad(x):
    t = jnp.tanh(_GELU_K * (x + 0.044715 * x * x * x))
    return 0.5 * (1.0 + t) + 0.5 * x * (1.0 - t * t) * _GELU_K * (1.0 + 3.0 * 0.044715 * x * x)


def _gelu_and_grad(x):
    t = jnp.tanh(_GELU_K * (x + 0.044715 * x * x * x))
    half = 0.5 * (1.0 + t)
    return x * half, half + 0.5 * x * (1.0 - t * t) * _GELU_K * (1.0 + 3.0 * 0.044715 * x * x)


def _neg_expm1(x):
    series = -x * (1.0 + x * (0.5 + x * (1.0 / 6.0 + x * (1.0 / 24.0 + x * (1.0 / 120.0 + x * (1.0 / 720.0))))))
    return jnp.where(jnp.abs(x) < 0.25, series, 1.0 - jnp.exp(x))


def _fold8(x):
    return x.reshape(x.shape[0] // 8, 8, x.shape[1]).sum(axis=0)


def _dot(a, b):
    return jnp.dot(a.astype(BF16), b.astype(BF16), preferred_element_type=F32)


def _dot_nt(a, b):
    return lax.dot_general(a.astype(BF16), b.astype(BF16), (((1,), (1,)), ((), ())), preferred_element_type=F32)


def _dot_tn(a, b):
    return lax.dot_general(a.astype(BF16), b.astype(BF16), (((0,), (0,)), ((), ())), preferred_element_type=F32)


def _full(shape):
    nd = len(shape)
    return pl.BlockSpec(shape, lambda *i: (0,) * nd)


def _matmul(a, b, *, mode, tm, tn, tk, out_dtype, name, add=None):
    if mode == "nn":
        (M, K), N = a.shape, b.shape[1]
    elif mode == "nt":
        (M, K), N = a.shape, b.shape[0]
    else:
        (K, M), N = a.shape, b.shape[1]
    tm, tn, tk = _pick(M, tm, 8), _pick(N, tn, 128), _pick(K, tk, 128)
    if mode == "tn":
        tm = _pick(M, tm, 128)
    nk = K // tk
    a_spec = {"nn": pl.BlockSpec((tm, tk), lambda i, j, k: (i, k)), "nt": pl.BlockSpec((tm, tk), lambda i, j, k: (i, k)),
              "tn": pl.BlockSpec((tk, tm), lambda i, j, k: (k, i))}[mode]
    b_spec = {"nn": pl.BlockSpec((tk, tn), lambda i, j, k: (k, j)), "nt": pl.BlockSpec((tn, tk), lambda i, j, k: (j, k)),
              "tn": pl.BlockSpec((tk, tn), lambda i, j, k: (k, j))}[mode]
    o_spec = pl.BlockSpec((tm, tn), lambda i, j, k: (i, j))
    dot = {"nn": _dot, "nt": _dot_nt, "tn": _dot_tn}[mode]
    has_add = add is not None

    def body(*refs):
        if has_add:
            a_ref, b_ref, add_ref, o_ref, acc_ref = refs
        else:
            a_ref, b_ref, o_ref, acc_ref = refs
        if nk == 1:
            part = dot(a_ref[...], b_ref[...])
            o_ref[...] = (part + add_ref[...] if has_add else part).astype(out_dtype)
            return
        k = pl.program_id(2)

        @pl.when(k == 0)
        def _():
            acc_ref[...] = jnp.zeros_like(acc_ref)

        acc_ref[...] += dot(a_ref[...], b_ref[...])

        @pl.when(k == nk - 1)
        def _():
            r = acc_ref[...]
            if has_add:
                r = r + add_ref[...]
            o_ref[...] = r.astype(out_dtype)

    ins = [a, b] + ([add] if has_add else [])
    specs = [a_spec, b_spec] + ([o_spec] if has_add else [])
    return pl.pallas_call(
        body, name=name, grid=(M // tm, N // tn, nk), in_specs=specs, out_specs=o_spec,
        out_shape=jax.ShapeDtypeStruct((M, N), out_dtype),
        scratch_shapes=[pltpu.VMEM((tm, tn) if nk > 1 else (8, LANES), F32)],
        compiler_params=pltpu.CompilerParams(dimension_semantics=("parallel", "parallel", "arbitrary")),
    )(*ins)


def _rmsnorm(x, g, *, name):
    T, D = x.shape
    tc = _pick(T, 256, 16)

    def body(x_ref, g_ref, h_ref):
        xv = x_ref[...]
        r = lax.rsqrt(jnp.mean(xv * xv, axis=-1, keepdims=True) + EPS)
        h_ref[...] = (xv * r * g_ref[...]).astype(BF16)

    row = pl.BlockSpec((tc, D), lambda i: (i, 0))
    return pl.pallas_call(
        body, name=name, grid=(T // tc,), in_specs=[row, _full((1, D))], out_specs=row,
        out_shape=jax.ShapeDtypeStruct((T, D), BF16), compiler_params=pltpu.CompilerParams(dimension_semantics=("parallel",)),
    )(x, g.reshape(1, D))


def _rms_bwd(dres, dh, x, g, *, name):
    T, D = x.shape
    tc = _pick(T, 256, 8)

    def body(dres_ref, dh_ref, x_ref, g_ref, dx_ref, dxb_ref, dg_ref):
        @pl.when(pl.program_id(0) == 0)
        def _():
            dg_ref[...] = jnp.zeros_like(dg_ref)

        xv, dhv = x_ref[...], dh_ref[...]
        r = lax.rsqrt(jnp.mean(xv * xv, axis=-1, keepdims=True) + EPS)
        dg_ref[...] += _fold8(dhv * xv * r)
        dyg = dhv * g_ref[...]
        dx = dres_ref[...] + r * dyg - xv * (r * r * r) * jnp.mean(dyg * xv, axis=-1, keepdims=True)
        dx_ref[...] = dx
        dxb_ref[...] = dx.astype(BF16)

    row = pl.BlockSpec((tc, D), lambda i: (i, 0))
    return pl.pallas_call(
        body, name=name, grid=(T // tc,), in_specs=[row, row, row, _full((1, D))],
        out_specs=[row, row, _full((8, D))],
        out_shape=[jax.ShapeDtypeStruct((T, D), F32), jax.ShapeDtypeStruct((T, D), BF16), jax.ShapeDtypeStruct((8, D), F32)],
        compiler_params=ARB,
    )(dres, dh, x, g.reshape(1, D))


def _loss_head(x, g, target):
    T, D = x.shape
    tc = _pick(T, 256, 8)

    def body(x_ref, g_ref, t_ref, dx_ref, dxb_ref, dg_ref, se_ref):
        @pl.when(pl.program_id(0) == 0)
        def _():
            dg_ref[...] = jnp.zeros_like(dg_ref)
            se_ref[...] = jnp.zeros_like(se_ref)

        xv = x_ref[...]
        r = lax.rsqrt(jnp.mean(xv * xv, axis=-1, keepdims=True) + EPS)
        gv = g_ref[...]
        err = xv * r * gv - t_ref[...]
        se_ref[...] += _fold8(err * err)
        dy = err * (1.0 / D)
        dg_ref[...] += _fold8(dy * xv * r)
        dyg = dy * gv
        dx = r * dyg - xv * (r * r * r) * jnp.mean(dyg * xv, axis=-1, keepdims=True)
        dx_ref[...] = dx
        dxb_ref[...] = dx.astype(BF16)

    row = pl.BlockSpec((tc, D), lambda i: (i, 0))
    return pl.pallas_call(
        body, name="loss_head", grid=(T // tc,), in_specs=[row, _full((1, D)), row],
        out_specs=[row, row, _full((8, D)), _full((8, D))],
        out_shape=[jax.ShapeDtypeStruct((T, D), F32), jax.ShapeDtypeStruct((T, D), BF16),
                   jax.ShapeDtypeStruct((8, D), F32), jax.ShapeDtypeStruct((8, D), F32)],
        compiler_params=ARB,
    )(x, g.reshape(1, D), target)


def _s5_scan_consts(ar, ai, reverse):
    if reverse:
        ai = -ai
    pr, pi_ = [ar], [ai]
    for _ in range(7):
        pr, pi_ = pr + [pr[-1] * ar - pi_[-1] * ai], pi_ + [pr[-1] * ai + pi_[-1] * ar]
    rows = jnp.arange(8)[:, None]
    planes = []
    for d in (1, 2, 4):
        mask = (rows <= 7 - d) if reverse else (rows >= d)
        planes += [jnp.where(mask, pr[d - 1][None, :], 0.0), jnp.where(mask, pi_[d - 1][None, :], 0.0)]
    order = list(range(7, -1, -1)) if reverse else list(range(8))
    planes += [jnp.stack([pr[k] for k in order]), jnp.stack([pi_[k] for k in order])]
    return jnp.stack(planes).astype(F32)


def _s5_block_scan(xr_ref, xi_ref, cst_ref, car_ref, nblk, reverse, lb, extra=None):
    ns = xr_ref.shape[1]
    crow = 0 if reverse else 7

    def blk(i, _):
        ib = (nblk - 1 - i) if reverse else i
        r0 = pl.multiple_of(ib * 8, 8)
        for k in range(ns // lb):
            sl = slice(k * lb, (k + 1) * lb)
            xr, xi = xr_ref[pl.ds(r0, 8), sl], xi_ref[pl.ds(r0, 8), sl]
            for n, d in enumerate((1, 2, 4)):
                ar, ai = cst_ref[2 * n, :, sl], cst_ref[2 * n + 1, :, sl]
                sh = (8 - d) if reverse else d
                rr, ri = pltpu.roll(xr, sh, 0), pltpu.roll(xi, sh, 0)
                xr, xi = xr + ar * rr - ai * ri, xi + ar * ri + ai * rr
            pr, pi_ = cst_ref[6, :, sl], cst_ref[7, :, sl]
            cr, ci = car_ref[0, crow:crow + 1, sl], car_ref[1, crow:crow + 1, sl]
            xr, xi = xr + pr * cr - pi_ * ci, xi + pr * ci + pi_ * cr
            xr_ref[pl.ds(r0, 8), sl] = xr
            xi_ref[pl.ds(r0, 8), sl] = xi
            if extra is not None:
                extra(r0, sl, xr, xi, cr, ci)
            car_ref[0, :, sl] = xr
            car_ref[1, :, sl] = xi
        return 0

    lax.fori_loop(0, nblk, blk, 0)


def _s5_fwd(proj, col, bre, bim, cre, cim, cst, dvec, wglu, bglu):
    T = proj.shape[0]
    nsb, cw, sw = bre.shape
    M, NS = nsb * cw, nsb * sw
    tc = _pick(T, 256, 8)

    def body(u_ref, bre_ref, bim_ref, cre_ref, cim_ref, cst_ref, d_ref, wg_ref, bg_ref,
             y_ref, sre_ref, sim_ref, yp_ref, z_ref, car_ref):
        @pl.when(pl.program_id(0) == 0)
        def _():
            car_ref[...] = jnp.zeros_like(car_ref)

        u = u_ref[...]
        ub = u.astype(BF16)
        for k in range(nsb):
            sre_ref[:, k * sw:(k + 1) * sw] = _dot(ub[:, k * cw:(k + 1) * cw], bre_ref[k])
            sim_ref[:, k * sw:(k + 1) * sw] = _dot(ub[:, k * cw:(k + 1) * cw], bim_ref[k])
        _s5_block_scan(sre_ref, sim_ref, cst_ref, car_ref, tc // 8, False, sw)
        for k in range(nsb):
            yk = _dot(sre_ref[:, k * sw:(k + 1) * sw], cre_ref[k]) - _dot(sim_ref[:, k * sw:(k + 1) * sw], cim_ref[k])
            yp_ref[:, k * cw:(k + 1) * cw] = yk + d_ref[:, k * cw:(k + 1) * cw] * u[:, k * cw:(k + 1) * cw]
        gl = _gelu(yp_ref[...])
        z = _dot(gl, wg_ref[...]) + bg_ref[...]
        z_ref[...] = z
        y_ref[...] = (gl * _sigmoid(z)).astype(BF16)

    rowm = pl.BlockSpec((tc, M), lambda i: (i, 0))
    rows = pl.BlockSpec((tc, NS), lambda i: (i, 0))
    return pl.pallas_call(
        body, name="s5_fwd", grid=(T // tc,),
        in_specs=[pl.BlockSpec((tc, M), lambda i: (i, col)), _full(bre.shape), _full(bim.shape), _full(cre.shape),
                  _full(cim.shape), _full(cst.shape), _full((1, M)), _full((M, M)), _full((1, M))],
        out_specs=[rowm, rows, rows, rowm, rowm],
        out_shape=[jax.ShapeDtypeStruct((T, M), BF16), jax.ShapeDtypeStruct((T, NS), F32), jax.ShapeDtypeStruct((T, NS), F32),
                   jax.ShapeDtypeStruct((T, M), F32), jax.ShapeDtypeStruct((T, M), F32)],
        scratch_shapes=[pltpu.VMEM((2, 8, NS), F32)], compiler_params=ARB,
    )(proj, bre, bim, cre, cim, cst, dvec.reshape(1, M), wglu, bglu.reshape(1, M))


def _s5_bwd(dmix, dcol, proj, col, sre, sim, yp, z, bre, bim, cre, cim, cst, dvec, wglu):
    T = proj.shape[0]
    nsb, cw, sw = bre.shape
    M, NS = nsb * cw, nsb * sw
    tc = _pick(T, 256, 8)
    nc = T // tc

    def body(do_ref, u_ref, sre_ref, sim_ref, yp_ref, z_ref, bre_ref, bim_ref, cre_ref, cim_ref, cst_ref, d_ref, wg_ref,
             du_ref, dbre_ref, dbim_ref, dcre_ref, dcim_ref, da_ref, dwg_ref, vec_ref, gre_ref, gim_ref, car_ref):
        @pl.when(pl.program_id(0) == 0)
        def _():
            for r in (dbre_ref, dbim_ref, dcre_ref, dcim_ref, da_ref, dwg_ref, vec_ref, car_ref):
                r[...] = jnp.zeros_like(r)

        u, do, ypv = u_ref[...], do_ref[...], yp_ref[...]
        gl = _gelu(ypv)
        sg = _sigmoid(z_ref[...])
        dz = do * gl * sg * (1.0 - sg)
        dgl = do * sg + _dot_nt(dz, wg_ref[...])
        dwg_ref[...] += _dot_tn(gl, dz)
        vec_ref[0] += _fold8(dz)
        dy = dgl * _gelu_grad(ypv)
        vec_ref[1] += _fold8(dy * u)
        dyb = dy.astype(BF16)
        for k in range(nsb):
            cs, ss = slice(k * cw, (k + 1) * cw), slice(k * sw, (k + 1) * sw)
            gre_ref[:, ss] = _dot_nt(dyb[:, cs], cre_ref[k])
            gim_ref[:, ss] = -_dot_nt(dyb[:, cs], cim_ref[k])
            dcre_ref[k] += _dot_tn(sre_ref[:, ss], dyb[:, cs])
            dcim_ref[k] -= _dot_tn(sim_ref[:, ss], dyb[:, cs])

        row = lax.broadcasted_iota(jnp.int32, (8, sw), 0)

        def extra(r0, sl, gr, gi, cr, ci):
            nr = jnp.where(row == 7, cr, pltpu.roll(gr, 7, 0))
            ni = jnp.where(row == 7, ci, pltpu.roll(gi, 7, 0))
            sr, si = sre_ref[pl.ds(r0, 8), sl], sim_ref[pl.ds(r0, 8), sl]
            da_ref[0, :, sl] += nr * sr + ni * si
            da_ref[1, :, sl] += ni * sr - nr * si

        _s5_block_scan(gre_ref, gim_ref, cst_ref, car_ref, tc // 8, True, sw, extra)
        ub = u.astype(BF16)
        for k in range(nsb):
            cs, ss = slice(k * cw, (k + 1) * cw), slice(k * sw, (k + 1) * sw)
            gr, gi = gre_ref[:, ss].astype(BF16), gim_ref[:, ss].astype(BF16)
            duk = dy[:, cs] * d_ref[:, cs] + _dot_nt(gr, bre_ref[k]) + _dot_nt(gi, bim_ref[k])
            du_ref[:, cs] = duk.astype(BF16)
            dbre_ref[k] += _dot_tn(ub[:, cs], gr)
            dbim_ref[k] += _dot_tn(ub[:, cs], gi)

    def rowm(c):
        return pl.BlockSpec((tc, M), lambda i: (nc - 1 - i, c))
    rows = pl.BlockSpec((tc, NS), lambda i: (nc - 1 - i, 0))
    return pl.pallas_call(
        body, name="s5_bwd", grid=(nc,),
        in_specs=[rowm(dcol), rowm(col), rows, rows, rowm(0), rowm(0), _full(bre.shape), _full(bim.shape), _full(cre.shape),
                  _full(cim.shape), _full(cst.shape), _full((1, M)), _full((M, M))],
        out_specs=[rowm(0), _full(bre.shape), _full(bim.shape), _full(cre.shape), _full(cim.shape), _full((2, 8, NS)),
                   _full((M, M)), _full((3, 8, M))],
        out_shape=[jax.ShapeDtypeStruct((T, M), BF16), jax.ShapeDtypeStruct(bre.shape, F32), jax.ShapeDtypeStruct(bim.shape, F32),
                   jax.ShapeDtypeStruct(cre.shape, F32), jax.ShapeDtypeStruct(cim.shape, F32), jax.ShapeDtypeStruct((2, 8, NS), F32),
                   jax.ShapeDtypeStruct((M, M), F32), jax.ShapeDtypeStruct((3, 8, M), F32)],
        scratch_shapes=[pltpu.VMEM((tc, NS), F32), pltpu.VMEM((tc, NS), F32), pltpu.VMEM((2, 8, NS), F32)],
        compiler_params=ARB,
    )(dmix, proj, sre, sim, yp, z, bre, bim, cre, cim, cst, dvec.reshape(1, M), wglu)


def _ln_silu(h1, lg, lb):
    mu = jnp.mean(h1, axis=-1, keepdims=True)
    xc = h1 - mu
    rstd = lax.rsqrt(jnp.mean(xc * xc, axis=-1, keepdims=True) + EPS)
    xhat = xc * rstd
    h2 = xhat * lg + lb
    return xhat, rstd, h2, _sigmoid(h2)


def _shifted_copies(buf_ref, sh_ref, cs, n):
    for s in range(1, 8):
        sh_ref[s, 0:n, :] = buf_ref[pl.ds(s, n), cs]


def _tap(buf_ref, sh_ref, cs, off, r0, rows):
    q, s = divmod(off, 8)
    return buf_ref[pl.ds(r0 + 8 * q, rows), cs] if s == 0 else sh_ref[s, pl.ds(r0 + 8 * q, rows), :]


def _cv_fwd(proj, vcol, gcol, wdw, bdw, lg, lb, wpw, bpw):
    T = proj.shape[0]
    M = wpw.shape[0]
    tc = _pick(T, 256, 8)
    rb = _pick(tc, 64, 8)
    H, K = CONV_HALO, 31

    def body(v_ref, g_ref, wdw_ref, bdw_ref, lg_ref, lb_ref, wpw_ref, bpw_ref, y_ref, h1_ref, buf_ref, sh_ref):
        @pl.when(pl.program_id(0) == 0)
        def _():
            buf_ref[0:H, :] = jnp.zeros((H, M), F32)

        buf_ref[H:H + tc, :] = v_ref[...] * _sigmoid(g_ref[...])

        def strip(s, _):
            cs = pl.ds(pl.multiple_of(s * LANES, LANES), LANES)
            _shifted_copies(buf_ref, sh_ref, cs, tc + H - 8)
            for r0 in range(0, tc, rb):
                acc = jnp.zeros((rb, LANES), F32) + bdw_ref[:, cs]
                for k in range(K):
                    acc = acc + wdw_ref[k:k + 1, cs] * _tap(buf_ref, sh_ref, cs, H - (K - 1) + k, r0, rb)
                h1_ref[pl.ds(r0, rb), cs] = acc
            return 0

        lax.fori_loop(0, M // LANES, strip, 0)
        _, _, h2, sg = _ln_silu(h1_ref[...], lg_ref[...], lb_ref[...])
        y_ref[...] = (_dot(h2 * sg, wpw_ref[...]) + bpw_ref[...]).astype(BF16)
        buf_ref[0:H, :] = buf_ref[tc:tc + H, :]

    def rowm(c):
        return pl.BlockSpec((tc, M), lambda i: (i, c))
    vec = _full((1, M))
    return pl.pallas_call(
        body, name="cv_fwd", grid=(T // tc,),
        in_specs=[rowm(vcol), rowm(gcol), _full((32, M)), vec, vec, vec, _full((M, M)), vec],
        out_specs=[rowm(0), rowm(0)],
        out_shape=[jax.ShapeDtypeStruct((T, M), BF16), jax.ShapeDtypeStruct((T, M), F32)],
        scratch_shapes=[pltpu.VMEM((tc + H, M), F32), pltpu.VMEM((8, tc + H, LANES), F32)], compiler_params=ARB,
    )(proj, proj, wdw, bdw.reshape(1, M), lg.reshape(1, M), lb.reshape(1, M), wpw, bpw.reshape(1, M))


def _cv_bwd(dmix, dcol, proj, vcol, gcol, h1, wdw, lg, lb, wpw):
    T = proj.shape[0]
    M = wpw.shape[0]
    tc = _pick(T, 256, 8)
    nc = T // tc
    rb = _pick(tc, 32, 16)
    H, K = CONV_HALO, 31
    hb = tc // H

    def body(do_ref, v_ref, g_ref, vh_ref, gh_ref, h1_ref, wdw_ref, lg_ref, lb_ref, wpw_ref,
             dv_ref, dg_ref, dwpw_ref, dwdw_ref, vec_ref, hbuf_ref, dbuf_ref, hsh_ref, dsh_ref):
        i = pl.program_id(0)

        @pl.when(i == 0)
        def _():
            for r in (dwpw_ref, dwdw_ref, vec_ref):
                r[...] = jnp.zeros_like(r)
            dbuf_ref[tc:tc + H, :] = jnp.zeros((H, M), F32)

        halo = vh_ref[...] * _sigmoid(gh_ref[...])
        hbuf_ref[0:H, :] = jnp.where(i == nc - 1, 0.0, halo)
        hbuf_ref[H:H + tc, :] = v_ref[...] * _sigmoid(g_ref[...])
        do = do_ref[...]
        xhat, rstd, h2, sg = _ln_silu(h1_ref[...], lg_ref[...], lb_ref[...])
        dwpw_ref[...] += _dot_tn(h2 * sg, do)
        vec_ref[0] += _fold8(do)
        dh2 = _dot_nt(do, wpw_ref[...]) * (sg * (1.0 + h2 * (1.0 - sg)))
        vec_ref[1] += _fold8(dh2 * xhat)
        vec_ref[2] += _fold8(dh2)
        dxh = dh2 * lg_ref[...]
        dh1 = rstd * (dxh - jnp.mean(dxh, axis=-1, keepdims=True) - xhat * jnp.mean(dxh * xhat, axis=-1, keepdims=True))
        vec_ref[3] += _fold8(dh1)
        dbuf_ref[0:tc, :] = dh1

        def strip(s, _):
            cs = pl.ds(pl.multiple_of(s * LANES, LANES), LANES)
            _shifted_copies(dbuf_ref, dsh_ref, cs, tc + H - 8)
            _shifted_copies(hbuf_ref, hsh_ref, cs, tc + H - 8)
            for r0 in range(0, tc, rb):
                d1 = dbuf_ref[pl.ds(r0, rb), cs]
                dh0 = jnp.zeros((rb, LANES), F32)
                for k in range(K):
                    dh0 = dh0 + wdw_ref[k:k + 1, cs] * _tap(dbuf_ref, dsh_ref, cs, K - 1 - k, r0, rb)
                    dwdw_ref[k, :, cs] += _fold8(d1 * _tap(hbuf_ref, hsh_ref, cs, H - (K - 1) + k, r0, rb))
                sg0 = _sigmoid(g_ref[pl.ds(r0, rb), cs])
                dv_ref[pl.ds(r0, rb), cs] = (dh0 * sg0).astype(BF16)
                dg_ref[pl.ds(r0, rb), cs] = (dh0 * v_ref[pl.ds(r0, rb), cs] * sg0 * (1.0 - sg0)).astype(BF16)
            return 0

        lax.fori_loop(0, M // LANES, strip, 0)
        dbuf_ref[tc:tc + H, :] = dbuf_ref[0:H, :]

    def rowm(c):
        return pl.BlockSpec((tc, M), lambda i: (nc - 1 - i, c))

    def halo(c):
        return pl.BlockSpec((H, M), lambda i: (jnp.maximum((nc - 1 - i) * hb - 1, 0), c))
    vec = _full((1, M))
    return pl.pallas_call(
        body, name="cv_bwd", grid=(nc,),
        in_specs=[rowm(dcol), rowm(vcol), rowm(gcol), halo(vcol), halo(gcol), rowm(0), _full((32, M)), vec, vec, _full((M, M))],
        out_specs=[rowm(0), rowm(0), _full((M, M)), _full((32, 8, M)), _full((5, 8, M))],
        out_shape=[jax.ShapeDtypeStruct((T, M), BF16), jax.ShapeDtypeStruct((T, M), BF16), jax.ShapeDtypeStruct((M, M), F32),
                   jax.ShapeDtypeStruct((32, 8, M), F32), jax.ShapeDtypeStruct((5, 8, M), F32)],
        scratch_shapes=[pltpu.VMEM((tc + H, M), F32), pltpu.VMEM((tc + H, M), F32), pltpu.VMEM((8, tc + H, LANES), F32),
                        pltpu.VMEM((8, tc + H, LANES), F32)], compiler_params=ARB,
    )(dmix, proj, proj, proj, proj, h1, wdw, lg.reshape(1, M), lb.reshape(1, M), wpw)


def _lru_gates(xc, r, ig, sp8):
    la = -r * sp8
    a = jnp.exp(la)
    mult = jnp.sqrt(_neg_expm1(2.0 * la))
    return a, mult, mult * (ig * xc)


def _lru_fwd(proj, xcol, gcol, wcv, bcv, wr, br, wi, bi, sp8):
    T = proj.shape[0]
    M = wr.shape[0]
    tc = _pick(T, 256, 8)
    H, K = SMALL_HALO, 4

    def body(x_ref, g_ref, wcv_ref, bcv_ref, wr_ref, br_ref, wi_ref, bi_ref, sp_ref,
             y_ref, xc_ref, r_ref, i_ref, hp_ref, buf_ref, a_ref, car_ref):
        @pl.when(pl.program_id(0) == 0)
        def _():
            buf_ref[0:H, :] = jnp.zeros((H, M), F32)
            car_ref[...] = jnp.zeros_like(car_ref)

        buf_ref[H:H + tc, :] = x_ref[...]
        xc = jnp.zeros((tc, M), F32) + bcv_ref[...]
        for k in range(K):
            xc = xc + wcv_ref[k:k + 1, :] * buf_ref[pl.ds(H - (K - 1) + k, tc), :]
        buf_ref[0:H, :] = buf_ref[tc:tc + H, :]
        r = _sigmoid(_dot(xc, wr_ref[...]) + br_ref[...])
        ig = _sigmoid(_dot(xc, wi_ref[...]) + bi_ref[...])
        xc_ref[...] = xc
        r_ref[...] = r
        i_ref[...] = ig
        a, _, bt = _lru_gates(xc, r, ig, sp_ref[...])
        a_ref[...] = a
        hp_ref[...] = bt
        row = lax.broadcasted_iota(jnp.int32, (8, M), 0)

        def blk(ib, _):
            r0 = pl.multiple_of(ib * 8, 8)
            av, bv = a_ref[pl.ds(r0, 8), :], hp_ref[pl.ds(r0, 8), :]
            for d in (1, 2, 4):
                m = row >= d
                bv = jnp.where(m, bv + av * pltpu.roll(bv, d, 0), bv)
                av = jnp.where(m, av * pltpu.roll(av, d, 0), av)
            cr = car_ref[7:8, :]
            h = bv + av * cr
            a_ref[pl.ds(r0, 8), :] = h
            hp_ref[pl.ds(r0, 8), :] = jnp.where(row == 0, cr, pltpu.roll(h, 1, 0))
            car_ref[...] = h
            return 0

        lax.fori_loop(0, tc // 8, blk, 0)
        y_ref[...] = (a_ref[...] * _gelu(g_ref[...])).astype(BF16)

    def rowm(c):
        return pl.BlockSpec((tc, M), lambda i: (i, c))
    vec = _full((1, M))
    return pl.pallas_call(
        body, name="lru_fwd", grid=(T // tc,),
        in_specs=[rowm(xcol), rowm(gcol), _full((8, M)), vec, _full((M, M)), vec, _full((M, M)), vec, vec],
        out_specs=[rowm(0)] * 5,
        out_shape=[jax.ShapeDtypeStruct((T, M), BF16)] + [jax.ShapeDtypeStruct((T, M), F32)] * 4,
        scratch_shapes=[pltpu.VMEM((tc + H, M), F32), pltpu.VMEM((tc, M), F32), pltpu.VMEM((8, M), F32)], compiler_params=ARB,
    )(proj, proj, wcv, bcv.reshape(1, M), wr, br.reshape(1, M), wi, bi.reshape(1, M), sp8.reshape(1, M))


def _lru_bwd(dmix, dcol, proj, xcol, gcol, xc, r, ig, hprev, wcv, wr, wi, sp8):
    T = proj.shape[0]
    M = wr.shape[0]
    tc = _pick(T, 256, 8)
    nc = T // tc
    H, K = SMALL_HALO, 4
    hb = tc // H

    def body(do_ref, x_ref, g_ref, xh_ref, xc_ref, r_ref, i_ref, hp_ref, wcv_ref, wr_ref, wi_ref, sp_ref,
             dx_ref, dg_ref, dwr_ref, dwi_ref, dwcv_ref, vec_ref, xbuf_ref, dbuf_ref, a_ref, gs_ref, car_ref):
        i = pl.program_id(0)

        @pl.when(i == 0)
        def _():
            for rf in (dwr_ref, dwi_ref, dwcv_ref, vec_ref, car_ref):
                rf[...] = jnp.zeros_like(rf)
            dbuf_ref[tc:tc + H, :] = jnp.zeros((H, M), F32)

        xcv, rv, iv, hpv, gv, do = xc_ref[...], r_ref[...], i_ref[...], hp_ref[...], g_ref[...], do_ref[...]
        sp = sp_ref[...]
        a, mult, bt = _lru_gates(xcv, rv, iv, sp)
        h = a * hpv + bt
        ge = _gelu(gv)
        dg_ref[...] = (do * h * _gelu_grad(gv)).astype(BF16)
        a_ref[...] = a
        gs_ref[...] = do * ge
        row = lax.broadcasted_iota(jnp.int32, (8, M), 0)

        def blk(n, _):
            r0 = pl.multiple_of((tc // 8 - 1 - n) * 8, 8)
            ablk = a_ref[pl.ds(r0, 8), :]
            av = jnp.where(row == 7, car_ref[0, 0:1, :], pltpu.roll(ablk, 7, 0))
            bv = gs_ref[pl.ds(r0, 8), :]
            for d in (1, 2, 4):
                m = row <= 7 - d
                bv = jnp.where(m, bv + av * pltpu.roll(bv, 8 - d, 0), bv)
                av = jnp.where(m, av * pltpu.roll(av, 8 - d, 0), av)
            gsv = bv + av * car_ref[1, 0:1, :]
            gs_ref[pl.ds(r0, 8), :] = gsv
            car_ref[0] = ablk
            car_ref[1] = gsv
            return 0

        lax.fori_loop(0, tc // 8, blk, 0)
        gs = gs_ref[...]
        dixc = gs * mult
        dla = gs * hpv * a - gs * (iv * xcv) * (a * a) / mult
        vec_ref[2] += _fold8(-dla * rv)
        dpr = (-dla * sp) * rv * (1.0 - rv)
        dpi = (dixc * xcv) * iv * (1.0 - iv)
        vec_ref[0] += _fold8(dpr)
        vec_ref[1] += _fold8(dpi)
        dwr_ref[...] += _dot_tn(xcv, dpr)
        dwi_ref[...] += _dot_tn(xcv, dpi)
        dxc = dixc * iv + _dot_nt(dpr, wr_ref[...]) + _dot_nt(dpi, wi_ref[...])
        vec_ref[3] += _fold8(dxc)
        xbuf_ref[0:H, :] = jnp.where(i == nc - 1, 0.0, xh_ref[...])
        xbuf_ref[H:H + tc, :] = x_ref[...]
        dbuf_ref[0:tc, :] = dxc
        dx = jnp.zeros((tc, M), F32)
        for k in range(K):
            dx = dx + wcv_ref[k:k + 1, :] * dbuf_ref[pl.ds(K - 1 - k, tc), :]
            dwcv_ref[k] += _fold8(dxc * xbuf_ref[pl.ds(H - (K - 1) + k, tc), :])
        dx_ref[...] = dx.astype(BF16)
        dbuf_ref[tc:tc + H, :] = dbuf_ref[0:H, :]

    def rowm(c):
        return pl.BlockSpec((tc, M), lambda i: (nc - 1 - i, c))
    halo = pl.BlockSpec((H, M), lambda i: (jnp.maximum((nc - 1 - i) * hb - 1, 0), xcol))
    vec = _full((1, M))
    return pl.pallas_call(
        body, name="lru_bwd", grid=(nc,),
        in_specs=[rowm(dcol), rowm(xcol), rowm(gcol), halo, rowm(0), rowm(0), rowm(0), rowm(0), _full((8, M)),
                  _full((M, M)), _full((M, M)), vec],
        out_specs=[rowm(0), rowm(0), _full((M, M)), _full((M, M)), _full((8, 8, M)), _full((4, 8, M))],
        out_shape=[jax.ShapeDtypeStruct((T, M), BF16), jax.ShapeDtypeStruct((T, M), BF16), jax.ShapeDtypeStruct((M, M), F32),
                   jax.ShapeDtypeStruct((M, M), F32), jax.ShapeDtypeStruct((8, 8, M), F32), jax.ShapeDtypeStruct((4, 8, M), F32)],
        scratch_shapes=[pltpu.VMEM((tc + H, M), F32), pltpu.VMEM((tc + H, M), F32), pltpu.VMEM((tc, M), F32),
                        pltpu.VMEM((tc, M), F32), pltpu.VMEM((2, 8, M), F32)],
        compiler_params=ARB,
    )(dmix, proj, proj, proj, xc, r, ig, hprev, wcv, wr, wi, sp8.reshape(1, M))


def _pool_diffs(buf_ref, x, t0, tc, gw):
    H = POOL_HALO
    pos = (lax.broadcasted_iota(jnp.int32, (tc, gw), 0) + t0 + 1).astype(F32)
    out, inv = [], []
    for gi, win in enumerate(POOL_WINDOWS):
        sl = slice(gi * gw, (gi + 1) * gw)
        s = x[:, sl]
        for j in range(1, win):
            s = s + buf_ref[pl.ds(H - j, tc), sl]
        ic = 1.0 / jnp.minimum(pos, float(win))
        out.append(s * ic - x[:, sl])
        inv.append(ic)
    return out, inv


def _pool_fwd(proj, col, pw, scale):
    T = proj.shape[0]
    ng, gw, _ = pw.shape
    M = ng * gw
    tc = _pick(T, 256, 8)
    H = POOL_HALO

    def body(x_ref, pw_ref, sc_ref, y_ref, buf_ref):
        @pl.when(pl.program_id(0) == 0)
        def _():
            buf_ref[0:H, :] = jnp.zeros((H, M), F32)

        x = x_ref[...]
        buf_ref[H:H + tc, :] = x
        diffs, _ = _pool_diffs(buf_ref, x, pl.program_id(0) * tc, tc, gw)
        for gi in range(ng):
            sl = slice(gi * gw, (gi + 1) * gw)
            y_ref[:, sl] = (_dot(diffs[gi], pw_ref[gi]) * sc_ref[:, sl]).astype(BF16)
        buf_ref[0:H, :] = buf_ref[tc:tc + H, :]

    return pl.pallas_call(
        body, name="pool_fwd", grid=(T // tc,),
        in_specs=[pl.BlockSpec((tc, M), lambda i: (i, col)), _full(pw.shape), _full((1, M))],
        out_specs=pl.BlockSpec((tc, M), lambda i: (i, 0)), out_shape=jax.ShapeDtypeStruct((T, M), BF16),
        scratch_shapes=[pltpu.VMEM((tc + H, M), F32)], compiler_params=ARB,
    )(proj, pw, scale.reshape(1, M))


def _pool_bwd(dmix, dcol, proj, col, pw, scale):
    T = proj.shape[0]
    ng, gw, _ = pw.shape
    M = ng * gw
    tc = _pick(T, 256, 8)
    nc = T // tc
    H = POOL_HALO
    hb = tc // H

    def body(do_ref, x_ref, xh_ref, pw_ref, sc_ref, dx_ref, dpw_ref, dsc_ref, buf_ref, ebuf_ref):
        i = pl.program_id(0)

        @pl.when(i == 0)
        def _():
            dpw_ref[...] = jnp.zeros_like(dpw_ref)
            dsc_ref[...] = jnp.zeros_like(dsc_ref)
            ebuf_ref[tc:tc + H, :] = jnp.zeros((H, M), F32)

        x, do = x_ref[...], do_ref[...]
        buf_ref[0:H, :] = jnp.where(i == nc - 1, 0.0, xh_ref[...])
        buf_ref[H:H + tc, :] = x
        diffs, inv = _pool_diffs(buf_ref, x, (nc - 1 - i) * tc, tc, gw)
        ddg = []
        for gi in range(ng):
            sl = slice(gi * gw, (gi + 1) * gw)
            dyy = do[:, sl] * sc_ref[:, sl]
            dsc_ref[:, sl] += _fold8(do[:, sl] * _dot(diffs[gi], pw_ref[gi]))
            dpw_ref[gi] += _dot_tn(diffs[gi], dyy)
            d = _dot_nt(dyy, pw_ref[gi])
            ddg.append(d)
            ebuf_ref[0:tc, sl] = d * inv[gi]
        for gi, win in enumerate(POOL_WINDOWS):
            sl = slice(gi * gw, (gi + 1) * gw)
            s = -ddg[gi]
            for j in range(win):
                s = s + ebuf_ref[pl.ds(j, tc), sl]
            dx_ref[:, sl] = s.astype(BF16)
        ebuf_ref[tc:tc + H, :] = ebuf_ref[0:H, :]

    return pl.pallas_call(
        body, name="pool_bwd", grid=(nc,),
        in_specs=[pl.BlockSpec((tc, M), lambda i: (nc - 1 - i, dcol)), pl.BlockSpec((tc, M), lambda i: (nc - 1 - i, col)),
                  pl.BlockSpec((H, M), lambda i: (jnp.maximum((nc - 1 - i) * hb - 1, 0), col)), _full(pw.shape), _full((1, M))],
        out_specs=[pl.BlockSpec((tc, M), lambda i: (nc - 1 - i, 0)), _full(pw.shape), _full((8, M))],
        out_shape=[jax.ShapeDtypeStruct((T, M), BF16), jax.ShapeDtypeStruct(pw.shape, F32), jax.ShapeDtypeStruct((8, M), F32)],
        scratch_shapes=[pltpu.VMEM((tc + H, M), F32), pltpu.VMEM((tc + H, M), F32)], compiler_params=ARB,
    )(dmix, proj, proj, pw, scale.reshape(1, M))


def _ffn_mid_fwd(up, wdw, bdw):
    T, F2 = up.shape
    F = F2 // 2
    tc = _pick(T, 256, 8)
    rb = _pick(tc, 64, 16)
    H, K = FFN_HALO, 3

    def body(gt_ref, val_ref, w_ref, b_ref, act_ref, buf_ref):
        @pl.when(pl.program_id(0) == 0)
        def _():
            buf_ref[0:H, :] = jnp.zeros((H, F), F32)

        buf_ref[H:H + tc, :] = gt_ref[...].astype(F32)

        def strip(s, _):
            cs = pl.ds(pl.multiple_of(s * LANES, LANES), LANES)
            taps = [w_ref[k:k + 1, cs] for k in range(K)]
            for r0 in range(0, tc, rb):
                gc = b_ref[:, cs] + taps[0] * buf_ref[pl.ds(H - (K - 1) + r0, rb), cs]
                for k in range(1, K):
                    gc = gc + taps[k] * buf_ref[pl.ds(H - (K - 1) + k + r0, rb), cs]
                act_ref[pl.ds(r0, rb), cs] = (_gelu(gc) * val_ref[pl.ds(r0, rb), cs].astype(F32)).astype(BF16)
            return 0

        lax.fori_loop(0, F // LANES, strip, 0)
        buf_ref[0:H, :] = buf_ref[tc:tc + H, :]

    return pl.pallas_call(
        body, name="ffn_mid_fwd", grid=(T // tc,),
        in_specs=[pl.BlockSpec((tc, F), lambda i: (i, 0)), pl.BlockSpec((tc, F), lambda i: (i, 1)), _full((8, F)), _full((1, F))],
        out_specs=pl.BlockSpec((tc, F), lambda i: (i, 0)), out_shape=jax.ShapeDtypeStruct((T, F), BF16),
        scratch_shapes=[pltpu.VMEM((tc + H, F), F32)], compiler_params=ARB,
    )(up, up, wdw, bdw.reshape(1, F))


def _ffn_mid_bwd(dact, up, wdw, bdw):
    T, F2 = up.shape
    F = F2 // 2
    tc = _pick(T, 128, 8)
    rb = _pick(tc, 64, 16)
    nc = T // tc
    H, K = FFN_HALO, 3
    hb = tc // H

    def body(da_ref, gt_ref, val_ref, gh_ref, w_ref, b_ref, dup_ref, dw_ref, db_ref, gbuf_ref, dbuf_ref):
        i = pl.program_id(0)

        @pl.when(i == 0)
        def _():
            dw_ref[...] = jnp.zeros_like(dw_ref)
            db_ref[...] = jnp.zeros_like(db_ref)
            dbuf_ref[tc:tc + H, :] = jnp.zeros((H, F), F32)

        gbuf_ref[0:H, :] = jnp.where(i == nc - 1, 0.0, gh_ref[...].astype(F32))
        gbuf_ref[H:H + tc, :] = gt_ref[...].astype(F32)

        def strip(s, _):
            cs = pl.ds(pl.multiple_of(s * LANES, LANES), LANES)
            cs_val = pl.ds(pl.multiple_of(F + s * LANES, LANES), LANES)
            taps = [w_ref[k:k + 1, cs] for k in range(K)]
            for r0 in range(0, tc, rb):
                shifted = [gbuf_ref[pl.ds(H - (K - 1) + k + r0, rb), cs] for k in range(K)]
                gc = b_ref[:, cs] + taps[0] * shifted[0]
                for k in range(1, K):
                    gc = gc + taps[k] * shifted[k]
                ge, gg = _gelu_and_grad(gc)
                da = da_ref[pl.ds(r0, rb), cs].astype(F32)
                dup_ref[pl.ds(r0, rb), cs_val] = (da * ge).astype(BF16)
                dgc = da * val_ref[pl.ds(r0, rb), cs].astype(F32) * gg
                dbuf_ref[pl.ds(r0, rb), cs] = dgc
                db_ref[:, cs] += _fold8(dgc)
                for k in range(K):
                    dw_ref[k, :, cs] += _fold8(dgc * shifted[k])
            for r0 in range(0, tc, rb):
                dgt = taps[0] * dbuf_ref[pl.ds(K - 1 + r0, rb), cs]
                for k in range(1, K):
                    dgt = dgt + taps[k] * dbuf_ref[pl.ds(K - 1 - k + r0, rb), cs]
                dup_ref[pl.ds(r0, rb), cs] = dgt.astype(BF16)
            return 0

        lax.fori_loop(0, F // LANES, strip, 0)
        dbuf_ref[tc:tc + H, :] = dbuf_ref[0:H, :]

    return pl.pallas_call(
        body, name="ffn_mid_bwd", grid=(nc,),
        in_specs=[pl.BlockSpec((tc, F), lambda i: (nc - 1 - i, 0)), pl.BlockSpec((tc, F), lambda i: (nc - 1 - i, 0)),
                  pl.BlockSpec((tc, F), lambda i: (nc - 1 - i, 1)),
                  pl.BlockSpec((H, F), lambda i: (jnp.maximum((nc - 1 - i) * hb - 1, 0), 0)), _full((8, F)), _full((1, F))],
        out_specs=[pl.BlockSpec((tc, F2), lambda i: (nc - 1 - i, 0)), _full((8, 8, F)), _full((8, F))],
        out_shape=[jax.ShapeDtypeStruct((T, F2), BF16), jax.ShapeDtypeStruct((8, 8, F), F32), jax.ShapeDtypeStruct((8, F), F32)],
        scratch_shapes=[pltpu.VMEM((tc + H, F), F32), pltpu.VMEM((tc + H, F), F32)], compiler_params=ARB,
    )(dact, up, up, up, wdw, bdw.reshape(1, F))


def _my_pos():
    return lax.axis_index("x"), lax.axis_index("y"), lax.axis_index("c")


def _flip(pos, k):
    x, y, c = pos
    return ((1 - x) if k & 4 else x, (1 - y) if k & 2 else y, (1 - c) if k & 1 else c)


def _rank(pos):
    return 4 * pos[0] + 2 * pos[1] + pos[2]


def _exchange(xs, name, gather):
    n = len(xs)
    shapes = [((N_DEV,) + tuple(x.shape)) if gather else tuple(x.shape) for x in xs]

    def body(*refs):
        ins, outs = refs[:n], refs[n:2 * n]
        send_sems, recv_sems, local_sems = refs[2 * n:]
        me = _my_pos()
        mine = _rank(me)
        local, sends = [], []
        for a in range(n):
            loc = pltpu.make_async_copy(ins[a] if gather else ins[a].at[mine], outs[a].at[mine], local_sems.at[a])
            loc.start()
            local.append(loc)
            for k in range(1, N_DEV):
                peer = _flip(me, k)
                cp = pltpu.make_async_remote_copy(
                    src_ref=ins[a] if gather else ins[a].at[_rank(peer)], dst_ref=outs[a].at[mine],
                    send_sem=send_sems.at[a, k - 1], recv_sem=recv_sems.at[a, k - 1], device_id=peer,
                    device_id_type=pl.DeviceIdType.MESH)
                cp.start()
                sends.append(cp)
        for a in range(n):
            for k in range(1, N_DEV):
                peer = _flip(me, k)
                pltpu.make_async_remote_copy(
                    src_ref=ins[a] if gather else ins[a].at[_rank(peer)], dst_ref=outs[a].at[_rank(peer)],
                    send_sem=send_sems.at[a, k - 1], recv_sem=recv_sems.at[a, k - 1], device_id=peer,
                    device_id_type=pl.DeviceIdType.MESH).wait_recv()
        for cp in sends:
            cp.wait_send()
        for loc in local:
            loc.wait()

    any_spec = pl.BlockSpec(memory_space=pl.ANY)
    return pl.pallas_call(
        body, name=name, in_specs=[any_spec] * n, out_specs=[any_spec] * n,
        out_shape=[jax.ShapeDtypeStruct(s, x.dtype) for s, x in zip(shapes, xs)],
        scratch_shapes=[pltpu.SemaphoreType.DMA((n, N_DEV - 1)), pltpu.SemaphoreType.DMA((n, N_DEV - 1)),
                        pltpu.SemaphoreType.DMA((n,))],
    )(*xs)


_HBM = pl.BlockSpec(memory_space=pltpu.HBM)
_SEM = pl.BlockSpec(memory_space=pltpu.SEMAPHORE)
_EFFECT = pltpu.SideEffectType.DATAFLOW_SIDE_EFFECTING


def _plan_direct(gather):
    def plan(bufs, ssem, rsem, me, outgoing):
        n = len(bufs) // 2
        out = []
        for a in range(n):
            for k in range(1, N_DEV):
                peer = _flip(me, k)
                out.append(pltpu.make_async_remote_copy(
                    src_ref=bufs[a] if gather else bufs[a].at[_rank(peer)],
                    dst_ref=bufs[n + a].at[_rank(me) if outgoing else _rank(peer)],
                    send_sem=ssem.at[a * 7 + k - 1], recv_sem=rsem.at[a * 7 + k - 1], device_id=peer,
                    device_id_type=pl.DeviceIdType.MESH))
        return out
    return plan, 7


_SAME_CORE = (2, 4, 6)


def _plan_own_block(bufs, ssem, rsem, me, outgoing):
    n = len(bufs) // 2
    out = []
    for a in range(n):
        for j, k in enumerate((1,) + _SAME_CORE):
            peer = _flip(me, k)
            out.append(pltpu.make_async_remote_copy(
                src_ref=bufs[a], dst_ref=bufs[n + a].at[_rank(me) if outgoing else _rank(peer)],
                send_sem=ssem.at[a * 4 + j], recv_sem=rsem.at[a * 4 + j], device_id=peer, device_id_type=pl.DeviceIdType.MESH))
    return out


def _plan_pass_on(bufs, ssem, rsem, me, outgoing):
    sibling = _flip(me, 1)
    out = []
    for a in range(len(bufs)):
        for j, k in enumerate(_SAME_CORE):
            slot = _rank(_flip(me, k) if outgoing else _flip(sibling, k))
            out.append(pltpu.make_async_remote_copy(
                src_ref=bufs[a].at[slot], dst_ref=bufs[a].at[slot], send_sem=ssem.at[a * 3 + j], recv_sem=rsem.at[a * 3 + j],
                device_id=sibling, device_id_type=pl.DeviceIdType.MESH))
    return out


def _split_start(bufs, plan, n_sems, name, after=None):
    nb = len(bufs)
    extra = [] if after is None else [after]

    def body(*refs):
        ssem, rsem = refs[nb + len(extra)], refs[nb + len(extra) + 1]
        for cp in plan(refs[:nb], ssem, rsem, _my_pos(), True):
            cp.start()
        token = refs[2 * nb + len(extra) + 2]
        token[...] = jnp.zeros_like(token)

    res = pl.pallas_call(
        body, name=name,
        out_shape=(pltpu.SemaphoreType.DMA((n_sems,)), pltpu.SemaphoreType.DMA((n_sems,)),
                   *[pltpu.HBM(t.shape, t.dtype) for t in bufs], jax.ShapeDtypeStruct((8, 128), F32)),
        in_specs=[_HBM] * nb + [pl.BlockSpec(memory_space=pl.ANY)] * len(extra),
        out_specs=(_SEM, _SEM, *([_HBM] * nb), pl.BlockSpec(memory_space=pltpu.VMEM)),
        input_output_aliases={i: 2 + i for i in range(nb)},
        compiler_params=pltpu.CompilerParams(has_side_effects=_EFFECT),
    )(*[pltpu.with_memory_space_constraint(t, pltpu.HBM) for t in bufs], *extra)
    return (res[0], res[1], list(res[2:2 + nb])), res[2 + nb]


def _split_wait(handle, plan, after, name):
    ssem, rsem, bufs = handle
    nb = len(bufs)

    def body(*refs):
        for cp in plan(refs[:nb], refs[nb], refs[nb + 1], _my_pos(), False):
            cp.wait_send()
            cp.wait_recv()

    res = pl.pallas_call(
        body, name=name, out_shape=tuple(pltpu.HBM(t.shape, t.dtype) for t in bufs),
        in_specs=[_HBM] * nb + [_SEM, _SEM, pl.BlockSpec(memory_space=pl.ANY)], out_specs=tuple([_HBM] * nb),
        input_output_aliases={i: i for i in range(nb)},
        compiler_params=pltpu.CompilerParams(has_side_effects=_EFFECT),
    )(*bufs, ssem, rsem, after)
    return list(res)


def _landing_zones(xs, gather):
    mine = _rank(_my_pos())
    lands = []
    for x in xs:
        own = x if gather else lax.dynamic_index_in_dim(x, mine, 0, keepdims=False)
        lands.append(lax.dynamic_update_index_in_dim(lax.empty((N_DEV,) + tuple(own.shape), x.dtype), own, mine, 0))
    return lands


def _gather_start(xs, name, after=None, two_level=True):
    bufs = list(xs) + _landing_zones(xs, True)
    if two_level:
        handle, token = _split_start(bufs, _plan_own_block, 4 * len(xs), name + "_start", after)
    else:
        plan, per = _plan_direct(True)
        handle, token = _split_start(bufs, plan, per * len(xs), name + "_start", after)
    return (handle, two_level, name), token


def _gather_wait(state, after):
    handle, two_level, name = state
    n = len(handle[2]) // 2
    if not two_level:
        return _split_wait(handle, _plan_direct(True)[0], after, name + "_wait")[n:]
    lands = _split_wait(handle, _plan_own_block, after, name + "_wait")[n:]
    handle, _ = _split_start(lands, _plan_pass_on, 3 * n, name + "_pass")
    return _split_wait(handle, _plan_pass_on, after, name + "_passed")


def _scatter_start(xs, name):
    plan, per = _plan_direct(False)
    handle, token = _split_start(list(xs) + _landing_zones(xs, False), plan, per * len(xs), name + "_start")
    return (handle, name), token


def _scatter_wait(state, after):
    handle, name = state
    return _split_wait(handle, _plan_direct(False)[0], after, name + "_wait")[len(handle[2]) // 2:]


def _adamw_math(w, g, m, v):
    m = ADAM_B1 * m + (1.0 - ADAM_B1) * g
    v = ADAM_B2 * v + (1.0 - ADAM_B2) * (g * g)
    m_hat = m / (1.0 - ADAM_B1 ** ADAM_STEP)
    v_hat = v / (1.0 - ADAM_B2 ** ADAM_STEP)
    return -ADAM_LR * (m_hat / (jnp.sqrt(v_hat) + ADAM_EPS) + ADAM_WD * w), m, v


def _sum8(parts, *, name):
    _, R, C = parts.shape
    tr = _pick(R, 256, 16)

    def body(p_ref, o_ref):
        acc = p_ref[0].astype(F32)
        for r in range(1, N_DEV):
            acc = acc + p_ref[r].astype(F32)
        o_ref[...] = acc

    return pl.pallas_call(
        body, name=name, grid=(R // tr,), in_specs=[pl.BlockSpec((N_DEV, tr, C), lambda i: (0, i, 0))],
        out_specs=pl.BlockSpec((tr, C), lambda i: (i, 0)), out_shape=jax.ShapeDtypeStruct((R, C), F32),
        compiler_params=pltpu.CompilerParams(dimension_semantics=("parallel",)),
    )(parts)


def _adamw_layer(g, w, m, v, layer, prev, *, name):
    depth, R, C = w.shape
    tr = _pick(R, 64, 16)
    summed = g.ndim == 3
    if prev is None:
        prev = [lax.empty((depth, R, C), F32) for _ in range(4)]

    def body(g_ref, w_ref, m_ref, v_ref, *rest):
        outs = rest[4:]
        if summed:
            gv = g_ref[0].astype(F32)
            for r in range(1, N_DEV):
                gv = gv + g_ref[r].astype(F32)
        else:
            gv = g_ref[...]
        d, nm, nv = _adamw_math(w_ref[...], gv, m_ref[...], v_ref[...])
        for o, val in zip(outs, (gv, d, nm, nv)):
            o[...] = val

    g_spec = pl.BlockSpec((N_DEV, tr, C), lambda i: (0, i, 0)) if summed else pl.BlockSpec((tr, C), lambda i: (i, 0))
    slab = pl.BlockSpec((None, tr, C), lambda i: (layer, i, 0))
    return pl.pallas_call(
        body, name=name, grid=(R // tr,), in_specs=[g_spec] + [slab] * 3 + [pl.BlockSpec(memory_space=pl.ANY)] * 4,
        out_specs=[slab] * 4, out_shape=[jax.ShapeDtypeStruct((depth, R, C), F32)] * 4,
        input_output_aliases={4 + q: q for q in range(4)},
        compiler_params=pltpu.CompilerParams(dimension_semantics=("parallel",)),
    )(g, w, m, v, *prev)


def _lane_view(shape):
    if len(shape) == 1:
        return (1, shape[0])
    size = math.prod(shape)
    if shape[-1] < 64 and size % LANES == 0:
        return (size // LANES, LANES)
    if len(shape) > 2 and shape[-2] % 8 == 0:
        return (size // shape[-1], shape[-1])
    return tuple(shape)


def _adamw_many(ws, gs, ms, vs, *, name):
    n = len(ws)
    views = [_lane_view(w.shape) for w in ws]

    def body(*refs):
        for i in range(n):
            d, nm, nv = _adamw_math(refs[i][...], refs[n + i][...], refs[2 * n + i][...], refs[3 * n + i][...])
            refs[4 * n + i][...] = d
            refs[5 * n + i][...] = nm
            refs[6 * n + i][...] = nv

    outs = pl.pallas_call(
        body, name=name, out_shape=[jax.ShapeDtypeStruct(s, F32) for s in views] * 3,
    )(*[a.reshape(s) for arrs in (ws, gs, ms, vs) for a, s in zip(arrs, views)])
    return [[o.reshape(w.shape) for o, w in zip(outs[q * n:(q + 1) * n], ws)] for q in range(3)]


def _pack(arrs, dtype=F32):
    flat = jnp.concatenate([a.reshape(-1).astype(dtype) for a in arrs])
    pad = (-flat.shape[0]) % 2048
    return jnp.pad(flat, (0, pad)).reshape(-1, 128)


def _unpack(buf, shapes):
    flat, out, off = buf.reshape(-1), [], 0
    for s in shapes:
        n = math.prod(s)
        out.append(flat[off:off + n].reshape(s))
        off += n
    return out


def _s5_params(lam_re, lam_im, log_step, b_re, b_im):
    G, P, H = b_re.shape
    step = jnp.exp(log_step)[:, None]
    mag = jnp.exp(lam_re * step)
    ar, ai = mag * jnp.cos(lam_im * step), mag * jnp.sin(lam_im * step)
    den = lam_re * lam_re + lam_im * lam_im
    qr = ((ar - 1.0) * lam_re + ai * lam_im) / den
    qi = (ai * lam_re - (ar - 1.0) * lam_im) / den
    bbr = qr[..., None] * b_re - qi[..., None] * b_im
    bbi = qr[..., None] * b_im + qi[..., None] * b_re
    return ar.reshape(-1), ai.reshape(-1), _s5_blocks_in(bbr), _s5_blocks_in(bbi)


def _s5_blocks_in(b):
    G, P, H = b.shape
    eye = jnp.eye(8, dtype=b.dtype)
    return jnp.einsum("kgph,gj->kghjp", b.reshape(G // 8, 8, P, H), eye).reshape(G // 8, 8 * H, 8 * P)


def _s5_blocks_in_diag(m, G, P, H):
    return jnp.einsum("kghgp->kgph", m.reshape(G // 8, 8, H, 8, P)).reshape(G, P, H)


def _s5_blocks_out(c):
    G, H, P = c.shape
    eye = jnp.eye(8, dtype=c.dtype)
    return jnp.einsum("kghp,gj->kgpjh", c.reshape(G // 8, 8, H, P), eye).reshape(G // 8, 8 * P, 8 * H)


def _s5_blocks_out_diag(m, G, H, P):
    return jnp.einsum("kgpgh->kghp", m.reshape(G // 8, 8, P, 8, H)).reshape(G, H, P)


def _head_blocks(w):
    nh, d, e = w.shape
    return jnp.einsum("hde,hk->hdke", w, jnp.eye(nh, dtype=w.dtype)).reshape(nh * d, nh * e)


def _head_blocks_diag(m, nh, d, e):
    return jnp.einsum("hdhe->hde", m.reshape(nh, d, nh, e))


def _pad_rows(w, rows):
    return jnp.pad(w, ((0, rows - w.shape[0]), (0, 0)))


def _unshard_cols(g):
    return jnp.transpose(g, (1, 0, 2)).reshape(g.shape[1], -1)


_NAMES = ['norm_mix_g', 'w_in', 's5_lam_re', 's5_lam_im', 's5_log_step', 's5_b_re', 's5_b_im', 's5_c_re', 's5_c_im', 's5_d',
          's5_w_glu', 's5_b_glu', 'cv_w_dw', 'cv_b_dw', 'cv_ln_g', 'cv_ln_b', 'cv_w_pw', 'cv_b_pw', 'lru_w_conv', 'lru_b_conv',
          'lru_w_r', 'lru_b_r', 'lru_w_i', 'lru_b_i', 'lru_lam', 'pool_w', 'pool_scale', 'w_out', 'norm_ffn_g', 'ffn_w_up',
          'ffn_w_dw', 'ffn_b_dw', 'ffn_w_down', 'norm_final_g']
_MATRIX = ('w_in', 'w_out', 'ffn_w_up', 'ffn_w_down', 's5_w_glu', 'cv_w_pw')
_TRANSPOSED = ('w_in', 'ffn_w_up')
_COLSHARD = ('cv_w_dw', 'lru_w_conv', 'ffn_w_dw')
_GROUPS = (('w_in', 's5_w_glu', 'cv_w_pw', 'w_out'), ('ffn_w_up', 'ffn_w_down'))
_SMALL_BF16 = ('s5_b_re', 's5_b_im', 's5_c_re', 's5_c_im', 'lru_w_r', 'lru_w_i', 'pool_w')
_SCATTER_GROUPS = (('w_out', 's5_w_glu', 'cv_w_pw'), ('w_in',), ('ffn_w_up', 'ffn_w_down'))


def kernel(x, norm_mix_g, w_in, s5_lam_re, s5_lam_im, s5_log_step, s5_b_re, s5_b_im, s5_c_re, s5_c_im, s5_d, s5_w_glu, s5_b_glu, cv_w_dw, cv_b_dw, cv_ln_g, cv_ln_b, cv_w_pw, cv_b_pw, lru_w_conv, lru_b_conv, lru_w_r, lru_b_r, lru_w_i, lru_b_i, lru_lam, pool_w, pool_scale, w_out, norm_ffn_g, ffn_w_up, ffn_w_dw, ffn_b_dw, ffn_w_down, norm_final_g, loss_target, m_norm_mix_g, m_w_in, m_s5_lam_re, m_s5_lam_im, m_s5_log_step, m_s5_b_re, m_s5_b_im, m_s5_c_re, m_s5_c_im, m_s5_d, m_s5_w_glu, m_s5_b_glu, m_cv_w_dw, m_cv_b_dw, m_cv_ln_g, m_cv_ln_b, m_cv_w_pw, m_cv_b_pw, m_lru_w_conv, m_lru_b_conv, m_lru_w_r, m_lru_b_r, m_lru_w_i, m_lru_b_i, m_lru_lam, m_pool_w, m_pool_scale, m_w_out, m_norm_ffn_g, m_ffn_w_up, m_ffn_w_dw, m_ffn_b_dw, m_ffn_w_down, m_norm_final_g, v_norm_mix_g, v_w_in, v_s5_lam_re, v_s5_lam_im, v_s5_log_step, v_s5_b_re, v_s5_b_im, v_s5_c_re, v_s5_c_im, v_s5_d, v_s5_w_glu, v_s5_b_glu, v_cv_w_dw, v_cv_b_dw, v_cv_ln_g, v_cv_ln_b, v_cv_w_pw, v_cv_b_pw, v_lru_w_conv, v_lru_b_conv, v_lru_w_r, v_lru_b_r, v_lru_w_i, v_lru_b_i, v_lru_lam, v_pool_w, v_pool_scale, v_w_out, v_norm_ffn_g, v_ffn_w_up, v_ffn_w_dw, v_ffn_b_dw, v_ffn_w_down, v_norm_final_g):
    args = (x, norm_mix_g, w_in, s5_lam_re, s5_lam_im, s5_log_step, s5_b_re, s5_b_im, s5_c_re, s5_c_im, s5_d, s5_w_glu, s5_b_glu, cv_w_dw, cv_b_dw, cv_ln_g, cv_ln_b, cv_w_pw, cv_b_pw, lru_w_conv, lru_b_conv, lru_w_r, lru_b_r, lru_w_i, lru_b_i, lru_lam, pool_w, pool_scale, w_out, norm_ffn_g, ffn_w_up, ffn_w_dw, ffn_b_dw, ffn_w_down, norm_final_g, loss_target, m_norm_mix_g, m_w_in, m_s5_lam_re, m_s5_lam_im, m_s5_log_step, m_s5_b_re, m_s5_b_im, m_s5_c_re, m_s5_c_im, m_s5_d, m_s5_w_glu, m_s5_b_glu, m_cv_w_dw, m_cv_b_dw, m_cv_ln_g, m_cv_ln_b, m_cv_w_pw, m_cv_b_pw, m_lru_w_conv, m_lru_b_conv, m_lru_w_r, m_lru_b_r, m_lru_w_i, m_lru_b_i, m_lru_lam, m_pool_w, m_pool_scale, m_w_out, m_norm_ffn_g, m_ffn_w_up, m_ffn_w_dw, m_ffn_b_dw, m_ffn_w_down, m_norm_final_g, v_norm_mix_g, v_w_in, v_s5_lam_re, v_s5_lam_im, v_s5_log_step, v_s5_b_re, v_s5_b_im, v_s5_c_re, v_s5_c_im, v_s5_d, v_s5_w_glu, v_s5_b_glu, v_cv_w_dw, v_cv_b_dw, v_cv_ln_g, v_cv_ln_b, v_cv_w_pw, v_cv_b_pw, v_lru_w_conv, v_lru_b_conv, v_lru_w_r, v_lru_b_r, v_lru_w_i, v_lru_b_i, v_lru_lam, v_pool_w, v_pool_scale, v_w_out, v_norm_ffn_g, v_ffn_w_up, v_ffn_w_dw, v_ffn_b_dw, v_ffn_w_down, v_norm_final_g)
    x, target = args[0], args[35]
    W = dict(zip(_NAMES, args[1:35]))
    Mo = dict(zip(_NAMES, args[36:70]))
    Vo = dict(zip(_NAMES, args[70:104]))
    x = x[0]
    target = target[0]
    T, D = x.shape
    depth = W['w_in'].shape[0]
    F = W['ffn_w_down'].shape[1] * N_DEV
    G, P, H = W['s5_b_re'].shape[1:]
    MW = G * H
    nh, hd, _ = W['lru_w_r'].shape[1:]
    me = _rank(_my_pos())

    gather, full = {}, {}

    def start_group(l, gi, after):
        shards = [(W[n][l].T if n in _TRANSPOSED else W[n][l]).astype(BF16) for n in _GROUPS[gi]]
        if (l, gi) == (0, 0):
            shards.append(_pack([W[n][k] for k in range(depth) for n in _COLSHARD]))
        gather[l, gi], tok = _gather_start(shards, "gather_%d%d" % (l, gi), after)
        return tok

    def land_weights(l, gi, after):
        got = _gather_wait(gather[l, gi], after)
        for n, g8 in zip(_GROUPS[gi], got):
            full[n, l] = g8.reshape(-1, g8.shape[-1])
        return got

    started = start_group(0, 0, None)[0, 0]
    got = land_weights(0, 0, W['norm_mix_g'][0] + started)
    started = start_group(0, 1, got[0])[0, 0]
    taps8 = got[-1]
    tap_shapes = [W[n][l].shape for l in range(depth) for n in _COLSHARD]
    taps = [_unshard_cols(jnp.stack(t)) for t in zip(*[_unpack(taps8[r], tap_shapes) for r in range(N_DEV)])]
    for l in range(depth):
        for j, n in enumerate(_COLSHARD):
            full[n, l] = taps[l * len(_COLSHARD) + j]

    saved = []
    for l in range(depth):
        g_mix = W['norm_mix_g'][l] + (0.0 if l else started)
        if l:
            land_weights(l, 0, x)
        s5p, s5p_vjp = jax.vjp(_s5_params, W['s5_lam_re'][l], W['s5_lam_im'][l], W['s5_log_step'][l], W['s5_b_re'][l], W['s5_b_im'][l])
        ar, ai, bre, bim = s5p
        cre, cim = _s5_blocks_out(W['s5_c_re'][l]), _s5_blocks_out(W['s5_c_im'][l])
        sp8, sp8_vjp = jax.vjp(lambda lam: LRU_C * jax.nn.softplus(-lam), W['lru_lam'][l])
        wr, wi = _head_blocks(W['lru_w_r'][l]), _head_blocks(W['lru_w_i'][l])
        c = dict(bre=bre.astype(BF16), bim=bim.astype(BF16), cre=cre.astype(BF16), cim=cim.astype(BF16),
                 cst_f=_s5_scan_consts(ar, ai, False), cst_b=_s5_scan_consts(ar, ai, True), sp8=sp8, wr=wr.astype(BF16),
                 wi=wi.astype(BF16), cvw=_pad_rows(full['cv_w_dw', l], 32), lruw=_pad_rows(full['lru_w_conv', l], 8),
                 ffw=_pad_rows(full['ffn_w_dw', l], 8), pw=W['pool_w'][l].astype(BF16), s5p_vjp=s5p_vjp, sp8_vjp=sp8_vjp, x0=x)
        h = _rmsnorm(x, g_mix, name="mix_norm")
        proj = _matmul(h, full['w_in', l], mode="nt", tm=1024, tn=512, tk=D, out_dtype=F32, name="in_proj")
        y0, sre, sim, yp, z = _s5_fwd(proj, 0, c['bre'], c['bim'], c['cre'], c['cim'], c['cst_f'], W['s5_d'][l],
                                     full['s5_w_glu', l], W['s5_b_glu'][l])
        y1, h1 = _cv_fwd(proj, 1, 2, c['cvw'], W['cv_b_dw'][l], W['cv_ln_g'][l], W['cv_ln_b'][l], full['cv_w_pw', l], W['cv_b_pw'][l])
        y2, xc, rr, ig, hprev = _lru_fwd(proj, 3, 4, c['lruw'], W['lru_b_conv'][l], c['wr'], W['lru_b_r'][l], c['wi'],
                                         W['lru_b_i'][l], sp8)
        y3 = _pool_fwd(proj, 5, c['pw'], W['pool_scale'][l])
        mixed = jnp.concatenate([y0, y1, y2, y3], axis=1)
        x1 = _matmul(mixed, full['w_out', l], mode="nn", tm=1024, tn=512, tk=2048, out_dtype=F32, name="out_proj", add=x)
        got = land_weights(l, 1, x1)
        g_ffn = W['norm_ffn_g'][l]
        if l + 1 < depth:
            g_ffn = g_ffn + start_group(l + 1, 0, got[0])[0, 0] + start_group(l + 1, 1, got[0])[0, 0]
        h2 = _rmsnorm(x1, g_ffn, name="ffn_norm")
        up = _matmul(h2, full['ffn_w_up', l], mode="nt", tm=2048, tn=256, tk=D, out_dtype=BF16, name="ffn_up")
        act = _ffn_mid_fwd(up, c['ffw'], W['ffn_b_dw'][l])
        x = _matmul(act, full['ffn_w_down', l], mode="nn", tm=1024, tn=512, tk=F, out_dtype=F32, name="ffn_down", add=x1)
        c.update(proj=proj, h=h, sre=sre, sim=sim, yp=yp, z=z, h1=h1, xc=xc, rr=rr, ig=ig, hprev=hprev, mixed=mixed, x1=x1,
                 up=up, h2=h2, act=act)
        saved.append(c)

    dx, dxb, dgf, se = _loss_head(x, W['norm_final_g'], target)
    loss = lax.psum(0.5 / D * jnp.sum(se), AXES)
    gsmall = {('norm_final_g', None): dgf.sum(0)}
    gmat, scatter = {}, {}

    def parts_of(m):
        return m.reshape(N_DEV, m.shape[0] // N_DEV, m.shape[1])

    for l in reversed(range(depth)):
        c = saved[l]
        dact = _matmul(dxb, full['ffn_w_down', l], mode="nt", tm=512, tn=F, tk=512, out_dtype=BF16, name="d_act")
        gmat['ffn_w_down', l] = _matmul(c['act'], dxb, mode="tn", tm=F, tn=512, tk=512, out_dtype=BF16, name="dw_down")
        dup, dffw, dffb = _ffn_mid_bwd(dact, c['up'], c['ffw'], W['ffn_b_dw'][l])
        gsmall['ffn_w_dw', l] = dffw.sum(1)[:3]
        gsmall['ffn_b_dw', l] = dffb.sum(0)
        dh2 = _matmul(dup, full['ffn_w_up', l], mode="nn", tm=1024, tn=512, tk=F, out_dtype=F32, name="d_h2")
        gmat['ffn_w_up', l] = _matmul(dup, c['h2'], mode="tn", tm=256, tn=D, tk=T, out_dtype=BF16, name="dw_up")
        scatter[l, 2], tok = _scatter_start([parts_of(gmat[n, l]) for n in _SCATTER_GROUPS[2]], "scatter_%d2" % l)
        dx1, dx1b, dg2 = _rms_bwd(dx, dh2, c['x1'], W['norm_ffn_g'][l] + tok[0, 0], name="ffn_norm_bwd")
        gsmall['norm_ffn_g', l] = dg2.sum(0)
        dmix = _matmul(dx1b, full['w_out', l], mode="nt", tm=1024, tn=512, tk=2048, out_dtype=F32, name="d_mixed")
        gmat['w_out', l] = _matmul(c['mixed'], dx1b, mode="tn", tm=512, tn=D, tk=T, out_dtype=BF16, name="dw_out")
        du, dbre, dbim, dcre, dcim, da, dwglu, v3 = _s5_bwd(dmix, 0, c['proj'], 0, c['sre'], c['sim'], c['yp'], c['z'], c['bre'],
                                                            c['bim'], c['cre'], c['cim'], c['cst_b'], W['s5_d'][l], full['s5_w_glu', l])
        da = da.sum(1)
        glr, gli, gls, gbr, gbi = c['s5p_vjp']((da[0], da[1], dbre, dbim))
        gsmall['s5_lam_re', l], gsmall['s5_lam_im', l], gsmall['s5_log_step', l] = glr, gli, gls
        gsmall['s5_b_re', l], gsmall['s5_b_im', l] = gbr, gbi
        gsmall['s5_c_re', l] = _s5_blocks_out_diag(dcre, G, H, P)
        gsmall['s5_c_im', l] = _s5_blocks_out_diag(dcim, G, H, P)
        gsmall['s5_b_glu', l], gsmall['s5_d', l] = v3[0].sum(0), v3[1].sum(0)
        gmat['s5_w_glu', l] = dwglu.astype(BF16)
        dv, dg, dwpw, dcvw, v5 = _cv_bwd(dmix, 1, c['proj'], 1, 2, c['h1'], c['cvw'], W['cv_ln_g'][l], W['cv_ln_b'][l], full['cv_w_pw', l])
        gmat['cv_w_pw', l] = dwpw.astype(BF16)
        gsmall['cv_w_dw', l] = dcvw.sum(1)[:31]
        gsmall['cv_b_pw', l], gsmall['cv_ln_g', l], gsmall['cv_ln_b', l], gsmall['cv_b_dw', l] = [v5[j].sum(0) for j in range(4)]
        scatter[l, 0], tok = _scatter_start([parts_of(gmat[n, l]) for n in _SCATTER_GROUPS[0]], "scatter_%d0" % l)
        tok = tok[0, 0]
        dlx, dlg, dwr, dwi, dlw, v4 = _lru_bwd(dmix, 2, c['proj'], 3, 4, c['xc'], c['rr'], c['ig'], c['hprev'], c['lruw'], c['wr'],
                                               c['wi'], c['sp8'] + tok)
        gsmall['lru_w_r', l], gsmall['lru_w_i', l] = _head_blocks_diag(dwr, nh, hd, hd), _head_blocks_diag(dwi, nh, hd, hd)
        gsmall['lru_w_conv', l] = dlw.sum(1)[:4]
        gsmall['lru_b_r', l], gsmall['lru_b_i', l] = v4[0].sum(0), v4[1].sum(0)
        gsmall['lru_lam', l] = c['sp8_vjp'](v4[2].sum(0))[0]
        gsmall['lru_b_conv', l] = v4[3].sum(0)
        dpx, dpw, dps = _pool_bwd(dmix, 3, c['proj'], 5, c['pw'], W['pool_scale'][l] + tok)
        gsmall['pool_w', l], gsmall['pool_scale', l] = dpw, dps.sum(0)
        dproj = jnp.concatenate([du, dv, dg, dlx, dlg, dpx], axis=1)
        dh = _matmul(dproj, full['w_in', l], mode="nn", tm=1024, tn=512, tk=dproj.shape[1], out_dtype=F32, name="d_h")
        gmat['w_in', l] = _matmul(dproj, c['h'], mode="tn", tm=512, tn=D, tk=T, out_dtype=BF16, name="dw_in")
        scatter[l, 1], tok = _scatter_start([parts_of(gmat[n, l]) for n in _SCATTER_GROUPS[1]], "scatter_%d1" % l)
        dx, dxb, dg1 = _rms_bwd(dx1, dh, c['x0'], W['norm_mix_g'][l] + tok[0, 0], name="mix_norm_bwd")
        gsmall['norm_mix_g', l] = dg1.sum(0)

    small_names = [n for n in _NAMES if n not in _MATRIX]
    small_split = ([n for n in small_names if n not in _SMALL_BF16], [n for n in small_names if n in _SMALL_BF16])
    small_packs, full_shapes = [], []
    for names, dtype in zip(small_split, (F32, BF16)):
        layers = [[None] if n == 'norm_final_g' else range(depth) for n in names]
        small_packs.append(_pack([gsmall[n, l] for n, ls in zip(names, layers) for l in ls], dtype))
        full_shapes.append([gsmall[n, None].shape if n == 'norm_final_g' else (depth,) + gsmall[n, 0].shape for n in names])
    small_state, tok = _gather_start(small_packs, "small_grads")

    stored_t = [n for n in _TRANSPOSED if W[n].shape[-1] % LANES]
    wmv = {n: [jnp.swapaxes(t[n], 1, 2) if n in stored_t else t[n] for t in (W, Mo, Vo)] for n in _MATRIX}
    stacked = {n: None for n in _MATRIX}

    def update_matrices(l, after):
        for gi in (2, 0, 1):
            for n, p8 in zip(_SCATTER_GROUPS[gi], _scatter_wait(scatter[l, gi], after)):
                g = _sum8(p8, name="sum_%s_%d" % (n, l)).T if n in _TRANSPOSED and n not in stored_t else p8
                stacked[n] = _adamw_layer(g, *wmv[n], l, stacked[n], name="adamw_%s_%d" % (n, l))

    after = tok
    for l in reversed(range(depth)):
        update_matrices(l, after)
        after = sum(stacked[n][1][l, 0, 0] for n in _MATRIX).reshape(1, 1)
    after_matrices = after
    gs = {}
    for names, shapes, small8 in zip(small_split, full_shapes, _gather_wait(small_state, after_matrices)):
        gs.update(zip(names, _unpack(_sum8(small8, name="sum_small_" + str(small8.dtype)), shapes)))
    for n in _COLSHARD:
        cw = W[n].shape[-1]
        gs[n] = lax.dynamic_slice_in_dim(gs[n], me * cw, cw, axis=gs[n].ndim - 1)
    upd = _adamw_many([W[n] for n in small_names], [gs[n] for n in small_names], [Mo[n] for n in small_names],
                      [Vo[n] for n in small_names], name="adamw_small")
    out = {n: (gs[n], upd[0][j], upd[1][j], upd[2][j]) for j, n in enumerate(small_names)}
    for n in _MATRIX:
        out[n] = tuple(jnp.swapaxes(t, 1, 2) for t in stacked[n]) if n in stored_t else tuple(stacked[n])
    return (loss, dx[None]) + tuple(out[n][q] for q in range(4) for n in _NAMES)
```

```python
import functools
import math

import jax
import jax.numpy as jnp
from jax import lax
from jax.experimental import pallas as pl
from jax.experimental.pallas import tpu as pltpu

F32 = jnp.float32
BF16 = jnp.bfloat16
N_DEV = 8
AXES = ("x", "y", "c")
EPS = 1e-6
S5_GROUP_CH = 16
S5_STATE = 64
LRU_HEADS = 8
LRU_C = 8.0
POOL_WINDOWS = (2, 4, 8, 16)
CONV_HALO = 32
SMALL_HALO = 8
FFN_HALO = 16
POOL_HALO = 16
LANES = 128
ADAM_LR, ADAM_B1, ADAM_B2, ADAM_EPS, ADAM_WD, ADAM_STEP = 0.001, 0.9, 0.999, 1e-08, 0.01, 10
_GELU_K = math.sqrt(2.0 / math.pi)
ARB = pltpu.CompilerParams(dimension_semantics=("arbitrary",))


def _pick(n, pref, mult):
    best = None
    for t in range(mult, min(n, pref) + 1, mult):
        if n % t == 0:
            best = t
    return best if best is not None else n


def _sigmoid(x):
    return 1.0 / (1.0 + jnp.exp(-x))


def _gelu(x):
    return 0.5 * x * (1.0 + jnp.tanh(_GELU_K * (x + 0.044715 * x * x * x)))


def _gelu_grad(x):
    t = jnp.tanh(_GELU_K * (x + 0.044715 * x * x * x))
    return 0.5 * (1.0 + t) + 0.5 * x * (1.0 - t * t) * _GELU_K * (1.0 + 3.0 * 0.044715 * x * x)


def _gelu_and_grad(x):
    t = jnp.tanh(_GELU_K * (x + 0.044715 * x * x * x))
    half = 0.5 * (1.0 + t)
    return x * half, half + 0.5 * x * (1.0 - t * t) * _GELU_K * (1.0 + 3.0 * 0.044715 * x * x)


def _neg_expm1(x):
    series = -x * (1.0 + x * (0.5 + x * (1.0 / 6.0 + x * (1.0 / 24.0 + x * (1.0 / 120.0 + x * (1.0 / 720.0))))))
    return jnp.where(jnp.abs(x) < 0.25, series, 1.0 - jnp.exp(x))


def _fold8(x):
    return x.reshape(x.shape[0] // 8, 8, x.shape[1]).sum(axis=0)


def _dot(a, b):
    return jnp.dot(a.astype(BF16), b.astype(BF16), preferred_element_type=F32)


def _dot_nt(a, b):
    return lax.dot_general(a.astype(BF16), b.astype(BF16), (((1,), (1,)), ((), ())), preferred_element_type=F32)


def _dot_tn(a, b):
    return lax.dot_general(a.astype(BF16), b.astype(BF16), (((0,), (0,)), ((), ())), preferred_element_type=F32)


def _full(shape):
    nd = len(shape)
    return pl.BlockSpec(shape, lambda *i: (0,) * nd)


def _matmul(a, b, *, mode, tm, tn, tk, out_dtype, name, add=None):
    if mode == "nn":
        (M, K), N = a.shape, b.shape[1]
    elif mode == "nt":
        (M, K), N = a.shape, b.shape[0]
    else:
        (K, M), N = a.shape, b.shape[1]
    tm, tn, tk = _pick(M, tm, 8), _pick(N, tn, 128), _pick(K, tk, 128)
    if mode == "tn":
        tm = _pick(M, tm, 128)
    nk = K // tk
    a_spec = {"nn": pl.BlockSpec((tm, tk), lambda i, j, k: (i, k)), "nt": pl.BlockSpec((tm, tk), lambda i, j, k: (i, k)),
              "tn": pl.BlockSpec((tk, tm), lambda i, j, k: (k, i))}[mode]
    b_spec = {"nn": pl.BlockSpec((tk, tn), lambda i, j, k: (k, j)), "nt": pl.BlockSpec((tn, tk), lambda i, j, k: (j, k)),
              "tn": pl.BlockSpec((tk, tn), lambda i, j, k: (k, j))}[mode]
    o_spec = pl.BlockSpec((tm, tn), lambda i, j, k: (i, j))
    dot = {"nn": _dot, "nt": _dot_nt, "tn": _dot_tn}[mode]
    has_add = add is not None

    def body(*refs):
        if has_add:
            a_ref, b_ref, add_ref, o_ref, acc_ref = refs
        else:
            a_ref, b_ref, o_ref, acc_ref = refs
        if nk == 1:
            part = dot(a_ref[...], b_ref[...])
            o_ref[...] = (part + add_ref[...] if has_add else part).astype(out_dtype)
            return
        k = pl.program_id(2)

        @pl.when(k == 0)
        def _():
            acc_ref[...] = jnp.zeros_like(acc_ref)

        acc_ref[...] += dot(a_ref[...], b_ref[...])

        @pl.when(k == nk - 1)
        def _():
            r = acc_ref[...]
            if has_add:
                r = r + add_ref[...]
            o_ref[...] = r.astype(out_dtype)

    ins = [a, b] + ([add] if has_add else [])
    specs = [a_spec, b_spec] + ([o_spec] if has_add else [])
    return pl.pallas_call(
        body, name=name, grid=(M // tm, N // tn, nk), in_specs=specs, out_specs=o_spec,
        out_shape=jax.ShapeDtypeStruct((M, N), out_dtype),
        scratch_shapes=[pltpu.VMEM((tm, tn) if nk > 1 else (8, LANES), F32)],
        compiler_params=pltpu.CompilerParams(dimension_semantics=("parallel", "parallel", "arbitrary")),
    )(*ins)


def _rmsnorm(x, g, *, name):
    T, D = x.shape
    tc = _pick(T, 256, 16)

    def body(x_ref, g_ref, h_ref):
        def rows16(n, _):
            rows = pl.ds(pl.multiple_of(n * 16, 16), 16)
            xv = x_ref[rows, :]
            r = lax.rsqrt(jnp.mean(xv * xv, axis=-1, keepdims=True) + EPS)
            h_ref[rows, :] = (xv * r * g_ref[...]).astype(BF16)
            return 0

        lax.fori_loop(0, tc // 16, rows16, 0, unroll=4)

    row = pl.BlockSpec((tc, D), lambda i: (i, 0))
    return pl.pallas_call(
        body, name=name, grid=(T // tc,), in_specs=[row, _full((1, D))], out_specs=row,
        out_shape=jax.ShapeDtypeStruct((T, D), BF16), compiler_params=pltpu.CompilerParams(dimension_semantics=("parallel",)),
    )(x, g.reshape(1, D))


def _rms_bwd(dres, dh, x, g, *, name):
    T, D = x.shape
    tc = _pick(T, 256, 16)

    def body(dres_ref, dh_ref, x_ref, g_ref, dx_ref, dxb_ref, dg_ref):
        @pl.when(pl.program_id(0) == 0)
        def _():
            dg_ref[...] = jnp.zeros_like(dg_ref)

        def rows16(n, _):
            rows = pl.ds(pl.multiple_of(n * 16, 16), 16)
            xv, dhv = x_ref[rows, :], dh_ref[rows, :]
            r = lax.rsqrt(jnp.mean(xv * xv, axis=-1, keepdims=True) + EPS)
            dg_ref[...] += _fold8(dhv * xv * r)
            dyg = dhv * g_ref[...]
            dx = dres_ref[rows, :] + r * dyg - xv * (r * r * r) * jnp.mean(dyg * xv, axis=-1, keepdims=True)
            dx_ref[rows, :] = dx
            dxb_ref[rows, :] = dx.astype(BF16)
            return 0

        lax.fori_loop(0, tc // 16, rows16, 0, unroll=4)

    row = pl.BlockSpec((tc, D), lambda i: (i, 0))
    return pl.pallas_call(
        body, name=name, grid=(T // tc,), in_specs=[row, row, row, _full((1, D))],
        out_specs=[row, row, _full((8, D))],
        out_shape=[jax.ShapeDtypeStruct((T, D), F32), jax.ShapeDtypeStruct((T, D), BF16), jax.ShapeDtypeStruct((8, D), F32)],
        compiler_params=ARB,
    )(dres, dh, x, g.reshape(1, D))


def _loss_head(x, g, target):
    T, D = x.shape
    tc = _pick(T, 256, 8)

    def body(x_ref, g_ref, t_ref, dx_ref, dxb_ref, dg_ref, se_ref):
        @pl.when(pl.program_id(0) == 0)
        def _():
            dg_ref[...] = jnp.zeros_like(dg_ref)
            se_ref[...] = jnp.zeros_like(se_ref)

        def rows16(n, _):
            rows = pl.ds(pl.multiple_of(n * 16, 16), 16)
            xv = x_ref[rows, :]
            r = lax.rsqrt(jnp.mean(xv * xv, axis=-1, keepdims=True) + EPS)
            gv = g_ref[...]
            err = xv * r * gv - t_ref[rows, :]
            se_ref[...] += _fold8(err * err)
            dy = err * (1.0 / D)
            dg_ref[...] += _fold8(dy * xv * r)
            dyg = dy * gv
            dx = r * dyg - xv * (r * r * r) * jnp.mean(dyg * xv, axis=-1, keepdims=True)
            dx_ref[rows, :] = dx
            dxb_ref[rows, :] = dx.astype(BF16)
            return 0

        lax.fori_loop(0, tc // 16, rows16, 0, unroll=4)

    row = pl.BlockSpec((tc, D), lambda i: (i, 0))
    return pl.pallas_call(
        body, name="loss_head", grid=(T // tc,), in_specs=[row, _full((1, D)), row],
        out_specs=[row, row, _full((8, D)), _full((8, D))],
        out_shape=[jax.ShapeDtypeStruct((T, D), F32), jax.ShapeDtypeStruct((T, D), BF16),
                   jax.ShapeDtypeStruct((8, D), F32), jax.ShapeDtypeStruct((8, D), F32)],
        compiler_params=ARB,
    )(x, g.reshape(1, D), target)


def _s5_scan_consts(ar, ai, reverse):
    if reverse:
        ai = -ai
    pr, pi_ = [ar], [ai]
    for _ in range(7):
        pr, pi_ = pr + [pr[-1] * ar - pi_[-1] * ai], pi_ + [pr[-1] * ai + pi_[-1] * ar]
    rows = jnp.arange(8)[:, None]
    planes = []
    for d in (1, 2, 4):
        mask = (rows <= 7 - d) if reverse else (rows >= d)
        planes += [jnp.where(mask, pr[d - 1][None, :], 0.0), jnp.where(mask, pi_[d - 1][None, :], 0.0)]
    order = list(range(7, -1, -1)) if reverse else list(range(8))
    planes += [jnp.stack([pr[k] for k in order]), jnp.stack([pi_[k] for k in order])]
    return jnp.stack(planes).astype(F32)


def _s5_block_scan(xr_ref, xi_ref, cst_ref, car_ref, nblk, reverse, lb, extra=None):
    ns = xr_ref.shape[1]
    crow = 0 if reverse else 7

    def blk(i, _):
        ib = (nblk - 1 - i) if reverse else i
        r0 = pl.multiple_of(ib * 8, 8)
        for k in range(ns // lb):
            sl = slice(k * lb, (k + 1) * lb)
            xr, xi = xr_ref[pl.ds(r0, 8), sl], xi_ref[pl.ds(r0, 8), sl]
            for n, d in enumerate((1, 2, 4)):
                ar, ai = cst_ref[2 * n, :, sl], cst_ref[2 * n + 1, :, sl]
                sh = (8 - d) if reverse else d
                rr, ri = pltpu.roll(xr, sh, 0), pltpu.roll(xi, sh, 0)
                xr, xi = xr + ar * rr - ai * ri, xi + ar * ri + ai * rr
            pr, pi_ = cst_ref[6, :, sl], cst_ref[7, :, sl]
            cr, ci = car_ref[0, crow:crow + 1, sl], car_ref[1, crow:crow + 1, sl]
            xr, xi = xr + pr * cr - pi_ * ci, xi + pr * ci + pi_ * cr
            xr_ref[pl.ds(r0, 8), sl] = xr
            xi_ref[pl.ds(r0, 8), sl] = xi
            if extra is not None:
                extra(r0, sl, xr, xi, cr, ci)
            car_ref[0, :, sl] = xr
            car_ref[1, :, sl] = xi
        return 0

    lax.fori_loop(0, nblk, blk, 0)


def _s5_fwd(proj, col, bre, bim, cre, cim, cst, dvec, wglu, bglu):
    T = proj.shape[0]
    nsb, cw, sw = bre.shape
    M, NS = nsb * cw, nsb * sw
    tc = _pick(T, 512, 8)

    def body(u_ref, bre_ref, bim_ref, cre_ref, cim_ref, cst_ref, d_ref, wg_ref, bg_ref,
             y_ref, sre_ref, sim_ref, yp_ref, z_ref, car_ref):
        @pl.when(pl.program_id(0) == 0)
        def _():
            car_ref[...] = jnp.zeros_like(car_ref)

        u = u_ref[...]
        ub = u.astype(BF16)
        for k in range(nsb):
            sre_ref[:, k * sw:(k + 1) * sw] = _dot(ub[:, k * cw:(k + 1) * cw], bre_ref[k])
            sim_ref[:, k * sw:(k + 1) * sw] = _dot(ub[:, k * cw:(k + 1) * cw], bim_ref[k])
        _s5_block_scan(sre_ref, sim_ref, cst_ref, car_ref, tc // 8, False, sw)
        for k in range(nsb):
            yk = _dot(sre_ref[:, k * sw:(k + 1) * sw], cre_ref[k]) - _dot(sim_ref[:, k * sw:(k + 1) * sw], cim_ref[k])
            yp_ref[:, k * cw:(k + 1) * cw] = yk + d_ref[:, k * cw:(k + 1) * cw] * u[:, k * cw:(k + 1) * cw]
        gl = _gelu(yp_ref[...])
        z = _dot(gl, wg_ref[...]) + bg_ref[...]
        z_ref[...] = z
        y_ref[...] = (gl * _sigmoid(z)).astype(BF16)

    rowm = pl.BlockSpec((tc, M), lambda i: (i, 0))
    rows = pl.BlockSpec((tc, NS), lambda i: (i, 0))
    return pl.pallas_call(
        body, name="s5_fwd", grid=(T // tc,),
        in_specs=[pl.BlockSpec((tc, M), lambda i: (i, col)), _full(bre.shape), _full(bim.shape), _full(cre.shape),
                  _full(cim.shape), _full(cst.shape), _full((1, M)), _full((M, M)), _full((1, M))],
        out_specs=[rowm, rows, rows, rowm, rowm],
        out_shape=[jax.ShapeDtypeStruct((T, M), BF16), jax.ShapeDtypeStruct((T, NS), F32), jax.ShapeDtypeStruct((T, NS), F32),
                   jax.ShapeDtypeStruct((T, M), F32), jax.ShapeDtypeStruct((T, M), F32)],
        scratch_shapes=[pltpu.VMEM((2, 8, NS), F32)], compiler_params=ARB,
    )(proj, bre, bim, cre, cim, cst, dvec.reshape(1, M), wglu, bglu.reshape(1, M))


def _s5_bwd(dmix, dcol, proj, col, sre, sim, yp, z, bre, bim, cre, cim, cst, dvec, wglu):
    T = proj.shape[0]
    nsb, cw, sw = bre.shape
    M, NS = nsb * cw, nsb * sw
    tc = _pick(T, 512, 8)
    nc = T // tc

    def body(do_ref, u_ref, sre_ref, sim_ref, yp_ref, z_ref, bre_ref, bim_ref, cre_ref, cim_ref, cst_ref, d_ref, wg_ref,
             du_ref, dbre_ref, dbim_ref, dcre_ref, dcim_ref, da_ref, dwg_ref, vec_ref, gre_ref, gim_ref, car_ref):
        @pl.when(pl.program_id(0) == 0)
        def _():
            for r in (dbre_ref, dbim_ref, dcre_ref, dcim_ref, da_ref, dwg_ref, vec_ref, car_ref):
                r[...] = jnp.zeros_like(r)

        u, do, ypv = u_ref[...], do_ref[...], yp_ref[...]
        gl = _gelu(ypv)
        sg = _sigmoid(z_ref[...])
        dz = do * gl * sg * (1.0 - sg)
        dgl = do * sg + _dot_nt(dz, wg_ref[...])
        dwg_ref[...] += _dot_tn(gl, dz)
        vec_ref[0] += _fold8(dz)
        dy = dgl * _gelu_grad(ypv)
        vec_ref[1] += _fold8(dy * u)
        dyb = dy.astype(BF16)
        for k in range(nsb):
            cs, ss = slice(k * cw, (k + 1) * cw), slice(k * sw, (k + 1) * sw)
            gre_ref[:, ss] = _dot_nt(dyb[:, cs], cre_ref[k])
            gim_ref[:, ss] = -_dot_nt(dyb[:, cs], cim_ref[k])
            dcre_ref[k] += _dot_tn(sre_ref[:, ss], dyb[:, cs])
            dcim_ref[k] -= _dot_tn(sim_ref[:, ss], dyb[:, cs])

        row = lax.broadcasted_iota(jnp.int32, (8, sw), 0)

        def extra(r0, sl, gr, gi, cr, ci):
            nr = jnp.where(row == 7, cr, pltpu.roll(gr, 7, 0))
            ni = jnp.where(row == 7, ci, pltpu.roll(gi, 7, 0))
            sr, si = sre_ref[pl.ds(r0, 8), sl], sim_ref[pl.ds(r0, 8), sl]
            da_ref[0, :, sl] += nr * sr + ni * si
            da_ref[1, :, sl] += ni * sr - nr * si

        _s5_block_scan(gre_ref, gim_ref, cst_ref, car_ref, tc // 8, True, sw, extra)
        ub = u.astype(BF16)
        for k in range(nsb):
            cs, ss = slice(k * cw, (k + 1) * cw), slice(k * sw, (k + 1) * sw)
            gr, gi = gre_ref[:, ss].astype(BF16), gim_ref[:, ss].astype(BF16)
            duk = dy[:, cs] * d_ref[:, cs] + _dot_nt(gr, bre_ref[k]) + _dot_nt(gi, bim_ref[k])
            du_ref[:, cs] = duk.astype(BF16)
            dbre_ref[k] += _dot_tn(ub[:, cs], gr)
            dbim_ref[k] += _dot_tn(ub[:, cs], gi)

    def rowm(c):
        return pl.BlockSpec((tc, M), lambda i: (nc - 1 - i, c))
    rows = pl.BlockSpec((tc, NS), lambda i: (nc - 1 - i, 0))
    return pl.pallas_call(
        body, name="s5_bwd", grid=(nc,),
        in_specs=[rowm(dcol), rowm(col), rows, rows, rowm(0), rowm(0), _full(bre.shape), _full(bim.shape), _full(cre.shape),
                  _full(cim.shape), _full(cst.shape), _full((1, M)), _full((M, M))],
        out_specs=[rowm(0), _full(bre.shape), _full(bim.shape), _full(cre.shape), _full(cim.shape), _full((2, 8, NS)),
                   _full((M, M)), _full((3, 8, M))],
        out_shape=[jax.ShapeDtypeStruct((T, M), BF16), jax.ShapeDtypeStruct(bre.shape, F32), jax.ShapeDtypeStruct(bim.shape, F32),
                   jax.ShapeDtypeStruct(cre.shape, F32), jax.ShapeDtypeStruct(cim.shape, F32), jax.ShapeDtypeStruct((2, 8, NS), F32),
                   jax.ShapeDtypeStruct((M, M), F32), jax.ShapeDtypeStruct((3, 8, M), F32)],
        scratch_shapes=[pltpu.VMEM((tc, NS), F32), pltpu.VMEM((tc, NS), F32), pltpu.VMEM((2, 8, NS), F32)],
        compiler_params=ARB,
    )(dmix, proj, sre, sim, yp, z, bre, bim, cre, cim, cst, dvec.reshape(1, M), wglu)


def _ln_silu(h1, lg, lb):
    mu = jnp.mean(h1, axis=-1, keepdims=True)
    xc = h1 - mu
    rstd = lax.rsqrt(jnp.mean(xc * xc, axis=-1, keepdims=True) + EPS)
    xhat = xc * rstd
    h2 = xhat * lg + lb
    return xhat, rstd, h2, _sigmoid(h2)


def _shifted_copies(buf_ref, sh_ref, cs, n):
    for s in range(1, 8):
        sh_ref[s, 0:n, :] = buf_ref[pl.ds(s, n), cs]


def _tap(buf_ref, sh_ref, cs, off, r0, rows):
    q, s = divmod(off, 8)
    return buf_ref[pl.ds(r0 + 8 * q, rows), cs] if s == 0 else sh_ref[s, pl.ds(r0 + 8 * q, rows), :]


def _cv_fwd(proj, vcol, gcol, wdw, bdw, lg, lb, wpw, bpw):
    T = proj.shape[0]
    M = wpw.shape[0]
    tc = _pick(T, 256, 8)
    rb = _pick(tc, 64, 8)
    H, K = CONV_HALO, 31

    def body(v_ref, g_ref, wdw_ref, bdw_ref, lg_ref, lb_ref, wpw_ref, bpw_ref, y_ref, h1_ref, buf_ref, sh_ref):
        @pl.when(pl.program_id(0) == 0)
        def _():
            buf_ref[0:H, :] = jnp.zeros((H, M), F32)

        buf_ref[H:H + tc, :] = v_ref[...] * _sigmoid(g_ref[...])

        def strip(s, _):
            cs = pl.ds(pl.multiple_of(s * LANES, LANES), LANES)
            _shifted_copies(buf_ref, sh_ref, cs, tc + H - 8)
            for r0 in range(0, tc, rb):
                acc = jnp.zeros((rb, LANES), F32) + bdw_ref[:, cs]
                for k in range(K):
                    acc = acc + wdw_ref[k:k + 1, cs] * _tap(buf_ref, sh_ref, cs, H - (K - 1) + k, r0, rb)
                h1_ref[pl.ds(r0, rb), cs] = acc
            return 0

        lax.fori_loop(0, M // LANES, strip, 0)
        _, _, h2, sg = _ln_silu(h1_ref[...], lg_ref[...], lb_ref[...])
        y_ref[...] = (_dot(h2 * sg, wpw_ref[...]) + bpw_ref[...]).astype(BF16)
        buf_ref[0:H, :] = buf_ref[tc:tc + H, :]

    def rowm(c):
        return pl.BlockSpec((tc, M), lambda i: (i, c))
    vec = _full((1, M))
    return pl.pallas_call(
        body, name="cv_fwd", grid=(T // tc,),
        in_specs=[rowm(vcol), rowm(gcol), _full((32, M)), vec, vec, vec, _full((M, M)), vec],
        out_specs=[rowm(0), rowm(0)],
        out_shape=[jax.ShapeDtypeStruct((T, M), BF16), jax.ShapeDtypeStruct((T, M), F32)],
        scratch_shapes=[pltpu.VMEM((tc + H, M), F32), pltpu.VMEM((8, tc + H, LANES), F32)], compiler_params=ARB,
    )(proj, proj, wdw, bdw.reshape(1, M), lg.reshape(1, M), lb.reshape(1, M), wpw, bpw.reshape(1, M))


def _cv_bwd(dmix, dcol, proj, vcol, gcol, h1, wdw, lg, lb, wpw):
    T = proj.shape[0]
    M = wpw.shape[0]
    tc = _pick(T, 256, 8)
    nc = T // tc
    rb = _pick(tc, 32, 16)
    H, K = CONV_HALO, 31
    hb = tc // H

    def body(do_ref, v_ref, g_ref, vh_ref, gh_ref, h1_ref, wdw_ref, lg_ref, lb_ref, wpw_ref,
             dv_ref, dg_ref, dwpw_ref, dwdw_ref, vec_ref, hbuf_ref, dbuf_ref, hsh_ref, dsh_ref):
        i = pl.program_id(0)

        @pl.when(i == 0)
        def _():
            for r in (dwpw_ref, dwdw_ref, vec_ref):
                r[...] = jnp.zeros_like(r)
            dbuf_ref[tc:tc + H, :] = jnp.zeros((H, M), F32)

        halo = vh_ref[...] * _sigmoid(gh_ref[...])
        hbuf_ref[0:H, :] = jnp.where(i == nc - 1, 0.0, halo)
        hbuf_ref[H:H + tc, :] = v_ref[...] * _sigmoid(g_ref[...])
        do = do_ref[...]
        xhat, rstd, h2, sg = _ln_silu(h1_ref[...], lg_ref[...], lb_ref[...])
        dwpw_ref[...] += _dot_tn(h2 * sg, do)
        vec_ref[0] += _fold8(do)
        dh2 = _dot_nt(do, wpw_ref[...]) * (sg * (1.0 + h2 * (1.0 - sg)))
        vec_ref[1] += _fold8(dh2 * xhat)
        vec_ref[2] += _fold8(dh2)
        dxh = dh2 * lg_ref[...]
        dh1 = rstd * (dxh - jnp.mean(dxh, axis=-1, keepdims=True) - xhat * jnp.mean(dxh * xhat, axis=-1, keepdims=True))
        vec_ref[3] += _fold8(dh1)
        dbuf_ref[0:tc, :] = dh1

        def strip(s, _):
            cs = pl.ds(pl.multiple_of(s * LANES, LANES), LANES)
            _shifted_copies(dbuf_ref, dsh_ref, cs, tc + H - 8)
            _shifted_copies(hbuf_ref, hsh_ref, cs, tc + H - 8)
            for r0 in range(0, tc, rb):
                d1 = dbuf_ref[pl.ds(r0, rb), cs]
                dh0 = jnp.zeros((rb, LANES), F32)
                for k in range(K):
                    dh0 = dh0 + wdw_ref[k:k + 1, cs] * _tap(dbuf_ref, dsh_ref, cs, K - 1 - k, r0, rb)
                    dwdw_ref[k, :, cs] += _fold8(d1 * _tap(hbuf_ref, hsh_ref, cs, H - (K - 1) + k, r0, rb))
                sg0 = _sigmoid(g_ref[pl.ds(r0, rb), cs])
                dv_ref[pl.ds(r0, rb), cs] = (dh0 * sg0).astype(BF16)
                dg_ref[pl.ds(r0, rb), cs] = (dh0 * v_ref[pl.ds(r0, rb), cs] * sg0 * (1.0 - sg0)).astype(BF16)
            return 0

        lax.fori_loop(0, M // LANES, strip, 0)
        dbuf_ref[tc:tc + H, :] = dbuf_ref[0:H, :]

    def rowm(c):
        return pl.BlockSpec((tc, M), lambda i: (nc - 1 - i, c))

    def halo(c):
        return pl.BlockSpec((H, M), lambda i: (jnp.maximum((nc - 1 - i) * hb - 1, 0), c))
    vec = _full((1, M))
    return pl.pallas_call(
        body, name="cv_bwd", grid=(nc,),
        in_specs=[rowm(dcol), rowm(vcol), rowm(gcol), halo(vcol), halo(gcol), rowm(0), _full((32, M)), vec, vec, _full((M, M))],
        out_specs=[rowm(0), rowm(0), _full((M, M)), _full((32, 8, M)), _full((5, 8, M))],
        out_shape=[jax.ShapeDtypeStruct((T, M), BF16), jax.ShapeDtypeStruct((T, M), BF16), jax.ShapeDtypeStruct((M, M), F32),
                   jax.ShapeDtypeStruct((32, 8, M), F32), jax.ShapeDtypeStruct((5, 8, M), F32)],
        scratch_shapes=[pltpu.VMEM((tc + H, M), F32), pltpu.VMEM((tc + H, M), F32), pltpu.VMEM((8, tc + H, LANES), F32),
                        pltpu.VMEM((8, tc + H, LANES), F32)], compiler_params=ARB,
    )(dmix, proj, proj, proj, proj, h1, wdw, lg.reshape(1, M), lb.reshape(1, M), wpw)


def _lru_gates(xc, r, ig, sp8):
    la = -r * sp8
    a = jnp.exp(la)
    mult = jnp.sqrt(_neg_expm1(2.0 * la))
    return a, mult, mult * (ig * xc)


def _lru_fwd(proj, xcol, gcol, wcv, bcv, wr, br, wi, bi, sp8):
    T = proj.shape[0]
    M = wr.shape[0]
    tc = _pick(T, 256, 8)
    H, K = SMALL_HALO, 4

    def body(x_ref, g_ref, wcv_ref, bcv_ref, wr_ref, br_ref, wi_ref, bi_ref, sp_ref,
             y_ref, xc_ref, r_ref, i_ref, hp_ref, buf_ref, a_ref, car_ref):
        @pl.when(pl.program_id(0) == 0)
        def _():
            buf_ref[0:H, :] = jnp.zeros((H, M), F32)
            car_ref[...] = jnp.zeros_like(car_ref)

        buf_ref[H:H + tc, :] = x_ref[...]
        xc = jnp.zeros((tc, M), F32) + bcv_ref[...]
        for k in range(K):
            xc = xc + wcv_ref[k:k + 1, :] * buf_ref[pl.ds(H - (K - 1) + k, tc), :]
        buf_ref[0:H, :] = buf_ref[tc:tc + H, :]
        r = _sigmoid(_dot(xc, wr_ref[...]) + br_ref[...])
        ig = _sigmoid(_dot(xc, wi_ref[...]) + bi_ref[...])
        xc_ref[...] = xc
        r_ref[...] = r
        i_ref[...] = ig
        a, _, bt = _lru_gates(xc, r, ig, sp_ref[...])
        a_ref[...] = a
        hp_ref[...] = bt
        row = lax.broadcasted_iota(jnp.int32, (8, M), 0)

        def blk(ib, _):
            r0 = pl.multiple_of(ib * 8, 8)
            av, bv = a_ref[pl.ds(r0, 8), :], hp_ref[pl.ds(r0, 8), :]
            for d in (1, 2, 4):
                m = row >= d
                bv = jnp.where(m, bv + av * pltpu.roll(bv, d, 0), bv)
                av = jnp.where(m, av * pltpu.roll(av, d, 0), av)
            cr = car_ref[7:8, :]
            h = bv + av * cr
            a_ref[pl.ds(r0, 8), :] = h
            hp_ref[pl.ds(r0, 8), :] = jnp.where(row == 0, cr, pltpu.roll(h, 1, 0))
            car_ref[...] = h
            return 0

        lax.fori_loop(0, tc // 8, blk, 0)
        y_ref[...] = (a_ref[...] * _gelu(g_ref[...])).astype(BF16)

    def rowm(c):
        return pl.BlockSpec((tc, M), lambda i: (i, c))
    vec = _full((1, M))
    return pl.pallas_call(
        body, name="lru_fwd", grid=(T // tc,),
        in_specs=[rowm(xcol), rowm(gcol), _full((8, M)), vec, _full((M, M)), vec, _full((M, M)), vec, vec],
        out_specs=[rowm(0)] * 5,
        out_shape=[jax.ShapeDtypeStruct((T, M), BF16)] + [jax.ShapeDtypeStruct((T, M), F32)] * 4,
        scratch_shapes=[pltpu.VMEM((tc + H, M), F32), pltpu.VMEM((tc, M), F32), pltpu.VMEM((8, M), F32)], compiler_params=ARB,
    )(proj, proj, wcv, bcv.reshape(1, M), wr, br.reshape(1, M), wi, bi.reshape(1, M), sp8.reshape(1, M))


def _lru_bwd(dmix, dcol, proj, xcol, gcol, xc, r, ig, hprev, wcv, wr, wi, sp8):
    T = proj.shape[0]
    M = wr.shape[0]
    tc = _pick(T, 256, 8)
    nc = T // tc
    H, K = SMALL_HALO, 4
    hb = tc // H

    def body(do_ref, x_ref, g_ref, xh_ref, xc_ref, r_ref, i_ref, hp_ref, wcv_ref, wr_ref, wi_ref, sp_ref,
             dx_ref, dg_ref, dwr_ref, dwi_ref, dwcv_ref, vec_ref, xbuf_ref, dbuf_ref, a_ref, gs_ref, car_ref):
        i = pl.program_id(0)

        @pl.when(i == 0)
        def _():
            for rf in (dwr_ref, dwi_ref, dwcv_ref, vec_ref, car_ref):
                rf[...] = jnp.zeros_like(rf)
            dbuf_ref[tc:tc + H, :] = jnp.zeros((H, M), F32)

        xcv, rv, iv, hpv, gv, do = xc_ref[...], r_ref[...], i_ref[...], hp_ref[...], g_ref[...], do_ref[...]
        sp = sp_ref[...]
        a, mult, bt = _lru_gates(xcv, rv, iv, sp)
        h = a * hpv + bt
        ge = _gelu(gv)
        dg_ref[...] = (do * h * _gelu_grad(gv)).astype(BF16)
        a_ref[...] = a
        gs_ref[...] = do * ge
        row = lax.broadcasted_iota(jnp.int32, (8, M), 0)

        def blk(n, _):
            r0 = pl.multiple_of((tc // 8 - 1 - n) * 8, 8)
            ablk = a_ref[pl.ds(r0, 8), :]
            av = jnp.where(row == 7, car_ref[0, 0:1, :], pltpu.roll(ablk, 7, 0))
            bv = gs_ref[pl.ds(r0, 8), :]
            for d in (1, 2, 4):
                m = row <= 7 - d
                bv = jnp.where(m, bv + av * pltpu.roll(bv, 8 - d, 0), bv)
                av = jnp.where(m, av * pltpu.roll(av, 8 - d, 0), av)
            gsv = bv + av * car_ref[1, 0:1, :]
            gs_ref[pl.ds(r0, 8), :] = gsv
            car_ref[0] = ablk
            car_ref[1] = gsv
            return 0

        lax.fori_loop(0, tc // 8, blk, 0)
        gs = gs_ref[...]
        dixc = gs * mult
        dla = gs * hpv * a - gs * (iv * xcv) * (a * a) / mult
        vec_ref[2] += _fold8(-dla * rv)
        dpr = (-dla * sp) * rv * (1.0 - rv)
        dpi = (dixc * xcv) * iv * (1.0 - iv)
        vec_ref[0] += _fold8(dpr)
        vec_ref[1] += _fold8(dpi)
        dwr_ref[...] += _dot_tn(xcv, dpr)
        dwi_ref[...] += _dot_tn(xcv, dpi)
        dxc = dixc * iv + _dot_nt(dpr, wr_ref[...]) + _dot_nt(dpi, wi_ref[...])
        vec_ref[3] += _fold8(dxc)
        xbuf_ref[0:H, :] = jnp.where(i == nc - 1, 0.0, xh_ref[...])
        xbuf_ref[H:H + tc, :] = x_ref[...]
        dbuf_ref[0:tc, :] = dxc
        dx = jnp.zeros((tc, M), F32)
        for k in range(K):
            dx = dx + wcv_ref[k:k + 1, :] * dbuf_ref[pl.ds(K - 1 - k, tc), :]
            dwcv_ref[k] += _fold8(dxc * xbuf_ref[pl.ds(H - (K - 1) + k, tc), :])
        dx_ref[...] = dx.astype(BF16)
        dbuf_ref[tc:tc + H, :] = dbuf_ref[0:H, :]

    def rowm(c):
        return pl.BlockSpec((tc, M), lambda i: (nc - 1 - i, c))
    halo = pl.BlockSpec((H, M), lambda i: (jnp.maximum((nc - 1 - i) * hb - 1, 0), xcol))
    vec = _full((1, M))
    return pl.pallas_call(
        body, name="lru_bwd", grid=(nc,),
        in_specs=[rowm(dcol), rowm(xcol), rowm(gcol), halo, rowm(0), rowm(0), rowm(0), rowm(0), _full((8, M)),
                  _full((M, M)), _full((M, M)), vec],
        out_specs=[rowm(0), rowm(0), _full((M, M)), _full((M, M)), _full((8, 8, M)), _full((4, 8, M))],
        out_shape=[jax.ShapeDtypeStruct((T, M), BF16), jax.ShapeDtypeStruct((T, M), BF16), jax.ShapeDtypeStruct((M, M), F32),
                   jax.ShapeDtypeStruct((M, M), F32), jax.ShapeDtypeStruct((8, 8, M), F32), jax.ShapeDtypeStruct((4, 8, M), F32)],
        scratch_shapes=[pltpu.VMEM((tc + H, M), F32), pltpu.VMEM((tc + H, M), F32), pltpu.VMEM((tc, M), F32),
                        pltpu.VMEM((tc, M), F32), pltpu.VMEM((2, 8, M), F32)],
        compiler_params=ARB,
    )(dmix, proj, proj, proj, xc, r, ig, hprev, wcv, wr, wi, sp8.reshape(1, M))


def _pool_diffs(buf_ref, x, t0, tc, gw):
    H = POOL_HALO
    pos = (lax.broadcasted_iota(jnp.int32, (tc, gw), 0) + t0 + 1).astype(F32)
    out, inv = [], []
    for gi, win in enumerate(POOL_WINDOWS):
        sl = slice(gi * gw, (gi + 1) * gw)
        s = x[:, sl]
        for j in range(1, win):
            s = s + buf_ref[pl.ds(H - j, tc), sl]
        ic = 1.0 / jnp.minimum(pos, float(win))
        out.append(s * ic - x[:, sl])
        inv.append(ic)
    return out, inv


def _pool_fwd(proj, col, pw, scale):
    T = proj.shape[0]
    ng, gw, _ = pw.shape
    M = ng * gw
    tc = _pick(T, 256, 8)
    H = POOL_HALO

    def body(x_ref, pw_ref, sc_ref, y_ref, buf_ref):
        @pl.when(pl.program_id(0) == 0)
        def _():
            buf_ref[0:H, :] = jnp.zeros((H, M), F32)

        x = x_ref[...]
        buf_ref[H:H + tc, :] = x
        diffs, _ = _pool_diffs(buf_ref, x, pl.program_id(0) * tc, tc, gw)
        for gi in range(ng):
            sl = slice(gi * gw, (gi + 1) * gw)
            y_ref[:, sl] = (_dot(diffs[gi], pw_ref[gi]) * sc_ref[:, sl]).astype(BF16)
        buf_ref[0:H, :] = buf_ref[tc:tc + H, :]

    return pl.pallas_call(
        body, name="pool_fwd", grid=(T // tc,),
        in_specs=[pl.BlockSpec((tc, M), lambda i: (i, col)), _full(pw.shape), _full((1, M))],
        out_specs=pl.BlockSpec((tc, M), lambda i: (i, 0)), out_shape=jax.ShapeDtypeStruct((T, M), BF16),
        scratch_shapes=[pltpu.VMEM((tc + H, M), F32)], compiler_params=ARB,
    )(proj, pw, scale.reshape(1, M))


def _pool_bwd(dmix, dcol, proj, col, pw, scale):
    T = proj.shape[0]
    ng, gw, _ = pw.shape
    M = ng * gw
    tc = _pick(T, 256, 8)
    nc = T // tc
    H = POOL_HALO
    hb = tc // H

    def body(do_ref, x_ref, xh_ref, pw_ref, sc_ref, dx_ref, dpw_ref, dsc_ref, buf_ref, ebuf_ref):
        i = pl.program_id(0)

        @pl.when(i == 0)
        def _():
            dpw_ref[...] = jnp.zeros_like(dpw_ref)
            dsc_ref[...] = jnp.zeros_like(dsc_ref)
            ebuf_ref[tc:tc + H, :] = jnp.zeros((H, M), F32)

        x, do = x_ref[...], do_ref[...]
        buf_ref[0:H, :] = jnp.where(i == nc - 1, 0.0, xh_ref[...])
        buf_ref[H:H + tc, :] = x
        diffs, inv = _pool_diffs(buf_ref, x, (nc - 1 - i) * tc, tc, gw)
        ddg = []
        for gi in range(ng):
            sl = slice(gi * gw, (gi + 1) * gw)
            dyy = do[:, sl] * sc_ref[:, sl]
            dsc_ref[:, sl] += _fold8(do[:, sl] * _dot(diffs[gi], pw_ref[gi]))
            dpw_ref[gi] += _dot_tn(diffs[gi], dyy)
            d = _dot_nt(dyy, pw_ref[gi])
            ddg.append(d)
            ebuf_ref[0:tc, sl] = d * inv[gi]
        for gi, win in enumerate(POOL_WINDOWS):
            sl = slice(gi * gw, (gi + 1) * gw)
            s = -ddg[gi]
            for j in range(win):
                s = s + ebuf_ref[pl.ds(j, tc), sl]
            dx_ref[:, sl] = s.astype(BF16)
        ebuf_ref[tc:tc + H, :] = ebuf_ref[0:H, :]

    return pl.pallas_call(
        body, name="pool_bwd", grid=(nc,),
        in_specs=[pl.BlockSpec((tc, M), lambda i: (nc - 1 - i, dcol)), pl.BlockSpec((tc, M), lambda i: (nc - 1 - i, col)),
                  pl.BlockSpec((H, M), lambda i: (jnp.maximum((nc - 1 - i) * hb - 1, 0), col)), _full(pw.shape), _full((1, M))],
        out_specs=[pl.BlockSpec((tc, M), lambda i: (nc - 1 - i, 0)), _full(pw.shape), _full((8, M))],
        out_shape=[jax.ShapeDtypeStruct((T, M), BF16), jax.ShapeDtypeStruct(pw.shape, F32), jax.ShapeDtypeStruct((8, M), F32)],
        scratch_shapes=[pltpu.VMEM((tc + H, M), F32), pltpu.VMEM((tc + H, M), F32)], compiler_params=ARB,
    )(dmix, proj, proj, pw, scale.reshape(1, M))


def _ffn_mid_fwd(up, wdw, bdw):
    T, F2 = up.shape
    F = F2 // 2
    tc = _pick(T, 256, 8)
    rb = _pick(tc, 64, 16)
    H, K = FFN_HALO, 3

    def body(gt_ref, val_ref, w_ref, b_ref, act_ref, buf_ref):
        @pl.when(pl.program_id(0) == 0)
        def _():
            buf_ref[0:H, :] = jnp.zeros((H, F), F32)

        buf_ref[H:H + tc, :] = gt_ref[...].astype(F32)

        def strip(s, _):
            cs = pl.ds(pl.multiple_of(s * LANES, LANES), LANES)
            taps = [w_ref[k:k + 1, cs] for k in range(K)]
            for r0 in range(0, tc, rb):
                gc = b_ref[:, cs] + taps[0] * buf_ref[pl.ds(H - (K - 1) + r0, rb), cs]
                for k in range(1, K):
                    gc = gc + taps[k] * buf_ref[pl.ds(H - (K - 1) + k + r0, rb), cs]
                act_ref[pl.ds(r0, rb), cs] = (_gelu(gc) * val_ref[pl.ds(r0, rb), cs].astype(F32)).astype(BF16)
            return 0

        lax.fori_loop(0, F // LANES, strip, 0)
        buf_ref[0:H, :] = buf_ref[tc:tc + H, :]

    return pl.pallas_call(
        body, name="ffn_mid_fwd", grid=(T // tc,),
        in_specs=[pl.BlockSpec((tc, F), lambda i: (i, 0)), pl.BlockSpec((tc, F), lambda i: (i, 1)), _full((8, F)), _full((1, F))],
        out_specs=pl.BlockSpec((tc, F), lambda i: (i, 0)), out_shape=jax.ShapeDtypeStruct((T, F), BF16),
        scratch_shapes=[pltpu.VMEM((tc + H, F), F32)], compiler_params=ARB,
    )(up, up, wdw, bdw.reshape(1, F))


def _ffn_mid_bwd(dact, up, wdw, bdw):
    T, F2 = up.shape
    F = F2 // 2
    tc = _pick(T, 128, 8)
    rb = _pick(tc, 64, 16)
    nc = T // tc
    H, K = FFN_HALO, 3
    hb = tc // H

    def body(da_ref, gt_ref, val_ref, gh_ref, w_ref, b_ref, dup_ref, dw_ref, db_ref, gbuf_ref, dbuf_ref):
        i = pl.program_id(0)

        @pl.when(i == 0)
        def _():
            dw_ref[...] = jnp.zeros_like(dw_ref)
            db_ref[...] = jnp.zeros_like(db_ref)
            dbuf_ref[tc:tc + H, :] = jnp.zeros((H, F), F32)

        gbuf_ref[0:H, :] = jnp.where(i == nc - 1, 0.0, gh_ref[...].astype(F32))
        gbuf_ref[H:H + tc, :] = gt_ref[...].astype(F32)

        def strip(s, _):
            cs = pl.ds(pl.multiple_of(s * LANES, LANES), LANES)
            cs_val = pl.ds(pl.multiple_of(F + s * LANES, LANES), LANES)
            taps = [w_ref[k:k + 1, cs] for k in range(K)]
            for r0 in range(0, tc, rb):
                shifted = [gbuf_ref[pl.ds(H - (K - 1) + k + r0, rb), cs] for k in range(K)]
                gc = b_ref[:, cs] + taps[0] * shifted[0]
                for k in range(1, K):
                    gc = gc + taps[k] * shifted[k]
                ge, gg = _gelu_and_grad(gc)
                da = da_ref[pl.ds(r0, rb), cs].astype(F32)
                dup_ref[pl.ds(r0, rb), cs_val] = (da * ge).astype(BF16)
                dgc = da * val_ref[pl.ds(r0, rb), cs].astype(F32) * gg
                dbuf_ref[pl.ds(r0, rb), cs] = dgc
                db_ref[:, cs] += _fold8(dgc)
                for k in range(K):
                    dw_ref[k, :, cs] += _fold8(dgc * shifted[k])
            for r0 in range(0, tc, rb):
                dgt = taps[0] * dbuf_ref[pl.ds(K - 1 + r0, rb), cs]
                for k in range(1, K):
                    dgt = dgt + taps[k] * dbuf_ref[pl.ds(K - 1 - k + r0, rb), cs]
                dup_ref[pl.ds(r0, rb), cs] = dgt.astype(BF16)
            return 0

        lax.fori_loop(0, F // LANES, strip, 0)
        dbuf_ref[tc:tc + H, :] = dbuf_ref[0:H, :]

    return pl.pallas_call(
        body, name="ffn_mid_bwd", grid=(nc,),
        in_specs=[pl.BlockSpec((tc, F), lambda i: (nc - 1 - i, 0)), pl.BlockSpec((tc, F), lambda i: (nc - 1 - i, 0)),
                  pl.BlockSpec((tc, F), lambda i: (nc - 1 - i, 1)),
                  pl.BlockSpec((H, F), lambda i: (jnp.maximum((nc - 1 - i) * hb - 1, 0), 0)), _full((8, F)), _full((1, F))],
        out_specs=[pl.BlockSpec((tc, F2), lambda i: (nc - 1 - i, 0)), _full((8, 8, F)), _full((8, F))],
        out_shape=[jax.ShapeDtypeStruct((T, F2), BF16), jax.ShapeDtypeStruct((8, 8, F), F32), jax.ShapeDtypeStruct((8, F), F32)],
        scratch_shapes=[pltpu.VMEM((tc + H, F), F32), pltpu.VMEM((tc + H, F), F32)], compiler_params=ARB,
    )(dact, up, up, up, wdw, bdw.reshape(1, F))


def _my_pos():
    return lax.axis_index("x"), lax.axis_index("y"), lax.axis_index("c")


def _flip(pos, k):
    x, y, c = pos
    return ((1 - x) if k & 4 else x, (1 - y) if k & 2 else y, (1 - c) if k & 1 else c)


def _rank(pos):
    return 4 * pos[0] + 2 * pos[1] + pos[2]


def _exchange(xs, name, gather):
    n = len(xs)
    shapes = [((N_DEV,) + tuple(x.shape)) if gather else tuple(x.shape) for x in xs]

    def body(*refs):
        ins, outs = refs[:n], refs[n:2 * n]
        send_sems, recv_sems, local_sems = refs[2 * n:]
        me = _my_pos()
        mine = _rank(me)
        local, sends = [], []
        for a in range(n):
            loc = pltpu.make_async_copy(ins[a] if gather else ins[a].at[mine], outs[a].at[mine], local_sems.at[a])
            loc.start()
            local.append(loc)
            for k in range(1, N_DEV):
                peer = _flip(me, k)
                cp = pltpu.make_async_remote_copy(
                    src_ref=ins[a] if gather else ins[a].at[_rank(peer)], dst_ref=outs[a].at[mine],
                    send_sem=send_sems.at[a, k - 1], recv_sem=recv_sems.at[a, k - 1], device_id=peer,
                    device_id_type=pl.DeviceIdType.MESH)
                cp.start()
                sends.append(cp)
        for a in range(n):
            for k in range(1, N_DEV):
                peer = _flip(me, k)
                pltpu.make_async_remote_copy(
                    src_ref=ins[a] if gather else ins[a].at[_rank(peer)], dst_ref=outs[a].at[_rank(peer)],
                    send_sem=send_sems.at[a, k - 1], recv_sem=recv_sems.at[a, k - 1], device_id=peer,
                    device_id_type=pl.DeviceIdType.MESH).wait_recv()
        for cp in sends:
            cp.wait_send()
        for loc in local:
            loc.wait()

    any_spec = pl.BlockSpec(memory_space=pl.ANY)
    return pl.pallas_call(
        body, name=name, in_specs=[any_spec] * n, out_specs=[any_spec] * n,
        out_shape=[jax.ShapeDtypeStruct(s, x.dtype) for s, x in zip(shapes, xs)],
        scratch_shapes=[pltpu.SemaphoreType.DMA((n, N_DEV - 1)), pltpu.SemaphoreType.DMA((n, N_DEV - 1)),
                        pltpu.SemaphoreType.DMA((n,))],
    )(*xs)


_HBM = pl.BlockSpec(memory_space=pltpu.HBM)
_SEM = pl.BlockSpec(memory_space=pltpu.SEMAPHORE)
_EFFECT = pltpu.SideEffectType.DATAFLOW_SIDE_EFFECTING


def _plan_direct(gather):
    def plan(bufs, ssem, rsem, me, outgoing):
        n = len(bufs) // 2
        out = []
        for a in range(n):
            for k in range(1, N_DEV):
                peer = _flip(me, k)
                out.append(pltpu.make_async_remote_copy(
                    src_ref=bufs[a] if gather else bufs[a].at[_rank(peer)],
                    dst_ref=bufs[n + a].at[_rank(me) if outgoing else _rank(peer)],
                    send_sem=ssem.at[a * 7 + k - 1], recv_sem=rsem.at[a * 7 + k - 1], device_id=peer,
                    device_id_type=pl.DeviceIdType.MESH))
        return out
    return plan, 7


_SAME_CORE = (2, 4, 6)


def _plan_own_block(bufs, ssem, rsem, me, outgoing):
    n = len(bufs) // 2
    out = []
    for a in range(n):
        for j, k in enumerate((1,) + _SAME_CORE):
            peer = _flip(me, k)
            out.append(pltpu.make_async_remote_copy(
                src_ref=bufs[a], dst_ref=bufs[n + a].at[_rank(me) if outgoing else _rank(peer)],
                send_sem=ssem.at[a * 4 + j], recv_sem=rsem.at[a * 4 + j], device_id=peer, device_id_type=pl.DeviceIdType.MESH))
    return out


def _plan_pass_on(bufs, ssem, rsem, me, outgoing):
    sibling = _flip(me, 1)
    out = []
    for a in range(len(bufs)):
        for j, k in enumerate(_SAME_CORE):
            slot = _rank(_flip(me, k) if outgoing else _flip(sibling, k))
            out.append(pltpu.make_async_remote_copy(
                src_ref=bufs[a].at[slot], dst_ref=bufs[a].at[slot], send_sem=ssem.at[a * 3 + j], recv_sem=rsem.at[a * 3 + j],
                device_id=sibling, device_id_type=pl.DeviceIdType.MESH))
    return out


def _split_start(bufs, plan, n_sems, name, after=None):
    nb = len(bufs)
    extra = [] if after is None else [after]

    def body(*refs):
        ssem, rsem = refs[nb + len(extra)], refs[nb + len(extra) + 1]
        for cp in plan(refs[:nb], ssem, rsem, _my_pos(), True):
            cp.start()
        token = refs[2 * nb + len(extra) + 2]
        token[...] = jnp.zeros_like(token)

    res = pl.pallas_call(
        body, name=name,
        out_shape=(pltpu.SemaphoreType.DMA((n_sems,)), pltpu.SemaphoreType.DMA((n_sems,)),
                   *[pltpu.HBM(t.shape, t.dtype) for t in bufs], jax.ShapeDtypeStruct((8, 128), F32)),
        in_specs=[_HBM] * nb + [pl.BlockSpec(memory_space=pl.ANY)] * len(extra),
        out_specs=(_SEM, _SEM, *([_HBM] * nb), pl.BlockSpec(memory_space=pltpu.VMEM)),
        input_output_aliases={i: 2 + i for i in range(nb)},
        compiler_params=pltpu.CompilerParams(has_side_effects=_EFFECT),
    )(*[pltpu.with_memory_space_constraint(t, pltpu.HBM) for t in bufs], *extra)
    return (res[0], res[1], list(res[2:2 + nb])), res[2 + nb]


def _split_wait(handle, plan, after, name):
    ssem, rsem, bufs = handle
    nb = len(bufs)

    def body(*refs):
        for cp in plan(refs[:nb], refs[nb], refs[nb + 1], _my_pos(), False):
            cp.wait_send()
            cp.wait_recv()

    res = pl.pallas_call(
        body, name=name, out_shape=tuple(pltpu.HBM(t.shape, t.dtype) for t in bufs),
        in_specs=[_HBM] * nb + [_SEM, _SEM, pl.BlockSpec(memory_space=pl.ANY)], out_specs=tuple([_HBM] * nb),
        input_output_aliases={i: i for i in range(nb)},
        compiler_params=pltpu.CompilerParams(has_side_effects=_EFFECT),
    )(*bufs, ssem, rsem, after)
    return list(res)


def _landing_zones(xs, gather):
    mine = _rank(_my_pos())
    lands = []
    for x in xs:
        own = x if gather else lax.dynamic_index_in_dim(x, mine, 0, keepdims=False)
        lands.append(lax.dynamic_update_index_in_dim(lax.empty((N_DEV,) + tuple(own.shape), x.dtype), own, mine, 0))
    return lands


def _gather_start(xs, name, after=None, two_level=True):
    bufs = list(xs) + _landing_zones(xs, True)
    if two_level:
        handle, token = _split_start(bufs, _plan_own_block, 4 * len(xs), name + "_start", after)
    else:
        plan, per = _plan_direct(True)
        handle, token = _split_start(bufs, plan, per * len(xs), name + "_start", after)
    return (handle, two_level, name), token


def _gather_wait(state, after):
    handle, two_level, name = state
    n = len(handle[2]) // 2
    if not two_level:
        return _split_wait(handle, _plan_direct(True)[0], after, name + "_wait")[n:]
    lands = _split_wait(handle, _plan_own_block, after, name + "_wait")[n:]
    handle, _ = _split_start(lands, _plan_pass_on, 3 * n, name + "_pass")
    return _split_wait(handle, _plan_pass_on, after, name + "_passed")


def _scatter_start(xs, name):
    plan, per = _plan_direct(False)
    handle, token = _split_start(list(xs) + _landing_zones(xs, False), plan, per * len(xs), name + "_start")
    return (handle, name), token


def _scatter_wait(state, after):
    handle, name = state
    return _split_wait(handle, _plan_direct(False)[0], after, name + "_wait")[len(handle[2]) // 2:]


def _adamw_math(w, g, m, v):
    m = ADAM_B1 * m + (1.0 - ADAM_B1) * g
    v = ADAM_B2 * v + (1.0 - ADAM_B2) * (g * g)
    m_hat = m / (1.0 - ADAM_B1 ** ADAM_STEP)
    v_hat = v / (1.0 - ADAM_B2 ** ADAM_STEP)
    return -ADAM_LR * (m_hat / (jnp.sqrt(v_hat) + ADAM_EPS) + ADAM_WD * w), m, v


def _sum8(parts, *, name):
    _, R, C = parts.shape
    tr = _pick(R, 256, 16)

    def body(p_ref, o_ref):
        acc = p_ref[0].astype(F32)
        for r in range(1, N_DEV):
            acc = acc + p_ref[r].astype(F32)
        o_ref[...] = acc

    return pl.pallas_call(
        body, name=name, grid=(R // tr,), in_specs=[pl.BlockSpec((N_DEV, tr, C), lambda i: (0, i, 0))],
        out_specs=pl.BlockSpec((tr, C), lambda i: (i, 0)), out_shape=jax.ShapeDtypeStruct((R, C), F32),
        compiler_params=pltpu.CompilerParams(dimension_semantics=("parallel",)),
    )(parts)


def _adamw_layer(g, w, m, v, layer, prev, *, name):
    depth, R, C = w.shape
    tr = _pick(R, 64, 16)
    summed = g.ndim == 3
    if prev is None:
        prev = [lax.empty((depth, R, C), F32) for _ in range(4)]

    def body(g_ref, w_ref, m_ref, v_ref, *rest):
        outs = rest[4:]
        if summed:
            gv = g_ref[0].astype(F32)
            for r in range(1, N_DEV):
                gv = gv + g_ref[r].astype(F32)
        else:
            gv = g_ref[...]
        d, nm, nv = _adamw_math(w_ref[...], gv, m_ref[...], v_ref[...])
        for o, val in zip(outs, (gv, d, nm, nv)):
            o[...] = val

    g_spec = pl.BlockSpec((N_DEV, tr, C), lambda i: (0, i, 0)) if summed else pl.BlockSpec((tr, C), lambda i: (i, 0))
    slab = pl.BlockSpec((None, tr, C), lambda i: (layer, i, 0))
    return pl.pallas_call(
        body, name=name, grid=(R // tr,), in_specs=[g_spec] + [slab] * 3 + [pl.BlockSpec(memory_space=pl.ANY)] * 4,
        out_specs=[slab] * 4, out_shape=[jax.ShapeDtypeStruct((depth, R, C), F32)] * 4,
        input_output_aliases={4 + q: q for q in range(4)},
        compiler_params=pltpu.CompilerParams(dimension_semantics=("parallel",)),
    )(g, w, m, v, *prev)


def _lane_view(shape):
    if len(shape) == 1:
        return (1, shape[0])
    size = math.prod(shape)
    if shape[-1] < 64 and size % LANES == 0:
        return (size // LANES, LANES)
    if len(shape) > 2 and shape[-2] % 8 == 0:
        return (size // shape[-1], shape[-1])
    return tuple(shape)


def _adamw_many(ws, gs, ms, vs, *, name):
    n = len(ws)
    views = [_lane_view(w.shape) for w in ws]

    def body(*refs):
        for i in range(n):
            d, nm, nv = _adamw_math(refs[i][...], refs[n + i][...], refs[2 * n + i][...], refs[3 * n + i][...])
            refs[4 * n + i][...] = d
            refs[5 * n + i][...] = nm
            refs[6 * n + i][...] = nv

    outs = pl.pallas_call(
        body, name=name, out_shape=[jax.ShapeDtypeStruct(s, F32) for s in views] * 3,
    )(*[a.reshape(s) for arrs in (ws, gs, ms, vs) for a, s in zip(arrs, views)])
    return [[o.reshape(w.shape) for o, w in zip(outs[q * n:(q + 1) * n], ws)] for q in range(3)]


def _pack(arrs, dtype=F32):
    flat = jnp.concatenate([a.reshape(-1).astype(dtype) for a in arrs])
    pad = (-flat.shape[0]) % 2048
    return jnp.pad(flat, (0, pad)).reshape(-1, 128)


def _unpack(buf, shapes):
    flat, out, off = buf.reshape(-1), [], 0
    for s in shapes:
        n = math.prod(s)
        out.append(flat[off:off + n].reshape(s))
        off += n
    return out


def _s5_params(lam_re, lam_im, log_step, b_re, b_im):
    G, P, H = b_re.shape
    step = jnp.exp(log_step)[:, None]
    mag = jnp.exp(lam_re * step)
    ar, ai = mag * jnp.cos(lam_im * step), mag * jnp.sin(lam_im * step)
    den = lam_re * lam_re + lam_im * lam_im
    qr = ((ar - 1.0) * lam_re + ai * lam_im) / den
    qi = (ai * lam_re - (ar - 1.0) * lam_im) / den
    bbr = qr[..., None] * b_re - qi[..., None] * b_im
    bbi = qr[..., None] * b_im + qi[..., None] * b_re
    return ar.reshape(-1), ai.reshape(-1), _s5_blocks_in(bbr), _s5_blocks_in(bbi)


def _s5_blocks_in(b):
    G, P, H = b.shape
    eye = jnp.eye(8, dtype=b.dtype)
    return jnp.einsum("kgph,gj->kghjp", b.reshape(G // 8, 8, P, H), eye).reshape(G // 8, 8 * H, 8 * P)


def _s5_blocks_in_diag(m, G, P, H):
    return jnp.einsum("kghgp->kgph", m.reshape(G // 8, 8, H, 8, P)).reshape(G, P, H)


def _s5_blocks_out(c):
    G, H, P = c.shape
    eye = jnp.eye(8, dtype=c.dtype)
    return jnp.einsum("kghp,gj->kgpjh", c.reshape(G // 8, 8, H, P), eye).reshape(G // 8, 8 * P, 8 * H)


def _s5_blocks_out_diag(m, G, H, P):
    return jnp.einsum("kgpgh->kghp", m.reshape(G // 8, 8, P, 8, H)).reshape(G, H, P)


def _head_blocks(w):
    nh, d, e = w.shape
    return jnp.einsum("hde,hk->hdke", w, jnp.eye(nh, dtype=w.dtype)).reshape(nh * d, nh * e)


def _head_blocks_diag(m, nh, d, e):
    return jnp.einsum("hdhe->hde", m.reshape(nh, d, nh, e))


def _pad_rows(w, rows):
    return jnp.pad(w, ((0, rows - w.shape[0]), (0, 0)))


def _unshard_cols(g):
    return jnp.transpose(g, (1, 0, 2)).reshape(g.shape[1], -1)


_NAMES = ['norm_mix_g', 'w_in', 's5_lam_re', 's5_lam_im', 's5_log_step', 's5_b_re', 's5_b_im', 's5_c_re', 's5_c_im', 's5_d',
          's5_w_glu', 's5_b_glu', 'cv_w_dw', 'cv_b_dw', 'cv_ln_g', 'cv_ln_b', 'cv_w_pw', 'cv_b_pw', 'lru_w_conv', 'lru_b_conv',
          'lru_w_r', 'lru_b_r', 'lru_w_i', 'lru_b_i', 'lru_lam', 'pool_w', 'pool_scale', 'w_out', 'norm_ffn_g', 'ffn_w_up',
          'ffn_w_dw', 'ffn_b_dw', 'ffn_w_down', 'norm_final_g']
_MATRIX = ('w_in', 'w_out', 'ffn_w_up', 'ffn_w_down', 's5_w_glu', 'cv_w_pw')
_TRANSPOSED = ('w_in', 'ffn_w_up')
_COLSHARD = ('cv_w_dw', 'lru_w_conv', 'ffn_w_dw')
_GROUPS = (('w_in', 's5_w_glu', 'cv_w_pw', 'w_out'), ('ffn_w_up', 'ffn_w_down'))
_SMALL_BF16 = ('s5_b_re', 's5_b_im', 's5_c_re', 's5_c_im', 'lru_w_r', 'lru_w_i', 'pool_w')
_SCATTER_GROUPS = (('w_out', 's5_w_glu', 'cv_w_pw'), ('w_in',), ('ffn_w_up', 'ffn_w_down'))


def kernel(x, norm_mix_g, w_in, s5_lam_re, s5_lam_im, s5_log_step, s5_b_re, s5_b_im, s5_c_re, s5_c_im, s5_d, s5_w_glu, s5_b_glu, cv_w_dw, cv_b_dw, cv_ln_g, cv_ln_b, cv_w_pw, cv_b_pw, lru_w_conv, lru_b_conv, lru_w_r, lru_b_r, lru_w_i, lru_b_i, lru_lam, pool_w, pool_scale, w_out, norm_ffn_g, ffn_w_up, ffn_w_dw, ffn_b_dw, ffn_w_down, norm_final_g, loss_target, m_norm_mix_g, m_w_in, m_s5_lam_re, m_s5_lam_im, m_s5_log_step, m_s5_b_re, m_s5_b_im, m_s5_c_re, m_s5_c_im, m_s5_d, m_s5_w_glu, m_s5_b_glu, m_cv_w_dw, m_cv_b_dw, m_cv_ln_g, m_cv_ln_b, m_cv_w_pw, m_cv_b_pw, m_lru_w_conv, m_lru_b_conv, m_lru_w_r, m_lru_b_r, m_lru_w_i, m_lru_b_i, m_lru_lam, m_pool_w, m_pool_scale, m_w_out, m_norm_ffn_g, m_ffn_w_up, m_ffn_w_dw, m_ffn_b_dw, m_ffn_w_down, m_norm_final_g, v_norm_mix_g, v_w_in, v_s5_lam_re, v_s5_lam_im, v_s5_log_step, v_s5_b_re, v_s5_b_im, v_s5_c_re, v_s5_c_im, v_s5_d, v_s5_w_glu, v_s5_b_glu, v_cv_w_dw, v_cv_b_dw, v_cv_ln_g, v_cv_ln_b, v_cv_w_pw, v_cv_b_pw, v_lru_w_conv, v_lru_b_conv, v_lru_w_r, v_lru_b_r, v_lru_w_i, v_lru_b_i, v_lru_lam, v_pool_w, v_pool_scale, v_w_out, v_norm_ffn_g, v_ffn_w_up, v_ffn_w_dw, v_ffn_b_dw, v_ffn_w_down, v_norm_final_g):
    args = (x, norm_mix_g, w_in, s5_lam_re, s5_lam_im, s5_log_step, s5_b_re, s5_b_im, s5_c_re, s5_c_im, s5_d, s5_w_glu, s5_b_glu, cv_w_dw, cv_b_dw, cv_ln_g, cv_ln_b, cv_w_pw, cv_b_pw, lru_w_conv, lru_b_conv, lru_w_r, lru_b_r, lru_w_i, lru_b_i, lru_lam, pool_w, pool_scale, w_out, norm_ffn_g, ffn_w_up, ffn_w_dw, ffn_b_dw, ffn_w_down, norm_final_g, loss_target, m_norm_mix_g, m_w_in, m_s5_lam_re, m_s5_lam_im, m_s5_log_step, m_s5_b_re, m_s5_b_im, m_s5_c_re, m_s5_c_im, m_s5_d, m_s5_w_glu, m_s5_b_glu, m_cv_w_dw, m_cv_b_dw, m_cv_ln_g, m_cv_ln_b, m_cv_w_pw, m_cv_b_pw, m_lru_w_conv, m_lru_b_conv, m_lru_w_r, m_lru_b_r, m_lru_w_i, m_lru_b_i, m_lru_lam, m_pool_w, m_pool_scale, m_w_out, m_norm_ffn_g, m_ffn_w_up, m_ffn_w_dw, m_ffn_b_dw, m_ffn_w_down, m_norm_final_g, v_norm_mix_g, v_w_in, v_s5_lam_re, v_s5_lam_im, v_s5_log_step, v_s5_b_re, v_s5_b_im, v_s5_c_re, v_s5_c_im, v_s5_d, v_s5_w_glu, v_s5_b_glu, v_cv_w_dw, v_cv_b_dw, v_cv_ln_g, v_cv_ln_b, v_cv_w_pw, v_cv_b_pw, v_lru_w_conv, v_lru_b_conv, v_lru_w_r, v_lru_b_r, v_lru_w_i, v_lru_b_i, v_lru_lam, v_pool_w, v_pool_scale, v_w_out, v_norm_ffn_g, v_ffn_w_up, v_ffn_w_dw, v_ffn_b_dw, v_ffn_w_down, v_norm_final_g)
    x, target = args[0], args[35]
    W = dict(zip(_NAMES, args[1:35]))
    Mo = dict(zip(_NAMES, args[36:70]))
    Vo = dict(zip(_NAMES, args[70:104]))
    x = x[0]
    target = target[0]
    T, D = x.shape
    depth = W['w_in'].shape[0]
    F = W['ffn_w_down'].shape[1] * N_DEV
    G, P, H = W['s5_b_re'].shape[1:]
    MW = G * H
    nh, hd, _ = W['lru_w_r'].shape[1:]
    me = _rank(_my_pos())

    gather, full = {}, {}

    def start_group(l, gi, after):
        shards = [(W[n][l].T if n in _TRANSPOSED else W[n][l]).astype(BF16) for n in _GROUPS[gi]]
        if (l, gi) == (0, 0):
            shards.append(_pack([W[n][k] for k in range(depth) for n in _COLSHARD]))
        gather[l, gi], tok = _gather_start(shards, "gather_%d%d" % (l, gi), after)
        return tok

    def land_weights(l, gi, after):
        got = _gather_wait(gather[l, gi], after)
        for n, g8 in zip(_GROUPS[gi], got):
            full[n, l] = g8.reshape(-1, g8.shape[-1])
        return got

    started = start_group(0, 0, None)[0, 0]
    got = land_weights(0, 0, W['norm_mix_g'][0] + started)
    started = start_group(0, 1, got[0])[0, 0]
    taps8 = got[-1]
    tap_shapes = [W[n][l].shape for l in range(depth) for n in _COLSHARD]
    taps = [_unshard_cols(jnp.stack(t)) for t in zip(*[_unpack(taps8[r], tap_shapes) for r in range(N_DEV)])]
    for l in range(depth):
        for j, n in enumerate(_COLSHARD):
            full[n, l] = taps[l * len(_COLSHARD) + j]

    saved = []
    for l in range(depth):
        g_mix = W['norm_mix_g'][l] + (0.0 if l else started)
        if l:
            land_weights(l, 0, x)
        s5p, s5p_vjp = jax.vjp(_s5_params, W['s5_lam_re'][l], W['s5_lam_im'][l], W['s5_log_step'][l], W['s5_b_re'][l], W['s5_b_im'][l])
        ar, ai, bre, bim = s5p
        cre, cim = _s5_blocks_out(W['s5_c_re'][l]), _s5_blocks_out(W['s5_c_im'][l])
        sp8, sp8_vjp = jax.vjp(lambda lam: LRU_C * jax.nn.softplus(-lam), W['lru_lam'][l])
        wr, wi = _head_blocks(W['lru_w_r'][l]), _head_blocks(W['lru_w_i'][l])
        c = dict(bre=bre.astype(BF16), bim=bim.astype(BF16), cre=cre.astype(BF16), cim=cim.astype(BF16),
                 cst_f=_s5_scan_consts(ar, ai, False), cst_b=_s5_scan_consts(ar, ai, True), sp8=sp8, wr=wr.astype(BF16),
                 wi=wi.astype(BF16), cvw=_pad_rows(full['cv_w_dw', l], 32), lruw=_pad_rows(full['lru_w_conv', l], 8),
                 ffw=_pad_rows(full['ffn_w_dw', l], 8), pw=W['pool_w'][l].astype(BF16), s5p_vjp=s5p_vjp, sp8_vjp=sp8_vjp, x0=x)
        h = _rmsnorm(x, g_mix, name="mix_norm")
        proj = _matmul(h, full['w_in', l], mode="nt", tm=1024, tn=512, tk=D, out_dtype=F32, name="in_proj")
        y0, sre, sim, yp, z = _s5_fwd(proj, 0, c['bre'], c['bim'], c['cre'], c['cim'], c['cst_f'], W['s5_d'][l],
                                     full['s5_w_glu', l], W['s5_b_glu'][l])
        y1, h1 = _cv_fwd(proj, 1, 2, c['cvw'], W['cv_b_dw'][l], W['cv_ln_g'][l], W['cv_ln_b'][l], full['cv_w_pw', l], W['cv_b_pw'][l])
        y2, xc, rr, ig, hprev = _lru_fwd(proj, 3, 4, c['lruw'], W['lru_b_conv'][l], c['wr'], W['lru_b_r'][l], c['wi'],
                                         W['lru_b_i'][l], sp8)
        y3 = _pool_fwd(proj, 5, c['pw'], W['pool_scale'][l])
        mixed = jnp.concatenate([y0, y1, y2, y3], axis=1)
        x1 = _matmul(mixed, full['w_out', l], mode="nn", tm=1024, tn=512, tk=2048, out_dtype=F32, name="out_proj", add=x)
        got = land_weights(l, 1, x1)
        g_ffn = W['norm_ffn_g'][l]
        if l + 1 < depth:
            g_ffn = g_ffn + start_group(l + 1, 0, got[0])[0, 0] + start_group(l + 1, 1, got[0])[0, 0]
        h2 = _rmsnorm(x1, g_ffn, name="ffn_norm")
        up = _matmul(h2, full['ffn_w_up', l], mode="nt", tm=2048, tn=256, tk=D, out_dtype=BF16, name="ffn_up")
        act = _ffn_mid_fwd(up, c['ffw'], W['ffn_b_dw'][l])
        x = _matmul(act, full['ffn_w_down', l], mode="nn", tm=1024, tn=512, tk=F, out_dtype=F32, name="ffn_down", add=x1)
        c.update(proj=proj, h=h, sre=sre, sim=sim, yp=yp, z=z, h1=h1, xc=xc, rr=rr, ig=ig, hprev=hprev, mixed=mixed, x1=x1,
                 up=up, h2=h2, act=act)
        saved.append(c)

    dx, dxb, dgf, se = _loss_head(x, W['norm_final_g'], target)
    loss = lax.psum(0.5 / D * jnp.sum(se), AXES)
    gsmall = {('norm_final_g', None): dgf.sum(0)}
    gmat, scatter = {}, {}

    def parts_of(m):
        return m.reshape(N_DEV, m.shape[0] // N_DEV, m.shape[1])

    for l in reversed(range(depth)):
        c = saved[l]
        dact = _matmul(dxb, full['ffn_w_down', l], mode="nt", tm=512, tn=F, tk=512, out_dtype=BF16, name="d_act")
        gmat['ffn_w_down', l] = _matmul(c['act'], dxb, mode="tn", tm=F, tn=512, tk=512, out_dtype=BF16, name="dw_down")
        dup, dffw, dffb = _ffn_mid_bwd(dact, c['up'], c['ffw'], W['ffn_b_dw'][l])
        gsmall['ffn_w_dw', l] = dffw.sum(1)[:3]
        gsmall['ffn_b_dw', l] = dffb.sum(0)
        dh2 = _matmul(dup, full['ffn_w_up', l], mode="nn", tm=1024, tn=512, tk=F, out_dtype=F32, name="d_h2")
        gmat['ffn_w_up', l] = _matmul(dup, c['h2'], mode="tn", tm=256, tn=D, tk=T, out_dtype=BF16, name="dw_up")
        scatter[l, 2], tok = _scatter_start([parts_of(gmat[n, l]) for n in _SCATTER_GROUPS[2]], "scatter_%d2" % l)
        dx1, dx1b, dg2 = _rms_bwd(dx, dh2, c['x1'], W['norm_ffn_g'][l] + tok[0, 0], name="ffn_norm_bwd")
        gsmall['norm_ffn_g', l] = dg2.sum(0)
        dmix = _matmul(dx1b, full['w_out', l], mode="nt", tm=1024, tn=512, tk=2048, out_dtype=F32, name="d_mixed")
        gmat['w_out', l] = _matmul(c['mixed'], dx1b, mode="tn", tm=512, tn=D, tk=T, out_dtype=BF16, name="dw_out")
        du, dbre, dbim, dcre, dcim, da, dwglu, v3 = _s5_bwd(dmix, 0, c['proj'], 0, c['sre'], c['sim'], c['yp'], c['z'], c['bre'],
                                                            c['bim'], c['cre'], c['cim'], c['cst_b'], W['s5_d'][l], full['s5_w_glu', l])
        da = da.sum(1)
        glr, gli, gls, gbr, gbi = c['s5p_vjp']((da[0], da[1], dbre, dbim))
        gsmall['s5_lam_re', l], gsmall['s5_lam_im', l], gsmall['s5_log_step', l] = glr, gli, gls
        gsmall['s5_b_re', l], gsmall['s5_b_im', l] = gbr, gbi
        gsmall['s5_c_re', l] = _s5_blocks_out_diag(dcre, G, H, P)
        gsmall['s5_c_im', l] = _s5_blocks_out_diag(dcim, G, H, P)
        gsmall['s5_b_glu', l], gsmall['s5_d', l] = v3[0].sum(0), v3[1].sum(0)
        gmat['s5_w_glu', l] = dwglu.astype(BF16)
        dv, dg, dwpw, dcvw, v5 = _cv_bwd(dmix, 1, c['proj'], 1, 2, c['h1'], c['cvw'], W['cv_ln_g'][l], W['cv_ln_b'][l], full['cv_w_pw', l])
        gmat['cv_w_pw', l] = dwpw.astype(BF16)
        gsmall['cv_w_dw', l] = dcvw.sum(1)[:31]
        gsmall['cv_b_pw', l], gsmall['cv_ln_g', l], gsmall['cv_ln_b', l], gsmall['cv_b_dw', l] = [v5[j].sum(0) for j in range(4)]
        scatter[l, 0], tok = _scatter_start([parts_of(gmat[n, l]) for n in _SCATTER_GROUPS[0]], "scatter_%d0" % l)
        tok = tok[0, 0]
        dlx, dlg, dwr, dwi, dlw, v4 = _lru_bwd(dmix, 2, c['proj'], 3, 4, c['xc'], c['rr'], c['ig'], c['hprev'], c['lruw'], c['wr'],
                                               c['wi'], c['sp8'] + tok)
        gsmall['lru_w_r', l], gsmall['lru_w_i', l] = _head_blocks_diag(dwr, nh, hd, hd), _head_blocks_diag(dwi, nh, hd, hd)
        gsmall['lru_w_conv', l] = dlw.sum(1)[:4]
        gsmall['lru_b_r', l], gsmall['lru_b_i', l] = v4[0].sum(0), v4[1].sum(0)
        gsmall['lru_lam', l] = c['sp8_vjp'](v4[2].sum(0))[0]
        gsmall['lru_b_conv', l] = v4[3].sum(0)
        dpx, dpw, dps = _pool_bwd(dmix, 3, c['proj'], 5, c['pw'], W['pool_scale'][l] + tok)
        gsmall['pool_w', l], gsmall['pool_scale', l] = dpw, dps.sum(0)
        dproj = jnp.concatenate([du, dv, dg, dlx, dlg, dpx], axis=1)
        dh = _matmul(dproj, full['w_in', l], mode="nn", tm=1024, tn=512, tk=dproj.shape[1], out_dtype=F32, name="d_h")
        gmat['w_in', l] = _matmul(dproj, c['h'], mode="tn", tm=512, tn=D, tk=T, out_dtype=BF16, name="dw_in")
        scatter[l, 1], tok = _scatter_start([parts_of(gmat[n, l]) for n in _SCATTER_GROUPS[1]], "scatter_%d1" % l)
        dx, dxb, dg1 = _rms_bwd(dx1, dh, c['x0'], W['norm_mix_g'][l] + tok[0, 0], name="mix_norm_bwd")
        gsmall['norm_mix_g', l] = dg1.sum(0)

    small_names = [n for n in _NAMES if n not in _MATRIX]
    small_split = ([n for n in small_names if n not in _SMALL_BF16], [n for n in small_names if n in _SMALL_BF16])
    small_packs, full_shapes = [], []
    for names, dtype in zip(small_split, (F32, BF16)):
        layers = [[None] if n == 'norm_final_g' else range(depth) for n in names]
        small_packs.append(_pack([gsmall[n, l] for n, ls in zip(names, layers) for l in ls], dtype))
        full_shapes.append([gsmall[n, None].shape if n == 'norm_final_g' else (depth,) + gsmall[n, 0].shape for n in names])
    small_state, tok = _gather_start(small_packs, "small_grads")

    stored_t = [n for n in _TRANSPOSED if W[n].shape[-1] % LANES]
    wmv = {n: [jnp.swapaxes(t[n], 1, 2) if n in stored_t else t[n] for t in (W, Mo, Vo)] for n in _MATRIX}
    stacked = {n: None for n in _MATRIX}

    def update_matrices(l, after):
        for gi in (2, 0, 1):
            for n, p8 in zip(_SCATTER_GROUPS[gi], _scatter_wait(scatter[l, gi], after)):
                g = _sum8(p8, name="sum_%s_%d" % (n, l)).T if n in _TRANSPOSED and n not in stored_t else p8
                stacked[n] = _adamw_layer(g, *wmv[n], l, stacked[n], name="adamw_%s_%d" % (n, l))

    after = tok
    for l in reversed(range(depth)):
        update_matrices(l, after)
        after = sum(stacked[n][1][l, 0, 0] for n in _MATRIX).reshape(1, 1)
    after_matrices = after
    gs = {}
    for names, shapes, small8 in zip(small_split, full_shapes, _gather_wait(small_state, after_matrices)):
        gs.update(zip(names, _unpack(_sum8(small8, name="sum_small_" + str(small8.dtype)), shapes)))
    for n in _COLSHARD:
        cw = W[n].shape[-1]
        gs[n] = lax.dynamic_slice_in_dim(gs[n], me * cw, cw, axis=gs[n].ndim - 1)
    upd = _adamw_many([W[n] for n in small_names], [gs[n] for n in small_names], [Mo[n] for n in small_names],
                      [Vo[n] for n in small_names], name="adamw_small")
    out = {n: (gs[n], upd[0][j], upd[1][j], upd[2][j]) for j, n in enumerate(small_names)}
    for n in _MATRIX:
        out[n] = tuple(jnp.swapaxes(t, 1, 2) for t in stacked[n]) if n in stored_t else tuple(stacked[n])
    return (loss, dx[None]) + tuple(out[n][q] for q in range(4) for n in _NAMES)
```

```python
import functools
import math

import jax
import jax.numpy as jnp
from jax import lax
from jax.experimental import pallas as pl
from jax.experimental.pallas import tpu as pltpu

F32 = jnp.float32
BF16 = jnp.bfloat16
N_DEV = 8
AXES = ("x", "y", "c")
EPS = 1e-6
S5_GROUP_CH = 16
S5_STATE = 64
LRU_HEADS = 8
LRU_C = 8.0
POOL_WINDOWS = (2, 4, 8, 16)
CONV_HALO = 32
SMALL_HALO = 8
FFN_HALO = 16
POOL_HALO = 16
LANES = 128
ADAM_LR, ADAM_B1, ADAM_B2, ADAM_EPS, ADAM_WD, ADAM_STEP = 0.001, 0.9, 0.999, 1e-08, 0.01, 10
_GELU_K = math.sqrt(2.0 / math.pi)
ARB = pltpu.CompilerParams(dimension_semantics=("arbitrary",))


def _pick(n, pref, mult):
    best = None
    for t in range(mult, min(n, pref) + 1, mult):
        if n % t == 0:
            best = t
    return best if best is not None else n


def _sigmoid(x):
    return 1.0 / (1.0 + jnp.exp(-x))


def _gelu(x):
    return 0.5 * x * (1.0 + jnp.tanh(_GELU_K * (x + 0.044715 * x * x * x)))


def _gelu_grad(x):
    t = jnp.tanh(_GELU_K * (x + 0.044715 * x * x * x))
    return 0.5 * (1.0 + t) + 0.5 * x * (1.0 - t * t) * _GELU_K * (1.0 + 3.0 * 0.044715 * x * x)


def _gelu_and_grad(x):
    t = jnp.tanh(_GELU_K * (x + 0.044715 * x * x * x))
    half = 0.5 * (1.0 + t)
    return x * half, half + 0.5 * x * (1.0 - t * t) * _GELU_K * (1.0 + 3.0 * 0.044715 * x * x)


def _neg_expm1(x):
    series = -x * (1.0 + x * (0.5 + x * (1.0 / 6.0 + x * (1.0 / 24.0 + x * (1.0 / 120.0 + x * (1.0 / 720.0))))))
    return jnp.where(jnp.abs(x) < 0.25, series, 1.0 - jnp.exp(x))


def _fold8(x):
    return x.reshape(x.shape[0] // 8, 8, x.shape[1]).sum(axis=0)


def _dot(a, b):
    return jnp.dot(a.astype(BF16), b.astype(BF16), preferred_element_type=F32)


def _dot_nt(a, b):
    return lax.dot_general(a.astype(BF16), b.astype(BF16), (((1,), (1,)), ((), ())), preferred_element_type=F32)


def _dot_tn(a, b):
    return lax.dot_general(a.astype(BF16), b.astype(BF16), (((0,), (0,)), ((), ())), preferred_element_type=F32)


def _full(shape):
    nd = len(shape)
    return pl.BlockSpec(shape, lambda *i: (0,) * nd)


def _matmul(a, b, *, mode, tm, tn, tk, out_dtype, name, add=None):
    if mode == "nn":
        (M, K), N = a.shape, b.shape[1]
    elif mode == "nt":
        (M, K), N = a.shape, b.shape[0]
    else:
        (K, M), N = a.shape, b.shape[1]
    tm, tn, tk = _pick(M, tm, 8), _pick(N, tn, 128), _pick(K, tk, 128)
    if mode == "tn":
        tm = _pick(M, tm, 128)
    nk = K // tk
    a_spec = {"nn": pl.BlockSpec((tm, tk), lambda i, j, k: (i, k)), "nt": pl.BlockSpec((tm, tk), lambda i, j, k: (i, k)),
              "tn": pl.BlockSpec((tk, tm), lambda i, j, k: (k, i))}[mode]
    b_spec = {"nn": pl.BlockSpec((tk, tn), lambda i, j, k: (k, j)), "nt": pl.BlockSpec((tn, tk), lambda i, j, k: (j, k)),
              "tn": pl.BlockSpec((tk, tn), lambda i, j, k: (k, j))}[mode]
    o_spec = pl.BlockSpec((tm, tn), lambda i, j, k: (i, j))
    dot = {"nn": _dot, "nt": _dot_nt, "tn": _dot_tn}[mode]
    has_add = add is not None

    def body(*refs):
        if has_add:
            a_ref, b_ref, add_ref, o_ref, acc_ref = refs
        else:
            a_ref, b_ref, o_ref, acc_ref = refs
        if nk == 1:
            part = dot(a_ref[...], b_ref[...])
            o_ref[...] = (part + add_ref[...] if has_add else part).astype(out_dtype)
            return
        k = pl.program_id(2)

        @pl.when(k == 0)
        def _():
            acc_ref[...] = jnp.zeros_like(acc_ref)

        acc_ref[...] += dot(a_ref[...], b_ref[...])

        @pl.when(k == nk - 1)
        def _():
            r = acc_ref[...]
            if has_add:
                r = r + add_ref[...]
            o_ref[...] = r.astype(out_dtype)

    ins = [a, b] + ([add] if has_add else [])
    specs = [a_spec, b_spec] + ([o_spec] if has_add else [])
    return pl.pallas_call(
        body, name=name, grid=(M // tm, N // tn, nk), in_specs=specs, out_specs=o_spec,
        out_shape=jax.ShapeDtypeStruct((M, N), out_dtype),
        scratch_shapes=[pltpu.VMEM((tm, tn) if nk > 1 else (8, LANES), F32)],
        compiler_params=pltpu.CompilerParams(dimension_semantics=("parallel", "parallel", "arbitrary")),
    )(*ins)


def _rmsnorm(x, g, *, name):
    T, D = x.shape
    tc = _pick(T, 512, 16)

    def body(x_ref, g_ref, h_ref):
        def rows16(n, _):
            rows = pl.ds(pl.multiple_of(n * 16, 16), 16)
            xv = x_ref[rows, :]
            r = lax.rsqrt(jnp.mean(xv * xv, axis=-1, keepdims=True) + EPS)
            h_ref[rows, :] = (xv * r * g_ref[...]).astype(BF16)
            return 0

        lax.fori_loop(0, tc // 16, rows16, 0, unroll=4)

    row = pl.BlockSpec((tc, D), lambda i: (i, 0))
    return pl.pallas_call(
        body, name=name, grid=(T // tc,), in_specs=[row, _full((1, D))], out_specs=row,
        out_shape=jax.ShapeDtypeStruct((T, D), BF16), compiler_params=pltpu.CompilerParams(dimension_semantics=("parallel",)),
    )(x, g.reshape(1, D))


def _rms_bwd(dres, dh, x, g, *, name):
    T, D = x.shape
    tc = _pick(T, 512, 16)

    def body(dres_ref, dh_ref, x_ref, g_ref, dx_ref, dxb_ref, dg_ref):
        @pl.when(pl.program_id(0) == 0)
        def _():
            dg_ref[...] = jnp.zeros_like(dg_ref)

        def rows16(n, _):
            rows = pl.ds(pl.multiple_of(n * 16, 16), 16)
            xv, dhv = x_ref[rows, :], dh_ref[rows, :]
            r = lax.rsqrt(jnp.mean(xv * xv, axis=-1, keepdims=True) + EPS)
            dg_ref[...] += _fold8(dhv * xv * r)
            dyg = dhv * g_ref[...]
            dx = dres_ref[rows, :] + r * dyg - xv * (r * r * r) * jnp.mean(dyg * xv, axis=-1, keepdims=True)
            dx_ref[rows, :] = dx
            dxb_ref[rows, :] = dx.astype(BF16)
            return 0

        lax.fori_loop(0, tc // 16, rows16, 0, unroll=4)

    row = pl.BlockSpec((tc, D), lambda i: (i, 0))
    return pl.pallas_call(
        body, name=name, grid=(T // tc,), in_specs=[row, row, row, _full((1, D))],
        out_specs=[row, row, _full((8, D))],
        out_shape=[jax.ShapeDtypeStruct((T, D), F32), jax.ShapeDtypeStruct((T, D), BF16), jax.ShapeDtypeStruct((8, D), F32)],
        compiler_params=ARB,
    )(dres, dh, x, g.reshape(1, D))


def _loss_head(x, g, target):
    T, D = x.shape
    tc = _pick(T, 256, 8)

    def body(x_ref, g_ref, t_ref, dx_ref, dxb_ref, dg_ref, se_ref):
        @pl.when(pl.program_id(0) == 0)
        def _():
            dg_ref[...] = jnp.zeros_like(dg_ref)
            se_ref[...] = jnp.zeros_like(se_ref)

        def rows16(n, _):
            rows = pl.ds(pl.multiple_of(n * 16, 16), 16)
            xv = x_ref[rows, :]
            r = lax.rsqrt(jnp.mean(xv * xv, axis=-1, keepdims=True) + EPS)
            gv = g_ref[...]
            err = xv * r * gv - t_ref[rows, :]
            se_ref[...] += _fold8(err * err)
            dy = err * (1.0 / D)
            dg_ref[...] += _fold8(dy * xv * r)
            dyg = dy * gv
            dx = r * dyg - xv * (r * r * r) * jnp.mean(dyg * xv, axis=-1, keepdims=True)
            dx_ref[rows, :] = dx
            dxb_ref[rows, :] = dx.astype(BF16)
            return 0

        lax.fori_loop(0, tc // 16, rows16, 0, unroll=4)

    row = pl.BlockSpec((tc, D), lambda i: (i, 0))
    return pl.pallas_call(
        body, name="loss_head", grid=(T // tc,), in_specs=[row, _full((1, D)), row],
        out_specs=[row, row, _full((8, D)), _full((8, D))],
        out_shape=[jax.ShapeDtypeStruct((T, D), F32), jax.ShapeDtypeStruct((T, D), BF16),
                   jax.ShapeDtypeStruct((8, D), F32), jax.ShapeDtypeStruct((8, D), F32)],
        compiler_params=ARB,
    )(x, g.reshape(1, D), target)


def _s5_scan_consts(ar, ai, reverse):
    if reverse:
        ai = -ai
    pr, pi_ = [ar], [ai]
    for _ in range(7):
        pr, pi_ = pr + [pr[-1] * ar - pi_[-1] * ai], pi_ + [pr[-1] * ai + pi_[-1] * ar]
    rows = jnp.arange(8)[:, None]
    planes = []
    for d in (1, 2, 4):
        mask = (rows <= 7 - d) if reverse else (rows >= d)
        planes += [jnp.where(mask, pr[d - 1][None, :], 0.0), jnp.where(mask, pi_[d - 1][None, :], 0.0)]
    order = list(range(7, -1, -1)) if reverse else list(range(8))
    planes += [jnp.stack([pr[k] for k in order]), jnp.stack([pi_[k] for k in order])]
    return jnp.stack(planes).astype(F32)


def _s5_block_scan(xr_ref, xi_ref, cst_ref, car_ref, nblk, reverse, lb, extra=None):
    ns = xr_ref.shape[1]
    crow = 0 if reverse else 7

    def blk(i, _):
        ib = (nblk - 1 - i) if reverse else i
        r0 = pl.multiple_of(ib * 8, 8)
        for k in range(ns // lb):
            sl = slice(k * lb, (k + 1) * lb)
            xr, xi = xr_ref[pl.ds(r0, 8), sl], xi_ref[pl.ds(r0, 8), sl]
            for n, d in enumerate((1, 2, 4)):
                ar, ai = cst_ref[2 * n, :, sl], cst_ref[2 * n + 1, :, sl]
                sh = (8 - d) if reverse else d
                rr, ri = pltpu.roll(xr, sh, 0), pltpu.roll(xi, sh, 0)
                xr, xi = xr + ar * rr - ai * ri, xi + ar * ri + ai * rr
            pr, pi_ = cst_ref[6, :, sl], cst_ref[7, :, sl]
            cr, ci = car_ref[0, crow:crow + 1, sl], car_ref[1, crow:crow + 1, sl]
            xr, xi = xr + pr * cr - pi_ * ci, xi + pr * ci + pi_ * cr
            xr_ref[pl.ds(r0, 8), sl] = xr
            xi_ref[pl.ds(r0, 8), sl] = xi
            if extra is not None:
                extra(r0, sl, xr, xi, cr, ci)
            car_ref[0, :, sl] = xr
            car_ref[1, :, sl] = xi
        return 0

    lax.fori_loop(0, nblk, blk, 0)


def _s5_fwd(proj, col, bre, bim, cre, cim, cst, dvec, wglu, bglu):
    T = proj.shape[0]
    nsb, cw, sw = bre.shape
    M, NS = nsb * cw, nsb * sw
    tc = _pick(T, 512, 8)

    def body(u_ref, bre_ref, bim_ref, cre_ref, cim_ref, cst_ref, d_ref, wg_ref, bg_ref,
             y_ref, sre_ref, sim_ref, yp_ref, z_ref, car_ref):
        @pl.when(pl.program_id(0) == 0)
        def _():
            car_ref[...] = jnp.zeros_like(car_ref)

        u = u_ref[...]
        ub = u.astype(BF16)
        for k in range(nsb):
            sre_ref[:, k * sw:(k + 1) * sw] = _dot(ub[:, k * cw:(k + 1) * cw], bre_ref[k])
            sim_ref[:, k * sw:(k + 1) * sw] = _dot(ub[:, k * cw:(k + 1) * cw], bim_ref[k])
        _s5_block_scan(sre_ref, sim_ref, cst_ref, car_ref, tc // 8, False, sw)
        for k in range(nsb):
            yk = _dot(sre_ref[:, k * sw:(k + 1) * sw], cre_ref[k]) - _dot(sim_ref[:, k * sw:(k + 1) * sw], cim_ref[k])
            yp_ref[:, k * cw:(k + 1) * cw] = yk + d_ref[:, k * cw:(k + 1) * cw] * u[:, k * cw:(k + 1) * cw]
        gl = _gelu(yp_ref[...])
        z = _dot(gl, wg_ref[...]) + bg_ref[...]
        z_ref[...] = z
        y_ref[...] = (gl * _sigmoid(z)).astype(BF16)

    rowm = pl.BlockSpec((tc, M), lambda i: (i, 0))
    rows = pl.BlockSpec((tc, NS), lambda i: (i, 0))
    return pl.pallas_call(
        body, name="s5_fwd", grid=(T // tc,),
        in_specs=[pl.BlockSpec((tc, M), lambda i: (i, col)), _full(bre.shape), _full(bim.shape), _full(cre.shape),
                  _full(cim.shape), _full(cst.shape), _full((1, M)), _full((M, M)), _full((1, M))],
        out_specs=[rowm, rows, rows, rowm, rowm],
        out_shape=[jax.ShapeDtypeStruct((T, M), BF16), jax.ShapeDtypeStruct((T, NS), F32), jax.ShapeDtypeStruct((T, NS), F32),
                   jax.ShapeDtypeStruct((T, M), F32), jax.ShapeDtypeStruct((T, M), F32)],
        scratch_shapes=[pltpu.VMEM((2, 8, NS), F32)], compiler_params=ARB,
    )(proj, bre, bim, cre, cim, cst, dvec.reshape(1, M), wglu, bglu.reshape(1, M))


def _s5_bwd(dmix, dcol, proj, col, sre, sim, yp, z, bre, bim, cre, cim, cst, dvec, wglu):
    T = proj.shape[0]
    nsb, cw, sw = bre.shape
    M, NS = nsb * cw, nsb * sw
    tc = _pick(T, 512, 8)
    nc = T // tc

    def body(do_ref, u_ref, sre_ref, sim_ref, yp_ref, z_ref, bre_ref, bim_ref, cre_ref, cim_ref, cst_ref, d_ref, wg_ref,
             du_ref, dbre_ref, dbim_ref, dcre_ref, dcim_ref, da_ref, dwg_ref, vec_ref, gre_ref, gim_ref, car_ref):
        @pl.when(pl.program_id(0) == 0)
        def _():
            for r in (dbre_ref, dbim_ref, dcre_ref, dcim_ref, da_ref, dwg_ref, vec_ref, car_ref):
                r[...] = jnp.zeros_like(r)

        u, do, ypv = u_ref[...], do_ref[...], yp_ref[...]
        gl = _gelu(ypv)
        sg = _sigmoid(z_ref[...])
        dz = do * gl * sg * (1.0 - sg)
        dgl = do * sg + _dot_nt(dz, wg_ref[...])
        dwg_ref[...] += _dot_tn(gl, dz)
        vec_ref[0] += _fold8(dz)
        dy = dgl * _gelu_grad(ypv)
        vec_ref[1] += _fold8(dy * u)
        dyb = dy.astype(BF16)
        for k in range(nsb):
            cs, ss = slice(k * cw, (k + 1) * cw), slice(k * sw, (k + 1) * sw)
            gre_ref[:, ss] = _dot_nt(dyb[:, cs], cre_ref[k])
            gim_ref[:, ss] = -_dot_nt(dyb[:, cs], cim_ref[k])
            dcre_ref[k] += _dot_tn(sre_ref[:, ss], dyb[:, cs])
            dcim_ref[k] -= _dot_tn(sim_ref[:, ss], dyb[:, cs])

        row = lax.broadcasted_iota(jnp.int32, (8, sw), 0)

        def extra(r0, sl, gr, gi, cr, ci):
            nr = jnp.where(row == 7, cr, pltpu.roll(gr, 7, 0))
            ni = jnp.where(row == 7, ci, pltpu.roll(gi, 7, 0))
            sr, si = sre_ref[pl.ds(r0, 8), sl], sim_ref[pl.ds(r0, 8), sl]
            da_ref[0, :, sl] += nr * sr + ni * si
            da_ref[1, :, sl] += ni * sr - nr * si

        _s5_block_scan(gre_ref, gim_ref, cst_ref, car_ref, tc // 8, True, sw, extra)
        ub = u.astype(BF16)
        for k in range(nsb):
            cs, ss = slice(k * cw, (k + 1) * cw), slice(k * sw, (k + 1) * sw)
            gr, gi = gre_ref[:, ss].astype(BF16), gim_ref[:, ss].astype(BF16)
            duk = dy[:, cs] * d_ref[:, cs] + _dot_nt(gr, bre_ref[k]) + _dot_nt(gi, bim_ref[k])
            du_ref[:, cs] = duk.astype(BF16)
            dbre_ref[k] += _dot_tn(ub[:, cs], gr)
            dbim_ref[k] += _dot_tn(ub[:, cs], gi)

    def rowm(c):
        return pl.BlockSpec((tc, M), lambda i: (nc - 1 - i, c))
    rows = pl.BlockSpec((tc, NS), lambda i: (nc - 1 - i, 0))
    return pl.pallas_call(
        body, name="s5_bwd", grid=(nc,),
        in_specs=[rowm(dcol), rowm(col), rows, rows, rowm(0), rowm(0), _full(bre.shape), _full(bim.shape), _full(cre.shape),
                  _full(cim.shape), _full(cst.shape), _full((1, M)), _full((M, M))],
        out_specs=[rowm(0), _full(bre.shape), _full(bim.shape), _full(cre.shape), _full(cim.shape), _full((2, 8, NS)),
                   _full((M, M)), _full((3, 8, M))],
        out_shape=[jax.ShapeDtypeStruct((T, M), BF16), jax.ShapeDtypeStruct(bre.shape, F32), jax.ShapeDtypeStruct(bim.shape, F32),
                   jax.ShapeDtypeStruct(cre.shape, F32), jax.ShapeDtypeStruct(cim.shape, F32), jax.ShapeDtypeStruct((2, 8, NS), F32),
                   jax.ShapeDtypeStruct((M, M), F32), jax.ShapeDtypeStruct((3, 8, M), F32)],
        scratch_shapes=[pltpu.VMEM((tc, NS), F32), pltpu.VMEM((tc, NS), F32), pltpu.VMEM((2, 8, NS), F32)],
        compiler_params=ARB,
    )(dmix, proj, sre, sim, yp, z, bre, bim, cre, cim, cst, dvec.reshape(1, M), wglu)


def _ln_silu(h1, lg, lb):
    mu = jnp.mean(h1, axis=-1, keepdims=True)
    xc = h1 - mu
    rstd = lax.rsqrt(jnp.mean(xc * xc, axis=-1, keepdims=True) + EPS)
    xhat = xc * rstd
    h2 = xhat * lg + lb
    return xhat, rstd, h2, _sigmoid(h2)


def _shifted_copies(buf_ref, sh_ref, cs, n):
    for s in range(1, 8):
        sh_ref[s, 0:n, :] = buf_ref[pl.ds(s, n), cs]


def _tap(buf_ref, sh_ref, cs, off, r0, rows):
    q, s = divmod(off, 8)
    return buf_ref[pl.ds(r0 + 8 * q, rows), cs] if s == 0 else sh_ref[s, pl.ds(r0 + 8 * q, rows), :]


def _cv_fwd(proj, vcol, gcol, wdw, bdw, lg, lb, wpw, bpw):
    T = proj.shape[0]
    M = wpw.shape[0]
    tc = _pick(T, 256, 8)
    rb = _pick(tc, 64, 8)
    H, K = CONV_HALO, 31

    def body(v_ref, g_ref, wdw_ref, bdw_ref, lg_ref, lb_ref, wpw_ref, bpw_ref, y_ref, h1_ref, buf_ref, sh_ref):
        @pl.when(pl.program_id(0) == 0)
        def _():
            buf_ref[0:H, :] = jnp.zeros((H, M), F32)

        buf_ref[H:H + tc, :] = v_ref[...] * _sigmoid(g_ref[...])

        def strip(s, _):
            cs = pl.ds(pl.multiple_of(s * LANES, LANES), LANES)
            _shifted_copies(buf_ref, sh_ref, cs, tc + H - 8)
            for r0 in range(0, tc, rb):
                acc = jnp.zeros((rb, LANES), F32) + bdw_ref[:, cs]
                for k in range(K):
                    acc = acc + wdw_ref[k:k + 1, cs] * _tap(buf_ref, sh_ref, cs, H - (K - 1) + k, r0, rb)
                h1_ref[pl.ds(r0, rb), cs] = acc
            return 0

        lax.fori_loop(0, M // LANES, strip, 0)
        _, _, h2, sg = _ln_silu(h1_ref[...], lg_ref[...], lb_ref[...])
        y_ref[...] = (_dot(h2 * sg, wpw_ref[...]) + bpw_ref[...]).astype(BF16)
        buf_ref[0:H, :] = buf_ref[tc:tc + H, :]

    def rowm(c):
        return pl.BlockSpec((tc, M), lambda i: (i, c))
    vec = _full((1, M))
    return pl.pallas_call(
        body, name="cv_fwd", grid=(T // tc,),
        in_specs=[rowm(vcol), rowm(gcol), _full((32, M)), vec, vec, vec, _full((M, M)), vec],
        out_specs=[rowm(0), rowm(0)],
        out_shape=[jax.ShapeDtypeStruct((T, M), BF16), jax.ShapeDtypeStruct((T, M), F32)],
        scratch_shapes=[pltpu.VMEM((tc + H, M), F32), pltpu.VMEM((8, tc + H, LANES), F32)], compiler_params=ARB,
    )(proj, proj, wdw, bdw.reshape(1, M), lg.reshape(1, M), lb.reshape(1, M), wpw, bpw.reshape(1, M))


def _cv_bwd(dmix, dcol, proj, vcol, gcol, h1, wdw, lg, lb, wpw):
    T = proj.shape[0]
    M = wpw.shape[0]
    tc = _pick(T, 256, 8)
    nc = T // tc
    rb = _pick(tc, 32, 16)
    H, K = CONV_HALO, 31
    hb = tc // H

    def body(do_ref, v_ref, g_ref, vh_ref, gh_ref, h1_ref, wdw_ref, lg_ref, lb_ref, wpw_ref,
             dv_ref, dg_ref, dwpw_ref, dwdw_ref, vec_ref, hbuf_ref, dbuf_ref, hsh_ref, dsh_ref):
        i = pl.program_id(0)

        @pl.when(i == 0)
        def _():
            for r in (dwpw_ref, dwdw_ref, vec_ref):
                r[...] = jnp.zeros_like(r)
            dbuf_ref[tc:tc + H, :] = jnp.zeros((H, M), F32)

        halo = vh_ref[...] * _sigmoid(gh_ref[...])
        hbuf_ref[0:H, :] = jnp.where(i == nc - 1, 0.0, halo)
        hbuf_ref[H:H + tc, :] = v_ref[...] * _sigmoid(g_ref[...])
        do = do_ref[...]
        xhat, rstd, h2, sg = _ln_silu(h1_ref[...], lg_ref[...], lb_ref[...])
        dwpw_ref[...] += _dot_tn(h2 * sg, do)
        vec_ref[0] += _fold8(do)
        dh2 = _dot_nt(do, wpw_ref[...]) * (sg * (1.0 + h2 * (1.0 - sg)))
        vec_ref[1] += _fold8(dh2 * xhat)
        vec_ref[2] += _fold8(dh2)
        dxh = dh2 * lg_ref[...]
        dh1 = rstd * (dxh - jnp.mean(dxh, axis=-1, keepdims=True) - xhat * jnp.mean(dxh * xhat, axis=-1, keepdims=True))
        vec_ref[3] += _fold8(dh1)
        dbuf_ref[0:tc, :] = dh1

        def strip(s, _):
            cs = pl.ds(pl.multiple_of(s * LANES, LANES), LANES)
            _shifted_copies(dbuf_ref, dsh_ref, cs, tc + H - 8)
            _shifted_copies(hbuf_ref, hsh_ref, cs, tc + H - 8)
            for r0 in range(0, tc, rb):
                d1 = dbuf_ref[pl.ds(r0, rb), cs]
                dh0 = jnp.zeros((rb, LANES), F32)
                for k in range(K):
                    dh0 = dh0 + wdw_ref[k:k + 1, cs] * _tap(dbuf_ref, dsh_ref, cs, K - 1 - k, r0, rb)
                    dwdw_ref[k, :, cs] += _fold8(d1 * _tap(hbuf_ref, hsh_ref, cs, H - (K - 1) + k, r0, rb))
                sg0 = _sigmoid(g_ref[pl.ds(r0, rb), cs])
                dv_ref[pl.ds(r0, rb), cs] = (dh0 * sg0).astype(BF16)
                dg_ref[pl.ds(r0, rb), cs] = (dh0 * v_ref[pl.ds(r0, rb), cs] * sg0 * (1.0 - sg0)).astype(BF16)
            return 0

        lax.fori_loop(0, M // LANES, strip, 0)
        dbuf_ref[tc:tc + H, :] = dbuf_ref[0:H, :]

    def rowm(c):
        return pl.BlockSpec((tc, M), lambda i: (nc - 1 - i, c))

    def halo(c):
        return pl.BlockSpec((H, M), lambda i: (jnp.maximum((nc - 1 - i) * hb - 1, 0), c))
    vec = _full((1, M))
    return pl.pallas_call(
        body, name="cv_bwd", grid=(nc,),
        in_specs=[rowm(dcol), rowm(vcol), rowm(gcol), halo(vcol), halo(gcol), rowm(0), _full((32, M)), vec, vec, _full((M, M))],
        out_specs=[rowm(0), rowm(0), _full((M, M)), _full((32, 8, M)), _full((5, 8, M))],
        out_shape=[jax.ShapeDtypeStruct((T, M), BF16), jax.ShapeDtypeStruct((T, M), BF16), jax.ShapeDtypeStruct((M, M), F32),
                   jax.ShapeDtypeStruct((32, 8, M), F32), jax.ShapeDtypeStruct((5, 8, M), F32)],
        scratch_shapes=[pltpu.VMEM((tc + H, M), F32), pltpu.VMEM((tc + H, M), F32), pltpu.VMEM((8, tc + H, LANES), F32),
                        pltpu.VMEM((8, tc + H, LANES), F32)], compiler_params=ARB,
    )(dmix, proj, proj, proj, proj, h1, wdw, lg.reshape(1, M), lb.reshape(1, M), wpw)


def _lru_gates(xc, r, ig, sp8):
    la = -r * sp8
    a = jnp.exp(la)
    mult = jnp.sqrt(_neg_expm1(2.0 * la))
    return a, mult, mult * (ig * xc)


def _lru_fwd(proj, xcol, gcol, wcv, bcv, wr, br, wi, bi, sp8):
    T = proj.shape[0]
    M = wr.shape[0]
    tc = _pick(T, 256, 8)
    H, K = SMALL_HALO, 4

    def body(x_ref, g_ref, wcv_ref, bcv_ref, wr_ref, br_ref, wi_ref, bi_ref, sp_ref,
             y_ref, xc_ref, r_ref, i_ref, hp_ref, buf_ref, a_ref, car_ref):
        @pl.when(pl.program_id(0) == 0)
        def _():
            buf_ref[0:H, :] = jnp.zeros((H, M), F32)
            car_ref[...] = jnp.zeros_like(car_ref)

        buf_ref[H:H + tc, :] = x_ref[...]
        xc = jnp.zeros((tc, M), F32) + bcv_ref[...]
        for k in range(K):
            xc = xc + wcv_ref[k:k + 1, :] * buf_ref[pl.ds(H - (K - 1) + k, tc), :]
        buf_ref[0:H, :] = buf_ref[tc:tc + H, :]
        r = _sigmoid(_dot(xc, wr_ref[...]) + br_ref[...])
        ig = _sigmoid(_dot(xc, wi_ref[...]) + bi_ref[...])
        xc_ref[...] = xc
        r_ref[...] = r
        i_ref[...] = ig
        a, _, bt = _lru_gates(xc, r, ig, sp_ref[...])
        a_ref[...] = a
        hp_ref[...] = bt
        row = lax.broadcasted_iota(jnp.int32, (8, M), 0)

        def blk(ib, _):
            r0 = pl.multiple_of(ib * 8, 8)
            av, bv = a_ref[pl.ds(r0, 8), :], hp_ref[pl.ds(r0, 8), :]
            for d in (1, 2, 4):
                m = row >= d
                bv = jnp.where(m, bv + av * pltpu.roll(bv, d, 0), bv)
                av = jnp.where(m, av * pltpu.roll(av, d, 0), av)
            cr = car_ref[7:8, :]
            h = bv + av * cr
            a_ref[pl.ds(r0, 8), :] = h
            hp_ref[pl.ds(r0, 8), :] = jnp.where(row == 0, cr, pltpu.roll(h, 1, 0))
            car_ref[...] = h
            return 0

        lax.fori_loop(0, tc // 8, blk, 0)
        y_ref[...] = (a_ref[...] * _gelu(g_ref[...])).astype(BF16)

    def rowm(c):
        return pl.BlockSpec((tc, M), lambda i: (i, c))
    vec = _full((1, M))
    return pl.pallas_call(
        body, name="lru_fwd", grid=(T // tc,),
        in_specs=[rowm(xcol), rowm(gcol), _full((8, M)), vec, _full((M, M)), vec, _full((M, M)), vec, vec],
        out_specs=[rowm(0)] * 5,
        out_shape=[jax.ShapeDtypeStruct((T, M), BF16)] + [jax.ShapeDtypeStruct((T, M), F32)] * 4,
        scratch_shapes=[pltpu.VMEM((tc + H, M), F32), pltpu.VMEM((tc, M), F32), pltpu.VMEM((8, M), F32)], compiler_params=ARB,
    )(proj, proj, wcv, bcv.reshape(1, M), wr, br.reshape(1, M), wi, bi.reshape(1, M), sp8.reshape(1, M))


def _lru_bwd(dmix, dcol, proj, xcol, gcol, xc, r, ig, hprev, wcv, wr, wi, sp8):
    T = proj.shape[0]
    M = wr.shape[0]
    tc = _pick(T, 256, 8)
    nc = T // tc
    H, K = SMALL_HALO, 4
    hb = tc // H

    def body(do_ref, x_ref, g_ref, xh_ref, xc_ref, r_ref, i_ref, hp_ref, wcv_ref, wr_ref, wi_ref, sp_ref,
             dx_ref, dg_ref, dwr_ref, dwi_ref, dwcv_ref, vec_ref, xbuf_ref, dbuf_ref, a_ref, gs_ref, car_ref):
        i = pl.program_id(0)

        @pl.when(i == 0)
        def _():
            for rf in (dwr_ref, dwi_ref, dwcv_ref, vec_ref, car_ref):
                rf[...] = jnp.zeros_like(rf)
            dbuf_ref[tc:tc + H, :] = jnp.zeros((H, M), F32)

        xcv, rv, iv, hpv, gv, do = xc_ref[...], r_ref[...], i_ref[...], hp_ref[...], g_ref[...], do_ref[...]
        sp = sp_ref[...]
        a, mult, bt = _lru_gates(xcv, rv, iv, sp)
        h = a * hpv + bt
        ge = _gelu(gv)
        dg_ref[...] = (do * h * _gelu_grad(gv)).astype(BF16)
        a_ref[...] = a
        gs_ref[...] = do * ge
        row = lax.broadcasted_iota(jnp.int32, (8, M), 0)

        def blk(n, _):
            r0 = pl.multiple_of((tc // 8 - 1 - n) * 8, 8)
            ablk = a_ref[pl.ds(r0, 8), :]
            av = jnp.where(row == 7, car_ref[0, 0:1, :], pltpu.roll(ablk, 7, 0))
            bv = gs_ref[pl.ds(r0, 8), :]
            for d in (1, 2, 4):
                m = row <= 7 - d
                bv = jnp.where(m, bv + av * pltpu.roll(bv, 8 - d, 0), bv)
                av = jnp.where(m, av * pltpu.roll(av, 8 - d, 0), av)
            gsv = bv + av * car_ref[1, 0:1, :]
            gs_ref[pl.ds(r0, 8), :] = gsv
            car_ref[0] = ablk
            car_ref[1] = gsv
            return 0

        lax.fori_loop(0, tc // 8, blk, 0)
        gs = gs_ref[...]
        dixc = gs * mult
        dla = gs * hpv * a - gs * (iv * xcv) * (a * a) / mult
        vec_ref[2] += _fold8(-dla * rv)
        dpr = (-dla * sp) * rv * (1.0 - rv)
        dpi = (dixc * xcv) * iv * (1.0 - iv)
        vec_ref[0] += _fold8(dpr)
        vec_ref[1] += _fold8(dpi)
        dwr_ref[...] += _dot_tn(xcv, dpr)
        dwi_ref[...] += _dot_tn(xcv, dpi)
        dxc = dixc * iv + _dot_nt(dpr, wr_ref[...]) + _dot_nt(dpi, wi_ref[...])
        vec_ref[3] += _fold8(dxc)
        xbuf_ref[0:H, :] = jnp.where(i == nc - 1, 0.0, xh_ref[...])
        xbuf_ref[H:H + tc, :] = x_ref[...]
        dbuf_ref[0:tc, :] = dxc
        dx = jnp.zeros((tc, M), F32)
        for k in range(K):
            dx = dx + wcv_ref[k:k + 1, :] * dbuf_ref[pl.ds(K - 1 - k, tc), :]
            dwcv_ref[k] += _fold8(dxc * xbuf_ref[pl.ds(H - (K - 1) + k, tc), :])
        dx_ref[...] = dx.astype(BF16)
        dbuf_ref[tc:tc + H, :] = dbuf_ref[0:H, :]

    def rowm(c):
        return pl.BlockSpec((tc, M), lambda i: (nc - 1 - i, c))
    halo = pl.BlockSpec((H, M), lambda i: (jnp.maximum((nc - 1 - i) * hb - 1, 0), xcol))
    vec = _full((1, M))
    return pl.pallas_call(
        body, name="lru_bwd", grid=(nc,),
        in_specs=[rowm(dcol), rowm(xcol), rowm(gcol), halo, rowm(0), rowm(0), rowm(0), rowm(0), _full((8, M)),
                  _full((M, M)), _full((M, M)), vec],
        out_specs=[rowm(0), rowm(0), _full((M, M)), _full((M, M)), _full((8, 8, M)), _full((4, 8, M))],
        out_shape=[jax.ShapeDtypeStruct((T, M), BF16), jax.ShapeDtypeStruct((T, M), BF16), jax.ShapeDtypeStruct((M, M), F32),
                   jax.ShapeDtypeStruct((M, M), F32), jax.ShapeDtypeStruct((8, 8, M), F32), jax.ShapeDtypeStruct((4, 8, M), F32)],
        scratch_shapes=[pltpu.VMEM((tc + H, M), F32), pltpu.VMEM((tc + H, M), F32), pltpu.VMEM((tc, M), F32),
                        pltpu.VMEM((tc, M), F32), pltpu.VMEM((2, 8, M), F32)],
        compiler_params=ARB,
    )(dmix, proj, proj, proj, xc, r, ig, hprev, wcv, wr, wi, sp8.reshape(1, M))


def _pool_diffs(buf_ref, x, t0, tc, gw):
    H = POOL_HALO
    pos = (lax.broadcasted_iota(jnp.int32, (tc, gw), 0) + t0 + 1).astype(F32)
    out, inv = [], []
    for gi, win in enumerate(POOL_WINDOWS):
        sl = slice(gi * gw, (gi + 1) * gw)
        s = x[:, sl]
        for j in range(1, win):
            s = s + buf_ref[pl.ds(H - j, tc), sl]
        ic = 1.0 / jnp.minimum(pos, float(win))
        out.append(s * ic - x[:, sl])
        inv.append(ic)
    return out, inv


def _pool_fwd(proj, col, pw, scale):
    T = proj.shape[0]
    ng, gw, _ = pw.shape
    M = ng * gw
    tc = _pick(T, 256, 8)
    H = POOL_HALO

    def body(x_ref, pw_ref, sc_ref, y_ref, buf_ref):
        @pl.when(pl.program_id(0) == 0)
        def _():
            buf_ref[0:H, :] = jnp.zeros((H, M), F32)

        x = x_ref[...]
        buf_ref[H:H + tc, :] = x
        diffs, _ = _pool_diffs(buf_ref, x, pl.program_id(0) * tc, tc, gw)
        for gi in range(ng):
            sl = slice(gi * gw, (gi + 1) * gw)
            y_ref[:, sl] = (_dot(diffs[gi], pw_ref[gi]) * sc_ref[:, sl]).astype(BF16)
        buf_ref[0:H, :] = buf_ref[tc:tc + H, :]

    return pl.pallas_call(
        body, name="pool_fwd", grid=(T // tc,),
        in_specs=[pl.BlockSpec((tc, M), lambda i: (i, col)), _full(pw.shape), _full((1, M))],
        out_specs=pl.BlockSpec((tc, M), lambda i: (i, 0)), out_shape=jax.ShapeDtypeStruct((T, M), BF16),
        scratch_shapes=[pltpu.VMEM((tc + H, M), F32)], compiler_params=ARB,
    )(proj, pw, scale.reshape(1, M))


def _pool_bwd(dmix, dcol, proj, col, pw, scale):
    T = proj.shape[0]
    ng, gw, _ = pw.shape
    M = ng * gw
    tc = _pick(T, 256, 8)
    nc = T // tc
    H = POOL_HALO
    hb = tc // H

    def body(do_ref, x_ref, xh_ref, pw_ref, sc_ref, dx_ref, dpw_ref, dsc_ref, buf_ref, ebuf_ref):
        i = pl.program_id(0)

        @pl.when(i == 0)
        def _():
            dpw_ref[...] = jnp.zeros_like(dpw_ref)
            dsc_ref[...] = jnp.zeros_like(dsc_ref)
            ebuf_ref[tc:tc + H, :] = jnp.zeros((H, M), F32)

        x, do = x_ref[...], do_ref[...]
        buf_ref[0:H, :] = jnp.where(i == nc - 1, 0.0, xh_ref[...])
        buf_ref[H:H + tc, :] = x
        diffs, inv = _pool_diffs(buf_ref, x, (nc - 1 - i) * tc, tc, gw)
        ddg = []
        for gi in range(ng):
            sl = slice(gi * gw, (gi + 1) * gw)
            dyy = do[:, sl] * sc_ref[:, sl]
            dsc_ref[:, sl] += _fold8(do[:, sl] * _dot(diffs[gi], pw_ref[gi]))
            dpw_ref[gi] += _dot_tn(diffs[gi], dyy)
            d = _dot_nt(dyy, pw_ref[gi])
            ddg.append(d)
            ebuf_ref[0:tc, sl] = d * inv[gi]
        for gi, win in enumerate(POOL_WINDOWS):
            sl = slice(gi * gw, (gi + 1) * gw)
            s = -ddg[gi]
            for j in range(win):
                s = s + ebuf_ref[pl.ds(j, tc), sl]
            dx_ref[:, sl] = s.astype(BF16)
        ebuf_ref[tc:tc + H, :] = ebuf_ref[0:H, :]

    return pl.pallas_call(
        body, name="pool_bwd", grid=(nc,),
        in_specs=[pl.BlockSpec((tc, M), lambda i: (nc - 1 - i, dcol)), pl.BlockSpec((tc, M), lambda i: (nc - 1 - i, col)),
                  pl.BlockSpec((H, M), lambda i: (jnp.maximum((nc - 1 - i) * hb - 1, 0), col)), _full(pw.shape), _full((1, M))],
        out_specs=[pl.BlockSpec((tc, M), lambda i: (nc - 1 - i, 0)), _full(pw.shape), _full((8, M))],
        out_shape=[jax.ShapeDtypeStruct((T, M), BF16), jax.ShapeDtypeStruct(pw.shape, F32), jax.ShapeDtypeStruct((8, M), F32)],
        scratch_shapes=[pltpu.VMEM((tc + H, M), F32), pltpu.VMEM((tc + H, M), F32)], compiler_params=ARB,
    )(dmix, proj, proj, pw, scale.reshape(1, M))


def _ffn_mid_fwd(up, wdw, bdw):
    T, F2 = up.shape
    F = F2 // 2
    tc = _pick(T, 256, 8)
    rb = _pick(tc, 64, 16)
    H, K = FFN_HALO, 3

    def body(gt_ref, val_ref, w_ref, b_ref, act_ref, buf_ref):
        @pl.when(pl.program_id(0) == 0)
        def _():
            buf_ref[0:H, :] = jnp.zeros((H, F), F32)

        buf_ref[H:H + tc, :] = gt_ref[...].astype(F32)

        def strip(s, _):
            cs = pl.ds(pl.multiple_of(s * LANES, LANES), LANES)
            taps = [w_ref[k:k + 1, cs] for k in range(K)]
            for r0 in range(0, tc, rb):
                gc = b_ref[:, cs] + taps[0] * buf_ref[pl.ds(H - (K - 1) + r0, rb), cs]
                for k in range(1, K):
                    gc = gc + taps[k] * buf_ref[pl.ds(H - (K - 1) + k + r0, rb), cs]
                act_ref[pl.ds(r0, rb), cs] = (_gelu(gc) * val_ref[pl.ds(r0, rb), cs].astype(F32)).astype(BF16)
            return 0

        lax.fori_loop(0, F // LANES, strip, 0)
        buf_ref[0:H, :] = buf_ref[tc:tc + H, :]

    return pl.pallas_call(
        body, name="ffn_mid_fwd", grid=(T // tc,),
        in_specs=[pl.BlockSpec((tc, F), lambda i: (i, 0)), pl.BlockSpec((tc, F), lambda i: (i, 1)), _full((8, F)), _full((1, F))],
        out_specs=pl.BlockSpec((tc, F), lambda i: (i, 0)), out_shape=jax.ShapeDtypeStruct((T, F), BF16),
        scratch_shapes=[pltpu.VMEM((tc + H, F), F32)], compiler_params=ARB,
    )(up, up, wdw, bdw.reshape(1, F))


def _ffn_mid_bwd(dact, up, wdw, bdw):
    T, F2 = up.shape
    F = F2 // 2
    tc = _pick(T, 128, 8)
    rb = _pick(tc, 64, 16)
    nc = T // tc
    H, K = FFN_HALO, 3
    hb = tc // H

    def body(da_ref, gt_ref, val_ref, gh_ref, w_ref, b_ref, dup_ref, dw_ref, db_ref, gbuf_ref, dbuf_ref):
        i = pl.program_id(0)

        @pl.when(i == 0)
        def _():
            dw_ref[...] = jnp.zeros_like(dw_ref)
            db_ref[...] = jnp.zeros_like(db_ref)
            dbuf_ref[tc:tc + H, :] = jnp.zeros((H, F), F32)

        gbuf_ref[0:H, :] = jnp.where(i == nc - 1, 0.0, gh_ref[...].astype(F32))
        gbuf_ref[H:H + tc, :] = gt_ref[...].astype(F32)

        def strip(s, _):
            cs = pl.ds(pl.multiple_of(s * LANES, LANES), LANES)
            cs_val = pl.ds(pl.multiple_of(F + s * LANES, LANES), LANES)
            taps = [w_ref[k:k + 1, cs] for k in range(K)]
            for r0 in range(0, tc, rb):
                shifted = [gbuf_ref[pl.ds(H - (K - 1) + k + r0, rb), cs] for k in range(K)]
                gc = b_ref[:, cs] + taps[0] * shifted[0]
                for k in range(1, K):
                    gc = gc + taps[k] * shifted[k]
                ge, gg = _gelu_and_grad(gc)
                da = da_ref[pl.ds(r0, rb), cs].astype(F32)
                dup_ref[pl.ds(r0, rb), cs_val] = (da * ge).astype(BF16)
                dgc = da * val_ref[pl.ds(r0, rb), cs].astype(F32) * gg
                dbuf_ref[pl.ds(r0, rb), cs] = dgc
                db_ref[:, cs] += _fold8(dgc)
                for k in range(K):
                    dw_ref[k, :, cs] += _fold8(dgc * shifted[k])
            for r0 in range(0, tc, rb):
                dgt = taps[0] * dbuf_ref[pl.ds(K - 1 + r0, rb), cs]
                for k in range(1, K):
                    dgt = dgt + taps[k] * dbuf_ref[pl.ds(K - 1 - k + r0, rb), cs]
                dup_ref[pl.ds(r0, rb), cs] = dgt.astype(BF16)
            return 0

        lax.fori_loop(0, F // LANES, strip, 0)
        dbuf_ref[tc:tc + H, :] = dbuf_ref[0:H, :]

    return pl.pallas_call(
        body, name="ffn_mid_bwd", grid=(nc,),
        in_specs=[pl.BlockSpec((tc, F), lambda i: (nc - 1 - i, 0)), pl.BlockSpec((tc, F), lambda i: (nc - 1 - i, 0)),
                  pl.BlockSpec((tc, F), lambda i: (nc - 1 - i, 1)),
                  pl.BlockSpec((H, F), lambda i: (jnp.maximum((nc - 1 - i) * hb - 1, 0), 0)), _full((8, F)), _full((1, F))],
        out_specs=[pl.BlockSpec((tc, F2), lambda i: (nc - 1 - i, 0)), _full((8, 8, F)), _full((8, F))],
        out_shape=[jax.ShapeDtypeStruct((T, F2), BF16), jax.ShapeDtypeStruct((8, 8, F), F32), jax.ShapeDtypeStruct((8, F), F32)],
        scratch_shapes=[pltpu.VMEM((tc + H, F), F32), pltpu.VMEM((tc + H, F), F32)], compiler_params=ARB,
    )(dact, up, up, up, wdw, bdw.reshape(1, F))


def _my_pos():
    return lax.axis_index("x"), lax.axis_index("y"), lax.axis_index("c")


def _flip(pos, k):
    x, y, c = pos
    return ((1 - x) if k & 4 else x, (1 - y) if k & 2 else y, (1 - c) if k & 1 else c)


def _rank(pos):
    return 4 * pos[0] + 2 * pos[1] + pos[2]


def _exchange(xs, name, gather):
    n = len(xs)
    shapes = [((N_DEV,) + tuple(x.shape)) if gather else tuple(x.shape) for x in xs]

    def body(*refs):
        ins, outs = refs[:n], refs[n:2 * n]
        send_sems, recv_sems, local_sems = refs[2 * n:]
        me = _my_pos()
        mine = _rank(me)
        local, sends = [], []
        for a in range(n):
            loc = pltpu.make_async_copy(ins[a] if gather else ins[a].at[mine], outs[a].at[mine], local_sems.at[a])
            loc.start()
            local.append(loc)
            for k in range(1, N_DEV):
                peer = _flip(me, k)
                cp = pltpu.make_async_remote_copy(
                    src_ref=ins[a] if gather else ins[a].at[_rank(peer)], dst_ref=outs[a].at[mine],
                    send_sem=send_sems.at[a, k - 1], recv_sem=recv_sems.at[a, k - 1], device_id=peer,
                    device_id_type=pl.DeviceIdType.MESH)
                cp.start()
                sends.append(cp)
        for a in range(n):
            for k in range(1, N_DEV):
                peer = _flip(me, k)
                pltpu.make_async_remote_copy(
                    src_ref=ins[a] if gather else ins[a].at[_rank(peer)], dst_ref=outs[a].at[_rank(peer)],
                    send_sem=send_sems.at[a, k - 1], recv_sem=recv_sems.at[a, k - 1], device_id=peer,
                    device_id_type=pl.DeviceIdType.MESH).wait_recv()
        for cp in sends:
            cp.wait_send()
        for loc in local:
            loc.wait()

    any_spec = pl.BlockSpec(memory_space=pl.ANY)
    return pl.pallas_call(
        body, name=name, in_specs=[any_spec] * n, out_specs=[any_spec] * n,
        out_shape=[jax.ShapeDtypeStruct(s, x.dtype) for s, x in zip(shapes, xs)],
        scratch_shapes=[pltpu.SemaphoreType.DMA((n, N_DEV - 1)), pltpu.SemaphoreType.DMA((n, N_DEV - 1)),
                        pltpu.SemaphoreType.DMA((n,))],
    )(*xs)


_HBM = pl.BlockSpec(memory_space=pltpu.HBM)
_SEM = pl.BlockSpec(memory_space=pltpu.SEMAPHORE)
_EFFECT = pltpu.SideEffectType.DATAFLOW_SIDE_EFFECTING


def _plan_direct(gather):
    def plan(bufs, ssem, rsem, me, outgoing):
        n = len(bufs) // 2
        out = []
        for a in range(n):
            for k in range(N_DEV):
                peer = _flip(me, k)
                out.append(pltpu.make_async_remote_copy(
                    src_ref=bufs[a] if gather else bufs[a].at[_rank(peer)],
                    dst_ref=bufs[n + a].at[_rank(me) if outgoing else _rank(peer)],
                    send_sem=ssem.at[a * N_DEV + k], recv_sem=rsem.at[a * N_DEV + k], device_id=peer,
                    device_id_type=pl.DeviceIdType.MESH))
        return out
    return plan, N_DEV


_SAME_CORE = (2, 4, 6)


def _plan_own_block(bufs, ssem, rsem, me, outgoing):
    n = len(bufs) // 2
    out = []
    for a in range(n):
        for j, k in enumerate((0, 1) + _SAME_CORE):
            peer = _flip(me, k)
            out.append(pltpu.make_async_remote_copy(
                src_ref=bufs[a], dst_ref=bufs[n + a].at[_rank(me) if outgoing else _rank(peer)],
                send_sem=ssem.at[a * 5 + j], recv_sem=rsem.at[a * 5 + j], device_id=peer, device_id_type=pl.DeviceIdType.MESH))
    return out


def _plan_pass_on(bufs, ssem, rsem, me, outgoing):
    sibling = _flip(me, 1)
    out = []
    for a in range(len(bufs)):
        for j, k in enumerate(_SAME_CORE):
            slot = _rank(_flip(me, k) if outgoing else _flip(sibling, k))
            out.append(pltpu.make_async_remote_copy(
                src_ref=bufs[a].at[slot], dst_ref=bufs[a].at[slot], send_sem=ssem.at[a * 3 + j], recv_sem=rsem.at[a * 3 + j],
                device_id=sibling, device_id_type=pl.DeviceIdType.MESH))
    return out


def _split_start(bufs, plan, n_sems, name, after=None):
    nb = len(bufs)
    extra = [] if after is None else [after]

    def body(*refs):
        ssem, rsem = refs[nb + len(extra)], refs[nb + len(extra) + 1]
        for cp in plan(refs[:nb], ssem, rsem, _my_pos(), True):
            cp.start()
        token = refs[2 * nb + len(extra) + 2]
        token[...] = jnp.zeros_like(token)

    res = pl.pallas_call(
        body, name=name,
        out_shape=(pltpu.SemaphoreType.DMA((n_sems,)), pltpu.SemaphoreType.DMA((n_sems,)),
                   *[pltpu.HBM(t.shape, t.dtype) for t in bufs], jax.ShapeDtypeStruct((8, 128), F32)),
        in_specs=[_HBM] * nb + [pl.BlockSpec(memory_space=pl.ANY)] * len(extra),
        out_specs=(_SEM, _SEM, *([_HBM] * nb), pl.BlockSpec(memory_space=pltpu.VMEM)),
        input_output_aliases={i: 2 + i for i in range(nb)},
        compiler_params=pltpu.CompilerParams(has_side_effects=_EFFECT),
    )(*[pltpu.with_memory_space_constraint(t, pltpu.HBM) for t in bufs], *extra)
    return (res[0], res[1], list(res[2:2 + nb])), res[2 + nb]


def _split_wait(handle, plan, after, name):
    ssem, rsem, bufs = handle
    nb = len(bufs)

    def body(*refs):
        for cp in plan(refs[:nb], refs[nb], refs[nb + 1], _my_pos(), False):
            cp.wait_send()
            cp.wait_recv()

    res = pl.pallas_call(
        body, name=name, out_shape=tuple(pltpu.HBM(t.shape, t.dtype) for t in bufs),
        in_specs=[_HBM] * nb + [_SEM, _SEM, pl.BlockSpec(memory_space=pl.ANY)], out_specs=tuple([_HBM] * nb),
        input_output_aliases={i: i for i in range(nb)},
        compiler_params=pltpu.CompilerParams(has_side_effects=_EFFECT),
    )(*bufs, ssem, rsem, after)
    return list(res)


def _landing_zones(xs, gather):
    return [lax.empty(((N_DEV,) + tuple(x.shape)) if gather else tuple(x.shape), x.dtype) for x in xs]


def _gather_start(xs, name, after=None, two_level=True):
    bufs = list(xs) + _landing_zones(xs, True)
    if two_level:
        handle, token = _split_start(bufs, _plan_own_block, 5 * len(xs), name + "_start", after)
    else:
        plan, per = _plan_direct(True)
        handle, token = _split_start(bufs, plan, per * len(xs), name + "_start", after)
    return (handle, two_level, name), token


def _gather_wait(state, after):
    handle, two_level, name = state
    n = len(handle[2]) // 2
    if not two_level:
        return _split_wait(handle, _plan_direct(True)[0], after, name + "_wait")[n:]
    lands = _split_wait(handle, _plan_own_block, after, name + "_wait")[n:]
    handle, _ = _split_start(lands, _plan_pass_on, 3 * n, name + "_pass")
    return _split_wait(handle, _plan_pass_on, after, name + "_passed")


def _scatter_start(xs, name):
    plan, per = _plan_direct(False)
    handle, token = _split_start(list(xs) + _landing_zones(xs, False), plan, per * len(xs), name + "_start")
    return (handle, name), token


def _scatter_wait(state, after):
    handle, name = state
    return _split_wait(handle, _plan_direct(False)[0], after, name + "_wait")[len(handle[2]) // 2:]


def _adamw_math(w, g, m, v):
    m = ADAM_B1 * m + (1.0 - ADAM_B1) * g
    v = ADAM_B2 * v + (1.0 - ADAM_B2) * (g * g)
    m_hat = m / (1.0 - ADAM_B1 ** ADAM_STEP)
    v_hat = v / (1.0 - ADAM_B2 ** ADAM_STEP)
    return -ADAM_LR * (m_hat / (jnp.sqrt(v_hat) + ADAM_EPS) + ADAM_WD * w), m, v


def _sum8(parts, *, name):
    _, R, C = parts.shape
    tr = _pick(R, 256, 16)

    def body(p_ref, o_ref):
        acc = p_ref[0].astype(F32)
        for r in range(1, N_DEV):
            acc = acc + p_ref[r].astype(F32)
        o_ref[...] = acc

    return pl.pallas_call(
        body, name=name, grid=(R // tr,), in_specs=[pl.BlockSpec((N_DEV, tr, C), lambda i: (0, i, 0))],
        out_specs=pl.BlockSpec((tr, C), lambda i: (i, 0)), out_shape=jax.ShapeDtypeStruct((R, C), F32),
        compiler_params=pltpu.CompilerParams(dimension_semantics=("parallel",)),
    )(parts)


def _adamw_layer(g, w, m, v, layer, prev, *, name):
    depth, R, C = w.shape
    tr = _pick(R, 64, 16)
    summed = g.ndim == 3
    if prev is None:
        prev = [lax.empty((depth, R, C), F32) for _ in range(4)]

    def body(g_ref, w_ref, m_ref, v_ref, *rest):
        outs = rest[4:]
        if summed:
            gv = g_ref[0].astype(F32)
            for r in range(1, N_DEV):
                gv = gv + g_ref[r].astype(F32)
        else:
            gv = g_ref[...]
        d, nm, nv = _adamw_math(w_ref[...], gv, m_ref[...], v_ref[...])
        for o, val in zip(outs, (gv, d, nm, nv)):
            o[...] = val

    g_spec = pl.BlockSpec((N_DEV, tr, C), lambda i: (0, i, 0)) if summed else pl.BlockSpec((tr, C), lambda i: (i, 0))
    slab = pl.BlockSpec((None, tr, C), lambda i: (layer, i, 0))
    return pl.pallas_call(
        body, name=name, grid=(R // tr,), in_specs=[g_spec] + [slab] * 3 + [pl.BlockSpec(memory_space=pl.ANY)] * 4,
        out_specs=[slab] * 4, out_shape=[jax.ShapeDtypeStruct((depth, R, C), F32)] * 4,
        input_output_aliases={4 + q: q for q in range(4)},
        compiler_params=pltpu.CompilerParams(dimension_semantics=("parallel",)),
    )(g, w, m, v, *prev)


def _lane_view(shape):
    if len(shape) == 1:
        return (1, shape[0])
    size = math.prod(shape)
    if shape[-1] < 64 and size % LANES == 0:
        return (size // LANES, LANES)
    if len(shape) > 2 and shape[-2] % 8 == 0:
        return (size // shape[-1], shape[-1])
    return tuple(shape)


def _adamw_many(ws, gs, ms, vs, *, name):
    n = len(ws)
    views = [_lane_view(w.shape) for w in ws]

    def body(*refs):
        for i in range(n):
            d, nm, nv = _adamw_math(refs[i][...], refs[n + i][...], refs[2 * n + i][...], refs[3 * n + i][...])
            refs[4 * n + i][...] = d
            refs[5 * n + i][...] = nm
            refs[6 * n + i][...] = nv

    outs = pl.pallas_call(
        body, name=name, out_shape=[jax.ShapeDtypeStruct(s, F32) for s in views] * 3,
    )(*[a.reshape(s) for arrs in (ws, gs, ms, vs) for a, s in zip(arrs, views)])
    return [[o.reshape(w.shape) for o, w in zip(outs[q * n:(q + 1) * n], ws)] for q in range(3)]


def _pack(arrs, dtype=F32):
    flat = jnp.concatenate([a.reshape(-1).astype(dtype) for a in arrs])
    pad = (-flat.shape[0]) % 2048
    return jnp.pad(flat, (0, pad)).reshape(-1, 128)


def _unpack(buf, shapes):
    flat, out, off = buf.reshape(-1), [], 0
    for s in shapes:
        n = math.prod(s)
        out.append(flat[off:off + n].reshape(s))
        off += n
    return out


def _s5_params(lam_re, lam_im, log_step, b_re, b_im):
    G, P, H = b_re.shape
    step = jnp.exp(log_step)[:, None]
    mag = jnp.exp(lam_re * step)
    ar, ai = mag * jnp.cos(lam_im * step), mag * jnp.sin(lam_im * step)
    den = lam_re * lam_re + lam_im * lam_im
    qr = ((ar - 1.0) * lam_re + ai * lam_im) / den
    qi = (ai * lam_re - (ar - 1.0) * lam_im) / den
    bbr = qr[..., None] * b_re - qi[..., None] * b_im
    bbi = qr[..., None] * b_im + qi[..., None] * b_re
    return ar.reshape(-1), ai.reshape(-1), _s5_blocks_in(bbr), _s5_blocks_in(bbi)


def _s5_blocks_in(b):
    G, P, H = b.shape
    eye = jnp.eye(8, dtype=b.dtype)
    return jnp.einsum("kgph,gj->kghjp", b.reshape(G // 8, 8, P, H), eye).reshape(G // 8, 8 * H, 8 * P)


def _s5_blocks_in_diag(m, G, P, H):
    return jnp.einsum("kghgp->kgph", m.reshape(G // 8, 8, H, 8, P)).reshape(G, P, H)


def _s5_blocks_out(c):
    G, H, P = c.shape
    eye = jnp.eye(8, dtype=c.dtype)
    return jnp.einsum("kghp,gj->kgpjh", c.reshape(G // 8, 8, H, P), eye).reshape(G // 8, 8 * P, 8 * H)


def _s5_blocks_out_diag(m, G, H, P):
    return jnp.einsum("kgpgh->kghp", m.reshape(G // 8, 8, P, 8, H)).reshape(G, H, P)


def _head_blocks(w):
    nh, d, e = w.shape
    return jnp.einsum("hde,hk->hdke", w, jnp.eye(nh, dtype=w.dtype)).reshape(nh * d, nh * e)


def _head_blocks_diag(m, nh, d, e):
    return jnp.einsum("hdhe->hde", m.reshape(nh, d, nh, e))


def _pad_rows(w, rows):
    return jnp.pad(w, ((0, rows - w.shape[0]), (0, 0)))


def _unshard_cols(g):
    return jnp.transpose(g, (1, 0, 2)).reshape(g.shape[1], -1)


_NAMES = ['norm_mix_g', 'w_in', 's5_lam_re', 's5_lam_im', 's5_log_step', 's5_b_re', 's5_b_im', 's5_c_re', 's5_c_im', 's5_d',
          's5_w_glu', 's5_b_glu', 'cv_w_dw', 'cv_b_dw', 'cv_ln_g', 'cv_ln_b', 'cv_w_pw', 'cv_b_pw', 'lru_w_conv', 'lru_b_conv',
          'lru_w_r', 'lru_b_r', 'lru_w_i', 'lru_b_i', 'lru_lam', 'pool_w', 'pool_scale', 'w_out', 'norm_ffn_g', 'ffn_w_up',
          'ffn_w_dw', 'ffn_b_dw', 'ffn_w_down', 'norm_final_g']
_MATRIX = ('w_in', 'w_out', 'ffn_w_up', 'ffn_w_down', 's5_w_glu', 'cv_w_pw')
_TRANSPOSED = ('w_in', 'ffn_w_up')
_COLSHARD = ('cv_w_dw', 'lru_w_conv', 'ffn_w_dw')
_GROUPS = (('w_in', 's5_w_glu', 'cv_w_pw', 'w_out'), ('ffn_w_up', 'ffn_w_down'))
_SMALL_BF16 = ('s5_b_re', 's5_b_im', 's5_c_re', 's5_c_im', 'lru_w_r', 'lru_w_i', 'pool_w')
_SCATTER_GROUPS = (('w_out', 's5_w_glu', 'cv_w_pw'), ('w_in',), ('ffn_w_up', 'ffn_w_down'))


def kernel(x, norm_mix_g, w_in, s5_lam_re, s5_lam_im, s5_log_step, s5_b_re, s5_b_im, s5_c_re, s5_c_im, s5_d, s5_w_glu, s5_b_glu, cv_w_dw, cv_b_dw, cv_ln_g, cv_ln_b, cv_w_pw, cv_b_pw, lru_w_conv, lru_b_conv, lru_w_r, lru_b_r, lru_w_i, lru_b_i, lru_lam, pool_w, pool_scale, w_out, norm_ffn_g, ffn_w_up, ffn_w_dw, ffn_b_dw, ffn_w_down, norm_final_g, loss_target, m_norm_mix_g, m_w_in, m_s5_lam_re, m_s5_lam_im, m_s5_log_step, m_s5_b_re, m_s5_b_im, m_s5_c_re, m_s5_c_im, m_s5_d, m_s5_w_glu, m_s5_b_glu, m_cv_w_dw, m_cv_b_dw, m_cv_ln_g, m_cv_ln_b, m_cv_w_pw, m_cv_b_pw, m_lru_w_conv, m_lru_b_conv, m_lru_w_r, m_lru_b_r, m_lru_w_i, m_lru_b_i, m_lru_lam, m_pool_w, m_pool_scale, m_w_out, m_norm_ffn_g, m_ffn_w_up, m_ffn_w_dw, m_ffn_b_dw, m_ffn_w_down, m_norm_final_g, v_norm_mix_g, v_w_in, v_s5_lam_re, v_s5_lam_im, v_s5_log_step, v_s5_b_re, v_s5_b_im, v_s5_c_re, v_s5_c_im, v_s5_d, v_s5_w_glu, v_s5_b_glu, v_cv_w_dw, v_cv_b_dw, v_cv_ln_g, v_cv_ln_b, v_cv_w_pw, v_cv_b_pw, v_lru_w_conv, v_lru_b_conv, v_lru_w_r, v_lru_b_r, v_lru_w_i, v_lru_b_i, v_lru_lam, v_pool_w, v_pool_scale, v_w_out, v_norm_ffn_g, v_ffn_w_up, v_ffn_w_dw, v_ffn_b_dw, v_ffn_w_down, v_norm_final_g):
    args = (x, norm_mix_g, w_in, s5_lam_re, s5_lam_im, s5_log_step, s5_b_re, s5_b_im, s5_c_re, s5_c_im, s5_d, s5_w_glu, s5_b_glu, cv_w_dw, cv_b_dw, cv_ln_g, cv_ln_b, cv_w_pw, cv_b_pw, lru_w_conv, lru_b_conv, lru_w_r, lru_b_r, lru_w_i, lru_b_i, lru_lam, pool_w, pool_scale, w_out, norm_ffn_g, ffn_w_up, ffn_w_dw, ffn_b_dw, ffn_w_down, norm_final_g, loss_target, m_norm_mix_g, m_w_in, m_s5_lam_re, m_s5_lam_im, m_s5_log_step, m_s5_b_re, m_s5_b_im, m_s5_c_re, m_s5_c_im, m_s5_d, m_s5_w_glu, m_s5_b_glu, m_cv_w_dw, m_cv_b_dw, m_cv_ln_g, m_cv_ln_b, m_cv_w_pw, m_cv_b_pw, m_lru_w_conv, m_lru_b_conv, m_lru_w_r, m_lru_b_r, m_lru_w_i, m_lru_b_i, m_lru_lam, m_pool_w, m_pool_scale, m_w_out, m_norm_ffn_g, m_ffn_w_up, m_ffn_w_dw, m_ffn_b_dw, m_ffn_w_down, m_norm_final_g, v_norm_mix_g, v_w_in, v_s5_lam_re, v_s5_lam_im, v_s5_log_step, v_s5_b_re, v_s5_b_im, v_s5_c_re, v_s5_c_im, v_s5_d, v_s5_w_glu, v_s5_b_glu, v_cv_w_dw, v_cv_b_dw, v_cv_ln_g, v_cv_ln_b, v_cv_w_pw, v_cv_b_pw, v_lru_w_conv, v_lru_b_conv, v_lru_w_r, v_lru_b_r, v_lru_w_i, v_lru_b_i, v_lru_lam, v_pool_w, v_pool_scale, v_w_out, v_norm_ffn_g, v_ffn_w_up, v_ffn_w_dw, v_ffn_b_dw, v_ffn_w_down, v_norm_final_g)
    x, target = args[0], args[35]
    W = dict(zip(_NAMES, args[1:35]))
    Mo = dict(zip(_NAMES, args[36:70]))
    Vo = dict(zip(_NAMES, args[70:104]))
    x = x[0]
    target = target[0]
    T, D = x.shape
    depth = W['w_in'].shape[0]
    F = W['ffn_w_down'].shape[1] * N_DEV
    G, P, H = W['s5_b_re'].shape[1:]
    MW = G * H
    nh, hd, _ = W['lru_w_r'].shape[1:]
    me = _rank(_my_pos())

    gather, full = {}, {}

    def start_group(l, gi, after):
        shards = [(W[n][l].T if n in _TRANSPOSED else W[n][l]).astype(BF16) for n in _GROUPS[gi]]
        if (l, gi) == (0, 0):
            shards.append(_pack([W[n][k] for k in range(depth) for n in _COLSHARD]))
        gather[l, gi], tok = _gather_start(shards, "gather_%d%d" % (l, gi), after)
        return tok

    def land_weights(l, gi, after):
        got = _gather_wait(gather[l, gi], after)
        for n, g8 in zip(_GROUPS[gi], got):
            full[n, l] = g8.reshape(-1, g8.shape[-1])
        return got

    started = start_group(0, 0, None)[0, 0]
    got = land_weights(0, 0, W['norm_mix_g'][0] + started)
    started = start_group(0, 1, got[0])[0, 0]
    taps8 = got[-1]
    tap_shapes = [W[n][l].shape for l in range(depth) for n in _COLSHARD]
    taps = [_unshard_cols(jnp.stack(t)) for t in zip(*[_unpack(taps8[r], tap_shapes) for r in range(N_DEV)])]
    for l in range(depth):
        for j, n in enumerate(_COLSHARD):
            full[n, l] = taps[l * len(_COLSHARD) + j]

    saved = []
    for l in range(depth):
        g_mix = W['norm_mix_g'][l] + (0.0 if l else started)
        if l:
            land_weights(l, 0, x)
        s5p, s5p_vjp = jax.vjp(_s5_params, W['s5_lam_re'][l], W['s5_lam_im'][l], W['s5_log_step'][l], W['s5_b_re'][l], W['s5_b_im'][l])
        ar, ai, bre, bim = s5p
        cre, cim = _s5_blocks_out(W['s5_c_re'][l]), _s5_blocks_out(W['s5_c_im'][l])
        sp8, sp8_vjp = jax.vjp(lambda lam: LRU_C * jax.nn.softplus(-lam), W['lru_lam'][l])
        wr, wi = _head_blocks(W['lru_w_r'][l]), _head_blocks(W['lru_w_i'][l])
        c = dict(bre=bre.astype(BF16), bim=bim.astype(BF16), cre=cre.astype(BF16), cim=cim.astype(BF16),
                 cst_f=_s5_scan_consts(ar, ai, False), cst_b=_s5_scan_consts(ar, ai, True), sp8=sp8, wr=wr.astype(BF16),
                 wi=wi.astype(BF16), cvw=_pad_rows(full['cv_w_dw', l], 32), lruw=_pad_rows(full['lru_w_conv', l], 8),
                 ffw=_pad_rows(full['ffn_w_dw', l], 8), pw=W['pool_w'][l].astype(BF16), s5p_vjp=s5p_vjp, sp8_vjp=sp8_vjp, x0=x)
        h = _rmsnorm(x, g_mix, name="mix_norm")
        proj = _matmul(h, full['w_in', l], mode="nt", tm=1024, tn=512, tk=D, out_dtype=F32, name="in_proj")
        y0, sre, sim, yp, z = _s5_fwd(proj, 0, c['bre'], c['bim'], c['cre'], c['cim'], c['cst_f'], W['s5_d'][l],
                                     full['s5_w_glu', l], W['s5_b_glu'][l])
        y1, h1 = _cv_fwd(proj, 1, 2, c['cvw'], W['cv_b_dw'][l], W['cv_ln_g'][l], W['cv_ln_b'][l], full['cv_w_pw', l], W['cv_b_pw'][l])
        y2, xc, rr, ig, hprev = _lru_fwd(proj, 3, 4, c['lruw'], W['lru_b_conv'][l], c['wr'], W['lru_b_r'][l], c['wi'],
                                         W['lru_b_i'][l], sp8)
        y3 = _pool_fwd(proj, 5, c['pw'], W['pool_scale'][l])
        mixed = jnp.concatenate([y0, y1, y2, y3], axis=1)
        x1 = _matmul(mixed, full['w_out', l], mode="nn", tm=1024, tn=512, tk=2048, out_dtype=F32, name="out_proj", add=x)
        got = land_weights(l, 1, x1)
        g_ffn = W['norm_ffn_g'][l]
        if l + 1 < depth:
            g_ffn = g_ffn + start_group(l + 1, 0, got[0])[0, 0] + start_group(l + 1, 1, got[0])[0, 0]
        h2 = _rmsnorm(x1, g_ffn, name="ffn_norm")
        up = _matmul(h2, full['ffn_w_up', l], mode="nt", tm=2048, tn=256, tk=D, out_dtype=BF16, name="ffn_up")
        act = _ffn_mid_fwd(up, c['ffw'], W['ffn_b_dw'][l])
        x = _matmul(act, full['ffn_w_down', l], mode="nn", tm=1024, tn=512, tk=F, out_dtype=F32, name="ffn_down", add=x1)
        c.update(proj=proj, h=h, sre=sre, sim=sim, yp=yp, z=z, h1=h1, xc=xc, rr=rr, ig=ig, hprev=hprev, mixed=mixed, x1=x1,
                 up=up, h2=h2, act=act)
        saved.append(c)

    dx, dxb, dgf, se = _loss_head(x, W['norm_final_g'], target)
    loss = lax.psum(0.5 / D * jnp.sum(se), AXES)
    gsmall = {('norm_final_g', None): dgf.sum(0)}
    gmat, scatter = {}, {}

    def parts_of(m):
        return m.reshape(N_DEV, m.shape[0] // N_DEV, m.shape[1])

    for l in reversed(range(depth)):
        c = saved[l]
        dact = _matmul(dxb, full['ffn_w_down', l], mode="nt", tm=512, tn=F, tk=512, out_dtype=BF16, name="d_act")
        gmat['ffn_w_down', l] = _matmul(c['act'], dxb, mode="tn", tm=F, tn=512, tk=512, out_dtype=BF16, name="dw_down")
        dup, dffw, dffb = _ffn_mid_bwd(dact, c['up'], c['ffw'], W['ffn_b_dw'][l])
        gsmall['ffn_w_dw', l] = dffw.sum(1)[:3]
        gsmall['ffn_b_dw', l] = dffb.sum(0)
        dh2 = _matmul(dup, full['ffn_w_up', l], mode="nn", tm=1024, tn=512, tk=F, out_dtype=F32, name="d_h2")
        gmat['ffn_w_up', l] = _matmul(dup, c['h2'], mode="tn", tm=256, tn=D, tk=T, out_dtype=BF16, name="dw_up")
        scatter[l, 2], tok = _scatter_start([parts_of(gmat[n, l]) for n in _SCATTER_GROUPS[2]], "scatter_%d2" % l)
        dx1, dx1b, dg2 = _rms_bwd(dx, dh2, c['x1'], W['norm_ffn_g'][l] + tok[0, 0], name="ffn_norm_bwd")
        gsmall['norm_ffn_g', l] = dg2.sum(0)
        dmix = _matmul(dx1b, full['w_out', l], mode="nt", tm=1024, tn=512, tk=2048, out_dtype=F32, name="d_mixed")
        gmat['w_out', l] = _matmul(c['mixed'], dx1b, mode="tn", tm=512, tn=D, tk=T, out_dtype=BF16, name="dw_out")
        du, dbre, dbim, dcre, dcim, da, dwglu, v3 = _s5_bwd(dmix, 0, c['proj'], 0, c['sre'], c['sim'], c['yp'], c['z'], c['bre'],
                                                            c['bim'], c['cre'], c['cim'], c['cst_b'], W['s5_d'][l], full['s5_w_glu', l])
        da = da.sum(1)
        glr, gli, gls, gbr, gbi = c['s5p_vjp']((da[0], da[1], dbre, dbim))
        gsmall['s5_lam_re', l], gsmall['s5_lam_im', l], gsmall['s5_log_step', l] = glr, gli, gls
        gsmall['s5_b_re', l], gsmall['s5_b_im', l] = gbr, gbi
        gsmall['s5_c_re', l] = _s5_blocks_out_diag(dcre, G, H, P)
        gsmall['s5_c_im', l] = _s5_blocks_out_diag(dcim, G, H, P)
        gsmall['s5_b_glu', l], gsmall['s5_d', l] = v3[0].sum(0), v3[1].sum(0)
        gmat['s5_w_glu', l] = dwglu.astype(BF16)
        dv, dg, dwpw, dcvw, v5 = _cv_bwd(dmix, 1, c['proj'], 1, 2, c['h1'], c['cvw'], W['cv_ln_g'][l], W['cv_ln_b'][l], full['cv_w_pw', l])
        gmat['cv_w_pw', l] = dwpw.astype(BF16)
        gsmall['cv_w_dw', l] = dcvw.sum(1)[:31]
        gsmall['cv_b_pw', l], gsmall['cv_ln_g', l], gsmall['cv_ln_b', l], gsmall['cv_b_dw', l] = [v5[j].sum(0) for j in range(4)]
        scatter[l, 0], tok = _scatter_start([parts_of(gmat[n, l]) for n in _SCATTER_GROUPS[0]], "scatter_%d0" % l)
        tok = tok[0, 0]
        dlx, dlg, dwr, dwi, dlw, v4 = _lru_bwd(dmix, 2, c['proj'], 3, 4, c['xc'], c['rr'], c['ig'], c['hprev'], c['lruw'], c['wr'],
                                               c['wi'], c['sp8'] + tok)
        gsmall['lru_w_r', l], gsmall['lru_w_i', l] = _head_blocks_diag(dwr, nh, hd, hd), _head_blocks_diag(dwi, nh, hd, hd)
        gsmall['lru_w_conv', l] = dlw.sum(1)[:4]
        gsmall['lru_b_r', l], gsmall['lru_b_i', l] = v4[0].sum(0), v4[1].sum(0)
        gsmall['lru_lam', l] = c['sp8_vjp'](v4[2].sum(0))[0]
        gsmall['lru_b_conv', l] = v4[3].sum(0)
        dpx, dpw, dps = _pool_bwd(dmix, 3, c['proj'], 5, c['pw'], W['pool_scale'][l] + tok)
        gsmall['pool_w', l], gsmall['pool_scale', l] = dpw, dps.sum(0)
        dproj = jnp.concatenate([du, dv, dg, dlx, dlg, dpx], axis=1)
        dh = _matmul(dproj, full['w_in', l], mode="nn", tm=1024, tn=512, tk=dproj.shape[1], out_dtype=F32, name="d_h")
        gmat['w_in', l] = _matmul(dproj, c['h'], mode="tn", tm=512, tn=D, tk=T, out_dtype=BF16, name="dw_in")
        scatter[l, 1], tok = _scatter_start([parts_of(gmat[n, l]) for n in _SCATTER_GROUPS[1]], "scatter_%d1" % l)
        dx, dxb, dg1 = _rms_bwd(dx1, dh, c['x0'], W['norm_mix_g'][l] + tok[0, 0], name="mix_norm_bwd")
        gsmall['norm_mix_g', l] = dg1.sum(0)

    small_names = [n for n in _NAMES if n not in _MATRIX]
    small_split = ([n for n in small_names if n not in _SMALL_BF16], [n for n in small_names if n in _SMALL_BF16])
    small_packs, full_shapes = [], []
    for names, dtype in zip(small_split, (F32, BF16)):
        layers = [[None] if n == 'norm_final_g' else range(depth) for n in names]
        small_packs.append(_pack([gsmall[n, l] for n, ls in zip(names, layers) for l in ls], dtype))
        full_shapes.append([gsmall[n, None].shape if n == 'norm_final_g' else (depth,) + gsmall[n, 0].shape for n in names])
    small_state, tok = _gather_start(small_packs, "small_grads")

    stored_t = [n for n in _TRANSPOSED if W[n].shape[-1] % LANES]
    wmv = {n: [jnp.swapaxes(t[n], 1, 2) if n in stored_t else t[n] for t in (W, Mo, Vo)] for n in _MATRIX}
    stacked = {n: None for n in _MATRIX}

    def update_matrices(l, after):
        for gi in (2, 0, 1):
            for n, p8 in zip(_SCATTER_GROUPS[gi], _scatter_wait(scatter[l, gi], after)):
                g = _sum8(p8, name="sum_%s_%d" % (n, l)).T if n in _TRANSPOSED and n not in stored_t else p8
                stacked[n] = _adamw_layer(g, *wmv[n], l, stacked[n], name="adamw_%s_%d" % (n, l))

    after = tok
    for l in reversed(range(depth)):
        update_matrices(l, after)
        after = sum(stacked[n][1][l, 0, 0] for n in _MATRIX).reshape(1, 1)
    after_matrices = after
    gs = {}
    for names, shapes, small8 in zip(small_split, full_shapes, _gather_wait(small_state, after_matrices)):
        gs.update(zip(names, _unpack(_sum8(small8, name="sum_small_" + str(small8.dtype)), shapes)))
    for n in _COLSHARD:
        cw = W[n].shape[-1]
        gs[n] = lax.dynamic_slice_in_dim(gs[n], me * cw, cw, axis=gs[n].ndim - 1)
    upd = _adamw_many([W[n] for n in small_names], [gs[n] for n in small_names], [Mo[n] for n in small_names],
                      [Vo[n] for n in small_names], name="adamw_small")
    out = {n: (gs[n], upd[0][j], upd[1][j], upd[2][j]) for j, n in enumerate(small_names)}
    for n in _MATRIX:
        out[n] = tuple(jnp.swapaxes(t, 1, 2) for t in stacked[n]) if n in stored_t else tuple(stacked[n])
    return (loss, dx[None]) + tuple(out[n][q] for q in range(4) for n in _NAMES)
```

```python
import math

import jax
import jax.numpy as jnp
from jax import lax
from jax.experimental import pallas as pl
from jax.experimental.pallas import tpu as pltpu

F32 = jnp.float32
BF16 = jnp.bfloat16
N_DEV = 8
AXES = ("x", "y", "c")
EPS = 1e-6
LRU_C = 8.0
POOL_WINDOWS = (2, 4, 8, 16)
CONV_HALO = 32
SMALL_HALO = 8
FFN_HALO = 16
POOL_HALO = 16
LANES = 128
ADAM_LR, ADAM_B1, ADAM_B2, ADAM_EPS, ADAM_WD, ADAM_STEP = 0.001, 0.9, 0.999, 1e-08, 0.01, 10
_GELU_K = math.sqrt(2.0 / math.pi)
ARB = pltpu.CompilerParams(dimension_semantics=("arbitrary",))


def _pick(n, pref, mult):
    best = None
    for t in range(mult, min(n, pref) + 1, mult):
        if n % t == 0:
            best = t
    return best if best is not None else n


def _sigmoid(x):
    return 1.0 / (1.0 + jnp.exp(-x))


def _gelu(x):
    return 0.5 * x * (1.0 + jnp.tanh(_GELU_K * (x + 0.044715 * x * x * x)))


def _gelu_grad(x):
    t = jnp.tanh(_GELU_K * (x + 0.044715 * x * x * x))
    return 0.5 * (1.0 + t) + 0.5 * x * (1.0 - t * t) * _GELU_K * (1.0 + 3.0 * 0.044715 * x * x)


def _gelu_and_grad(x):
    t = jnp.tanh(_GELU_K * (x + 0.044715 * x * x * x))
    half = 0.5 * (1.0 + t)
    return x * half, half + 0.5 * x * (1.0 - t * t) * _GELU_K * (1.0 + 3.0 * 0.044715 * x * x)


def _neg_expm1(x):
    series = -x * (1.0 + x * (0.5 + x * (1.0 / 6.0 + x * (1.0 / 24.0 + x * (1.0 / 120.0 + x * (1.0 / 720.0))))))
    return jnp.where(jnp.abs(x) < 0.25, series, 1.0 - jnp.exp(x))


def _fold8(x):
    return x.reshape(x.shape[0] // 8, 8, x.shape[1]).sum(axis=0)


def _dot(a, b):
    return jnp.dot(a.astype(BF16), b.astype(BF16), preferred_element_type=F32)


def _dot_nt(a, b):
    return lax.dot_general(a.astype(BF16), b.astype(BF16), (((1,), (1,)), ((), ())), preferred_element_type=F32)


def _dot_tn(a, b):
    return lax.dot_general(a.astype(BF16), b.astype(BF16), (((0,), (0,)), ((), ())), preferred_element_type=F32)


def _full(shape):
    nd = len(shape)
    return pl.BlockSpec(shape, lambda *i: (0,) * nd)


def _matmul(a, b, *, mode, tm, tn, tk, out_dtype, name, add=None):
    if mode == "nn":
        (M, K), N = a.shape, b.shape[1]
    elif mode == "nt":
        (M, K), N = a.shape, b.shape[0]
    else:
        (K, M), N = a.shape, b.shape[1]
    tm, tn, tk = _pick(M, tm, 8), _pick(N, tn, 128), _pick(K, tk, 128)
    if mode == "tn":
        tm = _pick(M, tm, 128)
    nk = K // tk
    a_spec = {"nn": pl.BlockSpec((tm, tk), lambda i, j, k: (i, k)), "nt": pl.BlockSpec((tm, tk), lambda i, j, k: (i, k)),
              "tn": pl.BlockSpec((tk, tm), lambda i, j, k: (k, i))}[mode]
    b_spec = {"nn": pl.BlockSpec((tk, tn), lambda i, j, k: (k, j)), "nt": pl.BlockSpec((tn, tk), lambda i, j, k: (j, k)),
              "tn": pl.BlockSpec((tk, tn), lambda i, j, k: (k, j))}[mode]
    o_spec = pl.BlockSpec((tm, tn), lambda i, j, k: (i, j))
    dot = {"nn": _dot, "nt": _dot_nt, "tn": _dot_tn}[mode]
    has_add = add is not None

    def body(*refs):
        if has_add:
            a_ref, b_ref, add_ref, o_ref, acc_ref = refs
        else:
            a_ref, b_ref, o_ref, acc_ref = refs
        if nk == 1:
            part = dot(a_ref[...], b_ref[...])
            o_ref[...] = (part + add_ref[...] if has_add else part).astype(out_dtype)
            return
        k = pl.program_id(2)

        @pl.when(k == 0)
        def _():
            acc_ref[...] = jnp.zeros_like(acc_ref)

        acc_ref[...] += dot(a_ref[...], b_ref[...])

        @pl.when(k == nk - 1)
        def _():
            r = acc_ref[...]
            if has_add:
                r = r + add_ref[...]
            o_ref[...] = r.astype(out_dtype)

    ins = [a, b] + ([add] if has_add else [])
    specs = [a_spec, b_spec] + ([o_spec] if has_add else [])
    return pl.pallas_call(
        body, name=name, grid=(M // tm, N // tn, nk), in_specs=specs, out_specs=o_spec,
        out_shape=jax.ShapeDtypeStruct((M, N), out_dtype),
        scratch_shapes=[pltpu.VMEM((tm, tn) if nk > 1 else (8, LANES), F32)],
        compiler_params=pltpu.CompilerParams(dimension_semantics=("parallel", "parallel", "arbitrary")),
    )(*ins)


def _rmsnorm(x, g, *, name):
    T, D = x.shape
    tc = _pick(T, 512, 16)

    def body(x_ref, g_ref, h_ref):
        def rows16(n, _):
            rows = pl.ds(pl.multiple_of(n * 16, 16), 16)
            xv = x_ref[rows, :]
            r = lax.rsqrt(jnp.mean(xv * xv, axis=-1, keepdims=True) + EPS)
            h_ref[rows, :] = (xv * r * g_ref[...]).astype(BF16)
            return 0

        lax.fori_loop(0, tc // 16, rows16, 0, unroll=4)

    row = pl.BlockSpec((tc, D), lambda i: (i, 0))
    return pl.pallas_call(
        body, name=name, grid=(T // tc,), in_specs=[row, _full((1, D))], out_specs=row,
        out_shape=jax.ShapeDtypeStruct((T, D), BF16), compiler_params=pltpu.CompilerParams(dimension_semantics=("parallel",)),
    )(x, g.reshape(1, D))


def _rms_bwd(dres, dh, x, g, *, name):
    T, D = x.shape
    tc = _pick(T, 512, 16)

    def body(dres_ref, dh_ref, x_ref, g_ref, dx_ref, dxb_ref, dg_ref):
        @pl.when(pl.program_id(0) == 0)
        def _():
            dg_ref[...] = jnp.zeros_like(dg_ref)

        def rows16(n, _):
            rows = pl.ds(pl.multiple_of(n * 16, 16), 16)
            xv, dhv = x_ref[rows, :], dh_ref[rows, :]
            r = lax.rsqrt(jnp.mean(xv * xv, axis=-1, keepdims=True) + EPS)
            dg_ref[...] += _fold8(dhv * xv * r)
            dyg = dhv * g_ref[...]
            dx = dres_ref[rows, :] + r * dyg - xv * (r * r * r) * jnp.mean(dyg * xv, axis=-1, keepdims=True)
            dx_ref[rows, :] = dx
            dxb_ref[rows, :] = dx.astype(BF16)
            return 0

        lax.fori_loop(0, tc // 16, rows16, 0, unroll=4)

    row = pl.BlockSpec((tc, D), lambda i: (i, 0))
    return pl.pallas_call(
        body, name=name, grid=(T // tc,), in_specs=[row, row, row, _full((1, D))],
        out_specs=[row, row, _full((8, D))],
        out_shape=[jax.ShapeDtypeStruct((T, D), F32), jax.ShapeDtypeStruct((T, D), BF16), jax.ShapeDtypeStruct((8, D), F32)],
        compiler_params=ARB,
    )(dres, dh, x, g.reshape(1, D))


def _loss_head(x, g, target):
    T, D = x.shape
    tc = _pick(T, 256, 8)

    def body(x_ref, g_ref, t_ref, dx_ref, dxb_ref, dg_ref, se_ref):
        @pl.when(pl.program_id(0) == 0)
        def _():
            dg_ref[...] = jnp.zeros_like(dg_ref)
            se_ref[...] = jnp.zeros_like(se_ref)

        def rows16(n, _):
            rows = pl.ds(pl.multiple_of(n * 16, 16), 16)
            xv = x_ref[rows, :]
            r = lax.rsqrt(jnp.mean(xv * xv, axis=-1, keepdims=True) + EPS)
            gv = g_ref[...]
            err = xv * r * gv - t_ref[rows, :]
            se_ref[...] += _fold8(err * err)
            dy = err * (1.0 / D)
            dg_ref[...] += _fold8(dy * xv * r)
            dyg = dy * gv
            dx = r * dyg - xv * (r * r * r) * jnp.mean(dyg * xv, axis=-1, keepdims=True)
            dx_ref[rows, :] = dx
            dxb_ref[rows, :] = dx.astype(BF16)
            return 0

        lax.fori_loop(0, tc // 16, rows16, 0, unroll=4)

    row = pl.BlockSpec((tc, D), lambda i: (i, 0))
    return pl.pallas_call(
        body, name="loss_head", grid=(T // tc,), in_specs=[row, _full((1, D)), row],
        out_specs=[row, row, _full((8, D)), _full((8, D))],
        out_shape=[jax.ShapeDtypeStruct((T, D), F32), jax.ShapeDtypeStruct((T, D), BF16),
                   jax.ShapeDtypeStruct((8, D), F32), jax.ShapeDtypeStruct((8, D), F32)],
        compiler_params=ARB,
    )(x, g.reshape(1, D), target)


def _s5_scan_consts(ar, ai, reverse):
    if reverse:
        ai = -ai
    pr, pi_ = [ar], [ai]
    for _ in range(7):
        pr, pi_ = pr + [pr[-1] * ar - pi_[-1] * ai], pi_ + [pr[-1] * ai + pi_[-1] * ar]
    rows = jnp.arange(8)[:, None]
    planes = []
    for d in (1, 2, 4):
        mask = (rows <= 7 - d) if reverse else (rows >= d)
        planes += [jnp.where(mask, pr[d - 1][None, :], 0.0), jnp.where(mask, pi_[d - 1][None, :], 0.0)]
    order = list(range(7, -1, -1)) if reverse else list(range(8))
    planes += [jnp.stack([pr[k] for k in order]), jnp.stack([pi_[k] for k in order])]
    return jnp.stack(planes).astype(F32)


def _s5_block_scan(xr_ref, xi_ref, cst_ref, car_ref, nblk, reverse, lb, extra=None):
    ns = xr_ref.shape[1]
    crow = 0 if reverse else 7

    def blk(i, _):
        ib = (nblk - 1 - i) if reverse else i
        r0 = pl.multiple_of(ib * 8, 8)
        for k in range(ns // lb):
            sl = slice(k * lb, (k + 1) * lb)
            xr, xi = xr_ref[pl.ds(r0, 8), sl], xi_ref[pl.ds(r0, 8), sl]
            for n, d in enumerate((1, 2, 4)):
                ar, ai = cst_ref[2 * n, :, sl], cst_ref[2 * n + 1, :, sl]
                sh = (8 - d) if reverse else d
                rr, ri = pltpu.roll(xr, sh, 0), pltpu.roll(xi, sh, 0)
                xr, xi = xr + ar * rr - ai * ri, xi + ar * ri + ai * rr
            pr, pi_ = cst_ref[6, :, sl], cst_ref[7, :, sl]
            cr, ci = car_ref[0, crow:crow + 1, sl], car_ref[1, crow:crow + 1, sl]
            xr, xi = xr + pr * cr - pi_ * ci, xi + pr * ci + pi_ * cr
            xr_ref[pl.ds(r0, 8), sl] = xr
            xi_ref[pl.ds(r0, 8), sl] = xi
            if extra is not None:
                extra(r0, sl, xr, xi, cr, ci)
            car_ref[0, :, sl] = xr
            car_ref[1, :, sl] = xi
        return 0

    lax.fori_loop(0, nblk, blk, 0)


def _s5_fwd(proj, col, bre, bim, cre, cim, cst, dvec, wglu, bglu):
    T = proj.shape[0]
    nsb, cw, sw = bre.shape
    M, NS = nsb * cw, nsb * sw
    tc = _pick(T, 512, 8)

    def body(u_ref, bre_ref, bim_ref, cre_ref, cim_ref, cst_ref, d_ref, wg_ref, bg_ref,
             y_ref, sre_ref, sim_ref, yp_ref, z_ref, car_ref):
        @pl.when(pl.program_id(0) == 0)
        def _():
            car_ref[...] = jnp.zeros_like(car_ref)

        u = u_ref[...]
        ub = u.astype(BF16)
        for k in range(nsb):
            sre_ref[:, k * sw:(k + 1) * sw] = _dot(ub[:, k * cw:(k + 1) * cw], bre_ref[k])
            sim_ref[:, k * sw:(k + 1) * sw] = _dot(ub[:, k * cw:(k + 1) * cw], bim_ref[k])
        _s5_block_scan(sre_ref, sim_ref, cst_ref, car_ref, tc // 8, False, sw)
        for k in range(nsb):
            yk = _dot(sre_ref[:, k * sw:(k + 1) * sw], cre_ref[k]) - _dot(sim_ref[:, k * sw:(k + 1) * sw], cim_ref[k])
            yp_ref[:, k * cw:(k + 1) * cw] = yk + d_ref[:, k * cw:(k + 1) * cw] * u[:, k * cw:(k + 1) * cw]
        gl = _gelu(yp_ref[...])
        z = _dot(gl, wg_ref[...]) + bg_ref[...]
        z_ref[...] = z
        y_ref[...] = (gl * _sigmoid(z)).astype(BF16)

    rowm = pl.BlockSpec((tc, M), lambda i: (i, 0))
    rows = pl.BlockSpec((tc, NS), lambda i: (i, 0))
    return pl.pallas_call(
        body, name="s5_fwd", grid=(T // tc,),
        in_specs=[pl.BlockSpec((tc, M), lambda i: (i, col)), _full(bre.shape), _full(bim.shape), _full(cre.shape),
                  _full(cim.shape), _full(cst.shape), _full((1, M)), _full((M, M)), _full((1, M))],
        out_specs=[rowm, rows, rows, rowm, rowm],
        out_shape=[jax.ShapeDtypeStruct((T, M), BF16), jax.ShapeDtypeStruct((T, NS), F32), jax.ShapeDtypeStruct((T, NS), F32),
                   jax.ShapeDtypeStruct((T, M), F32), jax.ShapeDtypeStruct((T, M), F32)],
        scratch_shapes=[pltpu.VMEM((2, 8, NS), F32)], compiler_params=ARB,
    )(proj, bre, bim, cre, cim, cst, dvec.reshape(1, M), wglu, bglu.reshape(1, M))


def _s5_bwd(dmix, dcol, proj, col, sre, sim, yp, z, bre, bim, cre, cim, cst, dvec, wglu):
    T = proj.shape[0]
    nsb, cw, sw = bre.shape
    M, NS = nsb * cw, nsb * sw
    tc = _pick(T, 512, 8)
    nc = T // tc

    def body(do_ref, u_ref, sre_ref, sim_ref, yp_ref, z_ref, bre_ref, bim_ref, cre_ref, cim_ref, cst_ref, d_ref, wg_ref,
             du_ref, dbre_ref, dbim_ref, dcre_ref, dcim_ref, da_ref, dwg_ref, vec_ref, gre_ref, gim_ref, car_ref):
        @pl.when(pl.program_id(0) == 0)
        def _():
            for r in (dbre_ref, dbim_ref, dcre_ref, dcim_ref, da_ref, dwg_ref, vec_ref, car_ref):
                r[...] = jnp.zeros_like(r)

        u, do, ypv = u_ref[...], do_ref[...], yp_ref[...]
        gl = _gelu(ypv)
        sg = _sigmoid(z_ref[...])
        dz = do * gl * sg * (1.0 - sg)
        dgl = do * sg + _dot_nt(dz, wg_ref[...])
        dwg_ref[...] += _dot_tn(gl, dz)
        vec_ref[0] += _fold8(dz)
        dy = dgl * _gelu_grad(ypv)
        vec_ref[1] += _fold8(dy * u)
        dyb = dy.astype(BF16)
        for k in range(nsb):
            cs, ss = slice(k * cw, (k + 1) * cw), slice(k * sw, (k + 1) * sw)
            gre_ref[:, ss] = _dot_nt(dyb[:, cs], cre_ref[k])
            gim_ref[:, ss] = -_dot_nt(dyb[:, cs], cim_ref[k])
            dcre_ref[k] += _dot_tn(sre_ref[:, ss], dyb[:, cs])
            dcim_ref[k] -= _dot_tn(sim_ref[:, ss], dyb[:, cs])

        row = lax.broadcasted_iota(jnp.int32, (8, sw), 0)

        def extra(r0, sl, gr, gi, cr, ci):
            nr = jnp.where(row == 7, cr, pltpu.roll(gr, 7, 0))
            ni = jnp.where(row == 7, ci, pltpu.roll(gi, 7, 0))
            sr, si = sre_ref[pl.ds(r0, 8), sl], sim_ref[pl.ds(r0, 8), sl]
            da_ref[0, :, sl] += nr * sr + ni * si
            da_ref[1, :, sl] += ni * sr - nr * si

        _s5_block_scan(gre_ref, gim_ref, cst_ref, car_ref, tc // 8, True, sw, extra)
        ub = u.astype(BF16)
        for k in range(nsb):
            cs, ss = slice(k * cw, (k + 1) * cw), slice(k * sw, (k + 1) * sw)
            gr, gi = gre_ref[:, ss].astype(BF16), gim_ref[:, ss].astype(BF16)
            duk = dy[:, cs] * d_ref[:, cs] + _dot_nt(gr, bre_ref[k]) + _dot_nt(gi, bim_ref[k])
            du_ref[:, cs] = duk.astype(BF16)
            dbre_ref[k] += _dot_tn(ub[:, cs], gr)
            dbim_ref[k] += _dot_tn(ub[:, cs], gi)

    def rowm(c):
        return pl.BlockSpec((tc, M), lambda i: (nc - 1 - i, c))
    rows = pl.BlockSpec((tc, NS), lambda i: (nc - 1 - i, 0))
    return pl.pallas_call(
        body, name="s5_bwd", grid=(nc,),
        in_specs=[rowm(dcol), rowm(col), rows, rows, rowm(0), rowm(0), _full(bre.shape), _full(bim.shape), _full(cre.shape),
                  _full(cim.shape), _full(cst.shape), _full((1, M)), _full((M, M))],
        out_specs=[rowm(0), _full(bre.shape), _full(bim.shape), _full(cre.shape), _full(cim.shape), _full((2, 8, NS)),
                   _full((M, M)), _full((3, 8, M))],
        out_shape=[jax.ShapeDtypeStruct((T, M), BF16), jax.ShapeDtypeStruct(bre.shape, F32), jax.ShapeDtypeStruct(bim.shape, F32),
                   jax.ShapeDtypeStruct(cre.shape, F32), jax.ShapeDtypeStruct(cim.shape, F32), jax.ShapeDtypeStruct((2, 8, NS), F32),
                   jax.ShapeDtypeStruct((M, M), F32), jax.ShapeDtypeStruct((3, 8, M), F32)],
        scratch_shapes=[pltpu.VMEM((tc, NS), F32), pltpu.VMEM((tc, NS), F32), pltpu.VMEM((2, 8, NS), F32)],
        compiler_params=ARB,
    )(dmix, proj, sre, sim, yp, z, bre, bim, cre, cim, cst, dvec.reshape(1, M), wglu)


def _ln_silu(h1, lg, lb):
    mu = jnp.mean(h1, axis=-1, keepdims=True)
    xc = h1 - mu
    rstd = lax.rsqrt(jnp.mean(xc * xc, axis=-1, keepdims=True) + EPS)
    xhat = xc * rstd
    h2 = xhat * lg + lb
    return xhat, rstd, h2, _sigmoid(h2)


def _shifted_copies(buf_ref, sh_ref, cs, n):
    for s in range(1, 8):
        sh_ref[s, 0:n, :] = buf_ref[pl.ds(s, n), cs]


def _tap(buf_ref, sh_ref, cs, off, r0, rows):
    q, s = divmod(off, 8)
    return buf_ref[pl.ds(r0 + 8 * q, rows), cs] if s == 0 else sh_ref[s, pl.ds(r0 + 8 * q, rows), :]


def _cv_fwd(proj, vcol, gcol, wdw, bdw, lg, lb, wpw, bpw):
    T = proj.shape[0]
    M = wpw.shape[0]
    tc = _pick(T, 256, 8)
    rb = _pick(tc, 64, 8)
    H, K = CONV_HALO, 31

    def body(v_ref, g_ref, wdw_ref, bdw_ref, lg_ref, lb_ref, wpw_ref, bpw_ref, y_ref, h1_ref, buf_ref, sh_ref):
        @pl.when(pl.program_id(0) == 0)
        def _():
            buf_ref[0:H, :] = jnp.zeros((H, M), F32)

        buf_ref[H:H + tc, :] = v_ref[...] * _sigmoid(g_ref[...])

        def strip(s, _):
            cs = pl.ds(pl.multiple_of(s * LANES, LANES), LANES)
            _shifted_copies(buf_ref, sh_ref, cs, tc + H - 8)
            for r0 in range(0, tc, rb):
                acc = jnp.zeros((rb, LANES), F32) + bdw_ref[:, cs]
                for k in range(K):
                    acc = acc + wdw_ref[k:k + 1, cs] * _tap(buf_ref, sh_ref, cs, H - (K - 1) + k, r0, rb)
                h1_ref[pl.ds(r0, rb), cs] = acc
            return 0

        lax.fori_loop(0, M // LANES, strip, 0)
        _, _, h2, sg = _ln_silu(h1_ref[...], lg_ref[...], lb_ref[...])
        y_ref[...] = (_dot(h2 * sg, wpw_ref[...]) + bpw_ref[...]).astype(BF16)
        buf_ref[0:H, :] = buf_ref[tc:tc + H, :]

    def rowm(c):
        return pl.BlockSpec((tc, M), lambda i: (i, c))
    vec = _full((1, M))
    return pl.pallas_call(
        body, name="cv_fwd", grid=(T // tc,),
        in_specs=[rowm(vcol), rowm(gcol), _full((32, M)), vec, vec, vec, _full((M, M)), vec],
        out_specs=[rowm(0), rowm(0)],
        out_shape=[jax.ShapeDtypeStruct((T, M), BF16), jax.ShapeDtypeStruct((T, M), F32)],
        scratch_shapes=[pltpu.VMEM((tc + H, M), F32), pltpu.VMEM((8, tc + H, LANES), F32)], compiler_params=ARB,
    )(proj, proj, wdw, bdw.reshape(1, M), lg.reshape(1, M), lb.reshape(1, M), wpw, bpw.reshape(1, M))


def _cv_bwd(dmix, dcol, proj, vcol, gcol, h1, wdw, lg, lb, wpw):
    T = proj.shape[0]
    M = wpw.shape[0]
    tc = _pick(T, 256, 8)
    nc = T // tc
    rb = _pick(tc, 32, 16)
    H, K = CONV_HALO, 31
    hb = tc // H

    def body(do_ref, v_ref, g_ref, vh_ref, gh_ref, h1_ref, wdw_ref, lg_ref, lb_ref, wpw_ref,
             dv_ref, dg_ref, dwpw_ref, dwdw_ref, vec_ref, hbuf_ref, dbuf_ref, hsh_ref, dsh_ref):
        i = pl.program_id(0)

        @pl.when(i == 0)
        def _():
            for r in (dwpw_ref, dwdw_ref, vec_ref):
                r[...] = jnp.zeros_like(r)
            dbuf_ref[tc:tc + H, :] = jnp.zeros((H, M), F32)

        halo = vh_ref[...] * _sigmoid(gh_ref[...])
        hbuf_ref[0:H, :] = jnp.where(i == nc - 1, 0.0, halo)
        hbuf_ref[H:H + tc, :] = v_ref[...] * _sigmoid(g_ref[...])
        do = do_ref[...]
        xhat, rstd, h2, sg = _ln_silu(h1_ref[...], lg_ref[...], lb_ref[...])
        dwpw_ref[...] += _dot_tn(h2 * sg, do)
        vec_ref[0] += _fold8(do)
        dh2 = _dot_nt(do, wpw_ref[...]) * (sg * (1.0 + h2 * (1.0 - sg)))
        vec_ref[1] += _fold8(dh2 * xhat)
        vec_ref[2] += _fold8(dh2)
        dxh = dh2 * lg_ref[...]
        dh1 = rstd * (dxh - jnp.mean(dxh, axis=-1, keepdims=True) - xhat * jnp.mean(dxh * xhat, axis=-1, keepdims=True))
        vec_ref[3] += _fold8(dh1)
        dbuf_ref[0:tc, :] = dh1

        def strip(s, _):
            cs = pl.ds(pl.multiple_of(s * LANES, LANES), LANES)
            _shifted_copies(dbuf_ref, dsh_ref, cs, tc + H - 8)
            _shifted_copies(hbuf_ref, hsh_ref, cs, tc + H - 8)
            for r0 in range(0, tc, rb):
                d1 = dbuf_ref[pl.ds(r0, rb), cs]
                dh0 = jnp.zeros((rb, LANES), F32)
                for k in range(K):
                    dh0 = dh0 + wdw_ref[k:k + 1, cs] * _tap(dbuf_ref, dsh_ref, cs, K - 1 - k, r0, rb)
                    dwdw_ref[k, :, cs] += _fold8(d1 * _tap(hbuf_ref, hsh_ref, cs, H - (K - 1) + k, r0, rb))
                sg0 = _sigmoid(g_ref[pl.ds(r0, rb), cs])
                dv_ref[pl.ds(r0, rb), cs] = (dh0 * sg0).astype(BF16)
                dg_ref[pl.ds(r0, rb), cs] = (dh0 * v_ref[pl.ds(r0, rb), cs] * sg0 * (1.0 - sg0)).astype(BF16)
            return 0

        lax.fori_loop(0, M // LANES, strip, 0)
        dbuf_ref[tc:tc + H, :] = dbuf_ref[0:H, :]

    def rowm(c):
        return pl.BlockSpec((tc, M), lambda i: (nc - 1 - i, c))

    def halo(c):
        return pl.BlockSpec((H, M), lambda i: (jnp.maximum((nc - 1 - i) * hb - 1, 0), c))
    vec = _full((1, M))
    return pl.pallas_call(
        body, name="cv_bwd", grid=(nc,),
        in_specs=[rowm(dcol), rowm(vcol), rowm(gcol), halo(vcol), halo(gcol), rowm(0), _full((32, M)), vec, vec, _full((M, M))],
        out_specs=[rowm(0), rowm(0), _full((M, M)), _full((32, 8, M)), _full((5, 8, M))],
        out_shape=[jax.ShapeDtypeStruct((T, M), BF16), jax.ShapeDtypeStruct((T, M), BF16), jax.ShapeDtypeStruct((M, M), F32),
                   jax.ShapeDtypeStruct((32, 8, M), F32), jax.ShapeDtypeStruct((5, 8, M), F32)],
        scratch_shapes=[pltpu.VMEM((tc + H, M), F32), pltpu.VMEM((tc + H, M), F32), pltpu.VMEM((8, tc + H, LANES), F32),
                        pltpu.VMEM((8, tc + H, LANES), F32)], compiler_params=ARB,
    )(dmix, proj, proj, proj, proj, h1, wdw, lg.reshape(1, M), lb.reshape(1, M), wpw)


def _lru_gates(xc, r, ig, sp8):
    la = -r * sp8
    a = jnp.exp(la)
    mult = jnp.sqrt(_neg_expm1(2.0 * la))
    return a, mult, mult * (ig * xc)


def _lru_fwd(proj, xcol, gcol, wcv, bcv, wr, br, wi, bi, sp8):
    T = proj.shape[0]
    M = wr.shape[0]
    tc = _pick(T, 256, 8)
    H, K = SMALL_HALO, 4

    def body(x_ref, g_ref, wcv_ref, bcv_ref, wr_ref, br_ref, wi_ref, bi_ref, sp_ref,
             y_ref, xc_ref, r_ref, i_ref, hp_ref, buf_ref, a_ref, car_ref):
        @pl.when(pl.program_id(0) == 0)
        def _():
            buf_ref[0:H, :] = jnp.zeros((H, M), F32)
            car_ref[...] = jnp.zeros_like(car_ref)

        buf_ref[H:H + tc, :] = x_ref[...]
        xc = jnp.zeros((tc, M), F32) + bcv_ref[...]
        for k in range(K):
            xc = xc + wcv_ref[k:k + 1, :] * buf_ref[pl.ds(H - (K - 1) + k, tc), :]
        buf_ref[0:H, :] = buf_ref[tc:tc + H, :]
        r = _sigmoid(_dot(xc, wr_ref[...]) + br_ref[...])
        ig = _sigmoid(_dot(xc, wi_ref[...]) + bi_ref[...])
        xc_ref[...] = xc
        r_ref[...] = r
        i_ref[...] = ig
        a, _, bt = _lru_gates(xc, r, ig, sp_ref[...])
        a_ref[...] = a
        hp_ref[...] = bt
        row = lax.broadcasted_iota(jnp.int32, (8, M), 0)

        def blk(ib, _):
            r0 = pl.multiple_of(ib * 8, 8)
            av, bv = a_ref[pl.ds(r0, 8), :], hp_ref[pl.ds(r0, 8), :]
            for d in (1, 2, 4):
                m = row >= d
                bv = jnp.where(m, bv + av * pltpu.roll(bv, d, 0), bv)
                av = jnp.where(m, av * pltpu.roll(av, d, 0), av)
            cr = car_ref[7:8, :]
            h = bv + av * cr
            a_ref[pl.ds(r0, 8), :] = h
            hp_ref[pl.ds(r0, 8), :] = jnp.where(row == 0, cr, pltpu.roll(h, 1, 0))
            car_ref[...] = h
            return 0

        lax.fori_loop(0, tc // 8, blk, 0)
        y_ref[...] = (a_ref[...] * _gelu(g_ref[...])).astype(BF16)

    def rowm(c):
        return pl.BlockSpec((tc, M), lambda i: (i, c))
    vec = _full((1, M))
    return pl.pallas_call(
        body, name="lru_fwd", grid=(T // tc,),
        in_specs=[rowm(xcol), rowm(gcol), _full((8, M)), vec, _full((M, M)), vec, _full((M, M)), vec, vec],
        out_specs=[rowm(0)] * 5,
        out_shape=[jax.ShapeDtypeStruct((T, M), BF16)] + [jax.ShapeDtypeStruct((T, M), F32)] * 4,
        scratch_shapes=[pltpu.VMEM((tc + H, M), F32), pltpu.VMEM((tc, M), F32), pltpu.VMEM((8, M), F32)], compiler_params=ARB,
    )(proj, proj, wcv, bcv.reshape(1, M), wr, br.reshape(1, M), wi, bi.reshape(1, M), sp8.reshape(1, M))


def _lru_bwd(dmix, dcol, proj, xcol, gcol, xc, r, ig, hprev, wcv, wr, wi, sp8):
    T = proj.shape[0]
    M = wr.shape[0]
    tc = _pick(T, 256, 8)
    nc = T // tc
    H, K = SMALL_HALO, 4
    hb = tc // H

    def body(do_ref, x_ref, g_ref, xh_ref, xc_ref, r_ref, i_ref, hp_ref, wcv_ref, wr_ref, wi_ref, sp_ref,
             dx_ref, dg_ref, dwr_ref, dwi_ref, dwcv_ref, vec_ref, xbuf_ref, dbuf_ref, a_ref, gs_ref, car_ref):
        i = pl.program_id(0)

        @pl.when(i == 0)
        def _():
            for rf in (dwr_ref, dwi_ref, dwcv_ref, vec_ref, car_ref):
                rf[...] = jnp.zeros_like(rf)
            dbuf_ref[tc:tc + H, :] = jnp.zeros((H, M), F32)

        xcv, rv, iv, hpv, gv, do = xc_ref[...], r_ref[...], i_ref[...], hp_ref[...], g_ref[...], do_ref[...]
        sp = sp_ref[...]
        a, mult, bt = _lru_gates(xcv, rv, iv, sp)
        h = a * hpv + bt
        ge = _gelu(gv)
        dg_ref[...] = (do * h * _gelu_grad(gv)).astype(BF16)
        a_ref[...] = a
        gs_ref[...] = do * ge
        row = lax.broadcasted_iota(jnp.int32, (8, M), 0)

        def blk(n, _):
            r0 = pl.multiple_of((tc // 8 - 1 - n) * 8, 8)
            ablk = a_ref[pl.ds(r0, 8), :]
            av = jnp.where(row == 7, car_ref[0, 0:1, :], pltpu.roll(ablk, 7, 0))
            bv = gs_ref[pl.ds(r0, 8), :]
            for d in (1, 2, 4):
                m = row <= 7 - d
                bv = jnp.where(m, bv + av * pltpu.roll(bv, 8 - d, 0), bv)
                av = jnp.where(m, av * pltpu.roll(av, 8 - d, 0), av)
            gsv = bv + av * car_ref[1, 0:1, :]
            gs_ref[pl.ds(r0, 8), :] = gsv
            car_ref[0] = ablk
            car_ref[1] = gsv
            return 0

        lax.fori_loop(0, tc // 8, blk, 0)
        gs = gs_ref[...]
        dixc = gs * mult
        dla = gs * hpv * a - gs * (iv * xcv) * (a * a) / mult
        vec_ref[2] += _fold8(-dla * rv)
        dpr = (-dla * sp) * rv * (1.0 - rv)
        dpi = (dixc * xcv) * iv * (1.0 - iv)
        vec_ref[0] += _fold8(dpr)
        vec_ref[1] += _fold8(dpi)
        dwr_ref[...] += _dot_tn(xcv, dpr)
        dwi_ref[...] += _dot_tn(xcv, dpi)
        dxc = dixc * iv + _dot_nt(dpr, wr_ref[...]) + _dot_nt(dpi, wi_ref[...])
        vec_ref[3] += _fold8(dxc)
        xbuf_ref[0:H, :] = jnp.where(i == nc - 1, 0.0, xh_ref[...])
        xbuf_ref[H:H + tc, :] = x_ref[...]
        dbuf_ref[0:tc, :] = dxc
        dx = jnp.zeros((tc, M), F32)
        for k in range(K):
            dx = dx + wcv_ref[k:k + 1, :] * dbuf_ref[pl.ds(K - 1 - k, tc), :]
            dwcv_ref[k] += _fold8(dxc * xbuf_ref[pl.ds(H - (K - 1) + k, tc), :])
        dx_ref[...] = dx.astype(BF16)
        dbuf_ref[tc:tc + H, :] = dbuf_ref[0:H, :]

    def rowm(c):
        return pl.BlockSpec((tc, M), lambda i: (nc - 1 - i, c))
    halo = pl.BlockSpec((H, M), lambda i: (jnp.maximum((nc - 1 - i) * hb - 1, 0), xcol))
    vec = _full((1, M))
    return pl.pallas_call(
        body, name="lru_bwd", grid=(nc,),
        in_specs=[rowm(dcol), rowm(xcol), rowm(gcol), halo, rowm(0), rowm(0), rowm(0), rowm(0), _full((8, M)),
                  _full((M, M)), _full((M, M)), vec],
        out_specs=[rowm(0), rowm(0), _full((M, M)), _full((M, M)), _full((8, 8, M)), _full((4, 8, M))],
        out_shape=[jax.ShapeDtypeStruct((T, M), BF16), jax.ShapeDtypeStruct((T, M), BF16), jax.ShapeDtypeStruct((M, M), F32),
                   jax.ShapeDtypeStruct((M, M), F32), jax.ShapeDtypeStruct((8, 8, M), F32), jax.ShapeDtypeStruct((4, 8, M), F32)],
        scratch_shapes=[pltpu.VMEM((tc + H, M), F32), pltpu.VMEM((tc + H, M), F32), pltpu.VMEM((tc, M), F32),
                        pltpu.VMEM((tc, M), F32), pltpu.VMEM((2, 8, M), F32)],
        compiler_params=ARB,
    )(dmix, proj, proj, proj, xc, r, ig, hprev, wcv, wr, wi, sp8.reshape(1, M))


def _pool_diffs(buf_ref, x, t0, tc, gw):
    H = POOL_HALO
    pos = (lax.broadcasted_iota(jnp.int32, (tc, gw), 0) + t0 + 1).astype(F32)
    out, inv = [], []
    for gi, win in enumerate(POOL_WINDOWS):
        sl = slice(gi * gw, (gi + 1) * gw)
        s = x[:, sl]
        for j in range(1, win):
            s = s + buf_ref[pl.ds(H - j, tc), sl]
        ic = 1.0 / jnp.minimum(pos, float(win))
        out.append(s * ic - x[:, sl])
        inv.append(ic)
    return out, inv


def _pool_fwd(proj, col, pw, scale):
    T = proj.shape[0]
    ng, gw, _ = pw.shape
    M = ng * gw
    tc = _pick(T, 256, 8)
    H = POOL_HALO

    def body(x_ref, pw_ref, sc_ref, y_ref, buf_ref):
        @pl.when(pl.program_id(0) == 0)
        def _():
            buf_ref[0:H, :] = jnp.zeros((H, M), F32)

        x = x_ref[...]
        buf_ref[H:H + tc, :] = x
        diffs, _ = _pool_diffs(buf_ref, x, pl.program_id(0) * tc, tc, gw)
        for gi in range(ng):
            sl = slice(gi * gw, (gi + 1) * gw)
            y_ref[:, sl] = (_dot(diffs[gi], pw_ref[gi]) * sc_ref[:, sl]).astype(BF16)
        buf_ref[0:H, :] = buf_ref[tc:tc + H, :]

    return pl.pallas_call(
        body, name="pool_fwd", grid=(T // tc,),
        in_specs=[pl.BlockSpec((tc, M), lambda i: (i, col)), _full(pw.shape), _full((1, M))],
        out_specs=pl.BlockSpec((tc, M), lambda i: (i, 0)), out_shape=jax.ShapeDtypeStruct((T, M), BF16),
        scratch_shapes=[pltpu.VMEM((tc + H, M), F32)], compiler_params=ARB,
    )(proj, pw, scale.reshape(1, M))


def _pool_bwd(dmix, dcol, proj, col, pw, scale):
    T = proj.shape[0]
    ng, gw, _ = pw.shape
    M = ng * gw
    tc = _pick(T, 256, 8)
    nc = T // tc
    H = POOL_HALO
    hb = tc // H

    def body(do_ref, x_ref, xh_ref, pw_ref, sc_ref, dx_ref, dpw_ref, dsc_ref, buf_ref, ebuf_ref):
        i = pl.program_id(0)

        @pl.when(i == 0)
        def _():
            dpw_ref[...] = jnp.zeros_like(dpw_ref)
            dsc_ref[...] = jnp.zeros_like(dsc_ref)
            ebuf_ref[tc:tc + H, :] = jnp.zeros((H, M), F32)

        x, do = x_ref[...], do_ref[...]
        buf_ref[0:H, :] = jnp.where(i == nc - 1, 0.0, xh_ref[...])
        buf_ref[H:H + tc, :] = x
        diffs, inv = _pool_diffs(buf_ref, x, (nc - 1 - i) * tc, tc, gw)
        ddg = []
        for gi in range(ng):
            sl = slice(gi * gw, (gi + 1) * gw)
            dyy = do[:, sl] * sc_ref[:, sl]
            dsc_ref[:, sl] += _fold8(do[:, sl] * _dot(diffs[gi], pw_ref[gi]))
            dpw_ref[gi] += _dot_tn(diffs[gi], dyy)
            d = _dot_nt(dyy, pw_ref[gi])
            ddg.append(d)
            ebuf_ref[0:tc, sl] = d * inv[gi]
        for gi, win in enumerate(POOL_WINDOWS):
            sl = slice(gi * gw, (gi + 1) * gw)
            s = -ddg[gi]
            for j in range(win):
                s = s + ebuf_ref[pl.ds(j, tc), sl]
            dx_ref[:, sl] = s.astype(BF16)
        ebuf_ref[tc:tc + H, :] = ebuf_ref[0:H, :]

    return pl.pallas_call(
        body, name="pool_bwd", grid=(nc,),
        in_specs=[pl.BlockSpec((tc, M), lambda i: (nc - 1 - i, dcol)), pl.BlockSpec((tc, M), lambda i: (nc - 1 - i, col)),
                  pl.BlockSpec((H, M), lambda i: (jnp.maximum((nc - 1 - i) * hb - 1, 0), col)), _full(pw.shape), _full((1, M))],
        out_specs=[pl.BlockSpec((tc, M), lambda i: (nc - 1 - i, 0)), _full(pw.shape), _full((8, M))],
        out_shape=[jax.ShapeDtypeStruct((T, M), BF16), jax.ShapeDtypeStruct(pw.shape, F32), jax.ShapeDtypeStruct((8, M), F32)],
        scratch_shapes=[pltpu.VMEM((tc + H, M), F32), pltpu.VMEM((tc + H, M), F32)], compiler_params=ARB,
    )(dmix, proj, proj, pw, scale.reshape(1, M))


def _ffn_mid_fwd(up, wdw, bdw):
    T, F2 = up.shape
    F = F2 // 2
    tc = _pick(T, 256, 8)
    rb = _pick(tc, 64, 16)
    H, K = FFN_HALO, 3

    def body(gt_ref, val_ref, w_ref, b_ref, act_ref, buf_ref):
        @pl.when(pl.program_id(0) == 0)
        def _():
            buf_ref[0:H, :] = jnp.zeros((H, F), F32)

        buf_ref[H:H + tc, :] = gt_ref[...].astype(F32)

        def strip(s, _):
            cs = pl.ds(pl.multiple_of(s * LANES, LANES), LANES)
            taps = [w_ref[k:k + 1, cs] for k in range(K)]
            for r0 in range(0, tc, rb):
                gc = b_ref[:, cs] + taps[0] * buf_ref[pl.ds(H - (K - 1) + r0, rb), cs]
                for k in range(1, K):
                    gc = gc + taps[k] * buf_ref[pl.ds(H - (K - 1) + k + r0, rb), cs]
                act_ref[pl.ds(r0, rb), cs] = (_gelu(gc) * val_ref[pl.ds(r0, rb), cs].astype(F32)).astype(BF16)
            return 0

        lax.fori_loop(0, F // LANES, strip, 0)
        buf_ref[0:H, :] = buf_ref[tc:tc + H, :]

    return pl.pallas_call(
        body, name="ffn_mid_fwd", grid=(T // tc,),
        in_specs=[pl.BlockSpec((tc, F), lambda i: (i, 0)), pl.BlockSpec((tc, F), lambda i: (i, 1)), _full((8, F)), _full((1, F))],
        out_specs=pl.BlockSpec((tc, F), lambda i: (i, 0)), out_shape=jax.ShapeDtypeStruct((T, F), BF16),
        scratch_shapes=[pltpu.VMEM((tc + H, F), F32)], compiler_params=ARB,
    )(up, up, wdw, bdw.reshape(1, F))


def _ffn_mid_bwd(dact, up, wdw, bdw):
    T, F2 = up.shape
    F = F2 // 2
    tc = _pick(T, 256, 8)
    rb = _pick(tc, 64, 16)
    nc = T // tc
    H, K = FFN_HALO, 3
    hb = tc // H

    def body(da_ref, gt_ref, val_ref, gh_ref, w_ref, b_ref, dup_ref, dw_ref, db_ref, gbuf_ref, dbuf_ref):
        i = pl.program_id(0)

        @pl.when(i == 0)
        def _():
            dw_ref[...] = jnp.zeros_like(dw_ref)
            db_ref[...] = jnp.zeros_like(db_ref)
            dbuf_ref[tc:tc + H, :] = jnp.zeros((H, F), F32)

        gbuf_ref[0:H, :] = jnp.where(i == nc - 1, 0.0, gh_ref[...].astype(F32))
        gbuf_ref[H:H + tc, :] = gt_ref[...].astype(F32)

        def strip(s, _):
            cs = pl.ds(pl.multiple_of(s * LANES, LANES), LANES)
            cs_val = pl.ds(pl.multiple_of(F + s * LANES, LANES), LANES)
            taps = [w_ref[k:k + 1, cs] for k in range(K)]
            for r0 in range(0, tc, rb):
                shifted = [gbuf_ref[pl.ds(H - (K - 1) + k + r0, rb), cs] for k in range(K)]
                gc = b_ref[:, cs] + taps[0] * shifted[0]
                for k in range(1, K):
                    gc = gc + taps[k] * shifted[k]
                ge, gg = _gelu_and_grad(gc)
                da = da_ref[pl.ds(r0, rb), cs].astype(F32)
                dup_ref[pl.ds(r0, rb), cs_val] = (da * ge).astype(BF16)
                dgc = da * val_ref[pl.ds(r0, rb), cs].astype(F32) * gg
                dbuf_ref[pl.ds(r0, rb), cs] = dgc
                db_ref[:, cs] += _fold8(dgc)
                for k in range(K):
                    dw_ref[k, :, cs] += _fold8(dgc * shifted[k])
            for r0 in range(0, tc, rb):
                dgt = taps[0] * dbuf_ref[pl.ds(K - 1 + r0, rb), cs]
                for k in range(1, K):
                    dgt = dgt + taps[k] * dbuf_ref[pl.ds(K - 1 - k + r0, rb), cs]
                dup_ref[pl.ds(r0, rb), cs] = dgt.astype(BF16)
            return 0

        lax.fori_loop(0, F // LANES, strip, 0)
        dbuf_ref[tc:tc + H, :] = dbuf_ref[0:H, :]

    return pl.pallas_call(
        body, name="ffn_mid_bwd", grid=(nc,),
        in_specs=[pl.BlockSpec((tc, F), lambda i: (nc - 1 - i, 0)), pl.BlockSpec((tc, F), lambda i: (nc - 1 - i, 0)),
                  pl.BlockSpec((tc, F), lambda i: (nc - 1 - i, 1)),
                  pl.BlockSpec((H, F), lambda i: (jnp.maximum((nc - 1 - i) * hb - 1, 0), 0)), _full((8, F)), _full((1, F))],
        out_specs=[pl.BlockSpec((tc, F2), lambda i: (nc - 1 - i, 0)), _full((8, 8, F)), _full((8, F))],
        out_shape=[jax.ShapeDtypeStruct((T, F2), BF16), jax.ShapeDtypeStruct((8, 8, F), F32), jax.ShapeDtypeStruct((8, F), F32)],
        scratch_shapes=[pltpu.VMEM((tc + H, F), F32), pltpu.VMEM((tc + H, F), F32)], compiler_params=ARB,
    )(dact, up, up, up, wdw, bdw.reshape(1, F))


def _my_pos():
    return lax.axis_index("x"), lax.axis_index("y"), lax.axis_index("c")


def _flip(pos, k):
    x, y, c = pos
    return ((1 - x) if k & 4 else x, (1 - y) if k & 2 else y, (1 - c) if k & 1 else c)


def _rank(pos):
    return 4 * pos[0] + 2 * pos[1] + pos[2]


_HBM = pl.BlockSpec(memory_space=pltpu.HBM)
_SEM = pl.BlockSpec(memory_space=pltpu.SEMAPHORE)
_EFFECT = pltpu.SideEffectType.DATAFLOW_SIDE_EFFECTING


def _plan_direct(gather):
    def plan(bufs, ssem, rsem, me, outgoing):
        n = len(bufs) // 2
        out = []
        for a in range(n):
            for k in range(N_DEV):
                peer = _flip(me, k)
                out.append(pltpu.make_async_remote_copy(
                    src_ref=bufs[a] if gather else bufs[a].at[_rank(peer)],
                    dst_ref=bufs[n + a].at[_rank(me) if outgoing else _rank(peer)],
                    send_sem=ssem.at[a * N_DEV + k], recv_sem=rsem.at[a * N_DEV + k], device_id=peer,
                    device_id_type=pl.DeviceIdType.MESH))
        return out
    return plan, N_DEV


_SAME_CORE = (2, 4, 6)


def _plan_own_block(bufs, ssem, rsem, me, outgoing):
    n = len(bufs) // 2
    out = []
    for a in range(n):
        for j, k in enumerate((0, 1) + _SAME_CORE):
            peer = _flip(me, k)
            out.append(pltpu.make_async_remote_copy(
                src_ref=bufs[a], dst_ref=bufs[n + a].at[_rank(me) if outgoing else _rank(peer)],
                send_sem=ssem.at[a * 5 + j], recv_sem=rsem.at[a * 5 + j], device_id=peer, device_id_type=pl.DeviceIdType.MESH))
    return out


def _plan_pass_on(bufs, ssem, rsem, me, outgoing):
    sibling = _flip(me, 1)
    out = []
    for a in range(len(bufs)):
        for j, k in enumerate(_SAME_CORE):
            slot = _rank(_flip(me, k) if outgoing else _flip(sibling, k))
            out.append(pltpu.make_async_remote_copy(
                src_ref=bufs[a].at[slot], dst_ref=bufs[a].at[slot], send_sem=ssem.at[a * 3 + j], recv_sem=rsem.at[a * 3 + j],
                device_id=sibling, device_id_type=pl.DeviceIdType.MESH))
    return out


def _split_start(bufs, plan, n_sems, name, after=None):
    nb = len(bufs)
    extra = [] if after is None else [after]

    def body(*refs):
        ssem, rsem = refs[nb + len(extra)], refs[nb + len(extra) + 1]
        for cp in plan(refs[:nb], ssem, rsem, _my_pos(), True):
            cp.start()
        token = refs[2 * nb + len(extra) + 2]
        token[...] = jnp.zeros_like(token)

    res = pl.pallas_call(
        body, name=name,
        out_shape=(pltpu.SemaphoreType.DMA((n_sems,)), pltpu.SemaphoreType.DMA((n_sems,)),
                   *[pltpu.HBM(t.shape, t.dtype) for t in bufs], jax.ShapeDtypeStruct((8, 128), F32)),
        in_specs=[_HBM] * nb + [pl.BlockSpec(memory_space=pl.ANY)] * len(extra),
        out_specs=(_SEM, _SEM, *([_HBM] * nb), pl.BlockSpec(memory_space=pltpu.VMEM)),
        input_output_aliases={i: 2 + i for i in range(nb)},
        compiler_params=pltpu.CompilerParams(has_side_effects=_EFFECT),
    )(*[pltpu.with_memory_space_constraint(t, pltpu.HBM) for t in bufs], *extra)
    return (res[0], res[1], list(res[2:2 + nb])), res[2 + nb]


def _split_wait(handle, plan, after, name):
    ssem, rsem, bufs = handle
    nb = len(bufs)

    def body(*refs):
        for cp in plan(refs[:nb], refs[nb], refs[nb + 1], _my_pos(), False):
            cp.wait_send()
            cp.wait_recv()

    res = pl.pallas_call(
        body, name=name, out_shape=tuple(pltpu.HBM(t.shape, t.dtype) for t in bufs),
        in_specs=[_HBM] * nb + [_SEM, _SEM, pl.BlockSpec(memory_space=pl.ANY)], out_specs=tuple([_HBM] * nb),
        input_output_aliases={i: i for i in range(nb)},
        compiler_params=pltpu.CompilerParams(has_side_effects=_EFFECT),
    )(*bufs, ssem, rsem, after)
    return list(res)


def _landing_zones(xs, gather):
    return [lax.empty(((N_DEV,) + tuple(x.shape)) if gather else tuple(x.shape), x.dtype) for x in xs]


def _gather_start(xs, name, after=None, two_level=True):
    bufs = list(xs) + _landing_zones(xs, True)
    if two_level:
        handle, token = _split_start(bufs, _plan_own_block, 5 * len(xs), name + "_start", after)
    else:
        plan, per = _plan_direct(True)
        handle, token = _split_start(bufs, plan, per * len(xs), name + "_start", after)
    return (handle, two_level, name), token


def _gather_wait(state, after):
    handle, two_level, name = state
    n = len(handle[2]) // 2
    if not two_level:
        return _split_wait(handle, _plan_direct(True)[0], after, name + "_wait")[n:]
    lands = _split_wait(handle, _plan_own_block, after, name + "_wait")[n:]
    handle, _ = _split_start(lands, _plan_pass_on, 3 * n, name + "_pass")
    return _split_wait(handle, _plan_pass_on, after, name + "_passed")


def _scatter_start(xs, name):
    plan, per = _plan_direct(False)
    handle, token = _split_start(list(xs) + _landing_zones(xs, False), plan, per * len(xs), name + "_start")
    return (handle, name), token


def _scatter_wait(state, after):
    handle, name = state
    return _split_wait(handle, _plan_direct(False)[0], after, name + "_wait")[len(handle[2]) // 2:]


def _adamw_math(w, g, m, v):
    m = ADAM_B1 * m + (1.0 - ADAM_B1) * g
    v = ADAM_B2 * v + (1.0 - ADAM_B2) * (g * g)
    m_hat = m / (1.0 - ADAM_B1 ** ADAM_STEP)
    v_hat = v / (1.0 - ADAM_B2 ** ADAM_STEP)
    return -ADAM_LR * (m_hat / (jnp.sqrt(v_hat) + ADAM_EPS) + ADAM_WD * w), m, v


def _sum8(parts, *, name):
    _, R, C = parts.shape
    tr = _pick(R, 256, 16)

    def body(p_ref, o_ref):
        acc = p_ref[0].astype(F32)
        for r in range(1, N_DEV):
            acc = acc + p_ref[r].astype(F32)
        o_ref[...] = acc

    return pl.pallas_call(
        body, name=name, grid=(R // tr,), in_specs=[pl.BlockSpec((N_DEV, tr, C), lambda i: (0, i, 0))],
        out_specs=pl.BlockSpec((tr, C), lambda i: (i, 0)), out_shape=jax.ShapeDtypeStruct((R, C), F32),
        compiler_params=pltpu.CompilerParams(dimension_semantics=("parallel",)),
    )(parts)


def _adamw_layer(g, w, m, v, layer, prev, *, name):
    depth, R, C = w.shape
    tr = _pick(R, 64, 16)
    summed = g.ndim == 3
    if prev is None:
        prev = [lax.empty((depth, R, C), F32) for _ in range(4)]

    def body(g_ref, w_ref, m_ref, v_ref, *rest):
        outs = rest[4:]
        if summed:
            gv = g_ref[0].astype(F32)
            for r in range(1, N_DEV):
                gv = gv + g_ref[r].astype(F32)
        else:
            gv = g_ref[...]
        d, nm, nv = _adamw_math(w_ref[...], gv, m_ref[...], v_ref[...])
        for o, val in zip(outs, (gv, d, nm, nv)):
            o[...] = val

    g_spec = pl.BlockSpec((N_DEV, tr, C), lambda i: (0, i, 0)) if summed else pl.BlockSpec((tr, C), lambda i: (i, 0))
    slab = pl.BlockSpec((None, tr, C), lambda i: (layer, i, 0))
    return pl.pallas_call(
        body, name=name, grid=(R // tr,), in_specs=[g_spec] + [slab] * 3 + [pl.BlockSpec(memory_space=pl.ANY)] * 4,
        out_specs=[slab] * 4, out_shape=[jax.ShapeDtypeStruct((depth, R, C), F32)] * 4,
        input_output_aliases={4 + q: q for q in range(4)},
        compiler_params=pltpu.CompilerParams(dimension_semantics=("parallel",)),
    )(g, w, m, v, *prev)


def _lane_view(shape):
    if len(shape) == 1:
        return (1, shape[0])
    size = math.prod(shape)
    if shape[-1] < 64 and size % LANES == 0:
        return (size // LANES, LANES)
    if len(shape) > 2 and shape[-2] % 8 == 0:
        return (size // shape[-1], shape[-1])
    return tuple(shape)


def _adamw_many(ws, gs, ms, vs, *, name):
    n = len(ws)
    views = [_lane_view(w.shape) for w in ws]

    def body(*refs):
        for i in range(n):
            d, nm, nv = _adamw_math(refs[i][...], refs[n + i][...], refs[2 * n + i][...], refs[3 * n + i][...])
            refs[4 * n + i][...] = d
            refs[5 * n + i][...] = nm
            refs[6 * n + i][...] = nv

    outs = pl.pallas_call(
        body, name=name, out_shape=[jax.ShapeDtypeStruct(s, F32) for s in views] * 3,
    )(*[a.reshape(s) for arrs in (ws, gs, ms, vs) for a, s in zip(arrs, views)])
    return [[o.reshape(w.shape) for o, w in zip(outs[q * n:(q + 1) * n], ws)] for q in range(3)]


def _pack(arrs, dtype=F32):
    flat = jnp.concatenate([a.reshape(-1).astype(dtype) for a in arrs])
    pad = (-flat.shape[0]) % 2048
    return jnp.pad(flat, (0, pad)).reshape(-1, 128)


def _unpack(buf, shapes):
    flat, out, off = buf.reshape(-1), [], 0
    for s in shapes:
        n = math.prod(s)
        out.append(flat[off:off + n].reshape(s))
        off += n
    return out


def _s5_params(lam_re, lam_im, log_step, b_re, b_im):
    G, P, H = b_re.shape
    step = jnp.exp(log_step)[:, None]
    mag = jnp.exp(lam_re * step)
    ar, ai = mag * jnp.cos(lam_im * step), mag * jnp.sin(lam_im * step)
    den = lam_re * lam_re + lam_im * lam_im
    qr = ((ar - 1.0) * lam_re + ai * lam_im) / den
    qi = (ai * lam_re - (ar - 1.0) * lam_im) / den
    bbr = qr[..., None] * b_re - qi[..., None] * b_im
    bbi = qr[..., None] * b_im + qi[..., None] * b_re
    return ar.reshape(-1), ai.reshape(-1), _s5_blocks_in(bbr), _s5_blocks_in(bbi)


def _s5_blocks_in(b):
    G, P, H = b.shape
    eye = jnp.eye(8, dtype=b.dtype)
    return jnp.einsum("kgph,gj->kghjp", b.reshape(G // 8, 8, P, H), eye).reshape(G // 8, 8 * H, 8 * P)


def _s5_blocks_out(c):
    G, H, P = c.shape
    eye = jnp.eye(8, dtype=c.dtype)
    return jnp.einsum("kghp,gj->kgpjh", c.reshape(G // 8, 8, H, P), eye).reshape(G // 8, 8 * P, 8 * H)


def _s5_blocks_out_diag(m, G, H, P):
    return jnp.einsum("kgpgh->kghp", m.reshape(G // 8, 8, P, 8, H)).reshape(G, H, P)


def _head_blocks(w):
    nh, d, e = w.shape
    return jnp.einsum("hde,hk->hdke", w, jnp.eye(nh, dtype=w.dtype)).reshape(nh * d, nh * e)


def _head_blocks_diag(m, nh, d, e):
    return jnp.einsum("hdhe->hde", m.reshape(nh, d, nh, e))


def _pad_rows(w, rows):
    return jnp.pad(w, ((0, rows - w.shape[0]), (0, 0)))


def _unshard_cols(g):
    return jnp.transpose(g, (1, 0, 2)).reshape(g.shape[1], -1)


_NAMES = ['norm_mix_g', 'w_in', 's5_lam_re', 's5_lam_im', 's5_log_step', 's5_b_re', 's5_b_im', 's5_c_re', 's5_c_im', 's5_d',
          's5_w_glu', 's5_b_glu', 'cv_w_dw', 'cv_b_dw', 'cv_ln_g', 'cv_ln_b', 'cv_w_pw', 'cv_b_pw', 'lru_w_conv', 'lru_b_conv',
          'lru_w_r', 'lru_b_r', 'lru_w_i', 'lru_b_i', 'lru_lam', 'pool_w', 'pool_scale', 'w_out', 'norm_ffn_g', 'ffn_w_up',
          'ffn_w_dw', 'ffn_b_dw', 'ffn_w_down', 'norm_final_g']
_MATRIX = ('w_in', 'w_out', 'ffn_w_up', 'ffn_w_down', 's5_w_glu', 'cv_w_pw')
_TRANSPOSED = ('w_in', 'ffn_w_up')
_COLSHARD = ('cv_w_dw', 'lru_w_conv', 'ffn_w_dw')
_GROUPS = (('w_in', 's5_w_glu', 'cv_w_pw', 'w_out'), ('ffn_w_up', 'ffn_w_down'))
_SMALL_BF16 = ('s5_b_re', 's5_b_im', 's5_c_re', 's5_c_im', 'lru_w_r', 'lru_w_i', 'pool_w')
_SCATTER_GROUPS = (('w_out', 's5_w_glu', 'cv_w_pw'), ('w_in',), ('ffn_w_up', 'ffn_w_down'))


def kernel(x, norm_mix_g, w_in, s5_lam_re, s5_lam_im, s5_log_step, s5_b_re, s5_b_im, s5_c_re, s5_c_im, s5_d, s5_w_glu, s5_b_glu, cv_w_dw, cv_b_dw, cv_ln_g, cv_ln_b, cv_w_pw, cv_b_pw, lru_w_conv, lru_b_conv, lru_w_r, lru_b_r, lru_w_i, lru_b_i, lru_lam, pool_w, pool_scale, w_out, norm_ffn_g, ffn_w_up, ffn_w_dw, ffn_b_dw, ffn_w_down, norm_final_g, loss_target, m_norm_mix_g, m_w_in, m_s5_lam_re, m_s5_lam_im, m_s5_log_step, m_s5_b_re, m_s5_b_im, m_s5_c_re, m_s5_c_im, m_s5_d, m_s5_w_glu, m_s5_b_glu, m_cv_w_dw, m_cv_b_dw, m_cv_ln_g, m_cv_ln_b, m_cv_w_pw, m_cv_b_pw, m_lru_w_conv, m_lru_b_conv, m_lru_w_r, m_lru_b_r, m_lru_w_i, m_lru_b_i, m_lru_lam, m_pool_w, m_pool_scale, m_w_out, m_norm_ffn_g, m_ffn_w_up, m_ffn_w_dw, m_ffn_b_dw, m_ffn_w_down, m_norm_final_g, v_norm_mix_g, v_w_in, v_s5_lam_re, v_s5_lam_im, v_s5_log_step, v_s5_b_re, v_s5_b_im, v_s5_c_re, v_s5_c_im, v_s5_d, v_s5_w_glu, v_s5_b_glu, v_cv_w_dw, v_cv_b_dw, v_cv_ln_g, v_cv_ln_b, v_cv_w_pw, v_cv_b_pw, v_lru_w_conv, v_lru_b_conv, v_lru_w_r, v_lru_b_r, v_lru_w_i, v_lru_b_i, v_lru_lam, v_pool_w, v_pool_scale, v_w_out, v_norm_ffn_g, v_ffn_w_up, v_ffn_w_dw, v_ffn_b_dw, v_ffn_w_down, v_norm_final_g):
    args = (x, norm_mix_g, w_in, s5_lam_re, s5_lam_im, s5_log_step, s5_b_re, s5_b_im, s5_c_re, s5_c_im, s5_d, s5_w_glu, s5_b_glu, cv_w_dw, cv_b_dw, cv_ln_g, cv_ln_b, cv_w_pw, cv_b_pw, lru_w_conv, lru_b_conv, lru_w_r, lru_b_r, lru_w_i, lru_b_i, lru_lam, pool_w, pool_scale, w_out, norm_ffn_g, ffn_w_up, ffn_w_dw, ffn_b_dw, ffn_w_down, norm_final_g, loss_target, m_norm_mix_g, m_w_in, m_s5_lam_re, m_s5_lam_im, m_s5_log_step, m_s5_b_re, m_s5_b_im, m_s5_c_re, m_s5_c_im, m_s5_d, m_s5_w_glu, m_s5_b_glu, m_cv_w_dw, m_cv_b_dw, m_cv_ln_g, m_cv_ln_b, m_cv_w_pw, m_cv_b_pw, m_lru_w_conv, m_lru_b_conv, m_lru_w_r, m_lru_b_r, m_lru_w_i, m_lru_b_i, m_lru_lam, m_pool_w, m_pool_scale, m_w_out, m_norm_ffn_g, m_ffn_w_up, m_ffn_w_dw, m_ffn_b_dw, m_ffn_w_down, m_norm_final_g, v_norm_mix_g, v_w_in, v_s5_lam_re, v_s5_lam_im, v_s5_log_step, v_s5_b_re, v_s5_b_im, v_s5_c_re, v_s5_c_im, v_s5_d, v_s5_w_glu, v_s5_b_glu, v_cv_w_dw, v_cv_b_dw, v_cv_ln_g, v_cv_ln_b, v_cv_w_pw, v_cv_b_pw, v_lru_w_conv, v_lru_b_conv, v_lru_w_r, v_lru_b_r, v_lru_w_i, v_lru_b_i, v_lru_lam, v_pool_w, v_pool_scale, v_w_out, v_norm_ffn_g, v_ffn_w_up, v_ffn_w_dw, v_ffn_b_dw, v_ffn_w_down, v_norm_final_g)
    x, target = args[0], args[35]
    W = dict(zip(_NAMES, args[1:35]))
    Mo = dict(zip(_NAMES, args[36:70]))
    Vo = dict(zip(_NAMES, args[70:104]))
    x = x[0]
    target = target[0]
    T, D = x.shape
    depth = W['w_in'].shape[0]
    F = W['ffn_w_down'].shape[1] * N_DEV
    G, P, H = W['s5_b_re'].shape[1:]
    nh, hd, _ = W['lru_w_r'].shape[1:]
    me = _rank(_my_pos())

    gather, full = {}, {}

    def start_group(l, gi, after):
        shards = [(W[n][l].T if n in _TRANSPOSED else W[n][l]).astype(BF16) for n in _GROUPS[gi]]
        if (l, gi) == (0, 0):
            shards.append(_pack([W[n][k] for k in range(depth) for n in _COLSHARD]))
        gather[l, gi], tok = _gather_start(shards, "gather_%d%d" % (l, gi), after)
        return tok

    def land_weights(l, gi, after):
        got = _gather_wait(gather[l, gi], after)
        for n, g8 in zip(_GROUPS[gi], got):
            full[n, l] = g8.reshape(-1, g8.shape[-1])
        return got

    started = start_group(0, 0, None)[0, 0]
    got = land_weights(0, 0, W['norm_mix_g'][0] + started)
    started = start_group(0, 1, got[0])[0, 0]
    taps8 = got[-1]
    tap_shapes = [W[n][l].shape for l in range(depth) for n in _COLSHARD]
    taps = [_unshard_cols(jnp.stack(t)) for t in zip(*[_unpack(taps8[r], tap_shapes) for r in range(N_DEV)])]
    for l in range(depth):
        for j, n in enumerate(_COLSHARD):
            full[n, l] = taps[l * len(_COLSHARD) + j]

    saved = []
    for l in range(depth):
        g_mix = W['norm_mix_g'][l] + (0.0 if l else started)
        if l:
            land_weights(l, 0, x)
        s5p, s5p_vjp = jax.vjp(_s5_params, W['s5_lam_re'][l], W['s5_lam_im'][l], W['s5_log_step'][l], W['s5_b_re'][l], W['s5_b_im'][l])
        ar, ai, bre, bim = s5p
        cre, cim = _s5_blocks_out(W['s5_c_re'][l]), _s5_blocks_out(W['s5_c_im'][l])
        sp8, sp8_vjp = jax.vjp(lambda lam: LRU_C * jax.nn.softplus(-lam), W['lru_lam'][l])
        wr, wi = _head_blocks(W['lru_w_r'][l]), _head_blocks(W['lru_w_i'][l])
        c = dict(bre=bre.astype(BF16), bim=bim.astype(BF16), cre=cre.astype(BF16), cim=cim.astype(BF16),
                 cst_f=_s5_scan_consts(ar, ai, False), cst_b=_s5_scan_consts(ar, ai, True), sp8=sp8, wr=wr.astype(BF16),
                 wi=wi.astype(BF16), cvw=_pad_rows(full['cv_w_dw', l], 32), lruw=_pad_rows(full['lru_w_conv', l], 8),
                 ffw=_pad_rows(full['ffn_w_dw', l], 8), pw=W['pool_w'][l].astype(BF16), s5p_vjp=s5p_vjp, sp8_vjp=sp8_vjp, x0=x)
        h = _rmsnorm(x, g_mix, name="mix_norm")
        proj = _matmul(h, full['w_in', l], mode="nt", tm=1024, tn=512, tk=D, out_dtype=F32, name="in_proj")
        y0, sre, sim, yp, z = _s5_fwd(proj, 0, c['bre'], c['bim'], c['cre'], c['cim'], c['cst_f'], W['s5_d'][l],
                                     full['s5_w_glu', l], W['s5_b_glu'][l])
        y1, h1 = _cv_fwd(proj, 1, 2, c['cvw'], W['cv_b_dw'][l], W['cv_ln_g'][l], W['cv_ln_b'][l], full['cv_w_pw', l], W['cv_b_pw'][l])
        y2, xc, rr, ig, hprev = _lru_fwd(proj, 3, 4, c['lruw'], W['lru_b_conv'][l], c['wr'], W['lru_b_r'][l], c['wi'],
                                         W['lru_b_i'][l], sp8)
        y3 = _pool_fwd(proj, 5, c['pw'], W['pool_scale'][l])
        mixed = jnp.concatenate([y0, y1, y2, y3], axis=1)
        x1 = _matmul(mixed, full['w_out', l], mode="nn", tm=1024, tn=512, tk=2048, out_dtype=F32, name="out_proj", add=x)
        got = land_weights(l, 1, x1)
        g_ffn = W['norm_ffn_g'][l]
        if l + 1 < depth:
            g_ffn = g_ffn + start_group(l + 1, 0, got[0])[0, 0] + start_group(l + 1, 1, got[0])[0, 0]
        h2 = _rmsnorm(x1, g_ffn, name="ffn_norm")
        up = _matmul(h2, full['ffn_w_up', l], mode="nt", tm=2048, tn=256, tk=D, out_dtype=BF16, name="ffn_up")
        act = _ffn_mid_fwd(up, c['ffw'], W['ffn_b_dw'][l])
        x = _matmul(act, full['ffn_w_down', l], mode="nn", tm=1024, tn=512, tk=F, out_dtype=F32, name="ffn_down", add=x1)
        c.update(proj=proj, h=h, sre=sre, sim=sim, yp=yp, z=z, h1=h1, xc=xc, rr=rr, ig=ig, hprev=hprev, mixed=mixed, x1=x1,
                 up=up, h2=h2, act=act)
        saved.append(c)

    dx, dxb, dgf, se = _loss_head(x, W['norm_final_g'], target)
    loss = lax.psum(0.5 / D * jnp.sum(se), AXES)
    gsmall = {('norm_final_g', None): dgf.sum(0)}
    gmat, scatter = {}, {}

    def parts_of(m):
        return m.reshape(N_DEV, m.shape[0] // N_DEV, m.shape[1])

    for l in reversed(range(depth)):
        c = saved[l]
        dact = _matmul(dxb, full['ffn_w_down', l], mode="nt", tm=512, tn=F, tk=512, out_dtype=BF16, name="d_act")
        gmat['ffn_w_down', l] = _matmul(c['act'], dxb, mode="tn", tm=F, tn=512, tk=512, out_dtype=BF16, name="dw_down")
        dup, dffw, dffb = _ffn_mid_bwd(dact, c['up'], c['ffw'], W['ffn_b_dw'][l])
        gsmall['ffn_w_dw', l] = dffw.sum(1)[:3]
        gsmall['ffn_b_dw', l] = dffb.sum(0)
        dh2 = _matmul(dup, full['ffn_w_up', l], mode="nn", tm=1024, tn=512, tk=F, out_dtype=F32, name="d_h2")
        gmat['ffn_w_up', l] = _matmul(dup, c['h2'], mode="tn", tm=256, tn=D, tk=T, out_dtype=BF16, name="dw_up")
        scatter[l, 2], tok = _scatter_start([parts_of(gmat[n, l]) for n in _SCATTER_GROUPS[2]], "scatter_%d2" % l)
        dx1, dx1b, dg2 = _rms_bwd(dx, dh2, c['x1'], W['norm_ffn_g'][l] + tok[0, 0], name="ffn_norm_bwd")
        gsmall['norm_ffn_g', l] = dg2.sum(0)
        dmix = _matmul(dx1b, full['w_out', l], mode="nt", tm=1024, tn=512, tk=2048, out_dtype=F32, name="d_mixed")
        gmat['w_out', l] = _matmul(c['mixed'], dx1b, mode="tn", tm=512, tn=D, tk=T, out_dtype=BF16, name="dw_out")
        du, dbre, dbim, dcre, dcim, da, dwglu, v3 = _s5_bwd(dmix, 0, c['proj'], 0, c['sre'], c['sim'], c['yp'], c['z'], c['bre'],
                                                            c['bim'], c['cre'], c['cim'], c['cst_b'], W['s5_d'][l], full['s5_w_glu', l])
        da = da.sum(1)
        glr, gli, gls, gbr, gbi = c['s5p_vjp']((da[0], da[1], dbre, dbim))
        gsmall['s5_lam_re', l], gsmall['s5_lam_im', l], gsmall['s5_log_step', l] = glr, gli, gls
        gsmall['s5_b_re', l], gsmall['s5_b_im', l] = gbr, gbi
        gsmall['s5_c_re', l] = _s5_blocks_out_diag(dcre, G, H, P)
        gsmall['s5_c_im', l] = _s5_blocks_out_diag(dcim, G, H, P)
        gsmall['s5_b_glu', l], gsmall['s5_d', l] = v3[0].sum(0), v3[1].sum(0)
        gmat['s5_w_glu', l] = dwglu.astype(BF16)
        dv, dg, dwpw, dcvw, v5 = _cv_bwd(dmix, 1, c['proj'], 1, 2, c['h1'], c['cvw'], W['cv_ln_g'][l], W['cv_ln_b'][l], full['cv_w_pw', l])
        gmat['cv_w_pw', l] = dwpw.astype(BF16)
        gsmall['cv_w_dw', l] = dcvw.sum(1)[:31]
        gsmall['cv_b_pw', l], gsmall['cv_ln_g', l], gsmall['cv_ln_b', l], gsmall['cv_b_dw', l] = [v5[j].sum(0) for j in range(4)]
        scatter[l, 0], tok = _scatter_start([parts_of(gmat[n, l]) for n in _SCATTER_GROUPS[0]], "scatter_%d0" % l)
        tok = tok[0, 0]
        dlx, dlg, dwr, dwi, dlw, v4 = _lru_bwd(dmix, 2, c['proj'], 3, 4, c['xc'], c['rr'], c['ig'], c['hprev'], c['lruw'], c['wr'],
                                               c['wi'], c['sp8'] + tok)
        gsmall['lru_w_r', l], gsmall['lru_w_i', l] = _head_blocks_diag(dwr, nh, hd, hd), _head_blocks_diag(dwi, nh, hd, hd)
        gsmall['lru_w_conv', l] = dlw.sum(1)[:4]
        gsmall['lru_b_r', l], gsmall['lru_b_i', l] = v4[0].sum(0), v4[1].sum(0)
        gsmall['lru_lam', l] = c['sp8_vjp'](v4[2].sum(0))[0]
        gsmall['lru_b_conv', l] = v4[3].sum(0)
        dpx, dpw, dps = _pool_bwd(dmix, 3, c['proj'], 5, c['pw'], W['pool_scale'][l] + tok)
        gsmall['pool_w', l], gsmall['pool_scale', l] = dpw, dps.sum(0)
        dproj = jnp.concatenate([du, dv, dg, dlx, dlg, dpx], axis=1)
        dh = _matmul(dproj, full['w_in', l], mode="nn", tm=1024, tn=512, tk=dproj.shape[1], out_dtype=F32, name="d_h")
        gmat['w_in', l] = _matmul(dproj, c['h'], mode="tn", tm=512, tn=D, tk=T, out_dtype=BF16, name="dw_in")
        scatter[l, 1], tok = _scatter_start([parts_of(gmat[n, l]) for n in _SCATTER_GROUPS[1]], "scatter_%d1" % l)
        dx, dxb, dg1 = _rms_bwd(dx1, dh, c['x0'], W['norm_mix_g'][l] + tok[0, 0], name="mix_norm_bwd")
        gsmall['norm_mix_g', l] = dg1.sum(0)

    small_names = [n for n in _NAMES if n not in _MATRIX]
    small_split = ([n for n in small_names if n not in _SMALL_BF16], [n for n in small_names if n in _SMALL_BF16])
    small_packs, full_shapes = [], []
    for names, dtype in zip(small_split, (F32, BF16)):
        layers = [[None] if n == 'norm_final_g' else range(depth) for n in names]
        small_packs.append(_pack([gsmall[n, l] for n, ls in zip(names, layers) for l in ls], dtype))
        full_shapes.append([gsmall[n, None].shape if n == 'norm_final_g' else (depth,) + gsmall[n, 0].shape for n in names])
    small_state, tok = _gather_start(small_packs, "small_grads")

    stored_t = [n for n in _TRANSPOSED if W[n].shape[-1] % LANES]
    wmv = {n: [jnp.swapaxes(t[n], 1, 2) if n in stored_t else t[n] for t in (W, Mo, Vo)] for n in _MATRIX}
    stacked = {n: None for n in _MATRIX}

    def update_matrices(l, after):
        for gi in (2, 0, 1):
            for n, p8 in zip(_SCATTER_GROUPS[gi], _scatter_wait(scatter[l, gi], after)):
                g = _sum8(p8, name="sum_%s_%d" % (n, l)).T if n in _TRANSPOSED and n not in stored_t else p8
                stacked[n] = _adamw_layer(g, *wmv[n], l, stacked[n], name="adamw_%s_%d" % (n, l))

    after = tok
    for l in reversed(range(depth)):
        update_matrices(l, after)
        after = sum(stacked[n][1][l, 0, 0] for n in _MATRIX).reshape(1, 1)
    after_matrices = after
    gs = {}
    for names, shapes, small8 in zip(small_split, full_shapes, _gather_wait(small_state, after_matrices)):
        gs.update(zip(names, _unpack(_sum8(small8, name="sum_small_" + str(small8.dtype)), shapes)))
    for n in _COLSHARD:
        cw = W[n].shape[-1]
        gs[n] = lax.dynamic_slice_in_dim(gs[n], me * cw, cw, axis=gs[n].ndim - 1)
    upd = _adamw_many([W[n] for n in small_names], [gs[n] for n in small_names], [Mo[n] for n in small_names],
                      [Vo[n] for n in small_names], name="adamw_small")
    out = {n: (gs[n], upd[0][j], upd[1][j], upd[2][j]) for j, n in enumerate(small_names)}
    for n in _MATRIX:
        out[n] = tuple(jnp.swapaxes(t, 1, 2) for t in stacked[n]) if n in stored_t else tuple(stacked[n])
    return (loss, dx[None]) + tuple(out[n][q] for q in range(4) for n in _NAMES)
```

```python
import math

import jax
import jax.numpy as jnp
from jax import lax
from jax.experimental import pallas as pl
from jax.experimental.pallas import tpu as pltpu

F32 = jnp.float32
BF16 = jnp.bfloat16
N_DEV = 8
AXES = ("x", "y", "c")
EPS = 1e-6
LRU_C = 8.0
POOL_WINDOWS = (2, 4, 8, 16)
CONV_HALO = 32
SMALL_HALO = 8
FFN_HALO = 16
POOL_HALO = 16
LANES = 128
ADAM_LR, ADAM_B1, ADAM_B2, ADAM_EPS, ADAM_WD, ADAM_STEP = 0.001, 0.9, 0.999, 1e-08, 0.01, 10
_GELU_K = math.sqrt(2.0 / math.pi)
ARB = pltpu.CompilerParams(dimension_semantics=("arbitrary",))


def _pick(n, pref, mult):
    best = None
    for t in range(mult, min(n, pref) + 1, mult):
        if n % t == 0:
            best = t
    return best if best is not None else n


def _sigmoid(x):
    return 1.0 / (1.0 + jnp.exp(-x))


def _gelu(x):
    return 0.5 * x * (1.0 + jnp.tanh(_GELU_K * (x + 0.044715 * x * x * x)))


def _gelu_grad(x):
    t = jnp.tanh(_GELU_K * (x + 0.044715 * x * x * x))
    return 0.5 * (1.0 + t) + 0.5 * x * (1.0 - t * t) * _GELU_K * (1.0 + 3.0 * 0.044715 * x * x)


def _gelu_and_grad(x):
    t = jnp.tanh(_GELU_K * (x + 0.044715 * x * x * x))
    half = 0.5 * (1.0 + t)
    return x * half, half + 0.5 * x * (1.0 - t * t) * _GELU_K * (1.0 + 3.0 * 0.044715 * x * x)


def _neg_expm1(x):
    series = -x * (1.0 + x * (0.5 + x * (1.0 / 6.0 + x * (1.0 / 24.0 + x * (1.0 / 120.0 + x * (1.0 / 720.0))))))
    return jnp.where(jnp.abs(x) < 0.25, series, 1.0 - jnp.exp(x))


def _fold8(x):
    return x.reshape(x.shape[0] // 8, 8, x.shape[1]).sum(axis=0)


def _dot(a, b):
    return jnp.dot(a.astype(BF16), b.astype(BF16), preferred_element_type=F32)


def _dot_nt(a, b):
    return lax.dot_general(a.astype(BF16), b.astype(BF16), (((1,), (1,)), ((), ())), preferred_element_type=F32)


def _dot_tn(a, b):
    return lax.dot_general(a.astype(BF16), b.astype(BF16), (((0,), (0,)), ((), ())), preferred_element_type=F32)


def _full(shape):
    nd = len(shape)
    return pl.BlockSpec(shape, lambda *i: (0,) * nd)


def _matmul(a, b, *, mode, tm, tn, tk, out_dtype, name, add=None):
    if mode == "nn":
        (M, K), N = a.shape, b.shape[1]
    elif mode == "nt":
        (M, K), N = a.shape, b.shape[0]
    else:
        (K, M), N = a.shape, b.shape[1]
    tm, tn, tk = _pick(M, tm, 8), _pick(N, tn, 128), _pick(K, tk, 128)
    if mode == "tn":
        tm = _pick(M, tm, 128)
    nk = K // tk
    a_spec = {"nn": pl.BlockSpec((tm, tk), lambda i, j, k: (i, k)), "nt": pl.BlockSpec((tm, tk), lambda i, j, k: (i, k)),
              "tn": pl.BlockSpec((tk, tm), lambda i, j, k: (k, i))}[mode]
    b_spec = {"nn": pl.BlockSpec((tk, tn), lambda i, j, k: (k, j)), "nt": pl.BlockSpec((tn, tk), lambda i, j, k: (j, k)),
              "tn": pl.BlockSpec((tk, tn), lambda i, j, k: (k, j))}[mode]
    o_spec = pl.BlockSpec((tm, tn), lambda i, j, k: (i, j))
    dot = {"nn": _dot, "nt": _dot_nt, "tn": _dot_tn}[mode]
    has_add = add is not None

    def body(*refs):
        if has_add:
            a_ref, b_ref, add_ref, o_ref, acc_ref = refs
        else:
            a_ref, b_ref, o_ref, acc_ref = refs
        if nk == 1:
            part = dot(a_ref[...], b_ref[...])
            o_ref[...] = (part + add_ref[...] if has_add else part).astype(out_dtype)
            return
        k = pl.program_id(2)

        @pl.when(k == 0)
        def _():
            acc_ref[...] = jnp.zeros_like(acc_ref)

        acc_ref[...] += dot(a_ref[...], b_ref[...])

        @pl.when(k == nk - 1)
        def _():
            r = acc_ref[...]
            if has_add:
                r = r + add_ref[...]
            o_ref[...] = r.astype(out_dtype)

    ins = [a, b] + ([add] if has_add else [])
    specs = [a_spec, b_spec] + ([o_spec] if has_add else [])
    return pl.pallas_call(
        body, name=name, grid=(M // tm, N // tn, nk), in_specs=specs, out_specs=o_spec,
        out_shape=jax.ShapeDtypeStruct((M, N), out_dtype),
        scratch_shapes=[pltpu.VMEM((tm, tn) if nk > 1 else (8, LANES), F32)],
        compiler_params=pltpu.CompilerParams(dimension_semantics=("parallel", "parallel", "arbitrary")),
    )(*ins)


def _rmsnorm(x, g, *, name):
    T, D = x.shape
    tc = _pick(T, 512, 16)

    def body(x_ref, g_ref, h_ref):
        def rows16(n, _):
            rows = pl.ds(pl.multiple_of(n * 16, 16), 16)
            xv = x_ref[rows, :]
            r = lax.rsqrt(jnp.mean(xv * xv, axis=-1, keepdims=True) + EPS)
            h_ref[rows, :] = (xv * r * g_ref[...]).astype(BF16)
            return 0

        lax.fori_loop(0, tc // 16, rows16, 0, unroll=4)

    row = pl.BlockSpec((tc, D), lambda i: (i, 0))
    return pl.pallas_call(
        body, name=name, grid=(T // tc,), in_specs=[row, _full((1, D))], out_specs=row,
        out_shape=jax.ShapeDtypeStruct((T, D), BF16), compiler_params=pltpu.CompilerParams(dimension_semantics=("parallel",)),
    )(x, g.reshape(1, D))


def _rms_bwd(dres, dh, x, g, *, name):
    T, D = x.shape
    tc = _pick(T, 512, 16)

    def body(dres_ref, dh_ref, x_ref, g_ref, dx_ref, dxb_ref, dg_ref):
        @pl.when(pl.program_id(0) == 0)
        def _():
            dg_ref[...] = jnp.zeros_like(dg_ref)

        def rows16(n, _):
            rows = pl.ds(pl.multiple_of(n * 16, 16), 16)
            xv, dhv = x_ref[rows, :], dh_ref[rows, :]
            r = lax.rsqrt(jnp.mean(xv * xv, axis=-1, keepdims=True) + EPS)
            dg_ref[...] += _fold8(dhv * xv * r)
            dyg = dhv * g_ref[...]
            dx = dres_ref[rows, :] + r * dyg - xv * (r * r * r) * jnp.mean(dyg * xv, axis=-1, keepdims=True)
            dx_ref[rows, :] = dx
            dxb_ref[rows, :] = dx.astype(BF16)
            return 0

        lax.fori_loop(0, tc // 16, rows16, 0, unroll=4)

    row = pl.BlockSpec((tc, D), lambda i: (i, 0))
    return pl.pallas_call(
        body, name=name, grid=(T // tc,), in_specs=[row, row, row, _full((1, D))],
        out_specs=[row, row, _full((8, D))],
        out_shape=[jax.ShapeDtypeStruct((T, D), F32), jax.ShapeDtypeStruct((T, D), BF16), jax.ShapeDtypeStruct((8, D), F32)],
        compiler_params=ARB,
    )(dres, dh, x, g.reshape(1, D))


def _loss_head(x, g, target):
    T, D = x.shape
    tc = _pick(T, 256, 8)

    def body(x_ref, g_ref, t_ref, dx_ref, dxb_ref, dg_ref, se_ref):
        @pl.when(pl.program_id(0) == 0)
        def _():
            dg_ref[...] = jnp.zeros_like(dg_ref)
            se_ref[...] = jnp.zeros_like(se_ref)

        def rows16(n, _):
            rows = pl.ds(pl.multiple_of(n * 16, 16), 16)
            xv = x_ref[rows, :]
            r = lax.rsqrt(jnp.mean(xv * xv, axis=-1, keepdims=True) + EPS)
            gv = g_ref[...]
            err = xv * r * gv - t_ref[rows, :]
            se_ref[...] += _fold8(err * err)
            dy = err * (1.0 / D)
            dg_ref[...] += _fold8(dy * xv * r)
            dyg = dy * gv
            dx = r * dyg - xv * (r * r * r) * jnp.mean(dyg * xv, axis=-1, keepdims=True)
            dx_ref[rows, :] = dx
            dxb_ref[rows, :] = dx.astype(BF16)
            return 0

        lax.fori_loop(0, tc // 16, rows16, 0, unroll=4)

    row = pl.BlockSpec((tc, D), lambda i: (i, 0))
    return pl.pallas_call(
        body, name="loss_head", grid=(T // tc,), in_specs=[row, _full((1, D)), row],
        out_specs=[row, row, _full((8, D)), _full((8, D))],
        out_shape=[jax.ShapeDtypeStruct((T, D), F32), jax.ShapeDtypeStruct((T, D), BF16),
                   jax.ShapeDtypeStruct((8, D), F32), jax.ShapeDtypeStruct((8, D), F32)],
        compiler_params=ARB,
    )(x, g.reshape(1, D), target)


def _s5_scan_consts(ar, ai, reverse):
    if reverse:
        ai = -ai
    pr, pi_ = [ar], [ai]
    for _ in range(7):
        pr, pi_ = pr + [pr[-1] * ar - pi_[-1] * ai], pi_ + [pr[-1] * ai + pi_[-1] * ar]
    rows = jnp.arange(8)[:, None]
    planes = []
    for d in (1, 2, 4):
        mask = (rows <= 7 - d) if reverse else (rows >= d)
        planes += [jnp.where(mask, pr[d - 1][None, :], 0.0), jnp.where(mask, pi_[d - 1][None, :], 0.0)]
    order = list(range(7, -1, -1)) if reverse else list(range(8))
    planes += [jnp.stack([pr[k] for k in order]), jnp.stack([pi_[k] for k in order])]
    return jnp.stack(planes).astype(F32)


def _s5_block_scan(xr_ref, xi_ref, cst_ref, car_ref, nblk, reverse, lb, extra=None):
    ns = xr_ref.shape[1]
    crow = 0 if reverse else 7

    def blk(i, _):
        ib = (nblk - 1 - i) if reverse else i
        r0 = pl.multiple_of(ib * 8, 8)
        for k in range(ns // lb):
            sl = slice(k * lb, (k + 1) * lb)
            xr, xi = xr_ref[pl.ds(r0, 8), sl], xi_ref[pl.ds(r0, 8), sl]
            for n, d in enumerate((1, 2, 4)):
                ar, ai = cst_ref[2 * n, :, sl], cst_ref[2 * n + 1, :, sl]
                sh = (8 - d) if reverse else d
                rr, ri = pltpu.roll(xr, sh, 0), pltpu.roll(xi, sh, 0)
                xr, xi = xr + ar * rr - ai * ri, xi + ar * ri + ai * rr
            pr, pi_ = cst_ref[6, :, sl], cst_ref[7, :, sl]
            cr, ci = car_ref[0, crow:crow + 1, sl], car_ref[1, crow:crow + 1, sl]
            xr, xi = xr + pr * cr - pi_ * ci, xi + pr * ci + pi_ * cr
            xr_ref[pl.ds(r0, 8), sl] = xr
            xi_ref[pl.ds(r0, 8), sl] = xi
            if extra is not None:
                extra(r0, sl, xr, xi, cr, ci)
            car_ref[0, :, sl] = xr
            car_ref[1, :, sl] = xi
        return 0

    lax.fori_loop(0, nblk, blk, 0)


def _s5_fwd(proj, col, bre, bim, cre, cim, cst, dvec, wglu, bglu):
    T = proj.shape[0]
    nsb, cw, sw = bre.shape
    M, NS = nsb * cw, nsb * sw
    tc = _pick(T, 512, 8)

    def body(u_ref, bre_ref, bim_ref, cre_ref, cim_ref, cst_ref, d_ref, wg_ref, bg_ref,
             y_ref, sre_ref, sim_ref, yp_ref, z_ref, car_ref):
        @pl.when(pl.program_id(0) == 0)
        def _():
            car_ref[...] = jnp.zeros_like(car_ref)

        u = u_ref[...]
        ub = u.astype(BF16)
        for k in range(nsb):
            sre_ref[:, k * sw:(k + 1) * sw] = _dot(ub[:, k * cw:(k + 1) * cw], bre_ref[k])
            sim_ref[:, k * sw:(k + 1) * sw] = _dot(ub[:, k * cw:(k + 1) * cw], bim_ref[k])
        _s5_block_scan(sre_ref, sim_ref, cst_ref, car_ref, tc // 8, False, sw)
        for k in range(nsb):
            yk = _dot(sre_ref[:, k * sw:(k + 1) * sw], cre_ref[k]) - _dot(sim_ref[:, k * sw:(k + 1) * sw], cim_ref[k])
            yp_ref[:, k * cw:(k + 1) * cw] = yk + d_ref[:, k * cw:(k + 1) * cw] * u[:, k * cw:(k + 1) * cw]
        gl = _gelu(yp_ref[...])
        z = _dot(gl, wg_ref[...]) + bg_ref[...]
        z_ref[...] = z
        y_ref[...] = (gl * _sigmoid(z)).astype(BF16)

    rowm = pl.BlockSpec((tc, M), lambda i: (i, 0))
    rows = pl.BlockSpec((tc, NS), lambda i: (i, 0))
    return pl.pallas_call(
        body, name="s5_fwd", grid=(T // tc,),
        in_specs=[pl.BlockSpec((tc, M), lambda i: (i, col)), _full(bre.shape), _full(bim.shape), _full(cre.shape),
                  _full(cim.shape), _full(cst.shape), _full((1, M)), _full((M, M)), _full((1, M))],
        out_specs=[rowm, rows, rows, rowm, rowm],
        out_shape=[jax.ShapeDtypeStruct((T, M), BF16), jax.ShapeDtypeStruct((T, NS), F32), jax.ShapeDtypeStruct((T, NS), F32),
                   jax.ShapeDtypeStruct((T, M), F32), jax.ShapeDtypeStruct((T, M), F32)],
        scratch_shapes=[pltpu.VMEM((2, 8, NS), F32)], compiler_params=ARB,
    )(proj, bre, bim, cre, cim, cst, dvec.reshape(1, M), wglu, bglu.reshape(1, M))


def _s5_bwd(dmix, dcol, proj, col, sre, sim, yp, z, bre, bim, cre, cim, cst, dvec, wglu):
    T = proj.shape[0]
    nsb, cw, sw = bre.shape
    M, NS = nsb * cw, nsb * sw
    tc = _pick(T, 512, 8)
    nc = T // tc

    def body(do_ref, u_ref, sre_ref, sim_ref, yp_ref, z_ref, bre_ref, bim_ref, cre_ref, cim_ref, cst_ref, d_ref, wg_ref,
             du_ref, dbre_ref, dbim_ref, dcre_ref, dcim_ref, da_ref, dwg_ref, vec_ref, gre_ref, gim_ref, car_ref):
        @pl.when(pl.program_id(0) == 0)
        def _():
            for r in (dbre_ref, dbim_ref, dcre_ref, dcim_ref, da_ref, dwg_ref, vec_ref, car_ref):
                r[...] = jnp.zeros_like(r)

        u, do, ypv = u_ref[...], do_ref[...], yp_ref[...]
        gl = _gelu(ypv)
        sg = _sigmoid(z_ref[...])
        dz = do * gl * sg * (1.0 - sg)
        dgl = do * sg + _dot_nt(dz, wg_ref[...])
        dwg_ref[...] += _dot_tn(gl, dz)
        vec_ref[0] += _fold8(dz)
        dy = dgl * _gelu_grad(ypv)
        vec_ref[1] += _fold8(dy * u)
        dyb = dy.astype(BF16)
        for k in range(nsb):
            cs, ss = slice(k * cw, (k + 1) * cw), slice(k * sw, (k + 1) * sw)
            gre_ref[:, ss] = _dot_nt(dyb[:, cs], cre_ref[k])
            gim_ref[:, ss] = -_dot_nt(dyb[:, cs], cim_ref[k])
            dcre_ref[k] += _dot_tn(sre_ref[:, ss], dyb[:, cs])
            dcim_ref[k] -= _dot_tn(sim_ref[:, ss], dyb[:, cs])

        row = lax.broadcasted_iota(jnp.int32, (8, sw), 0)

        def extra(r0, sl, gr, gi, cr, ci):
            nr = jnp.where(row == 7, cr, pltpu.roll(gr, 7, 0))
            ni = jnp.where(row == 7, ci, pltpu.roll(gi, 7, 0))
            sr, si = sre_ref[pl.ds(r0, 8), sl], sim_ref[pl.ds(r0, 8), sl]
            da_ref[0, :, sl] += nr * sr + ni * si
            da_ref[1, :, sl] += ni * sr - nr * si

        _s5_block_scan(gre_ref, gim_ref, cst_ref, car_ref, tc // 8, True, sw, extra)
        ub = u.astype(BF16)
        for k in range(nsb):
            cs, ss = slice(k * cw, (k + 1) * cw), slice(k * sw, (k + 1) * sw)
            gr, gi = gre_ref[:, ss].astype(BF16), gim_ref[:, ss].astype(BF16)
            duk = dy[:, cs] * d_ref[:, cs] + _dot_nt(gr, bre_ref[k]) + _dot_nt(gi, bim_ref[k])
            du_ref[:, cs] = duk.astype(BF16)
            dbre_ref[k] += _dot_tn(ub[:, cs], gr)
            dbim_ref[k] += _dot_tn(ub[:, cs], gi)

    def rowm(c):
        return pl.BlockSpec((tc, M), lambda i: (nc - 1 - i, c))
    rows = pl.BlockSpec((tc, NS), lambda i: (nc - 1 - i, 0))
    return pl.pallas_call(
        body, name="s5_bwd", grid=(nc,),
        in_specs=[rowm(dcol), rowm(col), rows, rows, rowm(0), rowm(0), _full(bre.shape), _full(bim.shape), _full(cre.shape),
                  _full(cim.shape), _full(cst.shape), _full((1, M)), _full((M, M))],
        out_specs=[rowm(0), _full(bre.shape), _full(bim.shape), _full(cre.shape), _full(cim.shape), _full((2, 8, NS)),
                   _full((M, M)), _full((3, 8, M))],
        out_shape=[jax.ShapeDtypeStruct((T, M), BF16), jax.ShapeDtypeStruct(bre.shape, F32), jax.ShapeDtypeStruct(bim.shape, F32),
                   jax.ShapeDtypeStruct(cre.shape, F32), jax.ShapeDtypeStruct(cim.shape, F32), jax.ShapeDtypeStruct((2, 8, NS), F32),
                   jax.ShapeDtypeStruct((M, M), F32), jax.ShapeDtypeStruct((3, 8, M), F32)],
        scratch_shapes=[pltpu.VMEM((tc, NS), F32), pltpu.VMEM((tc, NS), F32), pltpu.VMEM((2, 8, NS), F32)],
        compiler_params=ARB,
    )(dmix, proj, sre, sim, yp, z, bre, bim, cre, cim, cst, dvec.reshape(1, M), wglu)


def _ln_silu(h1, lg, lb):
    mu = jnp.mean(h1, axis=-1, keepdims=True)
    xc = h1 - mu
    rstd = lax.rsqrt(jnp.mean(xc * xc, axis=-1, keepdims=True) + EPS)
    xhat = xc * rstd
    h2 = xhat * lg + lb
    return xhat, rstd, h2, _sigmoid(h2)


def _shifted_copies(buf_ref, sh_ref, cs, n):
    for s in range(1, 8):
        sh_ref[s, 0:n, :] = buf_ref[pl.ds(s, n), cs]


def _tap(buf_ref, sh_ref, cs, off, r0, rows):
    q, s = divmod(off, 8)
    return buf_ref[pl.ds(r0 + 8 * q, rows), cs] if s == 0 else sh_ref[s, pl.ds(r0 + 8 * q, rows), :]


def _cv_fwd(proj, vcol, gcol, wdw, bdw, lg, lb, wpw, bpw):
    T = proj.shape[0]
    M = wpw.shape[0]
    tc = _pick(T, 256, 8)
    rb = _pick(tc, 64, 8)
    H, K = CONV_HALO, 31

    def body(v_ref, g_ref, wdw_ref, bdw_ref, lg_ref, lb_ref, wpw_ref, bpw_ref, y_ref, h1_ref, buf_ref, sh_ref):
        @pl.when(pl.program_id(0) == 0)
        def _():
            buf_ref[0:H, :] = jnp.zeros((H, M), F32)

        buf_ref[H:H + tc, :] = v_ref[...] * _sigmoid(g_ref[...])

        def strip(s, _):
            cs = pl.ds(pl.multiple_of(s * LANES, LANES), LANES)
            _shifted_copies(buf_ref, sh_ref, cs, tc + H - 8)
            for r0 in range(0, tc, rb):
                acc = jnp.zeros((rb, LANES), F32) + bdw_ref[:, cs]
                for k in range(K):
                    acc = acc + wdw_ref[k:k + 1, cs] * _tap(buf_ref, sh_ref, cs, H - (K - 1) + k, r0, rb)
                h1_ref[pl.ds(r0, rb), cs] = acc
            return 0

        lax.fori_loop(0, M // LANES, strip, 0)
        _, _, h2, sg = _ln_silu(h1_ref[...], lg_ref[...], lb_ref[...])
        y_ref[...] = (_dot(h2 * sg, wpw_ref[...]) + bpw_ref[...]).astype(BF16)
        buf_ref[0:H, :] = buf_ref[tc:tc + H, :]

    def rowm(c):
        return pl.BlockSpec((tc, M), lambda i: (i, c))
    vec = _full((1, M))
    return pl.pallas_call(
        body, name="cv_fwd", grid=(T // tc,),
        in_specs=[rowm(vcol), rowm(gcol), _full((32, M)), vec, vec, vec, _full((M, M)), vec],
        out_specs=[rowm(0), rowm(0)],
        out_shape=[jax.ShapeDtypeStruct((T, M), BF16), jax.ShapeDtypeStruct((T, M), F32)],
        scratch_shapes=[pltpu.VMEM((tc + H, M), F32), pltpu.VMEM((8, tc + H, LANES), F32)], compiler_params=ARB,
    )(proj, proj, wdw, bdw.reshape(1, M), lg.reshape(1, M), lb.reshape(1, M), wpw, bpw.reshape(1, M))


def _cv_bwd(dmix, dcol, proj, vcol, gcol, h1, wdw, lg, lb, wpw):
    T = proj.shape[0]
    M = wpw.shape[0]
    tc = _pick(T, 256, 8)
    nc = T // tc
    rb = _pick(tc, 32, 16)
    H, K = CONV_HALO, 31
    hb = tc // H

    def body(do_ref, v_ref, g_ref, vh_ref, gh_ref, h1_ref, wdw_ref, lg_ref, lb_ref, wpw_ref,
             dv_ref, dg_ref, dwpw_ref, dwdw_ref, vec_ref, hbuf_ref, dbuf_ref, hsh_ref, dsh_ref):
        i = pl.program_id(0)

        @pl.when(i == 0)
        def _():
            for r in (dwpw_ref, dwdw_ref, vec_ref):
                r[...] = jnp.zeros_like(r)
            dbuf_ref[tc:tc + H, :] = jnp.zeros((H, M), F32)

        halo = vh_ref[...] * _sigmoid(gh_ref[...])
        hbuf_ref[0:H, :] = jnp.where(i == nc - 1, 0.0, halo)
        hbuf_ref[H:H + tc, :] = v_ref[...] * _sigmoid(g_ref[...])
        do = do_ref[...]
        xhat, rstd, h2, sg = _ln_silu(h1_ref[...], lg_ref[...], lb_ref[...])
        dwpw_ref[...] += _dot_tn(h2 * sg, do)
        vec_ref[0] += _fold8(do)
        dh2 = _dot_nt(do, wpw_ref[...]) * (sg * (1.0 + h2 * (1.0 - sg)))
        vec_ref[1] += _fold8(dh2 * xhat)
        vec_ref[2] += _fold8(dh2)
        dxh = dh2 * lg_ref[...]
        dh1 = rstd * (dxh - jnp.mean(dxh, axis=-1, keepdims=True) - xhat * jnp.mean(dxh * xhat, axis=-1, keepdims=True))
        vec_ref[3] += _fold8(dh1)
        dbuf_ref[0:tc, :] = dh1

        def strip(s, _):
            cs = pl.ds(pl.multiple_of(s * LANES, LANES), LANES)
            _shifted_copies(dbuf_ref, dsh_ref, cs, tc + H - 8)
            _shifted_copies(hbuf_ref, hsh_ref, cs, tc + H - 8)
            for r0 in range(0, tc, rb):
                d1 = dbuf_ref[pl.ds(r0, rb), cs]
                dh0 = jnp.zeros((rb, LANES), F32)
                for k in range(K):
                    dh0 = dh0 + wdw_ref[k:k + 1, cs] * _tap(dbuf_ref, dsh_ref, cs, K - 1 - k, r0, rb)
                    dwdw_ref[k, :, cs] += _fold8(d1 * _tap(hbuf_ref, hsh_ref, cs, H - (K - 1) + k, r0, rb))
                sg0 = _sigmoid(g_ref[pl.ds(r0, rb), cs])
                dv_ref[pl.ds(r0, rb), cs] = (dh0 * sg0).astype(BF16)
                dg_ref[pl.ds(r0, rb), cs] = (dh0 * v_ref[pl.ds(r0, rb), cs] * sg0 * (1.0 - sg0)).astype(BF16)
            return 0

        lax.fori_loop(0, M // LANES, strip, 0)
        dbuf_ref[tc:tc + H, :] = dbuf_ref[0:H, :]

    def rowm(c):
        return pl.BlockSpec((tc, M), lambda i: (nc - 1 - i, c))

    def halo(c):
        return pl.BlockSpec((H, M), lambda i: (jnp.maximum((nc - 1 - i) * hb - 1, 0), c))
    vec = _full((1, M))
    return pl.pallas_call(
        body, name="cv_bwd", grid=(nc,),
        in_specs=[rowm(dcol), rowm(vcol), rowm(gcol), halo(vcol), halo(gcol), rowm(0), _full((32, M)), vec, vec, _full((M, M))],
        out_specs=[rowm(0), rowm(0), _full((M, M)), _full((32, 8, M)), _full((5, 8, M))],
        out_shape=[jax.ShapeDtypeStruct((T, M), BF16), jax.ShapeDtypeStruct((T, M), BF16), jax.ShapeDtypeStruct((M, M), F32),
                   jax.ShapeDtypeStruct((32, 8, M), F32), jax.ShapeDtypeStruct((5, 8, M), F32)],
        scratch_shapes=[pltpu.VMEM((tc + H, M), F32), pltpu.VMEM((tc + H, M), F32), pltpu.VMEM((8, tc + H, LANES), F32),
                        pltpu.VMEM((8, tc + H, LANES), F32)], compiler_params=ARB,
    )(dmix, proj, proj, proj, proj, h1, wdw, lg.reshape(1, M), lb.reshape(1, M), wpw)


def _lru_gates(xc, r, ig, sp8):
    la = -r * sp8
    a = jnp.exp(la)
    mult = jnp.sqrt(_neg_expm1(2.0 * la))
    return a, mult, mult * (ig * xc)


def _lru_fwd(proj, xcol, gcol, wcv, bcv, wr, br, wi, bi, sp8):
    T = proj.shape[0]
    M = wr.shape[0]
    tc = _pick(T, 256, 8)
    H, K = SMALL_HALO, 4

    def body(x_ref, g_ref, wcv_ref, bcv_ref, wr_ref, br_ref, wi_ref, bi_ref, sp_ref,
             y_ref, xc_ref, r_ref, i_ref, hp_ref, buf_ref, a_ref, car_ref):
        @pl.when(pl.program_id(0) == 0)
        def _():
            buf_ref[0:H, :] = jnp.zeros((H, M), F32)
            car_ref[...] = jnp.zeros_like(car_ref)

        buf_ref[H:H + tc, :] = x_ref[...]
        xc = jnp.zeros((tc, M), F32) + bcv_ref[...]
        for k in range(K):
            xc = xc + wcv_ref[k:k + 1, :] * buf_ref[pl.ds(H - (K - 1) + k, tc), :]
        buf_ref[0:H, :] = buf_ref[tc:tc + H, :]
        r = _sigmoid(_dot(xc, wr_ref[...]) + br_ref[...])
        ig = _sigmoid(_dot(xc, wi_ref[...]) + bi_ref[...])
        xc_ref[...] = xc
        r_ref[...] = r
        i_ref[...] = ig
        a, _, bt = _lru_gates(xc, r, ig, sp_ref[...])
        a_ref[...] = a
        hp_ref[...] = bt
        row = lax.broadcasted_iota(jnp.int32, (8, M), 0)

        def blk(ib, _):
            r0 = pl.multiple_of(ib * 8, 8)
            av, bv = a_ref[pl.ds(r0, 8), :], hp_ref[pl.ds(r0, 8), :]
            for d in (1, 2, 4):
                m = row >= d
                bv = jnp.where(m, bv + av * pltpu.roll(bv, d, 0), bv)
                av = jnp.where(m, av * pltpu.roll(av, d, 0), av)
            cr = car_ref[7:8, :]
            h = bv + av * cr
            a_ref[pl.ds(r0, 8), :] = h
            hp_ref[pl.ds(r0, 8), :] = jnp.where(row == 0, cr, pltpu.roll(h, 1, 0))
            car_ref[...] = h
            return 0

        lax.fori_loop(0, tc // 8, blk, 0)
        y_ref[...] = (a_ref[...] * _gelu(g_ref[...])).astype(BF16)

    def rowm(c):
        return pl.BlockSpec((tc, M), lambda i: (i, c))
    vec = _full((1, M))
    return pl.pallas_call(
        body, name="lru_fwd", grid=(T // tc,),
        in_specs=[rowm(xcol), rowm(gcol), _full((8, M)), vec, _full((M, M)), vec, _full((M, M)), vec, vec],
        out_specs=[rowm(0)] * 5,
        out_shape=[jax.ShapeDtypeStruct((T, M), BF16)] + [jax.ShapeDtypeStruct((T, M), F32)] * 4,
        scratch_shapes=[pltpu.VMEM((tc + H, M), F32), pltpu.VMEM((tc, M), F32), pltpu.VMEM((8, M), F32)], compiler_params=ARB,
    )(proj, proj, wcv, bcv.reshape(1, M), wr, br.reshape(1, M), wi, bi.reshape(1, M), sp8.reshape(1, M))


def _lru_bwd(dmix, dcol, proj, xcol, gcol, xc, r, ig, hprev, wcv, wr, wi, sp8):
    T = proj.shape[0]
    M = wr.shape[0]
    tc = _pick(T, 256, 8)
    nc = T // tc
    H, K = SMALL_HALO, 4
    hb = tc // H

    def body(do_ref, x_ref, g_ref, xh_ref, xc_ref, r_ref, i_ref, hp_ref, wcv_ref, wr_ref, wi_ref, sp_ref,
             dx_ref, dg_ref, dwr_ref, dwi_ref, dwcv_ref, vec_ref, xbuf_ref, dbuf_ref, a_ref, gs_ref, car_ref):
        i = pl.program_id(0)

        @pl.when(i == 0)
        def _():
            for rf in (dwr_ref, dwi_ref, dwcv_ref, vec_ref, car_ref):
                rf[...] = jnp.zeros_like(rf)
            dbuf_ref[tc:tc + H, :] = jnp.zeros((H, M), F32)

        xcv, rv, iv, hpv, gv, do = xc_ref[...], r_ref[...], i_ref[...], hp_ref[...], g_ref[...], do_ref[...]
        sp = sp_ref[...]
        a, mult, bt = _lru_gates(xcv, rv, iv, sp)
        h = a * hpv + bt
        ge = _gelu(gv)
        dg_ref[...] = (do * h * _gelu_grad(gv)).astype(BF16)
        a_ref[...] = a
        gs_ref[...] = do * ge
        row = lax.broadcasted_iota(jnp.int32, (8, M), 0)

        def blk(n, _):
            r0 = pl.multiple_of((tc // 8 - 1 - n) * 8, 8)
            ablk = a_ref[pl.ds(r0, 8), :]
            av = jnp.where(row == 7, car_ref[0, 0:1, :], pltpu.roll(ablk, 7, 0))
            bv = gs_ref[pl.ds(r0, 8), :]
            for d in (1, 2, 4):
                m = row <= 7 - d
                bv = jnp.where(m, bv + av * pltpu.roll(bv, 8 - d, 0), bv)
                av = jnp.where(m, av * pltpu.roll(av, 8 - d, 0), av)
            gsv = bv + av * car_ref[1, 0:1, :]
            gs_ref[pl.ds(r0, 8), :] = gsv
            car_ref[0] = ablk
            car_ref[1] = gsv
            return 0

        lax.fori_loop(0, tc // 8, blk, 0)
        gs = gs_ref[...]
        dixc = gs * mult
        dla = gs * hpv * a - gs * (iv * xcv) * (a * a) / mult
        vec_ref[2] += _fold8(-dla * rv)
        dpr = (-dla * sp) * rv * (1.0 - rv)
        dpi = (dixc * xcv) * iv * (1.0 - iv)
        vec_ref[0] += _fold8(dpr)
        vec_ref[1] += _fold8(dpi)
        dwr_ref[...] += _dot_tn(xcv, dpr)
        dwi_ref[...] += _dot_tn(xcv, dpi)
        dxc = dixc * iv + _dot_nt(dpr, wr_ref[...]) + _dot_nt(dpi, wi_ref[...])
        vec_ref[3] += _fold8(dxc)
        xbuf_ref[0:H, :] = jnp.where(i == nc - 1, 0.0, xh_ref[...])
        xbuf_ref[H:H + tc, :] = x_ref[...]
        dbuf_ref[0:tc, :] = dxc
        dx = jnp.zeros((tc, M), F32)
        for k in range(K):
            dx = dx + wcv_ref[k:k + 1, :] * dbuf_ref[pl.ds(K - 1 - k, tc), :]
            dwcv_ref[k] += _fold8(dxc * xbuf_ref[pl.ds(H - (K - 1) + k, tc), :])
        dx_ref[...] = dx.astype(BF16)
        dbuf_ref[tc:tc + H, :] = dbuf_ref[0:H, :]

    def rowm(c):
        return pl.BlockSpec((tc, M), lambda i: (nc - 1 - i, c))
    halo = pl.BlockSpec((H, M), lambda i: (jnp.maximum((nc - 1 - i) * hb - 1, 0), xcol))
    vec = _full((1, M))
    return pl.pallas_call(
        body, name="lru_bwd", grid=(nc,),
        in_specs=[rowm(dcol), rowm(xcol), rowm(gcol), halo, rowm(0), rowm(0), rowm(0), rowm(0), _full((8, M)),
                  _full((M, M)), _full((M, M)), vec],
        out_specs=[rowm(0), rowm(0), _full((M, M)), _full((M, M)), _full((8, 8, M)), _full((4, 8, M))],
        out_shape=[jax.ShapeDtypeStruct((T, M), BF16), jax.ShapeDtypeStruct((T, M), BF16), jax.ShapeDtypeStruct((M, M), F32),
                   jax.ShapeDtypeStruct((M, M), F32), jax.ShapeDtypeStruct((8, 8, M), F32), jax.ShapeDtypeStruct((4, 8, M), F32)],
        scratch_shapes=[pltpu.VMEM((tc + H, M), F32), pltpu.VMEM((tc + H, M), F32), pltpu.VMEM((tc, M), F32),
                        pltpu.VMEM((tc, M), F32), pltpu.VMEM((2, 8, M), F32)],
        compiler_params=ARB,
    )(dmix, proj, proj, proj, xc, r, ig, hprev, wcv, wr, wi, sp8.reshape(1, M))


def _pool_diffs(buf_ref, x, t0, tc, gw):
    H = POOL_HALO
    pos = (lax.broadcasted_iota(jnp.int32, (tc, gw), 0) + t0 + 1).astype(F32)
    out, inv = [], []
    for gi, win in enumerate(POOL_WINDOWS):
        sl = slice(gi * gw, (gi + 1) * gw)
        s = x[:, sl]
        for j in range(1, win):
            s = s + buf_ref[pl.ds(H - j, tc), sl]
        ic = 1.0 / jnp.minimum(pos, float(win))
        out.append(s * ic - x[:, sl])
        inv.append(ic)
    return out, inv


def _pool_fwd(proj, col, pw, scale):
    T = proj.shape[0]
    ng, gw, _ = pw.shape
    M = ng * gw
    tc = _pick(T, 256, 8)
    H = POOL_HALO

    def body(x_ref, pw_ref, sc_ref, y_ref, buf_ref):
        @pl.when(pl.program_id(0) == 0)
        def _():
            buf_ref[0:H, :] = jnp.zeros((H, M), F32)

        x = x_ref[...]
        buf_ref[H:H + tc, :] = x
        diffs, _ = _pool_diffs(buf_ref, x, pl.program_id(0) * tc, tc, gw)
        for gi in range(ng):
            sl = slice(gi * gw, (gi + 1) * gw)
            y_ref[:, sl] = (_dot(diffs[gi], pw_ref[gi]) * sc_ref[:, sl]).astype(BF16)
        buf_ref[0:H, :] = buf_ref[tc:tc + H, :]

    return pl.pallas_call(
        body, name="pool_fwd", grid=(T // tc,),
        in_specs=[pl.BlockSpec((tc, M), lambda i: (i, col)), _full(pw.shape), _full((1, M))],
        out_specs=pl.BlockSpec((tc, M), lambda i: (i, 0)), out_shape=jax.ShapeDtypeStruct((T, M), BF16),
        scratch_shapes=[pltpu.VMEM((tc + H, M), F32)], compiler_params=ARB,
    )(proj, pw, scale.reshape(1, M))


def _pool_bwd(dmix, dcol, proj, col, pw, scale):
    T = proj.shape[0]
    ng, gw, _ = pw.shape
    M = ng * gw
    tc = _pick(T, 256, 8)
    nc = T // tc
    H = POOL_HALO
    hb = tc // H

    def body(do_ref, x_ref, xh_ref, pw_ref, sc_ref, dx_ref, dpw_ref, dsc_ref, buf_ref, ebuf_ref):
        i = pl.program_id(0)

        @pl.when(i == 0)
        def _():
            dpw_ref[...] = jnp.zeros_like(dpw_ref)
            dsc_ref[...] = jnp.zeros_like(dsc_ref)
            ebuf_ref[tc:tc + H, :] = jnp.zeros((H, M), F32)

        x, do = x_ref[...], do_ref[...]
        buf_ref[0:H, :] = jnp.where(i == nc - 1, 0.0, xh_ref[...])
        buf_ref[H:H + tc, :] = x
        diffs, inv = _pool_diffs(buf_ref, x, (nc - 1 - i) * tc, tc, gw)
        ddg = []
        for gi in range(ng):
            sl = slice(gi * gw, (gi + 1) * gw)
            dyy = do[:, sl] * sc_ref[:, sl]
            dsc_ref[:, sl] += _fold8(do[:, sl] * _dot(diffs[gi], pw_ref[gi]))
            dpw_ref[gi] += _dot_tn(diffs[gi], dyy)
            d = _dot_nt(dyy, pw_ref[gi])
            ddg.append(d)
            ebuf_ref[0:tc, sl] = d * inv[gi]
        for gi, win in enumerate(POOL_WINDOWS):
            sl = slice(gi * gw, (gi + 1) * gw)
            s = -ddg[gi]
            for j in range(win):
                s = s + ebuf_ref[pl.ds(j, tc), sl]
            dx_ref[:, sl] = s.astype(BF16)
        ebuf_ref[tc:tc + H, :] = ebuf_ref[0:H, :]

    return pl.pallas_call(
        body, name="pool_bwd", grid=(nc,),
        in_specs=[pl.BlockSpec((tc, M), lambda i: (nc - 1 - i, dcol)), pl.BlockSpec((tc, M), lambda i: (nc - 1 - i, col)),
                  pl.BlockSpec((H, M), lambda i: (jnp.maximum((nc - 1 - i) * hb - 1, 0), col)), _full(pw.shape), _full((1, M))],
        out_specs=[pl.BlockSpec((tc, M), lambda i: (nc - 1 - i, 0)), _full(pw.shape), _full((8, M))],
        out_shape=[jax.ShapeDtypeStruct((T, M), BF16), jax.ShapeDtypeStruct(pw.shape, F32), jax.ShapeDtypeStruct((8, M), F32)],
        scratch_shapes=[pltpu.VMEM((tc + H, M), F32), pltpu.VMEM((tc + H, M), F32)], compiler_params=ARB,
    )(dmix, proj, proj, pw, scale.reshape(1, M))


def _ffn_mid_fwd(up, wdw, bdw):
    T, F2 = up.shape
    F = F2 // 2
    tc = _pick(T, 256, 8)
    rb = _pick(tc, 64, 16)
    H, K = FFN_HALO, 3

    def body(gt_ref, val_ref, w_ref, b_ref, act_ref, buf_ref):
        @pl.when(pl.program_id(0) == 0)
        def _():
            buf_ref[0:H, :] = jnp.zeros((H, F), F32)

        buf_ref[H:H + tc, :] = gt_ref[...].astype(F32)

        def strip(s, _):
            cs = pl.ds(pl.multiple_of(s * LANES, LANES), LANES)
            taps = [w_ref[k:k + 1, cs] for k in range(K)]
            for r0 in range(0, tc, rb):
                gc = b_ref[:, cs] + taps[0] * buf_ref[pl.ds(H - (K - 1) + r0, rb), cs]
                for k in range(1, K):
                    gc = gc + taps[k] * buf_ref[pl.ds(H - (K - 1) + k + r0, rb), cs]
                act_ref[pl.ds(r0, rb), cs] = (_gelu(gc) * val_ref[pl.ds(r0, rb), cs].astype(F32)).astype(BF16)
            return 0

        lax.fori_loop(0, F // LANES, strip, 0)
        buf_ref[0:H, :] = buf_ref[tc:tc + H, :]

    return pl.pallas_call(
        body, name="ffn_mid_fwd", grid=(T // tc,),
        in_specs=[pl.BlockSpec((tc, F), lambda i: (i, 0)), pl.BlockSpec((tc, F), lambda i: (i, 1)), _full((8, F)), _full((1, F))],
        out_specs=pl.BlockSpec((tc, F), lambda i: (i, 0)), out_shape=jax.ShapeDtypeStruct((T, F), BF16),
        scratch_shapes=[pltpu.VMEM((tc + H, F), F32)], compiler_params=ARB,
    )(up, up, wdw, bdw.reshape(1, F))


def _ffn_mid_bwd(dact, up, wdw, bdw):
    T, F2 = up.shape
    F = F2 // 2
    tc = _pick(T, 256, 8)
    rb = _pick(tc, 64, 16)
    nc = T // tc
    H, K = FFN_HALO, 3
    hb = tc // H

    def body(da_ref, gt_ref, val_ref, gh_ref, w_ref, b_ref, dup_ref, dw_ref, db_ref, gbuf_ref, dbuf_ref):
        i = pl.program_id(0)

        @pl.when(i == 0)
        def _():
            dw_ref[...] = jnp.zeros_like(dw_ref)
            db_ref[...] = jnp.zeros_like(db_ref)
            dbuf_ref[tc:tc + H, :] = jnp.zeros((H, F), F32)

        gbuf_ref[0:H, :] = jnp.where(i == nc - 1, 0.0, gh_ref[...].astype(F32))
        gbuf_ref[H:H + tc, :] = gt_ref[...].astype(F32)

        def strip(s, _):
            cs = pl.ds(pl.multiple_of(s * LANES, LANES), LANES)
            cs_val = pl.ds(pl.multiple_of(F + s * LANES, LANES), LANES)
            taps = [w_ref[k:k + 1, cs] for k in range(K)]
            for r0 in range(0, tc, rb):
                shifted = [gbuf_ref[pl.ds(H - (K - 1) + k + r0, rb), cs] for k in range(K)]
                gc = b_ref[:, cs] + taps[0] * shifted[0]
                for k in range(1, K):
                    gc = gc + taps[k] * shifted[k]
                ge, gg = _gelu_and_grad(gc)
                da = da_ref[pl.ds(r0, rb), cs].astype(F32)
                dup_ref[pl.ds(r0, rb), cs_val] = (da * ge).astype(BF16)
                dgc = da * val_ref[pl.ds(r0, rb), cs].astype(F32) * gg
                dbuf_ref[pl.ds(r0, rb), cs] = dgc
                db_ref[:, cs] += _fold8(dgc)
                for k in range(K):
                    dw_ref[k, :, cs] += _fold8(dgc * shifted[k])
            for r0 in range(0, tc, rb):
                dgt = taps[0] * dbuf_ref[pl.ds(K - 1 + r0, rb), cs]
                for k in range(1, K):
                    dgt = dgt + taps[k] * dbuf_ref[pl.ds(K - 1 - k + r0, rb), cs]
                dup_ref[pl.ds(r0, rb), cs] = dgt.astype(BF16)
            return 0

        lax.fori_loop(0, F // LANES, strip, 0)
        dbuf_ref[tc:tc + H, :] = dbuf_ref[0:H, :]

    return pl.pallas_call(
        body, name="ffn_mid_bwd", grid=(nc,),
        in_specs=[pl.BlockSpec((tc, F), lambda i: (nc - 1 - i, 0)), pl.BlockSpec((tc, F), lambda i: (nc - 1 - i, 0)),
                  pl.BlockSpec((tc, F), lambda i: (nc - 1 - i, 1)),
                  pl.BlockSpec((H, F), lambda i: (jnp.maximum((nc - 1 - i) * hb - 1, 0), 0)), _full((8, F)), _full((1, F))],
        out_specs=[pl.BlockSpec((tc, F2), lambda i: (nc - 1 - i, 0)), _full((8, 8, F)), _full((8, F))],
        out_shape=[jax.ShapeDtypeStruct((T, F2), BF16), jax.ShapeDtypeStruct((8, 8, F), F32), jax.ShapeDtypeStruct((8, F), F32)],
        scratch_shapes=[pltpu.VMEM((tc + H, F), F32), pltpu.VMEM((tc + H, F), F32)], compiler_params=ARB,
    )(dact, up, up, up, wdw, bdw.reshape(1, F))


def _my_pos():
    return lax.axis_index("x"), lax.axis_index("y"), lax.axis_index("c")


def _flip(pos, k):
    x, y, c = pos
    return ((1 - x) if k & 4 else x, (1 - y) if k & 2 else y, (1 - c) if k & 1 else c)


def _rank(pos):
    return 4 * pos[0] + 2 * pos[1] + pos[2]


_HBM = pl.BlockSpec(memory_space=pltpu.HBM)
_SEM = pl.BlockSpec(memory_space=pltpu.SEMAPHORE)
_EFFECT = pltpu.SideEffectType.DATAFLOW_SIDE_EFFECTING


def _plan_direct(gather):
    def plan(bufs, ssem, rsem, me, outgoing):
        n = len(bufs) // 2
        out = []
        for a in range(n):
            for k in range(N_DEV):
                peer = _flip(me, k)
                out.append(pltpu.make_async_remote_copy(
                    src_ref=bufs[a] if gather else bufs[a].at[_rank(peer)],
                    dst_ref=bufs[n + a].at[_rank(me) if outgoing else _rank(peer)],
                    send_sem=ssem.at[a * N_DEV + k], recv_sem=rsem.at[a * N_DEV + k], device_id=peer,
                    device_id_type=pl.DeviceIdType.MESH))
        return out
    return plan, N_DEV


_SAME_CORE = (2, 4, 6)


def _plan_own_block(bufs, ssem, rsem, me, outgoing):
    n = len(bufs) // 2
    out = []
    for a in range(n):
        for j, k in enumerate((0, 1) + _SAME_CORE):
            peer = _flip(me, k)
            out.append(pltpu.make_async_remote_copy(
                src_ref=bufs[a], dst_ref=bufs[n + a].at[_rank(me) if outgoing else _rank(peer)],
                send_sem=ssem.at[a * 5 + j], recv_sem=rsem.at[a * 5 + j], device_id=peer, device_id_type=pl.DeviceIdType.MESH))
    return out


def _plan_pass_on(bufs, ssem, rsem, me, outgoing):
    sibling = _flip(me, 1)
    out = []
    for a in range(len(bufs)):
        for j, k in enumerate(_SAME_CORE):
            slot = _rank(_flip(me, k) if outgoing else _flip(sibling, k))
            out.append(pltpu.make_async_remote_copy(
                src_ref=bufs[a].at[slot], dst_ref=bufs[a].at[slot], send_sem=ssem.at[a * 3 + j], recv_sem=rsem.at[a * 3 + j],
                device_id=sibling, device_id_type=pl.DeviceIdType.MESH))
    return out


def _split_start(bufs, plan, n_sems, name, after=None):
    nb = len(bufs)
    extra = [] if after is None else [after]

    def body(*refs):
        ssem, rsem = refs[nb + len(extra)], refs[nb + len(extra) + 1]
        for cp in plan(refs[:nb], ssem, rsem, _my_pos(), True):
            cp.start()
        token = refs[2 * nb + len(extra) + 2]
        token[...] = jnp.zeros_like(token)

    res = pl.pallas_call(
        body, name=name,
        out_shape=(pltpu.SemaphoreType.DMA((n_sems,)), pltpu.SemaphoreType.DMA((n_sems,)),
                   *[pltpu.HBM(t.shape, t.dtype) for t in bufs], jax.ShapeDtypeStruct((8, 128), F32)),
        in_specs=[_HBM] * nb + [pl.BlockSpec(memory_space=pl.ANY)] * len(extra),
        out_specs=(_SEM, _SEM, *([_HBM] * nb), pl.BlockSpec(memory_space=pltpu.VMEM)),
        input_output_aliases={i: 2 + i for i in range(nb)},
        compiler_params=pltpu.CompilerParams(has_side_effects=_EFFECT),
    )(*[pltpu.with_memory_space_constraint(t, pltpu.HBM) for t in bufs], *extra)
    return (res[0], res[1], list(res[2:2 + nb])), res[2 + nb]


def _split_wait(handle, plan, after, name):
    ssem, rsem, bufs = handle
    nb = len(bufs)

    def body(*refs):
        for cp in plan(refs[:nb], refs[nb], refs[nb + 1], _my_pos(), False):
            cp.wait_send()
            cp.wait_recv()

    res = pl.pallas_call(
        body, name=name, out_shape=tuple(pltpu.HBM(t.shape, t.dtype) for t in bufs),
        in_specs=[_HBM] * nb + [_SEM, _SEM, pl.BlockSpec(memory_space=pl.ANY)], out_specs=tuple([_HBM] * nb),
        input_output_aliases={i: i for i in range(nb)},
        compiler_params=pltpu.CompilerParams(has_side_effects=_EFFECT),
    )(*bufs, ssem, rsem, after)
    return list(res)


def _landing_zones(xs, gather):
    return [lax.empty(((N_DEV,) + tuple(x.shape)) if gather else tuple(x.shape), x.dtype) for x in xs]


def _gather_start(xs, name, after=None, two_level=True):
    bufs = list(xs) + _landing_zones(xs, True)
    if two_level:
        handle, token = _split_start(bufs, _plan_own_block, 5 * len(xs), name + "_start", after)
    else:
        plan, per = _plan_direct(True)
        handle, token = _split_start(bufs, plan, per * len(xs), name + "_start", after)
    return (handle, two_level, name), token


def _gather_wait(state, after):
    handle, two_level, name = state
    n = len(handle[2]) // 2
    if not two_level:
        return _split_wait(handle, _plan_direct(True)[0], after, name + "_wait")[n:]
    lands = _split_wait(handle, _plan_own_block, after, name + "_wait")[n:]
    handle, _ = _split_start(lands, _plan_pass_on, 3 * n, name + "_pass")
    return _split_wait(handle, _plan_pass_on, after, name + "_passed")


def _scatter_start(xs, name):
    plan, per = _plan_direct(False)
    handle, token = _split_start(list(xs) + _landing_zones(xs, False), plan, per * len(xs), name + "_start")
    return (handle, name), token


def _scatter_wait(state, after):
    handle, name = state
    return _split_wait(handle, _plan_direct(False)[0], after, name + "_wait")[len(handle[2]) // 2:]


def _adamw_math(w, g, m, v):
    m = ADAM_B1 * m + (1.0 - ADAM_B1) * g
    v = ADAM_B2 * v + (1.0 - ADAM_B2) * (g * g)
    m_hat = m / (1.0 - ADAM_B1 ** ADAM_STEP)
    v_hat = v / (1.0 - ADAM_B2 ** ADAM_STEP)
    return -ADAM_LR * (m_hat / (jnp.sqrt(v_hat) + ADAM_EPS) + ADAM_WD * w), m, v


def _sum8(parts, *, name):
    _, R, C = parts.shape
    tr = _pick(R, 256, 16)

    def body(p_ref, o_ref):
        acc = p_ref[0].astype(F32)
        for r in range(1, N_DEV):
            acc = acc + p_ref[r].astype(F32)
        o_ref[...] = acc

    return pl.pallas_call(
        body, name=name, grid=(R // tr,), in_specs=[pl.BlockSpec((N_DEV, tr, C), lambda i: (0, i, 0))],
        out_specs=pl.BlockSpec((tr, C), lambda i: (i, 0)), out_shape=jax.ShapeDtypeStruct((R, C), F32),
        compiler_params=pltpu.CompilerParams(dimension_semantics=("parallel",)),
    )(parts)


def _adamw_layer(g, w, m, v, layer, prev, *, name):
    depth, R, C = w.shape
    tr = _pick(R, 64, 16)
    summed = g.ndim == 3
    if prev is None:
        prev = [lax.empty((depth, R, C), F32) for _ in range(4)]

    def body(g_ref, w_ref, m_ref, v_ref, *rest):
        outs = rest[4:]
        if summed:
            gv = g_ref[0].astype(F32)
            for r in range(1, N_DEV):
                gv = gv + g_ref[r].astype(F32)
        else:
            gv = g_ref[...]
        d, nm, nv = _adamw_math(w_ref[...], gv, m_ref[...], v_ref[...])
        for o, val in zip(outs, (gv, d, nm, nv)):
            o[...] = val

    g_spec = pl.BlockSpec((N_DEV, tr, C), lambda i: (0, i, 0)) if summed else pl.BlockSpec((tr, C), lambda i: (i, 0))
    slab = pl.BlockSpec((None, tr, C), lambda i: (layer, i, 0))
    return pl.pallas_call(
        body, name=name, grid=(R // tr,), in_specs=[g_spec] + [slab] * 3 + [pl.BlockSpec(memory_space=pl.ANY)] * 4,
        out_specs=[slab] * 4, out_shape=[jax.ShapeDtypeStruct((depth, R, C), F32)] * 4,
        input_output_aliases={4 + q: q for q in range(4)},
        compiler_params=pltpu.CompilerParams(dimension_semantics=("parallel",)),
    )(g, w, m, v, *prev)


def _lane_view(shape):
    if len(shape) == 1:
        return (1, shape[0])
    size = math.prod(shape)
    if shape[-1] < 64 and size % LANES == 0:
        return (size // LANES, LANES)
    if len(shape) > 2 and shape[-2] % 8 == 0:
        return (size // shape[-1], shape[-1])
    return tuple(shape)


def _adamw_many(ws, gs, ms, vs, *, name):
    n = len(ws)
    views = [_lane_view(w.shape) for w in ws]

    def body(*refs):
        for i in range(n):
            d, nm, nv = _adamw_math(refs[i][...], refs[n + i][...], refs[2 * n + i][...], refs[3 * n + i][...])
            refs[4 * n + i][...] = d
            refs[5 * n + i][...] = nm
            refs[6 * n + i][...] = nv

    outs = pl.pallas_call(
        body, name=name, out_shape=[jax.ShapeDtypeStruct(s, F32) for s in views] * 3,
    )(*[a.reshape(s) for arrs in (ws, gs, ms, vs) for a, s in zip(arrs, views)])
    return [[o.reshape(w.shape) for o, w in zip(outs[q * n:(q + 1) * n], ws)] for q in range(3)]


def _pack(arrs, dtype=F32):
    flat = jnp.concatenate([a.reshape(-1).astype(dtype) for a in arrs])
    pad = (-flat.shape[0]) % 2048
    return jnp.pad(flat, (0, pad)).reshape(-1, 128)


def _unpack(buf, shapes):
    flat, out, off = buf.reshape(-1), [], 0
    for s in shapes:
        n = math.prod(s)
        out.append(flat[off:off + n].reshape(s))
        off += n
    return out


def _s5_params(lam_re, lam_im, log_step, b_re, b_im):
    G, P, H = b_re.shape
    step = jnp.exp(log_step)[:, None]
    mag = jnp.exp(lam_re * step)
    ar, ai = mag * jnp.cos(lam_im * step), mag * jnp.sin(lam_im * step)
    den = lam_re * lam_re + lam_im * lam_im
    qr = ((ar - 1.0) * lam_re + ai * lam_im) / den
    qi = (ai * lam_re - (ar - 1.0) * lam_im) / den
    bbr = qr[..., None] * b_re - qi[..., None] * b_im
    bbi = qr[..., None] * b_im + qi[..., None] * b_re
    return ar.reshape(-1), ai.reshape(-1), _s5_blocks_in(bbr), _s5_blocks_in(bbi)


def _s5_blocks_in(b):
    G, P, H = b.shape
    eye = jnp.eye(8, dtype=b.dtype)
    return jnp.einsum("kgph,gj->kghjp", b.reshape(G // 8, 8, P, H), eye).reshape(G // 8, 8 * H, 8 * P)


def _s5_blocks_out(c):
    G, H, P = c.shape
    eye = jnp.eye(8, dtype=c.dtype)
    return jnp.einsum("kghp,gj->kgpjh", c.reshape(G // 8, 8, H, P), eye).reshape(G // 8, 8 * P, 8 * H)


def _s5_blocks_out_diag(m, G, H, P):
    return jnp.einsum("kgpgh->kghp", m.reshape(G // 8, 8, P, 8, H)).reshape(G, H, P)


def _head_blocks(w):
    nh, d, e = w.shape
    return jnp.einsum("hde,hk->hdke", w, jnp.eye(nh, dtype=w.dtype)).reshape(nh * d, nh * e)


def _head_blocks_diag(m, nh, d, e):
    return jnp.einsum("hdhe->hde", m.reshape(nh, d, nh, e))


def _pad_rows(w, rows):
    return jnp.pad(w, ((0, rows - w.shape[0]), (0, 0)))


def _unshard_cols(g):
    return jnp.transpose(g, (1, 0, 2)).reshape(g.shape[1], -1)


_NAMES = ['norm_mix_g', 'w_in', 's5_lam_re', 's5_lam_im', 's5_log_step', 's5_b_re', 's5_b_im', 's5_c_re', 's5_c_im', 's5_d',
          's5_w_glu', 's5_b_glu', 'cv_w_dw', 'cv_b_dw', 'cv_ln_g', 'cv_ln_b', 'cv_w_pw', 'cv_b_pw', 'lru_w_conv', 'lru_b_conv',
          'lru_w_r', 'lru_b_r', 'lru_w_i', 'lru_b_i', 'lru_lam', 'pool_w', 'pool_scale', 'w_out', 'norm_ffn_g', 'ffn_w_up',
          'ffn_w_dw', 'ffn_b_dw', 'ffn_w_down', 'norm_final_g']
_MATRIX = ('w_in', 'w_out', 'ffn_w_up', 'ffn_w_down', 's5_w_glu', 'cv_w_pw')
_TRANSPOSED = ('w_in', 'ffn_w_up')
_COLSHARD = ('cv_w_dw', 'lru_w_conv', 'ffn_w_dw')
_GROUPS = (('w_in', 's5_w_glu', 'cv_w_pw', 'w_out'), ('ffn_w_up', 'ffn_w_down'))
_SMALL_BF16 = ('s5_b_re', 's5_b_im', 's5_c_re', 's5_c_im', 'lru_w_r', 'lru_w_i', 'pool_w')
_SCATTER_GROUPS = (('w_out', 's5_w_glu', 'cv_w_pw'), ('w_in',), ('ffn_w_up', 'ffn_w_down'))


def kernel(x, norm_mix_g, w_in, s5_lam_re, s5_lam_im, s5_log_step, s5_b_re, s5_b_im, s5_c_re, s5_c_im, s5_d, s5_w_glu, s5_b_glu, cv_w_dw, cv_b_dw, cv_ln_g, cv_ln_b, cv_w_pw, cv_b_pw, lru_w_conv, lru_b_conv, lru_w_r, lru_b_r, lru_w_i, lru_b_i, lru_lam, pool_w, pool_scale, w_out, norm_ffn_g, ffn_w_up, ffn_w_dw, ffn_b_dw, ffn_w_down, norm_final_g, loss_target, m_norm_mix_g, m_w_in, m_s5_lam_re, m_s5_lam_im, m_s5_log_step, m_s5_b_re, m_s5_b_im, m_s5_c_re, m_s5_c_im, m_s5_d, m_s5_w_glu, m_s5_b_glu, m_cv_w_dw, m_cv_b_dw, m_cv_ln_g, m_cv_ln_b, m_cv_w_pw, m_cv_b_pw, m_lru_w_conv, m_lru_b_conv, m_lru_w_r, m_lru_b_r, m_lru_w_i, m_lru_b_i, m_lru_lam, m_pool_w, m_pool_scale, m_w_out, m_norm_ffn_g, m_ffn_w_up, m_ffn_w_dw, m_ffn_b_dw, m_ffn_w_down, m_norm_final_g, v_norm_mix_g, v_w_in, v_s5_lam_re, v_s5_lam_im, v_s5_log_step, v_s5_b_re, v_s5_b_im, v_s5_c_re, v_s5_c_im, v_s5_d, v_s5_w_glu, v_s5_b_glu, v_cv_w_dw, v_cv_b_dw, v_cv_ln_g, v_cv_ln_b, v_cv_w_pw, v_cv_b_pw, v_lru_w_conv, v_lru_b_conv, v_lru_w_r, v_lru_b_r, v_lru_w_i, v_lru_b_i, v_lru_lam, v_pool_w, v_pool_scale, v_w_out, v_norm_ffn_g, v_ffn_w_up, v_ffn_w_dw, v_ffn_b_dw, v_ffn_w_down, v_norm_final_g):
    args = (x, norm_mix_g, w_in, s5_lam_re, s5_lam_im, s5_log_step, s5_b_re, s5_b_im, s5_c_re, s5_c_im, s5_d, s5_w_glu, s5_b_glu, cv_w_dw, cv_b_dw, cv_ln_g, cv_ln_b, cv_w_pw, cv_b_pw, lru_w_conv, lru_b_conv, lru_w_r, lru_b_r, lru_w_i, lru_b_i, lru_lam, pool_w, pool_scale, w_out, norm_ffn_g, ffn_w_up, ffn_w_dw, ffn_b_dw, ffn_w_down, norm_final_g, loss_target, m_norm_mix_g, m_w_in, m_s5_lam_re, m_s5_lam_im, m_s5_log_step, m_s5_b_re, m_s5_b_im, m_s5_c_re, m_s5_c_im, m_s5_d, m_s5_w_glu, m_s5_b_glu, m_cv_w_dw, m_cv_b_dw, m_cv_ln_g, m_cv_ln_b, m_cv_w_pw, m_cv_b_pw, m_lru_w_conv, m_lru_b_conv, m_lru_w_r, m_lru_b_r, m_lru_w_i, m_lru_b_i, m_lru_lam, m_pool_w, m_pool_scale, m_w_out, m_norm_ffn_g, m_ffn_w_up, m_ffn_w_dw, m_ffn_b_dw, m_ffn_w_down, m_norm_final_g, v_norm_mix_g, v_w_in, v_s5_lam_re, v_s5_lam_im, v_s5_log_step, v_s5_b_re, v_s5_b_im, v_s5_c_re, v_s5_c_im, v_s5_d, v_s5_w_glu, v_s5_b_glu, v_cv_w_dw, v_cv_b_dw, v_cv_ln_g, v_cv_ln_b, v_cv_w_pw, v_cv_b_pw, v_lru_w_conv, v_lru_b_conv, v_lru_w_r, v_lru_b_r, v_lru_w_i, v_lru_b_i, v_lru_lam, v_pool_w, v_pool_scale, v_w_out, v_norm_ffn_g, v_ffn_w_up, v_ffn_w_dw, v_ffn_b_dw, v_ffn_w_down, v_norm_final_g)
    x, target = args[0], args[35]
    W = dict(zip(_NAMES, args[1:35]))
    Mo = dict(zip(_NAMES, args[36:70]))
    Vo = dict(zip(_NAMES, args[70:104]))
    x = x[0]
    target = target[0]
    T, D = x.shape
    depth = W['w_in'].shape[0]
    F = W['ffn_w_down'].shape[1] * N_DEV
    G, P, H = W['s5_b_re'].shape[1:]
    nh, hd, _ = W['lru_w_r'].shape[1:]
    me = _rank(_my_pos())

    gather, full = {}, {}

    def start_group(l, gi, after):
        shards = [(W[n][l].T if n in _TRANSPOSED else W[n][l]).astype(BF16) for n in _GROUPS[gi]]
        if (l, gi) == (0, 0):
            shards.append(_pack([W[n][k] for k in range(depth) for n in _COLSHARD]))
        gather[l, gi], tok = _gather_start(shards, "gather_%d%d" % (l, gi), after)
        return tok

    def land_weights(l, gi, after):
        got = _gather_wait(gather[l, gi], after)
        for n, g8 in zip(_GROUPS[gi], got):
            full[n, l] = g8.reshape(-1, g8.shape[-1])
        return got

    started = start_group(0, 0, None)[0, 0]
    consts = []
    for l in range(depth):
        s5p, s5p_vjp = jax.vjp(_s5_params, W['s5_lam_re'][l], W['s5_lam_im'][l], W['s5_log_step'][l], W['s5_b_re'][l], W['s5_b_im'][l])
        ar, ai, bre, bim = s5p
        cre, cim = _s5_blocks_out(W['s5_c_re'][l]), _s5_blocks_out(W['s5_c_im'][l])
        sp8, sp8_vjp = jax.vjp(lambda lam: LRU_C * jax.nn.softplus(-lam), W['lru_lam'][l])
        wr, wi = _head_blocks(W['lru_w_r'][l]), _head_blocks(W['lru_w_i'][l])
        consts.append(dict(bre=bre.astype(BF16), bim=bim.astype(BF16), cre=cre.astype(BF16), cim=cim.astype(BF16),
                           cst_f=_s5_scan_consts(ar, ai, False), cst_b=_s5_scan_consts(ar, ai, True), sp8=sp8,
                           wr=wr.astype(BF16), wi=wi.astype(BF16), pw=W['pool_w'][l].astype(BF16)))
        consts[-1].update(s5p_vjp=s5p_vjp, sp8_vjp=sp8_vjp)
    ready = sum(c[k].reshape(-1)[0].astype(F32) for c in consts for k in c if not k.endswith('_vjp'))
    got = land_weights(0, 0, (ready + started).reshape(1, 1))
    started = start_group(0, 1, got[0])[0, 0]
    taps8 = got[-1]
    tap_shapes = [W[n][l].shape for l in range(depth) for n in _COLSHARD]
    taps = [_unshard_cols(jnp.stack(t)) for t in zip(*[_unpack(taps8[r], tap_shapes) for r in range(N_DEV)])]
    for l in range(depth):
        for j, n in enumerate(_COLSHARD):
            full[n, l] = taps[l * len(_COLSHARD) + j]

    saved = []
    for l in range(depth):
        g_mix = W['norm_mix_g'][l] + (0.0 if l else started)
        if l:
            land_weights(l, 0, x)
        c = dict(consts[l], cvw=_pad_rows(full['cv_w_dw', l], 32), lruw=_pad_rows(full['lru_w_conv', l], 8),
                 ffw=_pad_rows(full['ffn_w_dw', l], 8), x0=x)
        sp8 = c['sp8']
        h = _rmsnorm(x, g_mix, name="mix_norm")
        proj = _matmul(h, full['w_in', l], mode="nt", tm=1024, tn=512, tk=D, out_dtype=F32, name="in_proj")
        y0, sre, sim, yp, z = _s5_fwd(proj, 0, c['bre'], c['bim'], c['cre'], c['cim'], c['cst_f'], W['s5_d'][l],
                                     full['s5_w_glu', l], W['s5_b_glu'][l])
        y1, h1 = _cv_fwd(proj, 1, 2, c['cvw'], W['cv_b_dw'][l], W['cv_ln_g'][l], W['cv_ln_b'][l], full['cv_w_pw', l], W['cv_b_pw'][l])
        y2, xc, rr, ig, hprev = _lru_fwd(proj, 3, 4, c['lruw'], W['lru_b_conv'][l], c['wr'], W['lru_b_r'][l], c['wi'],
                                         W['lru_b_i'][l], sp8)
        y3 = _pool_fwd(proj, 5, c['pw'], W['pool_scale'][l])
        mixed = jnp.concatenate([y0, y1, y2, y3], axis=1)
        x1 = _matmul(mixed, full['w_out', l], mode="nn", tm=1024, tn=512, tk=2048, out_dtype=F32, name="out_proj", add=x)
        got = land_weights(l, 1, x1)
        g_ffn = W['norm_ffn_g'][l]
        if l + 1 < depth:
            g_ffn = g_ffn + start_group(l + 1, 0, got[0])[0, 0] + start_group(l + 1, 1, got[0])[0, 0]
        h2 = _rmsnorm(x1, g_ffn, name="ffn_norm")
        up = _matmul(h2, full['ffn_w_up', l], mode="nt", tm=2048, tn=256, tk=D, out_dtype=BF16, name="ffn_up")
        act = _ffn_mid_fwd(up, c['ffw'], W['ffn_b_dw'][l])
        x = _matmul(act, full['ffn_w_down', l], mode="nn", tm=1024, tn=512, tk=F, out_dtype=F32, name="ffn_down", add=x1)
        c.update(proj=proj, h=h, sre=sre, sim=sim, yp=yp, z=z, h1=h1, xc=xc, rr=rr, ig=ig, hprev=hprev, mixed=mixed, x1=x1,
                 up=up, h2=h2, act=act)
        saved.append(c)

    dx, dxb, dgf, se = _loss_head(x, W['norm_final_g'], target)
    loss = lax.psum(0.5 / D * jnp.sum(se), AXES)
    gsmall = {('norm_final_g', None): dgf.sum(0)}
    gmat, scatter = {}, {}

    def parts_of(m):
        return m.reshape(N_DEV, m.shape[0] // N_DEV, m.shape[1])

    for l in reversed(range(depth)):
        c = saved[l]
        dact = _matmul(dxb, full['ffn_w_down', l], mode="nt", tm=512, tn=F, tk=512, out_dtype=BF16, name="d_act")
        gmat['ffn_w_down', l] = _matmul(c['act'], dxb, mode="tn", tm=F, tn=512, tk=512, out_dtype=BF16, name="dw_down")
        dup, dffw, dffb = _ffn_mid_bwd(dact, c['up'], c['ffw'], W['ffn_b_dw'][l])
        gsmall['ffn_w_dw', l] = dffw.sum(1)[:3]
        gsmall['ffn_b_dw', l] = dffb.sum(0)
        dh2 = _matmul(dup, full['ffn_w_up', l], mode="nn", tm=1024, tn=512, tk=F, out_dtype=F32, name="d_h2")
        gmat['ffn_w_up', l] = _matmul(dup, c['h2'], mode="tn", tm=256, tn=D, tk=T, out_dtype=BF16, name="dw_up")
        scatter[l, 2], tok = _scatter_start([parts_of(gmat[n, l]) for n in _SCATTER_GROUPS[2]], "scatter_%d2" % l)
        dx1, dx1b, dg2 = _rms_bwd(dx, dh2, c['x1'], W['norm_ffn_g'][l] + tok[0, 0], name="ffn_norm_bwd")
        gsmall['norm_ffn_g', l] = dg2.sum(0)
        dmix = _matmul(dx1b, full['w_out', l], mode="nt", tm=1024, tn=512, tk=2048, out_dtype=F32, name="d_mixed")
        gmat['w_out', l] = _matmul(c['mixed'], dx1b, mode="tn", tm=512, tn=D, tk=T, out_dtype=BF16, name="dw_out")
        du, dbre, dbim, dcre, dcim, da, dwglu, v3 = _s5_bwd(dmix, 0, c['proj'], 0, c['sre'], c['sim'], c['yp'], c['z'], c['bre'],
                                                            c['bim'], c['cre'], c['cim'], c['cst_b'], W['s5_d'][l], full['s5_w_glu', l])
        da = da.sum(1)
        glr, gli, gls, gbr, gbi = c['s5p_vjp']((da[0], da[1], dbre, dbim))
        gsmall['s5_lam_re', l], gsmall['s5_lam_im', l], gsmall['s5_log_step', l] = glr, gli, gls
        gsmall['s5_b_re', l], gsmall['s5_b_im', l] = gbr, gbi
        gsmall['s5_c_re', l] = _s5_blocks_out_diag(dcre, G, H, P)
        gsmall['s5_c_im', l] = _s5_blocks_out_diag(dcim, G, H, P)
        gsmall['s5_b_glu', l], gsmall['s5_d', l] = v3[0].sum(0), v3[1].sum(0)
        gmat['s5_w_glu', l] = dwglu.astype(BF16)
        dv, dg, dwpw, dcvw, v5 = _cv_bwd(dmix, 1, c['proj'], 1, 2, c['h1'], c['cvw'], W['cv_ln_g'][l], W['cv_ln_b'][l], full['cv_w_pw', l])
        gmat['cv_w_pw', l] = dwpw.astype(BF16)
        gsmall['cv_w_dw', l] = dcvw.sum(1)[:31]
        gsmall['cv_b_pw', l], gsmall['cv_ln_g', l], gsmall['cv_ln_b', l], gsmall['cv_b_dw', l] = [v5[j].sum(0) for j in range(4)]
        scatter[l, 0], tok = _scatter_start([parts_of(gmat[n, l]) for n in _SCATTER_GROUPS[0]], "scatter_%d0" % l)
        tok = tok[0, 0]
        dlx, dlg, dwr, dwi, dlw, v4 = _lru_bwd(dmix, 2, c['proj'], 3, 4, c['xc'], c['rr'], c['ig'], c['hprev'], c['lruw'], c['wr'],
                                               c['wi'], c['sp8'] + tok)
        gsmall['lru_w_r', l], gsmall['lru_w_i', l] = _head_blocks_diag(dwr, nh, hd, hd), _head_blocks_diag(dwi, nh, hd, hd)
        gsmall['lru_w_conv', l] = dlw.sum(1)[:4]
        gsmall['lru_b_r', l], gsmall['lru_b_i', l] = v4[0].sum(0), v4[1].sum(0)
        gsmall['lru_lam', l] = c['sp8_vjp'](v4[2].sum(0))[0]
        gsmall['lru_b_conv', l] = v4[3].sum(0)
        dpx, dpw, dps = _pool_bwd(dmix, 3, c['proj'], 5, c['pw'], W['pool_scale'][l] + tok)
        gsmall['pool_w', l], gsmall['pool_scale', l] = dpw, dps.sum(0)
        dproj = jnp.concatenate([du, dv, dg, dlx, dlg, dpx], axis=1)
        dh = _matmul(dproj, full['w_in', l], mode="nn", tm=1024, tn=512, tk=dproj.shape[1], out_dtype=F32, name="d_h")
        gmat['w_in', l] = _matmul(dproj, c['h'], mode="tn", tm=512, tn=D, tk=T, out_dtype=BF16, name="dw_in")
        scatter[l, 1], tok = _scatter_start([parts_of(gmat[n, l]) for n in _SCATTER_GROUPS[1]], "scatter_%d1" % l)
        dx, dxb, dg1 = _rms_bwd(dx1, dh, c['x0'], W['norm_mix_g'][l] + tok[0, 0], name="mix_norm_bwd")
        gsmall['norm_mix_g', l] = dg1.sum(0)

    small_names = [n for n in _NAMES if n not in _MATRIX]
    small_split = ([n for n in small_names if n not in _SMALL_BF16], [n for n in small_names if n in _SMALL_BF16])
    small_packs, full_shapes = [], []
    for names, dtype in zip(small_split, (F32, BF16)):
        layers = [[None] if n == 'norm_final_g' else range(depth) for n in names]
        small_packs.append(_pack([gsmall[n, l] for n, ls in zip(names, layers) for l in ls], dtype))
        full_shapes.append([gsmall[n, None].shape if n == 'norm_final_g' else (depth,) + gsmall[n, 0].shape for n in names])
    small_state, tok = _gather_start(small_packs, "small_grads")

    stored_t = [n for n in _TRANSPOSED if W[n].shape[-1] % LANES]
    wmv = {n: [jnp.swapaxes(t[n], 1, 2) if n in stored_t else t[n] for t in (W, Mo, Vo)] for n in _MATRIX}
    stacked = {n: None for n in _MATRIX}

    def update_matrices(l, after):
        for gi in (2, 0, 1):
            for n, p8 in zip(_SCATTER_GROUPS[gi], _scatter_wait(scatter[l, gi], after)):
                g = _sum8(p8, name="sum_%s_%d" % (n, l)).T if n in _TRANSPOSED and n not in stored_t else p8
                stacked[n] = _adamw_layer(g, *wmv[n], l, stacked[n], name="adamw_%s_%d" % (n, l))

    after = tok
    for l in reversed(range(depth)):
        update_matrices(l, after)
        after = sum(stacked[n][1][l, 0, 0] for n in _MATRIX).reshape(1, 1)
    after_matrices = after
    gs = {}
    for names, shapes, small8 in zip(small_split, full_shapes, _gather_wait(small_state, after_matrices)):
        gs.update(zip(names, _unpack(_sum8(small8, name="sum_small_" + str(small8.dtype)), shapes)))
    for n in _COLSHARD:
        cw = W[n].shape[-1]
        gs[n] = lax.dynamic_slice_in_dim(gs[n], me * cw, cw, axis=gs[n].ndim - 1)
    upd = _adamw_many([W[n] for n in small_names], [gs[n] for n in small_names], [Mo[n] for n in small_names],
                      [Vo[n] for n in small_names], name="adamw_small")
    out = {n: (gs[n], upd[0][j], upd[1][j], upd[2][j]) for j, n in enumerate(small_names)}
    for n in _MATRIX:
        out[n] = tuple(jnp.swapaxes(t, 1, 2) for t in stacked[n]) if n in stored_t else tuple(stacked[n])
    return (loss, dx[None]) + tuple(out[n][q] for q in range(4) for n in _NAMES)
```

```python
import math

import jax
import jax.numpy as jnp
from jax import lax
from jax.experimental import pallas as pl
from jax.experimental.pallas import tpu as pltpu

F32 = jnp.float32
BF16 = jnp.bfloat16
N_DEV = 8
AXES = ("x", "y", "c")
EPS = 1e-6
LRU_C = 8.0
POOL_WINDOWS = (2, 4, 8, 16)
CONV_HALO = 32
SMALL_HALO = 8
FFN_HALO = 16
POOL_HALO = 16
LANES = 128
ADAM_LR, ADAM_B1, ADAM_B2, ADAM_EPS, ADAM_WD, ADAM_STEP = 0.001, 0.9, 0.999, 1e-08, 0.01, 10
_GELU_K = math.sqrt(2.0 / math.pi)
ARB = pltpu.CompilerParams(dimension_semantics=("arbitrary",))


def _pick(n, pref, mult):
    best = None
    for t in range(mult, min(n, pref) + 1, mult):
        if n % t == 0:
            best = t
    return best if best is not None else n


def _sigmoid(x):
    return 1.0 / (1.0 + jnp.exp(-x))


def _gelu(x):
    return 0.5 * x * (1.0 + jnp.tanh(_GELU_K * (x + 0.044715 * x * x * x)))


def _gelu_grad(x):
    t = jnp.tanh(_GELU_K * (x + 0.044715 * x * x * x))
    return 0.5 * (1.0 + t) + 0.5 * x * (1.0 - t * t) * _GELU_K * (1.0 + 3.0 * 0.044715 * x * x)


def _gelu_and_grad(x):
    t = jnp.tanh(_GELU_K * (x + 0.044715 * x * x * x))
    half = 0.5 * (1.0 + t)
    return x * half, half + 0.5 * x * (1.0 - t * t) * _GELU_K * (1.0 + 3.0 * 0.044715 * x * x)


def _neg_expm1(x):
    series = -x * (1.0 + x * (0.5 + x * (1.0 / 6.0 + x * (1.0 / 24.0 + x * (1.0 / 120.0 + x * (1.0 / 720.0))))))
    return jnp.where(jnp.abs(x) < 0.25, series, 1.0 - jnp.exp(x))


def _fold8(x):
    return x.reshape(x.shape[0] // 8, 8, x.shape[1]).sum(axis=0)


def _dot(a, b):
    return jnp.dot(a.astype(BF16), b.astype(BF16), preferred_element_type=F32)


def _dot_nt(a, b):
    return lax.dot_general(a.astype(BF16), b.astype(BF16), (((1,), (1,)), ((), ())), preferred_element_type=F32)


def _dot_tn(a, b):
    return lax.dot_general(a.astype(BF16), b.astype(BF16), (((0,), (0,)), ((), ())), preferred_element_type=F32)


def _full(shape):
    nd = len(shape)
    return pl.BlockSpec(shape, lambda *i: (0,) * nd)


def _matmul(a, b, *, mode, tm, tn, tk, out_dtype, name, add=None):
    if mode == "nn":
        (M, K), N = a.shape, b.shape[1]
    elif mode == "nt":
        (M, K), N = a.shape, b.shape[0]
    else:
        (K, M), N = a.shape, b.shape[1]
    tm, tn, tk = _pick(M, tm, 8), _pick(N, tn, 128), _pick(K, tk, 128)
    if mode == "tn":
        tm = _pick(M, tm, 128)
    nk = K // tk
    a_spec = {"nn": pl.BlockSpec((tm, tk), lambda i, j, k: (i, k)), "nt": pl.BlockSpec((tm, tk), lambda i, j, k: (i, k)),
              "tn": pl.BlockSpec((tk, tm), lambda i, j, k: (k, i))}[mode]
    b_spec = {"nn": pl.BlockSpec((tk, tn), lambda i, j, k: (k, j)), "nt": pl.BlockSpec((tn, tk), lambda i, j, k: (j, k)),
              "tn": pl.BlockSpec((tk, tn), lambda i, j, k: (k, j))}[mode]
    o_spec = pl.BlockSpec((tm, tn), lambda i, j, k: (i, j))
    dot = {"nn": _dot, "nt": _dot_nt, "tn": _dot_tn}[mode]
    has_add = add is not None

    def body(*refs):
        if has_add:
            a_ref, b_ref, add_ref, o_ref, acc_ref = refs
        else:
            a_ref, b_ref, o_ref, acc_ref = refs
        if nk == 1:
            part = dot(a_ref[...], b_ref[...])
            o_ref[...] = (part + add_ref[...] if has_add else part).astype(out_dtype)
            return
        k = pl.program_id(2)

        @pl.when(k == 0)
        def _():
            acc_ref[...] = jnp.zeros_like(acc_ref)

        acc_ref[...] += dot(a_ref[...], b_ref[...])

        @pl.when(k == nk - 1)
        def _():
            r = acc_ref[...]
            if has_add:
                r = r + add_ref[...]
            o_ref[...] = r.astype(out_dtype)

    ins = [a, b] + ([add] if has_add else [])
    specs = [a_spec, b_spec] + ([o_spec] if has_add else [])
    return pl.pallas_call(
        body, name=name, grid=(M // tm, N // tn, nk), in_specs=specs, out_specs=o_spec,
        out_shape=jax.ShapeDtypeStruct((M, N), out_dtype),
        scratch_shapes=[pltpu.VMEM((tm, tn) if nk > 1 else (8, LANES), F32)],
        compiler_params=pltpu.CompilerParams(dimension_semantics=("parallel", "parallel", "arbitrary")),
    )(*ins)


def _rmsnorm(x, g, *, name):
    T, D = x.shape
    tc = _pick(T, 512, 16)

    def body(x_ref, g_ref, h_ref):
        def rows16(n, _):
            rows = pl.ds(pl.multiple_of(n * 16, 16), 16)
            xv = x_ref[rows, :]
            r = lax.rsqrt(jnp.mean(xv * xv, axis=-1, keepdims=True) + EPS)
            h_ref[rows, :] = (xv * r * g_ref[...]).astype(BF16)
            return 0

        lax.fori_loop(0, tc // 16, rows16, 0, unroll=4)

    row = pl.BlockSpec((tc, D), lambda i: (i, 0))
    return pl.pallas_call(
        body, name=name, grid=(T // tc,), in_specs=[row, _full((1, D))], out_specs=row,
        out_shape=jax.ShapeDtypeStruct((T, D), BF16), compiler_params=pltpu.CompilerParams(dimension_semantics=("parallel",)),
    )(x, g.reshape(1, D))


def _rms_bwd(dres, dh, x, g, *, name):
    T, D = x.shape
    tc = _pick(T, 512, 16)

    def body(dres_ref, dh_ref, x_ref, g_ref, dx_ref, dxb_ref, dg_ref):
        @pl.when(pl.program_id(0) == 0)
        def _():
            dg_ref[...] = jnp.zeros_like(dg_ref)

        def rows16(n, _):
            rows = pl.ds(pl.multiple_of(n * 16, 16), 16)
            xv, dhv = x_ref[rows, :], dh_ref[rows, :]
            r = lax.rsqrt(jnp.mean(xv * xv, axis=-1, keepdims=True) + EPS)
            dg_ref[...] += _fold8(dhv * xv * r)
            dyg = dhv * g_ref[...]
            dx = dres_ref[rows, :] + r * dyg - xv * (r * r * r) * jnp.mean(dyg * xv, axis=-1, keepdims=True)
            dx_ref[rows, :] = dx
            dxb_ref[rows, :] = dx.astype(BF16)
            return 0

        lax.fori_loop(0, tc // 16, rows16, 0, unroll=4)

    row = pl.BlockSpec((tc, D), lambda i: (i, 0))
    return pl.pallas_call(
        body, name=name, grid=(T // tc,), in_specs=[row, row, row, _full((1, D))],
        out_specs=[row, row, _full((8, D))],
        out_shape=[jax.ShapeDtypeStruct((T, D), F32), jax.ShapeDtypeStruct((T, D), BF16), jax.ShapeDtypeStruct((8, D), F32)],
        compiler_params=ARB,
    )(dres, dh, x, g.reshape(1, D))


def _loss_head(x, g, target):
    T, D = x.shape
    tc = _pick(T, 256, 8)

    def body(x_ref, g_ref, t_ref, dx_ref, dxb_ref, dg_ref, se_ref):
        @pl.when(pl.program_id(0) == 0)
        def _():
            dg_ref[...] = jnp.zeros_like(dg_ref)
            se_ref[...] = jnp.zeros_like(se_ref)

        def rows16(n, _):
            rows = pl.ds(pl.multiple_of(n * 16, 16), 16)
            xv = x_ref[rows, :]
            r = lax.rsqrt(jnp.mean(xv * xv, axis=-1, keepdims=True) + EPS)
            gv = g_ref[...]
            err = xv * r * gv - t_ref[rows, :]
            se_ref[...] += _fold8(err * err)
            dy = err * (1.0 / D)
            dg_ref[...] += _fold8(dy * xv * r)
            dyg = dy * gv
            dx = r * dyg - xv * (r * r * r) * jnp.mean(dyg * xv, axis=-1, keepdims=True)
            dx_ref[rows, :] = dx
            dxb_ref[rows, :] = dx.astype(BF16)
            return 0

        lax.fori_loop(0, tc // 16, rows16, 0, unroll=4)

    row = pl.BlockSpec((tc, D), lambda i: (i, 0))
    return pl.pallas_call(
        body, name="loss_head", grid=(T // tc,), in_specs=[row, _full((1, D)), row],
        out_specs=[row, row, _full((8, D)), _full((8, D))],
        out_shape=[jax.ShapeDtypeStruct((T, D), F32), jax.ShapeDtypeStruct((T, D), BF16),
                   jax.ShapeDtypeStruct((8, D), F32), jax.ShapeDtypeStruct((8, D), F32)],
        compiler_params=ARB,
    )(x, g.reshape(1, D), target)


def _s5_scan_consts(ar, ai, reverse):
    if reverse:
        ai = -ai
    pr, pi_ = [ar], [ai]
    for _ in range(7):
        pr, pi_ = pr + [pr[-1] * ar - pi_[-1] * ai], pi_ + [pr[-1] * ai + pi_[-1] * ar]
    rows = jnp.arange(8)[:, None]
    planes = []
    for d in (1, 2, 4):
        mask = (rows <= 7 - d) if reverse else (rows >= d)
        planes += [jnp.where(mask, pr[d - 1][None, :], 0.0), jnp.where(mask, pi_[d - 1][None, :], 0.0)]
    order = list(range(7, -1, -1)) if reverse else list(range(8))
    planes += [jnp.stack([pr[k] for k in order]), jnp.stack([pi_[k] for k in order])]
    return jnp.stack(planes).astype(F32)


def _s5_block_scan(xr_ref, xi_ref, cst_ref, car_ref, nblk, reverse, lb, extra=None):
    ns = xr_ref.shape[1]
    crow = 0 if reverse else 7

    def blk(i, _):
        ib = (nblk - 1 - i) if reverse else i
        r0 = pl.multiple_of(ib * 8, 8)
        for k in range(ns // lb):
            sl = slice(k * lb, (k + 1) * lb)
            xr, xi = xr_ref[pl.ds(r0, 8), sl], xi_ref[pl.ds(r0, 8), sl]
            for n, d in enumerate((1, 2, 4)):
                ar, ai = cst_ref[2 * n, :, sl], cst_ref[2 * n + 1, :, sl]
                sh = (8 - d) if reverse else d
                rr, ri = pltpu.roll(xr, sh, 0), pltpu.roll(xi, sh, 0)
                xr, xi = xr + ar * rr - ai * ri, xi + ar * ri + ai * rr
            pr, pi_ = cst_ref[6, :, sl], cst_ref[7, :, sl]
            cr, ci = car_ref[0, crow:crow + 1, sl], car_ref[1, crow:crow + 1, sl]
            xr, xi = xr + pr * cr - pi_ * ci, xi + pr * ci + pi_ * cr
            xr_ref[pl.ds(r0, 8), sl] = xr
            xi_ref[pl.ds(r0, 8), sl] = xi
            if extra is not None:
                extra(r0, sl, xr, xi, cr, ci)
            car_ref[0, :, sl] = xr
            car_ref[1, :, sl] = xi
        return 0

    lax.fori_loop(0, nblk, blk, 0)


def _s5_fwd(proj, col, bre, bim, cre, cim, cst, dvec, wglu, bglu):
    T = proj.shape[0]
    nsb, cw, sw = bre.shape
    M, NS = nsb * cw, nsb * sw
    tc = _pick(T, 512, 8)

    def body(u_ref, bre_ref, bim_ref, cre_ref, cim_ref, cst_ref, d_ref, wg_ref, bg_ref,
             y_ref, sre_ref, sim_ref, yp_ref, z_ref, car_ref):
        @pl.when(pl.program_id(0) == 0)
        def _():
            car_ref[...] = jnp.zeros_like(car_ref)

        u = u_ref[...]
        ub = u.astype(BF16)
        for k in range(nsb):
            sre_ref[:, k * sw:(k + 1) * sw] = _dot(ub[:, k * cw:(k + 1) * cw], bre_ref[k])
            sim_ref[:, k * sw:(k + 1) * sw] = _dot(ub[:, k * cw:(k + 1) * cw], bim_ref[k])
        _s5_block_scan(sre_ref, sim_ref, cst_ref, car_ref, tc // 8, False, sw)
        for k in range(nsb):
            yk = _dot(sre_ref[:, k * sw:(k + 1) * sw], cre_ref[k]) - _dot(sim_ref[:, k * sw:(k + 1) * sw], cim_ref[k])
            yp_ref[:, k * cw:(k + 1) * cw] = yk + d_ref[:, k * cw:(k + 1) * cw] * u[:, k * cw:(k + 1) * cw]
        gl = _gelu(yp_ref[...])
        z = _dot(gl, wg_ref[...]) + bg_ref[...]
        z_ref[...] = z
        y_ref[...] = (gl * _sigmoid(z)).astype(BF16)

    rowm = pl.BlockSpec((tc, M), lambda i: (i, 0))
    rows = pl.BlockSpec((tc, NS), lambda i: (i, 0))
    return pl.pallas_call(
        body, name="s5_fwd", grid=(T // tc,),
        in_specs=[pl.BlockSpec((tc, M), lambda i: (i, col)), _full(bre.shape), _full(bim.shape), _full(cre.shape),
                  _full(cim.shape), _full(cst.shape), _full((1, M)), _full((M, M)), _full((1, M))],
        out_specs=[rowm, rows, rows, rowm, rowm],
        out_shape=[jax.ShapeDtypeStruct((T, M), BF16), jax.ShapeDtypeStruct((T, NS), F32), jax.ShapeDtypeStruct((T, NS), F32),
                   jax.ShapeDtypeStruct((T, M), F32), jax.ShapeDtypeStruct((T, M), F32)],
        scratch_shapes=[pltpu.VMEM((2, 8, NS), F32)], compiler_params=ARB,
    )(proj, bre, bim, cre, cim, cst, dvec.reshape(1, M), wglu, bglu.reshape(1, M))


def _s5_bwd(dmix, dcol, proj, col, sre, sim, yp, z, bre, bim, cre, cim, cst, dvec, wglu):
    T = proj.shape[0]
    nsb, cw, sw = bre.shape
    M, NS = nsb * cw, nsb * sw
    tc = _pick(T, 512, 8)
    nc = T // tc

    def body(do_ref, u_ref, sre_ref, sim_ref, yp_ref, z_ref, bre_ref, bim_ref, cre_ref, cim_ref, cst_ref, d_ref, wg_ref,
             du_ref, dbre_ref, dbim_ref, dcre_ref, dcim_ref, da_ref, dwg_ref, vec_ref, gre_ref, gim_ref, car_ref):
        @pl.when(pl.program_id(0) == 0)
        def _():
            for r in (dbre_ref, dbim_ref, dcre_ref, dcim_ref, da_ref, dwg_ref, vec_ref, car_ref):
                r[...] = jnp.zeros_like(r)

        u, do, ypv = u_ref[...], do_ref[...], yp_ref[...]
        gl = _gelu(ypv)
        sg = _sigmoid(z_ref[...])
        dz = do * gl * sg * (1.0 - sg)
        dgl = do * sg + _dot_nt(dz, wg_ref[...])
        dwg_ref[...] += _dot_tn(gl, dz)
        vec_ref[0] += _fold8(dz)
        dy = dgl * _gelu_grad(ypv)
        vec_ref[1] += _fold8(dy * u)
        dyb = dy.astype(BF16)
        for k in range(nsb):
            cs, ss = slice(k * cw, (k + 1) * cw), slice(k * sw, (k + 1) * sw)
            gre_ref[:, ss] = _dot_nt(dyb[:, cs], cre_ref[k])
            gim_ref[:, ss] = -_dot_nt(dyb[:, cs], cim_ref[k])
            dcre_ref[k] += _dot_tn(sre_ref[:, ss], dyb[:, cs])
            dcim_ref[k] -= _dot_tn(sim_ref[:, ss], dyb[:, cs])

        row = lax.broadcasted_iota(jnp.int32, (8, sw), 0)

        def extra(r0, sl, gr, gi, cr, ci):
            nr = jnp.where(row == 7, cr, pltpu.roll(gr, 7, 0))
            ni = jnp.where(row == 7, ci, pltpu.roll(gi, 7, 0))
            sr, si = sre_ref[pl.ds(r0, 8), sl], sim_ref[pl.ds(r0, 8), sl]
            da_ref[0, :, sl] += nr * sr + ni * si
            da_ref[1, :, sl] += ni * sr - nr * si

        _s5_block_scan(gre_ref, gim_ref, cst_ref, car_ref, tc // 8, True, sw, extra)
        ub = u.astype(BF16)
        for k in range(nsb):
            cs, ss = slice(k * cw, (k + 1) * cw), slice(k * sw, (k + 1) * sw)
            gr, gi = gre_ref[:, ss].astype(BF16), gim_ref[:, ss].astype(BF16)
            duk = dy[:, cs] * d_ref[:, cs] + _dot_nt(gr, bre_ref[k]) + _dot_nt(gi, bim_ref[k])
            du_ref[:, cs] = duk.astype(BF16)
            dbre_ref[k] += _dot_tn(ub[:, cs], gr)
            dbim_ref[k] += _dot_tn(ub[:, cs], gi)

    def rowm(c):
        return pl.BlockSpec((tc, M), lambda i: (nc - 1 - i, c))
    rows = pl.BlockSpec((tc, NS), lambda i: (nc - 1 - i, 0))
    return pl.pallas_call(
        body, name="s5_bwd", grid=(nc,),
        in_specs=[rowm(dcol), rowm(col), rows, rows, rowm(0), rowm(0), _full(bre.shape), _full(bim.shape), _full(cre.shape),
                  _full(cim.shape), _full(cst.shape), _full((1, M)), _full((M, M))],
        out_specs=[rowm(0), _full(bre.shape), _full(bim.shape), _full(cre.shape), _full(cim.shape), _full((2, 8, NS)),
                   _full((M, M)), _full((3, 8, M))],
        out_shape=[jax.ShapeDtypeStruct((T, M), BF16), jax.ShapeDtypeStruct(bre.shape, F32), jax.ShapeDtypeStruct(bim.shape, F32),
                   jax.ShapeDtypeStruct(cre.shape, F32), jax.ShapeDtypeStruct(cim.shape, F32), jax.ShapeDtypeStruct((2, 8, NS), F32),
                   jax.ShapeDtypeStruct((M, M), F32), jax.ShapeDtypeStruct((3, 8, M), F32)],
        scratch_shapes=[pltpu.VMEM((tc, NS), F32), pltpu.VMEM((tc, NS), F32), pltpu.VMEM((2, 8, NS), F32)],
        compiler_params=ARB,
    )(dmix, proj, sre, sim, yp, z, bre, bim, cre, cim, cst, dvec.reshape(1, M), wglu)


def _ln_silu(h1, lg, lb):
    mu = jnp.mean(h1, axis=-1, keepdims=True)
    xc = h1 - mu
    rstd = lax.rsqrt(jnp.mean(xc * xc, axis=-1, keepdims=True) + EPS)
    xhat = xc * rstd
    h2 = xhat * lg + lb
    return xhat, rstd, h2, _sigmoid(h2)


def _shifted_copies(buf_ref, sh_ref, cs, n):
    for s in range(1, 8):
        sh_ref[s, 0:n, :] = buf_ref[pl.ds(s, n), cs]


def _tap(buf_ref, sh_ref, cs, off, r0, rows):
    q, s = divmod(off, 8)
    return buf_ref[pl.ds(r0 + 8 * q, rows), cs] if s == 0 else sh_ref[s, pl.ds(r0 + 8 * q, rows), :]


def _cv_fwd(proj, vcol, gcol, wdw, bdw, lg, lb, wpw, bpw):
    T = proj.shape[0]
    M = wpw.shape[0]
    tc = _pick(T, 256, 8)
    rb = _pick(tc, 64, 8)
    H, K = CONV_HALO, 31

    def body(v_ref, g_ref, wdw_ref, bdw_ref, lg_ref, lb_ref, wpw_ref, bpw_ref, y_ref, h1_ref, buf_ref, sh_ref):
        @pl.when(pl.program_id(0) == 0)
        def _():
            buf_ref[0:H, :] = jnp.zeros((H, M), F32)

        buf_ref[H:H + tc, :] = v_ref[...] * _sigmoid(g_ref[...])

        def strip(s, _):
            cs = pl.ds(pl.multiple_of(s * LANES, LANES), LANES)
            _shifted_copies(buf_ref, sh_ref, cs, tc + H - 8)
            for r0 in range(0, tc, rb):
                acc = jnp.zeros((rb, LANES), F32) + bdw_ref[:, cs]
                for k in range(K):
                    acc = acc + wdw_ref[k:k + 1, cs] * _tap(buf_ref, sh_ref, cs, H - (K - 1) + k, r0, rb)
                h1_ref[pl.ds(r0, rb), cs] = acc
            return 0

        lax.fori_loop(0, M // LANES, strip, 0)
        _, _, h2, sg = _ln_silu(h1_ref[...], lg_ref[...], lb_ref[...])
        y_ref[...] = (_dot(h2 * sg, wpw_ref[...]) + bpw_ref[...]).astype(BF16)
        buf_ref[0:H, :] = buf_ref[tc:tc + H, :]

    def rowm(c):
        return pl.BlockSpec((tc, M), lambda i: (i, c))
    vec = _full((1, M))
    return pl.pallas_call(
        body, name="cv_fwd", grid=(T // tc,),
        in_specs=[rowm(vcol), rowm(gcol), _full((32, M)), vec, vec, vec, _full((M, M)), vec],
        out_specs=[rowm(0), rowm(0)],
        out_shape=[jax.ShapeDtypeStruct((T, M), BF16), jax.ShapeDtypeStruct((T, M), F32)],
        scratch_shapes=[pltpu.VMEM((tc + H, M), F32), pltpu.VMEM((8, tc + H, LANES), F32)], compiler_params=ARB,
    )(proj, proj, wdw, bdw.reshape(1, M), lg.reshape(1, M), lb.reshape(1, M), wpw, bpw.reshape(1, M))


def _cv_bwd(dmix, dcol, proj, vcol, gcol, h1, wdw, lg, lb, wpw):
    T = proj.shape[0]
    M = wpw.shape[0]
    tc = _pick(T, 256, 8)
    nc = T // tc
    rb = _pick(tc, 32, 16)
    H, K = CONV_HALO, 31
    hb = tc // H

    def body(do_ref, v_ref, g_ref, vh_ref, gh_ref, h1_ref, wdw_ref, lg_ref, lb_ref, wpw_ref,
             dv_ref, dg_ref, dwpw_ref, dwdw_ref, vec_ref, hbuf_ref, dbuf_ref, hsh_ref, dsh_ref):
        i = pl.program_id(0)

        @pl.when(i == 0)
        def _():
            for r in (dwpw_ref, dwdw_ref, vec_ref):
                r[...] = jnp.zeros_like(r)
            dbuf_ref[tc:tc + H, :] = jnp.zeros((H, M), F32)

        halo = vh_ref[...] * _sigmoid(gh_ref[...])
        hbuf_ref[0:H, :] = jnp.where(i == nc - 1, 0.0, halo)
        hbuf_ref[H:H + tc, :] = v_ref[...] * _sigmoid(g_ref[...])
        do = do_ref[...]
        xhat, rstd, h2, sg = _ln_silu(h1_ref[...], lg_ref[...], lb_ref[...])
        dwpw_ref[...] += _dot_tn(h2 * sg, do)
        vec_ref[0] += _fold8(do)
        dh2 = _dot_nt(do, wpw_ref[...]) * (sg * (1.0 + h2 * (1.0 - sg)))
        vec_ref[1] += _fold8(dh2 * xhat)
        vec_ref[2] += _fold8(dh2)
        dxh = dh2 * lg_ref[...]
        dh1 = rstd * (dxh - jnp.mean(dxh, axis=-1, keepdims=True) - xhat * jnp.mean(dxh * xhat, axis=-1, keepdims=True))
        vec_ref[3] += _fold8(dh1)
        dbuf_ref[0:tc, :] = dh1

        def strip(s, _):
            cs = pl.ds(pl.multiple_of(s * LANES, LANES), LANES)
            _shifted_copies(dbuf_ref, dsh_ref, cs, tc + H - 8)
            _shifted_copies(hbuf_ref, hsh_ref, cs, tc + H - 8)
            for r0 in range(0, tc, rb):
                d1 = dbuf_ref[pl.ds(r0, rb), cs]
                dh0 = jnp.zeros((rb, LANES), F32)
                for k in range(K):
                    dh0 = dh0 + wdw_ref[k:k + 1, cs] * _tap(dbuf_ref, dsh_ref, cs, K - 1 - k, r0, rb)
                    dwdw_ref[k, :, cs] += _fold8(d1 * _tap(hbuf_ref, hsh_ref, cs, H - (K - 1) + k, r0, rb))
                sg0 = _sigmoid(g_ref[pl.ds(r0, rb), cs])
                dv_ref[pl.ds(r0, rb), cs] = (dh0 * sg0).astype(BF16)
                dg_ref[pl.ds(r0, rb), cs] = (dh0 * v_ref[pl.ds(r0, rb), cs] * sg0 * (1.0 - sg0)).astype(BF16)
            return 0

        lax.fori_loop(0, M // LANES, strip, 0)
        dbuf_ref[tc:tc + H, :] = dbuf_ref[0:H, :]

    def rowm(c):
        return pl.BlockSpec((tc, M), lambda i: (nc - 1 - i, c))

    def halo(c):
        return pl.BlockSpec((H, M), lambda i: (jnp.maximum((nc - 1 - i) * hb - 1, 0), c))
    vec = _full((1, M))
    return pl.pallas_call(
        body, name="cv_bwd", grid=(nc,),
        in_specs=[rowm(dcol), rowm(vcol), rowm(gcol), halo(vcol), halo(gcol), rowm(0), _full((32, M)), vec, vec, _full((M, M))],
        out_specs=[rowm(0), rowm(0), _full((M, M)), _full((32, 8, M)), _full((5, 8, M))],
        out_shape=[jax.ShapeDtypeStruct((T, M), BF16), jax.ShapeDtypeStruct((T, M), BF16), jax.ShapeDtypeStruct((M, M), F32),
                   jax.ShapeDtypeStruct((32, 8, M), F32), jax.ShapeDtypeStruct((5, 8, M), F32)],
        scratch_shapes=[pltpu.VMEM((tc + H, M), F32), pltpu.VMEM((tc + H, M), F32), pltpu.VMEM((8, tc + H, LANES), F32),
                        pltpu.VMEM((8, tc + H, LANES), F32)], compiler_params=ARB,
    )(dmix, proj, proj, proj, proj, h1, wdw, lg.reshape(1, M), lb.reshape(1, M), wpw)


def _lru_gates(xc, r, ig, sp8):
    la = -r * sp8
    a = jnp.exp(la)
    mult = jnp.sqrt(_neg_expm1(2.0 * la))
    return a, mult, mult * (ig * xc)


def _lru_fwd(proj, xcol, gcol, wcv, bcv, wr, br, wi, bi, sp8):
    T = proj.shape[0]
    M = wr.shape[0]
    tc = _pick(T, 256, 8)
    H, K = SMALL_HALO, 4

    def body(x_ref, g_ref, wcv_ref, bcv_ref, wr_ref, br_ref, wi_ref, bi_ref, sp_ref,
             y_ref, xc_ref, r_ref, i_ref, hp_ref, buf_ref, a_ref, car_ref):
        @pl.when(pl.program_id(0) == 0)
        def _():
            buf_ref[0:H, :] = jnp.zeros((H, M), F32)
            car_ref[...] = jnp.zeros_like(car_ref)

        buf_ref[H:H + tc, :] = x_ref[...]
        xc = jnp.zeros((tc, M), F32) + bcv_ref[...]
        for k in range(K):
            xc = xc + wcv_ref[k:k + 1, :] * buf_ref[pl.ds(H - (K - 1) + k, tc), :]
        buf_ref[0:H, :] = buf_ref[tc:tc + H, :]
        r = _sigmoid(_dot(xc, wr_ref[...]) + br_ref[...])
        ig = _sigmoid(_dot(xc, wi_ref[...]) + bi_ref[...])
        xc_ref[...] = xc
        r_ref[...] = r
        i_ref[...] = ig
        a, _, bt = _lru_gates(xc, r, ig, sp_ref[...])
        a_ref[...] = a
        hp_ref[...] = bt
        row = lax.broadcasted_iota(jnp.int32, (8, M), 0)

        def blk(ib, _):
            r0 = pl.multiple_of(ib * 8, 8)
            av, bv = a_ref[pl.ds(r0, 8), :], hp_ref[pl.ds(r0, 8), :]
            for d in (1, 2, 4):
                m = row >= d
                bv = jnp.where(m, bv + av * pltpu.roll(bv, d, 0), bv)
                av = jnp.where(m, av * pltpu.roll(av, d, 0), av)
            cr = car_ref[7:8, :]
            h = bv + av * cr
            a_ref[pl.ds(r0, 8), :] = h
            hp_ref[pl.ds(r0, 8), :] = jnp.where(row == 0, cr, pltpu.roll(h, 1, 0))
            car_ref[...] = h
            return 0

        lax.fori_loop(0, tc // 8, blk, 0)
        y_ref[...] = (a_ref[...] * _gelu(g_ref[...])).astype(BF16)

    def rowm(c):
        return pl.BlockSpec((tc, M), lambda i: (i, c))
    vec = _full((1, M))
    return pl.pallas_call(
        body, name="lru_fwd", grid=(T // tc,),
        in_specs=[rowm(xcol), rowm(gcol), _full((8, M)), vec, _full((M, M)), vec, _full((M, M)), vec, vec],
        out_specs=[rowm(0)] * 5,
        out_shape=[jax.ShapeDtypeStruct((T, M), BF16)] + [jax.ShapeDtypeStruct((T, M), F32)] * 4,
        scratch_shapes=[pltpu.VMEM((tc + H, M), F32), pltpu.VMEM((tc, M), F32), pltpu.VMEM((8, M), F32)], compiler_params=ARB,
    )(proj, proj, wcv, bcv.reshape(1, M), wr, br.reshape(1, M), wi, bi.reshape(1, M), sp8.reshape(1, M))


def _lru_bwd(dmix, dcol, proj, xcol, gcol, xc, r, ig, hprev, wcv, wr, wi, sp8):
    T = proj.shape[0]
    M = wr.shape[0]
    tc = _pick(T, 256, 8)
    nc = T // tc
    H, K = SMALL_HALO, 4
    hb = tc // H

    def body(do_ref, x_ref, g_ref, xh_ref, xc_ref, r_ref, i_ref, hp_ref, wcv_ref, wr_ref, wi_ref, sp_ref,
             dx_ref, dg_ref, dwr_ref, dwi_ref, dwcv_ref, vec_ref, xbuf_ref, dbuf_ref, a_ref, gs_ref, car_ref):
        i = pl.program_id(0)

        @pl.when(i == 0)
        def _():
            for rf in (dwr_ref, dwi_ref, dwcv_ref, vec_ref, car_ref):
                rf[...] = jnp.zeros_like(rf)
            dbuf_ref[tc:tc + H, :] = jnp.zeros((H, M), F32)

        xcv, rv, iv, hpv, gv, do = xc_ref[...], r_ref[...], i_ref[...], hp_ref[...], g_ref[...], do_ref[...]
        sp = sp_ref[...]
        a, mult, bt = _lru_gates(xcv, rv, iv, sp)
        h = a * hpv + bt
        ge = _gelu(gv)
        dg_ref[...] = (do * h * _gelu_grad(gv)).astype(BF16)
        a_ref[...] = a
        gs_ref[...] = do * ge
        row = lax.broadcasted_iota(jnp.int32, (8, M), 0)

        def blk(n, _):
            r0 = pl.multiple_of((tc // 8 - 1 - n) * 8, 8)
            ablk = a_ref[pl.ds(r0, 8), :]
            av = jnp.where(row == 7, car_ref[0, 0:1, :], pltpu.roll(ablk, 7, 0))
            bv = gs_ref[pl.ds(r0, 8), :]
            for d in (1, 2, 4):
                m = row <= 7 - d
                bv = jnp.where(m, bv + av * pltpu.roll(bv, 8 - d, 0), bv)
                av = jnp.where(m, av * pltpu.roll(av, 8 - d, 0), av)
            gsv = bv + av * car_ref[1, 0:1, :]
            gs_ref[pl.ds(r0, 8), :] = gsv
            car_ref[0] = ablk
            car_ref[1] = gsv
            return 0

        lax.fori_loop(0, tc // 8, blk, 0)
        gs = gs_ref[...]
        dixc = gs * mult
        dla = gs * hpv * a - gs * (iv * xcv) * (a * a) / mult
        vec_ref[2] += _fold8(-dla * rv)
        dpr = (-dla * sp) * rv * (1.0 - rv)
        dpi = (dixc * xcv) * iv * (1.0 - iv)
        vec_ref[0] += _fold8(dpr)
        vec_ref[1] += _fold8(dpi)
        dwr_ref[...] += _dot_tn(xcv, dpr)
        dwi_ref[...] += _dot_tn(xcv, dpi)
        dxc = dixc * iv + _dot_nt(dpr, wr_ref[...]) + _dot_nt(dpi, wi_ref[...])
        vec_ref[3] += _fold8(dxc)
        xbuf_ref[0:H, :] = jnp.where(i == nc - 1, 0.0, xh_ref[...])
        xbuf_ref[H:H + tc, :] = x_ref[...]
        dbuf_ref[0:tc, :] = dxc
        dx = jnp.zeros((tc, M), F32)
        for k in range(K):
            dx = dx + wcv_ref[k:k + 1, :] * dbuf_ref[pl.ds(K - 1 - k, tc), :]
            dwcv_ref[k] += _fold8(dxc * xbuf_ref[pl.ds(H - (K - 1) + k, tc), :])
        dx_ref[...] = dx.astype(BF16)
        dbuf_ref[tc:tc + H, :] = dbuf_ref[0:H, :]

    def rowm(c):
        return pl.BlockSpec((tc, M), lambda i: (nc - 1 - i, c))
    halo = pl.BlockSpec((H, M), lambda i: (jnp.maximum((nc - 1 - i) * hb - 1, 0), xcol))
    vec = _full((1, M))
    return pl.pallas_call(
        body, name="lru_bwd", grid=(nc,),
        in_specs=[rowm(dcol), rowm(xcol), rowm(gcol), halo, rowm(0), rowm(0), rowm(0), rowm(0), _full((8, M)),
                  _full((M, M)), _full((M, M)), vec],
        out_specs=[rowm(0), rowm(0), _full((M, M)), _full((M, M)), _full((8, 8, M)), _full((4, 8, M))],
        out_shape=[jax.ShapeDtypeStruct((T, M), BF16), jax.ShapeDtypeStruct((T, M), BF16), jax.ShapeDtypeStruct((M, M), F32),
                   jax.ShapeDtypeStruct((M, M), F32), jax.ShapeDtypeStruct((8, 8, M), F32), jax.ShapeDtypeStruct((4, 8, M), F32)],
        scratch_shapes=[pltpu.VMEM((tc + H, M), F32), pltpu.VMEM((tc + H, M), F32), pltpu.VMEM((tc, M), F32),
                        pltpu.VMEM((tc, M), F32), pltpu.VMEM((2, 8, M), F32)],
        compiler_params=ARB,
    )(dmix, proj, proj, proj, xc, r, ig, hprev, wcv, wr, wi, sp8.reshape(1, M))


def _pool_diffs(buf_ref, x, t0, tc, gw):
    H = POOL_HALO
    pos = (lax.broadcasted_iota(jnp.int32, (tc, gw), 0) + t0 + 1).astype(F32)
    out, inv = [], []
    for gi, win in enumerate(POOL_WINDOWS):
        sl = slice(gi * gw, (gi + 1) * gw)
        s = x[:, sl]
        for j in range(1, win):
            s = s + buf_ref[pl.ds(H - j, tc), sl]
        ic = 1.0 / jnp.minimum(pos, float(win))
        out.append(s * ic - x[:, sl])
        inv.append(ic)
    return out, inv


def _pool_fwd(proj, col, pw, scale):
    T = proj.shape[0]
    ng, gw, _ = pw.shape
    M = ng * gw
    tc = _pick(T, 256, 8)
    H = POOL_HALO

    def body(x_ref, pw_ref, sc_ref, y_ref, buf_ref):
        @pl.when(pl.program_id(0) == 0)
        def _():
            buf_ref[0:H, :] = jnp.zeros((H, M), F32)

        x = x_ref[...]
        buf_ref[H:H + tc, :] = x
        diffs, _ = _pool_diffs(buf_ref, x, pl.program_id(0) * tc, tc, gw)
        for gi in range(ng):
            sl = slice(gi * gw, (gi + 1) * gw)
            y_ref[:, sl] = (_dot(diffs[gi], pw_ref[gi]) * sc_ref[:, sl]).astype(BF16)
        buf_ref[0:H, :] = buf_ref[tc:tc + H, :]

    return pl.pallas_call(
        body, name="pool_fwd", grid=(T // tc,),
        in_specs=[pl.BlockSpec((tc, M), lambda i: (i, col)), _full(pw.shape), _full((1, M))],
        out_specs=pl.BlockSpec((tc, M), lambda i: (i, 0)), out_shape=jax.ShapeDtypeStruct((T, M), BF16),
        scratch_shapes=[pltpu.VMEM((tc + H, M), F32)], compiler_params=ARB,
    )(proj, pw, scale.reshape(1, M))


def _pool_bwd(dmix, dcol, proj, col, pw, scale):
    T = proj.shape[0]
    ng, gw, _ = pw.shape
    M = ng * gw
    tc = _pick(T, 256, 8)
    nc = T // tc
    H = POOL_HALO
    hb = tc // H

    def body(do_ref, x_ref, xh_ref, pw_ref, sc_ref, dx_ref, dpw_ref, dsc_ref, buf_ref, ebuf_ref):
        i = pl.program_id(0)

        @pl.when(i == 0)
        def _():
            dpw_ref[...] = jnp.zeros_like(dpw_ref)
            dsc_ref[...] = jnp.zeros_like(dsc_ref)
            ebuf_ref[tc:tc + H, :] = jnp.zeros((H, M), F32)

        x, do = x_ref[...], do_ref[...]
        buf_ref[0:H, :] = jnp.where(i == nc - 1, 0.0, xh_ref[...])
        buf_ref[H:H + tc, :] = x
        diffs, inv = _pool_diffs(buf_ref, x, (nc - 1 - i) * tc, tc, gw)
        ddg = []
        for gi in range(ng):
            sl = slice(gi * gw, (gi + 1) * gw)
            dyy = do[:, sl] * sc_ref[:, sl]
            dsc_ref[:, sl] += _fold8(do[:, sl] * _dot(diffs[gi], pw_ref[gi]))
            dpw_ref[gi] += _dot_tn(diffs[gi], dyy)
            d = _dot_nt(dyy, pw_ref[gi])
            ddg.append(d)
            ebuf_ref[0:tc, sl] = d * inv[gi]
        for gi, win in enumerate(POOL_WINDOWS):
            sl = slice(gi * gw, (gi + 1) * gw)
            s = -ddg[gi]
            for j in range(win):
                s = s + ebuf_ref[pl.ds(j, tc), sl]
            dx_ref[:, sl] = s.astype(BF16)
        ebuf_ref[tc:tc + H, :] = ebuf_ref[0:H, :]

    return pl.pallas_call(
        body, name="pool_bwd", grid=(nc,),
        in_specs=[pl.BlockSpec((tc, M), lambda i: (nc - 1 - i, dcol)), pl.BlockSpec((tc, M), lambda i: (nc - 1 - i, col)),
                  pl.BlockSpec((H, M), lambda i: (jnp.maximum((nc - 1 - i) * hb - 1, 0), col)), _full(pw.shape), _full((1, M))],
        out_specs=[pl.BlockSpec((tc, M), lambda i: (nc - 1 - i, 0)), _full(pw.shape), _full((8, M))],
        out_shape=[jax.ShapeDtypeStruct((T, M), BF16), jax.ShapeDtypeStruct(pw.shape, F32), jax.ShapeDtypeStruct((8, M), F32)],
        scratch_shapes=[pltpu.VMEM((tc + H, M), F32), pltpu.VMEM((tc + H, M), F32)], compiler_params=ARB,
    )(dmix, proj, proj, pw, scale.reshape(1, M))


def _ffn_mid_fwd(up, wdw, bdw):
    T, F2 = up.shape
    F = F2 // 2
    tc = _pick(T, 512, 8)
    rb = _pick(tc, 64, 16)
    H, K = FFN_HALO, 3

    def body(gt_ref, val_ref, w_ref, b_ref, act_ref, buf_ref):
        @pl.when(pl.program_id(0) == 0)
        def _():
            buf_ref[0:H, :] = jnp.zeros((H, F), F32)

        buf_ref[H:H + tc, :] = gt_ref[...].astype(F32)

        def strip(s, _):
            cs = pl.ds(pl.multiple_of(s * LANES, LANES), LANES)
            taps = [w_ref[k:k + 1, cs] for k in range(K)]
            for r0 in range(0, tc, rb):
                gc = b_ref[:, cs] + taps[0] * buf_ref[pl.ds(H - (K - 1) + r0, rb), cs]
                for k in range(1, K):
                    gc = gc + taps[k] * buf_ref[pl.ds(H - (K - 1) + k + r0, rb), cs]
                act_ref[pl.ds(r0, rb), cs] = (_gelu(gc) * val_ref[pl.ds(r0, rb), cs].astype(F32)).astype(BF16)
            return 0

        lax.fori_loop(0, F // LANES, strip, 0)
        buf_ref[0:H, :] = buf_ref[tc:tc + H, :]

    return pl.pallas_call(
        body, name="ffn_mid_fwd", grid=(T // tc,),
        in_specs=[pl.BlockSpec((tc, F), lambda i: (i, 0)), pl.BlockSpec((tc, F), lambda i: (i, 1)), _full((8, F)), _full((1, F))],
        out_specs=pl.BlockSpec((tc, F), lambda i: (i, 0)), out_shape=jax.ShapeDtypeStruct((T, F), BF16),
        scratch_shapes=[pltpu.VMEM((tc + H, F), F32)], compiler_params=ARB,
    )(up, up, wdw, bdw.reshape(1, F))


def _ffn_mid_bwd(dact, up, wdw, bdw):
    T, F2 = up.shape
    F = F2 // 2
    tc = _pick(T, 256, 8)
    rb = _pick(tc, 64, 16)
    nc = T // tc
    H, K = FFN_HALO, 3
    hb = tc // H

    def body(da_ref, gt_ref, val_ref, gh_ref, w_ref, b_ref, dup_ref, dw_ref, db_ref, gbuf_ref, dbuf_ref):
        i = pl.program_id(0)

        @pl.when(i == 0)
        def _():
            dw_ref[...] = jnp.zeros_like(dw_ref)
            db_ref[...] = jnp.zeros_like(db_ref)
            dbuf_ref[tc:tc + H, :] = jnp.zeros((H, F), F32)

        gbuf_ref[0:H, :] = jnp.where(i == nc - 1, 0.0, gh_ref[...].astype(F32))
        gbuf_ref[H:H + tc, :] = gt_ref[...].astype(F32)

        def strip(s, _):
            cs = pl.ds(pl.multiple_of(s * LANES, LANES), LANES)
            cs_val = pl.ds(pl.multiple_of(F + s * LANES, LANES), LANES)
            taps = [w_ref[k:k + 1, cs] for k in range(K)]
            for r0 in range(0, tc, rb):
                shifted = [gbuf_ref[pl.ds(H - (K - 1) + k + r0, rb), cs] for k in range(K)]
                gc = b_ref[:, cs] + taps[0] * shifted[0]
                for k in range(1, K):
                    gc = gc + taps[k] * shifted[k]
                ge, gg = _gelu_and_grad(gc)
                da = da_ref[pl.ds(r0, rb), cs].astype(F32)
                dup_ref[pl.ds(r0, rb), cs_val] = (da * ge).astype(BF16)
                dgc = da * val_ref[pl.ds(r0, rb), cs].astype(F32) * gg
                dbuf_ref[pl.ds(r0, rb), cs] = dgc
                db_ref[:, cs] += _fold8(dgc)
                for k in range(K):
                    dw_ref[k, :, cs] += _fold8(dgc * shifted[k])
            for r0 in range(0, tc, rb):
                dgt = taps[0] * dbuf_ref[pl.ds(K - 1 + r0, rb), cs]
                for k in range(1, K):
                    dgt = dgt + taps[k] * dbuf_ref[pl.ds(K - 1 - k + r0, rb), cs]
                dup_ref[pl.ds(r0, rb), cs] = dgt.astype(BF16)
            return 0

        lax.fori_loop(0, F // LANES, strip, 0)
        dbuf_ref[tc:tc + H, :] = dbuf_ref[0:H, :]

    return pl.pallas_call(
        body, name="ffn_mid_bwd", grid=(nc,),
        in_specs=[pl.BlockSpec((tc, F), lambda i: (nc - 1 - i, 0)), pl.BlockSpec((tc, F), lambda i: (nc - 1 - i, 0)),
                  pl.BlockSpec((tc, F), lambda i: (nc - 1 - i, 1)),
                  pl.BlockSpec((H, F), lambda i: (jnp.maximum((nc - 1 - i) * hb - 1, 0), 0)), _full((8, F)), _full((1, F))],
        out_specs=[pl.BlockSpec((tc, F2), lambda i: (nc - 1 - i, 0)), _full((8, 8, F)), _full((8, F))],
        out_shape=[jax.ShapeDtypeStruct((T, F2), BF16), jax.ShapeDtypeStruct((8, 8, F), F32), jax.ShapeDtypeStruct((8, F), F32)],
        scratch_shapes=[pltpu.VMEM((tc + H, F), F32), pltpu.VMEM((tc + H, F), F32)], compiler_params=ARB,
    )(dact, up, up, up, wdw, bdw.reshape(1, F))


def _my_pos():
    return lax.axis_index("x"), lax.axis_index("y"), lax.axis_index("c")


def _flip(pos, k):
    x, y, c = pos
    return ((1 - x) if k & 4 else x, (1 - y) if k & 2 else y, (1 - c) if k & 1 else c)


def _rank(pos):
    return 4 * pos[0] + 2 * pos[1] + pos[2]


_HBM = pl.BlockSpec(memory_space=pltpu.HBM)
_SEM = pl.BlockSpec(memory_space=pltpu.SEMAPHORE)
_EFFECT = pltpu.SideEffectType.DATAFLOW_SIDE_EFFECTING


def _plan_direct(gather):
    def plan(bufs, ssem, rsem, me, outgoing):
        n = len(bufs) // 2
        out = []
        for a in range(n):
            for k in range(N_DEV):
                peer = _flip(me, k)
                out.append(pltpu.make_async_remote_copy(
                    src_ref=bufs[a] if gather else bufs[a].at[_rank(peer)],
                    dst_ref=bufs[n + a].at[_rank(me) if outgoing else _rank(peer)],
                    send_sem=ssem.at[a * N_DEV + k], recv_sem=rsem.at[a * N_DEV + k], device_id=peer,
                    device_id_type=pl.DeviceIdType.MESH))
        return out
    return plan, N_DEV


_SAME_CORE = (2, 4, 6)


def _plan_own_block(bufs, ssem, rsem, me, outgoing):
    n = len(bufs) // 2
    out = []
    for a in range(n):
        for j, k in enumerate((0, 1) + _SAME_CORE):
            peer = _flip(me, k)
            out.append(pltpu.make_async_remote_copy(
                src_ref=bufs[a], dst_ref=bufs[n + a].at[_rank(me) if outgoing else _rank(peer)],
                send_sem=ssem.at[a * 5 + j], recv_sem=rsem.at[a * 5 + j], device_id=peer, device_id_type=pl.DeviceIdType.MESH))
    return out


def _plan_pass_on(bufs, ssem, rsem, me, outgoing):
    sibling = _flip(me, 1)
    out = []
    for a in range(len(bufs)):
        for j, k in enumerate(_SAME_CORE):
            slot = _rank(_flip(me, k) if outgoing else _flip(sibling, k))
            out.append(pltpu.make_async_remote_copy(
                src_ref=bufs[a].at[slot], dst_ref=bufs[a].at[slot], send_sem=ssem.at[a * 3 + j], recv_sem=rsem.at[a * 3 + j],
                device_id=sibling, device_id_type=pl.DeviceIdType.MESH))
    return out


def _split_start(bufs, plan, n_sems, name, after=None):
    nb = len(bufs)
    extra = [] if after is None else [after]

    def body(*refs):
        ssem, rsem = refs[nb + len(extra)], refs[nb + len(extra) + 1]
        for cp in plan(refs[:nb], ssem, rsem, _my_pos(), True):
            cp.start()
        token = refs[2 * nb + len(extra) + 2]
        token[...] = jnp.zeros_like(token)

    res = pl.pallas_call(
        body, name=name,
        out_shape=(pltpu.SemaphoreType.DMA((n_sems,)), pltpu.SemaphoreType.DMA((n_sems,)),
                   *[pltpu.HBM(t.shape, t.dtype) for t in bufs], jax.ShapeDtypeStruct((8, 128), F32)),
        in_specs=[_HBM] * nb + [pl.BlockSpec(memory_space=pl.ANY)] * len(extra),
        out_specs=(_SEM, _SEM, *([_HBM] * nb), pl.BlockSpec(memory_space=pltpu.VMEM)),
        input_output_aliases={i: 2 + i for i in range(nb)},
        compiler_params=pltpu.CompilerParams(has_side_effects=_EFFECT),
    )(*[pltpu.with_memory_space_constraint(t, pltpu.HBM) for t in bufs], *extra)
    return (res[0], res[1], list(res[2:2 + nb])), res[2 + nb]


def _split_wait(handle, plan, after, name):
    ssem, rsem, bufs = handle
    nb = len(bufs)

    def body(*refs):
        for cp in plan(refs[:nb], refs[nb], refs[nb + 1], _my_pos(), False):
            cp.wait_send()
            cp.wait_recv()

    res = pl.pallas_call(
        body, name=name, out_shape=tuple(pltpu.HBM(t.shape, t.dtype) for t in bufs),
        in_specs=[_HBM] * nb + [_SEM, _SEM, pl.BlockSpec(memory_space=pl.ANY)], out_specs=tuple([_HBM] * nb),
        input_output_aliases={i: i for i in range(nb)},
        compiler_params=pltpu.CompilerParams(has_side_effects=_EFFECT),
    )(*bufs, ssem, rsem, after)
    return list(res)


def _landing_zones(xs, gather):
    return [lax.empty(((N_DEV,) + tuple(x.shape)) if gather else tuple(x.shape), x.dtype) for x in xs]


def _gather_start(xs, name, after=None, two_level=True):
    bufs = list(xs) + _landing_zones(xs, True)
    if two_level:
        handle, token = _split_start(bufs, _plan_own_block, 5 * len(xs), name + "_start", after)
    else:
        plan, per = _plan_direct(True)
        handle, token = _split_start(bufs, plan, per * len(xs), name + "_start", after)
    return (handle, two_level, name), token


def _gather_wait(state, after):
    handle, two_level, name = state
    n = len(handle[2]) // 2
    if not two_level:
        return _split_wait(handle, _plan_direct(True)[0], after, name + "_wait")[n:]
    lands = _split_wait(handle, _plan_own_block, after, name + "_wait")[n:]
    handle, _ = _split_start(lands, _plan_pass_on, 3 * n, name + "_pass")
    return _split_wait(handle, _plan_pass_on, after, name + "_passed")


def _scatter_start(xs, name):
    plan, per = _plan_direct(False)
    handle, token = _split_start(list(xs) + _landing_zones(xs, False), plan, per * len(xs), name + "_start")
    return (handle, name), token


def _scatter_wait(state, after):
    handle, name = state
    return _split_wait(handle, _plan_direct(False)[0], after, name + "_wait")[len(handle[2]) // 2:]


def _adamw_math(w, g, m, v):
    m = ADAM_B1 * m + (1.0 - ADAM_B1) * g
    v = ADAM_B2 * v + (1.0 - ADAM_B2) * (g * g)
    m_hat = m / (1.0 - ADAM_B1 ** ADAM_STEP)
    v_hat = v / (1.0 - ADAM_B2 ** ADAM_STEP)
    return -ADAM_LR * (m_hat / (jnp.sqrt(v_hat) + ADAM_EPS) + ADAM_WD * w), m, v


def _sum8(parts, *, name):
    _, R, C = parts.shape
    tr = _pick(R, 256, 16)

    def body(p_ref, o_ref):
        acc = p_ref[0].astype(F32)
        for r in range(1, N_DEV):
            acc = acc + p_ref[r].astype(F32)
        o_ref[...] = acc

    return pl.pallas_call(
        body, name=name, grid=(R // tr,), in_specs=[pl.BlockSpec((N_DEV, tr, C), lambda i: (0, i, 0))],
        out_specs=pl.BlockSpec((tr, C), lambda i: (i, 0)), out_shape=jax.ShapeDtypeStruct((R, C), F32),
        compiler_params=pltpu.CompilerParams(dimension_semantics=("parallel",)),
    )(parts)


def _adamw_layer(g, w, m, v, layer, prev, *, name):
    depth, R, C = w.shape
    tr = _pick(R, 64, 16)
    summed = g.ndim == 3
    if prev is None:
        prev = [lax.empty((depth, R, C), F32) for _ in range(4)]

    def body(g_ref, w_ref, m_ref, v_ref, *rest):
        outs = rest[4:]
        if summed:
            gv = g_ref[0].astype(F32)
            for r in range(1, N_DEV):
                gv = gv + g_ref[r].astype(F32)
        else:
            gv = g_ref[...]
        d, nm, nv = _adamw_math(w_ref[...], gv, m_ref[...], v_ref[...])
        for o, val in zip(outs, (gv, d, nm, nv)):
            o[...] = val

    g_spec = pl.BlockSpec((N_DEV, tr, C), lambda i: (0, i, 0)) if summed else pl.BlockSpec((tr, C), lambda i: (i, 0))
    slab = pl.BlockSpec((None, tr, C), lambda i: (layer, i, 0))
    return pl.pallas_call(
        body, name=name, grid=(R // tr,), in_specs=[g_spec] + [slab] * 3 + [pl.BlockSpec(memory_space=pl.ANY)] * 4,
        out_specs=[slab] * 4, out_shape=[jax.ShapeDtypeStruct((depth, R, C), F32)] * 4,
        input_output_aliases={4 + q: q for q in range(4)},
        compiler_params=pltpu.CompilerParams(dimension_semantics=("parallel",)),
    )(g, w, m, v, *prev)


def _lane_view(shape):
    if len(shape) == 1:
        return (1, shape[0])
    size = math.prod(shape)
    if shape[-1] < 64 and size % LANES == 0:
        return (size // LANES, LANES)
    if len(shape) > 2 and shape[-2] % 8 == 0:
        return (size // shape[-1], shape[-1])
    return tuple(shape)


def _adamw_many(ws, gs, ms, vs, *, name):
    n = len(ws)
    views = [_lane_view(w.shape) for w in ws]

    def body(*refs):
        for i in range(n):
            d, nm, nv = _adamw_math(refs[i][...], refs[n + i][...], refs[2 * n + i][...], refs[3 * n + i][...])
            refs[4 * n + i][...] = d
            refs[5 * n + i][...] = nm
            refs[6 * n + i][...] = nv

    outs = pl.pallas_call(
        body, name=name, out_shape=[jax.ShapeDtypeStruct(s, F32) for s in views] * 3,
    )(*[a.reshape(s) for arrs in (ws, gs, ms, vs) for a, s in zip(arrs, views)])
    return [[o.reshape(w.shape) for o, w in zip(outs[q * n:(q + 1) * n], ws)] for q in range(3)]


def _pack(arrs, dtype=F32):
    flat = jnp.concatenate([a.reshape(-1).astype(dtype) for a in arrs])
    pad = (-flat.shape[0]) % 2048
    return jnp.pad(flat, (0, pad)).reshape(-1, 128)


def _unpack(buf, shapes):
    flat, out, off = buf.reshape(-1), [], 0
    for s in shapes:
        n = math.prod(s)
        out.append(flat[off:off + n].reshape(s))
        off += n
    return out


def _s5_params(lam_re, lam_im, log_step, b_re, b_im):
    G, P, H = b_re.shape
    step = jnp.exp(log_step)[:, None]
    mag = jnp.exp(lam_re * step)
    ar, ai = mag * jnp.cos(lam_im * step), mag * jnp.sin(lam_im * step)
    den = lam_re * lam_re + lam_im * lam_im
    qr = ((ar - 1.0) * lam_re + ai * lam_im) / den
    qi = (ai * lam_re - (ar - 1.0) * lam_im) / den
    bbr = qr[..., None] * b_re - qi[..., None] * b_im
    bbi = qr[..., None] * b_im + qi[..., None] * b_re
    return ar.reshape(-1), ai.reshape(-1), _s5_blocks_in(bbr), _s5_blocks_in(bbi)


def _s5_blocks_in(b):
    G, P, H = b.shape
    eye = jnp.eye(8, dtype=b.dtype)
    return jnp.einsum("kgph,gj->kghjp", b.reshape(G // 8, 8, P, H), eye).reshape(G // 8, 8 * H, 8 * P)


def _s5_blocks_out(c):
    G, H, P = c.shape
    eye = jnp.eye(8, dtype=c.dtype)
    return jnp.einsum("kghp,gj->kgpjh", c.reshape(G // 8, 8, H, P), eye).reshape(G // 8, 8 * P, 8 * H)


def _s5_blocks_out_diag(m, G, H, P):
    return jnp.einsum("kgpgh->kghp", m.reshape(G // 8, 8, P, 8, H)).reshape(G, H, P)


def _head_blocks(w):
    nh, d, e = w.shape
    return jnp.einsum("hde,hk->hdke", w, jnp.eye(nh, dtype=w.dtype)).reshape(nh * d, nh * e)


def _head_blocks_diag(m, nh, d, e):
    return jnp.einsum("hdhe->hde", m.reshape(nh, d, nh, e))


def _pad_rows(w, rows):
    return jnp.pad(w, ((0, rows - w.shape[0]), (0, 0)))


def _unshard_cols(g):
    return jnp.transpose(g, (1, 0, 2)).reshape(g.shape[1], -1)


_NAMES = ['norm_mix_g', 'w_in', 's5_lam_re', 's5_lam_im', 's5_log_step', 's5_b_re', 's5_b_im', 's5_c_re', 's5_c_im', 's5_d',
          's5_w_glu', 's5_b_glu', 'cv_w_dw', 'cv_b_dw', 'cv_ln_g', 'cv_ln_b', 'cv_w_pw', 'cv_b_pw', 'lru_w_conv', 'lru_b_conv',
          'lru_w_r', 'lru_b_r', 'lru_w_i', 'lru_b_i', 'lru_lam', 'pool_w', 'pool_scale', 'w_out', 'norm_ffn_g', 'ffn_w_up',
          'ffn_w_dw', 'ffn_b_dw', 'ffn_w_down', 'norm_final_g']
_MATRIX = ('w_in', 'w_out', 'ffn_w_up', 'ffn_w_down', 's5_w_glu', 'cv_w_pw')
_TRANSPOSED = ('w_in', 'ffn_w_up')
_COLSHARD = ('cv_w_dw', 'lru_w_conv', 'ffn_w_dw')
_GROUPS = (('w_in', 's5_w_glu', 'cv_w_pw', 'w_out'), ('ffn_w_up', 'ffn_w_down'))
_SMALL_BF16 = ('s5_b_re', 's5_b_im', 's5_c_re', 's5_c_im', 'lru_w_r', 'lru_w_i', 'pool_w')
_SCATTER_GROUPS = (('w_out', 's5_w_glu', 'cv_w_pw'), ('w_in',), ('ffn_w_up', 'ffn_w_down'))


def kernel(x, norm_mix_g, w_in, s5_lam_re, s5_lam_im, s5_log_step, s5_b_re, s5_b_im, s5_c_re, s5_c_im, s5_d, s5_w_glu, s5_b_glu, cv_w_dw, cv_b_dw, cv_ln_g, cv_ln_b, cv_w_pw, cv_b_pw, lru_w_conv, lru_b_conv, lru_w_r, lru_b_r, lru_w_i, lru_b_i, lru_lam, pool_w, pool_scale, w_out, norm_ffn_g, ffn_w_up, ffn_w_dw, ffn_b_dw, ffn_w_down, norm_final_g, loss_target, m_norm_mix_g, m_w_in, m_s5_lam_re, m_s5_lam_im, m_s5_log_step, m_s5_b_re, m_s5_b_im, m_s5_c_re, m_s5_c_im, m_s5_d, m_s5_w_glu, m_s5_b_glu, m_cv_w_dw, m_cv_b_dw, m_cv_ln_g, m_cv_ln_b, m_cv_w_pw, m_cv_b_pw, m_lru_w_conv, m_lru_b_conv, m_lru_w_r, m_lru_b_r, m_lru_w_i, m_lru_b_i, m_lru_lam, m_pool_w, m_pool_scale, m_w_out, m_norm_ffn_g, m_ffn_w_up, m_ffn_w_dw, m_ffn_b_dw, m_ffn_w_down, m_norm_final_g, v_norm_mix_g, v_w_in, v_s5_lam_re, v_s5_lam_im, v_s5_log_step, v_s5_b_re, v_s5_b_im, v_s5_c_re, v_s5_c_im, v_s5_d, v_s5_w_glu, v_s5_b_glu, v_cv_w_dw, v_cv_b_dw, v_cv_ln_g, v_cv_ln_b, v_cv_w_pw, v_cv_b_pw, v_lru_w_conv, v_lru_b_conv, v_lru_w_r, v_lru_b_r, v_lru_w_i, v_lru_b_i, v_lru_lam, v_pool_w, v_pool_scale, v_w_out, v_norm_ffn_g, v_ffn_w_up, v_ffn_w_dw, v_ffn_b_dw, v_ffn_w_down, v_norm_final_g):
    args = (x, norm_mix_g, w_in, s5_lam_re, s5_lam_im, s5_log_step, s5_b_re, s5_b_im, s5_c_re, s5_c_im, s5_d, s5_w_glu, s5_b_glu, cv_w_dw, cv_b_dw, cv_ln_g, cv_ln_b, cv_w_pw, cv_b_pw, lru_w_conv, lru_b_conv, lru_w_r, lru_b_r, lru_w_i, lru_b_i, lru_lam, pool_w, pool_scale, w_out, norm_ffn_g, ffn_w_up, ffn_w_dw, ffn_b_dw, ffn_w_down, norm_final_g, loss_target, m_norm_mix_g, m_w_in, m_s5_lam_re, m_s5_lam_im, m_s5_log_step, m_s5_b_re, m_s5_b_im, m_s5_c_re, m_s5_c_im, m_s5_d, m_s5_w_glu, m_s5_b_glu, m_cv_w_dw, m_cv_b_dw, m_cv_ln_g, m_cv_ln_b, m_cv_w_pw, m_cv_b_pw, m_lru_w_conv, m_lru_b_conv, m_lru_w_r, m_lru_b_r, m_lru_w_i, m_lru_b_i, m_lru_lam, m_pool_w, m_pool_scale, m_w_out, m_norm_ffn_g, m_ffn_w_up, m_ffn_w_dw, m_ffn_b_dw, m_ffn_w_down, m_norm_final_g, v_norm_mix_g, v_w_in, v_s5_lam_re, v_s5_lam_im, v_s5_log_step, v_s5_b_re, v_s5_b_im, v_s5_c_re, v_s5_c_im, v_s5_d, v_s5_w_glu, v_s5_b_glu, v_cv_w_dw, v_cv_b_dw, v_cv_ln_g, v_cv_ln_b, v_cv_w_pw, v_cv_b_pw, v_lru_w_conv, v_lru_b_conv, v_lru_w_r, v_lru_b_r, v_lru_w_i, v_lru_b_i, v_lru_lam, v_pool_w, v_pool_scale, v_w_out, v_norm_ffn_g, v_ffn_w_up, v_ffn_w_dw, v_ffn_b_dw, v_ffn_w_down, v_norm_final_g)
    x, target = args[0], args[35]
    W = dict(zip(_NAMES, args[1:35]))
    Mo = dict(zip(_NAMES, args[36:70]))
    Vo = dict(zip(_NAMES, args[70:104]))
    x = x[0]
    target = target[0]
    T, D = x.shape
    depth = W['w_in'].shape[0]
    F = W['ffn_w_down'].shape[1] * N_DEV
    G, P, H = W['s5_b_re'].shape[1:]
    nh, hd, _ = W['lru_w_r'].shape[1:]
    me = _rank(_my_pos())

    gather, full = {}, {}

    def start_group(l, gi, after):
        shards = [(W[n][l].T if n in _TRANSPOSED else W[n][l]).astype(BF16) for n in _GROUPS[gi]]
        if (l, gi) == (0, 0):
            shards.append(_pack([W[n][k] for k in range(depth) for n in _COLSHARD]))
        gather[l, gi], tok = _gather_start(shards, "gather_%d%d" % (l, gi), after)
        return tok

    def land_weights(l, gi, after):
        got = _gather_wait(gather[l, gi], after)
        for n, g8 in zip(_GROUPS[gi], got):
            full[n, l] = g8.reshape(-1, g8.shape[-1])
        return got

    started = start_group(0, 0, None)[0, 0]
    consts = []
    for l in range(depth):
        s5p, s5p_vjp = jax.vjp(_s5_params, W['s5_lam_re'][l], W['s5_lam_im'][l], W['s5_log_step'][l], W['s5_b_re'][l], W['s5_b_im'][l])
        ar, ai, bre, bim = s5p
        cre, cim = _s5_blocks_out(W['s5_c_re'][l]), _s5_blocks_out(W['s5_c_im'][l])
        sp8, sp8_vjp = jax.vjp(lambda lam: LRU_C * jax.nn.softplus(-lam), W['lru_lam'][l])
        wr, wi = _head_blocks(W['lru_w_r'][l]), _head_blocks(W['lru_w_i'][l])
        consts.append(dict(bre=bre.astype(BF16), bim=bim.astype(BF16), cre=cre.astype(BF16), cim=cim.astype(BF16),
                           cst_f=_s5_scan_consts(ar, ai, False), cst_b=_s5_scan_consts(ar, ai, True), sp8=sp8,
                           wr=wr.astype(BF16), wi=wi.astype(BF16), pw=W['pool_w'][l].astype(BF16)))
        consts[-1].update(s5p_vjp=s5p_vjp, sp8_vjp=sp8_vjp)
    ready = sum(c[k].reshape(-1)[0].astype(F32) for c in consts for k in c if not k.endswith('_vjp'))
    got = land_weights(0, 0, (ready + started).reshape(1, 1))
    started = start_group(0, 1, got[0])[0, 0]
    taps8 = got[-1]
    tap_shapes = [W[n][l].shape for l in range(depth) for n in _COLSHARD]
    taps = [_unshard_cols(jnp.stack(t)) for t in zip(*[_unpack(taps8[r], tap_shapes) for r in range(N_DEV)])]
    for l in range(depth):
        for j, n in enumerate(_COLSHARD):
            full[n, l] = taps[l * len(_COLSHARD) + j]

    saved = []
    for l in range(depth):
        g_mix = W['norm_mix_g'][l] + (0.0 if l else started)
        if l:
            land_weights(l, 0, x)
        c = dict(consts[l], cvw=_pad_rows(full['cv_w_dw', l], 32), lruw=_pad_rows(full['lru_w_conv', l], 8),
                 ffw=_pad_rows(full['ffn_w_dw', l], 8), x0=x)
        sp8 = c['sp8']
        h = _rmsnorm(x, g_mix, name="mix_norm")
        proj = _matmul(h, full['w_in', l], mode="nt", tm=1024, tn=1024, tk=D, out_dtype=F32, name="in_proj")
        y0, sre, sim, yp, z = _s5_fwd(proj, 0, c['bre'], c['bim'], c['cre'], c['cim'], c['cst_f'], W['s5_d'][l],
                                     full['s5_w_glu', l], W['s5_b_glu'][l])
        y1, h1 = _cv_fwd(proj, 1, 2, c['cvw'], W['cv_b_dw'][l], W['cv_ln_g'][l], W['cv_ln_b'][l], full['cv_w_pw', l], W['cv_b_pw'][l])
        y2, xc, rr, ig, hprev = _lru_fwd(proj, 3, 4, c['lruw'], W['lru_b_conv'][l], c['wr'], W['lru_b_r'][l], c['wi'],
                                         W['lru_b_i'][l], sp8)
        y3 = _pool_fwd(proj, 5, c['pw'], W['pool_scale'][l])
        mixed = jnp.concatenate([y0, y1, y2, y3], axis=1)
        x1 = _matmul(mixed, full['w_out', l], mode="nn", tm=1024, tn=512, tk=2048, out_dtype=F32, name="out_proj", add=x)
        got = land_weights(l, 1, x1)
        g_ffn = W['norm_ffn_g'][l]
        if l + 1 < depth:
            g_ffn = g_ffn + start_group(l + 1, 0, got[0])[0, 0] + start_group(l + 1, 1, got[0])[0, 0]
        h2 = _rmsnorm(x1, g_ffn, name="ffn_norm")
        up = _matmul(h2, full['ffn_w_up', l], mode="nt", tm=2048, tn=256, tk=D, out_dtype=BF16, name="ffn_up")
        act = _ffn_mid_fwd(up, c['ffw'], W['ffn_b_dw'][l])
        x = _matmul(act, full['ffn_w_down', l], mode="nn", tm=1024, tn=512, tk=F, out_dtype=F32, name="ffn_down", add=x1)
        c.update(proj=proj, h=h, sre=sre, sim=sim, yp=yp, z=z, h1=h1, xc=xc, rr=rr, ig=ig, hprev=hprev, mixed=mixed, x1=x1,
                 up=up, h2=h2, act=act)
        saved.append(c)

    dx, dxb, dgf, se = _loss_head(x, W['norm_final_g'], target)
    loss = lax.psum(0.5 / D * jnp.sum(se), AXES)
    gsmall = {('norm_final_g', None): dgf.sum(0)}
    gmat, scatter = {}, {}

    def parts_of(m):
        return m.reshape(N_DEV, m.shape[0] // N_DEV, m.shape[1])

    for l in reversed(range(depth)):
        c = saved[l]
        dact = _matmul(dxb, full['ffn_w_down', l], mode="nt", tm=512, tn=F, tk=512, out_dtype=BF16, name="d_act")
        gmat['ffn_w_down', l] = _matmul(c['act'], dxb, mode="tn", tm=F, tn=512, tk=512, out_dtype=BF16, name="dw_down")
        dup, dffw, dffb = _ffn_mid_bwd(dact, c['up'], c['ffw'], W['ffn_b_dw'][l])
        gsmall['ffn_w_dw', l] = dffw.sum(1)[:3]
        gsmall['ffn_b_dw', l] = dffb.sum(0)
        dh2 = _matmul(dup, full['ffn_w_up', l], mode="nn", tm=1024, tn=512, tk=F, out_dtype=F32, name="d_h2")
        gmat['ffn_w_up', l] = _matmul(dup, c['h2'], mode="tn", tm=256, tn=D, tk=T, out_dtype=BF16, name="dw_up")
        scatter[l, 2], tok = _scatter_start([parts_of(gmat[n, l]) for n in _SCATTER_GROUPS[2]], "scatter_%d2" % l)
        dx1, dx1b, dg2 = _rms_bwd(dx, dh2, c['x1'], W['norm_ffn_g'][l] + tok[0, 0], name="ffn_norm_bwd")
        gsmall['norm_ffn_g', l] = dg2.sum(0)
        dmix = _matmul(dx1b, full['w_out', l], mode="nt", tm=1024, tn=512, tk=2048, out_dtype=F32, name="d_mixed")
        gmat['w_out', l] = _matmul(c['mixed'], dx1b, mode="tn", tm=512, tn=D, tk=T, out_dtype=BF16, name="dw_out")
        du, dbre, dbim, dcre, dcim, da, dwglu, v3 = _s5_bwd(dmix, 0, c['proj'], 0, c['sre'], c['sim'], c['yp'], c['z'], c['bre'],
                                                            c['bim'], c['cre'], c['cim'], c['cst_b'], W['s5_d'][l], full['s5_w_glu', l])
        da = da.sum(1)
        glr, gli, gls, gbr, gbi = c['s5p_vjp']((da[0], da[1], dbre, dbim))
        gsmall['s5_lam_re', l], gsmall['s5_lam_im', l], gsmall['s5_log_step', l] = glr, gli, gls
        gsmall['s5_b_re', l], gsmall['s5_b_im', l] = gbr, gbi
        gsmall['s5_c_re', l] = _s5_blocks_out_diag(dcre, G, H, P)
        gsmall['s5_c_im', l] = _s5_blocks_out_diag(dcim, G, H, P)
        gsmall['s5_b_glu', l], gsmall['s5_d', l] = v3[0].sum(0), v3[1].sum(0)
        gmat['s5_w_glu', l] = dwglu.astype(BF16)
        dv, dg, dwpw, dcvw, v5 = _cv_bwd(dmix, 1, c['proj'], 1, 2, c['h1'], c['cvw'], W['cv_ln_g'][l], W['cv_ln_b'][l], full['cv_w_pw', l])
        gmat['cv_w_pw', l] = dwpw.astype(BF16)
        gsmall['cv_w_dw', l] = dcvw.sum(1)[:31]
        gsmall['cv_b_pw', l], gsmall['cv_ln_g', l], gsmall['cv_ln_b', l], gsmall['cv_b_dw', l] = [v5[j].sum(0) for j in range(4)]
        scatter[l, 0], tok = _scatter_start([parts_of(gmat[n, l]) for n in _SCATTER_GROUPS[0]], "scatter_%d0" % l)
        tok = tok[0, 0]
        dlx, dlg, dwr, dwi, dlw, v4 = _lru_bwd(dmix, 2, c['proj'], 3, 4, c['xc'], c['rr'], c['ig'], c['hprev'], c['lruw'], c['wr'],
                                               c['wi'], c['sp8'] + tok)
        gsmall['lru_w_r', l], gsmall['lru_w_i', l] = _head_blocks_diag(dwr, nh, hd, hd), _head_blocks_diag(dwi, nh, hd, hd)
        gsmall['lru_w_conv', l] = dlw.sum(1)[:4]
        gsmall['lru_b_r', l], gsmall['lru_b_i', l] = v4[0].sum(0), v4[1].sum(0)
        gsmall['lru_lam', l] = c['sp8_vjp'](v4[2].sum(0))[0]
        gsmall['lru_b_conv', l] = v4[3].sum(0)
        dpx, dpw, dps = _pool_bwd(dmix, 3, c['proj'], 5, c['pw'], W['pool_scale'][l] + tok)
        gsmall['pool_w', l], gsmall['pool_scale', l] = dpw, dps.sum(0)
        dproj = jnp.concatenate([du, dv, dg, dlx, dlg, dpx], axis=1)
        dh = _matmul(dproj, full['w_in', l], mode="nn", tm=1024, tn=512, tk=dproj.shape[1], out_dtype=F32, name="d_h")
        gmat['w_in', l] = _matmul(dproj, c['h'], mode="tn", tm=512, tn=D, tk=T, out_dtype=BF16, name="dw_in")
        scatter[l, 1], tok = _scatter_start([parts_of(gmat[n, l]) for n in _SCATTER_GROUPS[1]], "scatter_%d1" % l)
        dx, dxb, dg1 = _rms_bwd(dx1, dh, c['x0'], W['norm_mix_g'][l] + tok[0, 0], name="mix_norm_bwd")
        gsmall['norm_mix_g', l] = dg1.sum(0)

    small_names = [n for n in _NAMES if n not in _MATRIX]
    small_split = ([n for n in small_names if n not in _SMALL_BF16], [n for n in small_names if n in _SMALL_BF16])
    small_packs, full_shapes = [], []
    for names, dtype in zip(small_split, (F32, BF16)):
        layers = [[None] if n == 'norm_final_g' else range(depth) for n in names]
        small_packs.append(_pack([gsmall[n, l] for n, ls in zip(names, layers) for l in ls], dtype))
        full_shapes.append([gsmall[n, None].shape if n == 'norm_final_g' else (depth,) + gsmall[n, 0].shape for n in names])
    small_state, tok = _gather_start(small_packs, "small_grads")

    stored_t = [n for n in _TRANSPOSED if W[n].shape[-1] % LANES]
    wmv = {n: [jnp.swapaxes(t[n], 1, 2) if n in stored_t else t[n] for t in (W, Mo, Vo)] for n in _MATRIX}
    stacked = {n: None for n in _MATRIX}

    def update_matrices(l, after):
        for gi in (2, 0, 1):
            for n, p8 in zip(_SCATTER_GROUPS[gi], _scatter_wait(scatter[l, gi], after)):
                g = _sum8(p8, name="sum_%s_%d" % (n, l)).T if n in _TRANSPOSED and n not in stored_t else p8
                stacked[n] = _adamw_layer(g, *wmv[n], l, stacked[n], name="adamw_%s_%d" % (n, l))

    after = tok
    for l in reversed(range(depth)):
        update_matrices(l, after)
        after = sum(stacked[n][1][l, 0, 0] for n in _MATRIX).reshape(1, 1)
    after_matrices = after
    gs = {}
    for names, shapes, small8 in zip(small_split, full_shapes, _gather_wait(small_state, after_matrices)):
        gs.update(zip(names, _unpack(_sum8(small8, name="sum_small_" + str(small8.dtype)), shapes)))
    for n in _COLSHARD:
        cw = W[n].shape[-1]
        gs[n] = lax.dynamic_slice_in_dim(gs[n], me * cw, cw, axis=gs[n].ndim - 1)
    upd = _adamw_many([W[n] for n in small_names], [gs[n] for n in small_names], [Mo[n] for n in small_names],
                      [Vo[n] for n in small_names], name="adamw_small")
    out = {n: (gs[n], upd[0][j], upd[1][j], upd[2][j]) for j, n in enumerate(small_names)}
    for n in _MATRIX:
        out[n] = tuple(jnp.swapaxes(t, 1, 2) for t in stacked[n]) if n in stored_t else tuple(stacked[n])
    return (loss, dx[None]) + tuple(out[n][q] for q in range(4) for n in _NAMES)
```

```python
import math

import jax
import jax.numpy as jnp
from jax import lax
from jax.experimental import pallas as pl
from jax.experimental.pallas import tpu as pltpu

F32 = jnp.float32
BF16 = jnp.bfloat16
N_DEV = 8
AXES = ("x", "y", "c")
EPS = 1e-6
LRU_C = 8.0
POOL_WINDOWS = (2, 4, 8, 16)
CONV_HALO = 32
SMALL_HALO = 8
FFN_HALO = 16
POOL_HALO = 16
LANES = 128
ADAM_LR, ADAM_B1, ADAM_B2, ADAM_EPS, ADAM_WD, ADAM_STEP = 0.001, 0.9, 0.999, 1e-08, 0.01, 10
_GELU_K = math.sqrt(2.0 / math.pi)
ARB = pltpu.CompilerParams(dimension_semantics=("arbitrary",))


def _pick(n, pref, mult):
    best = None
    for t in range(mult, min(n, pref) + 1, mult):
        if n % t == 0:
            best = t
    return best if best is not None else n


def _sigmoid(x):
    return 1.0 / (1.0 + jnp.exp(-x))


def _gelu(x):
    return 0.5 * x * (1.0 + jnp.tanh(_GELU_K * (x + 0.044715 * x * x * x)))


def _gelu_grad(x):
    t = jnp.tanh(_GELU_K * (x + 0.044715 * x * x * x))
    return 0.5 * (1.0 + t) + 0.5 * x * (1.0 - t * t) * _GELU_K * (1.0 + 3.0 * 0.044715 * x * x)


def _gelu_and_grad(x):
    t = jnp.tanh(_GELU_K * (x + 0.044715 * x * x * x))
    half = 0.5 * (1.0 + t)
    return x * half, half + 0.5 * x * (1.0 - t * t) * _GELU_K * (1.0 + 3.0 * 0.044715 * x * x)


def _neg_expm1(x):
    series = -x * (1.0 + x * (0.5 + x * (1.0 / 6.0 + x * (1.0 / 24.0 + x * (1.0 / 120.0 + x * (1.0 / 720.0))))))
    return jnp.where(jnp.abs(x) < 0.25, series, 1.0 - jnp.exp(x))


def _fold8(x):
    return x.reshape(x.shape[0] // 8, 8, x.shape[1]).sum(axis=0)


def _dot(a, b):
    return jnp.dot(a.astype(BF16), b.astype(BF16), preferred_element_type=F32)


def _dot_nt(a, b):
    return lax.dot_general(a.astype(BF16), b.astype(BF16), (((1,), (1,)), ((), ())), preferred_element_type=F32)


def _dot_tn(a, b):
    return lax.dot_general(a.astype(BF16), b.astype(BF16), (((0,), (0,)), ((), ())), preferred_element_type=F32)


def _full(shape):
    nd = len(shape)
    return pl.BlockSpec(shape, lambda *i: (0,) * nd)


def _matmul(a, b, *, mode, tm, tn, tk, out_dtype, name, add=None):
    if mode == "nn":
        (M, K), N = a.shape, b.shape[1]
    elif mode == "nt":
        (M, K), N = a.shape, b.shape[0]
    else:
        (K, M), N = a.shape, b.shape[1]
    tm, tn, tk = _pick(M, tm, 8), _pick(N, tn, 128), _pick(K, tk, 128)
    if mode == "tn":
        tm = _pick(M, tm, 128)
    nk = K // tk
    a_spec = {"nn": pl.BlockSpec((tm, tk), lambda i, j, k: (i, k)), "nt": pl.BlockSpec((tm, tk), lambda i, j, k: (i, k)),
              "tn": pl.BlockSpec((tk, tm), lambda i, j, k: (k, i))}[mode]
    b_spec = {"nn": pl.BlockSpec((tk, tn), lambda i, j, k: (k, j)), "nt": pl.BlockSpec((tn, tk), lambda i, j, k: (j, k)),
              "tn": pl.BlockSpec((tk, tn), lambda i, j, k: (k, j))}[mode]
    o_spec = pl.BlockSpec((tm, tn), lambda i, j, k: (i, j))
    dot = {"nn": _dot, "nt": _dot_nt, "tn": _dot_tn}[mode]
    has_add = add is not None

    def body(*refs):
        if has_add:
            a_ref, b_ref, add_ref, o_ref, acc_ref = refs
        else:
            a_ref, b_ref, o_ref, acc_ref = refs
        if nk == 1:
            part = dot(a_ref[...], b_ref[...])
            o_ref[...] = (part + add_ref[...] if has_add else part).astype(out_dtype)
            return
        k = pl.program_id(2)

        @pl.when(k == 0)
        def _():
            acc_ref[...] = jnp.zeros_like(acc_ref)

        acc_ref[...] += dot(a_ref[...], b_ref[...])

        @pl.when(k == nk - 1)
        def _():
            r = acc_ref[...]
            if has_add:
                r = r + add_ref[...]
            o_ref[...] = r.astype(out_dtype)

    ins = [a, b] + ([add] if has_add else [])
    specs = [a_spec, b_spec] + ([o_spec] if has_add else [])
    return pl.pallas_call(
        body, name=name, grid=(M // tm, N // tn, nk), in_specs=specs, out_specs=o_spec,
        out_shape=jax.ShapeDtypeStruct((M, N), out_dtype),
        scratch_shapes=[pltpu.VMEM((tm, tn) if nk > 1 else (8, LANES), F32)],
        compiler_params=pltpu.CompilerParams(dimension_semantics=("parallel", "parallel", "arbitrary")),
    )(*ins)


def _rmsnorm(x, g, *, name):
    T, D = x.shape
    tc = _pick(T, 512, 16)

    def body(x_ref, g_ref, h_ref):
        def rows16(n, _):
            rows = pl.ds(pl.multiple_of(n * 16, 16), 16)
            xv = x_ref[rows, :]
            r = lax.rsqrt(jnp.mean(xv * xv, axis=-1, keepdims=True) + EPS)
            h_ref[rows, :] = (xv * r * g_ref[...]).astype(BF16)
            return 0

        lax.fori_loop(0, tc // 16, rows16, 0, unroll=4)

    row = pl.BlockSpec((tc, D), lambda i: (i, 0))
    return pl.pallas_call(
        body, name=name, grid=(T // tc,), in_specs=[row, _full((1, D))], out_specs=row,
        out_shape=jax.ShapeDtypeStruct((T, D), BF16), compiler_params=pltpu.CompilerParams(dimension_semantics=("parallel",)),
    )(x, g.reshape(1, D))


def _rms_bwd(dres, dh, x, g, *, name):
    T, D = x.shape
    tc = _pick(T, 512, 16)

    def body(dres_ref, dh_ref, x_ref, g_ref, dx_ref, dxb_ref, dg_ref):
        @pl.when(pl.program_id(0) == 0)
        def _():
            dg_ref[...] = jnp.zeros_like(dg_ref)

        def rows16(n, _):
            rows = pl.ds(pl.multiple_of(n * 16, 16), 16)
            xv, dhv = x_ref[rows, :], dh_ref[rows, :]
            r = lax.rsqrt(jnp.mean(xv * xv, axis=-1, keepdims=True) + EPS)
            dg_ref[...] += _fold8(dhv * xv * r)
            dyg = dhv * g_ref[...]
            dx = dres_ref[rows, :] + r * dyg - xv * (r * r * r) * jnp.mean(dyg * xv, axis=-1, keepdims=True)
            dx_ref[rows, :] = dx
            dxb_ref[rows, :] = dx.astype(BF16)
            return 0

        lax.fori_loop(0, tc // 16, rows16, 0, unroll=4)

    row = pl.BlockSpec((tc, D), lambda i: (i, 0))
    return pl.pallas_call(
        body, name=name, grid=(T // tc,), in_specs=[row, row, row, _full((1, D))],
        out_specs=[row, row, _full((8, D))],
        out_shape=[jax.ShapeDtypeStruct((T, D), F32), jax.ShapeDtypeStruct((T, D), BF16), jax.ShapeDtypeStruct((8, D), F32)],
        compiler_params=ARB,
    )(dres, dh, x, g.reshape(1, D))


def _loss_head(x, g, target):
    T, D = x.shape
    tc = _pick(T, 256, 8)

    def body(x_ref, g_ref, t_ref, dx_ref, dxb_ref, dg_ref, se_ref):
        @pl.when(pl.program_id(0) == 0)
        def _():
            dg_ref[...] = jnp.zeros_like(dg_ref)
            se_ref[...] = jnp.zeros_like(se_ref)

        def rows16(n, _):
            rows = pl.ds(pl.multiple_of(n * 16, 16), 16)
            xv = x_ref[rows, :]
            r = lax.rsqrt(jnp.mean(xv * xv, axis=-1, keepdims=True) + EPS)
            gv = g_ref[...]
            err = xv * r * gv - t_ref[rows, :]
            se_ref[...] += _fold8(err * err)
            dy = err * (1.0 / D)
            dg_ref[...] += _fold8(dy * xv * r)
            dyg = dy * gv
            dx = r * dyg - xv * (r * r * r) * jnp.mean(dyg * xv, axis=-1, keepdims=True)
            dx_ref[rows, :] = dx
            dxb_ref[rows, :] = dx.astype(BF16)
            return 0

        lax.fori_loop(0, tc // 16, rows16, 0, unroll=4)

    row = pl.BlockSpec((tc, D), lambda i: (i, 0))
    return pl.pallas_call(
        body, name="loss_head", grid=(T // tc,), in_specs=[row, _full((1, D)), row],
        out_specs=[row, row, _full((8, D)), _full((8, D))],
        out_shape=[jax.ShapeDtypeStruct((T, D), F32), jax.ShapeDtypeStruct((T, D), BF16),
                   jax.ShapeDtypeStruct((8, D), F32), jax.ShapeDtypeStruct((8, D), F32)],
        compiler_params=ARB,
    )(x, g.reshape(1, D), target)


def _s5_scan_consts(ar, ai, reverse):
    if reverse:
        ai = -ai
    pr, pi_ = [ar], [ai]
    for _ in range(7):
        pr, pi_ = pr + [pr[-1] * ar - pi_[-1] * ai], pi_ + [pr[-1] * ai + pi_[-1] * ar]
    rows = jnp.arange(8)[:, None]
    planes = []
    for d in (1, 2, 4):
        mask = (rows <= 7 - d) if reverse else (rows >= d)
        planes += [jnp.where(mask, pr[d - 1][None, :], 0.0), jnp.where(mask, pi_[d - 1][None, :], 0.0)]
    order = list(range(7, -1, -1)) if reverse else list(range(8))
    planes += [jnp.stack([pr[k] for k in order]), jnp.stack([pi_[k] for k in order])]
    return jnp.stack(planes).astype(F32)


def _s5_block_scan(xr_ref, xi_ref, cst_ref, car_ref, nblk, reverse, lb, extra=None):
    ns = xr_ref.shape[1]
    crow = 0 if reverse else 7

    def blk(i, _):
        ib = (nblk - 1 - i) if reverse else i
        r0 = pl.multiple_of(ib * 8, 8)
        for k in range(ns // lb):
            sl = slice(k * lb, (k + 1) * lb)
            xr, xi = xr_ref[pl.ds(r0, 8), sl], xi_ref[pl.ds(r0, 8), sl]
            for n, d in enumerate((1, 2, 4)):
                ar, ai = cst_ref[2 * n, :, sl], cst_ref[2 * n + 1, :, sl]
                sh = (8 - d) if reverse else d
                rr, ri = pltpu.roll(xr, sh, 0), pltpu.roll(xi, sh, 0)
                xr, xi = xr + ar * rr - ai * ri, xi + ar * ri + ai * rr
            pr, pi_ = cst_ref[6, :, sl], cst_ref[7, :, sl]
            cr, ci = car_ref[0, crow:crow + 1, sl], car_ref[1, crow:crow + 1, sl]
            xr, xi = xr + pr * cr - pi_ * ci, xi + pr * ci + pi_ * cr
            xr_ref[pl.ds(r0, 8), sl] = xr
            xi_ref[pl.ds(r0, 8), sl] = xi
            if extra is not None:
                extra(r0, sl, xr, xi, cr, ci)
            car_ref[0, :, sl] = xr
            car_ref[1, :, sl] = xi
        return 0

    lax.fori_loop(0, nblk, blk, 0)


def _s5_fwd(proj, col, bre, bim, cre, cim, cst, dvec, wglu, bglu):
    T = proj.shape[0]
    nsb, cw, sw = bre.shape
    M, NS = nsb * cw, nsb * sw
    tc = _pick(T, 512, 8)

    def body(u_ref, bre_ref, bim_ref, cre_ref, cim_ref, cst_ref, d_ref, wg_ref, bg_ref,
             y_ref, sre_ref, sim_ref, yp_ref, z_ref, car_ref):
        @pl.when(pl.program_id(0) == 0)
        def _():
            car_ref[...] = jnp.zeros_like(car_ref)

        u = u_ref[...]
        ub = u.astype(BF16)
        for k in range(nsb):
            sre_ref[:, k * sw:(k + 1) * sw] = _dot(ub[:, k * cw:(k + 1) * cw], bre_ref[k])
            sim_ref[:, k * sw:(k + 1) * sw] = _dot(ub[:, k * cw:(k + 1) * cw], bim_ref[k])
        _s5_block_scan(sre_ref, sim_ref, cst_ref, car_ref, tc // 8, False, sw)
        for k in range(nsb):
            yk = _dot(sre_ref[:, k * sw:(k + 1) * sw], cre_ref[k]) - _dot(sim_ref[:, k * sw:(k + 1) * sw], cim_ref[k])
            yp_ref[:, k * cw:(k + 1) * cw] = yk + d_ref[:, k * cw:(k + 1) * cw] * u[:, k * cw:(k + 1) * cw]
        gl = _gelu(yp_ref[...])
        z = _dot(gl, wg_ref[...]) + bg_ref[...]
        z_ref[...] = z
        y_ref[...] = (gl * _sigmoid(z)).astype(BF16)

    rowm = pl.BlockSpec((tc, M), lambda i: (i, 0))
    rows = pl.BlockSpec((tc, NS), lambda i: (i, 0))
    return pl.pallas_call(
        body, name="s5_fwd", grid=(T // tc,),
        in_specs=[pl.BlockSpec((tc, M), lambda i: (i, col)), _full(bre.shape), _full(bim.shape), _full(cre.shape),
                  _full(cim.shape), _full(cst.shape), _full((1, M)), _full((M, M)), _full((1, M))],
        out_specs=[rowm, rows, rows, rowm, rowm],
        out_shape=[jax.ShapeDtypeStruct((T, M), BF16), jax.ShapeDtypeStruct((T, NS), F32), jax.ShapeDtypeStruct((T, NS), F32),
                   jax.ShapeDtypeStruct((T, M), F32), jax.ShapeDtypeStruct((T, M), F32)],
        scratch_shapes=[pltpu.VMEM((2, 8, NS), F32)], compiler_params=ARB,
    )(proj, bre, bim, cre, cim, cst, dvec.reshape(1, M), wglu, bglu.reshape(1, M))


def _s5_bwd(dmix, dcol, proj, col, sre, sim, yp, z, bre, bim, cre, cim, cst, dvec, wglu):
    T = proj.shape[0]
    nsb, cw, sw = bre.shape
    M, NS = nsb * cw, nsb * sw
    tc = _pick(T, 512, 8)
    nc = T // tc

    def body(do_ref, u_ref, sre_ref, sim_ref, yp_ref, z_ref, bre_ref, bim_ref, cre_ref, cim_ref, cst_ref, d_ref, wg_ref,
             du_ref, dbre_ref, dbim_ref, dcre_ref, dcim_ref, da_ref, dwg_ref, vec_ref, gre_ref, gim_ref, car_ref):
        @pl.when(pl.program_id(0) == 0)
        def _():
            for r in (dbre_ref, dbim_ref, dcre_ref, dcim_ref, da_ref, dwg_ref, vec_ref, car_ref):
                r[...] = jnp.zeros_like(r)

        u, do, ypv = u_ref[...], do_ref[...], yp_ref[...]
        gl = _gelu(ypv)
        sg = _sigmoid(z_ref[...])
        dz = do * gl * sg * (1.0 - sg)
        dgl = do * sg + _dot_nt(dz, wg_ref[...])
        dwg_ref[...] += _dot_tn(gl, dz)
        vec_ref[0] += _fold8(dz)
        dy = dgl * _gelu_grad(ypv)
        vec_ref[1] += _fold8(dy * u)
        dyb = dy.astype(BF16)
        for k in range(nsb):
            cs, ss = slice(k * cw, (k + 1) * cw), slice(k * sw, (k + 1) * sw)
            gre_ref[:, ss] = _dot_nt(dyb[:, cs], cre_ref[k])
            gim_ref[:, ss] = -_dot_nt(dyb[:, cs], cim_ref[k])
            dcre_ref[k] += _dot_tn(sre_ref[:, ss], dyb[:, cs])
            dcim_ref[k] -= _dot_tn(sim_ref[:, ss], dyb[:, cs])

        row = lax.broadcasted_iota(jnp.int32, (8, sw), 0)

        def extra(r0, sl, gr, gi, cr, ci):
            nr = jnp.where(row == 7, cr, pltpu.roll(gr, 7, 0))
            ni = jnp.where(row == 7, ci, pltpu.roll(gi, 7, 0))
            sr, si = sre_ref[pl.ds(r0, 8), sl], sim_ref[pl.ds(r0, 8), sl]
            da_ref[0, :, sl] += nr * sr + ni * si
            da_ref[1, :, sl] += ni * sr - nr * si

        _s5_block_scan(gre_ref, gim_ref, cst_ref, car_ref, tc // 8, True, sw, extra)
        ub = u.astype(BF16)
        for k in range(nsb):
            cs, ss = slice(k * cw, (k + 1) * cw), slice(k * sw, (k + 1) * sw)
            gr, gi = gre_ref[:, ss].astype(BF16), gim_ref[:, ss].astype(BF16)
            duk = dy[:, cs] * d_ref[:, cs] + _dot_nt(gr, bre_ref[k]) + _dot_nt(gi, bim_ref[k])
            du_ref[:, cs] = duk.astype(BF16)
            dbre_ref[k] += _dot_tn(ub[:, cs], gr)
            dbim_ref[k] += _dot_tn(ub[:, cs], gi)

    def rowm(c):
        return pl.BlockSpec((tc, M), lambda i: (nc - 1 - i, c))
    rows = pl.BlockSpec((tc, NS), lambda i: (nc - 1 - i, 0))
    return pl.pallas_call(
        body, name="s5_bwd", grid=(nc,),
        in_specs=[rowm(dcol), rowm(col), rows, rows, rowm(0), rowm(0), _full(bre.shape), _full(bim.shape), _full(cre.shape),
                  _full(cim.shape), _full(cst.shape), _full((1, M)), _full((M, M))],
        out_specs=[rowm(0), _full(bre.shape), _full(bim.shape), _full(cre.shape), _full(cim.shape), _full((2, 8, NS)),
                   _full((M, M)), _full((3, 8, M))],
        out_shape=[jax.ShapeDtypeStruct((T, M), BF16), jax.ShapeDtypeStruct(bre.shape, F32), jax.ShapeDtypeStruct(bim.shape, F32),
                   jax.ShapeDtypeStruct(cre.shape, F32), jax.ShapeDtypeStruct(cim.shape, F32), jax.ShapeDtypeStruct((2, 8, NS), F32),
                   jax.ShapeDtypeStruct((M, M), F32), jax.ShapeDtypeStruct((3, 8, M), F32)],
        scratch_shapes=[pltpu.VMEM((tc, NS), F32), pltpu.VMEM((tc, NS), F32), pltpu.VMEM((2, 8, NS), F32)],
        compiler_params=ARB,
    )(dmix, proj, sre, sim, yp, z, bre, bim, cre, cim, cst, dvec.reshape(1, M), wglu)


def _ln_silu(h1, lg, lb):
    mu = jnp.mean(h1, axis=-1, keepdims=True)
    xc = h1 - mu
    rstd = lax.rsqrt(jnp.mean(xc * xc, axis=-1, keepdims=True) + EPS)
    xhat = xc * rstd
    h2 = xhat * lg + lb
    return xhat, rstd, h2, _sigmoid(h2)


def _shifted_copies(buf_ref, sh_ref, cs, n):
    for s in range(1, 8):
        sh_ref[s, 0:n, :] = buf_ref[pl.ds(s, n), cs]


def _tap(buf_ref, sh_ref, cs, off, r0, rows):
    q, s = divmod(off, 8)
    return buf_ref[pl.ds(r0 + 8 * q, rows), cs] if s == 0 else sh_ref[s, pl.ds(r0 + 8 * q, rows), :]


def _cv_fwd(proj, vcol, gcol, wdw, bdw, lg, lb, wpw, bpw):
    T = proj.shape[0]
    M = wpw.shape[0]
    tc = _pick(T, 256, 8)
    rb = _pick(tc, 64, 8)
    H, K = CONV_HALO, 31

    def body(v_ref, g_ref, wdw_ref, bdw_ref, lg_ref, lb_ref, wpw_ref, bpw_ref, y_ref, h1_ref, buf_ref, sh_ref):
        @pl.when(pl.program_id(0) == 0)
        def _():
            buf_ref[0:H, :] = jnp.zeros((H, M), F32)

        buf_ref[H:H + tc, :] = v_ref[...] * _sigmoid(g_ref[...])

        def strip(s, _):
            cs = pl.ds(pl.multiple_of(s * LANES, LANES), LANES)
            _shifted_copies(buf_ref, sh_ref, cs, tc + H - 8)
            for r0 in range(0, tc, rb):
                acc = jnp.zeros((rb, LANES), F32) + bdw_ref[:, cs]
                for k in range(K):
                    acc = acc + wdw_ref[k:k + 1, cs] * _tap(buf_ref, sh_ref, cs, H - (K - 1) + k, r0, rb)
                h1_ref[pl.ds(r0, rb), cs] = acc
            return 0

        lax.fori_loop(0, M // LANES, strip, 0)
        _, _, h2, sg = _ln_silu(h1_ref[...], lg_ref[...], lb_ref[...])
        y_ref[...] = (_dot(h2 * sg, wpw_ref[...]) + bpw_ref[...]).astype(BF16)
        buf_ref[0:H, :] = buf_ref[tc:tc + H, :]

    def rowm(c):
        return pl.BlockSpec((tc, M), lambda i: (i, c))
    vec = _full((1, M))
    return pl.pallas_call(
        body, name="cv_fwd", grid=(T // tc,),
        in_specs=[rowm(vcol), rowm(gcol), _full((32, M)), vec, vec, vec, _full((M, M)), vec],
        out_specs=[rowm(0), rowm(0)],
        out_shape=[jax.ShapeDtypeStruct((T, M), BF16), jax.ShapeDtypeStruct((T, M), F32)],
        scratch_shapes=[pltpu.VMEM((tc + H, M), F32), pltpu.VMEM((8, tc + H, LANES), F32)], compiler_params=ARB,
    )(proj, proj, wdw, bdw.reshape(1, M), lg.reshape(1, M), lb.reshape(1, M), wpw, bpw.reshape(1, M))


def _cv_bwd(dmix, dcol, proj, vcol, gcol, h1, wdw, lg, lb, wpw):
    T = proj.shape[0]
    M = wpw.shape[0]
    tc = _pick(T, 256, 8)
    nc = T // tc
    rb = _pick(tc, 32, 16)
    H, K = CONV_HALO, 31
    hb = tc // H

    def body(do_ref, v_ref, g_ref, vh_ref, gh_ref, h1_ref, wdw_ref, lg_ref, lb_ref, wpw_ref,
             dv_ref, dg_ref, dwpw_ref, dwdw_ref, vec_ref, hbuf_ref, dbuf_ref, hsh_ref, dsh_ref):
        i = pl.program_id(0)

        @pl.when(i == 0)
        def _():
            for r in (dwpw_ref, dwdw_ref, vec_ref):
                r[...] = jnp.zeros_like(r)
            dbuf_ref[tc:tc + H, :] = jnp.zeros((H, M), F32)

        halo = vh_ref[...] * _sigmoid(gh_ref[...])
        hbuf_ref[0:H, :] = jnp.where(i == nc - 1, 0.0, halo)
        hbuf_ref[H:H + tc, :] = v_ref[...] * _sigmoid(g_ref[...])
        do = do_ref[...]
        xhat, rstd, h2, sg = _ln_silu(h1_ref[...], lg_ref[...], lb_ref[...])
        dwpw_ref[...] += _dot_tn(h2 * sg, do)
        vec_ref[0] += _fold8(do)
        dh2 = _dot_nt(do, wpw_ref[...]) * (sg * (1.0 + h2 * (1.0 - sg)))
        vec_ref[1] += _fold8(dh2 * xhat)
        vec_ref[2] += _fold8(dh2)
        dxh = dh2 * lg_ref[...]
        dh1 = rstd * (dxh - jnp.mean(dxh, axis=-1, keepdims=True) - xhat * jnp.mean(dxh * xhat, axis=-1, keepdims=True))
        vec_ref[3] += _fold8(dh1)
        dbuf_ref[0:tc, :] = dh1

        def strip(s, _):
            cs = pl.ds(pl.multiple_of(s * LANES, LANES), LANES)
            _shifted_copies(dbuf_ref, dsh_ref, cs, tc + H - 8)
            _shifted_copies(hbuf_ref, hsh_ref, cs, tc + H - 8)
            for r0 in range(0, tc, rb):
                d1 = dbuf_ref[pl.ds(r0, rb), cs]
                dh0 = jnp.zeros((rb, LANES), F32)
                for k in range(K):
                    dh0 = dh0 + wdw_ref[k:k + 1, cs] * _tap(dbuf_ref, dsh_ref, cs, K - 1 - k, r0, rb)
                    dwdw_ref[k, :, cs] += _fold8(d1 * _tap(hbuf_ref, hsh_ref, cs, H - (K - 1) + k, r0, rb))
                sg0 = _sigmoid(g_ref[pl.ds(r0, rb), cs])
                dv_ref[pl.ds(r0, rb), cs] = (dh0 * sg0).astype(BF16)
                dg_ref[pl.ds(r0, rb), cs] = (dh0 * v_ref[pl.ds(r0, rb), cs] * sg0 * (1.0 - sg0)).astype(BF16)
            return 0

        lax.fori_loop(0, M // LANES, strip, 0)
        dbuf_ref[tc:tc + H, :] = dbuf_ref[0:H, :]

    def rowm(c):
        return pl.BlockSpec((tc, M), lambda i: (nc - 1 - i, c))

    def halo(c):
        return pl.BlockSpec((H, M), lambda i: (jnp.maximum((nc - 1 - i) * hb - 1, 0), c))
    vec = _full((1, M))
    return pl.pallas_call(
        body, name="cv_bwd", grid=(nc,),
        in_specs=[rowm(dcol), rowm(vcol), rowm(gcol), halo(vcol), halo(gcol), rowm(0), _full((32, M)), vec, vec, _full((M, M))],
        out_specs=[rowm(0), rowm(0), _full((M, M)), _full((32, 8, M)), _full((5, 8, M))],
        out_shape=[jax.ShapeDtypeStruct((T, M), BF16), jax.ShapeDtypeStruct((T, M), BF16), jax.ShapeDtypeStruct((M, M), F32),
                   jax.ShapeDtypeStruct((32, 8, M), F32), jax.ShapeDtypeStruct((5, 8, M), F32)],
        scratch_shapes=[pltpu.VMEM((tc + H, M), F32), pltpu.VMEM((tc + H, M), F32), pltpu.VMEM((8, tc + H, LANES), F32),
                        pltpu.VMEM((8, tc + H, LANES), F32)], compiler_params=ARB,
    )(dmix, proj, proj, proj, proj, h1, wdw, lg.reshape(1, M), lb.reshape(1, M), wpw)


def _lru_gates(xc, r, ig, sp8):
    la = -r * sp8
    a = jnp.exp(la)
    mult = jnp.sqrt(_neg_expm1(2.0 * la))
    return a, mult, mult * (ig * xc)


def _lru_fwd(proj, xcol, gcol, wcv, bcv, wr, br, wi, bi, sp8):
    T = proj.shape[0]
    M = wr.shape[0]
    tc = _pick(T, 256, 8)
    H, K = SMALL_HALO, 4

    def body(x_ref, g_ref, wcv_ref, bcv_ref, wr_ref, br_ref, wi_ref, bi_ref, sp_ref,
             y_ref, xc_ref, r_ref, i_ref, hp_ref, buf_ref, a_ref, car_ref):
        @pl.when(pl.program_id(0) == 0)
        def _():
            buf_ref[0:H, :] = jnp.zeros((H, M), F32)
            car_ref[...] = jnp.zeros_like(car_ref)

        buf_ref[H:H + tc, :] = x_ref[...]
        xc = jnp.zeros((tc, M), F32) + bcv_ref[...]
        for k in range(K):
            xc = xc + wcv_ref[k:k + 1, :] * buf_ref[pl.ds(H - (K - 1) + k, tc), :]
        buf_ref[0:H, :] = buf_ref[tc:tc + H, :]
        r = _sigmoid(_dot(xc, wr_ref[...]) + br_ref[...])
        ig = _sigmoid(_dot(xc, wi_ref[...]) + bi_ref[...])
        xc_ref[...] = xc
        r_ref[...] = r
        i_ref[...] = ig
        a, _, bt = _lru_gates(xc, r, ig, sp_ref[...])
        a_ref[...] = a
        hp_ref[...] = bt
        row = lax.broadcasted_iota(jnp.int32, (8, M), 0)

        def blk(ib, _):
            r0 = pl.multiple_of(ib * 8, 8)
            av, bv = a_ref[pl.ds(r0, 8), :], hp_ref[pl.ds(r0, 8), :]
            for d in (1, 2, 4):
                m = row >= d
                bv = jnp.where(m, bv + av * pltpu.roll(bv, d, 0), bv)
                av = jnp.where(m, av * pltpu.roll(av, d, 0), av)
            cr = car_ref[7:8, :]
            h = bv + av * cr
            a_ref[pl.ds(r0, 8), :] = h
            hp_ref[pl.ds(r0, 8), :] = jnp.where(row == 0, cr, pltpu.roll(h, 1, 0))
            car_ref[...] = h
            return 0

        lax.fori_loop(0, tc // 8, blk, 0)
        y_ref[...] = (a_ref[...] * _gelu(g_ref[...])).astype(BF16)

    def rowm(c):
        return pl.BlockSpec((tc, M), lambda i: (i, c))
    vec = _full((1, M))
    return pl.pallas_call(
        body, name="lru_fwd", grid=(T // tc,),
        in_specs=[rowm(xcol), rowm(gcol), _full((8, M)), vec, _full((M, M)), vec, _full((M, M)), vec, vec],
        out_specs=[rowm(0)] * 5,
        out_shape=[jax.ShapeDtypeStruct((T, M), BF16)] + [jax.ShapeDtypeStruct((T, M), F32)] * 4,
        scratch_shapes=[pltpu.VMEM((tc + H, M), F32), pltpu.VMEM((tc, M), F32), pltpu.VMEM((8, M), F32)], compiler_params=ARB,
    )(proj, proj, wcv, bcv.reshape(1, M), wr, br.reshape(1, M), wi, bi.reshape(1, M), sp8.reshape(1, M))


def _lru_bwd(dmix, dcol, proj, xcol, gcol, xc, r, ig, hprev, wcv, wr, wi, sp8):
    T = proj.shape[0]
    M = wr.shape[0]
    tc = _pick(T, 256, 8)
    nc = T // tc
    H, K = SMALL_HALO, 4
    hb = tc // H

    def body(do_ref, x_ref, g_ref, xh_ref, xc_ref, r_ref, i_ref, hp_ref, wcv_ref, wr_ref, wi_ref, sp_ref,
             dx_ref, dg_ref, dwr_ref, dwi_ref, dwcv_ref, vec_ref, xbuf_ref, dbuf_ref, a_ref, gs_ref, car_ref):
        i = pl.program_id(0)

        @pl.when(i == 0)
        def _():
            for rf in (dwr_ref, dwi_ref, dwcv_ref, vec_ref, car_ref):
                rf[...] = jnp.zeros_like(rf)
            dbuf_ref[tc:tc + H, :] = jnp.zeros((H, M), F32)

        xcv, rv, iv, hpv, gv, do = xc_ref[...], r_ref[...], i_ref[...], hp_ref[...], g_ref[...], do_ref[...]
        sp = sp_ref[...]
        a, mult, bt = _lru_gates(xcv, rv, iv, sp)
        h = a * hpv + bt
        ge = _gelu(gv)
        dg_ref[...] = (do * h * _gelu_grad(gv)).astype(BF16)
        a_ref[...] = a
        gs_ref[...] = do * ge
        row = lax.broadcasted_iota(jnp.int32, (8, M), 0)

        def blk(n, _):
            r0 = pl.multiple_of((tc // 8 - 1 - n) * 8, 8)
            ablk = a_ref[pl.ds(r0, 8), :]
            av = jnp.where(row == 7, car_ref[0, 0:1, :], pltpu.roll(ablk, 7, 0))
            bv = gs_ref[pl.ds(r0, 8), :]
            for d in (1, 2, 4):
                m = row <= 7 - d
                bv = jnp.where(m, bv + av * pltpu.roll(bv, 8 - d, 0), bv)
                av = jnp.where(m, av * pltpu.roll(av, 8 - d, 0), av)
            gsv = bv + av * car_ref[1, 0:1, :]
            gs_ref[pl.ds(r0, 8), :] = gsv
            car_ref[0] = ablk
            car_ref[1] = gsv
            return 0

        lax.fori_loop(0, tc // 8, blk, 0)
        gs = gs_ref[...]
        dixc = gs * mult
        dla = gs * hpv * a - gs * (iv * xcv) * (a * a) / mult
        vec_ref[2] += _fold8(-dla * rv)
        dpr = (-dla * sp) * rv * (1.0 - rv)
        dpi = (dixc * xcv) * iv * (1.0 - iv)
        vec_ref[0] += _fold8(dpr)
        vec_ref[1] += _fold8(dpi)
        dwr_ref[...] += _dot_tn(xcv, dpr)
        dwi_ref[...] += _dot_tn(xcv, dpi)
        dxc = dixc * iv + _dot_nt(dpr, wr_ref[...]) + _dot_nt(dpi, wi_ref[...])
        vec_ref[3] += _fold8(dxc)
        xbuf_ref[0:H, :] = jnp.where(i == nc - 1, 0.0, xh_ref[...])
        xbuf_ref[H:H + tc, :] = x_ref[...]
        dbuf_ref[0:tc, :] = dxc
        dx = jnp.zeros((tc, M), F32)
        for k in range(K):
            dx = dx + wcv_ref[k:k + 1, :] * dbuf_ref[pl.ds(K - 1 - k, tc), :]
            dwcv_ref[k] += _fold8(dxc * xbuf_ref[pl.ds(H - (K - 1) + k, tc), :])
        dx_ref[...] = dx.astype(BF16)
        dbuf_ref[tc:tc + H, :] = dbuf_ref[0:H, :]

    def rowm(c):
        return pl.BlockSpec((tc, M), lambda i: (nc - 1 - i, c))
    halo = pl.BlockSpec((H, M), lambda i: (jnp.maximum((nc - 1 - i) * hb - 1, 0), xcol))
    vec = _full((1, M))
    return pl.pallas_call(
        body, name="lru_bwd", grid=(nc,),
        in_specs=[rowm(dcol), rowm(xcol), rowm(gcol), halo, rowm(0), rowm(0), rowm(0), rowm(0), _full((8, M)),
                  _full((M, M)), _full((M, M)), vec],
        out_specs=[rowm(0), rowm(0), _full((M, M)), _full((M, M)), _full((8, 8, M)), _full((4, 8, M))],
        out_shape=[jax.ShapeDtypeStruct((T, M), BF16), jax.ShapeDtypeStruct((T, M), BF16), jax.ShapeDtypeStruct((M, M), F32),
                   jax.ShapeDtypeStruct((M, M), F32), jax.ShapeDtypeStruct((8, 8, M), F32), jax.ShapeDtypeStruct((4, 8, M), F32)],
        scratch_shapes=[pltpu.VMEM((tc + H, M), F32), pltpu.VMEM((tc + H, M), F32), pltpu.VMEM((tc, M), F32),
                        pltpu.VMEM((tc, M), F32), pltpu.VMEM((2, 8, M), F32)],
        compiler_params=ARB,
    )(dmix, proj, proj, proj, xc, r, ig, hprev, wcv, wr, wi, sp8.reshape(1, M))


def _pool_diffs(buf_ref, x, t0, tc, gw):
    H = POOL_HALO
    pos = (lax.broadcasted_iota(jnp.int32, (tc, gw), 0) + t0 + 1).astype(F32)
    out, inv = [], []
    for gi, win in enumerate(POOL_WINDOWS):
        sl = slice(gi * gw, (gi + 1) * gw)
        s = x[:, sl]
        for j in range(1, win):
            s = s + buf_ref[pl.ds(H - j, tc), sl]
        ic = 1.0 / jnp.minimum(pos, float(win))
        out.append(s * ic - x[:, sl])
        inv.append(ic)
    return out, inv


def _pool_fwd(proj, col, pw, scale):
    T = proj.shape[0]
    ng, gw, _ = pw.shape
    M = ng * gw
    tc = _pick(T, 256, 8)
    H = POOL_HALO

    def body(x_ref, pw_ref, sc_ref, y_ref, buf_ref):
        @pl.when(pl.program_id(0) == 0)
        def _():
            buf_ref[0:H, :] = jnp.zeros((H, M), F32)

        x = x_ref[...]
        buf_ref[H:H + tc, :] = x
        diffs, _ = _pool_diffs(buf_ref, x, pl.program_id(0) * tc, tc, gw)
        for gi in range(ng):
            sl = slice(gi * gw, (gi + 1) * gw)
            y_ref[:, sl] = (_dot(diffs[gi], pw_ref[gi]) * sc_ref[:, sl]).astype(BF16)
        buf_ref[0:H, :] = buf_ref[tc:tc + H, :]

    return pl.pallas_call(
        body, name="pool_fwd", grid=(T // tc,),
        in_specs=[pl.BlockSpec((tc, M), lambda i: (i, col)), _full(pw.shape), _full((1, M))],
        out_specs=pl.BlockSpec((tc, M), lambda i: (i, 0)), out_shape=jax.ShapeDtypeStruct((T, M), BF16),
        scratch_shapes=[pltpu.VMEM((tc + H, M), F32)], compiler_params=ARB,
    )(proj, pw, scale.reshape(1, M))


def _pool_bwd(dmix, dcol, proj, col, pw, scale):
    T = proj.shape[0]
    ng, gw, _ = pw.shape
    M = ng * gw
    tc = _pick(T, 256, 8)
    nc = T // tc
    H = POOL_HALO
    hb = tc // H

    def body(do_ref, x_ref, xh_ref, pw_ref, sc_ref, dx_ref, dpw_ref, dsc_ref, buf_ref, ebuf_ref):
        i = pl.program_id(0)

        @pl.when(i == 0)
        def _():
            dpw_ref[...] = jnp.zeros_like(dpw_ref)
            dsc_ref[...] = jnp.zeros_like(dsc_ref)
            ebuf_ref[tc:tc + H, :] = jnp.zeros((H, M), F32)

        x, do = x_ref[...], do_ref[...]
        buf_ref[0:H, :] = jnp.where(i == nc - 1, 0.0, xh_ref[...])
        buf_ref[H:H + tc, :] = x
        diffs, inv = _pool_diffs(buf_ref, x, (nc - 1 - i) * tc, tc, gw)
        ddg = []
        for gi in range(ng):
            sl = slice(gi * gw, (gi + 1) * gw)
            dyy = do[:, sl] * sc_ref[:, sl]
            dsc_ref[:, sl] += _fold8(do[:, sl] * _dot(diffs[gi], pw_ref[gi]))
            dpw_ref[gi] += _dot_tn(diffs[gi], dyy)
            d = _dot_nt(dyy, pw_ref[gi])
            ddg.append(d)
            ebuf_ref[0:tc, sl] = d * inv[gi]
        for gi, win in enumerate(POOL_WINDOWS):
            sl = slice(gi * gw, (gi + 1) * gw)
            s = -ddg[gi]
            for j in range(win):
                s = s + ebuf_ref[pl.ds(j, tc), sl]
            dx_ref[:, sl] = s.astype(BF16)
        ebuf_ref[tc:tc + H, :] = ebuf_ref[0:H, :]

    return pl.pallas_call(
        body, name="pool_bwd", grid=(nc,),
        in_specs=[pl.BlockSpec((tc, M), lambda i: (nc - 1 - i, dcol)), pl.BlockSpec((tc, M), lambda i: (nc - 1 - i, col)),
                  pl.BlockSpec((H, M), lambda i: (jnp.maximum((nc - 1 - i) * hb - 1, 0), col)), _full(pw.shape), _full((1, M))],
        out_specs=[pl.BlockSpec((tc, M), lambda i: (nc - 1 - i, 0)), _full(pw.shape), _full((8, M))],
        out_shape=[jax.ShapeDtypeStruct((T, M), BF16), jax.ShapeDtypeStruct(pw.shape, F32), jax.ShapeDtypeStruct((8, M), F32)],
        scratch_shapes=[pltpu.VMEM((tc + H, M), F32), pltpu.VMEM((tc + H, M), F32)], compiler_params=ARB,
    )(dmix, proj, proj, pw, scale.reshape(1, M))


def _ffn_mid_fwd(up, wdw, bdw):
    T, F2 = up.shape
    F = F2 // 2
    tc = _pick(T, 512, 8)
    rb = _pick(tc, 64, 16)
    H, K = FFN_HALO, 3

    def body(gt_ref, val_ref, w_ref, b_ref, act_ref, buf_ref):
        @pl.when(pl.program_id(0) == 0)
        def _():
            buf_ref[0:H, :] = jnp.zeros((H, F), F32)

        buf_ref[H:H + tc, :] = gt_ref[...].astype(F32)

        def strip(s, _):
            cs = pl.ds(pl.multiple_of(s * LANES, LANES), LANES)
            taps = [w_ref[k:k + 1, cs] for k in range(K)]
            for r0 in range(0, tc, rb):
                gc = b_ref[:, cs] + taps[0] * buf_ref[pl.ds(H - (K - 1) + r0, rb), cs]
                for k in range(1, K):
                    gc = gc + taps[k] * buf_ref[pl.ds(H - (K - 1) + k + r0, rb), cs]
                act_ref[pl.ds(r0, rb), cs] = (_gelu(gc) * val_ref[pl.ds(r0, rb), cs].astype(F32)).astype(BF16)
            return 0

        lax.fori_loop(0, F // LANES, strip, 0)
        buf_ref[0:H, :] = buf_ref[tc:tc + H, :]

    return pl.pallas_call(
        body, name="ffn_mid_fwd", grid=(T // tc,),
        in_specs=[pl.BlockSpec((tc, F), lambda i: (i, 0)), pl.BlockSpec((tc, F), lambda i: (i, 1)), _full((8, F)), _full((1, F))],
        out_specs=pl.BlockSpec((tc, F), lambda i: (i, 0)), out_shape=jax.ShapeDtypeStruct((T, F), BF16),
        scratch_shapes=[pltpu.VMEM((tc + H, F), F32)], compiler_params=ARB,
    )(up, up, wdw, bdw.reshape(1, F))


def _ffn_mid_bwd(dact, up, wdw, bdw):
    T, F2 = up.shape
    F = F2 // 2
    tc = _pick(T, 256, 8)
    rb = _pick(tc, 64, 16)
    nc = T // tc
    H, K = FFN_HALO, 3
    hb = tc // H

    def body(da_ref, gt_ref, val_ref, gh_ref, w_ref, b_ref, dup_ref, dw_ref, db_ref, gbuf_ref, dbuf_ref):
        i = pl.program_id(0)

        @pl.when(i == 0)
        def _():
            dw_ref[...] = jnp.zeros_like(dw_ref)
            db_ref[...] = jnp.zeros_like(db_ref)
            dbuf_ref[tc:tc + H, :] = jnp.zeros((H, F), F32)

        gbuf_ref[0:H, :] = jnp.where(i == nc - 1, 0.0, gh_ref[...].astype(F32))
        gbuf_ref[H:H + tc, :] = gt_ref[...].astype(F32)

        def strip(s, _):
            cs = pl.ds(pl.multiple_of(s * LANES, LANES), LANES)
            cs_val = pl.ds(pl.multiple_of(F + s * LANES, LANES), LANES)
            taps = [w_ref[k:k + 1, cs] for k in range(K)]
            for r0 in range(0, tc, rb):
                shifted = [gbuf_ref[pl.ds(H - (K - 1) + k + r0, rb), cs] for k in range(K)]
                gc = b_ref[:, cs] + taps[0] * shifted[0]
                for k in range(1, K):
                    gc = gc + taps[k] * shifted[k]
                ge, gg = _gelu_and_grad(gc)
                da = da_ref[pl.ds(r0, rb), cs].astype(F32)
                dup_ref[pl.ds(r0, rb), cs_val] = (da * ge).astype(BF16)
                dgc = da * val_ref[pl.ds(r0, rb), cs].astype(F32) * gg
                dbuf_ref[pl.ds(r0, rb), cs] = dgc
                db_ref[:, cs] += _fold8(dgc)
                for k in range(K):
                    dw_ref[k, :, cs] += _fold8(dgc * shifted[k])
            for r0 in range(0, tc, rb):
                dgt = taps[0] * dbuf_ref[pl.ds(K - 1 + r0, rb), cs]
                for k in range(1, K):
                    dgt = dgt + taps[k] * dbuf_ref[pl.ds(K - 1 - k + r0, rb), cs]
                dup_ref[pl.ds(r0, rb), cs] = dgt.astype(BF16)
            return 0

        lax.fori_loop(0, F // LANES, strip, 0)
        dbuf_ref[tc:tc + H, :] = dbuf_ref[0:H, :]

    return pl.pallas_call(
        body, name="ffn_mid_bwd", grid=(nc,),
        in_specs=[pl.BlockSpec((tc, F), lambda i: (nc - 1 - i, 0)), pl.BlockSpec((tc, F), lambda i: (nc - 1 - i, 0)),
                  pl.BlockSpec((tc, F), lambda i: (nc - 1 - i, 1)),
                  pl.BlockSpec((H, F), lambda i: (jnp.maximum((nc - 1 - i) * hb - 1, 0), 0)), _full((8, F)), _full((1, F))],
        out_specs=[pl.BlockSpec((tc, F2), lambda i: (nc - 1 - i, 0)), _full((8, 8, F)), _full((8, F))],
        out_shape=[jax.ShapeDtypeStruct((T, F2), BF16), jax.ShapeDtypeStruct((8, 8, F), F32), jax.ShapeDtypeStruct((8, F), F32)],
        scratch_shapes=[pltpu.VMEM((tc + H, F), F32), pltpu.VMEM((tc + H, F), F32)], compiler_params=ARB,
    )(dact, up, up, up, wdw, bdw.reshape(1, F))


def _my_pos():
    return lax.axis_index("x"), lax.axis_index("y"), lax.axis_index("c")


def _flip(pos, k):
    x, y, c = pos
    return ((1 - x) if k & 4 else x, (1 - y) if k & 2 else y, (1 - c) if k & 1 else c)


def _rank(pos):
    return 4 * pos[0] + 2 * pos[1] + pos[2]


_HBM = pl.BlockSpec(memory_space=pltpu.HBM)
_SEM = pl.BlockSpec(memory_space=pltpu.SEMAPHORE)
_EFFECT = pltpu.SideEffectType.DATAFLOW_SIDE_EFFECTING


def _plan_direct(gather):
    def plan(bufs, ssem, rsem, me, outgoing):
        n = len(bufs) // 2
        out = []
        for a in range(n):
            for k in range(N_DEV):
                peer = _flip(me, k)
                out.append(pltpu.make_async_remote_copy(
                    src_ref=bufs[a] if gather else bufs[a].at[_rank(peer)],
                    dst_ref=bufs[n + a].at[_rank(me) if outgoing else _rank(peer)],
                    send_sem=ssem.at[a * N_DEV + k], recv_sem=rsem.at[a * N_DEV + k], device_id=peer,
                    device_id_type=pl.DeviceIdType.MESH))
        return out
    return plan, N_DEV


_SAME_CORE = (2, 4, 6)


def _plan_own_block(bufs, ssem, rsem, me, outgoing):
    n = len(bufs) // 2
    out = []
    for a in range(n):
        for j, k in enumerate((0, 1) + _SAME_CORE):
            peer = _flip(me, k)
            out.append(pltpu.make_async_remote_copy(
                src_ref=bufs[a], dst_ref=bufs[n + a].at[_rank(me) if outgoing else _rank(peer)],
                send_sem=ssem.at[a * 5 + j], recv_sem=rsem.at[a * 5 + j], device_id=peer, device_id_type=pl.DeviceIdType.MESH))
    return out


def _plan_pass_on(bufs, ssem, rsem, me, outgoing):
    sibling = _flip(me, 1)
    out = []
    for a in range(len(bufs)):
        for j, k in enumerate(_SAME_CORE):
            slot = _rank(_flip(me, k) if outgoing else _flip(sibling, k))
            out.append(pltpu.make_async_remote_copy(
                src_ref=bufs[a].at[slot], dst_ref=bufs[a].at[slot], send_sem=ssem.at[a * 3 + j], recv_sem=rsem.at[a * 3 + j],
                device_id=sibling, device_id_type=pl.DeviceIdType.MESH))
    return out


def _split_start(bufs, plan, n_sems, name, after=None):
    nb = len(bufs)
    extra = [] if after is None else [after]

    def body(*refs):
        ssem, rsem = refs[nb + len(extra)], refs[nb + len(extra) + 1]
        for cp in plan(refs[:nb], ssem, rsem, _my_pos(), True):
            cp.start()
        token = refs[2 * nb + len(extra) + 2]
        token[...] = jnp.zeros_like(token)

    res = pl.pallas_call(
        body, name=name,
        out_shape=(pltpu.SemaphoreType.DMA((n_sems,)), pltpu.SemaphoreType.DMA((n_sems,)),
                   *[pltpu.HBM(t.shape, t.dtype) for t in bufs], jax.ShapeDtypeStruct((8, 128), F32)),
        in_specs=[_HBM] * nb + [pl.BlockSpec(memory_space=pl.ANY)] * len(extra),
        out_specs=(_SEM, _SEM, *([_HBM] * nb), pl.BlockSpec(memory_space=pltpu.VMEM)),
        input_output_aliases={i: 2 + i for i in range(nb)},
        compiler_params=pltpu.CompilerParams(has_side_effects=_EFFECT),
    )(*[pltpu.with_memory_space_constraint(t, pltpu.HBM) for t in bufs], *extra)
    return (res[0], res[1], list(res[2:2 + nb])), res[2 + nb]


def _split_wait(handle, plan, after, name):
    ssem, rsem, bufs = handle
    nb = len(bufs)

    def body(*refs):
        for cp in plan(refs[:nb], refs[nb], refs[nb + 1], _my_pos(), False):
            cp.wait_send()
            cp.wait_recv()

    res = pl.pallas_call(
        body, name=name, out_shape=tuple(pltpu.HBM(t.shape, t.dtype) for t in bufs),
        in_specs=[_HBM] * nb + [_SEM, _SEM, pl.BlockSpec(memory_space=pl.ANY)], out_specs=tuple([_HBM] * nb),
        input_output_aliases={i: i for i in range(nb)},
        compiler_params=pltpu.CompilerParams(has_side_effects=_EFFECT),
    )(*bufs, ssem, rsem, after)
    return list(res)


def _landing_zones(xs, gather):
    return [lax.empty(((N_DEV,) + tuple(x.shape)) if gather else tuple(x.shape), x.dtype) for x in xs]


def _gather_start(xs, name, after=None, two_level=True):
    bufs = list(xs) + _landing_zones(xs, True)
    if two_level:
        handle, token = _split_start(bufs, _plan_own_block, 5 * len(xs), name + "_start", after)
    else:
        plan, per = _plan_direct(True)
        handle, token = _split_start(bufs, plan, per * len(xs), name + "_start", after)
    return (handle, two_level, name), token


def _gather_wait(state, after):
    handle, two_level, name = state
    n = len(handle[2]) // 2
    if not two_level:
        return _split_wait(handle, _plan_direct(True)[0], after, name + "_wait")[n:]
    lands = _split_wait(handle, _plan_own_block, after, name + "_wait")[n:]
    handle, _ = _split_start(lands, _plan_pass_on, 3 * n, name + "_pass")
    return _split_wait(handle, _plan_pass_on, after, name + "_passed")


def _scatter_start(xs, name):
    plan, per = _plan_direct(False)
    handle, token = _split_start(list(xs) + _landing_zones(xs, False), plan, per * len(xs), name + "_start")
    return (handle, name), token


def _scatter_wait(state, after):
    handle, name = state
    return _split_wait(handle, _plan_direct(False)[0], after, name + "_wait")[len(handle[2]) // 2:]


def _adamw_math(w, g, m, v):
    m = ADAM_B1 * m + (1.0 - ADAM_B1) * g
    v = ADAM_B2 * v + (1.0 - ADAM_B2) * (g * g)
    m_hat = m / (1.0 - ADAM_B1 ** ADAM_STEP)
    v_hat = v / (1.0 - ADAM_B2 ** ADAM_STEP)
    return -ADAM_LR * (m_hat / (jnp.sqrt(v_hat) + ADAM_EPS) + ADAM_WD * w), m, v


def _sum8(parts, *, name):
    _, R, C = parts.shape
    tr = _pick(R, 256, 16)

    def body(p_ref, o_ref):
        acc = p_ref[0].astype(F32)
        for r in range(1, N_DEV):
            acc = acc + p_ref[r].astype(F32)
        o_ref[...] = acc

    return pl.pallas_call(
        body, name=name, grid=(R // tr,), in_specs=[pl.BlockSpec((N_DEV, tr, C), lambda i: (0, i, 0))],
        out_specs=pl.BlockSpec((tr, C), lambda i: (i, 0)), out_shape=jax.ShapeDtypeStruct((R, C), F32),
        compiler_params=pltpu.CompilerParams(dimension_semantics=("parallel",)),
    )(parts)


def _adamw_layer(g, w, m, v, layer, prev, *, name):
    depth, R, C = w.shape
    tr = _pick(R, 128, 16)
    summed = g.ndim == 3
    if prev is None:
        prev = [lax.empty((depth, R, C), F32) for _ in range(4)]

    def body(g_ref, w_ref, m_ref, v_ref, *rest):
        outs = rest[4:]
        if summed:
            gv = g_ref[0].astype(F32)
            for r in range(1, N_DEV):
                gv = gv + g_ref[r].astype(F32)
        else:
            gv = g_ref[...]
        d, nm, nv = _adamw_math(w_ref[...], gv, m_ref[...], v_ref[...])
        for o, val in zip(outs, (gv, d, nm, nv)):
            o[...] = val

    g_spec = pl.BlockSpec((N_DEV, tr, C), lambda i: (0, i, 0)) if summed else pl.BlockSpec((tr, C), lambda i: (i, 0))
    slab = pl.BlockSpec((None, tr, C), lambda i: (layer, i, 0))
    return pl.pallas_call(
        body, name=name, grid=(R // tr,), in_specs=[g_spec] + [slab] * 3 + [pl.BlockSpec(memory_space=pl.ANY)] * 4,
        out_specs=[slab] * 4, out_shape=[jax.ShapeDtypeStruct((depth, R, C), F32)] * 4,
        input_output_aliases={4 + q: q for q in range(4)},
        compiler_params=pltpu.CompilerParams(dimension_semantics=("parallel",)),
    )(g, w, m, v, *prev)


def _lane_view(shape):
    if len(shape) == 1:
        return (1, shape[0])
    size = math.prod(shape)
    if shape[-1] < 64 and size % LANES == 0:
        return (size // LANES, LANES)
    if len(shape) > 2 and shape[-2] % 8 == 0:
        return (size // shape[-1], shape[-1])
    return tuple(shape)


def _adamw_many(ws, gs, ms, vs, *, name):
    n = len(ws)
    views = [_lane_view(w.shape) for w in ws]

    def body(*refs):
        for i in range(n):
            d, nm, nv = _adamw_math(refs[i][...], refs[n + i][...], refs[2 * n + i][...], refs[3 * n + i][...])
            refs[4 * n + i][...] = d
            refs[5 * n + i][...] = nm
            refs[6 * n + i][...] = nv

    outs = pl.pallas_call(
        body, name=name, out_shape=[jax.ShapeDtypeStruct(s, F32) for s in views] * 3,
    )(*[a.reshape(s) for arrs in (ws, gs, ms, vs) for a, s in zip(arrs, views)])
    return [[o.reshape(w.shape) for o, w in zip(outs[q * n:(q + 1) * n], ws)] for q in range(3)]


def _pack(arrs, dtype=F32):
    flat = jnp.concatenate([a.reshape(-1).astype(dtype) for a in arrs])
    pad = (-flat.shape[0]) % 2048
    return jnp.pad(flat, (0, pad)).reshape(-1, 128)


def _unpack(buf, shapes):
    flat, out, off = buf.reshape(-1), [], 0
    for s in shapes:
        n = math.prod(s)
        out.append(flat[off:off + n].reshape(s))
        off += n
    return out


def _s5_params(lam_re, lam_im, log_step, b_re, b_im):
    G, P, H = b_re.shape
    step = jnp.exp(log_step)[:, None]
    mag = jnp.exp(lam_re * step)
    ar, ai = mag * jnp.cos(lam_im * step), mag * jnp.sin(lam_im * step)
    den = lam_re * lam_re + lam_im * lam_im
    qr = ((ar - 1.0) * lam_re + ai * lam_im) / den
    qi = (ai * lam_re - (ar - 1.0) * lam_im) / den
    bbr = qr[..., None] * b_re - qi[..., None] * b_im
    bbi = qr[..., None] * b_im + qi[..., None] * b_re
    return ar.reshape(-1), ai.reshape(-1), _s5_blocks_in(bbr), _s5_blocks_in(bbi)


def _s5_blocks_in(b):
    G, P, H = b.shape
    eye = jnp.eye(8, dtype=b.dtype)
    return jnp.einsum("kgph,gj->kghjp", b.reshape(G // 8, 8, P, H), eye).reshape(G // 8, 8 * H, 8 * P)


def _s5_blocks_out(c):
    G, H, P = c.shape
    eye = jnp.eye(8, dtype=c.dtype)
    return jnp.einsum("kghp,gj->kgpjh", c.reshape(G // 8, 8, H, P), eye).reshape(G // 8, 8 * P, 8 * H)


def _s5_blocks_out_diag(m, G, H, P):
    return jnp.einsum("kgpgh->kghp", m.reshape(G // 8, 8, P, 8, H)).reshape(G, H, P)


def _head_blocks(w):
    nh, d, e = w.shape
    return jnp.einsum("hde,hk->hdke", w, jnp.eye(nh, dtype=w.dtype)).reshape(nh * d, nh * e)


def _head_blocks_diag(m, nh, d, e):
    return jnp.einsum("hdhe->hde", m.reshape(nh, d, nh, e))


def _pad_rows(w, rows):
    return jnp.pad(w, ((0, rows - w.shape[0]), (0, 0)))


def _unshard_cols(g):
    return jnp.transpose(g, (1, 0, 2)).reshape(g.shape[1], -1)


_NAMES = ['norm_mix_g', 'w_in', 's5_lam_re', 's5_lam_im', 's5_log_step', 's5_b_re', 's5_b_im', 's5_c_re', 's5_c_im', 's5_d',
          's5_w_glu', 's5_b_glu', 'cv_w_dw', 'cv_b_dw', 'cv_ln_g', 'cv_ln_b', 'cv_w_pw', 'cv_b_pw', 'lru_w_conv', 'lru_b_conv',
          'lru_w_r', 'lru_b_r', 'lru_w_i', 'lru_b_i', 'lru_lam', 'pool_w', 'pool_scale', 'w_out', 'norm_ffn_g', 'ffn_w_up',
          'ffn_w_dw', 'ffn_b_dw', 'ffn_w_down', 'norm_final_g']
_MATRIX = ('w_in', 'w_out', 'ffn_w_up', 'ffn_w_down', 's5_w_glu', 'cv_w_pw')
_TRANSPOSED = ('w_in', 'ffn_w_up')
_COLSHARD = ('cv_w_dw', 'lru_w_conv', 'ffn_w_dw')
_GROUPS = (('w_in', 's5_w_glu', 'cv_w_pw', 'w_out'), ('ffn_w_up', 'ffn_w_down'))
_SMALL_BF16 = ('s5_b_re', 's5_b_im', 's5_c_re', 's5_c_im', 'lru_w_r', 'lru_w_i', 'pool_w')
_SCATTER_GROUPS = (('w_out', 's5_w_glu', 'cv_w_pw'), ('w_in',), ('ffn_w_up', 'ffn_w_down'))


def kernel(x, norm_mix_g, w_in, s5_lam_re, s5_lam_im, s5_log_step, s5_b_re, s5_b_im, s5_c_re, s5_c_im, s5_d, s5_w_glu, s5_b_glu, cv_w_dw, cv_b_dw, cv_ln_g, cv_ln_b, cv_w_pw, cv_b_pw, lru_w_conv, lru_b_conv, lru_w_r, lru_b_r, lru_w_i, lru_b_i, lru_lam, pool_w, pool_scale, w_out, norm_ffn_g, ffn_w_up, ffn_w_dw, ffn_b_dw, ffn_w_down, norm_final_g, loss_target, m_norm_mix_g, m_w_in, m_s5_lam_re, m_s5_lam_im, m_s5_log_step, m_s5_b_re, m_s5_b_im, m_s5_c_re, m_s5_c_im, m_s5_d, m_s5_w_glu, m_s5_b_glu, m_cv_w_dw, m_cv_b_dw, m_cv_ln_g, m_cv_ln_b, m_cv_w_pw, m_cv_b_pw, m_lru_w_conv, m_lru_b_conv, m_lru_w_r, m_lru_b_r, m_lru_w_i, m_lru_b_i, m_lru_lam, m_pool_w, m_pool_scale, m_w_out, m_norm_ffn_g, m_ffn_w_up, m_ffn_w_dw, m_ffn_b_dw, m_ffn_w_down, m_norm_final_g, v_norm_mix_g, v_w_in, v_s5_lam_re, v_s5_lam_im, v_s5_log_step, v_s5_b_re, v_s5_b_im, v_s5_c_re, v_s5_c_im, v_s5_d, v_s5_w_glu, v_s5_b_glu, v_cv_w_dw, v_cv_b_dw, v_cv_ln_g, v_cv_ln_b, v_cv_w_pw, v_cv_b_pw, v_lru_w_conv, v_lru_b_conv, v_lru_w_r, v_lru_b_r, v_lru_w_i, v_lru_b_i, v_lru_lam, v_pool_w, v_pool_scale, v_w_out, v_norm_ffn_g, v_ffn_w_up, v_ffn_w_dw, v_ffn_b_dw, v_ffn_w_down, v_norm_final_g):
    args = (x, norm_mix_g, w_in, s5_lam_re, s5_lam_im, s5_log_step, s5_b_re, s5_b_im, s5_c_re, s5_c_im, s5_d, s5_w_glu, s5_b_glu, cv_w_dw, cv_b_dw, cv_ln_g, cv_ln_b, cv_w_pw, cv_b_pw, lru_w_conv, lru_b_conv, lru_w_r, lru_b_r, lru_w_i, lru_b_i, lru_lam, pool_w, pool_scale, w_out, norm_ffn_g, ffn_w_up, ffn_w_dw, ffn_b_dw, ffn_w_down, norm_final_g, loss_target, m_norm_mix_g, m_w_in, m_s5_lam_re, m_s5_lam_im, m_s5_log_step, m_s5_b_re, m_s5_b_im, m_s5_c_re, m_s5_c_im, m_s5_d, m_s5_w_glu, m_s5_b_glu, m_cv_w_dw, m_cv_b_dw, m_cv_ln_g, m_cv_ln_b, m_cv_w_pw, m_cv_b_pw, m_lru_w_conv, m_lru_b_conv, m_lru_w_r, m_lru_b_r, m_lru_w_i, m_lru_b_i, m_lru_lam, m_pool_w, m_pool_scale, m_w_out, m_norm_ffn_g, m_ffn_w_up, m_ffn_w_dw, m_ffn_b_dw, m_ffn_w_down, m_norm_final_g, v_norm_mix_g, v_w_in, v_s5_lam_re, v_s5_lam_im, v_s5_log_step, v_s5_b_re, v_s5_b_im, v_s5_c_re, v_s5_c_im, v_s5_d, v_s5_w_glu, v_s5_b_glu, v_cv_w_dw, v_cv_b_dw, v_cv_ln_g, v_cv_ln_b, v_cv_w_pw, v_cv_b_pw, v_lru_w_conv, v_lru_b_conv, v_lru_w_r, v_lru_b_r, v_lru_w_i, v_lru_b_i, v_lru_lam, v_pool_w, v_pool_scale, v_w_out, v_norm_ffn_g, v_ffn_w_up, v_ffn_w_dw, v_ffn_b_dw, v_ffn_w_down, v_norm_final_g)
    x, target = args[0], args[35]
    W = dict(zip(_NAMES, args[1:35]))
    Mo = dict(zip(_NAMES, args[36:70]))
    Vo = dict(zip(_NAMES, args[70:104]))
    x = x[0]
    target = target[0]
    T, D = x.shape
    depth = W['w_in'].shape[0]
    F = W['ffn_w_down'].shape[1] * N_DEV
    G, P, H = W['s5_b_re'].shape[1:]
    nh, hd, _ = W['lru_w_r'].shape[1:]
    me = _rank(_my_pos())

    gather, full = {}, {}

    def start_group(l, gi, after):
        shards = [(W[n][l].T if n in _TRANSPOSED else W[n][l]).astype(BF16) for n in _GROUPS[gi]]
        if (l, gi) == (0, 0):
            shards.append(_pack([W[n][k] for k in range(depth) for n in _COLSHARD]))
        gather[l, gi], tok = _gather_start(shards, "gather_%d%d" % (l, gi), after)
        return tok

    def land_weights(l, gi, after):
        got = _gather_wait(gather[l, gi], after)
        for n, g8 in zip(_GROUPS[gi], got):
            full[n, l] = g8.reshape(-1, g8.shape[-1])
        return got

    started = start_group(0, 0, None)[0, 0]
    consts = []
    for l in range(depth):
        s5p, s5p_vjp = jax.vjp(_s5_params, W['s5_lam_re'][l], W['s5_lam_im'][l], W['s5_log_step'][l], W['s5_b_re'][l], W['s5_b_im'][l])
        ar, ai, bre, bim = s5p
        cre, cim = _s5_blocks_out(W['s5_c_re'][l]), _s5_blocks_out(W['s5_c_im'][l])
        sp8, sp8_vjp = jax.vjp(lambda lam: LRU_C * jax.nn.softplus(-lam), W['lru_lam'][l])
        wr, wi = _head_blocks(W['lru_w_r'][l]), _head_blocks(W['lru_w_i'][l])
        consts.append(dict(bre=bre.astype(BF16), bim=bim.astype(BF16), cre=cre.astype(BF16), cim=cim.astype(BF16),
                           cst_f=_s5_scan_consts(ar, ai, False), cst_b=_s5_scan_consts(ar, ai, True), sp8=sp8,
                           wr=wr.astype(BF16), wi=wi.astype(BF16), pw=W['pool_w'][l].astype(BF16)))
        consts[-1].update(s5p_vjp=s5p_vjp, sp8_vjp=sp8_vjp)
    ready = sum(c[k].reshape(-1)[0].astype(F32) for c in consts for k in c if not k.endswith('_vjp'))
    got = land_weights(0, 0, (ready + started).reshape(1, 1))
    started = start_group(0, 1, got[0])[0, 0]
    taps8 = got[-1]
    tap_shapes = [W[n][l].shape for l in range(depth) for n in _COLSHARD]
    taps = [_unshard_cols(jnp.stack(t)) for t in zip(*[_unpack(taps8[r], tap_shapes) for r in range(N_DEV)])]
    for l in range(depth):
        for j, n in enumerate(_COLSHARD):
            full[n, l] = taps[l * len(_COLSHARD) + j]

    saved = []
    for l in range(depth):
        g_mix = W['norm_mix_g'][l] + (0.0 if l else started)
        if l:
            land_weights(l, 0, x)
        c = dict(consts[l], cvw=_pad_rows(full['cv_w_dw', l], 32), lruw=_pad_rows(full['lru_w_conv', l], 8),
                 ffw=_pad_rows(full['ffn_w_dw', l], 8), x0=x)
        sp8 = c['sp8']
        h = _rmsnorm(x, g_mix, name="mix_norm")
        proj = _matmul(h, full['w_in', l], mode="nt", tm=1024, tn=1024, tk=D, out_dtype=F32, name="in_proj")
        y0, sre, sim, yp, z = _s5_fwd(proj, 0, c['bre'], c['bim'], c['cre'], c['cim'], c['cst_f'], W['s5_d'][l],
                                     full['s5_w_glu', l], W['s5_b_glu'][l])
        y1, h1 = _cv_fwd(proj, 1, 2, c['cvw'], W['cv_b_dw'][l], W['cv_ln_g'][l], W['cv_ln_b'][l], full['cv_w_pw', l], W['cv_b_pw'][l])
        y2, xc, rr, ig, hprev = _lru_fwd(proj, 3, 4, c['lruw'], W['lru_b_conv'][l], c['wr'], W['lru_b_r'][l], c['wi'],
                                         W['lru_b_i'][l], sp8)
        y3 = _pool_fwd(proj, 5, c['pw'], W['pool_scale'][l])
        mixed = jnp.concatenate([y0, y1, y2, y3], axis=1)
        x1 = _matmul(mixed, full['w_out', l], mode="nn", tm=1024, tn=512, tk=2048, out_dtype=F32, name="out_proj", add=x)
        got = land_weights(l, 1, x1)
        g_ffn = W['norm_ffn_g'][l]
        if l + 1 < depth:
            g_ffn = g_ffn + start_group(l + 1, 0, got[0])[0, 0] + start_group(l + 1, 1, got[0])[0, 0]
        h2 = _rmsnorm(x1, g_ffn, name="ffn_norm")
        up = _matmul(h2, full['ffn_w_up', l], mode="nt", tm=2048, tn=256, tk=D, out_dtype=BF16, name="ffn_up")
        act = _ffn_mid_fwd(up, c['ffw'], W['ffn_b_dw'][l])
        x = _matmul(act, full['ffn_w_down', l], mode="nn", tm=1024, tn=512, tk=F, out_dtype=F32, name="ffn_down", add=x1)
        c.update(proj=proj, h=h, sre=sre, sim=sim, yp=yp, z=z, h1=h1, xc=xc, rr=rr, ig=ig, hprev=hprev, mixed=mixed, x1=x1,
                 up=up, h2=h2, act=act)
        saved.append(c)

    dx, dxb, dgf, se = _loss_head(x, W['norm_final_g'], target)
    loss = lax.psum(0.5 / D * jnp.sum(se), AXES)
    gsmall = {('norm_final_g', None): dgf.sum(0)}
    gmat, scatter = {}, {}

    def parts_of(m):
        return m.reshape(N_DEV, m.shape[0] // N_DEV, m.shape[1])

    for l in reversed(range(depth)):
        c = saved[l]
        dact = _matmul(dxb, full['ffn_w_down', l], mode="nt", tm=512, tn=F, tk=512, out_dtype=BF16, name="d_act")
        gmat['ffn_w_down', l] = _matmul(c['act'], dxb, mode="tn", tm=F, tn=512, tk=512, out_dtype=BF16, name="dw_down")
        dup, dffw, dffb = _ffn_mid_bwd(dact, c['up'], c['ffw'], W['ffn_b_dw'][l])
        gsmall['ffn_w_dw', l] = dffw.sum(1)[:3]
        gsmall['ffn_b_dw', l] = dffb.sum(0)
        dh2 = _matmul(dup, full['ffn_w_up', l], mode="nn", tm=1024, tn=512, tk=F, out_dtype=F32, name="d_h2")
        gmat['ffn_w_up', l] = _matmul(dup, c['h2'], mode="tn", tm=256, tn=D, tk=T, out_dtype=BF16, name="dw_up")
        scatter[l, 2], tok = _scatter_start([parts_of(gmat[n, l]) for n in _SCATTER_GROUPS[2]], "scatter_%d2" % l)
        dx1, dx1b, dg2 = _rms_bwd(dx, dh2, c['x1'], W['norm_ffn_g'][l] + tok[0, 0], name="ffn_norm_bwd")
        gsmall['norm_ffn_g', l] = dg2.sum(0)
        dmix = _matmul(dx1b, full['w_out', l], mode="nt", tm=1024, tn=512, tk=2048, out_dtype=F32, name="d_mixed")
        gmat['w_out', l] = _matmul(c['mixed'], dx1b, mode="tn", tm=512, tn=D, tk=T, out_dtype=BF16, name="dw_out")
        du, dbre, dbim, dcre, dcim, da, dwglu, v3 = _s5_bwd(dmix, 0, c['proj'], 0, c['sre'], c['sim'], c['yp'], c['z'], c['bre'],
                                                            c['bim'], c['cre'], c['cim'], c['cst_b'], W['s5_d'][l], full['s5_w_glu', l])
        da = da.sum(1)
        glr, gli, gls, gbr, gbi = c['s5p_vjp']((da[0], da[1], dbre, dbim))
        gsmall['s5_lam_re', l], gsmall['s5_lam_im', l], gsmall['s5_log_step', l] = glr, gli, gls
        gsmall['s5_b_re', l], gsmall['s5_b_im', l] = gbr, gbi
        gsmall['s5_c_re', l] = _s5_blocks_out_diag(dcre, G, H, P)
        gsmall['s5_c_im', l] = _s5_blocks_out_diag(dcim, G, H, P)
        gsmall['s5_b_glu', l], gsmall['s5_d', l] = v3[0].sum(0), v3[1].sum(0)
        gmat['s5_w_glu', l] = dwglu.astype(BF16)
        dv, dg, dwpw, dcvw, v5 = _cv_bwd(dmix, 1, c['proj'], 1, 2, c['h1'], c['cvw'], W['cv_ln_g'][l], W['cv_ln_b'][l], full['cv_w_pw', l])
        gmat['cv_w_pw', l] = dwpw.astype(BF16)
        gsmall['cv_w_dw', l] = dcvw.sum(1)[:31]
        gsmall['cv_b_pw', l], gsmall['cv_ln_g', l], gsmall['cv_ln_b', l], gsmall['cv_b_dw', l] = [v5[j].sum(0) for j in range(4)]
        scatter[l, 0], tok = _scatter_start([parts_of(gmat[n, l]) for n in _SCATTER_GROUPS[0]], "scatter_%d0" % l)
        tok = tok[0, 0]
        dlx, dlg, dwr, dwi, dlw, v4 = _lru_bwd(dmix, 2, c['proj'], 3, 4, c['xc'], c['rr'], c['ig'], c['hprev'], c['lruw'], c['wr'],
                                               c['wi'], c['sp8'] + tok)
        gsmall['lru_w_r', l], gsmall['lru_w_i', l] = _head_blocks_diag(dwr, nh, hd, hd), _head_blocks_diag(dwi, nh, hd, hd)
        gsmall['lru_w_conv', l] = dlw.sum(1)[:4]
        gsmall['lru_b_r', l], gsmall['lru_b_i', l] = v4[0].sum(0), v4[1].sum(0)
        gsmall['lru_lam', l] = c['sp8_vjp'](v4[2].sum(0))[0]
        gsmall['lru_b_conv', l] = v4[3].sum(0)
        dpx, dpw, dps = _pool_bwd(dmix, 3, c['proj'], 5, c['pw'], W['pool_scale'][l] + tok)
        gsmall['pool_w', l], gsmall['pool_scale', l] = dpw, dps.sum(0)
        dproj = jnp.concatenate([du, dv, dg, dlx, dlg, dpx], axis=1)
        dh = _matmul(dproj, full['w_in', l], mode="nn", tm=1024, tn=512, tk=dproj.shape[1], out_dtype=F32, name="d_h")
        gmat['w_in', l] = _matmul(dproj, c['h'], mode="tn", tm=512, tn=D, tk=T, out_dtype=BF16, name="dw_in")
        scatter[l, 1], tok = _scatter_start([parts_of(gmat[n, l]) for n in _SCATTER_GROUPS[1]], "scatter_%d1" % l)
        dx, dxb, dg1 = _rms_bwd(dx1, dh, c['x0'], W['norm_mix_g'][l] + tok[0, 0], name="mix_norm_bwd")
        gsmall['norm_mix_g', l] = dg1.sum(0)

    small_names = [n for n in _NAMES if n not in _MATRIX]
    small_split = ([n for n in small_names if n not in _SMALL_BF16], [n for n in small_names if n in _SMALL_BF16])
    small_packs, full_shapes = [], []
    for names, dtype in zip(small_split, (F32, BF16)):
        layers = [[None] if n == 'norm_final_g' else range(depth) for n in names]
        small_packs.append(_pack([gsmall[n, l] for n, ls in zip(names, layers) for l in ls], dtype))
        full_shapes.append([gsmall[n, None].shape if n == 'norm_final_g' else (depth,) + gsmall[n, 0].shape for n in names])
    small_state, tok = _gather_start(small_packs, "small_grads")

    stored_t = [n for n in _TRANSPOSED if W[n].shape[-1] % LANES]
    wmv = {n: [jnp.swapaxes(t[n], 1, 2) if n in stored_t else t[n] for t in (W, Mo, Vo)] for n in _MATRIX}
    stacked = {n: None for n in _MATRIX}

    def update_matrices(l, after):
        for gi in (2, 0, 1):
            for n, p8 in zip(_SCATTER_GROUPS[gi], _scatter_wait(scatter[l, gi], after)):
                g = _sum8(p8, name="sum_%s_%d" % (n, l)).T if n in _TRANSPOSED and n not in stored_t else p8
                stacked[n] = _adamw_layer(g, *wmv[n], l, stacked[n], name="adamw_%s_%d" % (n, l))

    after = tok
    for l in reversed(range(depth)):
        update_matrices(l, after)
        after = sum(stacked[n][1][l, 0, 0] for n in _MATRIX).reshape(1, 1)
    after_matrices = after
    gs = {}
    for names, shapes, small8 in zip(small_split, full_shapes, _gather_wait(small_state, after_matrices)):
        gs.update(zip(names, _unpack(_sum8(small8, name="sum_small_" + str(small8.dtype)), shapes)))
    for n in _COLSHARD:
        cw = W[n].shape[-1]
        gs[n] = lax.dynamic_slice_in_dim(gs[n], me * cw, cw, axis=gs[n].ndim - 1)
    upd = _adamw_many([W[n] for n in small_names], [gs[n] for n in small_names], [Mo[n] for n in small_names],
                      [Vo[n] for n in small_names], name="adamw_small")
    out = {n: (gs[n], upd[0][j], upd[1][j], upd[2][j]) for j, n in enumerate(small_names)}
    for n in _MATRIX:
        out[n] = tuple(jnp.swapaxes(t, 1, 2) for t in stacked[n]) if n in stored_t else tuple(stacked[n])
    return (loss, dx[None]) + tuple(out[n][q] for q in range(4) for n in _NAMES)
```
